```python
import math
import jax
import jax.numpy as jnp
from jax import lax
import numpy as np

D_MODEL = 1024
BATCH = 32
SEQ = 2048
DEPTH = 2

HEAD_DIM = 64
SSD_WIDTH = 3 * D_MODEL // 8
LRU_WIDTH = D_MODEL // 4
FOX_WIDTH = D_MODEL - SSD_WIDTH - LRU_WIDTH
SSD_HEADS = SSD_WIDTH // HEAD_DIM
SSD_GROUPS = 2
SSD_STATE = 128
SSD_CONV = 4
SSD_CONV_DIM = SSD_WIDTH + 2 * SSD_GROUPS * SSD_STATE
SSD_CHUNK = 128
LRU_BLOCKS = 4
LRU_BLOCK = LRU_WIDTH // LRU_BLOCKS
LRU_CONV = 4
LRU_C = 8.0
FOX_HEADS = FOX_WIDTH // HEAD_DIM
Q_BLOCK = 128
D_FF = ((8 * D_MODEL + 3 * 256 - 1) // (3 * 256)) * 256
PLE_DIM = 256
IN_SPLITS = (SSD_WIDTH, SSD_CONV_DIM, SSD_HEADS, LRU_WIDTH, LRU_WIDTH,
             FOX_WIDTH, FOX_WIDTH, FOX_WIDTH, FOX_HEADS)
IN_COLS = sum(IN_SPLITS)
EPS = 1e-6

kernel_name = "hymba_style_ssd_rglru_fox_trunk"


def _rmsnorm(x, g):
    xf = x.astype(jnp.float32)
    y = xf * lax.rsqrt(jnp.mean(xf * xf, axis=-1, keepdims=True) + EPS)
    return (y * g.astype(jnp.float32)).astype(x.dtype)


def _causal_dwconv(x, w, b):
    k, c = w.shape
    y = lax.conv_general_dilated(
        x, w[:, None, :].astype(x.dtype), window_strides=(1,),
        padding=[(k - 1, 0)], dimension_numbers=("NWC", "WIO", "NWC"),
        feature_group_count=c)
    return y + b.astype(x.dtype)


def _ssd(xs, dt, a, bm, cm, d_skip):
    b, s, h, p = xs.shape
    g, n = bm.shape[2], bm.shape[3]
    nc = s // SSD_CHUNK
    rep = h // g
    bh = jnp.repeat(bm, rep, axis=2)
    chh = jnp.repeat(cm, rep, axis=2)
    xdt = xs * dt[..., None]
    adt = a * dt
    chunk = lambda t: t.reshape((b, nc, SSD_CHUNK) + t.shape[2:])
    xc, bc, cc, ac = chunk(xdt), chunk(bh), chunk(chh), chunk(adt)
    acs = jnp.cumsum(ac, axis=2)
    seg = acs[:, :, :, None, :] - acs[:, :, None, :, :]
    causal = jnp.tril(jnp.ones((SSD_CHUNK, SSD_CHUNK), bool))
    lmat = jnp.exp(jnp.where(causal[None, None, :, :, None], seg, -jnp.inf))
    scores = jnp.einsum("bclhn,bcshn->bclsh", cc, bc) * lmat
    y_diag = jnp.einsum("bclsh,bcshp->bclhp", scores, xc)
    decay_s = jnp.exp(acs[:, :, -1:, :] - acs)
    states = jnp.einsum("bclhn,bclh,bclhp->bchpn", bc, decay_s, xc)
    chunk_decay = jnp.exp(acs[:, :, -1, :])

    def step(carry, inp):
        st, dec = inp
        return carry * dec[..., None, None] + st, carry

    init = jnp.zeros((b, h, p, n), states.dtype)
    _, prev = lax.scan(step, init, (jnp.moveaxis(states, 1, 0),
                                    jnp.moveaxis(chunk_decay, 1, 0).astype(states.dtype)))
    prev = jnp.moveaxis(prev, 0, 1)
    y_off = jnp.einsum("bclhn,bchpn,bclh->bclhp", cc, prev, jnp.exp(acs))
    y = (y_diag + y_off).reshape(b, s, h, p) + xs * d_skip[:, None]
    return y.astype(xs.dtype)


def _rglru(x, w_a, b_a, w_x, b_x, lam):
    b, s, w = x.shape
    xb = x.reshape(b, s, LRU_BLOCKS, LRU_BLOCK)
    r = jax.nn.sigmoid(jnp.einsum("bsgi,gij->bsgj", xb, w_a).reshape(b, s, w) + b_a)
    i = jax.nn.sigmoid(jnp.einsum("bsgi,gij->bsgj", xb, w_x).reshape(b, s, w) + b_x)
    log_a = -LRU_C * r.astype(jnp.float32) * jax.nn.softplus(-lam.astype(jnp.float32))
    a = jnp.exp(log_a)
    mult = jnp.sqrt(-jnp.expm1(2.0 * log_a))
    u = mult * (i * x).astype(jnp.float32)

    def comb(lhs, rhs):
        a1, b1 = lhs
        a2, b2 = rhs
        return a1 * a2, a2 * b1 + b2

    _, hseq = lax.associative_scan(comb, (a, u), axis=1)
    return hseq.astype(x.dtype)


def _forgetting_attention(q, k, v, log_f):
    s, e = q.shape[1], q.shape[3]
    cum = jnp.cumsum(log_f, axis=-1)
    scale = e ** -0.5
    outs = []
    for blk in range(s // Q_BLOCK):
        q0, q1 = blk * Q_BLOCK, (blk + 1) * Q_BLOCK
        logits = jnp.einsum("bqhe,bkhe->bhqk", q[:, q0:q1], k[:, :q1]).astype(jnp.float32) * scale
        logits = logits + cum[:, :, q0:q1, None] - cum[:, :, None, :q1]
        mask = (q0 + jnp.arange(Q_BLOCK))[:, None] >= jnp.arange(q1)[None, :]
        logits = jnp.where(mask[None, None], logits, -jnp.inf)
        probs = jax.nn.softmax(logits, axis=-1).astype(v.dtype)
        outs.append(jnp.einsum("bhqk,bkhe->bqhe", probs, v[:, :q1]))
    return jnp.concatenate(outs, axis=1)


def _mixer(u, w_in, ssd_conv_w, ssd_conv_b, ssd_dt_bias, ssd_a_log, ssd_d, ssd_norm_g,
           lru_conv_w, lru_conv_b, lru_w_a, lru_b_a, lru_w_x, lru_b_x, lru_lambda, lru_norm_g,
           fox_b_f, fox_norm_g, w_out):
    b, s, _ = u.shape
    proj = u @ w_in
    offs = [int(o) for o in np.cumsum(IN_SPLITS)[:-1]]
    (z, xbc, dt_raw, lru_x, lru_gate, fq, fk, fv, f_raw) = jnp.split(proj, offs, axis=-1)

    xbc = jax.nn.silu(_causal_dwconv(xbc, ssd_conv_w, ssd_conv_b))
    xs, bm, cm = jnp.split(xbc, [SSD_WIDTH, SSD_WIDTH + SSD_GROUPS * SSD_STATE], axis=-1)
    xs = xs.reshape(b, s, SSD_HEADS, HEAD_DIM)
    bm = bm.reshape(b, s, SSD_GROUPS, SSD_STATE)
    cm = cm.reshape(b, s, SSD_GROUPS, SSD_STATE)
    dt = jax.nn.softplus(dt_raw.astype(jnp.float32) + ssd_dt_bias.astype(jnp.float32))
    a = -jnp.exp(ssd_a_log.astype(jnp.float32))
    y_ssd = _ssd(xs, dt, a, bm, cm, ssd_d).reshape(b, s, SSD_WIDTH)
    y_ssd = _rmsnorm(y_ssd * jax.nn.silu(z), ssd_norm_g)

    xl = _causal_dwconv(lru_x, lru_conv_w, lru_conv_b)
    hl = _rglru(xl, lru_w_a, lru_b_a, lru_w_x, lru_b_x, lru_lambda)
    y_lru = _rmsnorm(hl * jax.nn.gelu(lru_gate), lru_norm_g)

    q = fq.reshape(b, s, FOX_HEADS, HEAD_DIM)
    k = fk.reshape(b, s, FOX_HEADS, HEAD_DIM)
    v = fv.reshape(b, s, FOX_HEADS, HEAD_DIM)
    log_f = jnp.transpose(jax.nn.log_sigmoid(f_raw.astype(jnp.float32) + fox_b_f.astype(jnp.float32)), (0, 2, 1))
    y_fox = _forgetting_attention(q, k, v, log_f).reshape(b, s, FOX_WIDTH)
    y_fox = _rmsnorm(y_fox, fox_norm_g)

    return jnp.concatenate([y_ssd, y_lru, y_fox], axis=-1) @ w_out


def _fwd_setup_inputs(seed: int = 0) -> dict:
    key = jax.random.key(seed)
    ks = jax.random.split(key, 32)
    f32 = jnp.float32
    L = DEPTH
    nrm = lambda k, shape, scale: jax.random.normal(k, shape, f32) * scale
    gain = lambda k, shape: 1.0 + 0.02 * jax.random.normal(k, shape, f32)
    dt0 = jnp.exp(jax.random.uniform(ks[6], (L, SSD_HEADS), f32, math.log(1e-3), math.log(1e-1)))
    lam_s = jax.random.uniform(ks[16], (L, LRU_WIDTH), f32, 0.9, 0.999) ** (1.0 / LRU_C)
    return {
        "x": jax.random.normal(ks[0], (BATCH, SEQ, D_MODEL), f32),
        "p": jax.random.normal(ks[1], (L, BATCH, SEQ, PLE_DIM), f32),
        "norm1_g": gain(ks[2], (L, D_MODEL)),
        "w_in": nrm(ks[3], (L, D_MODEL, IN_COLS), D_MODEL ** -0.5),
        "ssd_conv_w": nrm(ks[4], (L, SSD_CONV, SSD_CONV_DIM), SSD_CONV ** -0.5),
        "ssd_conv_b": nrm(ks[5], (L, SSD_CONV_DIM), 0.02),
        "ssd_dt_bias": dt0 + jnp.log(-jnp.expm1(-dt0)),
        "ssd_a_log": jnp.log(jax.random.uniform(ks[7], (L, SSD_HEADS), f32, 1.0, 16.0)),
        "ssd_d": gain(ks[8], (L, SSD_HEADS)),
        "ssd_norm_g": gain(ks[9], (L, SSD_WIDTH)),
        "lru_conv_w": nrm(ks[10], (L, LRU_CONV, LRU_WIDTH), LRU_CONV ** -0.5),
        "lru_conv_b": nrm(ks[11], (L, LRU_WIDTH), 0.02),
        "lru_w_a": nrm(ks[12], (L, LRU_BLOCKS, LRU_BLOCK, LRU_BLOCK), LRU_BLOCK ** -0.5),
        "lru_b_a": nrm(ks[13], (L, LRU_WIDTH), 0.02),
        "lru_w_x": nrm(ks[14], (L, LRU_BLOCKS, LRU_BLOCK, LRU_BLOCK), LRU_BLOCK ** -0.5),
        "lru_b_x": nrm(ks[15], (L, LRU_WIDTH), 0.02),
        "lru_lambda": jnp.log(lam_s) - jnp.log1p(-lam_s),
        "lru_norm_g": gain(ks[17], (L, LRU_WIDTH)),
        "fox_b_f": 3.0 + nrm(ks[18], (L, FOX_HEADS), 0.1),
        "fox_norm_g": gain(ks[19], (L, FOX_WIDTH)),
        "w_out": nrm(ks[20], (L, D_MODEL, D_MODEL), D_MODEL ** -0.5),
        "norm2_g": gain(ks[21], (L, D_MODEL)),
        "w_gate": nrm(ks[22], (L, D_MODEL, D_FF), D_MODEL ** -0.5),
        "w_up": nrm(ks[23], (L, D_MODEL, D_FF), D_MODEL ** -0.5),
        "w_down": nrm(ks[24], (L, D_FF, D_MODEL), D_FF ** -0.5),
        "norm3_g": gain(ks[25], (L, D_MODEL)),
        "w_ple_gate": nrm(ks[26], (L, D_MODEL, D_MODEL), D_MODEL ** -0.5),
        "b_ple_gate": nrm(ks[27], (L, D_MODEL), 0.02),
        "w_ple_proj": nrm(ks[28], (L, PLE_DIM, D_MODEL), PLE_DIM ** -0.5),
        "final_norm_g": gain(ks[29], (D_MODEL,)),
    }


def _fwd_reference(x, p, norm1_g, w_in, ssd_conv_w, ssd_conv_b, ssd_dt_bias, ssd_a_log, ssd_d,
              ssd_norm_g, lru_conv_w, lru_conv_b, lru_w_a, lru_b_a, lru_w_x, lru_b_x,
              lru_lambda, lru_norm_g, fox_b_f, fox_norm_g, w_out, norm2_g, w_gate, w_up,
              w_down, norm3_g, w_ple_gate, b_ple_gate, w_ple_proj, final_norm_g):
    h = x
    for i in range(DEPTH):
        u = _rmsnorm(h, norm1_g[i])
        h = h + _mixer(u, w_in[i], ssd_conv_w[i], ssd_conv_b[i], ssd_dt_bias[i], ssd_a_log[i],
                       ssd_d[i], ssd_norm_g[i], lru_conv_w[i], lru_conv_b[i], lru_w_a[i],
                       lru_b_a[i], lru_w_x[i], lru_b_x[i], lru_lambda[i], lru_norm_g[i],
                       fox_b_f[i], fox_norm_g[i], w_out[i])
        u = _rmsnorm(h, norm2_g[i])
        h = h + (jax.nn.silu(u @ w_gate[i]) * (u @ w_up[i])) @ w_down[i]
        u = _rmsnorm(h, norm3_g[i])
        gate = jax.nn.sigmoid(u @ w_ple_gate[i] + b_ple_gate[i])
        h = h + gate * (p[i] @ w_ple_proj[i])
    return _rmsnorm(h, final_norm_g)


import jax as _jax
import jax.numpy as _jnp

TWIN_FORMAT = 'train_step'
FWD_PARAMS = ['x', 'p', 'norm1_g', 'w_in', 'ssd_conv_w', 'ssd_conv_b', 'ssd_dt_bias', 'ssd_a_log', 'ssd_d', 'ssd_norm_g', 'lru_conv_w', 'lru_conv_b', 'lru_w_a', 'lru_b_a', 'lru_w_x', 'lru_b_x', 'lru_lambda', 'lru_norm_g', 'fox_b_f', 'fox_norm_g', 'w_out', 'norm2_g', 'w_gate', 'w_up', 'w_down', 'norm3_g', 'w_ple_gate', 'b_ple_gate', 'w_ple_proj', 'final_norm_g']
TWIN_WEIGHTS = ['norm1_g', 'w_in', 'ssd_conv_w', 'ssd_conv_b', 'ssd_dt_bias', 'ssd_a_log', 'ssd_d', 'ssd_norm_g', 'lru_conv_w', 'lru_conv_b', 'lru_w_a', 'lru_b_a', 'lru_w_x', 'lru_b_x', 'lru_lambda', 'lru_norm_g', 'fox_b_f', 'fox_norm_g', 'w_out', 'norm2_g', 'w_gate', 'w_up', 'w_down', 'norm3_g', 'w_ple_gate', 'b_ple_gate', 'w_ple_proj', 'final_norm_g']
TWIN_DIFF_INPUT = 'x'
TWIN_INPUTS = ['x', 'p', 'norm1_g', 'w_in', 'ssd_conv_w', 'ssd_conv_b', 'ssd_dt_bias', 'ssd_a_log', 'ssd_d', 'ssd_norm_g', 'lru_conv_w', 'lru_conv_b', 'lru_w_a', 'lru_b_a', 'lru_w_x', 'lru_b_x', 'lru_lambda', 'lru_norm_g', 'fox_b_f', 'fox_norm_g', 'w_out', 'norm2_g', 'w_gate', 'w_up', 'w_down', 'norm3_g', 'w_ple_gate', 'b_ple_gate', 'w_ple_proj', 'final_norm_g', 'loss_target', 'm_norm1_g', 'm_w_in', 'm_ssd_conv_w', 'm_ssd_conv_b', 'm_ssd_dt_bias', 'm_ssd_a_log', 'm_ssd_d', 'm_ssd_norm_g', 'm_lru_conv_w', 'm_lru_conv_b', 'm_lru_w_a', 'm_lru_b_a', 'm_lru_w_x', 'm_lru_b_x', 'm_lru_lambda', 'm_lru_norm_g', 'm_fox_b_f', 'm_fox_norm_g', 'm_w_out', 'm_norm2_g', 'm_w_gate', 'm_w_up', 'm_w_down', 'm_norm3_g', 'm_w_ple_gate', 'm_b_ple_gate', 'm_w_ple_proj', 'm_final_norm_g', 'v_norm1_g', 'v_w_in', 'v_ssd_conv_w', 'v_ssd_conv_b', 'v_ssd_dt_bias', 'v_ssd_a_log', 'v_ssd_d', 'v_ssd_norm_g', 'v_lru_conv_w', 'v_lru_conv_b', 'v_lru_w_a', 'v_lru_b_a', 'v_lru_w_x', 'v_lru_b_x', 'v_lru_lambda', 'v_lru_norm_g', 'v_fox_b_f', 'v_fox_norm_g', 'v_w_out', 'v_norm2_g', 'v_w_gate', 'v_w_up', 'v_w_down', 'v_norm3_g', 'v_w_ple_gate', 'v_b_ple_gate', 'v_w_ple_proj', 'v_final_norm_g']
TWIN_OUTPUTS = ['loss', 'grad_x', 'grad_norm1_g', 'grad_w_in', 'grad_ssd_conv_w', 'grad_ssd_conv_b', 'grad_ssd_dt_bias', 'grad_ssd_a_log', 'grad_ssd_d', 'grad_ssd_norm_g', 'grad_lru_conv_w', 'grad_lru_conv_b', 'grad_lru_w_a', 'grad_lru_b_a', 'grad_lru_w_x', 'grad_lru_b_x', 'grad_lru_lambda', 'grad_lru_norm_g', 'grad_fox_b_f', 'grad_fox_norm_g', 'grad_w_out', 'grad_norm2_g', 'grad_w_gate', 'grad_w_up', 'grad_w_down', 'grad_norm3_g', 'grad_w_ple_gate', 'grad_b_ple_gate', 'grad_w_ple_proj', 'grad_final_norm_g', 'delta_norm1_g', 'delta_w_in', 'delta_ssd_conv_w', 'delta_ssd_conv_b', 'delta_ssd_dt_bias', 'delta_ssd_a_log', 'delta_ssd_d', 'delta_ssd_norm_g', 'delta_lru_conv_w', 'delta_lru_conv_b', 'delta_lru_w_a', 'delta_lru_b_a', 'delta_lru_w_x', 'delta_lru_b_x', 'delta_lru_lambda', 'delta_lru_norm_g', 'delta_fox_b_f', 'delta_fox_norm_g', 'delta_w_out', 'delta_norm2_g', 'delta_w_gate', 'delta_w_up', 'delta_w_down', 'delta_norm3_g', 'delta_w_ple_gate', 'delta_b_ple_gate', 'delta_w_ple_proj', 'delta_final_norm_g', 'new_m_norm1_g', 'new_m_w_in', 'new_m_ssd_conv_w', 'new_m_ssd_conv_b', 'new_m_ssd_dt_bias', 'new_m_ssd_a_log', 'new_m_ssd_d', 'new_m_ssd_norm_g', 'new_m_lru_conv_w', 'new_m_lru_conv_b', 'new_m_lru_w_a', 'new_m_lru_b_a', 'new_m_lru_w_x', 'new_m_lru_b_x', 'new_m_lru_lambda', 'new_m_lru_norm_g', 'new_m_fox_b_f', 'new_m_fox_norm_g', 'new_m_w_out', 'new_m_norm2_g', 'new_m_w_gate', 'new_m_w_up', 'new_m_w_down', 'new_m_norm3_g', 'new_m_w_ple_gate', 'new_m_b_ple_gate', 'new_m_w_ple_proj', 'new_m_final_norm_g', 'new_v_norm1_g', 'new_v_w_in', 'new_v_ssd_conv_w', 'new_v_ssd_conv_b', 'new_v_ssd_dt_bias', 'new_v_ssd_a_log', 'new_v_ssd_d', 'new_v_ssd_norm_g', 'new_v_lru_conv_w', 'new_v_lru_conv_b', 'new_v_lru_w_a', 'new_v_lru_b_a', 'new_v_lru_w_x', 'new_v_lru_b_x', 'new_v_lru_lambda', 'new_v_lru_norm_g', 'new_v_fox_b_f', 'new_v_fox_norm_g', 'new_v_w_out', 'new_v_norm2_g', 'new_v_w_gate', 'new_v_w_up', 'new_v_w_down', 'new_v_norm3_g', 'new_v_w_ple_gate', 'new_v_b_ple_gate', 'new_v_w_ple_proj', 'new_v_final_norm_g']
TWIN_LEAF_KINDS = {'loss': 'loss', 'grad_x': 'grad_x', 'grad_norm1_g': 'grad_w', 'grad_w_in': 'grad_w', 'grad_ssd_conv_w': 'grad_w', 'grad_ssd_conv_b': 'grad_w', 'grad_ssd_dt_bias': 'grad_w', 'grad_ssd_a_log': 'grad_w', 'grad_ssd_d': 'grad_w', 'grad_ssd_norm_g': 'grad_w', 'grad_lru_conv_w': 'grad_w', 'grad_lru_conv_b': 'grad_w', 'grad_lru_w_a': 'grad_w', 'grad_lru_b_a': 'grad_w', 'grad_lru_w_x': 'grad_w', 'grad_lru_b_x': 'grad_w', 'grad_lru_lambda': 'grad_w', 'grad_lru_norm_g': 'grad_w', 'grad_fox_b_f': 'grad_w', 'grad_fox_norm_g': 'grad_w', 'grad_w_out': 'grad_w', 'grad_norm2_g': 'grad_w', 'grad_w_gate': 'grad_w', 'grad_w_up': 'grad_w', 'grad_w_down': 'grad_w', 'grad_norm3_g': 'grad_w', 'grad_w_ple_gate': 'grad_w', 'grad_b_ple_gate': 'grad_w', 'grad_w_ple_proj': 'grad_w', 'grad_final_norm_g': 'grad_w', 'delta_norm1_g': 'delta_w', 'delta_w_in': 'delta_w', 'delta_ssd_conv_w': 'delta_w', 'delta_ssd_conv_b': 'delta_w', 'delta_ssd_dt_bias': 'delta_w', 'delta_ssd_a_log': 'delta_w', 'delta_ssd_d': 'delta_w', 'delta_ssd_norm_g': 'delta_w', 'delta_lru_conv_w': 'delta_w', 'delta_lru_conv_b': 'delta_w', 'delta_lru_w_a': 'delta_w', 'delta_lru_b_a': 'delta_w', 'delta_lru_w_x': 'delta_w', 'delta_lru_b_x': 'delta_w', 'delta_lru_lambda': 'delta_w', 'delta_lru_norm_g': 'delta_w', 'delta_fox_b_f': 'delta_w', 'delta_fox_norm_g': 'delta_w', 'delta_w_out': 'delta_w', 'delta_norm2_g': 'delta_w', 'delta_w_gate': 'delta_w', 'delta_w_up': 'delta_w', 'delta_w_down': 'delta_w', 'delta_norm3_g': 'delta_w', 'delta_w_ple_gate': 'delta_w', 'delta_b_ple_gate': 'delta_w', 'delta_w_ple_proj': 'delta_w', 'delta_final_norm_g': 'delta_w', 'new_m_norm1_g': 'new_m', 'new_m_w_in': 'new_m', 'new_m_ssd_conv_w': 'new_m', 'new_m_ssd_conv_b': 'new_m', 'new_m_ssd_dt_bias': 'new_m', 'new_m_ssd_a_log': 'new_m', 'new_m_ssd_d': 'new_m', 'new_m_ssd_norm_g': 'new_m', 'new_m_lru_conv_w': 'new_m', 'new_m_lru_conv_b': 'new_m', 'new_m_lru_w_a': 'new_m', 'new_m_lru_b_a': 'new_m', 'new_m_lru_w_x': 'new_m', 'new_m_lru_b_x': 'new_m', 'new_m_lru_lambda': 'new_m', 'new_m_lru_norm_g': 'new_m', 'new_m_fox_b_f': 'new_m', 'new_m_fox_norm_g': 'new_m', 'new_m_w_out': 'new_m', 'new_m_norm2_g': 'new_m', 'new_m_w_gate': 'new_m', 'new_m_w_up': 'new_m', 'new_m_w_down': 'new_m', 'new_m_norm3_g': 'new_m', 'new_m_w_ple_gate': 'new_m', 'new_m_b_ple_gate': 'new_m', 'new_m_w_ple_proj': 'new_m', 'new_m_final_norm_g': 'new_m', 'new_v_norm1_g': 'new_v', 'new_v_w_in': 'new_v', 'new_v_ssd_conv_w': 'new_v', 'new_v_ssd_conv_b': 'new_v', 'new_v_ssd_dt_bias': 'new_v', 'new_v_ssd_a_log': 'new_v', 'new_v_ssd_d': 'new_v', 'new_v_ssd_norm_g': 'new_v', 'new_v_lru_conv_w': 'new_v', 'new_v_lru_conv_b': 'new_v', 'new_v_lru_w_a': 'new_v', 'new_v_lru_b_a': 'new_v', 'new_v_lru_w_x': 'new_v', 'new_v_lru_b_x': 'new_v', 'new_v_lru_lambda': 'new_v', 'new_v_lru_norm_g': 'new_v', 'new_v_fox_b_f': 'new_v', 'new_v_fox_norm_g': 'new_v', 'new_v_w_out': 'new_v', 'new_v_norm2_g': 'new_v', 'new_v_w_gate': 'new_v', 'new_v_w_up': 'new_v', 'new_v_w_down': 'new_v', 'new_v_norm3_g': 'new_v', 'new_v_w_ple_gate': 'new_v', 'new_v_b_ple_gate': 'new_v', 'new_v_w_ple_proj': 'new_v', 'new_v_final_norm_g': 'new_v'}


def _forward(args):
    return _fwd_reference(*[args[k] for k in FWD_PARAMS])


def _output_shape():
    out = _jax.eval_shape(lambda: _forward(_fwd_setup_inputs(0)))
    return out.shape, out.dtype

N_MICROBATCH = 1
ADAM_LR = 0.001
ADAM_B1 = 0.9
ADAM_B2 = 0.999
ADAM_EPS = 1e-08
ADAM_WD = 0.01
ADAM_STEP = 10
PER_EXAMPLE_BATCH_AXIS = {'x': 0, 'p': 1, 'loss_target': 0}
SHARED_INPUTS = []
_WEIGHT_DTYPES = {'norm1_g': _jnp.float32, 'w_in': _jnp.float32, 'ssd_conv_w': _jnp.float32, 'ssd_conv_b': _jnp.float32, 'ssd_dt_bias': _jnp.float32, 'ssd_a_log': _jnp.float32, 'ssd_d': _jnp.float32, 'ssd_norm_g': _jnp.float32, 'lru_conv_w': _jnp.float32, 'lru_conv_b': _jnp.float32, 'lru_w_a': _jnp.float32, 'lru_b_a': _jnp.float32, 'lru_w_x': _jnp.float32, 'lru_b_x': _jnp.float32, 'lru_lambda': _jnp.float32, 'lru_norm_g': _jnp.float32, 'fox_b_f': _jnp.float32, 'fox_norm_g': _jnp.float32, 'w_out': _jnp.float32, 'norm2_g': _jnp.float32, 'w_gate': _jnp.float32, 'w_up': _jnp.float32, 'w_down': _jnp.float32, 'norm3_g': _jnp.float32, 'w_ple_gate': _jnp.float32, 'b_ple_gate': _jnp.float32, 'w_ple_proj': _jnp.float32, 'final_norm_g': _jnp.float32}
MOMENT_SCALE = {'norm1_g': 2.778457e-01, 'w_in': 1.559235e-01, 'ssd_conv_w': 1.219919e-01, 'ssd_conv_b': 1.727793e-01, 'ssd_dt_bias': 3.486714e-01, 'ssd_a_log': 9.310853e-01, 'ssd_d': 2.631228e+00, 'ssd_norm_g': 1.787750e-01, 'lru_conv_w': 1.972063e-01, 'lru_conv_b': 2.723876e+00, 'lru_w_a': 1.096320e-01, 'lru_b_a': 5.750646e-02, 'lru_w_x': 2.114463e-01, 'lru_b_x': 6.318693e-02, 'lru_lambda': 1.069370e-01, 'lru_norm_g': 2.243441e-01, 'fox_b_f': 1.001825e+00, 'fox_norm_g': 1.852414e-01, 'w_out': 1.829555e-01, 'norm2_g': 1.316631e-01, 'w_gate': 5.610456e-02, 'w_up': 5.466320e-02, 'w_down': 9.053769e-02, 'norm3_g': 3.279000e-02, 'w_ple_gate': 3.164798e-02, 'b_ple_gate': 3.886306e-02, 'w_ple_proj': 8.115624e-02, 'final_norm_g': 6.384146e+01}


def _to_microbatches(a, axis):
    t = _jnp.moveaxis(a, axis, 0)
    t = t.reshape((N_MICROBATCH, t.shape[0] // N_MICROBATCH) + t.shape[1:])
    return _jnp.moveaxis(t, 1, axis + 1)


def setup_inputs(seed: int = 0) -> dict:
    inp = _fwd_setup_inputs(seed)
    key = _jax.random.fold_in(_jax.random.key(seed), 7919)
    shape, _ = _output_shape()
    out = dict(inp)
    out["loss_target"] = _jax.random.normal(_jax.random.fold_in(key, 0), shape, _jnp.float32)
    for i, name in enumerate(TWIN_WEIGHTS):
        w = inp[name].astype(_jnp.float32)
        if MOMENT_SCALE is None:
            s = _jnp.sqrt(_jnp.mean(_jnp.square(w)) + 1e-30)
        else:
            s = MOMENT_SCALE[name]
        km, kv = _jax.random.split(_jax.random.fold_in(key, i + 1))
        out[name] = w
        out["m_" + name] = s * _jax.random.normal(km, w.shape, _jnp.float32)
        out["v_" + name] = (s * s) * _jax.random.uniform(kv, w.shape, _jnp.float32, 0.5, 1.5)
    if N_MICROBATCH > 1:
        for name, axis in PER_EXAMPLE_BATCH_AXIS.items():
            out[name] = _to_microbatches(out[name], axis)
    return {'x': out['x'], 'p': out['p'], 'norm1_g': out['norm1_g'], 'w_in': out['w_in'], 'ssd_conv_w': out['ssd_conv_w'], 'ssd_conv_b': out['ssd_conv_b'], 'ssd_dt_bias': out['ssd_dt_bias'], 'ssd_a_log': out['ssd_a_log'], 'ssd_d': out['ssd_d'], 'ssd_norm_g': out['ssd_norm_g'], 'lru_conv_w': out['lru_conv_w'], 'lru_conv_b': out['lru_conv_b'], 'lru_w_a': out['lru_w_a'], 'lru_b_a': out['lru_b_a'], 'lru_w_x': out['lru_w_x'], 'lru_b_x': out['lru_b_x'], 'lru_lambda': out['lru_lambda'], 'lru_norm_g': out['lru_norm_g'], 'fox_b_f': out['fox_b_f'], 'fox_norm_g': out['fox_norm_g'], 'w_out': out['w_out'], 'norm2_g': out['norm2_g'], 'w_gate': out['w_gate'], 'w_up': out['w_up'], 'w_down': out['w_down'], 'norm3_g': out['norm3_g'], 'w_ple_gate': out['w_ple_gate'], 'b_ple_gate': out['b_ple_gate'], 'w_ple_proj': out['w_ple_proj'], 'final_norm_g': out['final_norm_g'], 'loss_target': out['loss_target'], 'm_norm1_g': out['m_norm1_g'], 'm_w_in': out['m_w_in'], 'm_ssd_conv_w': out['m_ssd_conv_w'], 'm_ssd_conv_b': out['m_ssd_conv_b'], 'm_ssd_dt_bias': out['m_ssd_dt_bias'], 'm_ssd_a_log': out['m_ssd_a_log'], 'm_ssd_d': out['m_ssd_d'], 'm_ssd_norm_g': out['m_ssd_norm_g'], 'm_lru_conv_w': out['m_lru_conv_w'], 'm_lru_conv_b': out['m_lru_conv_b'], 'm_lru_w_a': out['m_lru_w_a'], 'm_lru_b_a': out['m_lru_b_a'], 'm_lru_w_x': out['m_lru_w_x'], 'm_lru_b_x': out['m_lru_b_x'], 'm_lru_lambda': out['m_lru_lambda'], 'm_lru_norm_g': out['m_lru_norm_g'], 'm_fox_b_f': out['m_fox_b_f'], 'm_fox_norm_g': out['m_fox_norm_g'], 'm_w_out': out['m_w_out'], 'm_norm2_g': out['m_norm2_g'], 'm_w_gate': out['m_w_gate'], 'm_w_up': out['m_w_up'], 'm_w_down': out['m_w_down'], 'm_norm3_g': out['m_norm3_g'], 'm_w_ple_gate': out['m_w_ple_gate'], 'm_b_ple_gate': out['m_b_ple_gate'], 'm_w_ple_proj': out['m_w_ple_proj'], 'm_final_norm_g': out['m_final_norm_g'], 'v_norm1_g': out['v_norm1_g'], 'v_w_in': out['v_w_in'], 'v_ssd_conv_w': out['v_ssd_conv_w'], 'v_ssd_conv_b': out['v_ssd_conv_b'], 'v_ssd_dt_bias': out['v_ssd_dt_bias'], 'v_ssd_a_log': out['v_ssd_a_log'], 'v_ssd_d': out['v_ssd_d'], 'v_ssd_norm_g': out['v_ssd_norm_g'], 'v_lru_conv_w': out['v_lru_conv_w'], 'v_lru_conv_b': out['v_lru_conv_b'], 'v_lru_w_a': out['v_lru_w_a'], 'v_lru_b_a': out['v_lru_b_a'], 'v_lru_w_x': out['v_lru_w_x'], 'v_lru_b_x': out['v_lru_b_x'], 'v_lru_lambda': out['v_lru_lambda'], 'v_lru_norm_g': out['v_lru_norm_g'], 'v_fox_b_f': out['v_fox_b_f'], 'v_fox_norm_g': out['v_fox_norm_g'], 'v_w_out': out['v_w_out'], 'v_norm2_g': out['v_norm2_g'], 'v_w_gate': out['v_w_gate'], 'v_w_up': out['v_w_up'], 'v_w_down': out['v_w_down'], 'v_norm3_g': out['v_norm3_g'], 'v_w_ple_gate': out['v_w_ple_gate'], 'v_b_ple_gate': out['v_b_ple_gate'], 'v_w_ple_proj': out['v_w_ple_proj'], 'v_final_norm_g': out['v_final_norm_g']}


def _loss(weights, diff, rest, loss_target):
    with _jax.named_scope("forward"):
        args = {**rest, TWIN_DIFF_INPUT: diff, **{k: w.astype(_WEIGHT_DTYPES[k]) for k, w in weights.items()}}
        y = _forward(args)
    with _jax.named_scope("loss_head"):
        err = _jnp.square(y.astype(_jnp.float32) - loss_target)
        return 0.5 * _jnp.sum(_jnp.mean(err, axis=-1)) if err.ndim else 0.5 * err


def _adamw(w, g, m, v):
    m = ADAM_B1 * m + (1.0 - ADAM_B1) * g
    v = ADAM_B2 * v + (1.0 - ADAM_B2) * _jnp.square(g)
    m_hat = m / (1.0 - ADAM_B1 ** ADAM_STEP)
    v_hat = v / (1.0 - ADAM_B2 ** ADAM_STEP)
    delta = -ADAM_LR * (m_hat / (_jnp.sqrt(v_hat) + ADAM_EPS) + ADAM_WD * w)
    return delta, m, v


def reference(x, p, norm1_g, w_in, ssd_conv_w, ssd_conv_b, ssd_dt_bias, ssd_a_log, ssd_d, ssd_norm_g, lru_conv_w, lru_conv_b, lru_w_a, lru_b_a, lru_w_x, lru_b_x, lru_lambda, lru_norm_g, fox_b_f, fox_norm_g, w_out, norm2_g, w_gate, w_up, w_down, norm3_g, w_ple_gate, b_ple_gate, w_ple_proj, final_norm_g, loss_target, m_norm1_g, m_w_in, m_ssd_conv_w, m_ssd_conv_b, m_ssd_dt_bias, m_ssd_a_log, m_ssd_d, m_ssd_norm_g, m_lru_conv_w, m_lru_conv_b, m_lru_w_a, m_lru_b_a, m_lru_w_x, m_lru_b_x, m_lru_lambda, m_lru_norm_g, m_fox_b_f, m_fox_norm_g, m_w_out, m_norm2_g, m_w_gate, m_w_up, m_w_down, m_norm3_g, m_w_ple_gate, m_b_ple_gate, m_w_ple_proj, m_final_norm_g, v_norm1_g, v_w_in, v_ssd_conv_w, v_ssd_conv_b, v_ssd_dt_bias, v_ssd_a_log, v_ssd_d, v_ssd_norm_g, v_lru_conv_w, v_lru_conv_b, v_lru_w_a, v_lru_b_a, v_lru_w_x, v_lru_b_x, v_lru_lambda, v_lru_norm_g, v_fox_b_f, v_fox_norm_g, v_w_out, v_norm2_g, v_w_gate, v_w_up, v_w_down, v_norm3_g, v_w_ple_gate, v_b_ple_gate, v_w_ple_proj, v_final_norm_g):
    given = dict(x=x, p=p, norm1_g=norm1_g, w_in=w_in, ssd_conv_w=ssd_conv_w, ssd_conv_b=ssd_conv_b, ssd_dt_bias=ssd_dt_bias, ssd_a_log=ssd_a_log, ssd_d=ssd_d, ssd_norm_g=ssd_norm_g, lru_conv_w=lru_conv_w, lru_conv_b=lru_conv_b, lru_w_a=lru_w_a, lru_b_a=lru_b_a, lru_w_x=lru_w_x, lru_b_x=lru_b_x, lru_lambda=lru_lambda, lru_norm_g=lru_norm_g, fox_b_f=fox_b_f, fox_norm_g=fox_norm_g, w_out=w_out, norm2_g=norm2_g, w_gate=w_gate, w_up=w_up, w_down=w_down, norm3_g=norm3_g, w_ple_gate=w_ple_gate, b_ple_gate=b_ple_gate, w_ple_proj=w_ple_proj, final_norm_g=final_norm_g, loss_target=loss_target, m_norm1_g=m_norm1_g, m_w_in=m_w_in, m_ssd_conv_w=m_ssd_conv_w, m_ssd_conv_b=m_ssd_conv_b, m_ssd_dt_bias=m_ssd_dt_bias, m_ssd_a_log=m_ssd_a_log, m_ssd_d=m_ssd_d, m_ssd_norm_g=m_ssd_norm_g, m_lru_conv_w=m_lru_conv_w, m_lru_conv_b=m_lru_conv_b, m_lru_w_a=m_lru_w_a, m_lru_b_a=m_lru_b_a, m_lru_w_x=m_lru_w_x, m_lru_b_x=m_lru_b_x, m_lru_lambda=m_lru_lambda, m_lru_norm_g=m_lru_norm_g, m_fox_b_f=m_fox_b_f, m_fox_norm_g=m_fox_norm_g, m_w_out=m_w_out, m_norm2_g=m_norm2_g, m_w_gate=m_w_gate, m_w_up=m_w_up, m_w_down=m_w_down, m_norm3_g=m_norm3_g, m_w_ple_gate=m_w_ple_gate, m_b_ple_gate=m_b_ple_gate, m_w_ple_proj=m_w_ple_proj, m_final_norm_g=m_final_norm_g, v_norm1_g=v_norm1_g, v_w_in=v_w_in, v_ssd_conv_w=v_ssd_conv_w, v_ssd_conv_b=v_ssd_conv_b, v_ssd_dt_bias=v_ssd_dt_bias, v_ssd_a_log=v_ssd_a_log, v_ssd_d=v_ssd_d, v_ssd_norm_g=v_ssd_norm_g, v_lru_conv_w=v_lru_conv_w, v_lru_conv_b=v_lru_conv_b, v_lru_w_a=v_lru_w_a, v_lru_b_a=v_lru_b_a, v_lru_w_x=v_lru_w_x, v_lru_b_x=v_lru_b_x, v_lru_lambda=v_lru_lambda, v_lru_norm_g=v_lru_norm_g, v_fox_b_f=v_fox_b_f, v_fox_norm_g=v_fox_norm_g, v_w_out=v_w_out, v_norm2_g=v_norm2_g, v_w_gate=v_w_gate, v_w_up=v_w_up, v_w_down=v_w_down, v_norm3_g=v_norm3_g, v_w_ple_gate=v_w_ple_gate, v_b_ple_gate=v_b_ple_gate, v_w_ple_proj=v_w_ple_proj, v_final_norm_g=v_final_norm_g)
    weights = {n: given[n] for n in TWIN_WEIGHTS}
    shared = {n: given[n] for n in SHARED_INPUTS}
    per_example = {n: given[n] for n in ['x', 'p']}
    grad_fn = _jax.value_and_grad(_loss, argnums=(0, 1))

    def one_microbatch(ex, loss_target):
        ex = dict(ex)
        diff = ex.pop(TWIN_DIFF_INPUT)
        return grad_fn(weights, diff, {**shared, **ex}, loss_target)

    if N_MICROBATCH == 1:
        loss, (grad_w, grad_x) = one_microbatch(per_example, given["loss_target"])
    else:
        def body(carry, xs):
            loss_sum, grad_sum = carry
            l_k, (gw_k, gx_k) = one_microbatch(xs[0], xs[1])
            with _jax.named_scope("update"):
                return (loss_sum + l_k, _jax.tree.map(_jnp.add, grad_sum, gw_k)), gx_k

        init = (_jnp.zeros((), _jnp.float32), _jax.tree.map(_jnp.zeros_like, weights))
        (loss, grad_w), grad_x = _jax.lax.scan(body, init, (per_example, given["loss_target"]))
    with _jax.named_scope("update"):
        delta_w, new_m, new_v = {}, {}, {}
        for n in TWIN_WEIGHTS:
            delta_w[n], new_m[n], new_v[n] = _adamw(weights[n], grad_w[n], given["m_" + n], given["v_" + n])
    return (loss, grad_x, *[grad_w[n] for n in TWIN_WEIGHTS], *[delta_w[n] for n in TWIN_WEIGHTS],
            *[new_m[n] for n in TWIN_WEIGHTS], *[new_v[n] for n in TWIN_WEIGHTS])
```

```python
import functools
import math

import jax
import jax.numpy as jnp
import numpy as np
from jax import lax
from jax.experimental import pallas as pl
from jax.experimental.pallas import tpu as pltpu

F32, BF16 = jnp.float32, jnp.bfloat16
SDS = jax.ShapeDtypeStruct

D_MODEL = 1024
DEPTH = 2
HEAD_DIM = 64
N_HEADS = 6
SSD_W, LRU_W, FOX_W = 384, 256, 384
D_FF = 2816
PLE_DIM = 256
IN_COLS = 2956
EPS = 1e-6
LRU_C = 8.0
LANE = 128
V7X_VMEM_LIMIT = 56 * 1024 * 1024

PW = 3072
OFF_B, OFF_C, OFF_LX, OFF_LG, OFF_SM, OFF_Z, OFF_XS, OFF_Q, OFF_K, OFF_V = (
    0, 256, 512, 768, 1024, 1152, 1536, 1920, 2304, 2688)
FOX_LANE0 = 8

ADAM_LR, ADAM_B1, ADAM_B2, ADAM_EPS, ADAM_WD, ADAM_STEP = 0.001, 0.9, 0.999, 1e-08, 0.01, 10


def _iota(shape, dim):
    return lax.broadcasted_iota(jnp.int32, shape, dim)


def _pc(body, *, name, grid, in_specs, out_specs, out_shape, scratch=(), sem=None):
    return pl.pallas_call(
        body, name=name, grid=grid, in_specs=in_specs, out_specs=out_specs, out_shape=out_shape,
        scratch_shapes=list(scratch),
        compiler_params=pltpu.CompilerParams(dimension_semantics=sem, vmem_limit_bytes=V7X_VMEM_LIMIT))


def permute_in_cols(w):
    z = lambda n: jnp.zeros(w.shape[:-1] + (n,), w.dtype)
    s = lambda a, b: w[..., a:b]
    return jnp.concatenate([
        s(768, 1024), s(1024, 1280), s(1286, 1542), s(1542, 1798),
        s(1280, 1286), z(2), s(2950, 2956), z(LANE - 14),
        s(0, 384), s(384, 768), s(1798, 2182), s(2182, 2566), s(2566, 2950)], axis=-1)


def unpermute_in_cols(g):
    s = lambda a, n: g[..., a:a + n]
    return jnp.concatenate([
        s(OFF_Z, 384), s(OFF_XS, 384), s(OFF_B, 256), s(OFF_C, 256), s(OFF_SM, 6),
        s(OFF_LX, 256), s(OFF_LG, 256), s(OFF_Q, 384), s(OFF_K, 384), s(OFF_V, 384),
        s(OFF_SM + FOX_LANE0, 6)], axis=-1)


def _pick(n, cands):
    for c in cands:
        if n % c == 0:
            return c
    return n


def mm(a, b, *, name, ta=False, tb=False, out_dtypes=(F32,), extras=(), epilogue=None, tm=None, tn=None, tk=None):
    M = a.shape[1] if ta else a.shape[0]
    K = a.shape[0] if ta else a.shape[1]
    N = b.shape[0] if tb else b.shape[1]
    tm = tm or _pick(M, (512, 256, 128))
    tn = tn or _pick(N, (1024, 1408, 768, 512, 256, 128))
    tk = tk or _pick(K, (1024, 1408, 512, 256, 128))
    nk = K // tk
    n_ex, n_out = len(extras), len(out_dtypes)

    def body(a_ref, b_ref, *rest):
        ex_refs, out_refs, acc = rest[:n_ex], rest[n_ex:n_ex + n_out], rest[-1]
        k = pl.program_id(2)

        @pl.when(k == 0)
        def _():
            acc[...] = jnp.zeros_like(acc)

        dn = (((0 if ta else 1,), (1 if tb else 0,)), ((), ()))
        acc[...] += lax.dot_general(a_ref[...].astype(BF16), b_ref[...].astype(BF16), dn,
                                    preferred_element_type=F32)

        @pl.when(k == nk - 1)
        def _():
            r = acc[...]
            outs = epilogue(r, *[e[...] for e in ex_refs]) if epilogue is not None else (r,)
            for o_ref, o in zip(out_refs, outs):
                o_ref[...] = o.astype(o_ref.dtype)

    a_spec = pl.BlockSpec((tk, tm), lambda i, j, k: (k, i)) if ta else pl.BlockSpec((tm, tk), lambda i, j, k: (i, k))
    b_spec = pl.BlockSpec((tn, tk), lambda i, j, k: (j, k)) if tb else pl.BlockSpec((tk, tn), lambda i, j, k: (k, j))
    mn_spec = pl.BlockSpec((tm, tn), lambda i, j, k: (i, j))
    outs = _pc(body, name=name, grid=(M // tm, N // tn, nk),
               in_specs=[a_spec, b_spec] + [mn_spec] * n_ex,
               out_specs=[mn_spec] * n_out,
               out_shape=[SDS((M, N), dt) for dt in out_dtypes],
               scratch=[pltpu.VMEM((tm, tn), F32)],
               sem=("parallel", "parallel", "arbitrary"))(a, b, *extras)
    return outs[0] if n_out == 1 else outs


def rowwise(name, fn, rows, params, row_outs, acc_outs=(), tr=512):
    rows = [r if isinstance(r, tuple) else (r, 0, r.shape[1]) for r in rows]
    T = rows[0][0].shape[0]
    tr = min(tr, T)
    n_in, n_ro, n_ac = len(rows) + len(params), len(row_outs), len(acc_outs)

    def body(*refs):
        ins, outs = refs[:n_in], refs[n_in:]
        res = fn(*[r[...] for r in ins])
        if not isinstance(res, (tuple, list)):
            res = (res,)
        for k in range(n_ro):
            outs[k][...] = res[k].astype(outs[k].dtype)
        if n_ac:
            i = pl.program_id(0)

            @pl.when(i == 0)
            def _():
                for k in range(n_ac):
                    outs[n_ro + k][...] = res[n_ro + k]

            @pl.when(i > 0)
            def _():
                for k in range(n_ac):
                    outs[n_ro + k][...] += res[n_ro + k]

    in_specs = ([pl.BlockSpec((tr, w), functools.partial(lambda cb, i: (i, cb), cb)) for (_, cb, w) in rows]
                + [pl.BlockSpec(p.shape, lambda i: (0, 0)) for p in params])
    out_specs = ([pl.BlockSpec((tr, c), lambda i: (i, 0)) for (c, _) in row_outs]
                 + [pl.BlockSpec((1, c), lambda i: (0, 0)) for c in acc_outs])
    out_shape = [SDS((T, c), dt) for (c, dt) in row_outs] + [SDS((1, c), F32) for c in acc_outs]
    outs = _pc(body, name=name, grid=(T // tr,), in_specs=in_specs, out_specs=out_specs, out_shape=out_shape,
               sem=("arbitrary",) if n_ac else ("parallel",))(*[r[0] for r in rows], *params)
    return outs[0] if len(outs) == 1 else outs


def _rms(x, g):
    return x * lax.rsqrt(jnp.mean(x * x, axis=-1, keepdims=True) + EPS) * g


def _softplus(x):
    return jnp.maximum(x, 0.0) + jnp.log(1.0 + jnp.exp(-jnp.abs(x)))


def _silu(x):
    return x * jax.nn.sigmoid(x)


def _gelu(x):
    return 0.5 * x * (1.0 + jnp.tanh(math.sqrt(2.0 / math.pi) * (x + 0.044715 * (x * x * x))))


def _neg_expm1(x):
    series = x * (1 + x / 2 * (1 + x / 3 * (1 + x / 4 * (1 + x / 5 * (1 + x / 6 * (1 + x / 7))))))
    return -jnp.where(jnp.abs(x) < 0.3, series, jnp.exp(x) - 1.0)


def _swiglu(gu):
    return _silu(gu[:, :D_FF]) * gu[:, D_FF:]


def _ple(pg, pp, b):
    return jax.nn.sigmoid(pg + b) * pp


def _ssd_elt(small, xs_act, dtbias_row, alog_row):
    lane = _iota(small.shape, 1)
    dt = _softplus(small + dtbias_row)
    adt = jnp.where(lane < N_HEADS, -jnp.exp(alog_row) * dt, 0.0)
    head = _iota(xs_act.shape, 1) // HEAD_DIM
    dt_exp = jnp.zeros_like(xs_act)
    for h in range(N_HEADS):
        dth = jnp.sum(jnp.where(lane == h, dt, 0.0), axis=1, keepdims=True)
        dt_exp = dt_exp + jnp.where(head == h, dth, 0.0)
    return adt, xs_act * dt_exp


def _fox_elt(small, bf_row):
    lane = _iota(small.shape, 1)
    keep = (lane >= FOX_LANE0) & (lane < FOX_LANE0 + N_HEADS)
    return jnp.where(keep, -_softplus(-(small + bf_row)), 0.0)


def _lru_elt(xl, pre, b_ax, lam):
    r = jax.nn.sigmoid(pre[:, :LRU_W] + b_ax[:, :LRU_W])
    i = jax.nn.sigmoid(pre[:, LRU_W:] + b_ax[:, LRU_W:])
    log_a = -LRU_C * r * _softplus(-lam)
    a = jnp.exp(log_a)
    mult = jnp.sqrt(_neg_expm1(2.0 * log_a))
    return a, mult * (i * xl)


def _mix_post(yraw, xs_act, z, hl, lgate, yfox, dexp, g_ssd, g_lru, g_fox):
    y_ssd = _rms((yraw + xs_act * dexp) * _silu(z), g_ssd)
    y_lru = _rms(hl * _gelu(lgate), g_lru)
    y_fox = _rms(yfox, g_fox)
    return jnp.concatenate([y_ssd, y_lru, y_fox], axis=-1)


def _colsum(x):
    return jnp.sum(x, axis=0, keepdims=True)


def _shift_down(x, d):
    if d == 0:
        return x
    return jnp.where(_iota(x.shape, 0) >= d, pltpu.roll(x, d, 0), 0.0)


def _shift_up(x, d):
    if d == 0:
        return x
    s = x.shape[0]
    return jnp.where(_iota(x.shape, 0) < s - d, pltpu.roll(x, s - d, 0), 0.0)


def _conv_core(x, w, b):
    y = b + w[3:4, :] * x
    for k in range(3):
        y = y + w[k:k + 1, :] * _shift_down(x, 3 - k)
    return y


def seq_conv(name, src, col, width, w8, b, *, batch, silu, out_dtype):
    T = src.shape[0]
    S = T // batch
    c0 = col // LANE

    def body(x_ref, w_ref, b_ref, o_ref):
        y = _conv_core(x_ref[...], w_ref[...], b_ref[...])
        o_ref[...] = (_silu(y) if silu else y).astype(o_ref.dtype)

    return _pc(body, name=name, grid=(batch, width // LANE),
               in_specs=[pl.BlockSpec((S, LANE), lambda bi, ci: (bi, c0 + ci)),
                         pl.BlockSpec((8, LANE), lambda bi, ci: (0, ci)),
                         pl.BlockSpec((1, LANE), lambda bi, ci: (0, ci))],
               out_specs=pl.BlockSpec((S, LANE), lambda bi, ci: (bi, ci)),
               out_shape=SDS((T, width), out_dtype), sem=("parallel", "parallel"))(src, w8, b)


def seq_conv_bwd(name, src, col, width, w8, b, dy, *, batch, silu):
    T = src.shape[0]
    S = T // batch
    c0 = col // LANE

    def body(x_ref, w_ref, b_ref, dy_ref, dx_ref, dw_ref, db_ref):
        x, w = x_ref[...], w_ref[...]
        dpre = dy_ref[...].astype(F32)
        if silu:
            dpre = jax.vjp(_silu, _conv_core(x, w, b_ref[...]))[1](dpre)[0]
        dx = w[3:4, :] * dpre
        for k in range(3):
            dx = dx + w[k:k + 1, :] * _shift_up(dpre, 3 - k)
        dx_ref[...] = dx.astype(dx_ref.dtype)
        row8 = _iota((8, LANE), 0)
        dw = jnp.zeros((8, LANE), F32)
        for k in range(4):
            dw = dw + jnp.where(row8 == k, _colsum(dpre * _shift_down(x, 3 - k)), 0.0)
        db = _colsum(dpre)
        bi = pl.program_id(1)

        @pl.when(bi == 0)
        def _():
            dw_ref[...] = dw
            db_ref[...] = db

        @pl.when(bi > 0)
        def _():
            dw_ref[...] += dw
            db_ref[...] += db

    return _pc(body, name=name, grid=(width // LANE, batch),
               in_specs=[pl.BlockSpec((S, LANE), lambda ci, bi: (bi, c0 + ci)),
                         pl.BlockSpec((8, LANE), lambda ci, bi: (0, ci)),
                         pl.BlockSpec((1, LANE), lambda ci, bi: (0, ci)),
                         pl.BlockSpec((S, LANE), lambda ci, bi: (bi, ci))],
               out_specs=[pl.BlockSpec((S, LANE), lambda ci, bi: (bi, ci)),
                          pl.BlockSpec((8, LANE), lambda ci, bi: (0, ci)),
                          pl.BlockSpec((1, LANE), lambda ci, bi: (0, ci))],
               out_shape=[SDS((T, width), BF16), SDS((8, width), F32), SDS((1, width), F32)],
               sem=("parallel", "arbitrary"))(src, w8, b, dy)


def _split3_dot(tri, x):
    hi = x.astype(BF16)
    r1 = x - hi.astype(F32)
    mid = r1.astype(BF16)
    lo = (r1 - mid.astype(F32)).astype(BF16)
    d = lambda v: jnp.dot(tri, v, preferred_element_type=F32)
    return d(hi) + d(mid) + d(lo)


def seq_cumsum(name, x, *, batch, reverse=False, nsum=1, trow=None):
    T = x.shape[0]
    S = T // batch
    ch = min(256, S)
    nch = S // ch

    def body(x_ref, o_ref, *maybe_t):
        r, c = _iota((ch, ch), 0), _iota((ch, ch), 1)
        tri = jnp.where((c >= r) if reverse else (c <= r), 1.0, 0.0).astype(BF16)
        carry = jnp.zeros((1, LANE), F32)
        for k in (range(nch - 1, -1, -1) if reverse else range(nch)):
            xc = x_ref[k * ch:(k + 1) * ch, 0:LANE]
            for m in range(1, nsum):
                xc = xc + x_ref[k * ch:(k + 1) * ch, m * LANE:(m + 1) * LANE]
            o_ref[k * ch:(k + 1) * ch, :] = _split3_dot(tri, xc) + carry
            carry = carry + _colsum(xc)
        if trow is not None:
            maybe_t[0][...] = o_ref[...].T[trow:trow + 8, :]

    out_specs = [pl.BlockSpec((S, LANE), lambda bi: (bi, 0))]
    out_shape = [SDS((T, LANE), F32)]
    if trow is not None:
        out_specs.append(pl.BlockSpec((8, S), lambda bi: (bi, 0)))
        out_shape.append(SDS((batch * 8, S), F32))
    outs = _pc(body, name=name, grid=(batch,), in_specs=[pl.BlockSpec((S, LANE * nsum), lambda bi: (bi, 0))],
               out_specs=out_specs, out_shape=out_shape, sem=("parallel",))(x)
    return outs if trow is not None else outs[0]


def lru_scan(name, a, u, *, batch):
    T, W = a.shape
    S = T // batch

    def body(a_ref, u_ref, h_ref):
        row = _iota((8, W), 0)

        def step(g, h):
            off = pl.multiple_of(g * 8, 8)
            at, ut = a_ref[pl.ds(off, 8), :], u_ref[pl.ds(off, 8), :]
            acc = jnp.zeros((8, W), F32)
            for r in range(8):
                h = at[r:r + 1, :] * h + ut[r:r + 1, :]
                acc = jnp.where(row == r, jnp.broadcast_to(h, (8, W)), acc)
            h_ref[pl.ds(off, 8), :] = acc
            return h

        lax.fori_loop(0, S // 8, step, jnp.zeros((1, W), F32))

    spec = pl.BlockSpec((S, W), lambda bi: (bi, 0))
    return _pc(body, name=name, grid=(batch,), in_specs=[spec, spec], out_specs=spec,
               out_shape=SDS((T, W), F32), sem=("parallel",))(a, u)


def lru_scan_bwd(name, a, h, dh, *, batch):
    T, W = a.shape
    S = T // batch
    ng = S // 8

    def body(a_ref, h_ref, dh_ref, da_ref, du_ref):
        row = _iota((8, W), 0)

        def step(k, c):
            g_idx = ng - 1 - k
            off = pl.multiple_of(g_idx * 8, 8)
            offp = pl.multiple_of(jnp.maximum(g_idx - 1, 0) * 8, 8)
            at, ht, dt = a_ref[pl.ds(off, 8), :], h_ref[pl.ds(off, 8), :], dh_ref[pl.ds(off, 8), :]
            hp = jnp.where(g_idx > 0, h_ref[pl.ds(offp, 8), :], 0.0)
            da = jnp.zeros((8, W), F32)
            du = jnp.zeros((8, W), F32)
            for r in range(7, -1, -1):
                g = dt[r:r + 1, :] + c
                hprev = ht[r - 1:r, :] if r > 0 else hp[7:8, :]
                du = jnp.where(row == r, jnp.broadcast_to(g, (8, W)), du)
                da = jnp.where(row == r, jnp.broadcast_to(g * hprev, (8, W)), da)
                c = at[r:r + 1, :] * g
            da_ref[pl.ds(off, 8), :] = da
            du_ref[pl.ds(off, 8), :] = du
            return c

        lax.fori_loop(0, ng, step, jnp.zeros((1, W), F32))

    spec = pl.BlockSpec((S, W), lambda bi: (bi, 0))
    return _pc(body, name=name, grid=(batch,), in_specs=[spec] * 3, out_specs=[spec] * 2,
               out_shape=[SDS((T, W), F32)] * 2, sem=("parallel",))(a, h, dh)


def _nt(a, b):
    return lax.dot_general(a, b, (((1,), (1,)), ((), ())), preferred_element_type=F32)


def _tn(a, b):
    return lax.dot_general(a, b, (((0,), (0,)), ((), ())), preferred_element_type=F32)


def _tile(S):
    return min(256, S)


def ssd_attn_fwd(name, cm, bm, xd, cum, cum_t, *, batch):
    T = cm.shape[0]
    S = T // batch
    tq = tk = _tile(S)
    nq = S // tq

    def body(c_ref, b_ref, x_ref, cum_ref, cumt_ref, y_ref):
        i = pl.program_id(1)
        cq, cmq = cum_ref[...], c_ref[...]
        rowi, coli = _iota((tq, tk), 0), _iota((tq, tk), 1)
        half = _iota((tk, LANE), 1) // HEAD_DIM

        def step(j, accs):
            off = pl.multiple_of(j * tk, tk)
            bj = b_ref[pl.ds(off, tk), :]
            gm = [_nt(cmq[:, g * LANE:(g + 1) * LANE], bj[:, g * LANE:(g + 1) * LANE]) for g in range(2)]
            ckt = cumt_ref[:, pl.ds(off, tk)]
            causal = (rowi + i * tq) >= (coli + j * tk)
            new = []
            for p in range(3):
                xp = x_ref[pl.ds(off, tk), p * LANE:(p + 1) * LANE]
                ws, xs = [], []
                for hh in range(2):
                    h = 2 * p + hh
                    e = jnp.exp(jnp.where(causal, cq[:, h:h + 1] - ckt[h:h + 1, :], -jnp.inf))
                    ws.append((gm[h // 3] * e).astype(BF16))
                    xs.append(jnp.where(half == hh, xp, jnp.zeros_like(xp)))
                new.append(accs[p] + jnp.dot(jnp.concatenate(ws, axis=1), jnp.concatenate(xs, axis=0),
                                             preferred_element_type=F32))
            return tuple(new)

        accs = lax.fori_loop(0, i + 1, step, tuple(jnp.zeros((tq, LANE), F32) for _ in range(3)))
        y_ref[...] = jnp.concatenate(accs, axis=1)

    return _pc(body, name=name, grid=(batch, nq),
               in_specs=[pl.BlockSpec((tq, 256), lambda b, i: (b * nq + i, 0)),
                         pl.BlockSpec((S, 256), lambda b, i: (b, 0)),
                         pl.BlockSpec((S, SSD_W), lambda b, i: (b, 0)),
                         pl.BlockSpec((tq, LANE), lambda b, i: (b * nq + i, 0)),
                         pl.BlockSpec((8, S), lambda b, i: (b, 0))],
               out_specs=pl.BlockSpec((tq, SSD_W), lambda b, i: (b * nq + i, 0)),
               out_shape=SDS((T, SSD_W), F32), sem=("parallel", "parallel"))(cm, bm, xd, cum, cum_t)


def ssd_attn_bwd(name, cm, bm, xd, cum, cum_t, dy, *, batch):
    T = cm.shape[0]
    S = T // batch
    tq = tk = _tile(S)
    nq = S // tq

    def body(c_ref, b_ref, x_ref, cum_ref, cumt_ref, dy_ref, dx_ref, db_ref, dc_ref, dcum_ref, dcumt_ref):
        dx_ref[...] = jnp.zeros_like(dx_ref)
        db_ref[...] = jnp.zeros_like(db_ref)
        dcum_ref[...] = jnp.zeros_like(dcum_ref)
        dcumt_ref[...] = jnp.zeros_like(dcumt_ref)
        rowi, coli = _iota((tq, tk), 0), _iota((tq, tk), 1)
        halfq = _iota((tq, LANE), 1) // HEAD_DIM
        lane_q = _iota((tq, LANE), 1)

        def qblock(i, _):
            qoff = pl.multiple_of(i * tq, tq)
            cq = cum_ref[pl.ds(qoff, tq), :]
            cmq = c_ref[pl.ds(qoff, tq), :]
            dyq = dy_ref[pl.ds(qoff, tq), :]
            dyh = [[jnp.where(halfq == hh, dyq[:, p * LANE:(p + 1) * LANE], 0.0).astype(BF16) for hh in range(2)]
                   for p in range(3)]

            def step(j, carry):
                dcq, rs_acc = carry
                off = pl.multiple_of(j * tk, tk)
                bj = b_ref[pl.ds(off, tk), :]
                gm = [_nt(cmq[:, g * LANE:(g + 1) * LANE], bj[:, g * LANE:(g + 1) * LANE]) for g in range(2)]
                ckt = cumt_ref[:, pl.ds(off, tk)]
                causal = (rowi + i * tq) >= (coli + j * tk)
                dgm = [jnp.zeros((tq, tk), F32), jnp.zeros((tq, tk), F32)]
                for p in range(3):
                    xp = x_ref[pl.ds(off, tk), p * LANE:(p + 1) * LANE]
                    ws = []
                    for hh in range(2):
                        h = 2 * p + hh
                        e = jnp.exp(jnp.where(causal, cq[:, h:h + 1] - ckt[h:h + 1, :], -jnp.inf))
                        w = gm[h // 3] * e
                        dw = _nt(dyh[p][hh], xp)
                        zz = dw * w
                        rs_acc = rs_acc + jnp.where(lane_q == h, jnp.sum(zz, axis=1, keepdims=True), 0.0)
                        dcumt_ref[h:h + 1, pl.ds(off, tk)] += _colsum(zz)
                        dgm[h // 3] = dgm[h // 3] + dw * e
                        ws.append(w.astype(BF16))
                    dx_ref[pl.ds(off, tk), p * LANE:(p + 1) * LANE] += _tn(
                        jnp.concatenate(ws, axis=0), jnp.concatenate(dyh[p], axis=0))
                new_dcq = []
                for g in range(2):
                    dg = dgm[g].astype(BF16)
                    new_dcq.append(dcq[g] + jnp.dot(dg, bj[:, g * LANE:(g + 1) * LANE], preferred_element_type=F32))
                    db_ref[pl.ds(off, tk), g * LANE:(g + 1) * LANE] += _tn(dg, cmq[:, g * LANE:(g + 1) * LANE])
                return tuple(new_dcq), rs_acc

            dcq, rs_acc = lax.fori_loop(
                0, i + 1, step, ((jnp.zeros((tq, LANE), F32), jnp.zeros((tq, LANE), F32)), jnp.zeros((tq, LANE), F32)))
            dc_ref[pl.ds(qoff, tq), :] = jnp.concatenate(dcq, axis=1)
            dcum_ref[pl.ds(qoff, tq), :] += rs_acc
            return 0

        lax.fori_loop(0, nq, qblock, 0)
        dcum_ref[...] = dcum_ref[...] - dcumt_ref[...].T

    s256 = pl.BlockSpec((S, 256), lambda b: (b, 0))
    s384 = pl.BlockSpec((S, SSD_W), lambda b: (b, 0))
    s128 = pl.BlockSpec((S, LANE), lambda b: (b, 0))
    return _pc(body, name=name, grid=(batch,),
               in_specs=[s256, s256, s384, s128, pl.BlockSpec((8, S), lambda b: (b, 0)), s384],
               out_specs=[s384, s256, s256, s128],
               out_shape=[SDS((T, SSD_W), F32), SDS((T, 256), F32), SDS((T, 256), F32), SDS((T, LANE), F32)],
               scratch=[pltpu.VMEM((LANE, S), F32)], sem=("parallel",))(cm, bm, xd, cum, cum_t, dy)


NEG_BIG = -1e30


def fox_attn_fwd(name, proj, cum, cum_t, *, batch):
    T = proj.shape[0]
    S = T // batch
    tq = tk = _tile(S)
    nq = S // tq
    scale = HEAD_DIM ** -0.5
    qb, kb, vb = OFF_Q // LANE, OFF_K // LANE, OFF_V // LANE

    def body(q_ref, k_ref, v_ref, cum_ref, cumt_ref, o_ref, lse_ref):
        p, i = pl.program_id(1), pl.program_id(2)
        cq = cum_ref[...]
        lane_q = _iota((tq, LANE), 1)
        halfq, halfk = lane_q // HEAD_DIM, _iota((tk, LANE), 1) // HEAD_DIM
        qs = q_ref[...] * scale
        qh = [jnp.where(halfq == hh, qs, 0.0).astype(BF16) for hh in range(2)]
        rowi, coli = _iota((tq, tk), 0), _iota((tq, tk), 1)
        cqh = [jnp.sum(jnp.where(lane_q == FOX_LANE0 + 2 * p + hh, cq, 0.0), axis=1, keepdims=True) for hh in range(2)]
        row8 = _iota((8, tk), 0)

        def step(j, carry):
            ms, ls, acc = carry
            off = pl.multiple_of(j * tk, tk)
            kj = k_ref[pl.ds(off, tk), :].astype(BF16)
            vj = v_ref[pl.ds(off, tk), :].astype(BF16)
            ckt = cumt_ref[:, pl.ds(off, tk)]
            causal = (rowi + i * tq) >= (coli + j * tk)
            ps, vs, new_m, new_l, alphas = [], [], [], [], []
            for hh in range(2):
                ck = jnp.sum(jnp.where(row8 == 2 * p + hh, ckt, 0.0), axis=0, keepdims=True)
                logits = jnp.where(causal, _nt(qh[hh], kj) + cqh[hh] - ck, -jnp.inf)
                m = jnp.maximum(ms[hh], jnp.max(logits, axis=1, keepdims=True))
                alpha = jnp.exp(ms[hh] - m)
                pr = jnp.exp(logits - m)
                new_m.append(m)
                new_l.append(alpha * ls[hh] + jnp.sum(pr, axis=1, keepdims=True))
                alphas.append(alpha)
                ps.append(pr.astype(BF16))
                vs.append(jnp.where(halfk == hh, vj, jnp.zeros_like(vj)))
            acc = acc * jnp.where(halfq == 0, alphas[0], alphas[1]) + jnp.dot(
                jnp.concatenate(ps, axis=1), jnp.concatenate(vs, axis=0), preferred_element_type=F32)
            return tuple(new_m), tuple(new_l), acc

        init = ((jnp.full((tq, 1), NEG_BIG, F32),) * 2, (jnp.zeros((tq, 1), F32),) * 2, jnp.zeros((tq, LANE), F32))
        ms, ls, acc = lax.fori_loop(0, i + 1, step, init)
        o_ref[...] = acc / jnp.where(halfq == 0, ls[0], ls[1])
        lse_ref[...] = (jnp.where(lane_q == 0, ms[0] + jnp.log(ls[0]), 0.0)
                        + jnp.where(lane_q == 1, ms[1] + jnp.log(ls[1]), 0.0))

    return _pc(body, name=name, grid=(batch, 3, nq),
               in_specs=[pl.BlockSpec((tq, LANE), lambda b, p, i: (b * nq + i, qb + p)),
                         pl.BlockSpec((S, LANE), lambda b, p, i: (b, kb + p)),
                         pl.BlockSpec((S, LANE), lambda b, p, i: (b, vb + p)),
                         pl.BlockSpec((tq, LANE), lambda b, p, i: (b * nq + i, 0)),
                         pl.BlockSpec((8, S), lambda b, p, i: (b, 0))],
               out_specs=[pl.BlockSpec((tq, LANE), lambda b, p, i: (b * nq + i, p))] * 2,
               out_shape=[SDS((T, FOX_W), F32)] * 2, sem=("parallel", "parallel", "parallel"))(proj, proj, proj, cum, cum_t)


def fox_attn_bwd(name, proj, o, do, lse, cum, cum_t, *, batch):
    T = proj.shape[0]
    S = T // batch
    tq = tk = _tile(S)
    nq = S // tq
    scale = HEAD_DIM ** -0.5
    qb, kb, vb = OFF_Q // LANE, OFF_K // LANE, OFF_V // LANE

    def body(q_ref, k_ref, v_ref, o_ref, do_ref, lse_ref, cum_ref, cumt_ref,
             dq_ref, dk_ref, dv_ref, dcum_ref, dk_acc, dv_acc, dcumt_ref):
        p = pl.program_id(1)
        dk_acc[...] = jnp.zeros_like(dk_acc)
        dv_acc[...] = jnp.zeros_like(dv_acc)
        dcum_ref[...] = jnp.zeros_like(dcum_ref)
        dcumt_ref[...] = jnp.zeros_like(dcumt_ref)
        lane_q = _iota((tq, LANE), 1)
        halfq, halfk = lane_q // HEAD_DIM, _iota((tk, LANE), 1) // HEAD_DIM
        rowi, coli = _iota((tq, tk), 0), _iota((tq, tk), 1)
        row8 = _iota((8, tk), 0)

        def qblock(i, _):
            qoff = pl.multiple_of(i * tq, tq)
            cq = cum_ref[pl.ds(qoff, tq), :]
            qs = q_ref[pl.ds(qoff, tq), :] * scale
            doq = do_ref[pl.ds(qoff, tq), :]
            lse = lse_ref[pl.ds(qoff, tq), :]
            delta = doq * o_ref[pl.ds(qoff, tq), :]
            qh, doh, cqh, lseh, dlt = [], [], [], [], []
            for hh in range(2):
                qh.append(jnp.where(halfq == hh, qs, 0.0).astype(BF16))
                doh.append(jnp.where(halfq == hh, doq, 0.0).astype(BF16))
                cqh.append(jnp.sum(jnp.where(lane_q == FOX_LANE0 + 2 * p + hh, cq, 0.0), axis=1, keepdims=True))
                lseh.append(jnp.sum(jnp.where(lane_q == hh, lse, 0.0), axis=1, keepdims=True))
                dlt.append(jnp.sum(jnp.where(halfq == hh, delta, 0.0), axis=1, keepdims=True))

            def step(j, carry):
                dq, rs = carry
                off = pl.multiple_of(j * tk, tk)
                kj = k_ref[pl.ds(off, tk), :].astype(BF16)
                vj = v_ref[pl.ds(off, tk), :].astype(BF16)
                ckt = cumt_ref[:, pl.ds(off, tk)]
                causal = (rowi + i * tq) >= (coli + j * tk)
                dss, prs, ks = [], [], []
                for hh in range(2):
                    ck = jnp.sum(jnp.where(row8 == 2 * p + hh, ckt, 0.0), axis=0, keepdims=True)
                    logits = jnp.where(causal, _nt(qh[hh], kj) + cqh[hh] - ck, -jnp.inf)
                    pr = jnp.exp(logits - lseh[hh])
                    ds = pr * (_nt(doh[hh], vj) - dlt[hh])
                    rs = rs + jnp.where(lane_q == FOX_LANE0 + 2 * p + hh, jnp.sum(ds, axis=1, keepdims=True), 0.0)
                    cs = _colsum(ds)
                    dcumt_ref[0:8, pl.ds(off, tk)] += jnp.where(row8 == 2 * p + hh, cs, 0.0)
                    dss.append(ds.astype(BF16))
                    prs.append(pr.astype(BF16))
                    ks.append(jnp.where(halfk == hh, kj, jnp.zeros_like(kj)))
                dq = dq + jnp.dot(jnp.concatenate(dss, axis=1), jnp.concatenate(ks, axis=0), preferred_element_type=F32)
                dk_acc[pl.ds(off, tk), :] += _tn(jnp.concatenate(dss, axis=0), jnp.concatenate(qh, axis=0))
                dv_acc[pl.ds(off, tk), :] += _tn(jnp.concatenate(prs, axis=0), jnp.concatenate(doh, axis=0))
                return dq, rs

            dq, rs = lax.fori_loop(0, i + 1, step, (jnp.zeros((tq, LANE), F32), jnp.zeros((tq, LANE), F32)))
            dq_ref[pl.ds(qoff, tq), :] = (dq * scale).astype(dq_ref.dtype)
            dcum_ref[pl.ds(qoff, tq), :] += rs
            return 0

        lax.fori_loop(0, nq, qblock, 0)
        dk_ref[...] = dk_acc[...].astype(dk_ref.dtype)
        dv_ref[...] = dv_acc[...].astype(dv_ref.dtype)
        dct = dcumt_ref[...].T
        dcum_ref[...] = dcum_ref[...] - pltpu.roll(dct, FOX_LANE0, 1)

    sp = lambda c0: pl.BlockSpec((S, LANE), lambda b, p: (b, c0 + p))
    s0 = pl.BlockSpec((S, LANE), lambda b, p: (b, 0))
    return _pc(body, name=name, grid=(batch, 3),
               in_specs=[sp(qb), sp(kb), sp(vb), sp(0), sp(0), sp(0), s0, pl.BlockSpec((8, S), lambda b, p: (b, 0))],
               out_specs=[sp(0)] * 4,
               out_shape=[SDS((T, FOX_W), BF16)] * 3 + [SDS((T, FOX_W), F32)],
               scratch=[pltpu.VMEM((S, LANE), F32), pltpu.VMEM((S, LANE), F32), pltpu.VMEM((LANE, S), F32)],
               sem=("parallel", "parallel"))(proj, proj, proj, o, do, lse, cum, cum_t)


def _row(v, width=None, at=0):
    v = v.astype(F32)
    width = width or v.shape[0]
    return jnp.pad(v, (at, width - at - v.shape[0]))[None, :]


def _pad8(w4):
    return jnp.pad(w4.astype(F32), ((0, 4), (0, 0)))


def _block_diag(w):
    out = jnp.zeros((LRU_W, LRU_W), w.dtype)
    for g in range(4):
        out = lax.dynamic_update_slice(out, w[g], (g * 64, g * 64))
    return out


def prep_layer(f):
    cw, cb = f["ssd_conv_w"], f["ssd_conv_b"]
    return dict(
        win=permute_in_cols(f["w_in"]), wout=f["w_out"], wgu=jnp.concatenate([f["w_gate"], f["w_up"]], axis=1),
        wd=f["w_down"], wpg=f["w_ple_gate"], wpp=f["w_ple_proj"],
        wax=jnp.concatenate([_block_diag(f["lru_w_a"]), _block_diag(f["lru_w_x"])], axis=1),
        g1=_row(f["norm1_g"]), g2=_row(f["norm2_g"]), g3=_row(f["norm3_g"]),
        cw_xs=_pad8(cw[:, :384]), cb_xs=_row(cb[:384]), cw_b=_pad8(cw[:, 384:640]), cb_b=_row(cb[384:640]),
        cw_c=_pad8(cw[:, 640:]), cb_c=_row(cb[640:]), cw_l=_pad8(f["lru_conv_w"]), cb_l=_row(f["lru_conv_b"]),
        dtbias_row=_row(f["ssd_dt_bias"], LANE), alog_row=_row(f["ssd_a_log"], LANE),
        dexp=jnp.repeat(f["ssd_d"].astype(F32), HEAD_DIM)[None, :], g_ssd=_row(f["ssd_norm_g"]),
        b_ax=_row(jnp.concatenate([f["lru_b_a"], f["lru_b_x"]])), lam=_row(f["lru_lambda"]), g_lru=_row(f["lru_norm_g"]),
        bf_row=_row(f["fox_b_f"], LANE, FOX_LANE0), g_fox=_row(f["fox_norm_g"]), b_pg=_row(f["b_ple_gate"]))


def unprep_grads(g):
    blocks = lambda m: jnp.stack([m[i * 64:(i + 1) * 64, i * 64:(i + 1) * 64] for i in range(4)])
    return dict(
        norm1_g=g["g1"][0], w_in=unpermute_in_cols(g["win"]),
        ssd_conv_w=jnp.concatenate([g["cw_xs"][:4], g["cw_b"][:4], g["cw_c"][:4]], axis=1),
        ssd_conv_b=jnp.concatenate([g["cb_xs"][0], g["cb_b"][0], g["cb_c"][0]]),
        ssd_dt_bias=g["dtbias_row"][0, :N_HEADS], ssd_a_log=g["alog_row"][0, :N_HEADS],
        ssd_d=jnp.sum(g["dexp"].reshape(N_HEADS, HEAD_DIM), axis=1), ssd_norm_g=g["g_ssd"][0],
        lru_conv_w=g["cw_l"][:4], lru_conv_b=g["cb_l"][0],
        lru_w_a=blocks(g["wax"][:, :LRU_W]), lru_b_a=g["b_ax"][0, :LRU_W],
        lru_w_x=blocks(g["wax"][:, LRU_W:]), lru_b_x=g["b_ax"][0, LRU_W:],
        lru_lambda=g["lam"][0], lru_norm_g=g["g_lru"][0],
        fox_b_f=g["bf_row"][0, FOX_LANE0:FOX_LANE0 + N_HEADS], fox_norm_g=g["g_fox"][0],
        w_out=g["wout"], norm2_g=g["g2"][0], w_gate=g["wgu"][:, :D_FF], w_up=g["wgu"][:, D_FF:], w_down=g["wd"],
        norm3_g=g["g3"][0], w_ple_gate=g["wpg"], b_ple_gate=g["b_pg"][0], w_ple_proj=g["wpp"])


def _view(a, off, width):
    return (a, off // width, width)


def _add_epilogue(acc, e):
    return (acc + e,)


def mixer_fwd(proj, w, batch, tag):
    sm = _view(proj, OFF_SM, LANE)
    conv = functools.partial(seq_conv, batch=batch)
    cmc = conv(f"{tag}_conv_c", proj, OFF_C, 256, w["cw_c"], w["cb_c"], silu=True, out_dtype=BF16)
    bmc = conv(f"{tag}_conv_b", proj, OFF_B, 256, w["cw_b"], w["cb_b"], silu=True, out_dtype=BF16)
    xs_act = conv(f"{tag}_conv_xs", proj, OFF_XS, SSD_W, w["cw_xs"], w["cb_xs"], silu=True, out_dtype=F32)
    xl = conv(f"{tag}_conv_l", proj, OFF_LX, LRU_W, w["cw_l"], w["cb_l"], silu=False, out_dtype=F32)
    adt, xd = rowwise(f"{tag}_ssd_elt", _ssd_elt, [sm, xs_act], [w["dtbias_row"], w["alog_row"]],
                      [(LANE, F32), (SSD_W, BF16)])
    cum_a, cum_at = seq_cumsum(f"{tag}_cum_a", adt, batch=batch, trow=0)
    yraw = ssd_attn_fwd(f"{tag}_ssd_fwd", cmc, bmc, xd, cum_a, cum_at, batch=batch)
    logf = rowwise(f"{tag}_fox_elt", _fox_elt, [sm], [w["bf_row"]], [(LANE, F32)])
    cum_f, cum_ft = seq_cumsum(f"{tag}_cum_f", logf, batch=batch, trow=FOX_LANE0)
    o, lse = fox_attn_fwd(f"{tag}_fox_fwd", proj, cum_f, cum_ft, batch=batch)
    pre = mm(xl, w["wax"], name=f"{tag}_mm_lru_gates")
    a, u = rowwise(f"{tag}_lru_elt", _lru_elt, [xl, pre], [w["b_ax"], w["lam"]], [(LRU_W, F32), (LRU_W, F32)])
    hl = lru_scan(f"{tag}_lru_scan", a, u, batch=batch)
    ycat = rowwise(f"{tag}_mix_post", _mix_post,
                   [yraw, xs_act, _view(proj, OFF_Z, SSD_W), hl, _view(proj, OFF_LG, LRU_W), o],
                   [w["dexp"], w["g_ssd"], w["g_lru"], w["g_fox"]], [(D_MODEL, BF16)], tr=256)
    saved = dict(cmc=cmc, bmc=bmc, xs_act=xs_act, xl=xl, xd=xd, cum_a=cum_a, cum_at=cum_at, yraw=yraw,
                 cum_f=cum_f, cum_ft=cum_ft, o=o, lse=lse, pre=pre, a=a, hl=hl)
    return ycat, saved


def mixer_bwd(dycat, proj, w, s, batch, tag):
    sm = _view(proj, OFF_SM, LANE)
    g = {}

    def post_bwd(yraw, xs_act, z, hl, lg, o, dyc, dexp, g_ssd, g_lru, g_fox):
        return jax.vjp(_mix_post, yraw, xs_act, z, hl, lg, o, dexp, g_ssd, g_lru, g_fox)[1](dyc)

    (dyraw, dxs1, dz, dhl, dlg, do, g["dexp"], g["g_ssd"], g["g_lru"], g["g_fox"]) = rowwise(
        f"{tag}_mix_post_bwd", post_bwd,
        [s["yraw"], s["xs_act"], _view(proj, OFF_Z, SSD_W), s["hl"], _view(proj, OFF_LG, LRU_W), s["o"], dycat],
        [w["dexp"], w["g_ssd"], w["g_lru"], w["g_fox"]],
        [(SSD_W, F32), (SSD_W, F32), (SSD_W, BF16), (LRU_W, F32), (LRU_W, BF16), (FOX_W, F32)],
        [SSD_W, SSD_W, LRU_W, FOX_W], tr=256)

    dq, dk, dv, dcum3 = fox_attn_bwd(f"{tag}_fox_bwd", proj, s["o"], do, s["lse"], s["cum_f"], s["cum_ft"], batch=batch)
    dlogf = seq_cumsum(f"{tag}_rcum_f", dcum3, batch=batch, reverse=True, nsum=3)

    dxd, dbm, dcm, dcum_a = ssd_attn_bwd(f"{tag}_ssd_bwd", s["cmc"], s["bmc"], s["xd"], s["cum_a"], s["cum_at"], dyraw,
                                         batch=batch)
    dadt = seq_cumsum(f"{tag}_rcum_a", dcum_a, batch=batch, reverse=True)

    def ssd_elt_bwd(small, xs_act, dadt_, dxd_, dxs1_, dtbias, alog):
        dsm, dxs, ddtb, dalog = jax.vjp(_ssd_elt, small, xs_act, dtbias, alog)[1]((dadt_, dxd_))
        return dsm, dxs + dxs1_, ddtb, dalog

    dsm_s, dxs_act, g["dtbias_row"], g["alog_row"] = rowwise(
        f"{tag}_ssd_elt_bwd", ssd_elt_bwd, [sm, s["xs_act"], dadt, dxd, dxs1], [w["dtbias_row"], w["alog_row"]],
        [(LANE, F32), (SSD_W, F32)], [LANE, LANE])

    def fox_elt_bwd(small, dlogf_, dsm_s_, bf_row):
        dsm, dbf = jax.vjp(_fox_elt, small, bf_row)[1](dlogf_)
        return dsm + dsm_s_, dbf

    dsm, g["bf_row"] = rowwise(f"{tag}_fox_elt_bwd", fox_elt_bwd, [sm, dlogf, dsm_s], [w["bf_row"]],
                               [(LANE, BF16)], [LANE])

    cbwd = functools.partial(seq_conv_bwd, batch=batch)
    dxs_raw, g["cw_xs"], g["cb_xs"] = cbwd(f"{tag}_conv_xs_bwd", proj, OFF_XS, SSD_W, w["cw_xs"], w["cb_xs"], dxs_act, silu=True)
    db_raw, g["cw_b"], g["cb_b"] = cbwd(f"{tag}_conv_b_bwd", proj, OFF_B, 256, w["cw_b"], w["cb_b"], dbm, silu=True)
    dc_raw, g["cw_c"], g["cb_c"] = cbwd(f"{tag}_conv_c_bwd", proj, OFF_C, 256, w["cw_c"], w["cb_c"], dcm, silu=True)

    da, du = lru_scan_bwd(f"{tag}_lru_scan_bwd", s["a"], s["hl"], dhl, batch=batch)

    def lru_elt_bwd(xl, pre, da_, du_, b_ax, lam):
        return jax.vjp(_lru_elt, xl, pre, b_ax, lam)[1]((da_, du_))

    dxl1, dpre, g["b_ax"], g["lam"] = rowwise(
        f"{tag}_lru_elt_bwd", lru_elt_bwd, [s["xl"], s["pre"], da, du], [w["b_ax"], w["lam"]],
        [(LRU_W, F32), (2 * LRU_W, BF16)], [2 * LRU_W, LRU_W])
    g["wax"] = mm(s["xl"], dpre, ta=True, name=f"{tag}_mm_dwax")
    dxl = mm(dpre, w["wax"], tb=True, extras=[dxl1], epilogue=_add_epilogue, name=f"{tag}_mm_dxl")
    dlx_raw, g["cw_l"], g["cb_l"] = cbwd(f"{tag}_conv_l_bwd", proj, OFF_LX, LRU_W, w["cw_l"], w["cb_l"], dxl, silu=False)

    dproj = jnp.concatenate([db_raw, dc_raw, dlx_raw, dlg, dsm, dz, dxs_raw, dq, dk, dv], axis=1)
    return dproj, g


def layer_fwd(h0, p_l, w, batch, tag):
    u1 = rowwise(f"{tag}_rms1", _rms, [h0], [w["g1"]], [(D_MODEL, BF16)])
    proj = mm(u1, w["win"], name=f"{tag}_mm_in")
    ycat, ms = mixer_fwd(proj, w, batch, tag)
    h1 = mm(ycat, w["wout"], extras=[h0], epilogue=_add_epilogue, name=f"{tag}_mm_out")
    u2 = rowwise(f"{tag}_rms2", _rms, [h1], [w["g2"]], [(D_MODEL, BF16)])
    gu = mm(u2, w["wgu"], name=f"{tag}_mm_gu")
    act = rowwise(f"{tag}_swiglu", _swiglu, [gu], [], [(D_FF, BF16)], tr=256)
    h2 = mm(act, w["wd"], extras=[h1], epilogue=_add_epilogue, name=f"{tag}_mm_down")
    u3 = rowwise(f"{tag}_rms3", _rms, [h2], [w["g3"]], [(D_MODEL, BF16)])
    pg = mm(u3, w["wpg"], name=f"{tag}_mm_pg")
    pp = mm(p_l, w["wpp"], name=f"{tag}_mm_pp")
    h3 = rowwise(f"{tag}_ple", lambda pg_, pp_, h_, b: h_ + _ple(pg_, pp_, b), [pg, pp, h2], [w["b_pg"]],
                 [(D_MODEL, F32)])
    saved = dict(h0=h0, u1=u1, proj=proj, ycat=ycat, h1=h1, u2=u2, gu=gu, act=act, h2=h2, u3=u3, pg=pg, pp=pp, mixer=ms)
    return h3, saved


def _rms_bwd(h, du, dres, g):
    dh, dg = jax.vjp(_rms, h, g)[1](du)
    return dh + dres, dg


def layer_bwd(dh3, p_l, w, s, batch, tag):
    def ple_bwd(pg, pp, dh, b):
        return jax.vjp(_ple, pg, pp, b)[1](dh)

    d_pg, d_pp, g_bpg = rowwise(f"{tag}_ple_bwd", ple_bwd, [s["pg"], s["pp"], dh3], [w["b_pg"]],
                                [(D_MODEL, BF16), (D_MODEL, BF16)], [D_MODEL])
    g = dict(b_pg=g_bpg)
    g["wpp"] = mm(p_l, d_pp, ta=True, name=f"{tag}_mm_dwpp")
    g["wpg"] = mm(s["u3"], d_pg, ta=True, name=f"{tag}_mm_dwpg")
    du3 = mm(d_pg, w["wpg"], tb=True, name=f"{tag}_mm_du3")
    dh2, g["g3"] = rowwise(f"{tag}_rms3_bwd", _rms_bwd, [s["h2"], du3, dh3], [w["g3"]], [(D_MODEL, F32)], [D_MODEL])

    dact = mm(dh2, w["wd"], tb=True, name=f"{tag}_mm_dact")
    g["wd"] = mm(s["act"], dh2, ta=True, name=f"{tag}_mm_dwd")
    dgu = rowwise(f"{tag}_swiglu_bwd", lambda gu, da: jax.vjp(_swiglu, gu)[1](da)[0], [s["gu"], dact], [],
                  [(2 * D_FF, BF16)], tr=256)
    g["wgu"] = mm(s["u2"], dgu, ta=True, name=f"{tag}_mm_dwgu")
    du2 = mm(dgu, w["wgu"], tb=True, name=f"{tag}_mm_du2")
    dh1, g["g2"] = rowwise(f"{tag}_rms2_bwd", _rms_bwd, [s["h1"], du2, dh2], [w["g2"]], [(D_MODEL, F32)], [D_MODEL])

    dycat = mm(dh1, w["wout"], tb=True, name=f"{tag}_mm_dycat")
    g["wout"] = mm(s["ycat"], dh1, ta=True, name=f"{tag}_mm_dwout")
    dproj, gm = mixer_bwd(dycat, s["proj"], w, s["mixer"], batch, tag)
    g.update(gm)
    g["win"] = mm(s["u1"], dproj, ta=True, name=f"{tag}_mm_dwin")
    du1 = mm(dproj, w["win"], tb=True, name=f"{tag}_mm_du1")
    dh0, g["g1"] = rowwise(f"{tag}_rms1_bwd", _rms_bwd, [s["h0"], du1, dh1], [w["g1"]], [(D_MODEL, F32)], [D_MODEL])
    return dh0, g


def _loss_fwd_bwd(h, tgt, gf):
    def f(h_, gf_):
        e = _rms(h_, gf_) - tgt
        return 0.5 * jnp.sum(jnp.mean(e * e, axis=-1, keepdims=True), axis=0, keepdims=True)

    loss, vj = jax.vjp(f, h, gf)
    dh, dgf = vj(jnp.ones((1, 1), F32))
    return dh, jnp.broadcast_to(loss, (1, LANE)), dgf


def local_step(x, p, tgt, layers, final_g):
    batch, S, _ = x.shape
    T = batch * S
    h = x.reshape(T, D_MODEL)
    saved = []
    for l, w in enumerate(layers):
        h, s = layer_fwd(h, p[l].reshape(T, PLE_DIM), w, batch, f"l{l}")
        saved.append(s)
    dh, loss, dgf = rowwise("loss", _loss_fwd_bwd, [h, tgt.reshape(T, D_MODEL)], [_row(final_g)],
                            [(D_MODEL, F32)], [LANE, D_MODEL], tr=256)
    grads = [None] * len(layers)
    for l in reversed(range(len(layers))):
        dh, grads[l] = layer_bwd(dh, p[l].reshape(T, PLE_DIM), layers[l], saved[l], batch, f"l{l}")
    return loss[0, 0], dh.reshape(batch, S, D_MODEL), grads, dgf[0]


MESH = pl.DeviceIdType.MESH
N_DEV = 8
N_CHIP = 4
ANY = pl.BlockSpec(memory_space=pl.ANY)


def _pos():
    return lax.axis_index("x"), lax.axis_index("y"), lax.axis_index("c")


def _comm_call(body, name, out_shape, n_in, scratch):
    return pl.pallas_call(body, name=name, out_shape=out_shape, in_specs=[ANY] * n_in, out_specs=ANY,
                          scratch_shapes=scratch)


def all_gather8(name, blk):
    def body(x_ref, out_ref, send_sems, recv_sems, local_sem):
        x, y, c = _pos()
        me, sibling = (x, y, c), (x, y, 1 - c)
        chips = [(1 - x, y), (x, 1 - y), (1 - x, 1 - y)]

        def rows(px, py, pcore):
            return out_ref.at[4 * px + 2 * py + pcore]

        def copy(k, block, to, src=None):
            return pltpu.make_async_remote_copy(
                src_ref=rows(*block) if src is None else src, dst_ref=rows(*block),
                send_sem=send_sems.at[k], recv_sem=recv_sems.at[k], device_id=to, device_id_type=MESH)

        mine = pltpu.make_async_copy(x_ref, rows(*me), local_sem)
        mine.start()
        first = [copy(0, me, sibling, src=x_ref)]
        first += [copy(1 + j, me, (*chip, c), src=x_ref) for j, chip in enumerate(chips)]
        for cp in first:
            cp.start()
        passed = [copy(4 + j, (*chip, c), sibling) for j, chip in enumerate(chips)]
        for j, chip in enumerate(chips):
            copy(1 + j, (*chip, c), me).wait_recv()
            passed[j].start()
        copy(0, sibling, me).wait_recv()
        for j, chip in enumerate(chips):
            copy(4 + j, (*chip, 1 - c), me).wait_recv()
        for cp in first + passed:
            cp.wait_send()
        mine.wait()

    return _comm_call(body, name, SDS((N_DEV,) + blk.shape, blk.dtype), 1,
                      [pltpu.SemaphoreType.DMA((7,)), pltpu.SemaphoreType.DMA((7,)), pltpu.SemaphoreType.DMA])(blk)


def sibling_swap(name, v):
    def body(v_ref, out_ref, send_sem, recv_sem):
        x, y, c = _pos()
        cp = pltpu.make_async_remote_copy(src_ref=v_ref, dst_ref=out_ref, send_sem=send_sem, recv_sem=recv_sem,
                                          device_id=(x, y, 1 - c), device_id_type=MESH)
        cp.start()
        cp.wait()

    return _comm_call(body, name, SDS(v.shape, v.dtype), 1, [pltpu.SemaphoreType.DMA, pltpu.SemaphoreType.DMA])(v)


def chip_all_to_all(name, v):
    def body(v_ref, out_ref, send_sems, recv_sems, local_sem):
        x, y, c = _pos()
        my_chip = 2 * x + y
        chips = [(1 - x, y), (x, 1 - y), (1 - x, 1 - y)]
        mine = pltpu.make_async_copy(v_ref.at[my_chip], out_ref.at[my_chip], local_sem)
        mine.start()
        cps = [pltpu.make_async_remote_copy(
            src_ref=v_ref.at[2 * px + py], dst_ref=out_ref.at[my_chip], send_sem=send_sems.at[k],
            recv_sem=recv_sems.at[k], device_id=(px, py, c), device_id_type=MESH) for k, (px, py) in enumerate(chips)]
        for cp in cps:
            cp.start()
        for k, (px, py) in enumerate(chips):
            pltpu.make_async_remote_copy(
                src_ref=v_ref.at[my_chip], dst_ref=out_ref.at[2 * px + py], send_sem=send_sems.at[k],
                recv_sem=recv_sems.at[k], device_id=(px, py, c), device_id_type=MESH).wait_recv()
        for cp in cps:
            cp.wait_send()
        mine.wait()

    return _comm_call(body, name, SDS(v.shape, v.dtype), 1,
                      [pltpu.SemaphoreType.DMA((3,)), pltpu.SemaphoreType.DMA((3,)), pltpu.SemaphoreType.DMA])(v)


def sum_slices(name, v, tr=512):
    n, R, C = v.shape
    tr = _pick(R, (tr, 256, 128, 64, 32, 16, 8))

    def body(v_ref, o_ref):
        acc = v_ref[0]
        for k in range(1, n):
            acc = acc + v_ref[k]
        o_ref[...] = acc

    return _pc(body, name=name, grid=(R // tr,), in_specs=[pl.BlockSpec((n, tr, C), lambda i: (0, i, 0))],
               out_specs=pl.BlockSpec((tr, C), lambda i: (i, 0)), out_shape=SDS((R, C), v.dtype), sem=("parallel",))(v)


def add_slices(name, a, b):
    n, R, C = a.shape
    tr = _pick(R, (512, 256, 128, 64, 32, 16, 8))

    def body(a_ref, b_ref, o_ref):
        o_ref[...] = a_ref[...] + b_ref[...]

    spec = pl.BlockSpec((1, tr, C), lambda k, i: (k, i, 0))
    return _pc(body, name=name, grid=(n, R // tr), in_specs=[spec, spec], out_specs=spec,
               out_shape=SDS(a.shape, a.dtype), sem=("parallel", "parallel"))(a, b)


def adamw(name, w, g, m, v):
    L, R, C = w.shape
    tr = _pick(R, (256, 128, 64, 32, 16, 8))
    c1 = 1.0 / (1.0 - ADAM_B1 ** ADAM_STEP)
    c2 = 1.0 / (1.0 - ADAM_B2 ** ADAM_STEP)

    def body(w_ref, g_ref, m_ref, v_ref, d_ref, nm_ref, nv_ref):
        gv = g_ref[...]
        nm = ADAM_B1 * m_ref[...] + (1.0 - ADAM_B1) * gv
        nv = ADAM_B2 * v_ref[...] + (1.0 - ADAM_B2) * (gv * gv)
        d_ref[...] = -ADAM_LR * ((nm * c1) / (jnp.sqrt(nv * c2) + ADAM_EPS) + ADAM_WD * w_ref[...])
        nm_ref[...] = nm
        nv_ref[...] = nv

    spec = pl.BlockSpec((1, tr, C), lambda l, i: (l, i, 0))
    return _pc(body, name=name, grid=(L, R // tr), in_specs=[spec] * 4, out_specs=[spec] * 3,
               out_shape=[SDS(w.shape, F32)] * 3, sem=("parallel", "parallel"))(w, g, m, v)


WEIGHTS = ["norm1_g", "w_in", "ssd_conv_w", "ssd_conv_b", "ssd_dt_bias", "ssd_a_log", "ssd_d", "ssd_norm_g",
           "lru_conv_w", "lru_conv_b", "lru_w_a", "lru_b_a", "lru_w_x", "lru_b_x", "lru_lambda", "lru_norm_g",
           "fox_b_f", "fox_norm_g", "w_out", "norm2_g", "w_gate", "w_up", "w_down", "norm3_g", "w_ple_gate",
           "b_ple_gate", "w_ple_proj", "final_norm_g"]
BIG = {"w_in": 2, "w_out": 1, "w_gate": 2, "w_up": 2, "w_down": 1, "w_ple_gate": 1, "w_ple_proj": 2}
SHARDED_SMALL = {"ssd_conv_w": 2, "lru_conv_w": 2}
SMALL = [n for n in WEIGHTS if n not in BIG]


def _pack(arrs, rows_multiple):
    flat = jnp.concatenate([a.reshape(-1) for a in arrs])
    per = rows_multiple * LANE
    n = -(-flat.shape[0] // per) * per
    return jnp.pad(flat, (0, n - flat.shape[0])).reshape(n // LANE, LANE)


def _unpack(flat2d, shapes):
    flat = flat2d.reshape(-1)
    out, off = [], 0
    for s in shapes:
        n = int(np.prod(s))
        out.append(flat[off:off + n].reshape(s))
        off += n
    return out


def _gather_shards(name, shards, axes, dtype):
    c = lax.axis_index("c")
    packed = _pack([s.astype(dtype) for s in shards], 32)
    half = packed.shape[0] // 2
    mine = lax.dynamic_slice_in_dim(packed, c * half, half, 0)
    got = all_gather8(name, mine).reshape(N_CHIP, 2 * half, LANE)
    per_chip = [_unpack(got[k], [s.shape for s in shards]) for k in range(N_CHIP)]
    return [jnp.concatenate([per_chip[k][i] for k in range(N_CHIP)], axis=ax) for i, ax in enumerate(axes)]


def _split_shards(full, ax):
    n = full.shape[ax] // N_CHIP
    return [lax.slice_in_dim(full, k * n, (k + 1) * n, axis=ax) for k in range(N_CHIP)]


def kernel(x, p, norm1_g, w_in, ssd_conv_w, ssd_conv_b, ssd_dt_bias, ssd_a_log, ssd_d, ssd_norm_g, lru_conv_w, lru_conv_b, lru_w_a, lru_b_a, lru_w_x, lru_b_x, lru_lambda, lru_norm_g, fox_b_f, fox_norm_g, w_out, norm2_g, w_gate, w_up, w_down, norm3_g, w_ple_gate, b_ple_gate, w_ple_proj, final_norm_g, loss_target, m_norm1_g, m_w_in, m_ssd_conv_w, m_ssd_conv_b, m_ssd_dt_bias, m_ssd_a_log, m_ssd_d, m_ssd_norm_g, m_lru_conv_w, m_lru_conv_b, m_lru_w_a, m_lru_b_a, m_lru_w_x, m_lru_b_x, m_lru_lambda, m_lru_norm_g, m_fox_b_f, m_fox_norm_g, m_w_out, m_norm2_g, m_w_gate, m_w_up, m_w_down, m_norm3_g, m_w_ple_gate, m_b_ple_gate, m_w_ple_proj, m_final_norm_g, v_norm1_g, v_w_in, v_ssd_conv_w, v_ssd_conv_b, v_ssd_dt_bias, v_ssd_a_log, v_ssd_d, v_ssd_norm_g, v_lru_conv_w, v_lru_conv_b, v_lru_w_a, v_lru_b_a, v_lru_w_x, v_lru_b_x, v_lru_lambda, v_lru_norm_g, v_fox_b_f, v_fox_norm_g, v_w_out, v_norm2_g, v_w_gate, v_w_up, v_w_down, v_norm3_g, v_w_ple_gate, v_b_ple_gate, v_w_ple_proj, v_final_norm_g):
    args = (norm1_g, w_in, ssd_conv_w, ssd_conv_b, ssd_dt_bias, ssd_a_log, ssd_d, ssd_norm_g, lru_conv_w, lru_conv_b, lru_w_a, lru_b_a, lru_w_x, lru_b_x, lru_lambda, lru_norm_g, fox_b_f, fox_norm_g, w_out, norm2_g, w_gate, w_up, w_down, norm3_g, w_ple_gate, b_ple_gate, w_ple_proj, final_norm_g)
    m_args = (m_norm1_g, m_w_in, m_ssd_conv_w, m_ssd_conv_b, m_ssd_dt_bias, m_ssd_a_log, m_ssd_d, m_ssd_norm_g, m_lru_conv_w, m_lru_conv_b, m_lru_w_a, m_lru_b_a, m_lru_w_x, m_lru_b_x, m_lru_lambda, m_lru_norm_g, m_fox_b_f, m_fox_norm_g, m_w_out, m_norm2_g, m_w_gate, m_w_up, m_w_down, m_norm3_g, m_w_ple_gate, m_b_ple_gate, m_w_ple_proj, m_final_norm_g)
    v_args = (v_norm1_g, v_w_in, v_ssd_conv_w, v_ssd_conv_b, v_ssd_dt_bias, v_ssd_a_log, v_ssd_d, v_ssd_norm_g, v_lru_conv_w, v_lru_conv_b, v_lru_w_a, v_lru_b_a, v_lru_w_x, v_lru_b_x, v_lru_lambda, v_lru_norm_g, v_fox_b_f, v_fox_norm_g, v_w_out, v_norm2_g, v_w_gate, v_w_up, v_w_down, v_norm3_g, v_w_ple_gate, v_b_ple_gate, v_w_ple_proj, v_final_norm_g)
    w = dict(zip(WEIGHTS, args))
    mom = dict(zip(WEIGHTS, m_args))
    var = dict(zip(WEIGHTS, v_args))
    xi, yi, ci = _pos()
    chip = 2 * xi + yi

    big_names = list(BIG)
    full = dict(zip(big_names, _gather_shards("gather_big", [w[n] for n in big_names], [BIG[n] for n in big_names], BF16)))
    full.update(zip(SHARDED_SMALL, _gather_shards("gather_conv", [w[n] for n in SHARDED_SMALL],
                                                  list(SHARDED_SMALL.values()), F32)))
    for n in WEIGHTS:
        full.setdefault(n, w[n])
    layers = [prep_layer({n: full[n][l] for n in WEIGHTS if n != "final_norm_g"}) for l in range(DEPTH)]

    loss, grad_x, grads, g_final = local_step(x, p, loss_target, layers, final_norm_g)
    loss = lax.psum(loss, ("x", "y", "c"))
    gl = [unprep_grads(g) for g in grads]
    gfull = {n: jnp.stack([gl[l][n] for l in range(DEPTH)]) for n in WEIGHTS if n != "final_norm_g"}
    gfull["final_norm_g"] = g_final

    small_shapes = [gfull[n].shape for n in SMALL]
    gs = _pack([gfull[n] for n in SMALL], 8)
    gs = sum_slices("sum_small", all_gather8("gather_small_grads", gs))
    gsum = dict(zip(SMALL, _unpack(gs, small_shapes)))
    for n, ax in SHARDED_SMALL.items():
        k = gsum[n].shape[ax] // N_CHIP
        gsum[n] = lax.dynamic_slice_in_dim(gsum[n], chip * k, k, ax)

    per_chip = [_pack([_split_shards(gfull[n], BIG[n])[k] for n in big_names], 1024) for k in range(N_CHIP)]
    gb = jnp.stack(per_chip)
    half = gb.shape[1] // 2
    keep = lax.dynamic_slice_in_dim(gb, ci * half, half, 1)
    give = lax.dynamic_slice_in_dim(gb, (1 - ci) * half, half, 1)
    part = add_slices("add_sibling", keep, sibling_swap("swap_halves", give))
    mine = sum_slices("sum_chips", chip_all_to_all("a2a_chips", part))
    other = sibling_swap("swap_result", mine)
    lo = jnp.where(ci == 0, mine, other)
    hi = jnp.where(ci == 0, other, mine)
    gsum.update(zip(big_names, _unpack(jnp.concatenate([lo, hi], axis=0), [w[n].shape for n in big_names])))

    delta, new_m, new_v = {}, {}, {}
    for n in big_names:
        delta[n], new_m[n], new_v[n] = adamw(f"adamw_{n}", w[n], gsum[n], mom[n], var[n])
    shapes = [w[n].shape for n in SMALL]
    pk = lambda d: _pack([d[n] for n in SMALL], 8)[None]
    ds, ms, vs = adamw("adamw_small", pk(w), pk(gsum), pk(mom), pk(var))
    for d, packed in ((delta, ds), (new_m, ms), (new_v, vs)):
        d.update(zip(SMALL, _unpack(packed[0], shapes)))

    return (loss, grad_x, *[gsum[n] for n in WEIGHTS], *[delta[n] for n in WEIGHTS],
            *[new_m[n] for n in WEIGHTS], *[new_v[n] for n in WEIGHTS])
```

```python
import functools
import math

import jax
import jax.numpy as jnp
import numpy as np
from jax import lax
from jax.experimental import pallas as pl
from jax.experimental.pallas import tpu as pltpu

F32, BF16 = jnp.float32, jnp.bfloat16
SDS = jax.ShapeDtypeStruct

D_MODEL = 1024
DEPTH = 2
HEAD_DIM = 64
N_HEADS = 6
SSD_W, LRU_W, FOX_W = 384, 256, 384
D_FF = 2816
PLE_DIM = 256
IN_COLS = 2956
EPS = 1e-6
LRU_C = 8.0
LANE = 128
V7X_VMEM_LIMIT = 56 * 1024 * 1024

PW = 3072
OFF_B, OFF_C, OFF_LX, OFF_LG, OFF_SM, OFF_Z, OFF_XS, OFF_Q, OFF_K, OFF_V = (
    0, 256, 512, 768, 1024, 1152, 1536, 1920, 2304, 2688)
FOX_LANE0 = 8

ADAM_LR, ADAM_B1, ADAM_B2, ADAM_EPS, ADAM_WD, ADAM_STEP = 0.001, 0.9, 0.999, 1e-08, 0.01, 10


def _iota(shape, dim):
    return lax.broadcasted_iota(jnp.int32, shape, dim)


def _pc(body, *, name, grid, in_specs, out_specs, out_shape, scratch=(), sem=None):
    return pl.pallas_call(
        body, name=name, grid=grid, in_specs=in_specs, out_specs=out_specs, out_shape=out_shape,
        scratch_shapes=list(scratch),
        compiler_params=pltpu.CompilerParams(dimension_semantics=sem, vmem_limit_bytes=V7X_VMEM_LIMIT))


def permute_in_cols(w):
    z = lambda n: jnp.zeros(w.shape[:-1] + (n,), w.dtype)
    s = lambda a, b: w[..., a:b]
    return jnp.concatenate([
        s(768, 1024), s(1024, 1280), s(1286, 1542), s(1542, 1798),
        s(1280, 1286), z(2), s(2950, 2956), z(LANE - 14),
        s(0, 384), s(384, 768), s(1798, 2182), s(2182, 2566), s(2566, 2950)], axis=-1)


def unpermute_in_cols(g):
    s = lambda a, n: g[..., a:a + n]
    return jnp.concatenate([
        s(OFF_Z, 384), s(OFF_XS, 384), s(OFF_B, 256), s(OFF_C, 256), s(OFF_SM, 6),
        s(OFF_LX, 256), s(OFF_LG, 256), s(OFF_Q, 384), s(OFF_K, 384), s(OFF_V, 384),
        s(OFF_SM + FOX_LANE0, 6)], axis=-1)


def _pick(n, cands):
    for c in cands:
        if n % c == 0:
            return c
    return n


def mm(a, b, *, name, ta=False, tb=False, out_dtypes=(F32,), extras=(), epilogue=None, tm=None, tn=None, tk=None):
    M = a.shape[1] if ta else a.shape[0]
    K = a.shape[0] if ta else a.shape[1]
    N = b.shape[0] if tb else b.shape[1]
    tm = tm or _pick(M, (1024, 1408, 512, 256, 128))
    tn = tn or _pick(N, (1024, 1408, 768, 512, 256, 128))
    tk = tk or _pick(K, (1024, 1408, 512, 256, 128))
    nm, nn, nk = M // tm, N // tn, K // tk
    n_ex, n_out = len(extras), len(out_dtypes)
    a_bytes, b_bytes = M * K * a.dtype.itemsize, K * N * b.dtype.itemsize
    rows_inner = a_bytes * nn + b_bytes <= a_bytes + b_bytes * nm

    def ij(g0, g1):
        return (g1, g0) if rows_inner else (g0, g1)

    def body(a_ref, b_ref, *rest):
        ex_refs, out_refs = rest[:n_ex], rest[n_ex:n_ex + n_out]
        dn = (((0 if ta else 1,), (1 if tb else 0,)), ((), ()))
        part = lax.dot_general(a_ref[...].astype(BF16), b_ref[...].astype(BF16), dn, preferred_element_type=F32)

        def finish(r):
            outs = epilogue(r, *[e[...] for e in ex_refs]) if epilogue is not None else (r,)
            for o_ref, o in zip(out_refs, outs):
                o_ref[...] = o.astype(o_ref.dtype)

        if nk == 1:
            finish(part)
            return
        acc = rest[-1]
        k = pl.program_id(2)

        @pl.when(k == 0)
        def _():
            acc[...] = part

        @pl.when(k > 0)
        def _():
            acc[...] += part

        @pl.when(k == nk - 1)
        def _():
            finish(acc[...])

    def a_map(g0, g1, k):
        i, _ = ij(g0, g1)
        return (k, i) if ta else (i, k)

    def b_map(g0, g1, k):
        _, j = ij(g0, g1)
        return (j, k) if tb else (k, j)

    a_spec = pl.BlockSpec((tk, tm) if ta else (tm, tk), a_map)
    b_spec = pl.BlockSpec((tn, tk) if tb else (tk, tn), b_map)
    mn_spec = pl.BlockSpec((tm, tn), lambda g0, g1, k: ij(g0, g1))
    outs = _pc(body, name=name, grid=(nn, nm, nk) if rows_inner else (nm, nn, nk),
               in_specs=[a_spec, b_spec] + [mn_spec] * n_ex,
               out_specs=[mn_spec] * n_out,
               out_shape=[SDS((M, N), dt) for dt in out_dtypes],
               scratch=[pltpu.VMEM((tm, tn), F32)] if nk > 1 else [],
               sem=("parallel", "parallel", "arbitrary"))(a, b, *extras)
    return outs[0] if n_out == 1 else outs


def rowwise(name, fn, rows, params, row_outs, acc_outs=(), tr=512):
    rows = [r if isinstance(r, tuple) else (r, 0, r.shape[1]) for r in rows]
    T = rows[0][0].shape[0]
    tr = min(tr, T)
    n_in, n_ro, n_ac = len(rows) + len(params), len(row_outs), len(acc_outs)

    def body(*refs):
        ins, outs = refs[:n_in], refs[n_in:]
        res = fn(*[r[...] for r in ins])
        if not isinstance(res, (tuple, list)):
            res = (res,)
        for k in range(n_ro):
            outs[k][...] = res[k].astype(outs[k].dtype)
        if n_ac:
            i = pl.program_id(0)

            @pl.when(i == 0)
            def _():
                for k in range(n_ac):
                    outs[n_ro + k][...] = res[n_ro + k]

            @pl.when(i > 0)
            def _():
                for k in range(n_ac):
                    outs[n_ro + k][...] += res[n_ro + k]

    in_specs = ([pl.BlockSpec((tr, w), functools.partial(lambda cb, i: (i, cb), cb)) for (_, cb, w) in rows]
                + [pl.BlockSpec(p.shape, lambda i: (0, 0)) for p in params])
    out_specs = ([pl.BlockSpec((tr, c), lambda i: (i, 0)) for (c, _) in row_outs]
                 + [pl.BlockSpec((1, c), lambda i: (0, 0)) for c in acc_outs])
    out_shape = [SDS((T, c), dt) for (c, dt) in row_outs] + [SDS((1, c), F32) for c in acc_outs]
    outs = _pc(body, name=name, grid=(T // tr,), in_specs=in_specs, out_specs=out_specs, out_shape=out_shape,
               sem=("arbitrary",) if n_ac else ("parallel",))(*[r[0] for r in rows], *params)
    return outs[0] if len(outs) == 1 else outs


def _rms(x, g):
    return x * lax.rsqrt(jnp.mean(x * x, axis=-1, keepdims=True) + EPS) * g


def _softplus(x):
    return jnp.maximum(x, 0.0) + jnp.log(1.0 + jnp.exp(-jnp.abs(x)))


def _silu(x):
    return x * jax.nn.sigmoid(x)


def _gelu(x):
    return 0.5 * x * (1.0 + jnp.tanh(math.sqrt(2.0 / math.pi) * (x + 0.044715 * (x * x * x))))


def _neg_expm1(x):
    series = x * (1 + x / 2 * (1 + x / 3 * (1 + x / 4 * (1 + x / 5 * (1 + x / 6 * (1 + x / 7))))))
    return -jnp.where(jnp.abs(x) < 0.3, series, jnp.exp(x) - 1.0)


def _swiglu(gu):
    return _silu(gu[:, :D_FF]) * gu[:, D_FF:]


def _ple(pg, pp, b):
    return jax.nn.sigmoid(pg + b) * pp


def _ssd_elt(small, xs_act, dtbias_row, alog_row):
    lane = _iota(small.shape, 1)
    dt = _softplus(small + dtbias_row)
    adt = jnp.where(lane < N_HEADS, -jnp.exp(alog_row) * dt, 0.0)
    head = _iota(xs_act.shape, 1) // HEAD_DIM
    dt_exp = jnp.zeros_like(xs_act)
    for h in range(N_HEADS):
        dth = jnp.sum(jnp.where(lane == h, dt, 0.0), axis=1, keepdims=True)
        dt_exp = dt_exp + jnp.where(head == h, dth, 0.0)
    return adt, xs_act * dt_exp


def _fox_elt(small, bf_row):
    lane = _iota(small.shape, 1)
    keep = (lane >= FOX_LANE0) & (lane < FOX_LANE0 + N_HEADS)
    return jnp.where(keep, -_softplus(-(small + bf_row)), 0.0)


def _lru_elt(xl, pre, b_ax, lam):
    r = jax.nn.sigmoid(pre[:, :LRU_W] + b_ax[:, :LRU_W])
    i = jax.nn.sigmoid(pre[:, LRU_W:] + b_ax[:, LRU_W:])
    log_a = -LRU_C * r * _softplus(-lam)
    a = jnp.exp(log_a)
    mult = jnp.sqrt(_neg_expm1(2.0 * log_a))
    return a, mult * (i * xl)


def _mix_post(yraw, xs_act, z, hl, lgate, yfox, dexp, g_ssd, g_lru, g_fox):
    y_ssd = _rms((yraw + xs_act * dexp) * _silu(z), g_ssd)
    y_lru = _rms(hl * _gelu(lgate), g_lru)
    y_fox = _rms(yfox, g_fox)
    return jnp.concatenate([y_ssd, y_lru, y_fox], axis=-1)


def _colsum(x):
    return jnp.sum(x, axis=0, keepdims=True)


def _shift_down(x, d):
    if d == 0:
        return x
    return jnp.where(_iota(x.shape, 0) >= d, pltpu.roll(x, d, 0), 0.0)


def _shift_up(x, d):
    if d == 0:
        return x
    s = x.shape[0]
    return jnp.where(_iota(x.shape, 0) < s - d, pltpu.roll(x, s - d, 0), 0.0)


def _conv_core(x, w, b):
    y = b + w[3:4, :] * x
    for k in range(3):
        y = y + w[k:k + 1, :] * _shift_down(x, 3 - k)
    return y


def seq_conv(name, src, col, width, w8, b, *, batch, silu, out_dtype):
    T = src.shape[0]
    S = T // batch
    c0 = col // LANE

    def body(x_ref, w_ref, b_ref, o_ref):
        y = _conv_core(x_ref[...], w_ref[...], b_ref[...])
        o_ref[...] = (_silu(y) if silu else y).astype(o_ref.dtype)

    return _pc(body, name=name, grid=(batch, width // LANE),
               in_specs=[pl.BlockSpec((S, LANE), lambda bi, ci: (bi, c0 + ci)),
                         pl.BlockSpec((8, LANE), lambda bi, ci: (0, ci)),
                         pl.BlockSpec((1, LANE), lambda bi, ci: (0, ci))],
               out_specs=pl.BlockSpec((S, LANE), lambda bi, ci: (bi, ci)),
               out_shape=SDS((T, width), out_dtype), sem=("parallel", "parallel"))(src, w8, b)


def seq_conv_bwd(name, src, col, width, w8, b, dy, *, batch, silu):
    T = src.shape[0]
    S = T // batch
    c0 = col // LANE

    def body(x_ref, w_ref, b_ref, dy_ref, dx_ref, dw_ref, db_ref):
        x, w = x_ref[...], w_ref[...]
        dpre = dy_ref[...].astype(F32)
        if silu:
            dpre = jax.vjp(_silu, _conv_core(x, w, b_ref[...]))[1](dpre)[0]
        dx = w[3:4, :] * dpre
        for k in range(3):
            dx = dx + w[k:k + 1, :] * _shift_up(dpre, 3 - k)
        dx_ref[...] = dx.astype(dx_ref.dtype)
        row8 = _iota((8, LANE), 0)
        dw = jnp.zeros((8, LANE), F32)
        for k in range(4):
            dw = dw + jnp.where(row8 == k, _colsum(dpre * _shift_down(x, 3 - k)), 0.0)
        db = _colsum(dpre)
        bi = pl.program_id(1)

        @pl.when(bi == 0)
        def _():
            dw_ref[...] = dw
            db_ref[...] = db

        @pl.when(bi > 0)
        def _():
            dw_ref[...] += dw
            db_ref[...] += db

    return _pc(body, name=name, grid=(width // LANE, batch),
               in_specs=[pl.BlockSpec((S, LANE), lambda ci, bi: (bi, c0 + ci)),
                         pl.BlockSpec((8, LANE), lambda ci, bi: (0, ci)),
                         pl.BlockSpec((1, LANE), lambda ci, bi: (0, ci)),
                         pl.BlockSpec((S, LANE), lambda ci, bi: (bi, ci))],
               out_specs=[pl.BlockSpec((S, LANE), lambda ci, bi: (bi, ci)),
                          pl.BlockSpec((8, LANE), lambda ci, bi: (0, ci)),
                          pl.BlockSpec((1, LANE), lambda ci, bi: (0, ci))],
               out_shape=[SDS((T, width), BF16), SDS((8, width), F32), SDS((1, width), F32)],
               sem=("parallel", "arbitrary"))(src, w8, b, dy)


def _split3_dot(tri, x):
    hi = x.astype(BF16)
    r1 = x - hi.astype(F32)
    mid = r1.astype(BF16)
    lo = (r1 - mid.astype(F32)).astype(BF16)
    d = lambda v: jnp.dot(tri, v, preferred_element_type=F32)
    return d(hi) + d(mid) + d(lo)


def seq_cumsum(name, x, *, batch, reverse=False, nsum=1, trow=None):
    T = x.shape[0]
    S = T // batch
    ch = min(256, S)
    nch = S // ch

    def body(x_ref, o_ref, *maybe_t):
        r, c = _iota((ch, ch), 0), _iota((ch, ch), 1)
        tri = jnp.where((c >= r) if reverse else (c <= r), 1.0, 0.0).astype(BF16)
        carry = jnp.zeros((1, LANE), F32)
        for k in (range(nch - 1, -1, -1) if reverse else range(nch)):
            xc = x_ref[k * ch:(k + 1) * ch, 0:LANE]
            for m in range(1, nsum):
                xc = xc + x_ref[k * ch:(k + 1) * ch, m * LANE:(m + 1) * LANE]
            o_ref[k * ch:(k + 1) * ch, :] = _split3_dot(tri, xc) + carry
            carry = carry + _colsum(xc)
        if trow is not None:
            maybe_t[0][...] = o_ref[...].T[trow:trow + 8, :]

    out_specs = [pl.BlockSpec((S, LANE), lambda bi: (bi, 0))]
    out_shape = [SDS((T, LANE), F32)]
    if trow is not None:
        out_specs.append(pl.BlockSpec((8, S), lambda bi: (bi, 0)))
        out_shape.append(SDS((batch * 8, S), F32))
    outs = _pc(body, name=name, grid=(batch,), in_specs=[pl.BlockSpec((S, LANE * nsum), lambda bi: (bi, 0))],
               out_specs=out_specs, out_shape=out_shape, sem=("parallel",))(x)
    return outs if trow is not None else outs[0]


def lru_scan(name, a, u, *, batch):
    T, W = a.shape
    S = T // batch

    def body(a_ref, u_ref, h_ref):
        row = _iota((8, W), 0)

        def step(g, h):
            off = pl.multiple_of(g * 8, 8)
            at, ut = a_ref[pl.ds(off, 8), :], u_ref[pl.ds(off, 8), :]
            acc = jnp.zeros((8, W), F32)
            for r in range(8):
                h = at[r:r + 1, :] * h + ut[r:r + 1, :]
                acc = jnp.where(row == r, jnp.broadcast_to(h, (8, W)), acc)
            h_ref[pl.ds(off, 8), :] = acc
            return h

        lax.fori_loop(0, S // 8, step, jnp.zeros((1, W), F32))

    spec = pl.BlockSpec((S, W), lambda bi: (bi, 0))
    return _pc(body, name=name, grid=(batch,), in_specs=[spec, spec], out_specs=spec,
               out_shape=SDS((T, W), F32), sem=("parallel",))(a, u)


def lru_scan_bwd(name, a, h, dh, *, batch):
    T, W = a.shape
    S = T // batch
    ng = S // 8

    def body(a_ref, h_ref, dh_ref, da_ref, du_ref):
        row = _iota((8, W), 0)

        def step(k, c):
            g_idx = ng - 1 - k
            off = pl.multiple_of(g_idx * 8, 8)
            offp = pl.multiple_of(jnp.maximum(g_idx - 1, 0) * 8, 8)
            at, ht, dt = a_ref[pl.ds(off, 8), :], h_ref[pl.ds(off, 8), :], dh_ref[pl.ds(off, 8), :]
            hp = jnp.where(g_idx > 0, h_ref[pl.ds(offp, 8), :], 0.0)
            da = jnp.zeros((8, W), F32)
            du = jnp.zeros((8, W), F32)
            for r in range(7, -1, -1):
                g = dt[r:r + 1, :] + c
                hprev = ht[r - 1:r, :] if r > 0 else hp[7:8, :]
                du = jnp.where(row == r, jnp.broadcast_to(g, (8, W)), du)
                da = jnp.where(row == r, jnp.broadcast_to(g * hprev, (8, W)), da)
                c = at[r:r + 1, :] * g
            da_ref[pl.ds(off, 8), :] = da
            du_ref[pl.ds(off, 8), :] = du
            return c

        lax.fori_loop(0, ng, step, jnp.zeros((1, W), F32))

    spec = pl.BlockSpec((S, W), lambda bi: (bi, 0))
    return _pc(body, name=name, grid=(batch,), in_specs=[spec] * 3, out_specs=[spec] * 2,
               out_shape=[SDS((T, W), F32)] * 2, sem=("parallel",))(a, h, dh)


def _nt(a, b):
    return lax.dot_general(a, b, (((1,), (1,)), ((), ())), preferred_element_type=F32)


def _tn(a, b):
    return lax.dot_general(a, b, (((0,), (0,)), ((), ())), preferred_element_type=F32)


def _tile(S):
    return min(256, S)


def ssd_attn_fwd(name, cm, bm, xd, cum, cum_t, *, batch):
    T = cm.shape[0]
    S = T // batch
    tq = tk = _tile(S)
    nq = S // tq

    def body(c_ref, b_ref, x_ref, cum_ref, cumt_ref, y_ref):
        i = pl.program_id(1)
        cq, cmq = cum_ref[...], c_ref[...]
        rowi, coli = _iota((tq, tk), 0), _iota((tq, tk), 1)
        half = _iota((tk, LANE), 1) // HEAD_DIM

        def step(j, accs, diag):
            off = pl.multiple_of(j * tk, tk)
            bj = b_ref[pl.ds(off, tk), :]
            gm = [_nt(cmq[:, g * LANE:(g + 1) * LANE], bj[:, g * LANE:(g + 1) * LANE]) for g in range(2)]
            ckt = cumt_ref[:, pl.ds(off, tk)]
            new = []
            for p in range(3):
                xp = x_ref[pl.ds(off, tk), p * LANE:(p + 1) * LANE]
                ws, xs = [], []
                for hh in range(2):
                    h = 2 * p + hh
                    seg = cq[:, h:h + 1] - ckt[h:h + 1, :]
                    e = jnp.exp(jnp.where(rowi >= coli, seg, -jnp.inf) if diag else seg)
                    ws.append((gm[h // 3] * e).astype(BF16))
                    xs.append(jnp.where(half == hh, xp, jnp.zeros_like(xp)))
                new.append(accs[p] + jnp.dot(jnp.concatenate(ws, axis=1), jnp.concatenate(xs, axis=0),
                                             preferred_element_type=F32))
            return tuple(new)

        accs = lax.fori_loop(0, i, functools.partial(step, diag=False),
                             tuple(jnp.zeros((tq, LANE), F32) for _ in range(3)))
        accs = step(i, accs, True)
        y_ref[...] = jnp.concatenate(accs, axis=1)

    return _pc(body, name=name, grid=(batch, nq),
               in_specs=[pl.BlockSpec((tq, 256), lambda b, i: (b * nq + i, 0)),
                         pl.BlockSpec((S, 256), lambda b, i: (b, 0)),
                         pl.BlockSpec((S, SSD_W), lambda b, i: (b, 0)),
                         pl.BlockSpec((tq, LANE), lambda b, i: (b * nq + i, 0)),
                         pl.BlockSpec((8, S), lambda b, i: (b, 0))],
               out_specs=pl.BlockSpec((tq, SSD_W), lambda b, i: (b * nq + i, 0)),
               out_shape=SDS((T, SSD_W), F32), sem=("parallel", "parallel"))(cm, bm, xd, cum, cum_t)


def ssd_attn_bwd(name, cm, bm, xd, cum, cum_t, dy, *, batch):
    T = cm.shape[0]
    S = T // batch
    tq = tk = _tile(S)
    nq = S // tq

    def body(c_ref, b_ref, x_ref, cum_ref, cumt_ref, dy_ref, dx_ref, db_ref, dc_ref, dcum_ref, dcumt_ref):
        dx_ref[...] = jnp.zeros_like(dx_ref)
        db_ref[...] = jnp.zeros_like(db_ref)
        dcum_ref[...] = jnp.zeros_like(dcum_ref)
        dcumt_ref[...] = jnp.zeros_like(dcumt_ref)
        rowi, coli = _iota((tq, tk), 0), _iota((tq, tk), 1)
        halfq = _iota((tq, LANE), 1) // HEAD_DIM
        lane_q = _iota((tq, LANE), 1)

        def qblock(i, _):
            qoff = pl.multiple_of(i * tq, tq)
            cq = cum_ref[pl.ds(qoff, tq), :]
            cmq = c_ref[pl.ds(qoff, tq), :]
            dyq = dy_ref[pl.ds(qoff, tq), :]
            dyh = [[jnp.where(halfq == hh, dyq[:, p * LANE:(p + 1) * LANE], 0.0).astype(BF16) for hh in range(2)]
                   for p in range(3)]

            def step(j, carry, diag):
                dcq, rs_acc = carry
                off = pl.multiple_of(j * tk, tk)
                bj = b_ref[pl.ds(off, tk), :]
                gm = [_nt(cmq[:, g * LANE:(g + 1) * LANE], bj[:, g * LANE:(g + 1) * LANE]) for g in range(2)]
                ckt = cumt_ref[:, pl.ds(off, tk)]
                dgm = [jnp.zeros((tq, tk), F32), jnp.zeros((tq, tk), F32)]
                for p in range(3):
                    xp = x_ref[pl.ds(off, tk), p * LANE:(p + 1) * LANE]
                    ws = []
                    for hh in range(2):
                        h = 2 * p + hh
                        seg = cq[:, h:h + 1] - ckt[h:h + 1, :]
                        e = jnp.exp(jnp.where(rowi >= coli, seg, -jnp.inf) if diag else seg)
                        w = gm[h // 3] * e
                        dw = _nt(dyh[p][hh], xp)
                        zz = dw * w
                        rs_acc = rs_acc + jnp.where(lane_q == h, jnp.sum(zz, axis=1, keepdims=True), 0.0)
                        dcumt_ref[h:h + 1, pl.ds(off, tk)] += _colsum(zz)
                        dgm[h // 3] = dgm[h // 3] + dw * e
                        ws.append(w.astype(BF16))
                    dx_ref[pl.ds(off, tk), p * LANE:(p + 1) * LANE] += _tn(
                        jnp.concatenate(ws, axis=0), jnp.concatenate(dyh[p], axis=0))
                new_dcq = []
                for g in range(2):
                    dg = dgm[g].astype(BF16)
                    new_dcq.append(dcq[g] + jnp.dot(dg, bj[:, g * LANE:(g + 1) * LANE], preferred_element_type=F32))
                    db_ref[pl.ds(off, tk), g * LANE:(g + 1) * LANE] += _tn(dg, cmq[:, g * LANE:(g + 1) * LANE])
                return tuple(new_dcq), rs_acc

            carry = lax.fori_loop(
                0, i, functools.partial(step, diag=False),
                ((jnp.zeros((tq, LANE), F32), jnp.zeros((tq, LANE), F32)), jnp.zeros((tq, LANE), F32)))
            dcq, rs_acc = step(i, carry, True)
            dc_ref[pl.ds(qoff, tq), :] = jnp.concatenate(dcq, axis=1)
            dcum_ref[pl.ds(qoff, tq), :] += rs_acc
            return 0

        lax.fori_loop(0, nq, qblock, 0)
        dcum_ref[...] = dcum_ref[...] - dcumt_ref[...].T

    s256 = pl.BlockSpec((S, 256), lambda b: (b, 0))
    s384 = pl.BlockSpec((S, SSD_W), lambda b: (b, 0))
    s128 = pl.BlockSpec((S, LANE), lambda b: (b, 0))
    return _pc(body, name=name, grid=(batch,),
               in_specs=[s256, s256, s384, s128, pl.BlockSpec((8, S), lambda b: (b, 0)), s384],
               out_specs=[s384, s256, s256, s128],
               out_shape=[SDS((T, SSD_W), F32), SDS((T, 256), F32), SDS((T, 256), F32), SDS((T, LANE), F32)],
               scratch=[pltpu.VMEM((LANE, S), F32)], sem=("parallel",))(cm, bm, xd, cum, cum_t, dy)


NEG_BIG = -1e30


def fox_attn_fwd(name, proj, cum, cum_t, *, batch):
    T = proj.shape[0]
    S = T // batch
    tq = tk = _tile(S)
    nq = S // tq
    scale = HEAD_DIM ** -0.5
    qb, kb, vb = OFF_Q // LANE, OFF_K // LANE, OFF_V // LANE

    def body(q_ref, k_ref, v_ref, cum_ref, cumt_ref, o_ref, lse_ref):
        p, i = pl.program_id(1), pl.program_id(2)
        cq = cum_ref[...]
        lane_q = _iota((tq, LANE), 1)
        halfq, halfk = lane_q // HEAD_DIM, _iota((tk, LANE), 1) // HEAD_DIM
        qs = q_ref[...] * scale
        qh = [jnp.where(halfq == hh, qs, 0.0).astype(BF16) for hh in range(2)]
        rowi, coli = _iota((tq, tk), 0), _iota((tq, tk), 1)
        cqh = [jnp.sum(jnp.where(lane_q == FOX_LANE0 + 2 * p + hh, cq, 0.0), axis=1, keepdims=True) for hh in range(2)]
        row8 = _iota((8, tk), 0)

        def step(j, carry, diag):
            ms, ls, acc = carry
            off = pl.multiple_of(j * tk, tk)
            kj = k_ref[pl.ds(off, tk), :].astype(BF16)
            vj = v_ref[pl.ds(off, tk), :].astype(BF16)
            ckt = cumt_ref[:, pl.ds(off, tk)]
            ps, vs, new_m, new_l, alphas = [], [], [], [], []
            for hh in range(2):
                ck = jnp.sum(jnp.where(row8 == 2 * p + hh, ckt, 0.0), axis=0, keepdims=True)
                logits = _nt(qh[hh], kj) + (cqh[hh] - ck)
                if diag:
                    logits = jnp.where(rowi >= coli, logits, -jnp.inf)
                m = jnp.maximum(ms[hh], jnp.max(logits, axis=1, keepdims=True))
                alpha = jnp.exp(ms[hh] - m)
                pr = jnp.exp(logits - m)
                new_m.append(m)
                new_l.append(alpha * ls[hh] + jnp.sum(pr, axis=1, keepdims=True))
                alphas.append(alpha)
                ps.append(pr.astype(BF16))
                vs.append(jnp.where(halfk == hh, vj, jnp.zeros_like(vj)))
            acc = acc * jnp.where(halfq == 0, alphas[0], alphas[1]) + jnp.dot(
                jnp.concatenate(ps, axis=1), jnp.concatenate(vs, axis=0), preferred_element_type=F32)
            return tuple(new_m), tuple(new_l), acc

        init = ((jnp.full((tq, 1), NEG_BIG, F32),) * 2, (jnp.zeros((tq, 1), F32),) * 2, jnp.zeros((tq, LANE), F32))
        ms, ls, acc = step(i, lax.fori_loop(0, i, functools.partial(step, diag=False), init), True)
        o_ref[...] = acc / jnp.where(halfq == 0, ls[0], ls[1])
        lse_ref[...] = (jnp.where(lane_q == 0, ms[0] + jnp.log(ls[0]), 0.0)
                        + jnp.where(lane_q == 1, ms[1] + jnp.log(ls[1]), 0.0))

    return _pc(body, name=name, grid=(batch, 3, nq),
               in_specs=[pl.BlockSpec((tq, LANE), lambda b, p, i: (b * nq + i, qb + p)),
                         pl.BlockSpec((S, LANE), lambda b, p, i: (b, kb + p)),
                         pl.BlockSpec((S, LANE), lambda b, p, i: (b, vb + p)),
                         pl.BlockSpec((tq, LANE), lambda b, p, i: (b * nq + i, 0)),
                         pl.BlockSpec((8, S), lambda b, p, i: (b, 0))],
               out_specs=[pl.BlockSpec((tq, LANE), lambda b, p, i: (b * nq + i, p))] * 2,
               out_shape=[SDS((T, FOX_W), F32)] * 2, sem=("parallel", "parallel", "parallel"))(proj, proj, proj, cum, cum_t)


def fox_attn_bwd(name, proj, o, do, lse, cum, cum_t, *, batch):
    T = proj.shape[0]
    S = T // batch
    tq = tk = _tile(S)
    nq = S // tq
    scale = HEAD_DIM ** -0.5
    qb, kb, vb = OFF_Q // LANE, OFF_K // LANE, OFF_V // LANE

    def body(q_ref, k_ref, v_ref, o_ref, do_ref, lse_ref, cum_ref, cumt_ref,
             dq_ref, dk_ref, dv_ref, dcum_ref, dk_acc, dv_acc, dcumt_ref):
        p = pl.program_id(1)
        dk_acc[...] = jnp.zeros_like(dk_acc)
        dv_acc[...] = jnp.zeros_like(dv_acc)
        dcum_ref[...] = jnp.zeros_like(dcum_ref)
        dcumt_ref[...] = jnp.zeros_like(dcumt_ref)
        lane_q = _iota((tq, LANE), 1)
        halfq, halfk = lane_q // HEAD_DIM, _iota((tk, LANE), 1) // HEAD_DIM
        rowi, coli = _iota((tq, tk), 0), _iota((tq, tk), 1)
        row8 = _iota((8, tk), 0)

        def qblock(i, _):
            qoff = pl.multiple_of(i * tq, tq)
            cq = cum_ref[pl.ds(qoff, tq), :]
            qs = q_ref[pl.ds(qoff, tq), :] * scale
            doq = do_ref[pl.ds(qoff, tq), :]
            lse = lse_ref[pl.ds(qoff, tq), :]
            delta = doq * o_ref[pl.ds(qoff, tq), :]
            qh, doh, cqh, lseh, dlt = [], [], [], [], []
            for hh in range(2):
                qh.append(jnp.where(halfq == hh, qs, 0.0).astype(BF16))
                doh.append(jnp.where(halfq == hh, doq, 0.0).astype(BF16))
                cqh.append(jnp.sum(jnp.where(lane_q == FOX_LANE0 + 2 * p + hh, cq, 0.0), axis=1, keepdims=True))
                lseh.append(jnp.sum(jnp.where(lane_q == hh, lse, 0.0), axis=1, keepdims=True))
                dlt.append(jnp.sum(jnp.where(halfq == hh, delta, 0.0), axis=1, keepdims=True))

            def step(j, carry, diag):
                dq, rs = carry
                off = pl.multiple_of(j * tk, tk)
                kj = k_ref[pl.ds(off, tk), :].astype(BF16)
                vj = v_ref[pl.ds(off, tk), :].astype(BF16)
                ckt = cumt_ref[:, pl.ds(off, tk)]
                dss, prs, ks = [], [], []
                for hh in range(2):
                    ck = jnp.sum(jnp.where(row8 == 2 * p + hh, ckt, 0.0), axis=0, keepdims=True)
                    logits = _nt(qh[hh], kj) + ((cqh[hh] - lseh[hh]) - ck)
                    if diag:
                        logits = jnp.where(rowi >= coli, logits, -jnp.inf)
                    pr = jnp.exp(logits)
                    ds = pr * (_nt(doh[hh], vj) - dlt[hh])
                    rs = rs + jnp.where(lane_q == FOX_LANE0 + 2 * p + hh, jnp.sum(ds, axis=1, keepdims=True), 0.0)
                    cs = _colsum(ds)
                    dcumt_ref[0:8, pl.ds(off, tk)] += jnp.where(row8 == 2 * p + hh, cs, 0.0)
                    dss.append(ds.astype(BF16))
                    prs.append(pr.astype(BF16))
                    ks.append(jnp.where(halfk == hh, kj, jnp.zeros_like(kj)))
                dq = dq + jnp.dot(jnp.concatenate(dss, axis=1), jnp.concatenate(ks, axis=0), preferred_element_type=F32)
                dk_acc[pl.ds(off, tk), :] += _tn(jnp.concatenate(dss, axis=0), jnp.concatenate(qh, axis=0))
                dv_acc[pl.ds(off, tk), :] += _tn(jnp.concatenate(prs, axis=0), jnp.concatenate(doh, axis=0))
                return dq, rs

            carry = lax.fori_loop(0, i, functools.partial(step, diag=False),
                                  (jnp.zeros((tq, LANE), F32), jnp.zeros((tq, LANE), F32)))
            dq, rs = step(i, carry, True)
            dq_ref[pl.ds(qoff, tq), :] = (dq * scale).astype(dq_ref.dtype)
            dcum_ref[pl.ds(qoff, tq), :] += rs
            return 0

        lax.fori_loop(0, nq, qblock, 0)
        dk_ref[...] = dk_acc[...].astype(dk_ref.dtype)
        dv_ref[...] = dv_acc[...].astype(dv_ref.dtype)
        dct = dcumt_ref[...].T
        dcum_ref[...] = dcum_ref[...] - pltpu.roll(dct, FOX_LANE0, 1)

    sp = lambda c0: pl.BlockSpec((S, LANE), lambda b, p: (b, c0 + p))
    s0 = pl.BlockSpec((S, LANE), lambda b, p: (b, 0))
    return _pc(body, name=name, grid=(batch, 3),
               in_specs=[sp(qb), sp(kb), sp(vb), sp(0), sp(0), sp(0), s0, pl.BlockSpec((8, S), lambda b, p: (b, 0))],
               out_specs=[sp(0)] * 4,
               out_shape=[SDS((T, FOX_W), BF16)] * 3 + [SDS((T, FOX_W), F32)],
               scratch=[pltpu.VMEM((S, LANE), F32), pltpu.VMEM((S, LANE), F32), pltpu.VMEM((LANE, S), F32)],
               sem=("parallel", "parallel"))(proj, proj, proj, o, do, lse, cum, cum_t)


def _row(v, width=None, at=0):
    v = v.astype(F32)
    width = width or v.shape[0]
    return jnp.pad(v, (at, width - at - v.shape[0]))[None, :]


def _pad8(w4):
    return jnp.pad(w4.astype(F32), ((0, 4), (0, 0)))


def _block_diag(w):
    out = jnp.zeros((LRU_W, LRU_W), w.dtype)
    for g in range(4):
        out = lax.dynamic_update_slice(out, w[g], (g * 64, g * 64))
    return out


def prep_layer(f):
    cw, cb = f["ssd_conv_w"], f["ssd_conv_b"]
    return dict(
        win=permute_in_cols(f["w_in"]), wout=f["w_out"], wgu=jnp.concatenate([f["w_gate"], f["w_up"]], axis=1),
        wd=f["w_down"], wpg=f["w_ple_gate"], wpp=f["w_ple_proj"],
        wax=jnp.concatenate([_block_diag(f["lru_w_a"]), _block_diag(f["lru_w_x"])], axis=1),
        g1=_row(f["norm1_g"]), g2=_row(f["norm2_g"]), g3=_row(f["norm3_g"]),
        cw_xs=_pad8(cw[:, :384]), cb_xs=_row(cb[:384]), cw_b=_pad8(cw[:, 384:640]), cb_b=_row(cb[384:640]),
        cw_c=_pad8(cw[:, 640:]), cb_c=_row(cb[640:]), cw_l=_pad8(f["lru_conv_w"]), cb_l=_row(f["lru_conv_b"]),
        dtbias_row=_row(f["ssd_dt_bias"], LANE), alog_row=_row(f["ssd_a_log"], LANE),
        dexp=jnp.repeat(f["ssd_d"].astype(F32), HEAD_DIM)[None, :], g_ssd=_row(f["ssd_norm_g"]),
        b_ax=_row(jnp.concatenate([f["lru_b_a"], f["lru_b_x"]])), lam=_row(f["lru_lambda"]), g_lru=_row(f["lru_norm_g"]),
        bf_row=_row(f["fox_b_f"], LANE, FOX_LANE0), g_fox=_row(f["fox_norm_g"]), b_pg=_row(f["b_ple_gate"]))


def unprep_grads(g):
    blocks = lambda m: jnp.stack([m[i * 64:(i + 1) * 64, i * 64:(i + 1) * 64] for i in range(4)])
    return dict(
        norm1_g=g["g1"][0], w_in=unpermute_in_cols(g["win"]),
        ssd_conv_w=jnp.concatenate([g["cw_xs"][:4], g["cw_b"][:4], g["cw_c"][:4]], axis=1),
        ssd_conv_b=jnp.concatenate([g["cb_xs"][0], g["cb_b"][0], g["cb_c"][0]]),
        ssd_dt_bias=g["dtbias_row"][0, :N_HEADS], ssd_a_log=g["alog_row"][0, :N_HEADS],
        ssd_d=jnp.sum(g["dexp"].reshape(N_HEADS, HEAD_DIM), axis=1), ssd_norm_g=g["g_ssd"][0],
        lru_conv_w=g["cw_l"][:4], lru_conv_b=g["cb_l"][0],
        lru_w_a=blocks(g["wax"][:, :LRU_W]), lru_b_a=g["b_ax"][0, :LRU_W],
        lru_w_x=blocks(g["wax"][:, LRU_W:]), lru_b_x=g["b_ax"][0, LRU_W:],
        lru_lambda=g["lam"][0], lru_norm_g=g["g_lru"][0],
        fox_b_f=g["bf_row"][0, FOX_LANE0:FOX_LANE0 + N_HEADS], fox_norm_g=g["g_fox"][0],
        w_out=g["wout"], norm2_g=g["g2"][0], w_gate=g["wgu"][:, :D_FF], w_up=g["wgu"][:, D_FF:], w_down=g["wd"],
        norm3_g=g["g3"][0], w_ple_gate=g["wpg"], b_ple_gate=g["b_pg"][0], w_ple_proj=g["wpp"])


def _view(a, off, width):
    return (a, off // width, width)


def _add_epilogue(acc, e):
    return (acc + e,)


def mixer_fwd(proj, w, batch, tag):
    sm = _view(proj, OFF_SM, LANE)
    conv = functools.partial(seq_conv, batch=batch)
    cmc = conv(f"{tag}_conv_c", proj, OFF_C, 256, w["cw_c"], w["cb_c"], silu=True, out_dtype=BF16)
    bmc = conv(f"{tag}_conv_b", proj, OFF_B, 256, w["cw_b"], w["cb_b"], silu=True, out_dtype=BF16)
    xs_act = conv(f"{tag}_conv_xs", proj, OFF_XS, SSD_W, w["cw_xs"], w["cb_xs"], silu=True, out_dtype=F32)
    xl = conv(f"{tag}_conv_l", proj, OFF_LX, LRU_W, w["cw_l"], w["cb_l"], silu=False, out_dtype=F32)
    adt, xd = rowwise(f"{tag}_ssd_elt", _ssd_elt, [sm, xs_act], [w["dtbias_row"], w["alog_row"]],
                      [(LANE, F32), (SSD_W, BF16)])
    cum_a, cum_at = seq_cumsum(f"{tag}_cum_a", adt, batch=batch, trow=0)
    yraw = ssd_attn_fwd(f"{tag}_ssd_fwd", cmc, bmc, xd, cum_a, cum_at, batch=batch)
    logf = rowwise(f"{tag}_fox_elt", _fox_elt, [sm], [w["bf_row"]], [(LANE, F32)])
    cum_f, cum_ft = seq_cumsum(f"{tag}_cum_f", logf, batch=batch, trow=FOX_LANE0)
    o, lse = fox_attn_fwd(f"{tag}_fox_fwd", proj, cum_f, cum_ft, batch=batch)
    pre = mm(xl, w["wax"], name=f"{tag}_mm_lru_gates")
    a, u = rowwise(f"{tag}_lru_elt", _lru_elt, [xl, pre], [w["b_ax"], w["lam"]], [(LRU_W, F32), (LRU_W, F32)])
    hl = lru_scan(f"{tag}_lru_scan", a, u, batch=batch)
    ycat = rowwise(f"{tag}_mix_post", _mix_post,
                   [yraw, xs_act, _view(proj, OFF_Z, SSD_W), hl, _view(proj, OFF_LG, LRU_W), o],
                   [w["dexp"], w["g_ssd"], w["g_lru"], w["g_fox"]], [(D_MODEL, BF16)], tr=256)
    saved = dict(cmc=cmc, bmc=bmc, xs_act=xs_act, xl=xl, xd=xd, cum_a=cum_a, cum_at=cum_at, yraw=yraw,
                 cum_f=cum_f, cum_ft=cum_ft, o=o, lse=lse, pre=pre, a=a, hl=hl)
    return ycat, saved


def mixer_bwd(dycat, proj, w, s, batch, tag):
    sm = _view(proj, OFF_SM, LANE)
    g = {}

    def post_bwd(yraw, xs_act, z, hl, lg, o, dyc, dexp, g_ssd, g_lru, g_fox):
        return jax.vjp(_mix_post, yraw, xs_act, z, hl, lg, o, dexp, g_ssd, g_lru, g_fox)[1](dyc)

    (dyraw, dxs1, dz, dhl, dlg, do, g["dexp"], g["g_ssd"], g["g_lru"], g["g_fox"]) = rowwise(
        f"{tag}_mix_post_bwd", post_bwd,
        [s["yraw"], s["xs_act"], _view(proj, OFF_Z, SSD_W), s["hl"], _view(proj, OFF_LG, LRU_W), s["o"], dycat],
        [w["dexp"], w["g_ssd"], w["g_lru"], w["g_fox"]],
        [(SSD_W, F32), (SSD_W, F32), (SSD_W, BF16), (LRU_W, F32), (LRU_W, BF16), (FOX_W, F32)],
        [SSD_W, SSD_W, LRU_W, FOX_W], tr=256)

    dq, dk, dv, dcum3 = fox_attn_bwd(f"{tag}_fox_bwd", proj, s["o"], do, s["lse"], s["cum_f"], s["cum_ft"], batch=batch)
    dlogf = seq_cumsum(f"{tag}_rcum_f", dcum3, batch=batch, reverse=True, nsum=3)

    dxd, dbm, dcm, dcum_a = ssd_attn_bwd(f"{tag}_ssd_bwd", s["cmc"], s["bmc"], s["xd"], s["cum_a"], s["cum_at"], dyraw,
                                         batch=batch)
    dadt = seq_cumsum(f"{tag}_rcum_a", dcum_a, batch=batch, reverse=True)

    def ssd_elt_bwd(small, xs_act, dadt_, dxd_, dxs1_, dtbias, alog):
        dsm, dxs, ddtb, dalog = jax.vjp(_ssd_elt, small, xs_act, dtbias, alog)[1]((dadt_, dxd_))
        return dsm, dxs + dxs1_, ddtb, dalog

    dsm_s, dxs_act, g["dtbias_row"], g["alog_row"] = rowwise(
        f"{tag}_ssd_elt_bwd", ssd_elt_bwd, [sm, s["xs_act"], dadt, dxd, dxs1], [w["dtbias_row"], w["alog_row"]],
        [(LANE, F32), (SSD_W, F32)], [LANE, LANE])

    def fox_elt_bwd(small, dlogf_, dsm_s_, bf_row):
        dsm, dbf = jax.vjp(_fox_elt, small, bf_row)[1](dlogf_)
        return dsm + dsm_s_, dbf

    dsm, g["bf_row"] = rowwise(f"{tag}_fox_elt_bwd", fox_elt_bwd, [sm, dlogf, dsm_s], [w["bf_row"]],
                               [(LANE, BF16)], [LANE])

    cbwd = functools.partial(seq_conv_bwd, batch=batch)
    dxs_raw, g["cw_xs"], g["cb_xs"] = cbwd(f"{tag}_conv_xs_bwd", proj, OFF_XS, SSD_W, w["cw_xs"], w["cb_xs"], dxs_act, silu=True)
    db_raw, g["cw_b"], g["cb_b"] = cbwd(f"{tag}_conv_b_bwd", proj, OFF_B, 256, w["cw_b"], w["cb_b"], dbm, silu=True)
    dc_raw, g["cw_c"], g["cb_c"] = cbwd(f"{tag}_conv_c_bwd", proj, OFF_C, 256, w["cw_c"], w["cb_c"], dcm, silu=True)

    da, du = lru_scan_bwd(f"{tag}_lru_scan_bwd", s["a"], s["hl"], dhl, batch=batch)

    def lru_elt_bwd(xl, pre, da_, du_, b_ax, lam):
        return jax.vjp(_lru_elt, xl, pre, b_ax, lam)[1]((da_, du_))

    dxl1, dpre, g["b_ax"], g["lam"] = rowwise(
        f"{tag}_lru_elt_bwd", lru_elt_bwd, [s["xl"], s["pre"], da, du], [w["b_ax"], w["lam"]],
        [(LRU_W, F32), (2 * LRU_W, BF16)], [2 * LRU_W, LRU_W])
    g["wax"] = mm(s["xl"], dpre, ta=True, name=f"{tag}_mm_dwax")
    dxl = mm(dpre, w["wax"], tb=True, extras=[dxl1], epilogue=_add_epilogue, name=f"{tag}_mm_dxl")
    dlx_raw, g["cw_l"], g["cb_l"] = cbwd(f"{tag}_conv_l_bwd", proj, OFF_LX, LRU_W, w["cw_l"], w["cb_l"], dxl, silu=False)

    dproj = jnp.concatenate([db_raw, dc_raw, dlx_raw, dlg, dsm, dz, dxs_raw, dq, dk, dv], axis=1)
    return dproj, g


def layer_fwd(h0, p_l, w, batch, tag):
    u1 = rowwise(f"{tag}_rms1", _rms, [h0], [w["g1"]], [(D_MODEL, BF16)])
    proj = mm(u1, w["win"], name=f"{tag}_mm_in")
    ycat, ms = mixer_fwd(proj, w, batch, tag)
    h1 = mm(ycat, w["wout"], extras=[h0], epilogue=_add_epilogue, name=f"{tag}_mm_out")
    u2 = rowwise(f"{tag}_rms2", _rms, [h1], [w["g2"]], [(D_MODEL, BF16)])
    gu = mm(u2, w["wgu"], out_dtypes=(BF16,), name=f"{tag}_mm_gu")
    act = rowwise(f"{tag}_swiglu", lambda gu_: _swiglu(gu_.astype(F32)), [gu], [], [(D_FF, BF16)])
    h2 = mm(act, w["wd"], extras=[h1], epilogue=_add_epilogue, tm=512, tk=D_FF, name=f"{tag}_mm_down")
    u3 = rowwise(f"{tag}_rms3", _rms, [h2], [w["g3"]], [(D_MODEL, BF16)])
    pg = mm(u3, w["wpg"], name=f"{tag}_mm_pg")
    pp = mm(p_l, w["wpp"], name=f"{tag}_mm_pp")
    h3 = rowwise(f"{tag}_ple", lambda pg_, pp_, h_, b: h_ + _ple(pg_, pp_, b), [pg, pp, h2], [w["b_pg"]],
                 [(D_MODEL, F32)])
    saved = dict(h0=h0, u1=u1, proj=proj, ycat=ycat, h1=h1, u2=u2, gu=gu, act=act, h2=h2, u3=u3, pg=pg, pp=pp, mixer=ms)
    return h3, saved


def _rms_bwd(h, du, dres, g):
    dh, dg = jax.vjp(_rms, h, g)[1](du)
    return dh + dres, dg


def layer_bwd(dh3, p_l, w, s, batch, tag):
    def ple_bwd(pg, pp, dh, b):
        return jax.vjp(_ple, pg, pp, b)[1](dh)

    d_pg, d_pp, g_bpg = rowwise(f"{tag}_ple_bwd", ple_bwd, [s["pg"], s["pp"], dh3], [w["b_pg"]],
                                [(D_MODEL, BF16), (D_MODEL, BF16)], [D_MODEL])
    g = dict(b_pg=g_bpg)
    g["wpp"] = mm(p_l, d_pp, ta=True, name=f"{tag}_mm_dwpp")
    g["wpg"] = mm(s["u3"], d_pg, ta=True, name=f"{tag}_mm_dwpg")
    du3 = mm(d_pg, w["wpg"], tb=True, name=f"{tag}_mm_du3")
    dh2, g["g3"] = rowwise(f"{tag}_rms3_bwd", _rms_bwd, [s["h2"], du3, dh3], [w["g3"]], [(D_MODEL, F32)], [D_MODEL])

    dact = mm(dh2, w["wd"], tb=True, out_dtypes=(BF16,), name=f"{tag}_mm_dact")
    g["wd"] = mm(s["act"], dh2, ta=True, name=f"{tag}_mm_dwd")
    dgu = rowwise(f"{tag}_swiglu_bwd", lambda gu, da: jax.vjp(_swiglu, gu.astype(F32))[1](da.astype(F32))[0],
                  [s["gu"], dact], [], [(2 * D_FF, BF16)], tr=256)
    g["wgu"] = mm(s["u2"], dgu, ta=True, name=f"{tag}_mm_dwgu")
    du2 = mm(dgu, w["wgu"], tb=True, name=f"{tag}_mm_du2")
    dh1, g["g2"] = rowwise(f"{tag}_rms2_bwd", _rms_bwd, [s["h1"], du2, dh2], [w["g2"]], [(D_MODEL, F32)], [D_MODEL])

    dycat = mm(dh1, w["wout"], tb=True, name=f"{tag}_mm_dycat")
    g["wout"] = mm(s["ycat"], dh1, ta=True, name=f"{tag}_mm_dwout")
    dproj, gm = mixer_bwd(dycat, s["proj"], w, s["mixer"], batch, tag)
    g.update(gm)
    g["win"] = mm(s["u1"], dproj, ta=True, name=f"{tag}_mm_dwin")
    du1 = mm(dproj, w["win"], tb=True, name=f"{tag}_mm_du1")
    dh0, g["g1"] = rowwise(f"{tag}_rms1_bwd", _rms_bwd, [s["h0"], du1, dh1], [w["g1"]], [(D_MODEL, F32)], [D_MODEL])
    return dh0, g


def _loss_fwd_bwd(h, tgt, gf):
    def f(h_, gf_):
        e = _rms(h_, gf_) - tgt
        return 0.5 * jnp.sum(jnp.mean(e * e, axis=-1, keepdims=True), axis=0, keepdims=True)

    loss, vj = jax.vjp(f, h, gf)
    dh, dgf = vj(jnp.ones((1, 1), F32))
    return dh, jnp.broadcast_to(loss, (1, LANE)), dgf


def local_step(x, p, tgt, layers, final_g):
    batch, S, _ = x.shape
    T = batch * S
    h = x.reshape(T, D_MODEL)
    saved = []
    for l, w in enumerate(layers):
        h, s = layer_fwd(h, p[l].reshape(T, PLE_DIM), w, batch, f"l{l}")
        saved.append(s)
    dh, loss, dgf = rowwise("loss", _loss_fwd_bwd, [h, tgt.reshape(T, D_MODEL)], [_row(final_g)],
                            [(D_MODEL, F32)], [LANE, D_MODEL], tr=256)
    grads = [None] * len(layers)
    for l in reversed(range(len(layers))):
        dh, grads[l] = layer_bwd(dh, p[l].reshape(T, PLE_DIM), layers[l], saved[l], batch, f"l{l}")
    return loss[0, 0], dh.reshape(batch, S, D_MODEL), grads, dgf[0]


MESH = pl.DeviceIdType.MESH
N_DEV = 8
N_CHIP = 4
ANY = pl.BlockSpec(memory_space=pl.ANY)


def _pos():
    return lax.axis_index("x"), lax.axis_index("y"), lax.axis_index("c")


def _comm_call(body, name, out_shape, n_in, scratch):
    return pl.pallas_call(body, name=name, out_shape=out_shape, in_specs=[ANY] * n_in, out_specs=ANY,
                          scratch_shapes=scratch)


def all_gather8(name, blk):
    def body(x_ref, out_ref, send_sems, recv_sems, local_sem):
        x, y, c = _pos()
        me, sibling = (x, y, c), (x, y, 1 - c)
        chips = [(1 - x, y), (x, 1 - y), (1 - x, 1 - y)]

        def rows(px, py, pcore):
            return out_ref.at[4 * px + 2 * py + pcore]

        def copy(k, block, to, src=None):
            return pltpu.make_async_remote_copy(
                src_ref=rows(*block) if src is None else src, dst_ref=rows(*block),
                send_sem=send_sems.at[k], recv_sem=recv_sems.at[k], device_id=to, device_id_type=MESH)

        mine = pltpu.make_async_copy(x_ref, rows(*me), local_sem)
        mine.start()
        first = [copy(0, me, sibling, src=x_ref)]
        first += [copy(1 + j, me, (*chip, c), src=x_ref) for j, chip in enumerate(chips)]
        for cp in first:
            cp.start()
        passed = [copy(4 + j, (*chip, c), sibling) for j, chip in enumerate(chips)]
        for j, chip in enumerate(chips):
            copy(1 + j, (*chip, c), me).wait_recv()
            passed[j].start()
        copy(0, sibling, me).wait_recv()
        for j, chip in enumerate(chips):
            copy(4 + j, (*chip, 1 - c), me).wait_recv()
        for cp in first + passed:
            cp.wait_send()
        mine.wait()

    return _comm_call(body, name, SDS((N_DEV,) + blk.shape, blk.dtype), 1,
                      [pltpu.SemaphoreType.DMA((7,)), pltpu.SemaphoreType.DMA((7,)), pltpu.SemaphoreType.DMA])(blk)


def sibling_swap(name, v):
    def body(v_ref, out_ref, send_sem, recv_sem):
        x, y, c = _pos()
        cp = pltpu.make_async_remote_copy(src_ref=v_ref, dst_ref=out_ref, send_sem=send_sem, recv_sem=recv_sem,
                                          device_id=(x, y, 1 - c), device_id_type=MESH)
        cp.start()
        cp.wait()

    return _comm_call(body, name, SDS(v.shape, v.dtype), 1, [pltpu.SemaphoreType.DMA, pltpu.SemaphoreType.DMA])(v)


def chip_all_to_all(name, v):
    def body(v_ref, out_ref, send_sems, recv_sems, local_sem):
        x, y, c = _pos()
        my_chip = 2 * x + y
        chips = [(1 - x, y), (x, 1 - y), (1 - x, 1 - y)]
        mine = pltpu.make_async_copy(v_ref.at[my_chip], out_ref.at[my_chip], local_sem)
        mine.start()
        cps = [pltpu.make_async_remote_copy(
            src_ref=v_ref.at[2 * px + py], dst_ref=out_ref.at[my_chip], send_sem=send_sems.at[k],
            recv_sem=recv_sems.at[k], device_id=(px, py, c), device_id_type=MESH) for k, (px, py) in enumerate(chips)]
        for cp in cps:
            cp.start()
        for k, (px, py) in enumerate(chips):
            pltpu.make_async_remote_copy(
                src_ref=v_ref.at[my_chip], dst_ref=out_ref.at[2 * px + py], send_sem=send_sems.at[k],
                recv_sem=recv_sems.at[k], device_id=(px, py, c), device_id_type=MESH).wait_recv()
        for cp in cps:
            cp.wait_send()
        mine.wait()

    return _comm_call(body, name, SDS(v.shape, v.dtype), 1,
                      [pltpu.SemaphoreType.DMA((3,)), pltpu.SemaphoreType.DMA((3,)), pltpu.SemaphoreType.DMA])(v)


def sum_slices(name, v, tr=512):
    n, R, C = v.shape
    tr = _pick(R, (tr, 256, 128, 64, 32, 16, 8))

    def body(v_ref, o_ref):
        acc = v_ref[0].astype(F32)
        for k in range(1, n):
            acc = acc + v_ref[k].astype(F32)
        o_ref[...] = acc

    return _pc(body, name=name, grid=(R // tr,), in_specs=[pl.BlockSpec((n, tr, C), lambda i: (0, i, 0))],
               out_specs=pl.BlockSpec((tr, C), lambda i: (i, 0)), out_shape=SDS((R, C), F32), sem=("parallel",))(v)


def add_slices(name, a, b, out_dtype):
    n, R, C = a.shape
    tr = _pick(R, (512, 256, 128, 64, 32, 16, 8))

    def body(a_ref, b_ref, o_ref):
        o_ref[...] = (a_ref[...].astype(F32) + b_ref[...].astype(F32)).astype(o_ref.dtype)

    spec = pl.BlockSpec((1, tr, C), lambda k, i: (k, i, 0))
    return _pc(body, name=name, grid=(n, R // tr), in_specs=[spec, spec], out_specs=spec,
               out_shape=SDS(a.shape, out_dtype), sem=("parallel", "parallel"))(a, b)


def adamw(name, w, g, m, v):
    L, R, C = w.shape
    tr = _pick(R, (256, 128, 64, 32, 16, 8))
    c1 = 1.0 / (1.0 - ADAM_B1 ** ADAM_STEP)
    c2 = 1.0 / (1.0 - ADAM_B2 ** ADAM_STEP)

    def body(w_ref, g_ref, m_ref, v_ref, d_ref, nm_ref, nv_ref):
        gv = g_ref[...]
        nm = ADAM_B1 * m_ref[...] + (1.0 - ADAM_B1) * gv
        nv = ADAM_B2 * v_ref[...] + (1.0 - ADAM_B2) * (gv * gv)
        d_ref[...] = -ADAM_LR * ((nm * c1) / (jnp.sqrt(nv * c2) + ADAM_EPS) + ADAM_WD * w_ref[...])
        nm_ref[...] = nm
        nv_ref[...] = nv

    spec = pl.BlockSpec((1, tr, C), lambda l, i: (l, i, 0))
    return _pc(body, name=name, grid=(L, R // tr), in_specs=[spec] * 4, out_specs=[spec] * 3,
               out_shape=[SDS(w.shape, F32)] * 3, sem=("parallel", "parallel"))(w, g, m, v)


WEIGHTS = ["norm1_g", "w_in", "ssd_conv_w", "ssd_conv_b", "ssd_dt_bias", "ssd_a_log", "ssd_d", "ssd_norm_g",
           "lru_conv_w", "lru_conv_b", "lru_w_a", "lru_b_a", "lru_w_x", "lru_b_x", "lru_lambda", "lru_norm_g",
           "fox_b_f", "fox_norm_g", "w_out", "norm2_g", "w_gate", "w_up", "w_down", "norm3_g", "w_ple_gate",
           "b_ple_gate", "w_ple_proj", "final_norm_g"]
BIG = {"w_in": 2, "w_out": 1, "w_gate": 2, "w_up": 2, "w_down": 1, "w_ple_gate": 1, "w_ple_proj": 2}
SHARDED_SMALL = {"ssd_conv_w": 2, "lru_conv_w": 2}
SMALL = [n for n in WEIGHTS if n not in BIG]


def _pack(arrs, rows_multiple):
    flat = jnp.concatenate([a.reshape(-1) for a in arrs])
    per = rows_multiple * LANE
    n = -(-flat.shape[0] // per) * per
    return jnp.pad(flat, (0, n - flat.shape[0])).reshape(n // LANE, LANE)


def _unpack(flat2d, shapes):
    flat = flat2d.reshape(-1)
    out, off = [], 0
    for s in shapes:
        n = int(np.prod(s))
        out.append(flat[off:off + n].reshape(s))
        off += n
    return out


def _gather_shards(name, shards, axes, dtype):
    c = lax.axis_index("c")
    packed = _pack([s.astype(dtype) for s in shards], 32)
    half = packed.shape[0] // 2
    mine = lax.dynamic_slice_in_dim(packed, c * half, half, 0)
    got = all_gather8(name, mine).reshape(N_CHIP, 2 * half, LANE)
    per_chip = [_unpack(got[k], [s.shape for s in shards]) for k in range(N_CHIP)]
    return [jnp.concatenate([per_chip[k][i] for k in range(N_CHIP)], axis=ax) for i, ax in enumerate(axes)]


def _split_shards(full, ax):
    n = full.shape[ax] // N_CHIP
    return [lax.slice_in_dim(full, k * n, (k + 1) * n, axis=ax) for k in range(N_CHIP)]


def kernel(x, p, norm1_g, w_in, ssd_conv_w, ssd_conv_b, ssd_dt_bias, ssd_a_log, ssd_d, ssd_norm_g, lru_conv_w, lru_conv_b, lru_w_a, lru_b_a, lru_w_x, lru_b_x, lru_lambda, lru_norm_g, fox_b_f, fox_norm_g, w_out, norm2_g, w_gate, w_up, w_down, norm3_g, w_ple_gate, b_ple_gate, w_ple_proj, final_norm_g, loss_target, m_norm1_g, m_w_in, m_ssd_conv_w, m_ssd_conv_b, m_ssd_dt_bias, m_ssd_a_log, m_ssd_d, m_ssd_norm_g, m_lru_conv_w, m_lru_conv_b, m_lru_w_a, m_lru_b_a, m_lru_w_x, m_lru_b_x, m_lru_lambda, m_lru_norm_g, m_fox_b_f, m_fox_norm_g, m_w_out, m_norm2_g, m_w_gate, m_w_up, m_w_down, m_norm3_g, m_w_ple_gate, m_b_ple_gate, m_w_ple_proj, m_final_norm_g, v_norm1_g, v_w_in, v_ssd_conv_w, v_ssd_conv_b, v_ssd_dt_bias, v_ssd_a_log, v_ssd_d, v_ssd_norm_g, v_lru_conv_w, v_lru_conv_b, v_lru_w_a, v_lru_b_a, v_lru_w_x, v_lru_b_x, v_lru_lambda, v_lru_norm_g, v_fox_b_f, v_fox_norm_g, v_w_out, v_norm2_g, v_w_gate, v_w_up, v_w_down, v_norm3_g, v_w_ple_gate, v_b_ple_gate, v_w_ple_proj, v_final_norm_g):
    args = (norm1_g, w_in, ssd_conv_w, ssd_conv_b, ssd_dt_bias, ssd_a_log, ssd_d, ssd_norm_g, lru_conv_w, lru_conv_b, lru_w_a, lru_b_a, lru_w_x, lru_b_x, lru_lambda, lru_norm_g, fox_b_f, fox_norm_g, w_out, norm2_g, w_gate, w_up, w_down, norm3_g, w_ple_gate, b_ple_gate, w_ple_proj, final_norm_g)
    m_args = (m_norm1_g, m_w_in, m_ssd_conv_w, m_ssd_conv_b, m_ssd_dt_bias, m_ssd_a_log, m_ssd_d, m_ssd_norm_g, m_lru_conv_w, m_lru_conv_b, m_lru_w_a, m_lru_b_a, m_lru_w_x, m_lru_b_x, m_lru_lambda, m_lru_norm_g, m_fox_b_f, m_fox_norm_g, m_w_out, m_norm2_g, m_w_gate, m_w_up, m_w_down, m_norm3_g, m_w_ple_gate, m_b_ple_gate, m_w_ple_proj, m_final_norm_g)
    v_args = (v_norm1_g, v_w_in, v_ssd_conv_w, v_ssd_conv_b, v_ssd_dt_bias, v_ssd_a_log, v_ssd_d, v_ssd_norm_g, v_lru_conv_w, v_lru_conv_b, v_lru_w_a, v_lru_b_a, v_lru_w_x, v_lru_b_x, v_lru_lambda, v_lru_norm_g, v_fox_b_f, v_fox_norm_g, v_w_out, v_norm2_g, v_w_gate, v_w_up, v_w_down, v_norm3_g, v_w_ple_gate, v_b_ple_gate, v_w_ple_proj, v_final_norm_g)
    w = dict(zip(WEIGHTS, args))
    mom = dict(zip(WEIGHTS, m_args))
    var = dict(zip(WEIGHTS, v_args))
    xi, yi, ci = _pos()
    chip = 2 * xi + yi

    big_names = list(BIG)
    full = dict(zip(big_names, _gather_shards("gather_big", [w[n] for n in big_names], [BIG[n] for n in big_names], BF16)))
    full.update(zip(SHARDED_SMALL, _gather_shards("gather_conv", [w[n] for n in SHARDED_SMALL],
                                                  list(SHARDED_SMALL.values()), F32)))
    for n in WEIGHTS:
        full.setdefault(n, w[n])
    layers = [prep_layer({n: full[n][l] for n in WEIGHTS if n != "final_norm_g"}) for l in range(DEPTH)]

    loss, grad_x, grads, g_final = local_step(x, p, loss_target, layers, final_norm_g)
    loss = lax.psum(loss, ("x", "y", "c"))
    gl = [unprep_grads(g) for g in grads]
    gfull = {n: jnp.stack([gl[l][n] for l in range(DEPTH)]) for n in WEIGHTS if n != "final_norm_g"}
    gfull["final_norm_g"] = g_final

    small_shapes = [gfull[n].shape for n in SMALL]
    gs = _pack([gfull[n] for n in SMALL], 8)
    gs = sum_slices("sum_small", all_gather8("gather_small_grads", gs))
    gsum = dict(zip(SMALL, _unpack(gs, small_shapes)))
    for n, ax in SHARDED_SMALL.items():
        k = gsum[n].shape[ax] // N_CHIP
        gsum[n] = lax.dynamic_slice_in_dim(gsum[n], chip * k, k, ax)

    per_chip = [_pack([_split_shards(gfull[n], BIG[n])[k] for n in big_names], 1024) for k in range(N_CHIP)]
    gb = jnp.stack(per_chip)
    half = gb.shape[1] // 2
    keep = lax.dynamic_slice_in_dim(gb, ci * half, half, 1)
    give = lax.dynamic_slice_in_dim(gb, (1 - ci) * half, half, 1).astype(BF16)
    part = add_slices("add_sibling", keep, sibling_swap("swap_halves", give), BF16)
    mine = sum_slices("sum_chips", chip_all_to_all("a2a_chips", part))
    other = sibling_swap("swap_result", mine)
    lo = jnp.where(ci == 0, mine, other)
    hi = jnp.where(ci == 0, other, mine)
    gsum.update(zip(big_names, _unpack(jnp.concatenate([lo, hi], axis=0), [w[n].shape for n in big_names])))

    delta, new_m, new_v = {}, {}, {}
    for n in big_names:
        delta[n], new_m[n], new_v[n] = adamw(f"adamw_{n}", w[n], gsum[n], mom[n], var[n])
    shapes = [w[n].shape for n in SMALL]
    pk = lambda d: _pack([d[n] for n in SMALL], 8)[None]
    ds, ms, vs = adamw("adamw_small", pk(w), pk(gsum), pk(mom), pk(var))
    for d, packed in ((delta, ds), (new_m, ms), (new_v, vs)):
        d.update(zip(SMALL, _unpack(packed[0], shapes)))

    return (loss, grad_x, *[gsum[n] for n in WEIGHTS], *[delta[n] for n in WEIGHTS],
            *[new_m[n] for n in WEIGHTS], *[new_v[n] for n in WEIGHTS])
```

```python
import functools
import math

import jax
import jax.numpy as jnp
import numpy as np
from jax import lax
from jax.experimental import pallas as pl
from jax.experimental.pallas import tpu as pltpu

F32, BF16 = jnp.float32, jnp.bfloat16
SDS = jax.ShapeDtypeStruct

D_MODEL = 1024
DEPTH = 2
HEAD_DIM = 64
N_HEADS = 6
SSD_W, LRU_W, FOX_W = 384, 256, 384
D_FF = 2816
PLE_DIM = 256
IN_COLS = 2956
EPS = 1e-6
LRU_C = 8.0
LANE = 128
V7X_VMEM_LIMIT = 56 * 1024 * 1024

PW = 3072
OFF_B, OFF_C, OFF_LX, OFF_LG, OFF_SM, OFF_Z, OFF_XS, OFF_Q, OFF_K, OFF_V = (
    0, 256, 512, 768, 1024, 1152, 1536, 1920, 2304, 2688)
FOX_LANE0 = 8

ADAM_LR, ADAM_B1, ADAM_B2, ADAM_EPS, ADAM_WD, ADAM_STEP = 0.001, 0.9, 0.999, 1e-08, 0.01, 10


def _iota(shape, dim):
    return lax.broadcasted_iota(jnp.int32, shape, dim)


def _pc(body, *, name, grid, in_specs, out_specs, out_shape, scratch=(), sem=None):
    return pl.pallas_call(
        body, name=name, grid=grid, in_specs=in_specs, out_specs=out_specs, out_shape=out_shape,
        scratch_shapes=list(scratch),
        compiler_params=pltpu.CompilerParams(dimension_semantics=sem, vmem_limit_bytes=V7X_VMEM_LIMIT))


def permute_in_cols(w):
    z = lambda n: jnp.zeros(w.shape[:-1] + (n,), w.dtype)
    s = lambda a, b: w[..., a:b]
    return jnp.concatenate([
        s(768, 1024), s(1024, 1280), s(1286, 1542), s(1542, 1798),
        s(1280, 1286), z(2), s(2950, 2956), z(LANE - 14),
        s(0, 384), s(384, 768), s(1798, 2182), s(2182, 2566), s(2566, 2950)], axis=-1)


def unpermute_in_cols(g):
    s = lambda a, n: g[..., a:a + n]
    return jnp.concatenate([
        s(OFF_Z, 384), s(OFF_XS, 384), s(OFF_B, 256), s(OFF_C, 256), s(OFF_SM, 6),
        s(OFF_LX, 256), s(OFF_LG, 256), s(OFF_Q, 384), s(OFF_K, 384), s(OFF_V, 384),
        s(OFF_SM + FOX_LANE0, 6)], axis=-1)


def _pick(n, cands):
    for c in cands:
        if n % c == 0:
            return c
    return n


def mm(a, b, *, name, ta=False, tb=False, out_dtypes=(F32,), extras=(), col_params=(), partials=0, epilogue=None,
       tm=None, tn=None, tk=None):
    bs = list(b) if isinstance(b, (list, tuple)) else [b]
    pair_sum = isinstance(a, (list, tuple))
    a_list = list(a) if pair_sum else [a]
    assert not pair_sum or len(a_list) == len(bs)
    a = a_list[0]
    n_a = len(a_list)
    n_acc = 1 if pair_sum else len(bs)
    extras = [e if isinstance(e, tuple) else (e, 0) for e in extras]
    M = a.shape[1] if ta else a.shape[0]
    K = a.shape[0] if ta else a.shape[1]
    N = bs[0].shape[0] if tb else bs[0].shape[1]
    tm = tm or _pick(M, (1024, 1408, 512, 256, 128))
    tn = tn or _pick(N, (1024, 1408, 768, 512, 256, 128))
    tk = tk or _pick(K, (1024, 1408, 512, 256, 128))
    nm, nn, nk = M // tm, N // tn, K // tk
    n_b, n_ex, n_cp, n_out = len(bs), len(extras), len(col_params), len(out_dtypes)
    a_bytes, b_bytes = n_a * M * K * a.dtype.itemsize, n_b * K * N * bs[0].dtype.itemsize
    rows_inner = a_bytes * nn + b_bytes <= a_bytes + b_bytes * nm

    def ij(g0, g1):
        return (g1, g0) if rows_inner else (g0, g1)

    def body(*rest):
        a_refs, rest = rest[:n_a], rest[n_a:]
        b_refs, rest = rest[:n_b], rest[n_b:]
        in_refs, rest = rest[:n_ex + n_cp], rest[n_ex + n_cp:]
        out_refs, accs = rest[:n_out + partials], rest[n_out + partials:]
        dn = (((0 if ta else 1,), (1 if tb else 0,)), ((), ()))
        dot = lambda x_ref, y_ref: lax.dot_general(x_ref[...].astype(BF16), y_ref[...].astype(BF16), dn,
                                                   preferred_element_type=F32)
        if pair_sum:
            parts = [functools.reduce(lambda u, v: u + v, [dot(x, y) for x, y in zip(a_refs, b_refs)])]
        else:
            parts = [dot(a_refs[0], b_ref) for b_ref in b_refs]

        def finish(rs):
            outs = epilogue(*rs, *[e[...] for e in in_refs]) if epilogue is not None else tuple(rs)
            for o_ref, o in zip(out_refs[:n_out], outs):
                o_ref[...] = o.astype(o_ref.dtype)
            for o_ref, o in zip(out_refs[n_out:], outs[n_out:]):
                o_ref[0] = o

        if nk == 1:
            finish(parts)
            return
        k = pl.program_id(2)

        @pl.when(k == 0)
        def _():
            for acc, part in zip(accs, parts):
                acc[...] = part

        @pl.when(k > 0)
        def _():
            for acc, part in zip(accs, parts):
                acc[...] += part

        @pl.when(k == nk - 1)
        def _():
            finish([acc[...] for acc in accs])

    def a_map(g0, g1, k):
        i, _ = ij(g0, g1)
        return (k, i) if ta else (i, k)

    def b_map(g0, g1, k):
        _, j = ij(g0, g1)
        return (j, k) if tb else (k, j)

    def ex_map(off, g0, g1, k):
        i, j = ij(g0, g1)
        return (i, j + off)

    a_spec = pl.BlockSpec((tk, tm) if ta else (tm, tk), a_map)
    b_spec = pl.BlockSpec((tn, tk) if tb else (tk, tn), b_map)
    mn_spec = pl.BlockSpec((tm, tn), functools.partial(ex_map, 0))
    outs = _pc(body, name=name, grid=(nn, nm, nk) if rows_inner else (nm, nn, nk),
               in_specs=([a_spec] * n_a + [b_spec] * n_b
                         + [pl.BlockSpec((tm, tn), functools.partial(ex_map, off)) for _, off in extras]
                         + [pl.BlockSpec((1, tn), lambda g0, g1, k: (0, ij(g0, g1)[1]))] * n_cp),
               out_specs=([mn_spec] * n_out
                          + [pl.BlockSpec((1, 1, tn), lambda g0, g1, k: (ij(g0, g1)[0], 0, ij(g0, g1)[1]))] * partials),
               out_shape=[SDS((M, N), dt) for dt in out_dtypes] + [SDS((nm, 1, N), F32)] * partials,
               scratch=[pltpu.VMEM((tm, tn), F32)] * n_acc if nk > 1 else [],
               sem=("parallel", "parallel", "arbitrary"))(*a_list, *bs, *[e for e, _ in extras], *col_params)
    return outs[0] if len(outs) == 1 else outs


def rowwise(name, fn, rows, params, row_outs, acc_outs=(), tr=512):
    rows = [r if isinstance(r, tuple) else (r, 0, r.shape[1]) for r in rows]
    T = rows[0][0].shape[0]
    tr = min(tr, T)
    n_in, n_ro, n_ac = len(rows) + len(params), len(row_outs), len(acc_outs)

    def body(*refs):
        ins, outs = refs[:n_in], refs[n_in:]
        res = fn(*[r[...] for r in ins])
        if not isinstance(res, (tuple, list)):
            res = (res,)
        for k in range(n_ro):
            outs[k][...] = res[k].astype(outs[k].dtype)
        if n_ac:
            i = pl.program_id(0)

            @pl.when(i == 0)
            def _():
                for k in range(n_ac):
                    outs[n_ro + k][...] = res[n_ro + k]

            @pl.when(i > 0)
            def _():
                for k in range(n_ac):
                    outs[n_ro + k][...] += res[n_ro + k]

    in_specs = ([pl.BlockSpec((tr, w), functools.partial(lambda cb, i: (i, cb), cb)) for (_, cb, w) in rows]
                + [pl.BlockSpec(p.shape, lambda i: (0, 0)) for p in params])
    out_specs = ([pl.BlockSpec((tr, c), lambda i: (i, 0)) for (c, _) in row_outs]
                 + [pl.BlockSpec((1, c), lambda i: (0, 0)) for c in acc_outs])
    out_shape = [SDS((T, c), dt) for (c, dt) in row_outs] + [SDS((1, c), F32) for c in acc_outs]
    outs = _pc(body, name=name, grid=(T // tr,), in_specs=in_specs, out_specs=out_specs, out_shape=out_shape,
               sem=("arbitrary",) if n_ac else ("parallel",))(*[r[0] for r in rows], *params)
    return outs[0] if len(outs) == 1 else outs


def _rms(x, g):
    return x * lax.rsqrt(jnp.mean(x * x, axis=-1, keepdims=True) + EPS) * g


def _softplus(x):
    return jnp.maximum(x, 0.0) + jnp.log(1.0 + jnp.exp(-jnp.abs(x)))


def _silu(x):
    return x * jax.nn.sigmoid(x)


def _gelu(x):
    return 0.5 * x * (1.0 + jnp.tanh(math.sqrt(2.0 / math.pi) * (x + 0.044715 * (x * x * x))))


def _neg_expm1(x):
    series = x * (1 + x / 2 * (1 + x / 3 * (1 + x / 4 * (1 + x / 5 * (1 + x / 6 * (1 + x / 7))))))
    return -jnp.where(jnp.abs(x) < 0.3, series, jnp.exp(x) - 1.0)


def _swiglu(gu):
    return _silu(gu[:, :D_FF]) * gu[:, D_FF:]


def _ple(pg, pp, b):
    return jax.nn.sigmoid(pg + b) * pp


def _ssd_elt(small, xs_act, dtbias_row, alog_row):
    lane = _iota(small.shape, 1)
    dt = _softplus(small + dtbias_row)
    adt = jnp.where(lane < N_HEADS, -jnp.exp(alog_row) * dt, 0.0)
    head = _iota(xs_act.shape, 1) // HEAD_DIM
    dt_exp = jnp.zeros_like(xs_act)
    for h in range(N_HEADS):
        dth = jnp.sum(jnp.where(lane == h, dt, 0.0), axis=1, keepdims=True)
        dt_exp = dt_exp + jnp.where(head == h, dth, 0.0)
    return adt, xs_act * dt_exp


def _fox_elt(small, bf_row):
    lane = _iota(small.shape, 1)
    keep = (lane >= FOX_LANE0) & (lane < FOX_LANE0 + N_HEADS)
    return jnp.where(keep, -_softplus(-(small + bf_row)), 0.0)


def _lru_elt(xl, pre, b_ax, lam):
    r = jax.nn.sigmoid(pre[:, :LRU_W] + b_ax[:, :LRU_W])
    i = jax.nn.sigmoid(pre[:, LRU_W:] + b_ax[:, LRU_W:])
    log_a = -LRU_C * r * _softplus(-lam)
    a = jnp.exp(log_a)
    mult = jnp.sqrt(_neg_expm1(2.0 * log_a))
    return a, mult * (i * xl)


def _mix_post(yraw, xs_act, z, hl, lgate, yfox, dexp, g_ssd, g_lru, g_fox):
    y_ssd = _rms((yraw + xs_act * dexp) * _silu(z), g_ssd)
    y_lru = _rms(hl * _gelu(lgate), g_lru)
    y_fox = _rms(yfox, g_fox)
    return jnp.concatenate([y_ssd, y_lru, y_fox], axis=-1)


def _colsum(x):
    return jnp.sum(x, axis=0, keepdims=True)


def _shift_down(x, d):
    if d == 0:
        return x
    return jnp.where(_iota(x.shape, 0) >= d, pltpu.roll(x, d, 0), 0.0)


def _shift_up(x, d):
    if d == 0:
        return x
    s = x.shape[0]
    return jnp.where(_iota(x.shape, 0) < s - d, pltpu.roll(x, s - d, 0), 0.0)


def _conv_core(x, w, b):
    y = b + w[3:4, :] * x
    for k in range(3):
        y = y + w[k:k + 1, :] * _shift_down(x, 3 - k)
    return y


def seq_conv(name, src, col, width, w8, b, *, batch, silu, out_dtype):
    T = src.shape[0]
    S = T // batch
    c0 = col // LANE

    def body(x_ref, w_ref, b_ref, o_ref):
        y = _conv_core(x_ref[...], w_ref[...], b_ref[...])
        o_ref[...] = (_silu(y) if silu else y).astype(o_ref.dtype)

    return _pc(body, name=name, grid=(batch, width // LANE),
               in_specs=[pl.BlockSpec((S, LANE), lambda bi, ci: (bi, c0 + ci)),
                         pl.BlockSpec((8, LANE), lambda bi, ci: (0, ci)),
                         pl.BlockSpec((1, LANE), lambda bi, ci: (0, ci))],
               out_specs=pl.BlockSpec((S, LANE), lambda bi, ci: (bi, ci)),
               out_shape=SDS((T, width), out_dtype), sem=("parallel", "parallel"))(src, w8, b)


def seq_conv_bwd(name, src, col, width, w8, b, dy, *, batch, silu):
    T = src.shape[0]
    S = T // batch
    c0 = col // LANE

    def body(x_ref, w_ref, b_ref, dy_ref, dx_ref, dw_ref, db_ref):
        x, w = x_ref[...], w_ref[...]
        dpre = dy_ref[...].astype(F32)
        if silu:
            dpre = jax.vjp(_silu, _conv_core(x, w, b_ref[...]))[1](dpre)[0]
        dx = w[3:4, :] * dpre
        for k in range(3):
            dx = dx + w[k:k + 1, :] * _shift_up(dpre, 3 - k)
        dx_ref[...] = dx.astype(dx_ref.dtype)
        row8 = _iota((8, LANE), 0)
        dw = jnp.zeros((8, LANE), F32)
        for k in range(4):
            dw = dw + jnp.where(row8 == k, _colsum(dpre * _shift_down(x, 3 - k)), 0.0)
        db = _colsum(dpre)
        bi = pl.program_id(1)

        @pl.when(bi == 0)
        def _():
            dw_ref[...] = dw
            db_ref[...] = db

        @pl.when(bi > 0)
        def _():
            dw_ref[...] += dw
            db_ref[...] += db

    return _pc(body, name=name, grid=(width // LANE, batch),
               in_specs=[pl.BlockSpec((S, LANE), lambda ci, bi: (bi, c0 + ci)),
                         pl.BlockSpec((8, LANE), lambda ci, bi: (0, ci)),
                         pl.BlockSpec((1, LANE), lambda ci, bi: (0, ci)),
                         pl.BlockSpec((S, LANE), lambda ci, bi: (bi, ci))],
               out_specs=[pl.BlockSpec((S, LANE), lambda ci, bi: (bi, ci)),
                          pl.BlockSpec((8, LANE), lambda ci, bi: (0, ci)),
                          pl.BlockSpec((1, LANE), lambda ci, bi: (0, ci))],
               out_shape=[SDS((T, width), BF16), SDS((8, width), F32), SDS((1, width), F32)],
               sem=("parallel", "arbitrary"))(src, w8, b, dy)


def _split3_dot(tri, x):
    hi = x.astype(BF16)
    r1 = x - hi.astype(F32)
    mid = r1.astype(BF16)
    lo = (r1 - mid.astype(F32)).astype(BF16)
    d = lambda v: jnp.dot(tri, v, preferred_element_type=F32)
    return d(hi) + d(mid) + d(lo)


def seq_cumsum(name, x, *, batch, reverse=False, nsum=1, trow=None):
    T = x.shape[0]
    S = T // batch
    ch = min(256, S)
    nch = S // ch

    def body(x_ref, o_ref, *maybe_t):
        r, c = _iota((ch, ch), 0), _iota((ch, ch), 1)
        tri = jnp.where((c >= r) if reverse else (c <= r), 1.0, 0.0).astype(BF16)
        carry = jnp.zeros((1, LANE), F32)
        for k in (range(nch - 1, -1, -1) if reverse else range(nch)):
            xc = x_ref[k * ch:(k + 1) * ch, 0:LANE]
            for m in range(1, nsum):
                xc = xc + x_ref[k * ch:(k + 1) * ch, m * LANE:(m + 1) * LANE]
            o_ref[k * ch:(k + 1) * ch, :] = _split3_dot(tri, xc) + carry
            carry = carry + _colsum(xc)
        if trow is not None:
            maybe_t[0][...] = o_ref[...].T[trow:trow + 8, :]

    out_specs = [pl.BlockSpec((S, LANE), lambda bi: (bi, 0))]
    out_shape = [SDS((T, LANE), F32)]
    if trow is not None:
        out_specs.append(pl.BlockSpec((8, S), lambda bi: (bi, 0)))
        out_shape.append(SDS((batch * 8, S), F32))
    outs = _pc(body, name=name, grid=(batch,), in_specs=[pl.BlockSpec((S, LANE * nsum), lambda bi: (bi, 0))],
               out_specs=out_specs, out_shape=out_shape, sem=("parallel",))(x)
    return outs if trow is not None else outs[0]


def lru_scan(name, a, u, *, batch):
    T, W = a.shape
    S = T // batch

    def body(a_ref, u_ref, h_ref):
        row = _iota((8, W), 0)

        def step(g, h):
            off = pl.multiple_of(g * 8, 8)
            at, ut = a_ref[pl.ds(off, 8), :], u_ref[pl.ds(off, 8), :]
            acc = jnp.zeros((8, W), F32)
            for r in range(8):
                h = at[r:r + 1, :] * h + ut[r:r + 1, :]
                acc = jnp.where(row == r, jnp.broadcast_to(h, (8, W)), acc)
            h_ref[pl.ds(off, 8), :] = acc
            return h

        lax.fori_loop(0, S // 8, step, jnp.zeros((1, W), F32))

    spec = pl.BlockSpec((S, W), lambda bi: (bi, 0))
    return _pc(body, name=name, grid=(batch,), in_specs=[spec, spec], out_specs=spec,
               out_shape=SDS((T, W), F32), sem=("parallel",))(a, u)


def lru_scan_bwd(name, a, h, dh, *, batch):
    T, W = a.shape
    S = T // batch
    ng = S // 8

    def body(a_ref, h_ref, dh_ref, da_ref, du_ref):
        row = _iota((8, W), 0)

        def step(k, c):
            g_idx = ng - 1 - k
            off = pl.multiple_of(g_idx * 8, 8)
            offp = pl.multiple_of(jnp.maximum(g_idx - 1, 0) * 8, 8)
            at, ht, dt = a_ref[pl.ds(off, 8), :], h_ref[pl.ds(off, 8), :], dh_ref[pl.ds(off, 8), :]
            hp = jnp.where(g_idx > 0, h_ref[pl.ds(offp, 8), :], 0.0)
            da = jnp.zeros((8, W), F32)
            du = jnp.zeros((8, W), F32)
            for r in range(7, -1, -1):
                g = dt[r:r + 1, :] + c
                hprev = ht[r - 1:r, :] if r > 0 else hp[7:8, :]
                du = jnp.where(row == r, jnp.broadcast_to(g, (8, W)), du)
                da = jnp.where(row == r, jnp.broadcast_to(g * hprev, (8, W)), da)
                c = at[r:r + 1, :] * g
            da_ref[pl.ds(off, 8), :] = da
            du_ref[pl.ds(off, 8), :] = du
            return c

        lax.fori_loop(0, ng, step, jnp.zeros((1, W), F32))

    spec = pl.BlockSpec((S, W), lambda bi: (bi, 0))
    return _pc(body, name=name, grid=(batch,), in_specs=[spec] * 3, out_specs=[spec] * 2,
               out_shape=[SDS((T, W), F32)] * 2, sem=("parallel",))(a, h, dh)


def _nt(a, b):
    return lax.dot_general(a, b, (((1,), (1,)), ((), ())), preferred_element_type=F32)


def _tn(a, b):
    return lax.dot_general(a, b, (((0,), (0,)), ((), ())), preferred_element_type=F32)


def _tile(S, t=256):
    return min(t, S)


def ssd_attn_fwd(name, cm, bm, xd, cum, cum_t, *, batch):
    T = cm.shape[0]
    S = T // batch
    tq = tk = _tile(S)
    nq = S // tq

    def body(c_ref, b_ref, x_ref, cum_ref, cumt_ref, y_ref):
        i = pl.program_id(1)
        cq, cmq = cum_ref[...], c_ref[...]
        rowi, coli = _iota((tq, tk), 0), _iota((tq, tk), 1)
        half = _iota((tk, LANE), 1) // HEAD_DIM

        def step(j, accs, diag):
            off = pl.multiple_of(j * tk, tk)
            bj = b_ref[pl.ds(off, tk), :]
            gm = [_nt(cmq[:, g * LANE:(g + 1) * LANE], bj[:, g * LANE:(g + 1) * LANE]) for g in range(2)]
            ckt = cumt_ref[:, pl.ds(off, tk)]
            new = []
            for p in range(3):
                xp = x_ref[pl.ds(off, tk), p * LANE:(p + 1) * LANE]
                ws, xs = [], []
                for hh in range(2):
                    h = 2 * p + hh
                    seg = cq[:, h:h + 1] - ckt[h:h + 1, :]
                    e = jnp.exp(jnp.where(rowi >= coli, seg, -jnp.inf) if diag else seg)
                    ws.append((gm[h // 3] * e).astype(BF16))
                    xs.append(jnp.where(half == hh, xp, jnp.zeros_like(xp)))
                new.append(accs[p] + jnp.dot(jnp.concatenate(ws, axis=1), jnp.concatenate(xs, axis=0),
                                             preferred_element_type=F32))
            return tuple(new)

        accs = lax.fori_loop(0, i, functools.partial(step, diag=False),
                             tuple(jnp.zeros((tq, LANE), F32) for _ in range(3)))
        accs = step(i, accs, True)
        y_ref[...] = jnp.concatenate(accs, axis=1)

    return _pc(body, name=name, grid=(batch, nq),
               in_specs=[pl.BlockSpec((tq, 256), lambda b, i: (b * nq + i, 0)),
                         pl.BlockSpec((S, 256), lambda b, i: (b, 0)),
                         pl.BlockSpec((S, SSD_W), lambda b, i: (b, 0)),
                         pl.BlockSpec((tq, LANE), lambda b, i: (b * nq + i, 0)),
                         pl.BlockSpec((8, S), lambda b, i: (b, 0))],
               out_specs=pl.BlockSpec((tq, SSD_W), lambda b, i: (b * nq + i, 0)),
               out_shape=SDS((T, SSD_W), F32), sem=("parallel", "parallel"))(cm, bm, xd, cum, cum_t)


def ssd_attn_bwd(name, cm, bm, xd, cum, cum_t, dy, *, batch):
    T = cm.shape[0]
    S = T // batch
    tq = tk = _tile(S, 512)
    nq = S // tq

    def body(c_ref, b_ref, x_ref, cum_ref, cumt_ref, dy_ref, dx_ref, db_ref, dc_ref, dcum_ref, dcumt_ref):
        dx_ref[...] = jnp.zeros_like(dx_ref)
        db_ref[...] = jnp.zeros_like(db_ref)
        dcum_ref[...] = jnp.zeros_like(dcum_ref)
        dcumt_ref[...] = jnp.zeros_like(dcumt_ref)
        rowi, coli = _iota((tq, tk), 0), _iota((tq, tk), 1)
        halfq = _iota((tq, LANE), 1) // HEAD_DIM
        lane_q = _iota((tq, LANE), 1)

        def qblock(i, _):
            qoff = pl.multiple_of(i * tq, tq)
            cq = cum_ref[pl.ds(qoff, tq), :]
            cmq = c_ref[pl.ds(qoff, tq), :]
            dyq = dy_ref[pl.ds(qoff, tq), :]
            dyh = [[jnp.where(halfq == hh, dyq[:, p * LANE:(p + 1) * LANE], 0.0).astype(BF16) for hh in range(2)]
                   for p in range(3)]

            def step(j, carry, diag):
                dcq, rs_acc = carry
                off = pl.multiple_of(j * tk, tk)
                bj = b_ref[pl.ds(off, tk), :]
                gm = [_nt(cmq[:, g * LANE:(g + 1) * LANE], bj[:, g * LANE:(g + 1) * LANE]) for g in range(2)]
                ckt = cumt_ref[:, pl.ds(off, tk)]
                dgm = [jnp.zeros((tq, tk), F32), jnp.zeros((tq, tk), F32)]
                for p in range(3):
                    xp = x_ref[pl.ds(off, tk), p * LANE:(p + 1) * LANE]
                    ws = []
                    for hh in range(2):
                        h = 2 * p + hh
                        seg = cq[:, h:h + 1] - ckt[h:h + 1, :]
                        e = jnp.exp(jnp.where(rowi >= coli, seg, -jnp.inf) if diag else seg)
                        w = gm[h // 3] * e
                        dw = _nt(dyh[p][hh], xp)
                        zz = dw * w
                        rs_acc = rs_acc + jnp.where(lane_q == h, jnp.sum(zz, axis=1, keepdims=True), 0.0)
                        dcumt_ref[h:h + 1, pl.ds(off, tk)] += _colsum(zz)
                        dgm[h // 3] = dgm[h // 3] + dw * e
                        ws.append(w.astype(BF16))
                    dx_ref[pl.ds(off, tk), p * LANE:(p + 1) * LANE] += _tn(
                        jnp.concatenate(ws, axis=0), jnp.concatenate(dyh[p], axis=0))
                new_dcq = []
                for g in range(2):
                    dg = dgm[g].astype(BF16)
                    new_dcq.append(dcq[g] + jnp.dot(dg, bj[:, g * LANE:(g + 1) * LANE], preferred_element_type=F32))
                    db_ref[pl.ds(off, tk), g * LANE:(g + 1) * LANE] += _tn(dg, cmq[:, g * LANE:(g + 1) * LANE])
                return tuple(new_dcq), rs_acc

            carry = lax.fori_loop(
                0, i, functools.partial(step, diag=False),
                ((jnp.zeros((tq, LANE), F32), jnp.zeros((tq, LANE), F32)), jnp.zeros((tq, LANE), F32)))
            dcq, rs_acc = step(i, carry, True)
            dc_ref[pl.ds(qoff, tq), :] = jnp.concatenate(dcq, axis=1)
            dcum_ref[pl.ds(qoff, tq), :] += rs_acc
            return 0

        lax.fori_loop(0, nq, qblock, 0)
        dcum_ref[...] = dcum_ref[...] - dcumt_ref[...].T

    s256 = pl.BlockSpec((S, 256), lambda b: (b, 0))
    s384 = pl.BlockSpec((S, SSD_W), lambda b: (b, 0))
    s128 = pl.BlockSpec((S, LANE), lambda b: (b, 0))
    return _pc(body, name=name, grid=(batch,),
               in_specs=[s256, s256, s384, s128, pl.BlockSpec((8, S), lambda b: (b, 0)), s384],
               out_specs=[s384, s256, s256, s128],
               out_shape=[SDS((T, SSD_W), F32), SDS((T, 256), F32), SDS((T, 256), F32), SDS((T, LANE), F32)],
               scratch=[pltpu.VMEM((LANE, S), F32)], sem=("parallel",))(cm, bm, xd, cum, cum_t, dy)


NEG_BIG = -1e30


def fox_attn_fwd(name, proj, cum, cum_t, *, batch):
    T = proj.shape[0]
    S = T // batch
    tq = tk = _tile(S, 512)
    nq = S // tq
    scale = HEAD_DIM ** -0.5
    qb, kb, vb = OFF_Q // LANE, OFF_K // LANE, OFF_V // LANE

    def body(q_ref, k_ref, v_ref, cum_ref, cumt_ref, o_ref, lse_ref):
        p, i = pl.program_id(1), pl.program_id(2)
        cq = cum_ref[...]
        lane_q = _iota((tq, LANE), 1)
        halfq, halfk = lane_q // HEAD_DIM, _iota((tk, LANE), 1) // HEAD_DIM
        qs = q_ref[...] * scale
        qh = [jnp.where(halfq == hh, qs, 0.0).astype(BF16) for hh in range(2)]
        rowi, coli = _iota((tq, tk), 0), _iota((tq, tk), 1)
        cqh = [jnp.sum(jnp.where(lane_q == FOX_LANE0 + 2 * p + hh, cq, 0.0), axis=1, keepdims=True) for hh in range(2)]
        row8 = _iota((8, tk), 0)

        def step(j, carry, diag):
            ms, ls, acc = carry
            off = pl.multiple_of(j * tk, tk)
            kj = k_ref[pl.ds(off, tk), :].astype(BF16)
            vj = v_ref[pl.ds(off, tk), :].astype(BF16)
            ckt = cumt_ref[:, pl.ds(off, tk)]
            ps, vs, new_m, new_l, alphas = [], [], [], [], []
            for hh in range(2):
                ck = jnp.sum(jnp.where(row8 == 2 * p + hh, ckt, 0.0), axis=0, keepdims=True)
                logits = _nt(qh[hh], kj) + (cqh[hh] - ck)
                if diag:
                    logits = jnp.where(rowi >= coli, logits, -jnp.inf)
                m = jnp.maximum(ms[hh], jnp.max(logits, axis=1, keepdims=True))
                alpha = jnp.exp(ms[hh] - m)
                pr = jnp.exp(logits - m)
                new_m.append(m)
                new_l.append(alpha * ls[hh] + jnp.sum(pr, axis=1, keepdims=True))
                alphas.append(alpha)
                ps.append(pr.astype(BF16))
                vs.append(jnp.where(halfk == hh, vj, jnp.zeros_like(vj)))
            acc = acc * jnp.where(halfq == 0, alphas[0], alphas[1]) + jnp.dot(
                jnp.concatenate(ps, axis=1), jnp.concatenate(vs, axis=0), preferred_element_type=F32)
            return tuple(new_m), tuple(new_l), acc

        init = ((jnp.full((tq, 1), NEG_BIG, F32),) * 2, (jnp.zeros((tq, 1), F32),) * 2, jnp.zeros((tq, LANE), F32))
        ms, ls, acc = step(i, lax.fori_loop(0, i, functools.partial(step, diag=False), init), True)
        o_ref[...] = acc / jnp.where(halfq == 0, ls[0], ls[1])
        lse_ref[...] = (jnp.where(lane_q == 0, ms[0] + jnp.log(ls[0]), 0.0)
                        + jnp.where(lane_q == 1, ms[1] + jnp.log(ls[1]), 0.0))

    return _pc(body, name=name, grid=(batch, 3, nq),
               in_specs=[pl.BlockSpec((tq, LANE), lambda b, p, i: (b * nq + i, qb + p)),
                         pl.BlockSpec((S, LANE), lambda b, p, i: (b, kb + p)),
                         pl.BlockSpec((S, LANE), lambda b, p, i: (b, vb + p)),
                         pl.BlockSpec((tq, LANE), lambda b, p, i: (b * nq + i, 0)),
                         pl.BlockSpec((8, S), lambda b, p, i: (b, 0))],
               out_specs=[pl.BlockSpec((tq, LANE), lambda b, p, i: (b * nq + i, p))] * 2,
               out_shape=[SDS((T, FOX_W), F32)] * 2, sem=("parallel", "parallel", "parallel"))(proj, proj, proj, cum, cum_t)


def fox_attn_bwd(name, proj, o, do, lse, cum, cum_t, *, batch):
    T = proj.shape[0]
    S = T // batch
    tq = tk = _tile(S, 512)
    nq = S // tq
    scale = HEAD_DIM ** -0.5
    qb, kb, vb = OFF_Q // LANE, OFF_K // LANE, OFF_V // LANE

    def body(q_ref, k_ref, v_ref, o_ref, do_ref, lse_ref, cum_ref, cumt_ref,
             dq_ref, dk_ref, dv_ref, dcum_ref, dk_acc, dv_acc, dcumt_ref):
        p = pl.program_id(1)
        dk_acc[...] = jnp.zeros_like(dk_acc)
        dv_acc[...] = jnp.zeros_like(dv_acc)
        dcum_ref[...] = jnp.zeros_like(dcum_ref)
        dcumt_ref[...] = jnp.zeros_like(dcumt_ref)
        lane_q = _iota((tq, LANE), 1)
        halfq, halfk = lane_q // HEAD_DIM, _iota((tk, LANE), 1) // HEAD_DIM
        rowi, coli = _iota((tq, tk), 0), _iota((tq, tk), 1)
        row8 = _iota((8, tk), 0)

        def qblock(i, _):
            qoff = pl.multiple_of(i * tq, tq)
            cq = cum_ref[pl.ds(qoff, tq), :]
            qs = q_ref[pl.ds(qoff, tq), :] * scale
            doq = do_ref[pl.ds(qoff, tq), :]
            lse = lse_ref[pl.ds(qoff, tq), :]
            delta = doq * o_ref[pl.ds(qoff, tq), :]
            qh, doh, cqh, lseh, dlt = [], [], [], [], []
            for hh in range(2):
                qh.append(jnp.where(halfq == hh, qs, 0.0).astype(BF16))
                doh.append(jnp.where(halfq == hh, doq, 0.0).astype(BF16))
                cqh.append(jnp.sum(jnp.where(lane_q == FOX_LANE0 + 2 * p + hh, cq, 0.0), axis=1, keepdims=True))
                lseh.append(jnp.sum(jnp.where(lane_q == hh, lse, 0.0), axis=1, keepdims=True))
                dlt.append(jnp.sum(jnp.where(halfq == hh, delta, 0.0), axis=1, keepdims=True))

            def step(j, carry, diag):
                dq, rs = carry
                off = pl.multiple_of(j * tk, tk)
                kj = k_ref[pl.ds(off, tk), :].astype(BF16)
                vj = v_ref[pl.ds(off, tk), :].astype(BF16)
                ckt = cumt_ref[:, pl.ds(off, tk)]
                dss, prs, ks = [], [], []
                for hh in range(2):
                    ck = jnp.sum(jnp.where(row8 == 2 * p + hh, ckt, 0.0), axis=0, keepdims=True)
                    logits = _nt(qh[hh], kj) + ((cqh[hh] - lseh[hh]) - ck)
                    if diag:
                        logits = jnp.where(rowi >= coli, logits, -jnp.inf)
                    pr = jnp.exp(logits)
                    ds = pr * (_nt(doh[hh], vj) - dlt[hh])
                    rs = rs + jnp.where(lane_q == FOX_LANE0 + 2 * p + hh, jnp.sum(ds, axis=1, keepdims=True), 0.0)
                    cs = _colsum(ds)
                    dcumt_ref[0:8, pl.ds(off, tk)] += jnp.where(row8 == 2 * p + hh, cs, 0.0)
                    dss.append(ds.astype(BF16))
                    prs.append(pr.astype(BF16))
                    ks.append(jnp.where(halfk == hh, kj, jnp.zeros_like(kj)))
                dq = dq + jnp.dot(jnp.concatenate(dss, axis=1), jnp.concatenate(ks, axis=0), preferred_element_type=F32)
                dk_acc[pl.ds(off, tk), :] += _tn(jnp.concatenate(dss, axis=0), jnp.concatenate(qh, axis=0))
                dv_acc[pl.ds(off, tk), :] += _tn(jnp.concatenate(prs, axis=0), jnp.concatenate(doh, axis=0))
                return dq, rs

            carry = lax.fori_loop(0, i, functools.partial(step, diag=False),
                                  (jnp.zeros((tq, LANE), F32), jnp.zeros((tq, LANE), F32)))
            dq, rs = step(i, carry, True)
            dq_ref[pl.ds(qoff, tq), :] = (dq * scale).astype(dq_ref.dtype)
            dcum_ref[pl.ds(qoff, tq), :] += rs
            return 0

        lax.fori_loop(0, nq, qblock, 0)
        dk_ref[...] = dk_acc[...].astype(dk_ref.dtype)
        dv_ref[...] = dv_acc[...].astype(dv_ref.dtype)
        dct = dcumt_ref[...].T
        dcum_ref[...] = dcum_ref[...] - pltpu.roll(dct, FOX_LANE0, 1)

    sp = lambda c0: pl.BlockSpec((S, LANE), lambda b, p: (b, c0 + p))
    s0 = pl.BlockSpec((S, LANE), lambda b, p: (b, 0))
    return _pc(body, name=name, grid=(batch, 3),
               in_specs=[sp(qb), sp(kb), sp(vb), sp(0), sp(0), sp(0), s0, pl.BlockSpec((8, S), lambda b, p: (b, 0))],
               out_specs=[sp(0)] * 4,
               out_shape=[SDS((T, FOX_W), BF16)] * 3 + [SDS((T, FOX_W), F32)],
               scratch=[pltpu.VMEM((S, LANE), F32), pltpu.VMEM((S, LANE), F32), pltpu.VMEM((LANE, S), F32)],
               sem=("parallel", "parallel"))(proj, proj, proj, o, do, lse, cum, cum_t)


def _row(v, width=None, at=0):
    v = v.astype(F32)
    width = width or v.shape[0]
    return jnp.pad(v, (at, width - at - v.shape[0]))[None, :]


def _pad8(w4):
    return jnp.pad(w4.astype(F32), ((0, 4), (0, 0)))


def _block_diag(w):
    out = jnp.zeros((LRU_W, LRU_W), w.dtype)
    for g in range(4):
        out = lax.dynamic_update_slice(out, w[g], (g * 64, g * 64))
    return out


def prep_layer(f):
    cw, cb = f["ssd_conv_w"], f["ssd_conv_b"]
    return dict(
        win=permute_in_cols(f["w_in"]), wout=f["w_out"], wg=f["w_gate"], wu=f["w_up"], wd=f["w_down"], wpg=f["w_ple_gate"], wpp=f["w_ple_proj"],
        wax=jnp.concatenate([_block_diag(f["lru_w_a"]), _block_diag(f["lru_w_x"])], axis=1),
        g1=_row(f["norm1_g"]), g2=_row(f["norm2_g"]), g3=_row(f["norm3_g"]),
        cw_xs=_pad8(cw[:, :384]), cb_xs=_row(cb[:384]), cw_b=_pad8(cw[:, 384:640]), cb_b=_row(cb[384:640]),
        cw_c=_pad8(cw[:, 640:]), cb_c=_row(cb[640:]), cw_l=_pad8(f["lru_conv_w"]), cb_l=_row(f["lru_conv_b"]),
        dtbias_row=_row(f["ssd_dt_bias"], LANE), alog_row=_row(f["ssd_a_log"], LANE),
        dexp=jnp.repeat(f["ssd_d"].astype(F32), HEAD_DIM)[None, :], g_ssd=_row(f["ssd_norm_g"]),
        b_ax=_row(jnp.concatenate([f["lru_b_a"], f["lru_b_x"]])), lam=_row(f["lru_lambda"]), g_lru=_row(f["lru_norm_g"]),
        bf_row=_row(f["fox_b_f"], LANE, FOX_LANE0), g_fox=_row(f["fox_norm_g"]), b_pg=_row(f["b_ple_gate"]))


def unprep_grads(g):
    blocks = lambda m: jnp.stack([m[i * 64:(i + 1) * 64, i * 64:(i + 1) * 64] for i in range(4)])
    return dict(
        norm1_g=g["g1"][0], w_in=unpermute_in_cols(g["win"]),
        ssd_conv_w=jnp.concatenate([g["cw_xs"][:4], g["cw_b"][:4], g["cw_c"][:4]], axis=1),
        ssd_conv_b=jnp.concatenate([g["cb_xs"][0], g["cb_b"][0], g["cb_c"][0]]),
        ssd_dt_bias=g["dtbias_row"][0, :N_HEADS], ssd_a_log=g["alog_row"][0, :N_HEADS],
        ssd_d=jnp.sum(g["dexp"].reshape(N_HEADS, HEAD_DIM), axis=1), ssd_norm_g=g["g_ssd"][0],
        lru_conv_w=g["cw_l"][:4], lru_conv_b=g["cb_l"][0],
        lru_w_a=blocks(g["wax"][:, :LRU_W]), lru_b_a=g["b_ax"][0, :LRU_W],
        lru_w_x=blocks(g["wax"][:, LRU_W:]), lru_b_x=g["b_ax"][0, LRU_W:],
        lru_lambda=g["lam"][0], lru_norm_g=g["g_lru"][0],
        fox_b_f=g["bf_row"][0, FOX_LANE0:FOX_LANE0 + N_HEADS], fox_norm_g=g["g_fox"][0],
        w_out=g["wout"], norm2_g=g["g2"][0], w_gate=g["wg"], w_up=g["wu"], w_down=g["wd"],
        norm3_g=g["g3"][0], w_ple_gate=g["wpg"], b_ple_gate=g["b_pg"][0], w_ple_proj=g["wpp"])


def _view(a, off, width):
    return (a, off // width, width)


def _add_epilogue(acc, e):
    return (acc + e,)


def mixer_fwd(proj, w, batch, tag):
    sm = _view(proj, OFF_SM, LANE)
    conv = functools.partial(seq_conv, batch=batch)
    cmc = conv(f"{tag}_conv_c", proj, OFF_C, 256, w["cw_c"], w["cb_c"], silu=True, out_dtype=BF16)
    bmc = conv(f"{tag}_conv_b", proj, OFF_B, 256, w["cw_b"], w["cb_b"], silu=True, out_dtype=BF16)
    xs_act = conv(f"{tag}_conv_xs", proj, OFF_XS, SSD_W, w["cw_xs"], w["cb_xs"], silu=True, out_dtype=F32)
    xl = conv(f"{tag}_conv_l", proj, OFF_LX, LRU_W, w["cw_l"], w["cb_l"], silu=False, out_dtype=F32)
    adt, xd = rowwise(f"{tag}_ssd_elt", _ssd_elt, [sm, xs_act], [w["dtbias_row"], w["alog_row"]],
                      [(LANE, F32), (SSD_W, BF16)])
    cum_a, cum_at = seq_cumsum(f"{tag}_cum_a", adt, batch=batch, trow=0)
    yraw = ssd_attn_fwd(f"{tag}_ssd_fwd", cmc, bmc, xd, cum_a, cum_at, batch=batch)
    logf = rowwise(f"{tag}_fox_elt", _fox_elt, [sm], [w["bf_row"]], [(LANE, F32)])
    cum_f, cum_ft = seq_cumsum(f"{tag}_cum_f", logf, batch=batch, trow=FOX_LANE0)
    o, lse = fox_attn_fwd(f"{tag}_fox_fwd", proj, cum_f, cum_ft, batch=batch)
    pre = mm(xl, w["wax"], name=f"{tag}_mm_lru_gates")
    a, u = rowwise(f"{tag}_lru_elt", _lru_elt, [xl, pre], [w["b_ax"], w["lam"]], [(LRU_W, F32), (LRU_W, F32)])
    hl = lru_scan(f"{tag}_lru_scan", a, u, batch=batch)
    ycat = rowwise(f"{tag}_mix_post", _mix_post,
                   [yraw, xs_act, _view(proj, OFF_Z, SSD_W), hl, _view(proj, OFF_LG, LRU_W), o],
                   [w["dexp"], w["g_ssd"], w["g_lru"], w["g_fox"]], [(D_MODEL, BF16)], tr=256)
    saved = dict(cmc=cmc, bmc=bmc, xs_act=xs_act, xl=xl, xd=xd, cum_a=cum_a, cum_at=cum_at, yraw=yraw,
                 cum_f=cum_f, cum_ft=cum_ft, o=o, lse=lse, pre=pre, a=a, hl=hl)
    return ycat, saved


def mixer_bwd(dycat, proj, w, s, batch, tag):
    sm = _view(proj, OFF_SM, LANE)
    g = {}

    def post_bwd(yraw, xs_act, z, hl, lg, o, dyc, dexp, g_ssd, g_lru, g_fox):
        return jax.vjp(_mix_post, yraw, xs_act, z, hl, lg, o, dexp, g_ssd, g_lru, g_fox)[1](dyc)

    (dyraw, dxs1, dz, dhl, dlg, do, g["dexp"], g["g_ssd"], g["g_lru"], g["g_fox"]) = rowwise(
        f"{tag}_mix_post_bwd", post_bwd,
        [s["yraw"], s["xs_act"], _view(proj, OFF_Z, SSD_W), s["hl"], _view(proj, OFF_LG, LRU_W), s["o"], dycat],
        [w["dexp"], w["g_ssd"], w["g_lru"], w["g_fox"]],
        [(SSD_W, F32), (SSD_W, F32), (SSD_W, BF16), (LRU_W, F32), (LRU_W, BF16), (FOX_W, F32)],
        [SSD_W, SSD_W, LRU_W, FOX_W], tr=256)

    dq, dk, dv, dcum3 = fox_attn_bwd(f"{tag}_fox_bwd", proj, s["o"], do, s["lse"], s["cum_f"], s["cum_ft"], batch=batch)
    dlogf = seq_cumsum(f"{tag}_rcum_f", dcum3, batch=batch, reverse=True, nsum=3)

    dxd, dbm, dcm, dcum_a = ssd_attn_bwd(f"{tag}_ssd_bwd", s["cmc"], s["bmc"], s["xd"], s["cum_a"], s["cum_at"], dyraw,
                                         batch=batch)
    dadt = seq_cumsum(f"{tag}_rcum_a", dcum_a, batch=batch, reverse=True)

    def ssd_elt_bwd(small, xs_act, dadt_, dxd_, dxs1_, dtbias, alog):
        dsm, dxs, ddtb, dalog = jax.vjp(_ssd_elt, small, xs_act, dtbias, alog)[1]((dadt_, dxd_))
        return dsm, dxs + dxs1_, ddtb, dalog

    dsm_s, dxs_act, g["dtbias_row"], g["alog_row"] = rowwise(
        f"{tag}_ssd_elt_bwd", ssd_elt_bwd, [sm, s["xs_act"], dadt, dxd, dxs1], [w["dtbias_row"], w["alog_row"]],
        [(LANE, F32), (SSD_W, F32)], [LANE, LANE])

    def fox_elt_bwd(small, dlogf_, dsm_s_, bf_row):
        dsm, dbf = jax.vjp(_fox_elt, small, bf_row)[1](dlogf_)
        return dsm + dsm_s_, dbf

    dsm, g["bf_row"] = rowwise(f"{tag}_fox_elt_bwd", fox_elt_bwd, [sm, dlogf, dsm_s], [w["bf_row"]],
                               [(LANE, BF16)], [LANE])

    cbwd = functools.partial(seq_conv_bwd, batch=batch)
    dxs_raw, g["cw_xs"], g["cb_xs"] = cbwd(f"{tag}_conv_xs_bwd", proj, OFF_XS, SSD_W, w["cw_xs"], w["cb_xs"], dxs_act, silu=True)
    db_raw, g["cw_b"], g["cb_b"] = cbwd(f"{tag}_conv_b_bwd", proj, OFF_B, 256, w["cw_b"], w["cb_b"], dbm, silu=True)
    dc_raw, g["cw_c"], g["cb_c"] = cbwd(f"{tag}_conv_c_bwd", proj, OFF_C, 256, w["cw_c"], w["cb_c"], dcm, silu=True)

    da, du = lru_scan_bwd(f"{tag}_lru_scan_bwd", s["a"], s["hl"], dhl, batch=batch)

    def lru_elt_bwd(xl, pre, da_, du_, b_ax, lam):
        return jax.vjp(_lru_elt, xl, pre, b_ax, lam)[1]((da_, du_))

    dxl1, dpre, g["b_ax"], g["lam"] = rowwise(
        f"{tag}_lru_elt_bwd", lru_elt_bwd, [s["xl"], s["pre"], da, du], [w["b_ax"], w["lam"]],
        [(LRU_W, F32), (2 * LRU_W, BF16)], [2 * LRU_W, LRU_W])
    g["wax"] = mm(s["xl"], dpre, ta=True, name=f"{tag}_mm_dwax")
    dxl = mm(dpre, w["wax"], tb=True, extras=[dxl1], epilogue=_add_epilogue, name=f"{tag}_mm_dxl")
    dlx_raw, g["cw_l"], g["cb_l"] = cbwd(f"{tag}_conv_l_bwd", proj, OFF_LX, LRU_W, w["cw_l"], w["cb_l"], dxl, silu=False)

    dproj = jnp.concatenate([db_raw, dc_raw, dlx_raw, dlg, dsm, dz, dxs_raw, dq, dk, dv], axis=1)
    return dproj, g


def layer_fwd(h0, p_l, w, batch, tag):
    u1 = rowwise(f"{tag}_rms1", _rms, [h0], [w["g1"]], [(D_MODEL, BF16)])
    proj = mm(u1, w["win"], name=f"{tag}_mm_in")
    ycat, ms = mixer_fwd(proj, w, batch, tag)
    h1 = mm(ycat, w["wout"], extras=[h0], epilogue=_add_epilogue, name=f"{tag}_mm_out")
    u2 = rowwise(f"{tag}_rms2", _rms, [h1], [w["g2"]], [(D_MODEL, BF16)])
    gate, up, act = mm(u2, [w["wg"], w["wu"]], out_dtypes=(BF16, BF16, BF16), epilogue=_swiglu_epilogue,
                       tm=512, tn=D_FF // 2, name=f"{tag}_mm_gu")
    h2 = mm(act, w["wd"], extras=[h1], epilogue=_add_epilogue, tm=512, tk=D_FF, name=f"{tag}_mm_down")
    u3 = rowwise(f"{tag}_rms3", _rms, [h2], [w["g3"]], [(D_MODEL, BF16)])
    pp = mm(p_l, w["wpp"], name=f"{tag}_mm_pp")
    h3, pg = mm(u3, w["wpg"], extras=[pp, h2], col_params=[w["b_pg"]], epilogue=_ple_epilogue,
                out_dtypes=(F32, F32), tm=512, name=f"{tag}_mm_pg")
    saved = dict(h0=h0, u1=u1, proj=proj, ycat=ycat, h1=h1, u2=u2, gate=gate, up=up, act=act, h2=h2, u3=u3, pg=pg,
                 pp=pp, mixer=ms)
    return h3, saved


def _swiglu_epilogue(acc_g, acc_u):
    return acc_g, acc_u, _silu(acc_g) * acc_u


def _swiglu_bwd_epilogue(dact, gate, up):
    return jax.vjp(lambda g_, u_: _silu(g_) * u_, gate.astype(F32), up.astype(F32))[1](dact)


def _ple_epilogue(acc, pp, h2, b):
    return h2 + _ple(acc, pp, b), acc


def _rms_bwd_epilogue(du, h, dres, g):
    dh, dg = jax.vjp(_rms, h, g)[1](du)
    return dh + dres, dg


def layer_bwd(dh3, p_l, w, s, batch, tag):
    def ple_bwd(pg, pp, dh, b):
        return jax.vjp(_ple, pg, pp, b)[1](dh)

    norm_bwd = dict(epilogue=_rms_bwd_epilogue, partials=1, tm=512, tn=D_MODEL, tb=True)

    d_pg, d_pp, g_bpg = rowwise(f"{tag}_ple_bwd", ple_bwd, [s["pg"], s["pp"], dh3], [w["b_pg"]],
                                [(D_MODEL, BF16), (D_MODEL, BF16)], [D_MODEL])
    g = dict(b_pg=g_bpg)
    g["wpp"] = mm(p_l, d_pp, ta=True, name=f"{tag}_mm_dwpp")
    g["wpg"] = mm(s["u3"], d_pg, ta=True, name=f"{tag}_mm_dwpg")
    dh2, dg3 = mm(d_pg, w["wpg"], extras=[s["h2"], dh3], col_params=[w["g3"]], name=f"{tag}_mm_du3", **norm_bwd)
    g["g3"] = sum_slices(f"{tag}_sum_dg3", dg3)

    d_gate, d_up = mm(dh2, w["wd"], tb=True, extras=[s["gate"], s["up"]], epilogue=_swiglu_bwd_epilogue,
                      out_dtypes=(BF16, BF16), tm=512, tn=D_FF // 2, name=f"{tag}_mm_dact")
    g["wd"] = mm(s["act"], dh2, ta=True, name=f"{tag}_mm_dwd")
    g["wg"] = mm(s["u2"], d_gate, ta=True, name=f"{tag}_mm_dwg")
    g["wu"] = mm(s["u2"], d_up, ta=True, name=f"{tag}_mm_dwu")
    dh1, dg2 = mm([d_gate, d_up], [w["wg"], w["wu"]], extras=[s["h1"], dh2], col_params=[w["g2"]],
                  name=f"{tag}_mm_du2", **norm_bwd)
    g["g2"] = sum_slices(f"{tag}_sum_dg2", dg2)

    dycat = mm(dh1, w["wout"], tb=True, name=f"{tag}_mm_dycat")
    g["wout"] = mm(s["ycat"], dh1, ta=True, name=f"{tag}_mm_dwout")
    dproj, gm = mixer_bwd(dycat, s["proj"], w, s["mixer"], batch, tag)
    g.update(gm)
    g["win"] = mm(s["u1"], dproj, ta=True, name=f"{tag}_mm_dwin")
    dh0, dg1 = mm(dproj, w["win"], extras=[s["h0"], dh1], col_params=[w["g1"]], name=f"{tag}_mm_du1", **norm_bwd)
    g["g1"] = sum_slices(f"{tag}_sum_dg1", dg1)
    return dh0, g


def _loss_fwd_bwd(h, tgt, gf):
    def f(h_, gf_):
        e = _rms(h_, gf_) - tgt
        return 0.5 * jnp.sum(jnp.mean(e * e, axis=-1, keepdims=True), axis=0, keepdims=True)

    loss, vj = jax.vjp(f, h, gf)
    dh, dgf = vj(jnp.ones((1, 1), F32))
    return dh, jnp.broadcast_to(loss, (1, LANE)), dgf


def local_step(x, p, tgt, layers, final_g):
    batch, S, _ = x.shape
    T = batch * S
    h = x.reshape(T, D_MODEL)
    saved = []
    for l, w in enumerate(layers):
        h, s = layer_fwd(h, p[l].reshape(T, PLE_DIM), w, batch, f"l{l}")
        saved.append(s)
    dh, loss, dgf = rowwise("loss", _loss_fwd_bwd, [h, tgt.reshape(T, D_MODEL)], [_row(final_g)],
                            [(D_MODEL, F32)], [LANE, D_MODEL], tr=256)
    grads = [None] * len(layers)
    for l in reversed(range(len(layers))):
        dh, grads[l] = layer_bwd(dh, p[l].reshape(T, PLE_DIM), layers[l], saved[l], batch, f"l{l}")
    return loss[0, 0], dh.reshape(batch, S, D_MODEL), grads, dgf[0]


MESH = pl.DeviceIdType.MESH
N_DEV = 8
N_CHIP = 4
ANY = pl.BlockSpec(memory_space=pl.ANY)


def _pos():
    return lax.axis_index("x"), lax.axis_index("y"), lax.axis_index("c")


def _comm_call(body, name, out_shape, n_in, scratch):
    return pl.pallas_call(body, name=name, out_shape=out_shape, in_specs=[ANY] * n_in, out_specs=ANY,
                          scratch_shapes=scratch)


def all_gather8(name, blk):
    def body(x_ref, out_ref, send_sems, recv_sems, local_sem):
        x, y, c = _pos()
        me, sibling = (x, y, c), (x, y, 1 - c)
        chips = [(1 - x, y), (x, 1 - y), (1 - x, 1 - y)]

        def rows(px, py, pcore):
            return out_ref.at[4 * px + 2 * py + pcore]

        def copy(k, block, to, src=None):
            return pltpu.make_async_remote_copy(
                src_ref=rows(*block) if src is None else src, dst_ref=rows(*block),
                send_sem=send_sems.at[k], recv_sem=recv_sems.at[k], device_id=to, device_id_type=MESH)

        mine = pltpu.make_async_copy(x_ref, rows(*me), local_sem)
        mine.start()
        first = [copy(0, me, sibling, src=x_ref)]
        first += [copy(1 + j, me, (*chip, c), src=x_ref) for j, chip in enumerate(chips)]
        for cp in first:
            cp.start()
        passed = [copy(4 + j, (*chip, c), sibling) for j, chip in enumerate(chips)]
        for j, chip in enumerate(chips):
            copy(1 + j, (*chip, c), me).wait_recv()
            passed[j].start()
        copy(0, sibling, me).wait_recv()
        for j, chip in enumerate(chips):
            copy(4 + j, (*chip, 1 - c), me).wait_recv()
        for cp in first + passed:
            cp.wait_send()
        mine.wait()

    return _comm_call(body, name, SDS((N_DEV,) + blk.shape, blk.dtype), 1,
                      [pltpu.SemaphoreType.DMA((7,)), pltpu.SemaphoreType.DMA((7,)), pltpu.SemaphoreType.DMA])(blk)


def sibling_swap(name, v):
    def body(v_ref, out_ref, send_sem, recv_sem):
        x, y, c = _pos()
        cp = pltpu.make_async_remote_copy(src_ref=v_ref, dst_ref=out_ref, send_sem=send_sem, recv_sem=recv_sem,
                                          device_id=(x, y, 1 - c), device_id_type=MESH)
        cp.start()
        cp.wait()

    return _comm_call(body, name, SDS(v.shape, v.dtype), 1, [pltpu.SemaphoreType.DMA, pltpu.SemaphoreType.DMA])(v)


def chip_all_to_all(name, v):
    def body(v_ref, out_ref, send_sems, recv_sems, local_sem):
        x, y, c = _pos()
        my_chip = 2 * x + y
        chips = [(1 - x, y), (x, 1 - y), (1 - x, 1 - y)]
        mine = pltpu.make_async_copy(v_ref.at[my_chip], out_ref.at[my_chip], local_sem)
        mine.start()
        cps = [pltpu.make_async_remote_copy(
            src_ref=v_ref.at[2 * px + py], dst_ref=out_ref.at[my_chip], send_sem=send_sems.at[k],
            recv_sem=recv_sems.at[k], device_id=(px, py, c), device_id_type=MESH) for k, (px, py) in enumerate(chips)]
        for cp in cps:
            cp.start()
        for k, (px, py) in enumerate(chips):
            pltpu.make_async_remote_copy(
                src_ref=v_ref.at[my_chip], dst_ref=out_ref.at[2 * px + py], send_sem=send_sems.at[k],
                recv_sem=recv_sems.at[k], device_id=(px, py, c), device_id_type=MESH).wait_recv()
        for cp in cps:
            cp.wait_send()
        mine.wait()

    return _comm_call(body, name, SDS(v.shape, v.dtype), 1,
                      [pltpu.SemaphoreType.DMA((3,)), pltpu.SemaphoreType.DMA((3,)), pltpu.SemaphoreType.DMA])(v)


def sum_slices(name, v, tr=512):
    n, R, C = v.shape
    tr = _pick(R, (3072, 2048, 1024, tr, 256, 128, 64, 32, 16, 8))

    def body(v_ref, o_ref):
        acc = v_ref[0].astype(F32)
        for k in range(1, n):
            acc = acc + v_ref[k].astype(F32)
        o_ref[...] = acc

    return _pc(body, name=name, grid=(R // tr,), in_specs=[pl.BlockSpec((n, tr, C), lambda i: (0, i, 0))],
               out_specs=pl.BlockSpec((tr, C), lambda i: (i, 0)), out_shape=SDS((R, C), F32), sem=("parallel",))(v)


def add_slices(name, a, b, out_dtype):
    n, R, C = a.shape
    tr = _pick(R, (3072, 2048, 1024, 512, 256, 128, 64, 32, 16, 8))

    def body(a_ref, b_ref, o_ref):
        o_ref[...] = (a_ref[...].astype(F32) + b_ref[...].astype(F32)).astype(o_ref.dtype)

    spec = pl.BlockSpec((1, tr, C), lambda k, i: (k, i, 0))
    return _pc(body, name=name, grid=(n, R // tr), in_specs=[spec, spec], out_specs=spec,
               out_shape=SDS(a.shape, out_dtype), sem=("parallel", "parallel"))(a, b)


def adamw(name, w, g, m, v):
    L, R, C = w.shape
    tr = _pick(R, (256, 128, 64, 32, 16, 8))
    c1 = 1.0 / (1.0 - ADAM_B1 ** ADAM_STEP)
    c2 = 1.0 / (1.0 - ADAM_B2 ** ADAM_STEP)

    def body(w_ref, g_ref, m_ref, v_ref, d_ref, nm_ref, nv_ref):
        gv = g_ref[...]
        nm = ADAM_B1 * m_ref[...] + (1.0 - ADAM_B1) * gv
        nv = ADAM_B2 * v_ref[...] + (1.0 - ADAM_B2) * (gv * gv)
        d_ref[...] = -ADAM_LR * ((nm * c1) / (jnp.sqrt(nv * c2) + ADAM_EPS) + ADAM_WD * w_ref[...])
        nm_ref[...] = nm
        nv_ref[...] = nv

    spec = pl.BlockSpec((1, tr, C), lambda l, i: (l, i, 0))
    return _pc(body, name=name, grid=(L, R // tr), in_specs=[spec] * 4, out_specs=[spec] * 3,
               out_shape=[SDS(w.shape, F32)] * 3, sem=("parallel", "parallel"))(w, g, m, v)


WEIGHTS = ["norm1_g", "w_in", "ssd_conv_w", "ssd_conv_b", "ssd_dt_bias", "ssd_a_log", "ssd_d", "ssd_norm_g",
           "lru_conv_w", "lru_conv_b", "lru_w_a", "lru_b_a", "lru_w_x", "lru_b_x", "lru_lambda", "lru_norm_g",
           "fox_b_f", "fox_norm_g", "w_out", "norm2_g", "w_gate", "w_up", "w_down", "norm3_g", "w_ple_gate",
           "b_ple_gate", "w_ple_proj", "final_norm_g"]
BIG = {"w_in": 2, "w_out": 1, "w_gate": 2, "w_up": 2, "w_down": 1, "w_ple_gate": 1, "w_ple_proj": 2}
SHARDED_SMALL = {"ssd_conv_w": 2, "lru_conv_w": 2}
SMALL = [n for n in WEIGHTS if n not in BIG]


def _pack(arrs, rows_multiple):
    flat = jnp.concatenate([a.reshape(-1) for a in arrs])
    per = rows_multiple * LANE
    n = -(-flat.shape[0] // per) * per
    return jnp.pad(flat, (0, n - flat.shape[0])).reshape(n // LANE, LANE)


def _unpack(flat2d, shapes):
    flat = flat2d.reshape(-1)
    out, off = [], 0
    for s in shapes:
        n = int(np.prod(s))
        out.append(flat[off:off + n].reshape(s))
        off += n
    return out


def _gather_shards(name, shards, axes, dtype):
    c = lax.axis_index("c")
    packed = _pack([s.astype(dtype) for s in shards], 32)
    half = packed.shape[0] // 2
    mine = lax.dynamic_slice_in_dim(packed, c * half, half, 0)
    got = all_gather8(name, mine).reshape(N_CHIP, 2 * half, LANE)
    per_chip = [_unpack(got[k], [s.shape for s in shards]) for k in range(N_CHIP)]
    return [jnp.concatenate([per_chip[k][i] for k in range(N_CHIP)], axis=ax) for i, ax in enumerate(axes)]


def _split_shards(full, ax):
    n = full.shape[ax] // N_CHIP
    return [lax.slice_in_dim(full, k * n, (k + 1) * n, axis=ax) for k in range(N_CHIP)]


def kernel(x, p, norm1_g, w_in, ssd_conv_w, ssd_conv_b, ssd_dt_bias, ssd_a_log, ssd_d, ssd_norm_g, lru_conv_w, lru_conv_b, lru_w_a, lru_b_a, lru_w_x, lru_b_x, lru_lambda, lru_norm_g, fox_b_f, fox_norm_g, w_out, norm2_g, w_gate, w_up, w_down, norm3_g, w_ple_gate, b_ple_gate, w_ple_proj, final_norm_g, loss_target, m_norm1_g, m_w_in, m_ssd_conv_w, m_ssd_conv_b, m_ssd_dt_bias, m_ssd_a_log, m_ssd_d, m_ssd_norm_g, m_lru_conv_w, m_lru_conv_b, m_lru_w_a, m_lru_b_a, m_lru_w_x, m_lru_b_x, m_lru_lambda, m_lru_norm_g, m_fox_b_f, m_fox_norm_g, m_w_out, m_norm2_g, m_w_gate, m_w_up, m_w_down, m_norm3_g, m_w_ple_gate, m_b_ple_gate, m_w_ple_proj, m_final_norm_g, v_norm1_g, v_w_in, v_ssd_conv_w, v_ssd_conv_b, v_ssd_dt_bias, v_ssd_a_log, v_ssd_d, v_ssd_norm_g, v_lru_conv_w, v_lru_conv_b, v_lru_w_a, v_lru_b_a, v_lru_w_x, v_lru_b_x, v_lru_lambda, v_lru_norm_g, v_fox_b_f, v_fox_norm_g, v_w_out, v_norm2_g, v_w_gate, v_w_up, v_w_down, v_norm3_g, v_w_ple_gate, v_b_ple_gate, v_w_ple_proj, v_final_norm_g):
    args = (norm1_g, w_in, ssd_conv_w, ssd_conv_b, ssd_dt_bias, ssd_a_log, ssd_d, ssd_norm_g, lru_conv_w, lru_conv_b, lru_w_a, lru_b_a, lru_w_x, lru_b_x, lru_lambda, lru_norm_g, fox_b_f, fox_norm_g, w_out, norm2_g, w_gate, w_up, w_down, norm3_g, w_ple_gate, b_ple_gate, w_ple_proj, final_norm_g)
    m_args = (m_norm1_g, m_w_in, m_ssd_conv_w, m_ssd_conv_b, m_ssd_dt_bias, m_ssd_a_log, m_ssd_d, m_ssd_norm_g, m_lru_conv_w, m_lru_conv_b, m_lru_w_a, m_lru_b_a, m_lru_w_x, m_lru_b_x, m_lru_lambda, m_lru_norm_g, m_fox_b_f, m_fox_norm_g, m_w_out, m_norm2_g, m_w_gate, m_w_up, m_w_down, m_norm3_g, m_w_ple_gate, m_b_ple_gate, m_w_ple_proj, m_final_norm_g)
    v_args = (v_norm1_g, v_w_in, v_ssd_conv_w, v_ssd_conv_b, v_ssd_dt_bias, v_ssd_a_log, v_ssd_d, v_ssd_norm_g, v_lru_conv_w, v_lru_conv_b, v_lru_w_a, v_lru_b_a, v_lru_w_x, v_lru_b_x, v_lru_lambda, v_lru_norm_g, v_fox_b_f, v_fox_norm_g, v_w_out, v_norm2_g, v_w_gate, v_w_up, v_w_down, v_norm3_g, v_w_ple_gate, v_b_ple_gate, v_w_ple_proj, v_final_norm_g)
    w = dict(zip(WEIGHTS, args))
    mom = dict(zip(WEIGHTS, m_args))
    var = dict(zip(WEIGHTS, v_args))
    xi, yi, ci = _pos()
    chip = 2 * xi + yi

    big_names = list(BIG)
    full = dict(zip(big_names, _gather_shards("gather_big", [w[n] for n in big_names], [BIG[n] for n in big_names], BF16)))
    full.update(zip(SHARDED_SMALL, _gather_shards("gather_conv", [w[n] for n in SHARDED_SMALL],
                                                  list(SHARDED_SMALL.values()), F32)))
    for n in WEIGHTS:
        full.setdefault(n, w[n])
    layers = [prep_layer({n: full[n][l] for n in WEIGHTS if n != "final_norm_g"}) for l in range(DEPTH)]

    loss, grad_x, grads, g_final = local_step(x, p, loss_target, layers, final_norm_g)
    loss = lax.psum(loss, ("x", "y", "c"))
    gl = [unprep_grads(g) for g in grads]
    gfull = {n: jnp.stack([gl[l][n] for l in range(DEPTH)]) for n in WEIGHTS if n != "final_norm_g"}
    gfull["final_norm_g"] = g_final

    small_shapes = [gfull[n].shape for n in SMALL]
    gs = _pack([gfull[n] for n in SMALL], 8)
    gs = sum_slices("sum_small", all_gather8("gather_small_grads", gs))
    gsum = dict(zip(SMALL, _unpack(gs, small_shapes)))
    for n, ax in SHARDED_SMALL.items():
        k = gsum[n].shape[ax] // N_CHIP
        gsum[n] = lax.dynamic_slice_in_dim(gsum[n], chip * k, k, ax)

    per_chip = [_pack([_split_shards(gfull[n], BIG[n])[k] for n in big_names], 1024) for k in range(N_CHIP)]
    half = per_chip[0].shape[0] // 2
    keep = jnp.stack([lax.dynamic_slice_in_dim(g_k, ci * half, half, 0) for g_k in per_chip])
    give = jnp.stack([lax.dynamic_slice_in_dim(g_k, (1 - ci) * half, half, 0).astype(BF16) for g_k in per_chip])
    part = add_slices("add_sibling", keep, sibling_swap("swap_halves", give), BF16)
    mine = sum_slices("sum_chips", chip_all_to_all("a2a_chips", part))
    other = sibling_swap("swap_result", mine)
    lo = jnp.where(ci == 0, mine, other)
    hi = jnp.where(ci == 0, other, mine)
    gsum.update(zip(big_names, _unpack(jnp.concatenate([lo, hi], axis=0), [w[n].shape for n in big_names])))

    delta, new_m, new_v = {}, {}, {}
    for n in big_names:
        delta[n], new_m[n], new_v[n] = adamw(f"adamw_{n}", w[n], gsum[n], mom[n], var[n])
    shapes = [w[n].shape for n in SMALL]
    pk = lambda d: _pack([d[n] for n in SMALL], 8)[None]
    ds, ms, vs = adamw("adamw_small", pk(w), pk(gsum), pk(mom), pk(var))
    for d, packed in ((delta, ds), (new_m, ms), (new_v, vs)):
        d.update(zip(SMALL, _unpack(packed[0], shapes)))

    return (loss, grad_x, *[gsum[n] for n in WEIGHTS], *[delta[n] for n in WEIGHTS],
            *[new_m[n] for n in WEIGHTS], *[new_v[n] for n in WEIGHTS])
```

```python
import functools
import math

import jax
import jax.numpy as jnp
import numpy as np
from jax import lax
from jax.experimental import pallas as pl
from jax.experimental.pallas import tpu as pltpu

F32, BF16 = jnp.float32, jnp.bfloat16
SDS = jax.ShapeDtypeStruct

D_MODEL = 1024
DEPTH = 2
HEAD_DIM = 64
N_HEADS = 6
SSD_W, LRU_W, FOX_W = 384, 256, 384
D_FF = 2816
PLE_DIM = 256
IN_COLS = 2956
EPS = 1e-6
LRU_C = 8.0
LANE = 128
V7X_VMEM_LIMIT = 56 * 1024 * 1024

PW = 3072
OFF_B, OFF_C, OFF_LX, OFF_LG, OFF_SM, OFF_Z, OFF_XS, OFF_Q, OFF_K, OFF_V = (
    0, 256, 512, 768, 1024, 1152, 1536, 1920, 2304, 2688)
FOX_LANE0 = 8

ADAM_LR, ADAM_B1, ADAM_B2, ADAM_EPS, ADAM_WD, ADAM_STEP = 0.001, 0.9, 0.999, 1e-08, 0.01, 10


def _iota(shape, dim):
    return lax.broadcasted_iota(jnp.int32, shape, dim)


def _pc(body, *, name, grid, in_specs, out_specs, out_shape, scratch=(), sem=None):
    return pl.pallas_call(
        body, name=name, grid=grid, in_specs=in_specs, out_specs=out_specs, out_shape=out_shape,
        scratch_shapes=list(scratch),
        compiler_params=pltpu.CompilerParams(dimension_semantics=sem, vmem_limit_bytes=V7X_VMEM_LIMIT))


def permute_in_cols(w):
    z = lambda n: jnp.zeros(w.shape[:-1] + (n,), w.dtype)
    s = lambda a, b: w[..., a:b]
    return jnp.concatenate([
        s(768, 1024), s(1024, 1280), s(1286, 1542), s(1542, 1798),
        s(1280, 1286), z(2), s(2950, 2956), z(LANE - 14),
        s(0, 384), s(384, 768), s(1798, 2182), s(2182, 2566), s(2566, 2950)], axis=-1)


def unpermute_in_cols(g):
    s = lambda a, n: g[..., a:a + n]
    return jnp.concatenate([
        s(OFF_Z, 384), s(OFF_XS, 384), s(OFF_B, 256), s(OFF_C, 256), s(OFF_SM, 6),
        s(OFF_LX, 256), s(OFF_LG, 256), s(OFF_Q, 384), s(OFF_K, 384), s(OFF_V, 384),
        s(OFF_SM + FOX_LANE0, 6)], axis=-1)


def _pick(n, cands):
    for c in cands:
        if n % c == 0:
            return c
    return n


def mm(a, b, *, name, ta=False, tb=False, out_dtypes=(F32,), extras=(), col_params=(), partials=0, epilogue=None,
       tm=None, tn=None, tk=None):
    bs = list(b) if isinstance(b, (list, tuple)) else [b]
    pair_sum = isinstance(a, (list, tuple))
    a_list = list(a) if pair_sum else [a]
    assert not pair_sum or len(a_list) == len(bs)
    a = a_list[0]
    n_a = len(a_list)
    n_acc = 1 if pair_sum else len(bs)
    extras = [e if isinstance(e, tuple) else (e, 0) for e in extras]
    M = a.shape[1] if ta else a.shape[0]
    K = a.shape[0] if ta else a.shape[1]
    N = bs[0].shape[0] if tb else bs[0].shape[1]
    tm = tm or _pick(M, (1024, 1408, 512, 256, 128))
    tn = tn or _pick(N, (1024, 1408, 768, 512, 256, 128))
    tk = tk or _pick(K, (1024, 1408, 512, 256, 128))
    nm, nn, nk = M // tm, N // tn, K // tk
    n_b, n_ex, n_cp, n_out = len(bs), len(extras), len(col_params), len(out_dtypes)
    a_bytes, b_bytes = n_a * M * K * a.dtype.itemsize, n_b * K * N * bs[0].dtype.itemsize
    rows_inner = a_bytes * nn + b_bytes <= a_bytes + b_bytes * nm

    def ij(g0, g1):
        return (g1, g0) if rows_inner else (g0, g1)

    def body(*rest):
        a_refs, rest = rest[:n_a], rest[n_a:]
        b_refs, rest = rest[:n_b], rest[n_b:]
        in_refs, rest = rest[:n_ex + n_cp], rest[n_ex + n_cp:]
        out_refs, accs = rest[:n_out + partials], rest[n_out + partials:]
        dn = (((0 if ta else 1,), (1 if tb else 0,)), ((), ()))
        dot = lambda x_ref, y_ref: lax.dot_general(x_ref[...].astype(BF16), y_ref[...].astype(BF16), dn,
                                                   preferred_element_type=F32)
        if pair_sum:
            parts = [functools.reduce(lambda u, v: u + v, [dot(x, y) for x, y in zip(a_refs, b_refs)])]
        else:
            parts = [dot(a_refs[0], b_ref) for b_ref in b_refs]

        def finish(rs):
            outs = epilogue(*rs, *[e[...] for e in in_refs]) if epilogue is not None else tuple(rs)
            for o_ref, o in zip(out_refs[:n_out], outs):
                o_ref[...] = o.astype(o_ref.dtype)
            for o_ref, o in zip(out_refs[n_out:], outs[n_out:]):
                o_ref[0] = o

        if nk == 1:
            finish(parts)
            return
        k = pl.program_id(2)

        @pl.when(k == 0)
        def _():
            for acc, part in zip(accs, parts):
                acc[...] = part

        @pl.when(k > 0)
        def _():
            for acc, part in zip(accs, parts):
                acc[...] += part

        @pl.when(k == nk - 1)
        def _():
            finish([acc[...] for acc in accs])

    def a_map(g0, g1, k):
        i, _ = ij(g0, g1)
        return (k, i) if ta else (i, k)

    def b_map(g0, g1, k):
        _, j = ij(g0, g1)
        return (j, k) if tb else (k, j)

    def ex_map(off, g0, g1, k):
        i, j = ij(g0, g1)
        return (i, j + off)

    a_spec = pl.BlockSpec((tk, tm) if ta else (tm, tk), a_map)
    b_spec = pl.BlockSpec((tn, tk) if tb else (tk, tn), b_map)
    mn_spec = pl.BlockSpec((tm, tn), functools.partial(ex_map, 0))
    outs = _pc(body, name=name, grid=(nn, nm, nk) if rows_inner else (nm, nn, nk),
               in_specs=([a_spec] * n_a + [b_spec] * n_b
                         + [pl.BlockSpec((tm, tn), functools.partial(ex_map, off)) for _, off in extras]
                         + [pl.BlockSpec((1, tn), lambda g0, g1, k: (0, ij(g0, g1)[1]))] * n_cp),
               out_specs=([mn_spec] * n_out
                          + [pl.BlockSpec((1, 1, tn), lambda g0, g1, k: (ij(g0, g1)[0], 0, ij(g0, g1)[1]))] * partials),
               out_shape=[SDS((M, N), dt) for dt in out_dtypes] + [SDS((nm, 1, N), F32)] * partials,
               scratch=[pltpu.VMEM((tm, tn), F32)] * n_acc if nk > 1 else [],
               sem=("parallel", "parallel", "arbitrary"))(*a_list, *bs, *[e for e, _ in extras], *col_params)
    return outs[0] if len(outs) == 1 else outs


def rowwise(name, fn, rows, params, row_outs, acc_outs=(), tr=512):
    rows = [r if isinstance(r, tuple) else (r, 0, r.shape[1]) for r in rows]
    T = rows[0][0].shape[0]
    tr = min(tr, T)
    n_in, n_ro, n_ac = len(rows) + len(params), len(row_outs), len(acc_outs)

    def body(*refs):
        ins, outs = refs[:n_in], refs[n_in:]
        res = fn(*[r[...] for r in ins])
        if not isinstance(res, (tuple, list)):
            res = (res,)
        for k in range(n_ro):
            outs[k][...] = res[k].astype(outs[k].dtype)
        if n_ac:
            i = pl.program_id(0)

            @pl.when(i == 0)
            def _():
                for k in range(n_ac):
                    outs[n_ro + k][...] = res[n_ro + k]

            @pl.when(i > 0)
            def _():
                for k in range(n_ac):
                    outs[n_ro + k][...] += res[n_ro + k]

    in_specs = ([pl.BlockSpec((tr, w), functools.partial(lambda cb, i: (i, cb), cb)) for (_, cb, w) in rows]
                + [pl.BlockSpec(p.shape, lambda i: (0, 0)) for p in params])
    out_specs = ([pl.BlockSpec((tr, c), lambda i: (i, 0)) for (c, _) in row_outs]
                 + [pl.BlockSpec((1, c), lambda i: (0, 0)) for c in acc_outs])
    out_shape = [SDS((T, c), dt) for (c, dt) in row_outs] + [SDS((1, c), F32) for c in acc_outs]
    outs = _pc(body, name=name, grid=(T // tr,), in_specs=in_specs, out_specs=out_specs, out_shape=out_shape,
               sem=("arbitrary",) if n_ac else ("parallel",))(*[r[0] for r in rows], *params)
    return outs[0] if len(outs) == 1 else outs


def _rms(x, g):
    return x * lax.rsqrt(jnp.mean(x * x, axis=-1, keepdims=True) + EPS) * g


def _softplus(x):
    return jnp.maximum(x, 0.0) + jnp.log(1.0 + jnp.exp(-jnp.abs(x)))


def _silu(x):
    return x * jax.nn.sigmoid(x)


def _gelu(x):
    return 0.5 * x * (1.0 + jnp.tanh(math.sqrt(2.0 / math.pi) * (x + 0.044715 * (x * x * x))))


def _neg_expm1(x):
    series = x * (1 + x / 2 * (1 + x / 3 * (1 + x / 4 * (1 + x / 5 * (1 + x / 6 * (1 + x / 7))))))
    return -jnp.where(jnp.abs(x) < 0.3, series, jnp.exp(x) - 1.0)


def _swiglu(gu):
    return _silu(gu[:, :D_FF]) * gu[:, D_FF:]


def _ple(pg, pp, b):
    return jax.nn.sigmoid(pg + b) * pp


def _ssd_elt(small, xs_act, dtbias_row, alog_row):
    lane = _iota(small.shape, 1)
    dt = _softplus(small + dtbias_row)
    adt = jnp.where(lane < N_HEADS, -jnp.exp(alog_row) * dt, 0.0)
    head = _iota(xs_act.shape, 1) // HEAD_DIM
    dt_exp = jnp.zeros_like(xs_act)
    for h in range(N_HEADS):
        dth = jnp.sum(jnp.where(lane == h, dt, 0.0), axis=1, keepdims=True)
        dt_exp = dt_exp + jnp.where(head == h, dth, 0.0)
    return adt, xs_act * dt_exp


def _fox_elt(small, bf_row):
    lane = _iota(small.shape, 1)
    keep = (lane >= FOX_LANE0) & (lane < FOX_LANE0 + N_HEADS)
    return jnp.where(keep, -_softplus(-(small + bf_row)), 0.0)


def _lru_elt(xl, pre, b_ax, lam):
    r = jax.nn.sigmoid(pre[:, :LRU_W] + b_ax[:, :LRU_W])
    i = jax.nn.sigmoid(pre[:, LRU_W:] + b_ax[:, LRU_W:])
    log_a = -LRU_C * r * _softplus(-lam)
    a = jnp.exp(log_a)
    mult = jnp.sqrt(_neg_expm1(2.0 * log_a))
    return a, mult * (i * xl)


def _mix_post(yraw, xs_act, z, hl, lgate, yfox, dexp, g_ssd, g_lru, g_fox):
    y_ssd = _rms((yraw + xs_act * dexp) * _silu(z), g_ssd)
    y_lru = _rms(hl * _gelu(lgate), g_lru)
    y_fox = _rms(yfox, g_fox)
    return jnp.concatenate([y_ssd, y_lru, y_fox], axis=-1)


def _colsum(x):
    return jnp.sum(x, axis=0, keepdims=True)


def _shift_down(x, d):
    if d == 0:
        return x
    return jnp.where(_iota(x.shape, 0) >= d, pltpu.roll(x, d, 0), 0.0)


def _shift_up(x, d):
    if d == 0:
        return x
    s = x.shape[0]
    return jnp.where(_iota(x.shape, 0) < s - d, pltpu.roll(x, s - d, 0), 0.0)


def _conv_core(x, w, b):
    y = b + w[3:4, :] * x
    for k in range(3):
        y = y + w[k:k + 1, :] * _shift_down(x, 3 - k)
    return y


def seq_conv(name, src, col, width, w8, b, *, batch, silu, out_dtype):
    T = src.shape[0]
    S = T // batch
    c0 = col // LANE

    def body(x_ref, w_ref, b_ref, o_ref):
        y = _conv_core(x_ref[...], w_ref[...], b_ref[...])
        o_ref[...] = (_silu(y) if silu else y).astype(o_ref.dtype)

    return _pc(body, name=name, grid=(batch, width // LANE),
               in_specs=[pl.BlockSpec((S, LANE), lambda bi, ci: (bi, c0 + ci)),
                         pl.BlockSpec((8, LANE), lambda bi, ci: (0, ci)),
                         pl.BlockSpec((1, LANE), lambda bi, ci: (0, ci))],
               out_specs=pl.BlockSpec((S, LANE), lambda bi, ci: (bi, ci)),
               out_shape=SDS((T, width), out_dtype), sem=("parallel", "parallel"))(src, w8, b)


def seq_conv_bwd(name, src, col, width, w8, b, dy, *, batch, silu):
    T = src.shape[0]
    S = T // batch
    c0 = col // LANE

    def body(x_ref, w_ref, b_ref, dy_ref, dx_ref, dw_ref, db_ref):
        x, w = x_ref[...], w_ref[...]
        dpre = dy_ref[...].astype(F32)
        if silu:
            dpre = jax.vjp(_silu, _conv_core(x, w, b_ref[...]))[1](dpre)[0]
        dx = w[3:4, :] * dpre
        for k in range(3):
            dx = dx + w[k:k + 1, :] * _shift_up(dpre, 3 - k)
        dx_ref[...] = dx.astype(dx_ref.dtype)
        row8 = _iota((8, LANE), 0)
        dw = jnp.zeros((8, LANE), F32)
        for k in range(4):
            dw = dw + jnp.where(row8 == k, _colsum(dpre * _shift_down(x, 3 - k)), 0.0)
        db = _colsum(dpre)
        bi = pl.program_id(1)

        @pl.when(bi == 0)
        def _():
            dw_ref[...] = dw
            db_ref[...] = db

        @pl.when(bi > 0)
        def _():
            dw_ref[...] += dw
            db_ref[...] += db

    return _pc(body, name=name, grid=(width // LANE, batch),
               in_specs=[pl.BlockSpec((S, LANE), lambda ci, bi: (bi, c0 + ci)),
                         pl.BlockSpec((8, LANE), lambda ci, bi: (0, ci)),
                         pl.BlockSpec((1, LANE), lambda ci, bi: (0, ci)),
                         pl.BlockSpec((S, LANE), lambda ci, bi: (bi, ci))],
               out_specs=[pl.BlockSpec((S, LANE), lambda ci, bi: (bi, ci)),
                          pl.BlockSpec((8, LANE), lambda ci, bi: (0, ci)),
                          pl.BlockSpec((1, LANE), lambda ci, bi: (0, ci))],
               out_shape=[SDS((T, width), BF16), SDS((8, width), F32), SDS((1, width), F32)],
               sem=("parallel", "arbitrary"))(src, w8, b, dy)


def _split3_dot(tri, x):
    hi = x.astype(BF16)
    r1 = x - hi.astype(F32)
    mid = r1.astype(BF16)
    lo = (r1 - mid.astype(F32)).astype(BF16)
    d = lambda v: jnp.dot(tri, v, preferred_element_type=F32)
    return d(hi) + d(mid) + d(lo)


def seq_cumsum(name, x, *, batch, reverse=False, nsum=1, trow=None):
    T = x.shape[0]
    S = T // batch
    ch = min(256, S)
    nch = S // ch

    def body(x_ref, o_ref, *maybe_t):
        r, c = _iota((ch, ch), 0), _iota((ch, ch), 1)
        tri = jnp.where((c >= r) if reverse else (c <= r), 1.0, 0.0).astype(BF16)
        carry = jnp.zeros((1, LANE), F32)
        for k in (range(nch - 1, -1, -1) if reverse else range(nch)):
            xc = x_ref[k * ch:(k + 1) * ch, 0:LANE]
            for m in range(1, nsum):
                xc = xc + x_ref[k * ch:(k + 1) * ch, m * LANE:(m + 1) * LANE]
            o_ref[k * ch:(k + 1) * ch, :] = _split3_dot(tri, xc) + carry
            carry = carry + _colsum(xc)
        if trow is not None:
            maybe_t[0][...] = o_ref[...].T[trow:trow + 8, :]

    out_specs = [pl.BlockSpec((S, LANE), lambda bi: (bi, 0))]
    out_shape = [SDS((T, LANE), F32)]
    if trow is not None:
        out_specs.append(pl.BlockSpec((8, S), lambda bi: (bi, 0)))
        out_shape.append(SDS((batch * 8, S), F32))
    outs = _pc(body, name=name, grid=(batch,), in_specs=[pl.BlockSpec((S, LANE * nsum), lambda bi: (bi, 0))],
               out_specs=out_specs, out_shape=out_shape, sem=("parallel",))(x)
    return outs if trow is not None else outs[0]


def lru_scan(name, a, u, *, batch):
    T, W = a.shape
    S = T // batch

    def body(a_ref, u_ref, h_ref):
        row = _iota((8, W), 0)

        def step(g, h):
            off = pl.multiple_of(g * 8, 8)
            at, ut = a_ref[pl.ds(off, 8), :], u_ref[pl.ds(off, 8), :]
            acc = jnp.zeros((8, W), F32)
            for r in range(8):
                h = at[r:r + 1, :] * h + ut[r:r + 1, :]
                acc = jnp.where(row == r, jnp.broadcast_to(h, (8, W)), acc)
            h_ref[pl.ds(off, 8), :] = acc
            return h

        lax.fori_loop(0, S // 8, step, jnp.zeros((1, W), F32))

    spec = pl.BlockSpec((S, W), lambda bi: (bi, 0))
    return _pc(body, name=name, grid=(batch,), in_specs=[spec, spec], out_specs=spec,
               out_shape=SDS((T, W), F32), sem=("parallel",))(a, u)


def lru_scan_bwd(name, a, h, dh, *, batch):
    T, W = a.shape
    S = T // batch
    ng = S // 8

    def body(a_ref, h_ref, dh_ref, da_ref, du_ref):
        row = _iota((8, W), 0)

        def step(k, c):
            g_idx = ng - 1 - k
            off = pl.multiple_of(g_idx * 8, 8)
            offp = pl.multiple_of(jnp.maximum(g_idx - 1, 0) * 8, 8)
            at, ht, dt = a_ref[pl.ds(off, 8), :], h_ref[pl.ds(off, 8), :], dh_ref[pl.ds(off, 8), :]
            hp = jnp.where(g_idx > 0, h_ref[pl.ds(offp, 8), :], 0.0)
            da = jnp.zeros((8, W), F32)
            du = jnp.zeros((8, W), F32)
            for r in range(7, -1, -1):
                g = dt[r:r + 1, :] + c
                hprev = ht[r - 1:r, :] if r > 0 else hp[7:8, :]
                du = jnp.where(row == r, jnp.broadcast_to(g, (8, W)), du)
                da = jnp.where(row == r, jnp.broadcast_to(g * hprev, (8, W)), da)
                c = at[r:r + 1, :] * g
            da_ref[pl.ds(off, 8), :] = da
            du_ref[pl.ds(off, 8), :] = du
            return c

        lax.fori_loop(0, ng, step, jnp.zeros((1, W), F32))

    spec = pl.BlockSpec((S, W), lambda bi: (bi, 0))
    return _pc(body, name=name, grid=(batch,), in_specs=[spec] * 3, out_specs=[spec] * 2,
               out_shape=[SDS((T, W), F32)] * 2, sem=("parallel",))(a, h, dh)


def _nt(a, b):
    return lax.dot_general(a, b, (((1,), (1,)), ((), ())), preferred_element_type=F32)


def _tn(a, b):
    return lax.dot_general(a, b, (((0,), (0,)), ((), ())), preferred_element_type=F32)


def _tile(S, t=256):
    return min(t, S)


def ssd_attn_fwd(name, cm, bm, xd, cum, cum_t, *, batch):
    T = cm.shape[0]
    S = T // batch
    tq = tk = _tile(S)
    nq = S // tq

    def body(c_ref, b_ref, x_ref, cum_ref, cumt_ref, y_ref):
        i = pl.program_id(1)
        cq, cmq = cum_ref[...], c_ref[...]
        rowi, coli = _iota((tq, tk), 0), _iota((tq, tk), 1)
        half = _iota((tk, LANE), 1) // HEAD_DIM

        def step(j, accs, diag):
            off = pl.multiple_of(j * tk, tk)
            bj = b_ref[pl.ds(off, tk), :]
            gm = [_nt(cmq[:, g * LANE:(g + 1) * LANE], bj[:, g * LANE:(g + 1) * LANE]) for g in range(2)]
            ckt = cumt_ref[:, pl.ds(off, tk)]
            new = []
            for p in range(3):
                xp = x_ref[pl.ds(off, tk), p * LANE:(p + 1) * LANE]
                ws, xs = [], []
                for hh in range(2):
                    h = 2 * p + hh
                    seg = cq[:, h:h + 1] - ckt[h:h + 1, :]
                    e = jnp.exp(jnp.where(rowi >= coli, seg, -jnp.inf) if diag else seg)
                    ws.append((gm[h // 3] * e).astype(BF16))
                    xs.append(jnp.where(half == hh, xp, jnp.zeros_like(xp)))
                new.append(accs[p] + jnp.dot(jnp.concatenate(ws, axis=1), jnp.concatenate(xs, axis=0),
                                             preferred_element_type=F32))
            return tuple(new)

        accs = lax.fori_loop(0, i, functools.partial(step, diag=False),
                             tuple(jnp.zeros((tq, LANE), F32) for _ in range(3)))
        accs = step(i, accs, True)
        y_ref[...] = jnp.concatenate(accs, axis=1)

    return _pc(body, name=name, grid=(batch, nq),
               in_specs=[pl.BlockSpec((tq, 256), lambda b, i: (b * nq + i, 0)),
                         pl.BlockSpec((S, 256), lambda b, i: (b, 0)),
                         pl.BlockSpec((S, SSD_W), lambda b, i: (b, 0)),
                         pl.BlockSpec((tq, LANE), lambda b, i: (b * nq + i, 0)),
                         pl.BlockSpec((8, S), lambda b, i: (b, 0))],
               out_specs=pl.BlockSpec((tq, SSD_W), lambda b, i: (b * nq + i, 0)),
               out_shape=SDS((T, SSD_W), F32), sem=("parallel", "parallel"))(cm, bm, xd, cum, cum_t)


def ssd_attn_bwd(name, cm, bm, xd, cum, cum_t, dy, *, batch):
    T = cm.shape[0]
    S = T // batch
    tq = tk = _tile(S, 512)
    nq = S // tq

    def body(c_ref, b_ref, x_ref, cum_ref, cumt_ref, dy_ref, dx_ref, db_ref, dc_ref, dcum_ref, dcumt_ref):
        dx_ref[...] = jnp.zeros_like(dx_ref)
        db_ref[...] = jnp.zeros_like(db_ref)
        dcum_ref[...] = jnp.zeros_like(dcum_ref)
        dcumt_ref[...] = jnp.zeros_like(dcumt_ref)
        rowi, coli = _iota((tq, tk), 0), _iota((tq, tk), 1)
        halfq = _iota((tq, LANE), 1) // HEAD_DIM
        lane_q = _iota((tq, LANE), 1)

        def qblock(i, _):
            qoff = pl.multiple_of(i * tq, tq)
            cq = cum_ref[pl.ds(qoff, tq), :]
            cmq = c_ref[pl.ds(qoff, tq), :]
            dyq = dy_ref[pl.ds(qoff, tq), :]
            dyh = [[jnp.where(halfq == hh, dyq[:, p * LANE:(p + 1) * LANE], 0.0).astype(BF16) for hh in range(2)]
                   for p in range(3)]

            def step(j, carry, diag):
                dcq, rs_acc = carry
                off = pl.multiple_of(j * tk, tk)
                bj = b_ref[pl.ds(off, tk), :]
                gm = [_nt(cmq[:, g * LANE:(g + 1) * LANE], bj[:, g * LANE:(g + 1) * LANE]) for g in range(2)]
                ckt = cumt_ref[:, pl.ds(off, tk)]
                dgm = [jnp.zeros((tq, tk), F32), jnp.zeros((tq, tk), F32)]
                for p in range(3):
                    xp = x_ref[pl.ds(off, tk), p * LANE:(p + 1) * LANE]
                    ws = []
                    for hh in range(2):
                        h = 2 * p + hh
                        seg = cq[:, h:h + 1] - ckt[h:h + 1, :]
                        e = jnp.exp(jnp.where(rowi >= coli, seg, -jnp.inf) if diag else seg)
                        w = gm[h // 3] * e
                        dw = _nt(dyh[p][hh], xp)
                        zz = dw * w
                        rs_acc = rs_acc + jnp.where(lane_q == h, jnp.sum(zz, axis=1, keepdims=True), 0.0)
                        dcumt_ref[h:h + 1, pl.ds(off, tk)] += _colsum(zz)
                        dgm[h // 3] = dgm[h // 3] + dw * e
                        ws.append(w.astype(BF16))
                    dx_ref[pl.ds(off, tk), p * LANE:(p + 1) * LANE] += _tn(
                        jnp.concatenate(ws, axis=0), jnp.concatenate(dyh[p], axis=0))
                new_dcq = []
                for g in range(2):
                    dg = dgm[g].astype(BF16)
                    new_dcq.append(dcq[g] + jnp.dot(dg, bj[:, g * LANE:(g + 1) * LANE], preferred_element_type=F32))
                    db_ref[pl.ds(off, tk), g * LANE:(g + 1) * LANE] += _tn(dg, cmq[:, g * LANE:(g + 1) * LANE])
                return tuple(new_dcq), rs_acc

            carry = lax.fori_loop(
                0, i, functools.partial(step, diag=False),
                ((jnp.zeros((tq, LANE), F32), jnp.zeros((tq, LANE), F32)), jnp.zeros((tq, LANE), F32)))
            dcq, rs_acc = step(i, carry, True)
            dc_ref[pl.ds(qoff, tq), :] = jnp.concatenate(dcq, axis=1)
            dcum_ref[pl.ds(qoff, tq), :] += rs_acc
            return 0

        lax.fori_loop(0, nq, qblock, 0)
        dcum_ref[...] = dcum_ref[...] - dcumt_ref[...].T

    s256 = pl.BlockSpec((S, 256), lambda b: (b, 0))
    s384 = pl.BlockSpec((S, SSD_W), lambda b: (b, 0))
    s128 = pl.BlockSpec((S, LANE), lambda b: (b, 0))
    return _pc(body, name=name, grid=(batch,),
               in_specs=[s256, s256, s384, s128, pl.BlockSpec((8, S), lambda b: (b, 0)), s384],
               out_specs=[s384, s256, s256, s128],
               out_shape=[SDS((T, SSD_W), F32), SDS((T, 256), F32), SDS((T, 256), F32), SDS((T, LANE), F32)],
               scratch=[pltpu.VMEM((LANE, S), F32)], sem=("parallel",))(cm, bm, xd, cum, cum_t, dy)


NEG_BIG = -1e30


def fox_attn_fwd(name, proj, cum, cum_t, *, batch):
    T = proj.shape[0]
    S = T // batch
    tq = tk = _tile(S, 512)
    nq = S // tq
    scale = HEAD_DIM ** -0.5
    qb, kb, vb = OFF_Q // LANE, OFF_K // LANE, OFF_V // LANE

    def body(q_ref, k_ref, v_ref, cum_ref, cumt_ref, o_ref, lse_ref):
        p, i = pl.program_id(1), pl.program_id(2)
        cq = cum_ref[...]
        lane_q = _iota((tq, LANE), 1)
        halfq, halfk = lane_q // HEAD_DIM, _iota((tk, LANE), 1) // HEAD_DIM
        qs = q_ref[...] * scale
        qh = [jnp.where(halfq == hh, qs, 0.0).astype(BF16) for hh in range(2)]
        rowi, coli = _iota((tq, tk), 0), _iota((tq, tk), 1)
        cqh = [jnp.sum(jnp.where(lane_q == FOX_LANE0 + 2 * p + hh, cq, 0.0), axis=1, keepdims=True) for hh in range(2)]
        row8 = _iota((8, tk), 0)

        def step(j, carry, diag):
            ms, ls, acc = carry
            off = pl.multiple_of(j * tk, tk)
            kj = k_ref[pl.ds(off, tk), :].astype(BF16)
            vj = v_ref[pl.ds(off, tk), :].astype(BF16)
            ckt = cumt_ref[:, pl.ds(off, tk)]
            ps, vs, new_m, new_l, alphas = [], [], [], [], []
            for hh in range(2):
                ck = jnp.sum(jnp.where(row8 == 2 * p + hh, ckt, 0.0), axis=0, keepdims=True)
                logits = _nt(qh[hh], kj) + (cqh[hh] - ck)
                if diag:
                    logits = jnp.where(rowi >= coli, logits, -jnp.inf)
                m = jnp.maximum(ms[hh], jnp.max(logits, axis=1, keepdims=True))
                alpha = jnp.exp(ms[hh] - m)
                pr = jnp.exp(logits - m)
                new_m.append(m)
                new_l.append(alpha * ls[hh] + jnp.sum(pr, axis=1, keepdims=True))
                alphas.append(alpha)
                ps.append(pr.astype(BF16))
                vs.append(jnp.where(halfk == hh, vj, jnp.zeros_like(vj)))
            acc = acc * jnp.where(halfq == 0, alphas[0], alphas[1]) + jnp.dot(
                jnp.concatenate(ps, axis=1), jnp.concatenate(vs, axis=0), preferred_element_type=F32)
            return tuple(new_m), tuple(new_l), acc

        init = ((jnp.full((tq, 1), NEG_BIG, F32),) * 2, (jnp.zeros((tq, 1), F32),) * 2, jnp.zeros((tq, LANE), F32))
        ms, ls, acc = step(i, lax.fori_loop(0, i, functools.partial(step, diag=False), init), True)
        o_ref[...] = acc / jnp.where(halfq == 0, ls[0], ls[1])
        lse_ref[...] = (jnp.where(lane_q == 0, ms[0] + jnp.log(ls[0]), 0.0)
                        + jnp.where(lane_q == 1, ms[1] + jnp.log(ls[1]), 0.0))

    return _pc(body, name=name, grid=(batch, 3, nq),
               in_specs=[pl.BlockSpec((tq, LANE), lambda b, p, i: (b * nq + i, qb + p)),
                         pl.BlockSpec((S, LANE), lambda b, p, i: (b, kb + p)),
                         pl.BlockSpec((S, LANE), lambda b, p, i: (b, vb + p)),
                         pl.BlockSpec((tq, LANE), lambda b, p, i: (b * nq + i, 0)),
                         pl.BlockSpec((8, S), lambda b, p, i: (b, 0))],
               out_specs=[pl.BlockSpec((tq, LANE), lambda b, p, i: (b * nq + i, p))] * 2,
               out_shape=[SDS((T, FOX_W), F32)] * 2, sem=("parallel", "parallel", "parallel"))(proj, proj, proj, cum, cum_t)


def fox_attn_bwd(name, proj, o, do, lse, cum, cum_t, *, batch):
    T = proj.shape[0]
    S = T // batch
    tq = tk = _tile(S, 512)
    nq = S // tq
    scale = HEAD_DIM ** -0.5
    qb, kb, vb = OFF_Q // LANE, OFF_K // LANE, OFF_V // LANE

    def body(q_ref, k_ref, v_ref, o_ref, do_ref, lse_ref, cum_ref, cumt_ref,
             dq_ref, dk_ref, dv_ref, dcum_ref, dk_acc, dv_acc, dcumt_ref):
        p = pl.program_id(1)
        dk_acc[...] = jnp.zeros_like(dk_acc)
        dv_acc[...] = jnp.zeros_like(dv_acc)
        dcum_ref[...] = jnp.zeros_like(dcum_ref)
        dcumt_ref[...] = jnp.zeros_like(dcumt_ref)
        lane_q = _iota((tq, LANE), 1)
        halfq, halfk = lane_q // HEAD_DIM, _iota((tk, LANE), 1) // HEAD_DIM
        rowi, coli = _iota((tq, tk), 0), _iota((tq, tk), 1)
        row8 = _iota((8, tk), 0)

        def qblock(i, _):
            qoff = pl.multiple_of(i * tq, tq)
            cq = cum_ref[pl.ds(qoff, tq), :]
            qs = q_ref[pl.ds(qoff, tq), :] * scale
            doq = do_ref[pl.ds(qoff, tq), :]
            lse = lse_ref[pl.ds(qoff, tq), :]
            delta = doq * o_ref[pl.ds(qoff, tq), :]
            qh, doh, cqh, lseh, dlt = [], [], [], [], []
            for hh in range(2):
                qh.append(jnp.where(halfq == hh, qs, 0.0).astype(BF16))
                doh.append(jnp.where(halfq == hh, doq, 0.0).astype(BF16))
                cqh.append(jnp.sum(jnp.where(lane_q == FOX_LANE0 + 2 * p + hh, cq, 0.0), axis=1, keepdims=True))
                lseh.append(jnp.sum(jnp.where(lane_q == hh, lse, 0.0), axis=1, keepdims=True))
                dlt.append(jnp.sum(jnp.where(halfq == hh, delta, 0.0), axis=1, keepdims=True))

            def step(j, carry, diag):
                dq, rs = carry
                off = pl.multiple_of(j * tk, tk)
                kj = k_ref[pl.ds(off, tk), :].astype(BF16)
                vj = v_ref[pl.ds(off, tk), :].astype(BF16)
                ckt = cumt_ref[:, pl.ds(off, tk)]
                dss, prs, ks = [], [], []
                for hh in range(2):
                    ck = jnp.sum(jnp.where(row8 == 2 * p + hh, ckt, 0.0), axis=0, keepdims=True)
                    logits = _nt(qh[hh], kj) + ((cqh[hh] - lseh[hh]) - ck)
                    if diag:
                        logits = jnp.where(rowi >= coli, logits, -jnp.inf)
                    pr = jnp.exp(logits)
                    ds = pr * (_nt(doh[hh], vj) - dlt[hh])
                    rs = rs + jnp.where(lane_q == FOX_LANE0 + 2 * p + hh, jnp.sum(ds, axis=1, keepdims=True), 0.0)
                    cs = _colsum(ds)
                    dcumt_ref[0:8, pl.ds(off, tk)] += jnp.where(row8 == 2 * p + hh, cs, 0.0)
                    dss.append(ds.astype(BF16))
                    prs.append(pr.astype(BF16))
                    ks.append(jnp.where(halfk == hh, kj, jnp.zeros_like(kj)))
                dq = dq + jnp.dot(jnp.concatenate(dss, axis=1), jnp.concatenate(ks, axis=0), preferred_element_type=F32)
                dk_acc[pl.ds(off, tk), :] += _tn(jnp.concatenate(dss, axis=0), jnp.concatenate(qh, axis=0))
                dv_acc[pl.ds(off, tk), :] += _tn(jnp.concatenate(prs, axis=0), jnp.concatenate(doh, axis=0))
                return dq, rs

            carry = lax.fori_loop(0, i, functools.partial(step, diag=False),
                                  (jnp.zeros((tq, LANE), F32), jnp.zeros((tq, LANE), F32)))
            dq, rs = step(i, carry, True)
            dq_ref[pl.ds(qoff, tq), :] = (dq * scale).astype(dq_ref.dtype)
            dcum_ref[pl.ds(qoff, tq), :] += rs
            return 0

        lax.fori_loop(0, nq, qblock, 0)
        dk_ref[...] = dk_acc[...].astype(dk_ref.dtype)
        dv_ref[...] = dv_acc[...].astype(dv_ref.dtype)
        dct = dcumt_ref[...].T
        dcum_ref[...] = dcum_ref[...] - pltpu.roll(dct, FOX_LANE0, 1)

    sp = lambda c0: pl.BlockSpec((S, LANE), lambda b, p: (b, c0 + p))
    s0 = pl.BlockSpec((S, LANE), lambda b, p: (b, 0))
    return _pc(body, name=name, grid=(batch, 3),
               in_specs=[sp(qb), sp(kb), sp(vb), sp(0), sp(0), sp(0), s0, pl.BlockSpec((8, S), lambda b, p: (b, 0))],
               out_specs=[sp(0)] * 4,
               out_shape=[SDS((T, FOX_W), BF16)] * 3 + [SDS((T, FOX_W), F32)],
               scratch=[pltpu.VMEM((S, LANE), F32), pltpu.VMEM((S, LANE), F32), pltpu.VMEM((LANE, S), F32)],
               sem=("parallel", "parallel"))(proj, proj, proj, o, do, lse, cum, cum_t)


def _row(v, width=None, at=0):
    v = v.astype(F32)
    width = width or v.shape[0]
    return jnp.pad(v, (at, width - at - v.shape[0]))[None, :]


def _pad8(w4):
    return jnp.pad(w4.astype(F32), ((0, 4), (0, 0)))


def _block_diag(w):
    out = jnp.zeros((LRU_W, LRU_W), w.dtype)
    for g in range(4):
        out = lax.dynamic_update_slice(out, w[g], (g * 64, g * 64))
    return out


def prep_layer(f):
    cw, cb = f["ssd_conv_w"], f["ssd_conv_b"]
    return dict(
        win=permute_in_cols(f["w_in"]), wout=f["w_out"], wg=f["w_gate"], wu=f["w_up"], wd=f["w_down"], wpg=f["w_ple_gate"], wpp=f["w_ple_proj"],
        wax=jnp.concatenate([_block_diag(f["lru_w_a"]), _block_diag(f["lru_w_x"])], axis=1),
        g1=_row(f["norm1_g"]), g2=_row(f["norm2_g"]), g3=_row(f["norm3_g"]),
        cw_xs=_pad8(cw[:, :384]), cb_xs=_row(cb[:384]), cw_b=_pad8(cw[:, 384:640]), cb_b=_row(cb[384:640]),
        cw_c=_pad8(cw[:, 640:]), cb_c=_row(cb[640:]), cw_l=_pad8(f["lru_conv_w"]), cb_l=_row(f["lru_conv_b"]),
        dtbias_row=_row(f["ssd_dt_bias"], LANE), alog_row=_row(f["ssd_a_log"], LANE),
        dexp=jnp.repeat(f["ssd_d"].astype(F32), HEAD_DIM)[None, :], g_ssd=_row(f["ssd_norm_g"]),
        b_ax=_row(jnp.concatenate([f["lru_b_a"], f["lru_b_x"]])), lam=_row(f["lru_lambda"]), g_lru=_row(f["lru_norm_g"]),
        bf_row=_row(f["fox_b_f"], LANE, FOX_LANE0), g_fox=_row(f["fox_norm_g"]), b_pg=_row(f["b_ple_gate"]))


def unprep_grads(g):
    blocks = lambda m: jnp.stack([m[i * 64:(i + 1) * 64, i * 64:(i + 1) * 64] for i in range(4)])
    return dict(
        norm1_g=g["g1"][0], w_in=unpermute_in_cols(g["win"]),
        ssd_conv_w=jnp.concatenate([g["cw_xs"][:4], g["cw_b"][:4], g["cw_c"][:4]], axis=1),
        ssd_conv_b=jnp.concatenate([g["cb_xs"][0], g["cb_b"][0], g["cb_c"][0]]),
        ssd_dt_bias=g["dtbias_row"][0, :N_HEADS], ssd_a_log=g["alog_row"][0, :N_HEADS],
        ssd_d=jnp.sum(g["dexp"].reshape(N_HEADS, HEAD_DIM), axis=1), ssd_norm_g=g["g_ssd"][0],
        lru_conv_w=g["cw_l"][:4], lru_conv_b=g["cb_l"][0],
        lru_w_a=blocks(g["wax"][:, :LRU_W]), lru_b_a=g["b_ax"][0, :LRU_W],
        lru_w_x=blocks(g["wax"][:, LRU_W:]), lru_b_x=g["b_ax"][0, LRU_W:],
        lru_lambda=g["lam"][0], lru_norm_g=g["g_lru"][0],
        fox_b_f=g["bf_row"][0, FOX_LANE0:FOX_LANE0 + N_HEADS], fox_norm_g=g["g_fox"][0],
        w_out=g["wout"], norm2_g=g["g2"][0], w_gate=g["wg"], w_up=g["wu"], w_down=g["wd"],
        norm3_g=g["g3"][0], w_ple_gate=g["wpg"], b_ple_gate=g["b_pg"][0], w_ple_proj=g["wpp"])


def _view(a, off, width):
    return (a, off // width, width)


def _add_epilogue(acc, e):
    return (acc + e,)


def mixer_fwd(proj, w, batch, tag):
    sm = _view(proj, OFF_SM, LANE)
    conv = functools.partial(seq_conv, batch=batch)
    cmc = conv(f"{tag}_conv_c", proj, OFF_C, 256, w["cw_c"], w["cb_c"], silu=True, out_dtype=BF16)
    bmc = conv(f"{tag}_conv_b", proj, OFF_B, 256, w["cw_b"], w["cb_b"], silu=True, out_dtype=BF16)
    xs_act = conv(f"{tag}_conv_xs", proj, OFF_XS, SSD_W, w["cw_xs"], w["cb_xs"], silu=True, out_dtype=F32)
    xl = conv(f"{tag}_conv_l", proj, OFF_LX, LRU_W, w["cw_l"], w["cb_l"], silu=False, out_dtype=F32)
    adt, xd = rowwise(f"{tag}_ssd_elt", _ssd_elt, [sm, xs_act], [w["dtbias_row"], w["alog_row"]],
                      [(LANE, F32), (SSD_W, BF16)])
    cum_a, cum_at = seq_cumsum(f"{tag}_cum_a", adt, batch=batch, trow=0)
    yraw = ssd_attn_fwd(f"{tag}_ssd_fwd", cmc, bmc, xd, cum_a, cum_at, batch=batch)
    logf = rowwise(f"{tag}_fox_elt", _fox_elt, [sm], [w["bf_row"]], [(LANE, F32)])
    cum_f, cum_ft = seq_cumsum(f"{tag}_cum_f", logf, batch=batch, trow=FOX_LANE0)
    o, lse = fox_attn_fwd(f"{tag}_fox_fwd", proj, cum_f, cum_ft, batch=batch)
    pre = mm(xl, w["wax"], name=f"{tag}_mm_lru_gates")
    a, u = rowwise(f"{tag}_lru_elt", _lru_elt, [xl, pre], [w["b_ax"], w["lam"]], [(LRU_W, F32), (LRU_W, F32)])
    hl = lru_scan(f"{tag}_lru_scan", a, u, batch=batch)
    ycat = rowwise(f"{tag}_mix_post", _mix_post,
                   [yraw, xs_act, _view(proj, OFF_Z, SSD_W), hl, _view(proj, OFF_LG, LRU_W), o],
                   [w["dexp"], w["g_ssd"], w["g_lru"], w["g_fox"]], [(D_MODEL, BF16)], tr=256)
    saved = dict(cmc=cmc, bmc=bmc, xs_act=xs_act, xl=xl, xd=xd, cum_a=cum_a, cum_at=cum_at, yraw=yraw,
                 cum_f=cum_f, cum_ft=cum_ft, o=o, lse=lse, pre=pre, a=a, hl=hl)
    return ycat, saved


def mixer_bwd(dycat, proj, w, s, batch, tag):
    sm = _view(proj, OFF_SM, LANE)
    g = {}

    def post_bwd(yraw, xs_act, z, hl, lg, o, dyc, dexp, g_ssd, g_lru, g_fox):
        return jax.vjp(_mix_post, yraw, xs_act, z, hl, lg, o, dexp, g_ssd, g_lru, g_fox)[1](dyc)

    (dyraw, dxs1, dz, dhl, dlg, do, g["dexp"], g["g_ssd"], g["g_lru"], g["g_fox"]) = rowwise(
        f"{tag}_mix_post_bwd", post_bwd,
        [s["yraw"], s["xs_act"], _view(proj, OFF_Z, SSD_W), s["hl"], _view(proj, OFF_LG, LRU_W), s["o"], dycat],
        [w["dexp"], w["g_ssd"], w["g_lru"], w["g_fox"]],
        [(SSD_W, F32), (SSD_W, F32), (SSD_W, BF16), (LRU_W, F32), (LRU_W, BF16), (FOX_W, F32)],
        [SSD_W, SSD_W, LRU_W, FOX_W], tr=256)

    dq, dk, dv, dcum3 = fox_attn_bwd(f"{tag}_fox_bwd", proj, s["o"], do, s["lse"], s["cum_f"], s["cum_ft"], batch=batch)
    dlogf = seq_cumsum(f"{tag}_rcum_f", dcum3, batch=batch, reverse=True, nsum=3)

    dxd, dbm, dcm, dcum_a = ssd_attn_bwd(f"{tag}_ssd_bwd", s["cmc"], s["bmc"], s["xd"], s["cum_a"], s["cum_at"], dyraw,
                                         batch=batch)
    dadt = seq_cumsum(f"{tag}_rcum_a", dcum_a, batch=batch, reverse=True)

    def ssd_elt_bwd(small, xs_act, dadt_, dxd_, dxs1_, dtbias, alog):
        dsm, dxs, ddtb, dalog = jax.vjp(_ssd_elt, small, xs_act, dtbias, alog)[1]((dadt_, dxd_))
        return dsm, dxs + dxs1_, ddtb, dalog

    dsm_s, dxs_act, g["dtbias_row"], g["alog_row"] = rowwise(
        f"{tag}_ssd_elt_bwd", ssd_elt_bwd, [sm, s["xs_act"], dadt, dxd, dxs1], [w["dtbias_row"], w["alog_row"]],
        [(LANE, F32), (SSD_W, F32)], [LANE, LANE])

    def fox_elt_bwd(small, dlogf_, dsm_s_, bf_row):
        dsm, dbf = jax.vjp(_fox_elt, small, bf_row)[1](dlogf_)
        return dsm + dsm_s_, dbf

    dsm, g["bf_row"] = rowwise(f"{tag}_fox_elt_bwd", fox_elt_bwd, [sm, dlogf, dsm_s], [w["bf_row"]],
                               [(LANE, BF16)], [LANE])

    cbwd = functools.partial(seq_conv_bwd, batch=batch)
    dxs_raw, g["cw_xs"], g["cb_xs"] = cbwd(f"{tag}_conv_xs_bwd", proj, OFF_XS, SSD_W, w["cw_xs"], w["cb_xs"], dxs_act, silu=True)
    db_raw, g["cw_b"], g["cb_b"] = cbwd(f"{tag}_conv_b_bwd", proj, OFF_B, 256, w["cw_b"], w["cb_b"], dbm, silu=True)
    dc_raw, g["cw_c"], g["cb_c"] = cbwd(f"{tag}_conv_c_bwd", proj, OFF_C, 256, w["cw_c"], w["cb_c"], dcm, silu=True)

    da, du = lru_scan_bwd(f"{tag}_lru_scan_bwd", s["a"], s["hl"], dhl, batch=batch)

    def lru_elt_bwd(xl, pre, da_, du_, b_ax, lam):
        return jax.vjp(_lru_elt, xl, pre, b_ax, lam)[1]((da_, du_))

    dxl1, dpre, g["b_ax"], g["lam"] = rowwise(
        f"{tag}_lru_elt_bwd", lru_elt_bwd, [s["xl"], s["pre"], da, du], [w["b_ax"], w["lam"]],
        [(LRU_W, F32), (2 * LRU_W, BF16)], [2 * LRU_W, LRU_W])
    g["wax"] = mm(s["xl"], dpre, ta=True, name=f"{tag}_mm_dwax")
    dxl = mm(dpre, w["wax"], tb=True, extras=[dxl1], epilogue=_add_epilogue, name=f"{tag}_mm_dxl")
    dlx_raw, g["cw_l"], g["cb_l"] = cbwd(f"{tag}_conv_l_bwd", proj, OFF_LX, LRU_W, w["cw_l"], w["cb_l"], dxl, silu=False)

    dproj = jnp.concatenate([db_raw, dc_raw, dlx_raw, dlg, dsm, dz, dxs_raw, dq, dk, dv], axis=1)
    return dproj, g


def layer_fwd(h0, p_l, w, batch, tag):
    u1 = rowwise(f"{tag}_rms1", _rms, [h0], [w["g1"]], [(D_MODEL, BF16)])
    proj = mm(u1, w["win"], name=f"{tag}_mm_in")
    ycat, ms = mixer_fwd(proj, w, batch, tag)
    h1 = mm(ycat, w["wout"], extras=[h0], epilogue=_add_epilogue, name=f"{tag}_mm_out")
    u2 = rowwise(f"{tag}_rms2", _rms, [h1], [w["g2"]], [(D_MODEL, BF16)])
    gate, up, act = mm(u2, [w["wg"], w["wu"]], out_dtypes=(BF16, BF16, BF16), epilogue=_swiglu_epilogue,
                       tm=512, tn=D_FF // 2, name=f"{tag}_mm_gu")
    h2 = mm(act, w["wd"], extras=[h1], epilogue=_add_epilogue, tm=512, tk=D_FF, name=f"{tag}_mm_down")
    u3 = rowwise(f"{tag}_rms3", _rms, [h2], [w["g3"]], [(D_MODEL, BF16)])
    pp = mm(p_l, w["wpp"], name=f"{tag}_mm_pp")
    h3, pg = mm(u3, w["wpg"], extras=[pp, h2], col_params=[w["b_pg"]], epilogue=_ple_epilogue,
                out_dtypes=(F32, F32), tm=512, name=f"{tag}_mm_pg")
    saved = dict(h0=h0, u1=u1, proj=proj, ycat=ycat, h1=h1, u2=u2, gate=gate, up=up, act=act, h2=h2, u3=u3, pg=pg,
                 pp=pp, mixer=ms)
    return h3, saved


def _swiglu_epilogue(acc_g, acc_u):
    return acc_g, acc_u, _silu(acc_g) * acc_u


def _swiglu_bwd_epilogue(dact, gate, up):
    return jax.vjp(lambda g_, u_: _silu(g_) * u_, gate.astype(F32), up.astype(F32))[1](dact)


def _ple_epilogue(acc, pp, h2, b):
    return h2 + _ple(acc, pp, b), acc


def _rms_bwd_epilogue(du, h, dres, g):
    dh, dg = jax.vjp(_rms, h, g)[1](du)
    return dh + dres, dg


def layer_bwd(dh3, p_l, w, s, batch, tag):
    def ple_bwd(pg, pp, dh, b):
        return jax.vjp(_ple, pg, pp, b)[1](dh)

    norm_bwd = dict(epilogue=_rms_bwd_epilogue, partials=1, tm=512, tn=D_MODEL, tb=True)

    d_pg, d_pp, g_bpg = rowwise(f"{tag}_ple_bwd", ple_bwd, [s["pg"], s["pp"], dh3], [w["b_pg"]],
                                [(D_MODEL, BF16), (D_MODEL, BF16)], [D_MODEL])
    g = dict(b_pg=g_bpg)
    g["wpp"] = mm(p_l, d_pp, ta=True, name=f"{tag}_mm_dwpp")
    g["wpg"] = mm(s["u3"], d_pg, ta=True, name=f"{tag}_mm_dwpg")
    dh2, dg3 = mm(d_pg, w["wpg"], extras=[s["h2"], dh3], col_params=[w["g3"]], name=f"{tag}_mm_du3", **norm_bwd)
    g["g3"] = sum_slices(f"{tag}_sum_dg3", dg3)

    d_gate, d_up = mm(dh2, w["wd"], tb=True, extras=[s["gate"], s["up"]], epilogue=_swiglu_bwd_epilogue,
                      out_dtypes=(BF16, BF16), tm=512, tn=D_FF // 2, name=f"{tag}_mm_dact")
    g["wd"] = mm(s["act"], dh2, ta=True, name=f"{tag}_mm_dwd")
    g["wg"] = mm(s["u2"], d_gate, ta=True, name=f"{tag}_mm_dwg")
    g["wu"] = mm(s["u2"], d_up, ta=True, name=f"{tag}_mm_dwu")
    dh1, dg2 = mm([d_gate, d_up], [w["wg"], w["wu"]], extras=[s["h1"], dh2], col_params=[w["g2"]],
                  name=f"{tag}_mm_du2", **norm_bwd)
    g["g2"] = sum_slices(f"{tag}_sum_dg2", dg2)

    dycat = mm(dh1, w["wout"], tb=True, name=f"{tag}_mm_dycat")
    g["wout"] = mm(s["ycat"], dh1, ta=True, name=f"{tag}_mm_dwout")
    dproj, gm = mixer_bwd(dycat, s["proj"], w, s["mixer"], batch, tag)
    g.update(gm)
    g["win"] = mm(s["u1"], dproj, ta=True, name=f"{tag}_mm_dwin")
    dh0, dg1 = mm(dproj, w["win"], extras=[s["h0"], dh1], col_params=[w["g1"]], name=f"{tag}_mm_du1", **norm_bwd)
    g["g1"] = sum_slices(f"{tag}_sum_dg1", dg1)
    return dh0, g


def _loss_fwd_bwd(h, tgt, gf):
    def f(h_, gf_):
        e = _rms(h_, gf_) - tgt
        return 0.5 * jnp.sum(jnp.mean(e * e, axis=-1, keepdims=True), axis=0, keepdims=True)

    loss, vj = jax.vjp(f, h, gf)
    dh, dgf = vj(jnp.ones((1, 1), F32))
    return dh, jnp.broadcast_to(loss, (1, LANE)), dgf


def local_step(x, p, tgt, layers, final_g):
    batch, S, _ = x.shape
    T = batch * S
    h = x.reshape(T, D_MODEL)
    saved = []
    for l, w in enumerate(layers):
        h, s = layer_fwd(h, p[l].reshape(T, PLE_DIM), w, batch, f"l{l}")
        saved.append(s)
    dh, loss, dgf = rowwise("loss", _loss_fwd_bwd, [h, tgt.reshape(T, D_MODEL)], [_row(final_g)],
                            [(D_MODEL, F32)], [LANE, D_MODEL], tr=256)
    grads = [None] * len(layers)
    for l in reversed(range(len(layers))):
        dh, grads[l] = layer_bwd(dh, p[l].reshape(T, PLE_DIM), layers[l], saved[l], batch, f"l{l}")
    return loss[0, 0], dh.reshape(batch, S, D_MODEL), grads, dgf[0]


MESH = pl.DeviceIdType.MESH
N_DEV = 8
N_CHIP = 4
ANY = pl.BlockSpec(memory_space=pl.ANY)


def _pos():
    return lax.axis_index("x"), lax.axis_index("y"), lax.axis_index("c")


def _comm_call(body, name, out_shape, n_in, scratch):
    return pl.pallas_call(body, name=name, out_shape=out_shape, in_specs=[ANY] * n_in, out_specs=ANY,
                          scratch_shapes=scratch)


def all_gather8(name, blk):
    def body(x_ref, out_ref, send_sems, recv_sems, local_sem):
        x, y, c = _pos()
        me, sibling = (x, y, c), (x, y, 1 - c)
        chips = [(1 - x, y), (x, 1 - y), (1 - x, 1 - y)]

        def rows(px, py, pcore):
            return out_ref.at[4 * px + 2 * py + pcore]

        def copy(k, block, to, src=None):
            return pltpu.make_async_remote_copy(
                src_ref=rows(*block) if src is None else src, dst_ref=rows(*block),
                send_sem=send_sems.at[k], recv_sem=recv_sems.at[k], device_id=to, device_id_type=MESH)

        mine = pltpu.make_async_copy(x_ref, rows(*me), local_sem)
        mine.start()
        first = [copy(0, me, sibling, src=x_ref)]
        first += [copy(1 + j, me, (*chip, c), src=x_ref) for j, chip in enumerate(chips)]
        for cp in first:
            cp.start()
        passed = [copy(4 + j, (*chip, c), sibling) for j, chip in enumerate(chips)]
        for j, chip in enumerate(chips):
            copy(1 + j, (*chip, c), me).wait_recv()
            passed[j].start()
        copy(0, sibling, me).wait_recv()
        for j, chip in enumerate(chips):
            copy(4 + j, (*chip, 1 - c), me).wait_recv()
        for cp in first + passed:
            cp.wait_send()
        mine.wait()

    return _comm_call(body, name, SDS((N_DEV,) + blk.shape, blk.dtype), 1,
                      [pltpu.SemaphoreType.DMA((7,)), pltpu.SemaphoreType.DMA((7,)), pltpu.SemaphoreType.DMA])(blk)


def _comm_call_list(body, name, out_shapes, n_in, scratch):
    return pl.pallas_call(body, name=name, out_shape=out_shapes, in_specs=[ANY] * n_in, out_specs=[ANY] * len(out_shapes),
                          scratch_shapes=scratch)


def gather_layers(name, shards):
    n_t = len(shards)

    def body(*refs):
        x_refs, out_refs = refs[:n_t], refs[n_t:2 * n_t]
        send_sems, recv_sems, local_sems = refs[2 * n_t:]
        x, y, c = _pos()
        my_chip, me, sibling = 2 * x + y, (x, y, c), (x, y, 1 - c)
        chips = [(1 - x, y), (x, 1 - y), (1 - x, 1 - y)]

        def copy(k, t, chip_idx, layer, to, src=None):
            dst = out_refs[t].at[chip_idx, layer]
            return pltpu.make_async_remote_copy(
                src_ref=dst if src is None else src, dst_ref=dst, send_sem=send_sems.at[k, t],
                recv_sem=recv_sems.at[k, t], device_id=to, device_id_type=MESH)

        local, first, passed = [], [], []
        for t in range(n_t):
            src = x_refs[t].at[c]
            local.append(pltpu.make_async_copy(src, out_refs[t].at[my_chip, c], local_sems.at[t]))
            first.append(copy(0, t, my_chip, c, sibling, src=src))
            first += [copy(1 + j, t, my_chip, c, (px, py, c), src=src) for j, (px, py) in enumerate(chips)]
        for cp in local + first:
            cp.start()
        for j, (px, py) in enumerate(chips):
            for t in range(n_t):
                copy(1 + j, t, 2 * px + py, c, me).wait_recv()
                passed.append(copy(4 + j, t, 2 * px + py, c, sibling))
                passed[-1].start()
        for t in range(n_t):
            copy(0, t, my_chip, 1 - c, me).wait_recv()
            for j, (px, py) in enumerate(chips):
                copy(4 + j, t, 2 * px + py, 1 - c, me).wait_recv()
        for cp in first + passed:
            cp.wait_send()
        for cp in local:
            cp.wait()

    return _comm_call_list(body, name, [SDS((N_CHIP,) + s.shape, s.dtype) for s in shards], n_t,
                           [pltpu.SemaphoreType.DMA((7, n_t)), pltpu.SemaphoreType.DMA((7, n_t)),
                            pltpu.SemaphoreType.DMA((n_t,))])(*shards)


def swap_with_sibling(name, vs):
    n_t = len(vs)

    def body(*refs):
        v_refs, out_refs, send_sems, recv_sems = refs[:n_t], refs[n_t:2 * n_t], refs[-2], refs[-1]
        x, y, c = _pos()
        cps = [pltpu.make_async_remote_copy(src_ref=v_refs[t], dst_ref=out_refs[t], send_sem=send_sems.at[t],
                                            recv_sem=recv_sems.at[t], device_id=(x, y, 1 - c), device_id_type=MESH)
               for t in range(n_t)]
        for cp in cps:
            cp.start()
        for cp in cps:
            cp.wait()

    return _comm_call_list(body, name, [SDS(v.shape, v.dtype) for v in vs], n_t,
                           [pltpu.SemaphoreType.DMA((n_t,)), pltpu.SemaphoreType.DMA((n_t,))])(*vs)


def chips_all_to_all(name, vs):
    n_t = len(vs)

    def body(*refs):
        v_refs, out_refs = refs[:n_t], refs[n_t:2 * n_t]
        send_sems, recv_sems, local_sems = refs[2 * n_t:]
        x, y, c = _pos()
        my_chip = 2 * x + y
        chips = [(1 - x, y), (x, 1 - y), (1 - x, 1 - y)]
        local = [pltpu.make_async_copy(v_refs[t].at[my_chip], out_refs[t].at[my_chip], local_sems.at[t])
                 for t in range(n_t)]

        def copy(k, t, src_slot, dst_slot):
            px, py = chips[k]
            return pltpu.make_async_remote_copy(
                src_ref=v_refs[t].at[src_slot], dst_ref=out_refs[t].at[dst_slot], send_sem=send_sems.at[k, t],
                recv_sem=recv_sems.at[k, t], device_id=(px, py, c), device_id_type=MESH)

        sends = [copy(k, t, 2 * chips[k][0] + chips[k][1], my_chip) for k in range(3) for t in range(n_t)]
        for cp in local + sends:
            cp.start()
        for k in range(3):
            for t in range(n_t):
                copy(k, t, my_chip, 2 * chips[k][0] + chips[k][1]).wait_recv()
        for cp in sends:
            cp.wait_send()
        for cp in local:
            cp.wait()

    return _comm_call_list(body, name, [SDS(v.shape, v.dtype) for v in vs], n_t,
                           [pltpu.SemaphoreType.DMA((3, n_t)), pltpu.SemaphoreType.DMA((3, n_t)),
                            pltpu.SemaphoreType.DMA((n_t,))])(*vs)


def share_layers(name, mine):
    n_t = len(mine)

    def body(*refs):
        v_refs, out_refs = refs[:n_t], refs[n_t:2 * n_t]
        send_sems, recv_sems, local_sems = refs[2 * n_t:]
        x, y, c = _pos()
        local = [pltpu.make_async_copy(v_refs[t], out_refs[t].at[c], local_sems.at[t]) for t in range(n_t)]
        sends = [pltpu.make_async_remote_copy(src_ref=v_refs[t], dst_ref=out_refs[t].at[c], send_sem=send_sems.at[t],
                                              recv_sem=recv_sems.at[t], device_id=(x, y, 1 - c), device_id_type=MESH)
                 for t in range(n_t)]
        for cp in local + sends:
            cp.start()
        for t in range(n_t):
            pltpu.make_async_remote_copy(src_ref=v_refs[t], dst_ref=out_refs[t].at[1 - c], send_sem=send_sems.at[t],
                                         recv_sem=recv_sems.at[t], device_id=(x, y, 1 - c),
                                         device_id_type=MESH).wait_recv()
        for cp in sends:
            cp.wait_send()
        for cp in local:
            cp.wait()

    return _comm_call_list(body, name, [SDS((2,) + v.shape, v.dtype) for v in mine], n_t,
                           [pltpu.SemaphoreType.DMA((n_t,)), pltpu.SemaphoreType.DMA((n_t,)),
                            pltpu.SemaphoreType.DMA((n_t,))])(*mine)


_ROW_BLOCKS = (1024, 704, 512, 352, 256, 128, 64, 32, 16, 8)


def sum_slices(name, v, tr=512):
    n, R, C = v.shape
    tr = _pick(R, _ROW_BLOCKS)

    def body(v_ref, o_ref):
        acc = v_ref[0].astype(F32)
        for k in range(1, n):
            acc = acc + v_ref[k].astype(F32)
        o_ref[...] = acc

    return _pc(body, name=name, grid=(R // tr,), in_specs=[pl.BlockSpec((n, tr, C), lambda i: (0, i, 0))],
               out_specs=pl.BlockSpec((tr, C), lambda i: (i, 0)), out_shape=SDS((R, C), F32), sem=("parallel",))(v)


def add_slices(name, a, b, out_dtype):
    n, R, C = a.shape
    tr = _pick(R, _ROW_BLOCKS)

    def body(a_ref, b_ref, o_ref):
        o_ref[...] = (a_ref[...].astype(F32) + b_ref[...].astype(F32)).astype(o_ref.dtype)

    spec = pl.BlockSpec((1, tr, C), lambda k, i: (k, i, 0))
    return _pc(body, name=name, grid=(n, R // tr), in_specs=[spec, spec], out_specs=spec,
               out_shape=SDS(a.shape, out_dtype), sem=("parallel", "parallel"))(a, b)


def adamw(name, w, g, m, v):
    L, R, C = w.shape
    tr = _pick(R, (256, 128, 64, 32, 16, 8))
    c1 = 1.0 / (1.0 - ADAM_B1 ** ADAM_STEP)
    c2 = 1.0 / (1.0 - ADAM_B2 ** ADAM_STEP)

    def body(w_ref, g_ref, m_ref, v_ref, d_ref, nm_ref, nv_ref):
        gv = g_ref[...]
        nm = ADAM_B1 * m_ref[...] + (1.0 - ADAM_B1) * gv
        nv = ADAM_B2 * v_ref[...] + (1.0 - ADAM_B2) * (gv * gv)
        d_ref[...] = -ADAM_LR * ((nm * c1) / (jnp.sqrt(nv * c2) + ADAM_EPS) + ADAM_WD * w_ref[...])
        nm_ref[...] = nm
        nv_ref[...] = nv

    spec = pl.BlockSpec((1, tr, C), lambda l, i: (l, i, 0))
    return _pc(body, name=name, grid=(L, R // tr), in_specs=[spec] * 4, out_specs=[spec] * 3,
               out_shape=[SDS(w.shape, F32)] * 3, sem=("parallel", "parallel"))(w, g, m, v)


WEIGHTS = ["norm1_g", "w_in", "ssd_conv_w", "ssd_conv_b", "ssd_dt_bias", "ssd_a_log", "ssd_d", "ssd_norm_g",
           "lru_conv_w", "lru_conv_b", "lru_w_a", "lru_b_a", "lru_w_x", "lru_b_x", "lru_lambda", "lru_norm_g",
           "fox_b_f", "fox_norm_g", "w_out", "norm2_g", "w_gate", "w_up", "w_down", "norm3_g", "w_ple_gate",
           "b_ple_gate", "w_ple_proj", "final_norm_g"]
BIG = {"w_in": 2, "w_out": 1, "w_gate": 2, "w_up": 2, "w_down": 1, "w_ple_gate": 1, "w_ple_proj": 2}
SHARDED_SMALL = {"ssd_conv_w": 2, "lru_conv_w": 2}
SMALL = [n for n in WEIGHTS if n not in BIG]


def _pack(arrs, rows_multiple):
    flat = jnp.concatenate([a.reshape(-1) for a in arrs])
    per = rows_multiple * LANE
    n = -(-flat.shape[0] // per) * per
    return jnp.pad(flat, (0, n - flat.shape[0])).reshape(n // LANE, LANE)


def _unpack(flat2d, shapes):
    flat = flat2d.reshape(-1)
    out, off = [], 0
    for s in shapes:
        n = int(np.prod(s))
        out.append(flat[off:off + n].reshape(s))
        off += n
    return out


def _gather_shards(name, shards, axes, dtype):
    c = lax.axis_index("c")
    packed = _pack([s.astype(dtype) for s in shards], 32)
    half = packed.shape[0] // 2
    mine = lax.dynamic_slice_in_dim(packed, c * half, half, 0)
    got = all_gather8(name, mine).reshape(N_CHIP, 2 * half, LANE)
    per_chip = [_unpack(got[k], [s.shape for s in shards]) for k in range(N_CHIP)]
    return [jnp.concatenate([per_chip[k][i] for k in range(N_CHIP)], axis=ax) for i, ax in enumerate(axes)]


def kernel(x, p, norm1_g, w_in, ssd_conv_w, ssd_conv_b, ssd_dt_bias, ssd_a_log, ssd_d, ssd_norm_g, lru_conv_w, lru_conv_b, lru_w_a, lru_b_a, lru_w_x, lru_b_x, lru_lambda, lru_norm_g, fox_b_f, fox_norm_g, w_out, norm2_g, w_gate, w_up, w_down, norm3_g, w_ple_gate, b_ple_gate, w_ple_proj, final_norm_g, loss_target, m_norm1_g, m_w_in, m_ssd_conv_w, m_ssd_conv_b, m_ssd_dt_bias, m_ssd_a_log, m_ssd_d, m_ssd_norm_g, m_lru_conv_w, m_lru_conv_b, m_lru_w_a, m_lru_b_a, m_lru_w_x, m_lru_b_x, m_lru_lambda, m_lru_norm_g, m_fox_b_f, m_fox_norm_g, m_w_out, m_norm2_g, m_w_gate, m_w_up, m_w_down, m_norm3_g, m_w_ple_gate, m_b_ple_gate, m_w_ple_proj, m_final_norm_g, v_norm1_g, v_w_in, v_ssd_conv_w, v_ssd_conv_b, v_ssd_dt_bias, v_ssd_a_log, v_ssd_d, v_ssd_norm_g, v_lru_conv_w, v_lru_conv_b, v_lru_w_a, v_lru_b_a, v_lru_w_x, v_lru_b_x, v_lru_lambda, v_lru_norm_g, v_fox_b_f, v_fox_norm_g, v_w_out, v_norm2_g, v_w_gate, v_w_up, v_w_down, v_norm3_g, v_w_ple_gate, v_b_ple_gate, v_w_ple_proj, v_final_norm_g):
    args = (norm1_g, w_in, ssd_conv_w, ssd_conv_b, ssd_dt_bias, ssd_a_log, ssd_d, ssd_norm_g, lru_conv_w, lru_conv_b, lru_w_a, lru_b_a, lru_w_x, lru_b_x, lru_lambda, lru_norm_g, fox_b_f, fox_norm_g, w_out, norm2_g, w_gate, w_up, w_down, norm3_g, w_ple_gate, b_ple_gate, w_ple_proj, final_norm_g)
    m_args = (m_norm1_g, m_w_in, m_ssd_conv_w, m_ssd_conv_b, m_ssd_dt_bias, m_ssd_a_log, m_ssd_d, m_ssd_norm_g, m_lru_conv_w, m_lru_conv_b, m_lru_w_a, m_lru_b_a, m_lru_w_x, m_lru_b_x, m_lru_lambda, m_lru_norm_g, m_fox_b_f, m_fox_norm_g, m_w_out, m_norm2_g, m_w_gate, m_w_up, m_w_down, m_norm3_g, m_w_ple_gate, m_b_ple_gate, m_w_ple_proj, m_final_norm_g)
    v_args = (v_norm1_g, v_w_in, v_ssd_conv_w, v_ssd_conv_b, v_ssd_dt_bias, v_ssd_a_log, v_ssd_d, v_ssd_norm_g, v_lru_conv_w, v_lru_conv_b, v_lru_w_a, v_lru_b_a, v_lru_w_x, v_lru_b_x, v_lru_lambda, v_lru_norm_g, v_fox_b_f, v_fox_norm_g, v_w_out, v_norm2_g, v_w_gate, v_w_up, v_w_down, v_norm3_g, v_w_ple_gate, v_b_ple_gate, v_w_ple_proj, v_final_norm_g)
    w = dict(zip(WEIGHTS, args))
    mom = dict(zip(WEIGHTS, m_args))
    var = dict(zip(WEIGHTS, v_args))
    xi, yi, ci = _pos()
    chip = 2 * xi + yi

    big_names = list(BIG)
    got = gather_layers("gather_big", [w[n].astype(BF16) for n in big_names])
    full = dict(zip(SHARDED_SMALL, _gather_shards("gather_conv", [w[n] for n in SHARDED_SMALL],
                                                  list(SHARDED_SMALL.values()), F32)))
    per_layer = []
    for l in range(DEPTH):
        f = {n: (full[n][l] if n in full else w[n][l]) for n in SMALL if n != "final_norm_g"}
        for n, g4 in zip(big_names, got):
            f[n] = (g4[:, l].reshape(-1, g4.shape[-1]) if BIG[n] == 1
                    else jnp.concatenate([g4[k, l] for k in range(N_CHIP)], axis=1))
        per_layer.append(f)
    layers = [prep_layer(f) for f in per_layer]

    loss, grad_x, grads, g_final = local_step(x, p, loss_target, layers, final_norm_g)
    loss = lax.psum(loss, ("x", "y", "c"))
    gl = [unprep_grads(g) for g in grads]

    gsmall = {n: jnp.stack([gl[l][n] for l in range(DEPTH)]) for n in SMALL if n != "final_norm_g"}
    gsmall["final_norm_g"] = g_final
    small_shapes = [gsmall[n].shape for n in SMALL]
    gs = _pack([gsmall[n] for n in SMALL], 8)
    gs = sum_slices("sum_small", all_gather8("gather_small_grads", gs))
    gsum = dict(zip(SMALL, _unpack(gs, small_shapes)))
    for n, ax in SHARDED_SMALL.items():
        k = gsum[n].shape[ax] // N_CHIP
        gsum[n] = lax.dynamic_slice_in_dim(gsum[n], chip * k, k, ax)

    def chip_slices(a, n):
        return a.reshape(N_CHIP, -1, a.shape[1]) if BIG[n] == 1 else jnp.stack(jnp.split(a, N_CHIP, axis=1))

    keep = [chip_slices(jnp.where(ci == 0, gl[0][n], gl[1][n]), n) for n in big_names]
    give = [chip_slices(jnp.where(ci == 0, gl[1][n], gl[0][n]), n).astype(BF16) for n in big_names]
    got = swap_with_sibling("swap_layers", give)
    part = [add_slices(f"add_sibling_{n}", k_, g_, BF16) for n, k_, g_ in zip(big_names, keep, got)]
    arrived = chips_all_to_all("a2a_chips", part)
    mine = [sum_slices(f"sum_chips_{n}", a_) for n, a_ in zip(big_names, arrived)]
    gsum.update(zip(big_names, share_layers("share_layers", mine)))

    delta, new_m, new_v = {}, {}, {}
    for n in big_names:
        delta[n], new_m[n], new_v[n] = adamw(f"adamw_{n}", w[n], gsum[n], mom[n], var[n])
    shapes = [w[n].shape for n in SMALL]
    pk = lambda d: _pack([d[n] for n in SMALL], 8)[None]
    ds, ms, vs = adamw("adamw_small", pk(w), pk(gsum), pk(mom), pk(var))
    for d, packed in ((delta, ds), (new_m, ms), (new_v, vs)):
        d.update(zip(SMALL, _unpack(packed[0], shapes)))

    return (loss, grad_x, *[gsum[n] for n in WEIGHTS], *[delta[n] for n in WEIGHTS],
            *[new_m[n] for n in WEIGHTS], *[new_v[n] for n in WEIGHTS])
```

```python
import functools
import math

import jax
import jax.numpy as jnp
import numpy as np
from jax import lax
from jax.experimental import pallas as pl
from jax.experimental.pallas import tpu as pltpu

F32, BF16 = jnp.float32, jnp.bfloat16
SDS = jax.ShapeDtypeStruct

D_MODEL = 1024
DEPTH = 2
HEAD_DIM = 64
N_HEADS = 6
SSD_W, LRU_W, FOX_W = 384, 256, 384
D_FF = 2816
PLE_DIM = 256
IN_COLS = 2956
EPS = 1e-6
LRU_C = 8.0
LANE = 128
V7X_VMEM_LIMIT = 56 * 1024 * 1024

PW = 3072
OFF_B, OFF_C, OFF_LX, OFF_LG, OFF_SM, OFF_Z, OFF_XS, OFF_Q, OFF_K, OFF_V = (
    0, 256, 512, 768, 1024, 1152, 1536, 1920, 2304, 2688)
FOX_LANE0 = 8

ADAM_LR, ADAM_B1, ADAM_B2, ADAM_EPS, ADAM_WD, ADAM_STEP = 0.001, 0.9, 0.999, 1e-08, 0.01, 10


def _iota(shape, dim):
    return lax.broadcasted_iota(jnp.int32, shape, dim)


def _pc(body, *, name, grid, in_specs, out_specs, out_shape, scratch=(), sem=None):
    return pl.pallas_call(
        body, name=name, grid=grid, in_specs=in_specs, out_specs=out_specs, out_shape=out_shape,
        scratch_shapes=list(scratch),
        compiler_params=pltpu.CompilerParams(dimension_semantics=sem, vmem_limit_bytes=V7X_VMEM_LIMIT))


def permute_in_cols(w):
    z = lambda n: jnp.zeros(w.shape[:-1] + (n,), w.dtype)
    s = lambda a, b: w[..., a:b]
    return jnp.concatenate([
        s(768, 1024), s(1024, 1280), s(1286, 1542), s(1542, 1798),
        s(1280, 1286), z(2), s(2950, 2956), z(LANE - 14),
        s(0, 384), s(384, 768), s(1798, 2182), s(2182, 2566), s(2566, 2950)], axis=-1)


def unpermute_in_cols(g):
    s = lambda a, n: g[..., a:a + n]
    return jnp.concatenate([
        s(OFF_Z, 384), s(OFF_XS, 384), s(OFF_B, 256), s(OFF_C, 256), s(OFF_SM, 6),
        s(OFF_LX, 256), s(OFF_LG, 256), s(OFF_Q, 384), s(OFF_K, 384), s(OFF_V, 384),
        s(OFF_SM + FOX_LANE0, 6)], axis=-1)


def _pick(n, cands):
    for c in cands:
        if n % c == 0:
            return c
    return n


def mm(a, b, *, name, ta=False, tb=False, out_dtypes=(F32,), extras=(), col_params=(), partials=0, epilogue=None,
       tm=None, tn=None, tk=None):
    bs = list(b) if isinstance(b, (list, tuple)) else [b]
    pair_sum = isinstance(a, (list, tuple))
    a_list = list(a) if pair_sum else [a]
    assert not pair_sum or len(a_list) == len(bs)
    a = a_list[0]
    n_a = len(a_list)
    n_acc = 1 if pair_sum else len(bs)
    extras = [e if isinstance(e, tuple) else (e, 0) for e in extras]
    M = a.shape[1] if ta else a.shape[0]
    K = a.shape[0] if ta else a.shape[1]
    N = bs[0].shape[0] if tb else bs[0].shape[1]
    tm = tm or _pick(M, (1024, 1408, 512, 256, 128))
    tn = tn or _pick(N, (1024, 1408, 768, 512, 256, 128))
    tk = tk or _pick(K, (1024, 1408, 512, 256, 128))
    nm, nn, nk = M // tm, N // tn, K // tk
    n_b, n_ex, n_cp, n_out = len(bs), len(extras), len(col_params), len(out_dtypes)
    a_bytes, b_bytes = n_a * M * K * a.dtype.itemsize, n_b * K * N * bs[0].dtype.itemsize
    rows_inner = a_bytes * nn + b_bytes <= a_bytes + b_bytes * nm

    def ij(g0, g1):
        return (g1, g0) if rows_inner else (g0, g1)

    def body(*rest):
        a_refs, rest = rest[:n_a], rest[n_a:]
        b_refs, rest = rest[:n_b], rest[n_b:]
        in_refs, rest = rest[:n_ex + n_cp], rest[n_ex + n_cp:]
        out_refs, accs = rest[:n_out + partials], rest[n_out + partials:]
        dn = (((0 if ta else 1,), (1 if tb else 0,)), ((), ()))
        dot = lambda x_ref, y_ref: lax.dot_general(x_ref[...].astype(BF16), y_ref[...].astype(BF16), dn,
                                                   preferred_element_type=F32)
        if pair_sum:
            parts = [functools.reduce(lambda u, v: u + v, [dot(x, y) for x, y in zip(a_refs, b_refs)])]
        else:
            parts = [dot(a_refs[0], b_ref) for b_ref in b_refs]

        def finish(rs):
            outs = epilogue(*rs, *[e[...] for e in in_refs]) if epilogue is not None else tuple(rs)
            for o_ref, o in zip(out_refs[:n_out], outs):
                o_ref[...] = o.astype(o_ref.dtype)
            for o_ref, o in zip(out_refs[n_out:], outs[n_out:]):
                o_ref[0] = o

        if nk == 1:
            finish(parts)
            return
        k = pl.program_id(2)

        @pl.when(k == 0)
        def _():
            for acc, part in zip(accs, parts):
                acc[...] = part

        @pl.when(k > 0)
        def _():
            for acc, part in zip(accs, parts):
                acc[...] += part

        @pl.when(k == nk - 1)
        def _():
            finish([acc[...] for acc in accs])

    def a_map(g0, g1, k):
        i, _ = ij(g0, g1)
        return (k, i) if ta else (i, k)

    def b_map(g0, g1, k):
        _, j = ij(g0, g1)
        return (j, k) if tb else (k, j)

    def ex_map(off, g0, g1, k):
        i, j = ij(g0, g1)
        return (i, j + off)

    a_spec = pl.BlockSpec((tk, tm) if ta else (tm, tk), a_map)
    b_spec = pl.BlockSpec((tn, tk) if tb else (tk, tn), b_map)
    mn_spec = pl.BlockSpec((tm, tn), functools.partial(ex_map, 0))
    outs = _pc(body, name=name, grid=(nn, nm, nk) if rows_inner else (nm, nn, nk),
               in_specs=([a_spec] * n_a + [b_spec] * n_b
                         + [pl.BlockSpec((tm, tn), functools.partial(ex_map, off)) for _, off in extras]
                         + [pl.BlockSpec((1, tn), lambda g0, g1, k: (0, ij(g0, g1)[1]))] * n_cp),
               out_specs=([mn_spec] * n_out
                          + [pl.BlockSpec((1, 1, tn), lambda g0, g1, k: (ij(g0, g1)[0], 0, ij(g0, g1)[1]))] * partials),
               out_shape=[SDS((M, N), dt) for dt in out_dtypes] + [SDS((nm, 1, N), F32)] * partials,
               scratch=[pltpu.VMEM((tm, tn), F32)] * n_acc if nk > 1 else [],
               sem=("parallel", "parallel", "arbitrary"))(*a_list, *bs, *[e for e, _ in extras], *col_params)
    return outs[0] if len(outs) == 1 else outs


def rowwise(name, fn, rows, params, row_outs, acc_outs=(), tr=512):
    rows = [r if isinstance(r, tuple) else (r, 0, r.shape[1]) for r in rows]
    T = rows[0][0].shape[0]
    tr = min(tr, T)
    n_in, n_ro, n_ac = len(rows) + len(params), len(row_outs), len(acc_outs)

    def body(*refs):
        ins, outs = refs[:n_in], refs[n_in:]
        res = fn(*[r[...] for r in ins])
        if not isinstance(res, (tuple, list)):
            res = (res,)
        for k in range(n_ro):
            outs[k][...] = res[k].astype(outs[k].dtype)
        if n_ac:
            i = pl.program_id(0)

            @pl.when(i == 0)
            def _():
                for k in range(n_ac):
                    outs[n_ro + k][...] = res[n_ro + k]

            @pl.when(i > 0)
            def _():
                for k in range(n_ac):
                    outs[n_ro + k][...] += res[n_ro + k]

    in_specs = ([pl.BlockSpec((tr, w), functools.partial(lambda cb, i: (i, cb), cb)) for (_, cb, w) in rows]
                + [pl.BlockSpec(p.shape, lambda i: (0, 0)) for p in params])
    out_specs = ([pl.BlockSpec((tr, c), lambda i: (i, 0)) for (c, _) in row_outs]
                 + [pl.BlockSpec((1, c), lambda i: (0, 0)) for c in acc_outs])
    out_shape = [SDS((T, c), dt) for (c, dt) in row_outs] + [SDS((1, c), F32) for c in acc_outs]
    outs = _pc(body, name=name, grid=(T // tr,), in_specs=in_specs, out_specs=out_specs, out_shape=out_shape,
               sem=("arbitrary",) if n_ac else ("parallel",))(*[r[0] for r in rows], *params)
    return outs[0] if len(outs) == 1 else outs


def _rms(x, g):
    return x * lax.rsqrt(jnp.mean(x * x, axis=-1, keepdims=True) + EPS) * g


def _softplus(x):
    return jnp.maximum(x, 0.0) + jnp.log(1.0 + jnp.exp(-jnp.abs(x)))


def _silu(x):
    return x * jax.nn.sigmoid(x)


def _gelu(x):
    return 0.5 * x * (1.0 + jnp.tanh(math.sqrt(2.0 / math.pi) * (x + 0.044715 * (x * x * x))))


def _neg_expm1(x):
    series = x * (1 + x / 2 * (1 + x / 3 * (1 + x / 4 * (1 + x / 5 * (1 + x / 6 * (1 + x / 7))))))
    return -jnp.where(jnp.abs(x) < 0.3, series, jnp.exp(x) - 1.0)


def _swiglu(gu):
    return _silu(gu[:, :D_FF]) * gu[:, D_FF:]


def _ple(pg, pp, b):
    return jax.nn.sigmoid(pg + b) * pp


def _ssd_elt(small, xs_act, dtbias_row, alog_row):
    lane = _iota(small.shape, 1)
    dt = _softplus(small + dtbias_row)
    adt = jnp.where(lane < N_HEADS, -jnp.exp(alog_row) * dt, 0.0)
    head = _iota(xs_act.shape, 1) // HEAD_DIM
    dt_exp = jnp.zeros_like(xs_act)
    for h in range(N_HEADS):
        dth = jnp.sum(jnp.where(lane == h, dt, 0.0), axis=1, keepdims=True)
        dt_exp = dt_exp + jnp.where(head == h, dth, 0.0)
    return adt, xs_act * dt_exp


def _fox_elt(small, bf_row):
    lane = _iota(small.shape, 1)
    keep = (lane >= FOX_LANE0) & (lane < FOX_LANE0 + N_HEADS)
    return jnp.where(keep, -_softplus(-(small + bf_row)), 0.0)


def _lru_elt(xl, pre, b_ax, lam):
    r = jax.nn.sigmoid(pre[:, :LRU_W] + b_ax[:, :LRU_W])
    i = jax.nn.sigmoid(pre[:, LRU_W:] + b_ax[:, LRU_W:])
    log_a = -LRU_C * r * _softplus(-lam)
    a = jnp.exp(log_a)
    mult = jnp.sqrt(_neg_expm1(2.0 * log_a))
    return a, mult * (i * xl)


def _mix_post(yraw, xs_act, z, hl, lgate, yfox, dexp, g_ssd, g_lru, g_fox):
    y_ssd = _rms((yraw + xs_act * dexp) * _silu(z), g_ssd)
    y_lru = _rms(hl * _gelu(lgate), g_lru)
    y_fox = _rms(yfox, g_fox)
    return jnp.concatenate([y_ssd, y_lru, y_fox], axis=-1)


def _colsum(x):
    return jnp.sum(x, axis=0, keepdims=True)


def _shift_down(x, d):
    if d == 0:
        return x
    return jnp.where(_iota(x.shape, 0) >= d, pltpu.roll(x, d, 0), 0.0)


def _shift_up(x, d):
    if d == 0:
        return x
    s = x.shape[0]
    return jnp.where(_iota(x.shape, 0) < s - d, pltpu.roll(x, s - d, 0), 0.0)


def _conv_core(x, w, b):
    y = b + w[3:4, :] * x
    for k in range(3):
        y = y + w[k:k + 1, :] * _shift_down(x, 3 - k)
    return y


def seq_conv(name, src, col, width, w8, b, *, batch, silu, out_dtype):
    T = src.shape[0]
    S = T // batch
    c0 = col // LANE

    def body(x_ref, w_ref, b_ref, o_ref):
        y = _conv_core(x_ref[...], w_ref[...], b_ref[...])
        o_ref[...] = (_silu(y) if silu else y).astype(o_ref.dtype)

    return _pc(body, name=name, grid=(batch, width // LANE),
               in_specs=[pl.BlockSpec((S, LANE), lambda bi, ci: (bi, c0 + ci)),
                         pl.BlockSpec((8, LANE), lambda bi, ci: (0, ci)),
                         pl.BlockSpec((1, LANE), lambda bi, ci: (0, ci))],
               out_specs=pl.BlockSpec((S, LANE), lambda bi, ci: (bi, ci)),
               out_shape=SDS((T, width), out_dtype), sem=("parallel", "parallel"))(src, w8, b)


def seq_conv_bwd(name, src, col, width, w8, b, dy, *, batch, silu):
    T = src.shape[0]
    S = T // batch
    c0 = col // LANE

    def body(x_ref, w_ref, b_ref, dy_ref, dx_ref, dw_ref, db_ref):
        x, w = x_ref[...], w_ref[...]
        dpre = dy_ref[...].astype(F32)
        if silu:
            dpre = jax.vjp(_silu, _conv_core(x, w, b_ref[...]))[1](dpre)[0]
        dx = w[3:4, :] * dpre
        for k in range(3):
            dx = dx + w[k:k + 1, :] * _shift_up(dpre, 3 - k)
        dx_ref[...] = dx.astype(dx_ref.dtype)
        row8 = _iota((8, LANE), 0)
        dw = jnp.zeros((8, LANE), F32)
        for k in range(4):
            dw = dw + jnp.where(row8 == k, _colsum(dpre * _shift_down(x, 3 - k)), 0.0)
        db = _colsum(dpre)
        bi = pl.program_id(1)

        @pl.when(bi == 0)
        def _():
            dw_ref[...] = dw
            db_ref[...] = db

        @pl.when(bi > 0)
        def _():
            dw_ref[...] += dw
            db_ref[...] += db

    return _pc(body, name=name, grid=(width // LANE, batch),
               in_specs=[pl.BlockSpec((S, LANE), lambda ci, bi: (bi, c0 + ci)),
                         pl.BlockSpec((8, LANE), lambda ci, bi: (0, ci)),
                         pl.BlockSpec((1, LANE), lambda ci, bi: (0, ci)),
                         pl.BlockSpec((S, LANE), lambda ci, bi: (bi, ci))],
               out_specs=[pl.BlockSpec((S, LANE), lambda ci, bi: (bi, ci)),
                          pl.BlockSpec((8, LANE), lambda ci, bi: (0, ci)),
                          pl.BlockSpec((1, LANE), lambda ci, bi: (0, ci))],
               out_shape=[SDS((T, width), BF16), SDS((8, width), F32), SDS((1, width), F32)],
               sem=("parallel", "arbitrary"))(src, w8, b, dy)


def _split3_dot(tri, x):
    hi = x.astype(BF16)
    r1 = x - hi.astype(F32)
    mid = r1.astype(BF16)
    lo = (r1 - mid.astype(F32)).astype(BF16)
    d = lambda v: jnp.dot(tri, v, preferred_element_type=F32)
    return d(hi) + d(mid) + d(lo)


def seq_cumsum(name, x, *, batch, reverse=False, nsum=1, trow=None):
    T = x.shape[0]
    S = T // batch
    ch = min(256, S)
    nch = S // ch

    def body(x_ref, o_ref, *maybe_t):
        r, c = _iota((ch, ch), 0), _iota((ch, ch), 1)
        tri = jnp.where((c >= r) if reverse else (c <= r), 1.0, 0.0).astype(BF16)
        carry = jnp.zeros((1, LANE), F32)
        for k in (range(nch - 1, -1, -1) if reverse else range(nch)):
            xc = x_ref[k * ch:(k + 1) * ch, 0:LANE]
            for m in range(1, nsum):
                xc = xc + x_ref[k * ch:(k + 1) * ch, m * LANE:(m + 1) * LANE]
            o_ref[k * ch:(k + 1) * ch, :] = _split3_dot(tri, xc) + carry
            carry = carry + _colsum(xc)
        if trow is not None:
            maybe_t[0][...] = o_ref[...].T[trow:trow + 8, :]

    out_specs = [pl.BlockSpec((S, LANE), lambda bi: (bi, 0))]
    out_shape = [SDS((T, LANE), F32)]
    if trow is not None:
        out_specs.append(pl.BlockSpec((8, S), lambda bi: (bi, 0)))
        out_shape.append(SDS((batch * 8, S), F32))
    outs = _pc(body, name=name, grid=(batch,), in_specs=[pl.BlockSpec((S, LANE * nsum), lambda bi: (bi, 0))],
               out_specs=out_specs, out_shape=out_shape, sem=("parallel",))(x)
    return outs if trow is not None else outs[0]


def lru_scan(name, a, u, *, batch):
    T, W = a.shape
    S = T // batch

    def body(a_ref, u_ref, h_ref):
        row = _iota((8, W), 0)

        def step(g, h):
            off = pl.multiple_of(g * 8, 8)
            at, ut = a_ref[pl.ds(off, 8), :], u_ref[pl.ds(off, 8), :]
            acc = jnp.zeros((8, W), F32)
            for r in range(8):
                h = at[r:r + 1, :] * h + ut[r:r + 1, :]
                acc = jnp.where(row == r, jnp.broadcast_to(h, (8, W)), acc)
            h_ref[pl.ds(off, 8), :] = acc
            return h

        lax.fori_loop(0, S // 8, step, jnp.zeros((1, W), F32))

    spec = pl.BlockSpec((S, W), lambda bi: (bi, 0))
    return _pc(body, name=name, grid=(batch,), in_specs=[spec, spec], out_specs=spec,
               out_shape=SDS((T, W), F32), sem=("parallel",))(a, u)


def lru_scan_bwd(name, a, h, dh, *, batch):
    T, W = a.shape
    S = T // batch
    ng = S // 8

    def body(a_ref, h_ref, dh_ref, da_ref, du_ref):
        row = _iota((8, W), 0)

        def step(k, c):
            g_idx = ng - 1 - k
            off = pl.multiple_of(g_idx * 8, 8)
            offp = pl.multiple_of(jnp.maximum(g_idx - 1, 0) * 8, 8)
            at, ht, dt = a_ref[pl.ds(off, 8), :], h_ref[pl.ds(off, 8), :], dh_ref[pl.ds(off, 8), :]
            hp = jnp.where(g_idx > 0, h_ref[pl.ds(offp, 8), :], 0.0)
            da = jnp.zeros((8, W), F32)
            du = jnp.zeros((8, W), F32)
            for r in range(7, -1, -1):
                g = dt[r:r + 1, :] + c
                hprev = ht[r - 1:r, :] if r > 0 else hp[7:8, :]
                du = jnp.where(row == r, jnp.broadcast_to(g, (8, W)), du)
                da = jnp.where(row == r, jnp.broadcast_to(g * hprev, (8, W)), da)
                c = at[r:r + 1, :] * g
            da_ref[pl.ds(off, 8), :] = da
            du_ref[pl.ds(off, 8), :] = du
            return c

        lax.fori_loop(0, ng, step, jnp.zeros((1, W), F32))

    spec = pl.BlockSpec((S, W), lambda bi: (bi, 0))
    return _pc(body, name=name, grid=(batch,), in_specs=[spec] * 3, out_specs=[spec] * 2,
               out_shape=[SDS((T, W), F32)] * 2, sem=("parallel",))(a, h, dh)


def _nt(a, b):
    return lax.dot_general(a, b, (((1,), (1,)), ((), ())), preferred_element_type=F32)


def _tn(a, b):
    return lax.dot_general(a, b, (((0,), (0,)), ((), ())), preferred_element_type=F32)


def _tile(S, t=256):
    return min(t, S)


def ssd_attn_fwd(name, cm, bm, xd, cum, cum_t, *, batch):
    T = cm.shape[0]
    S = T // batch
    tq = tk = _tile(S)
    nq = S // tq

    def body(c_ref, b_ref, x_ref, cum_ref, cumt_ref, y_ref):
        i = pl.program_id(1)
        cq, cmq = cum_ref[...], c_ref[...]
        rowi, coli = _iota((tq, tk), 0), _iota((tq, tk), 1)
        half = _iota((tk, LANE), 1) // HEAD_DIM

        def step(j, accs, diag):
            off = pl.multiple_of(j * tk, tk)
            bj = b_ref[pl.ds(off, tk), :]
            gm = [_nt(cmq[:, g * LANE:(g + 1) * LANE], bj[:, g * LANE:(g + 1) * LANE]) for g in range(2)]
            ckt = cumt_ref[:, pl.ds(off, tk)]
            new = []
            for p in range(3):
                xp = x_ref[pl.ds(off, tk), p * LANE:(p + 1) * LANE]
                ws, xs = [], []
                for hh in range(2):
                    h = 2 * p + hh
                    seg = cq[:, h:h + 1] - ckt[h:h + 1, :]
                    e = jnp.exp(jnp.where(rowi >= coli, seg, -jnp.inf) if diag else seg)
                    ws.append((gm[h // 3] * e).astype(BF16))
                    xs.append(jnp.where(half == hh, xp, jnp.zeros_like(xp)))
                new.append(accs[p] + jnp.dot(jnp.concatenate(ws, axis=1), jnp.concatenate(xs, axis=0),
                                             preferred_element_type=F32))
            return tuple(new)

        accs = lax.fori_loop(0, i, functools.partial(step, diag=False),
                             tuple(jnp.zeros((tq, LANE), F32) for _ in range(3)))
        accs = step(i, accs, True)
        y_ref[...] = jnp.concatenate(accs, axis=1)

    return _pc(body, name=name, grid=(batch, nq),
               in_specs=[pl.BlockSpec((tq, 256), lambda b, i: (b * nq + i, 0)),
                         pl.BlockSpec((S, 256), lambda b, i: (b, 0)),
                         pl.BlockSpec((S, SSD_W), lambda b, i: (b, 0)),
                         pl.BlockSpec((tq, LANE), lambda b, i: (b * nq + i, 0)),
                         pl.BlockSpec((8, S), lambda b, i: (b, 0))],
               out_specs=pl.BlockSpec((tq, SSD_W), lambda b, i: (b * nq + i, 0)),
               out_shape=SDS((T, SSD_W), F32), sem=("parallel", "parallel"))(cm, bm, xd, cum, cum_t)


def ssd_attn_bwd(name, cm, bm, xd, cum, cum_t, dy, *, batch):
    T = cm.shape[0]
    S = T // batch
    tq = tk = _tile(S, 512)
    nq = S // tq

    def body(c_ref, b_ref, x_ref, cum_ref, cumt_ref, dy_ref, dx_ref, db_ref, dc_ref, dcum_ref, dcumt_ref):
        dx_ref[...] = jnp.zeros_like(dx_ref)
        db_ref[...] = jnp.zeros_like(db_ref)
        dcum_ref[...] = jnp.zeros_like(dcum_ref)
        dcumt_ref[...] = jnp.zeros_like(dcumt_ref)
        rowi, coli = _iota((tq, tk), 0), _iota((tq, tk), 1)
        halfq = _iota((tq, LANE), 1) // HEAD_DIM
        lane_q = _iota((tq, LANE), 1)

        def qblock(i, _):
            qoff = pl.multiple_of(i * tq, tq)
            cq = cum_ref[pl.ds(qoff, tq), :]
            cmq = c_ref[pl.ds(qoff, tq), :]
            dyq = dy_ref[pl.ds(qoff, tq), :]
            dyh = [[jnp.where(halfq == hh, dyq[:, p * LANE:(p + 1) * LANE], 0.0).astype(BF16) for hh in range(2)]
                   for p in range(3)]

            def step(j, carry, diag):
                dcq, rs_acc = carry
                off = pl.multiple_of(j * tk, tk)
                bj = b_ref[pl.ds(off, tk), :]
                gm = [_nt(cmq[:, g * LANE:(g + 1) * LANE], bj[:, g * LANE:(g + 1) * LANE]) for g in range(2)]
                ckt = cumt_ref[:, pl.ds(off, tk)]
                dgm = [jnp.zeros((tq, tk), F32), jnp.zeros((tq, tk), F32)]
                for p in range(3):
                    xp = x_ref[pl.ds(off, tk), p * LANE:(p + 1) * LANE]
                    ws = []
                    for hh in range(2):
                        h = 2 * p + hh
                        seg = cq[:, h:h + 1] - ckt[h:h + 1, :]
                        e = jnp.exp(jnp.where(rowi >= coli, seg, -jnp.inf) if diag else seg)
                        w = gm[h // 3] * e
                        dw = _nt(dyh[p][hh], xp)
                        zz = dw * w
                        rs_acc = rs_acc + jnp.where(lane_q == h, jnp.sum(zz, axis=1, keepdims=True), 0.0)
                        dcumt_ref[h:h + 1, pl.ds(off, tk)] += _colsum(zz)
                        dgm[h // 3] = dgm[h // 3] + dw * e
                        ws.append(w.astype(BF16))
                    dx_ref[pl.ds(off, tk), p * LANE:(p + 1) * LANE] += _tn(
                        jnp.concatenate(ws, axis=0), jnp.concatenate(dyh[p], axis=0))
                new_dcq = []
                for g in range(2):
                    dg = dgm[g].astype(BF16)
                    new_dcq.append(dcq[g] + jnp.dot(dg, bj[:, g * LANE:(g + 1) * LANE], preferred_element_type=F32))
                    db_ref[pl.ds(off, tk), g * LANE:(g + 1) * LANE] += _tn(dg, cmq[:, g * LANE:(g + 1) * LANE])
                return tuple(new_dcq), rs_acc

            carry = lax.fori_loop(
                0, i, functools.partial(step, diag=False),
                ((jnp.zeros((tq, LANE), F32), jnp.zeros((tq, LANE), F32)), jnp.zeros((tq, LANE), F32)))
            dcq, rs_acc = step(i, carry, True)
            dc_ref[pl.ds(qoff, tq), :] = jnp.concatenate(dcq, axis=1)
            dcum_ref[pl.ds(qoff, tq), :] += rs_acc
            return 0

        lax.fori_loop(0, nq, qblock, 0)
        dcum_ref[...] = dcum_ref[...] - dcumt_ref[...].T

    s256 = pl.BlockSpec((S, 256), lambda b: (b, 0))
    s384 = pl.BlockSpec((S, SSD_W), lambda b: (b, 0))
    s128 = pl.BlockSpec((S, LANE), lambda b: (b, 0))
    return _pc(body, name=name, grid=(batch,),
               in_specs=[s256, s256, s384, s128, pl.BlockSpec((8, S), lambda b: (b, 0)), s384],
               out_specs=[s384, s256, s256, s128],
               out_shape=[SDS((T, SSD_W), F32), SDS((T, 256), F32), SDS((T, 256), F32), SDS((T, LANE), F32)],
               scratch=[pltpu.VMEM((LANE, S), F32)], sem=("parallel",))(cm, bm, xd, cum, cum_t, dy)


NEG_BIG = -1e30


def fox_attn_fwd(name, proj, cum, cum_t, *, batch):
    T = proj.shape[0]
    S = T // batch
    tq = tk = _tile(S, 512)
    nq = S // tq
    scale = HEAD_DIM ** -0.5
    qb, kb, vb = OFF_Q // LANE, OFF_K // LANE, OFF_V // LANE

    def body(q_ref, k_ref, v_ref, cum_ref, cumt_ref, o_ref, lse_ref):
        p, i = pl.program_id(1), pl.program_id(2)
        cq = cum_ref[...]
        lane_q = _iota((tq, LANE), 1)
        halfq, halfk = lane_q // HEAD_DIM, _iota((tk, LANE), 1) // HEAD_DIM
        qs = q_ref[...] * scale
        qh = [jnp.where(halfq == hh, qs, 0.0).astype(BF16) for hh in range(2)]
        rowi, coli = _iota((tq, tk), 0), _iota((tq, tk), 1)
        cqh = [jnp.sum(jnp.where(lane_q == FOX_LANE0 + 2 * p + hh, cq, 0.0), axis=1, keepdims=True) for hh in range(2)]
        row8 = _iota((8, tk), 0)

        def step(j, carry, diag):
            ms, ls, acc = carry
            off = pl.multiple_of(j * tk, tk)
            kj = k_ref[pl.ds(off, tk), :].astype(BF16)
            vj = v_ref[pl.ds(off, tk), :].astype(BF16)
            ckt = cumt_ref[:, pl.ds(off, tk)]
            ps, vs, new_m, new_l, alphas = [], [], [], [], []
            for hh in range(2):
                ck = jnp.sum(jnp.where(row8 == 2 * p + hh, ckt, 0.0), axis=0, keepdims=True)
                logits = _nt(qh[hh], kj) + (cqh[hh] - ck)
                if diag:
                    logits = jnp.where(rowi >= coli, logits, -jnp.inf)
                m = jnp.maximum(ms[hh], jnp.max(logits, axis=1, keepdims=True))
                alpha = jnp.exp(ms[hh] - m)
                pr = jnp.exp(logits - m)
                new_m.append(m)
                new_l.append(alpha * ls[hh] + jnp.sum(pr, axis=1, keepdims=True))
                alphas.append(alpha)
                ps.append(pr.astype(BF16))
                vs.append(jnp.where(halfk == hh, vj, jnp.zeros_like(vj)))
            acc = acc * jnp.where(halfq == 0, alphas[0], alphas[1]) + jnp.dot(
                jnp.concatenate(ps, axis=1), jnp.concatenate(vs, axis=0), preferred_element_type=F32)
            return tuple(new_m), tuple(new_l), acc

        init = ((jnp.full((tq, 1), NEG_BIG, F32),) * 2, (jnp.zeros((tq, 1), F32),) * 2, jnp.zeros((tq, LANE), F32))
        ms, ls, acc = step(i, lax.fori_loop(0, i, functools.partial(step, diag=False), init), True)
        o_ref[...] = acc / jnp.where(halfq == 0, ls[0], ls[1])
        lse_ref[...] = (jnp.where(lane_q == 0, ms[0] + jnp.log(ls[0]), 0.0)
                        + jnp.where(lane_q == 1, ms[1] + jnp.log(ls[1]), 0.0))

    return _pc(body, name=name, grid=(batch, 3, nq),
               in_specs=[pl.BlockSpec((tq, LANE), lambda b, p, i: (b * nq + i, qb + p)),
                         pl.BlockSpec((S, LANE), lambda b, p, i: (b, kb + p)),
                         pl.BlockSpec((S, LANE), lambda b, p, i: (b, vb + p)),
                         pl.BlockSpec((tq, LANE), lambda b, p, i: (b * nq + i, 0)),
                         pl.BlockSpec((8, S), lambda b, p, i: (b, 0))],
               out_specs=[pl.BlockSpec((tq, LANE), lambda b, p, i: (b * nq + i, p))] * 2,
               out_shape=[SDS((T, FOX_W), F32)] * 2, sem=("parallel", "parallel", "parallel"))(proj, proj, proj, cum, cum_t)


def fox_attn_bwd(name, proj, o, do, lse, cum, cum_t, *, batch):
    T = proj.shape[0]
    S = T // batch
    tq = tk = _tile(S, 512)
    nq = S // tq
    scale = HEAD_DIM ** -0.5
    qb, kb, vb = OFF_Q // LANE, OFF_K // LANE, OFF_V // LANE

    def body(q_ref, k_ref, v_ref, o_ref, do_ref, lse_ref, cum_ref, cumt_ref,
             dq_ref, dk_ref, dv_ref, dcum_ref, dk_acc, dv_acc, dcumt_ref):
        p = pl.program_id(1)
        dk_acc[...] = jnp.zeros_like(dk_acc)
        dv_acc[...] = jnp.zeros_like(dv_acc)
        dcum_ref[...] = jnp.zeros_like(dcum_ref)
        dcumt_ref[...] = jnp.zeros_like(dcumt_ref)
        lane_q = _iota((tq, LANE), 1)
        halfq, halfk = lane_q // HEAD_DIM, _iota((tk, LANE), 1) // HEAD_DIM
        rowi, coli = _iota((tq, tk), 0), _iota((tq, tk), 1)
        row8 = _iota((8, tk), 0)

        def qblock(i, _):
            qoff = pl.multiple_of(i * tq, tq)
            cq = cum_ref[pl.ds(qoff, tq), :]
            qs = q_ref[pl.ds(qoff, tq), :] * scale
            doq = do_ref[pl.ds(qoff, tq), :]
            lse = lse_ref[pl.ds(qoff, tq), :]
            delta = doq * o_ref[pl.ds(qoff, tq), :]
            qh, doh, cqh, lseh, dlt = [], [], [], [], []
            for hh in range(2):
                qh.append(jnp.where(halfq == hh, qs, 0.0).astype(BF16))
                doh.append(jnp.where(halfq == hh, doq, 0.0).astype(BF16))
                cqh.append(jnp.sum(jnp.where(lane_q == FOX_LANE0 + 2 * p + hh, cq, 0.0), axis=1, keepdims=True))
                lseh.append(jnp.sum(jnp.where(lane_q == hh, lse, 0.0), axis=1, keepdims=True))
                dlt.append(jnp.sum(jnp.where(halfq == hh, delta, 0.0), axis=1, keepdims=True))

            def step(j, carry, diag):
                dq, rs = carry
                off = pl.multiple_of(j * tk, tk)
                kj = k_ref[pl.ds(off, tk), :].astype(BF16)
                vj = v_ref[pl.ds(off, tk), :].astype(BF16)
                ckt = cumt_ref[:, pl.ds(off, tk)]
                dss, prs, ks = [], [], []
                for hh in range(2):
                    ck = jnp.sum(jnp.where(row8 == 2 * p + hh, ckt, 0.0), axis=0, keepdims=True)
                    logits = _nt(qh[hh], kj) + ((cqh[hh] - lseh[hh]) - ck)
                    if diag:
                        logits = jnp.where(rowi >= coli, logits, -jnp.inf)
                    pr = jnp.exp(logits)
                    ds = pr * (_nt(doh[hh], vj) - dlt[hh])
                    rs = rs + jnp.where(lane_q == FOX_LANE0 + 2 * p + hh, jnp.sum(ds, axis=1, keepdims=True), 0.0)
                    cs = _colsum(ds)
                    dcumt_ref[0:8, pl.ds(off, tk)] += jnp.where(row8 == 2 * p + hh, cs, 0.0)
                    dss.append(ds.astype(BF16))
                    prs.append(pr.astype(BF16))
                    ks.append(jnp.where(halfk == hh, kj, jnp.zeros_like(kj)))
                dq = dq + jnp.dot(jnp.concatenate(dss, axis=1), jnp.concatenate(ks, axis=0), preferred_element_type=F32)
                dk_acc[pl.ds(off, tk), :] += _tn(jnp.concatenate(dss, axis=0), jnp.concatenate(qh, axis=0))
                dv_acc[pl.ds(off, tk), :] += _tn(jnp.concatenate(prs, axis=0), jnp.concatenate(doh, axis=0))
                return dq, rs

            carry = lax.fori_loop(0, i, functools.partial(step, diag=False),
                                  (jnp.zeros((tq, LANE), F32), jnp.zeros((tq, LANE), F32)))
            dq, rs = step(i, carry, True)
            dq_ref[pl.ds(qoff, tq), :] = (dq * scale).astype(dq_ref.dtype)
            dcum_ref[pl.ds(qoff, tq), :] += rs
            return 0

        lax.fori_loop(0, nq, qblock, 0)
        dk_ref[...] = dk_acc[...].astype(dk_ref.dtype)
        dv_ref[...] = dv_acc[...].astype(dv_ref.dtype)
        dct = dcumt_ref[...].T
        dcum_ref[...] = dcum_ref[...] - pltpu.roll(dct, FOX_LANE0, 1)

    sp = lambda c0: pl.BlockSpec((S, LANE), lambda b, p: (b, c0 + p))
    s0 = pl.BlockSpec((S, LANE), lambda b, p: (b, 0))
    return _pc(body, name=name, grid=(batch, 3),
               in_specs=[sp(qb), sp(kb), sp(vb), sp(0), sp(0), sp(0), s0, pl.BlockSpec((8, S), lambda b, p: (b, 0))],
               out_specs=[sp(0)] * 4,
               out_shape=[SDS((T, FOX_W), BF16)] * 3 + [SDS((T, FOX_W), F32)],
               scratch=[pltpu.VMEM((S, LANE), F32), pltpu.VMEM((S, LANE), F32), pltpu.VMEM((LANE, S), F32)],
               sem=("parallel", "parallel"))(proj, proj, proj, o, do, lse, cum, cum_t)


def _row(v, width=None, at=0):
    v = v.astype(F32)
    width = width or v.shape[0]
    return jnp.pad(v, (at, width - at - v.shape[0]))[None, :]


def _pad8(w4):
    return jnp.pad(w4.astype(F32), ((0, 4), (0, 0)))


def _block_diag(w):
    out = jnp.zeros((LRU_W, LRU_W), w.dtype)
    for g in range(4):
        out = lax.dynamic_update_slice(out, w[g], (g * 64, g * 64))
    return out


def prep_layer(f):
    cw, cb = f["ssd_conv_w"], f["ssd_conv_b"]
    return dict(
        win=permute_in_cols(f["w_in"]), wout=f["w_out"], wg=f["w_gate"], wu=f["w_up"], wd=f["w_down"], wpg=f["w_ple_gate"], wpp=f["w_ple_proj"],
        wax=jnp.concatenate([_block_diag(f["lru_w_a"]), _block_diag(f["lru_w_x"])], axis=1),
        g1=_row(f["norm1_g"]), g2=_row(f["norm2_g"]), g3=_row(f["norm3_g"]),
        cw_xs=_pad8(cw[:, :384]), cb_xs=_row(cb[:384]), cw_b=_pad8(cw[:, 384:640]), cb_b=_row(cb[384:640]),
        cw_c=_pad8(cw[:, 640:]), cb_c=_row(cb[640:]), cw_l=_pad8(f["lru_conv_w"]), cb_l=_row(f["lru_conv_b"]),
        dtbias_row=_row(f["ssd_dt_bias"], LANE), alog_row=_row(f["ssd_a_log"], LANE),
        dexp=jnp.repeat(f["ssd_d"].astype(F32), HEAD_DIM)[None, :], g_ssd=_row(f["ssd_norm_g"]),
        b_ax=_row(jnp.concatenate([f["lru_b_a"], f["lru_b_x"]])), lam=_row(f["lru_lambda"]), g_lru=_row(f["lru_norm_g"]),
        bf_row=_row(f["fox_b_f"], LANE, FOX_LANE0), g_fox=_row(f["fox_norm_g"]), b_pg=_row(f["b_ple_gate"]))


def unprep_grads(g):
    blocks = lambda m: jnp.stack([m[i * 64:(i + 1) * 64, i * 64:(i + 1) * 64] for i in range(4)])
    return dict(
        norm1_g=g["g1"][0], w_in=unpermute_in_cols(g["win"]),
        ssd_conv_w=jnp.concatenate([g["cw_xs"][:4], g["cw_b"][:4], g["cw_c"][:4]], axis=1),
        ssd_conv_b=jnp.concatenate([g["cb_xs"][0], g["cb_b"][0], g["cb_c"][0]]),
        ssd_dt_bias=g["dtbias_row"][0, :N_HEADS], ssd_a_log=g["alog_row"][0, :N_HEADS],
        ssd_d=jnp.sum(g["dexp"].reshape(N_HEADS, HEAD_DIM), axis=1), ssd_norm_g=g["g_ssd"][0],
        lru_conv_w=g["cw_l"][:4], lru_conv_b=g["cb_l"][0],
        lru_w_a=blocks(g["wax"][:, :LRU_W]), lru_b_a=g["b_ax"][0, :LRU_W],
        lru_w_x=blocks(g["wax"][:, LRU_W:]), lru_b_x=g["b_ax"][0, LRU_W:],
        lru_lambda=g["lam"][0], lru_norm_g=g["g_lru"][0],
        fox_b_f=g["bf_row"][0, FOX_LANE0:FOX_LANE0 + N_HEADS], fox_norm_g=g["g_fox"][0],
        w_out=g["wout"], norm2_g=g["g2"][0], w_gate=g["wg"], w_up=g["wu"], w_down=g["wd"],
        norm3_g=g["g3"][0], w_ple_gate=g["wpg"], b_ple_gate=g["b_pg"][0], w_ple_proj=g["wpp"])


def _view(a, off, width):
    return (a, off // width, width)


def _add_epilogue(acc, e):
    return (acc + e,)


def mixer_fwd(proj, w, batch, tag):
    sm = _view(proj, OFF_SM, LANE)
    conv = functools.partial(seq_conv, batch=batch)
    cmc = conv(f"{tag}_conv_c", proj, OFF_C, 256, w["cw_c"], w["cb_c"], silu=True, out_dtype=BF16)
    bmc = conv(f"{tag}_conv_b", proj, OFF_B, 256, w["cw_b"], w["cb_b"], silu=True, out_dtype=BF16)
    xs_act = conv(f"{tag}_conv_xs", proj, OFF_XS, SSD_W, w["cw_xs"], w["cb_xs"], silu=True, out_dtype=F32)
    xl = conv(f"{tag}_conv_l", proj, OFF_LX, LRU_W, w["cw_l"], w["cb_l"], silu=False, out_dtype=F32)
    adt, xd = rowwise(f"{tag}_ssd_elt", _ssd_elt, [sm, xs_act], [w["dtbias_row"], w["alog_row"]],
                      [(LANE, F32), (SSD_W, BF16)])
    cum_a, cum_at = seq_cumsum(f"{tag}_cum_a", adt, batch=batch, trow=0)
    yraw = ssd_attn_fwd(f"{tag}_ssd_fwd", cmc, bmc, xd, cum_a, cum_at, batch=batch)
    logf = rowwise(f"{tag}_fox_elt", _fox_elt, [sm], [w["bf_row"]], [(LANE, F32)])
    cum_f, cum_ft = seq_cumsum(f"{tag}_cum_f", logf, batch=batch, trow=FOX_LANE0)
    o, lse = fox_attn_fwd(f"{tag}_fox_fwd", proj, cum_f, cum_ft, batch=batch)
    pre = mm(xl, w["wax"], name=f"{tag}_mm_lru_gates")
    a, u = rowwise(f"{tag}_lru_elt", _lru_elt, [xl, pre], [w["b_ax"], w["lam"]], [(LRU_W, F32), (LRU_W, F32)])
    hl = lru_scan(f"{tag}_lru_scan", a, u, batch=batch)
    ycat = rowwise(f"{tag}_mix_post", _mix_post,
                   [yraw, xs_act, _view(proj, OFF_Z, SSD_W), hl, _view(proj, OFF_LG, LRU_W), o],
                   [w["dexp"], w["g_ssd"], w["g_lru"], w["g_fox"]], [(D_MODEL, BF16)], tr=256)
    saved = dict(cmc=cmc, bmc=bmc, xs_act=xs_act, xl=xl, xd=xd, cum_a=cum_a, cum_at=cum_at, yraw=yraw,
                 cum_f=cum_f, cum_ft=cum_ft, o=o, lse=lse, pre=pre, a=a, hl=hl)
    return ycat, saved


def mixer_bwd(dycat, proj, w, s, batch, tag):
    sm = _view(proj, OFF_SM, LANE)
    g = {}

    def post_bwd(yraw, xs_act, z, hl, lg, o, dyc, dexp, g_ssd, g_lru, g_fox):
        return jax.vjp(_mix_post, yraw, xs_act, z, hl, lg, o, dexp, g_ssd, g_lru, g_fox)[1](dyc)

    (dyraw, dxs1, dz, dhl, dlg, do, g["dexp"], g["g_ssd"], g["g_lru"], g["g_fox"]) = rowwise(
        f"{tag}_mix_post_bwd", post_bwd,
        [s["yraw"], s["xs_act"], _view(proj, OFF_Z, SSD_W), s["hl"], _view(proj, OFF_LG, LRU_W), s["o"], dycat],
        [w["dexp"], w["g_ssd"], w["g_lru"], w["g_fox"]],
        [(SSD_W, F32), (SSD_W, F32), (SSD_W, BF16), (LRU_W, F32), (LRU_W, BF16), (FOX_W, F32)],
        [SSD_W, SSD_W, LRU_W, FOX_W], tr=256)

    dq, dk, dv, dcum3 = fox_attn_bwd(f"{tag}_fox_bwd", proj, s["o"], do, s["lse"], s["cum_f"], s["cum_ft"], batch=batch)
    dlogf = seq_cumsum(f"{tag}_rcum_f", dcum3, batch=batch, reverse=True, nsum=3)

    dxd, dbm, dcm, dcum_a = ssd_attn_bwd(f"{tag}_ssd_bwd", s["cmc"], s["bmc"], s["xd"], s["cum_a"], s["cum_at"], dyraw,
                                         batch=batch)
    dadt = seq_cumsum(f"{tag}_rcum_a", dcum_a, batch=batch, reverse=True)

    def ssd_elt_bwd(small, xs_act, dadt_, dxd_, dxs1_, dtbias, alog):
        dsm, dxs, ddtb, dalog = jax.vjp(_ssd_elt, small, xs_act, dtbias, alog)[1]((dadt_, dxd_))
        return dsm, dxs + dxs1_, ddtb, dalog

    dsm_s, dxs_act, g["dtbias_row"], g["alog_row"] = rowwise(
        f"{tag}_ssd_elt_bwd", ssd_elt_bwd, [sm, s["xs_act"], dadt, dxd, dxs1], [w["dtbias_row"], w["alog_row"]],
        [(LANE, F32), (SSD_W, F32)], [LANE, LANE])

    def fox_elt_bwd(small, dlogf_, dsm_s_, bf_row):
        dsm, dbf = jax.vjp(_fox_elt, small, bf_row)[1](dlogf_)
        return dsm + dsm_s_, dbf

    dsm, g["bf_row"] = rowwise(f"{tag}_fox_elt_bwd", fox_elt_bwd, [sm, dlogf, dsm_s], [w["bf_row"]],
                               [(LANE, BF16)], [LANE])

    cbwd = functools.partial(seq_conv_bwd, batch=batch)
    dxs_raw, g["cw_xs"], g["cb_xs"] = cbwd(f"{tag}_conv_xs_bwd", proj, OFF_XS, SSD_W, w["cw_xs"], w["cb_xs"], dxs_act, silu=True)
    db_raw, g["cw_b"], g["cb_b"] = cbwd(f"{tag}_conv_b_bwd", proj, OFF_B, 256, w["cw_b"], w["cb_b"], dbm, silu=True)
    dc_raw, g["cw_c"], g["cb_c"] = cbwd(f"{tag}_conv_c_bwd", proj, OFF_C, 256, w["cw_c"], w["cb_c"], dcm, silu=True)

    da, du = lru_scan_bwd(f"{tag}_lru_scan_bwd", s["a"], s["hl"], dhl, batch=batch)

    def lru_elt_bwd(xl, pre, da_, du_, b_ax, lam):
        return jax.vjp(_lru_elt, xl, pre, b_ax, lam)[1]((da_, du_))

    dxl1, dpre, g["b_ax"], g["lam"] = rowwise(
        f"{tag}_lru_elt_bwd", lru_elt_bwd, [s["xl"], s["pre"], da, du], [w["b_ax"], w["lam"]],
        [(LRU_W, F32), (2 * LRU_W, BF16)], [2 * LRU_W, LRU_W])
    g["wax"] = mm(s["xl"], dpre, ta=True, name=f"{tag}_mm_dwax")
    dxl = mm(dpre, w["wax"], tb=True, extras=[dxl1], epilogue=_add_epilogue, name=f"{tag}_mm_dxl")
    dlx_raw, g["cw_l"], g["cb_l"] = cbwd(f"{tag}_conv_l_bwd", proj, OFF_LX, LRU_W, w["cw_l"], w["cb_l"], dxl, silu=False)

    dproj = jnp.concatenate([db_raw, dc_raw, dlx_raw, dlg, dsm, dz, dxs_raw, dq, dk, dv], axis=1)
    return dproj, g


def layer_fwd(h0, p_l, w, batch, tag):
    u1 = rowwise(f"{tag}_rms1", _rms, [h0], [w["g1"]], [(D_MODEL, BF16)])
    proj = mm(u1, w["win"], name=f"{tag}_mm_in")
    ycat, ms = mixer_fwd(proj, w, batch, tag)
    h1 = mm(ycat, w["wout"], extras=[h0], epilogue=_add_epilogue, name=f"{tag}_mm_out")
    u2 = rowwise(f"{tag}_rms2", _rms, [h1], [w["g2"]], [(D_MODEL, BF16)])
    gate, up, act = mm(u2, [w["wg"], w["wu"]], out_dtypes=(BF16, BF16, BF16), epilogue=_swiglu_epilogue,
                       tm=512, tn=D_FF // 2, name=f"{tag}_mm_gu")
    h2 = mm(act, w["wd"], extras=[h1], epilogue=_add_epilogue, tm=512, tk=D_FF, name=f"{tag}_mm_down")
    u3 = rowwise(f"{tag}_rms3", _rms, [h2], [w["g3"]], [(D_MODEL, BF16)])
    pp = mm(p_l, w["wpp"], name=f"{tag}_mm_pp")
    h3, pg = mm(u3, w["wpg"], extras=[pp, h2], col_params=[w["b_pg"]], epilogue=_ple_epilogue,
                out_dtypes=(F32, F32), tm=512, name=f"{tag}_mm_pg")
    saved = dict(h0=h0, u1=u1, proj=proj, ycat=ycat, h1=h1, u2=u2, gate=gate, up=up, act=act, h2=h2, u3=u3, pg=pg,
                 pp=pp, mixer=ms)
    return h3, saved


def _swiglu_epilogue(acc_g, acc_u):
    return acc_g, acc_u, _silu(acc_g) * acc_u


def _swiglu_bwd_epilogue(dact, gate, up):
    return jax.vjp(lambda g_, u_: _silu(g_) * u_, gate.astype(F32), up.astype(F32))[1](dact)


def _ple_epilogue(acc, pp, h2, b):
    return h2 + _ple(acc, pp, b), acc


def _rms_bwd_epilogue(du, h, dres, g):
    dh, dg = jax.vjp(_rms, h, g)[1](du)
    return dh + dres, dg


def layer_bwd(dh3, p_l, w, s, batch, tag):
    def ple_bwd(pg, pp, dh, b):
        return jax.vjp(_ple, pg, pp, b)[1](dh)

    norm_bwd = dict(epilogue=_rms_bwd_epilogue, partials=1, tm=512, tn=D_MODEL, tb=True)

    d_pg, d_pp, g_bpg = rowwise(f"{tag}_ple_bwd", ple_bwd, [s["pg"], s["pp"], dh3], [w["b_pg"]],
                                [(D_MODEL, BF16), (D_MODEL, BF16)], [D_MODEL])
    g = dict(b_pg=g_bpg)
    g["wpp"] = mm(p_l, d_pp, ta=True, name=f"{tag}_mm_dwpp")
    g["wpg"] = mm(s["u3"], d_pg, ta=True, name=f"{tag}_mm_dwpg")
    dh2, dg3 = mm(d_pg, w["wpg"], extras=[s["h2"], dh3], col_params=[w["g3"]], name=f"{tag}_mm_du3", **norm_bwd)
    g["g3"] = sum_slices(f"{tag}_sum_dg3", dg3)

    d_gate, d_up = mm(dh2, w["wd"], tb=True, extras=[s["gate"], s["up"]], epilogue=_swiglu_bwd_epilogue,
                      out_dtypes=(BF16, BF16), tm=512, tn=D_FF // 2, name=f"{tag}_mm_dact")
    g["wd"] = mm(s["act"], dh2, ta=True, name=f"{tag}_mm_dwd")
    g["wg"] = mm(s["u2"], d_gate, ta=True, name=f"{tag}_mm_dwg")
    g["wu"] = mm(s["u2"], d_up, ta=True, name=f"{tag}_mm_dwu")
    dh1, dg2 = mm([d_gate, d_up], [w["wg"], w["wu"]], extras=[s["h1"], dh2], col_params=[w["g2"]],
                  name=f"{tag}_mm_du2", **norm_bwd)
    g["g2"] = sum_slices(f"{tag}_sum_dg2", dg2)

    dycat = mm(dh1, w["wout"], tb=True, name=f"{tag}_mm_dycat")
    g["wout"] = mm(s["ycat"], dh1, ta=True, name=f"{tag}_mm_dwout")
    dproj, gm = mixer_bwd(dycat, s["proj"], w, s["mixer"], batch, tag)
    g.update(gm)
    g["win"] = mm(s["u1"], dproj, ta=True, name=f"{tag}_mm_dwin")
    dh0, dg1 = mm(dproj, w["win"], extras=[s["h0"], dh1], col_params=[w["g1"]], name=f"{tag}_mm_du1", **norm_bwd)
    g["g1"] = sum_slices(f"{tag}_sum_dg1", dg1)
    return dh0, g


def _loss_fwd_bwd(h, tgt, gf):
    def f(h_, gf_):
        e = _rms(h_, gf_) - tgt
        return 0.5 * jnp.sum(jnp.mean(e * e, axis=-1, keepdims=True), axis=0, keepdims=True)

    loss, vj = jax.vjp(f, h, gf)
    dh, dgf = vj(jnp.ones((1, 1), F32))
    return dh, jnp.broadcast_to(loss, (1, LANE)), dgf


def local_step(x, p, tgt, layers, final_g):
    batch, S, _ = x.shape
    T = batch * S
    h = x.reshape(T, D_MODEL)
    saved = []
    for l, w in enumerate(layers):
        h, s = layer_fwd(h, p[l].reshape(T, PLE_DIM), w, batch, f"l{l}")
        saved.append(s)
    dh, loss, dgf = rowwise("loss", _loss_fwd_bwd, [h, tgt.reshape(T, D_MODEL)], [_row(final_g)],
                            [(D_MODEL, F32)], [LANE, D_MODEL], tr=256)
    grads = [None] * len(layers)
    for l in reversed(range(len(layers))):
        dh, grads[l] = layer_bwd(dh, p[l].reshape(T, PLE_DIM), layers[l], saved[l], batch, f"l{l}")
    return loss[0, 0], dh.reshape(batch, S, D_MODEL), grads, dgf[0]


MESH = pl.DeviceIdType.MESH
N_DEV = 8
N_CHIP = 4
ANY = pl.BlockSpec(memory_space=pl.ANY)


def _pos():
    return lax.axis_index("x"), lax.axis_index("y"), lax.axis_index("c")


def _comm_call(body, name, out_shape, n_in, scratch):
    return pl.pallas_call(body, name=name, out_shape=out_shape, in_specs=[ANY] * n_in, out_specs=ANY,
                          scratch_shapes=scratch)


def all_gather8(name, blk):
    def body(x_ref, out_ref, send_sems, recv_sems, local_sem):
        x, y, c = _pos()
        me, sibling = (x, y, c), (x, y, 1 - c)
        chips = [(1 - x, y), (x, 1 - y), (1 - x, 1 - y)]

        def rows(px, py, pcore):
            return out_ref.at[4 * px + 2 * py + pcore]

        def copy(k, block, to, src=None):
            return pltpu.make_async_remote_copy(
                src_ref=rows(*block) if src is None else src, dst_ref=rows(*block),
                send_sem=send_sems.at[k], recv_sem=recv_sems.at[k], device_id=to, device_id_type=MESH)

        mine = pltpu.make_async_copy(x_ref, rows(*me), local_sem)
        mine.start()
        first = [copy(0, me, sibling, src=x_ref)]
        first += [copy(1 + j, me, (*chip, c), src=x_ref) for j, chip in enumerate(chips)]
        for cp in first:
            cp.start()
        passed = [copy(4 + j, (*chip, c), sibling) for j, chip in enumerate(chips)]
        for j, chip in enumerate(chips):
            copy(1 + j, (*chip, c), me).wait_recv()
            passed[j].start()
        copy(0, sibling, me).wait_recv()
        for j, chip in enumerate(chips):
            copy(4 + j, (*chip, 1 - c), me).wait_recv()
        for cp in first + passed:
            cp.wait_send()
        mine.wait()

    return _comm_call(body, name, SDS((N_DEV,) + blk.shape, blk.dtype), 1,
                      [pltpu.SemaphoreType.DMA((7,)), pltpu.SemaphoreType.DMA((7,)), pltpu.SemaphoreType.DMA])(blk)


def _comm_call_list(body, name, out_shapes, n_in, scratch):
    return pl.pallas_call(body, name=name, out_shape=out_shapes, in_specs=[ANY] * n_in, out_specs=[ANY] * len(out_shapes),
                          scratch_shapes=scratch)


def gather_layers(name, shards):
    n_t = len(shards)

    def body(*refs):
        x_refs, out_refs = refs[:n_t], refs[n_t:2 * n_t]
        send_sems, recv_sems, local_sems = refs[2 * n_t:]
        x, y, c = _pos()
        my_chip, me, sibling = 2 * x + y, (x, y, c), (x, y, 1 - c)
        chips = [(1 - x, y), (x, 1 - y), (1 - x, 1 - y)]

        def copy(k, t, chip_idx, layer, to, src=None):
            dst = out_refs[t].at[chip_idx, layer]
            return pltpu.make_async_remote_copy(
                src_ref=dst if src is None else src, dst_ref=dst, send_sem=send_sems.at[k, t],
                recv_sem=recv_sems.at[k, t], device_id=to, device_id_type=MESH)

        local, first, passed = [], [], []
        for t in range(n_t):
            src = x_refs[t].at[c]
            local.append(pltpu.make_async_copy(src, out_refs[t].at[my_chip, c], local_sems.at[t]))
            first.append(copy(0, t, my_chip, c, sibling, src=src))
            first += [copy(1 + j, t, my_chip, c, (px, py, c), src=src) for j, (px, py) in enumerate(chips)]
        for cp in local + first:
            cp.start()
        for j, (px, py) in enumerate(chips):
            for t in range(n_t):
                copy(1 + j, t, 2 * px + py, c, me).wait_recv()
                passed.append(copy(4 + j, t, 2 * px + py, c, sibling))
                passed[-1].start()
        for t in range(n_t):
            copy(0, t, my_chip, 1 - c, me).wait_recv()
            for j, (px, py) in enumerate(chips):
                copy(4 + j, t, 2 * px + py, 1 - c, me).wait_recv()
        for cp in first + passed:
            cp.wait_send()
        for cp in local:
            cp.wait()

    return _comm_call_list(body, name, [SDS((N_CHIP,) + s.shape, s.dtype) for s in shards], n_t,
                           [pltpu.SemaphoreType.DMA((7, n_t)), pltpu.SemaphoreType.DMA((7, n_t)),
                            pltpu.SemaphoreType.DMA((n_t,))])(*shards)


def swap_with_sibling(name, vs):
    n_t = len(vs)

    def body(*refs):
        v_refs, out_refs, send_sems, recv_sems = refs[:n_t], refs[n_t:2 * n_t], refs[-2], refs[-1]
        x, y, c = _pos()
        cps = [pltpu.make_async_remote_copy(src_ref=v_refs[t], dst_ref=out_refs[t], send_sem=send_sems.at[t],
                                            recv_sem=recv_sems.at[t], device_id=(x, y, 1 - c), device_id_type=MESH)
               for t in range(n_t)]
        for cp in cps:
            cp.start()
        for cp in cps:
            cp.wait()

    return _comm_call_list(body, name, [SDS(v.shape, v.dtype) for v in vs], n_t,
                           [pltpu.SemaphoreType.DMA((n_t,)), pltpu.SemaphoreType.DMA((n_t,))])(*vs)


def chips_all_to_all(name, vs):
    n_t = len(vs)

    def body(*refs):
        v_refs, out_refs = refs[:n_t], refs[n_t:2 * n_t]
        send_sems, recv_sems, local_sems = refs[2 * n_t:]
        x, y, c = _pos()
        my_chip = 2 * x + y
        chips = [(1 - x, y), (x, 1 - y), (1 - x, 1 - y)]
        local = [pltpu.make_async_copy(v_refs[t].at[my_chip], out_refs[t].at[my_chip], local_sems.at[t])
                 for t in range(n_t)]

        def copy(k, t, src_slot, dst_slot):
            px, py = chips[k]
            return pltpu.make_async_remote_copy(
                src_ref=v_refs[t].at[src_slot], dst_ref=out_refs[t].at[dst_slot], send_sem=send_sems.at[k, t],
                recv_sem=recv_sems.at[k, t], device_id=(px, py, c), device_id_type=MESH)

        sends = [copy(k, t, 2 * chips[k][0] + chips[k][1], my_chip) for k in range(3) for t in range(n_t)]
        for cp in local + sends:
            cp.start()
        for k in range(3):
            for t in range(n_t):
                copy(k, t, my_chip, 2 * chips[k][0] + chips[k][1]).wait_recv()
        for cp in sends:
            cp.wait_send()
        for cp in local:
            cp.wait()

    return _comm_call_list(body, name, [SDS(v.shape, v.dtype) for v in vs], n_t,
                           [pltpu.SemaphoreType.DMA((3, n_t)), pltpu.SemaphoreType.DMA((3, n_t)),
                            pltpu.SemaphoreType.DMA((n_t,))])(*vs)


_ROW_BLOCKS = (1024, 704, 512, 352, 256, 128, 64, 32, 16, 8)


def sum_slices(name, v, tr=512):
    n, R, C = v.shape
    tr = _pick(R, _ROW_BLOCKS)

    def body(v_ref, o_ref):
        acc = v_ref[0].astype(F32)
        for k in range(1, n):
            acc = acc + v_ref[k].astype(F32)
        o_ref[...] = acc

    return _pc(body, name=name, grid=(R // tr,), in_specs=[pl.BlockSpec((n, tr, C), lambda i: (0, i, 0))],
               out_specs=pl.BlockSpec((tr, C), lambda i: (i, 0)), out_shape=SDS((R, C), F32), sem=("parallel",))(v)


def add_slices(name, a, b, out_dtype):
    n, R, C = a.shape
    tr = _pick(R, _ROW_BLOCKS)

    def body(a_ref, b_ref, o_ref):
        o_ref[...] = (a_ref[...].astype(F32) + b_ref[...].astype(F32)).astype(o_ref.dtype)

    spec = pl.BlockSpec((1, tr, C), lambda k, i: (k, i, 0))
    return _pc(body, name=name, grid=(n, R // tr), in_specs=[spec, spec], out_specs=spec,
               out_shape=SDS(a.shape, out_dtype), sem=("parallel", "parallel"))(a, b)


def adamw(name, w, g, m, v):
    L, R, C = w.shape
    tr = _pick(R, (256, 128, 64, 32, 16, 8))
    c1 = 1.0 / (1.0 - ADAM_B1 ** ADAM_STEP)
    c2 = 1.0 / (1.0 - ADAM_B2 ** ADAM_STEP)

    def body(w_ref, g_ref, m_ref, v_ref, d_ref, nm_ref, nv_ref):
        gv = g_ref[...]
        nm = ADAM_B1 * m_ref[...] + (1.0 - ADAM_B1) * gv
        nv = ADAM_B2 * v_ref[...] + (1.0 - ADAM_B2) * (gv * gv)
        d_ref[...] = -ADAM_LR * ((nm * c1) / (jnp.sqrt(nv * c2) + ADAM_EPS) + ADAM_WD * w_ref[...])
        nm_ref[...] = nm
        nv_ref[...] = nv

    spec = pl.BlockSpec((1, tr, C), lambda l, i: (l, i, 0))
    return _pc(body, name=name, grid=(L, R // tr), in_specs=[spec] * 4, out_specs=[spec] * 3,
               out_shape=[SDS(w.shape, F32)] * 3, sem=("parallel", "parallel"))(w, g, m, v)


WEIGHTS = ["norm1_g", "w_in", "ssd_conv_w", "ssd_conv_b", "ssd_dt_bias", "ssd_a_log", "ssd_d", "ssd_norm_g",
           "lru_conv_w", "lru_conv_b", "lru_w_a", "lru_b_a", "lru_w_x", "lru_b_x", "lru_lambda", "lru_norm_g",
           "fox_b_f", "fox_norm_g", "w_out", "norm2_g", "w_gate", "w_up", "w_down", "norm3_g", "w_ple_gate",
           "b_ple_gate", "w_ple_proj", "final_norm_g"]
BIG = {"w_in": 2, "w_out": 1, "w_gate": 2, "w_up": 2, "w_down": 1, "w_ple_gate": 1, "w_ple_proj": 2}
SHARDED_SMALL = {"ssd_conv_w": 2, "lru_conv_w": 2}
SMALL = [n for n in WEIGHTS if n not in BIG]


def _pack(arrs, rows_multiple):
    flat = jnp.concatenate([a.reshape(-1) for a in arrs])
    per = rows_multiple * LANE
    n = -(-flat.shape[0] // per) * per
    return jnp.pad(flat, (0, n - flat.shape[0])).reshape(n // LANE, LANE)


def _unpack(flat2d, shapes):
    flat = flat2d.reshape(-1)
    out, off = [], 0
    for s in shapes:
        n = int(np.prod(s))
        out.append(flat[off:off + n].reshape(s))
        off += n
    return out


def _gather_shards(name, shards, axes, dtype):
    c = lax.axis_index("c")
    packed = _pack([s.astype(dtype) for s in shards], 32)
    half = packed.shape[0] // 2
    mine = lax.dynamic_slice_in_dim(packed, c * half, half, 0)
    got = all_gather8(name, mine).reshape(N_CHIP, 2 * half, LANE)
    per_chip = [_unpack(got[k], [s.shape for s in shards]) for k in range(N_CHIP)]
    return [jnp.concatenate([per_chip[k][i] for k in range(N_CHIP)], axis=ax) for i, ax in enumerate(axes)]


def kernel(x, p, norm1_g, w_in, ssd_conv_w, ssd_conv_b, ssd_dt_bias, ssd_a_log, ssd_d, ssd_norm_g, lru_conv_w, lru_conv_b, lru_w_a, lru_b_a, lru_w_x, lru_b_x, lru_lambda, lru_norm_g, fox_b_f, fox_norm_g, w_out, norm2_g, w_gate, w_up, w_down, norm3_g, w_ple_gate, b_ple_gate, w_ple_proj, final_norm_g, loss_target, m_norm1_g, m_w_in, m_ssd_conv_w, m_ssd_conv_b, m_ssd_dt_bias, m_ssd_a_log, m_ssd_d, m_ssd_norm_g, m_lru_conv_w, m_lru_conv_b, m_lru_w_a, m_lru_b_a, m_lru_w_x, m_lru_b_x, m_lru_lambda, m_lru_norm_g, m_fox_b_f, m_fox_norm_g, m_w_out, m_norm2_g, m_w_gate, m_w_up, m_w_down, m_norm3_g, m_w_ple_gate, m_b_ple_gate, m_w_ple_proj, m_final_norm_g, v_norm1_g, v_w_in, v_ssd_conv_w, v_ssd_conv_b, v_ssd_dt_bias, v_ssd_a_log, v_ssd_d, v_ssd_norm_g, v_lru_conv_w, v_lru_conv_b, v_lru_w_a, v_lru_b_a, v_lru_w_x, v_lru_b_x, v_lru_lambda, v_lru_norm_g, v_fox_b_f, v_fox_norm_g, v_w_out, v_norm2_g, v_w_gate, v_w_up, v_w_down, v_norm3_g, v_w_ple_gate, v_b_ple_gate, v_w_ple_proj, v_final_norm_g):
    args = (norm1_g, w_in, ssd_conv_w, ssd_conv_b, ssd_dt_bias, ssd_a_log, ssd_d, ssd_norm_g, lru_conv_w, lru_conv_b, lru_w_a, lru_b_a, lru_w_x, lru_b_x, lru_lambda, lru_norm_g, fox_b_f, fox_norm_g, w_out, norm2_g, w_gate, w_up, w_down, norm3_g, w_ple_gate, b_ple_gate, w_ple_proj, final_norm_g)
    m_args = (m_norm1_g, m_w_in, m_ssd_conv_w, m_ssd_conv_b, m_ssd_dt_bias, m_ssd_a_log, m_ssd_d, m_ssd_norm_g, m_lru_conv_w, m_lru_conv_b, m_lru_w_a, m_lru_b_a, m_lru_w_x, m_lru_b_x, m_lru_lambda, m_lru_norm_g, m_fox_b_f, m_fox_norm_g, m_w_out, m_norm2_g, m_w_gate, m_w_up, m_w_down, m_norm3_g, m_w_ple_gate, m_b_ple_gate, m_w_ple_proj, m_final_norm_g)
    v_args = (v_norm1_g, v_w_in, v_ssd_conv_w, v_ssd_conv_b, v_ssd_dt_bias, v_ssd_a_log, v_ssd_d, v_ssd_norm_g, v_lru_conv_w, v_lru_conv_b, v_lru_w_a, v_lru_b_a, v_lru_w_x, v_lru_b_x, v_lru_lambda, v_lru_norm_g, v_fox_b_f, v_fox_norm_g, v_w_out, v_norm2_g, v_w_gate, v_w_up, v_w_down, v_norm3_g, v_w_ple_gate, v_b_ple_gate, v_w_ple_proj, v_final_norm_g)
    w = dict(zip(WEIGHTS, args))
    mom = dict(zip(WEIGHTS, m_args))
    var = dict(zip(WEIGHTS, v_args))
    xi, yi, ci = _pos()
    chip = 2 * xi + yi

    big_names = list(BIG)
    got = gather_layers("gather_big", [w[n].astype(BF16) for n in big_names])
    full = dict(zip(SHARDED_SMALL, _gather_shards("gather_conv", [w[n] for n in SHARDED_SMALL],
                                                  list(SHARDED_SMALL.values()), F32)))
    per_layer = []
    for l in range(DEPTH):
        f = {n: (full[n][l] if n in full else w[n][l]) for n in SMALL if n != "final_norm_g"}
        for n, g4 in zip(big_names, got):
            f[n] = (g4[:, l].reshape(-1, g4.shape[-1]) if BIG[n] == 1
                    else jnp.concatenate([g4[k, l] for k in range(N_CHIP)], axis=1))
        per_layer.append(f)
    layers = [prep_layer(f) for f in per_layer]

    loss, grad_x, grads, g_final = local_step(x, p, loss_target, layers, final_norm_g)
    loss = lax.psum(loss, ("x", "y", "c"))
    gl = [unprep_grads(g) for g in grads]

    gsmall = {n: jnp.stack([gl[l][n] for l in range(DEPTH)]) for n in SMALL if n != "final_norm_g"}
    gsmall["final_norm_g"] = g_final
    small_shapes = [gsmall[n].shape for n in SMALL]
    gs = _pack([gsmall[n] for n in SMALL], 8)
    gs = sum_slices("sum_small", all_gather8("gather_small_grads", gs))
    gsum = dict(zip(SMALL, _unpack(gs, small_shapes)))
    for n, ax in SHARDED_SMALL.items():
        k = gsum[n].shape[ax] // N_CHIP
        gsum[n] = lax.dynamic_slice_in_dim(gsum[n], chip * k, k, ax)

    def chip_slices(a, n):
        return a.reshape(N_CHIP, -1, a.shape[1]) if BIG[n] == 1 else jnp.stack(jnp.split(a, N_CHIP, axis=1))

    keep = [chip_slices(jnp.where(ci == 0, gl[0][n], gl[1][n]), n) for n in big_names]
    give = [chip_slices(jnp.where(ci == 0, gl[1][n], gl[0][n]), n).astype(BF16) for n in big_names]
    got = swap_with_sibling("swap_layers", give)
    part = [add_slices(f"add_sibling_{n}", k_, g_, BF16) for n, k_, g_ in zip(big_names, keep, got)]
    arrived = chips_all_to_all("a2a_chips", part)
    mine = [sum_slices(f"sum_chips_{n}", a_) for n, a_ in zip(big_names, arrived)]
    other = swap_with_sibling("swap_results", mine)
    for n, m_, o_ in zip(big_names, mine, other):
        gsum[n] = jnp.stack([jnp.where(ci == 0, m_, o_), jnp.where(ci == 0, o_, m_)])

    delta, new_m, new_v = {}, {}, {}
    for n in big_names:
        delta[n], new_m[n], new_v[n] = adamw(f"adamw_{n}", w[n], gsum[n], mom[n], var[n])
    shapes = [w[n].shape for n in SMALL]
    pk = lambda d: _pack([d[n] for n in SMALL], 8)[None]
    ds, ms, vs = adamw("adamw_small", pk(w), pk(gsum), pk(mom), pk(var))
    for d, packed in ((delta, ds), (new_m, ms), (new_v, vs)):
        d.update(zip(SMALL, _unpack(packed[0], shapes)))

    return (loss, grad_x, *[gsum[n] for n in WEIGHTS], *[delta[n] for n in WEIGHTS],
            *[new_m[n] for n in WEIGHTS], *[new_v[n] for n in WEIGHTS])
```

```python
import functools
import math

import jax
import jax.numpy as jnp
import numpy as np
from jax import lax
from jax.experimental import pallas as pl
from jax.experimental.pallas import tpu as pltpu

F32, BF16 = jnp.float32, jnp.bfloat16
SDS = jax.ShapeDtypeStruct

D_MODEL = 1024
DEPTH = 2
HEAD_DIM = 64
N_HEADS = 6
SSD_W, LRU_W, FOX_W = 384, 256, 384
D_FF = 2816
PLE_DIM = 256
IN_COLS = 2956
EPS = 1e-6
LRU_C = 8.0
LANE = 128
V7X_VMEM_LIMIT = 56 * 1024 * 1024

PW = 3072
OFF_B, OFF_C, OFF_LX, OFF_LG, OFF_SM, OFF_Z, OFF_XS, OFF_Q, OFF_K, OFF_V = (
    0, 256, 512, 768, 1024, 1152, 1536, 1920, 2304, 2688)
FOX_LANE0 = 8

ADAM_LR, ADAM_B1, ADAM_B2, ADAM_EPS, ADAM_WD, ADAM_STEP = 0.001, 0.9, 0.999, 1e-08, 0.01, 10


def _iota(shape, dim):
    return lax.broadcasted_iota(jnp.int32, shape, dim)


class Carried:
    def __init__(self):
        self.offers, self.results = {}, {}

    def offer(self, call_name, make_exchange):
        self.offers[call_name] = make_exchange

    def take(self, call_name):
        make = self.offers.pop(call_name, None)
        return None if make is None else make()

    def deliver(self, call_name, results):
        self.results[call_name] = results


def _run(call, args, name, comm, carried):
    if comm is None:
        return call(*args)
    own, brought = call(*args)
    carried.deliver(name, brought)
    return own


def _pc(body, *, name, grid, in_specs, out_specs, out_shape, scratch=(), sem=None, comm=None):
    if comm is None:
        return pl.pallas_call(
            body, name=name, grid=grid, in_specs=in_specs, out_specs=out_specs, out_shape=out_shape,
            scratch_shapes=list(scratch),
            compiler_params=pltpu.CompilerParams(dimension_semantics=sem, vmem_limit_bytes=V7X_VMEM_LIMIT))
    single = not isinstance(out_shape, (list, tuple))
    out_specs_l = [out_specs] if single else list(out_specs)
    out_shape_l = [out_shape] if single else list(out_shape)
    n_in, n_out, n_scr, n_ci, n_co = len(in_specs), len(out_shape_l), len(scratch), len(comm.inputs), len(comm.out_shapes)

    def hosted(*refs):
        ins, cins = refs[:n_in], refs[n_in:n_in + n_ci]
        outs, couts = refs[n_in + n_ci:n_in + n_ci + n_out], refs[n_in + n_ci + n_out:n_in + n_ci + n_out + n_co]
        rest = refs[n_in + n_ci + n_out + n_co:]
        scr, csems = rest[:n_scr], rest[n_scr:]
        ids = [pl.program_id(d) for d in range(len(grid))]
        first = functools.reduce(jnp.logical_and, [i == 0 for i in ids])
        last = functools.reduce(jnp.logical_and, [i == g - 1 for i, g in zip(ids, grid)])

        @pl.when(first)
        def _():
            comm.start(cins, couts, csems)

        body(*ins, *outs, *scr)

        @pl.when(last)
        def _():
            comm.wait(cins, couts, csems)

    call = pl.pallas_call(
        hosted, name=name, grid=grid, in_specs=list(in_specs) + [ANY] * n_ci,
        out_specs=out_specs_l + [ANY] * n_co, out_shape=out_shape_l + list(comm.out_shapes),
        scratch_shapes=list(scratch) + list(comm.sems),
        input_output_aliases={n_in + a: n_out + b for a, b in comm.aliases.items()},
        compiler_params=pltpu.CompilerParams(dimension_semantics=("arbitrary",) * len(grid),
                                             vmem_limit_bytes=V7X_VMEM_LIMIT))

    def run(*args):
        res = call(*args, *comm.inputs)
        own = res[:n_out]
        return (own[0] if single else own), list(res[n_out:])

    return run


def permute_in_cols(w):
    z = lambda n: jnp.zeros(w.shape[:-1] + (n,), w.dtype)
    s = lambda a, b: w[..., a:b]
    return jnp.concatenate([
        s(768, 1024), s(1024, 1280), s(1286, 1542), s(1542, 1798),
        s(1280, 1286), z(2), s(2950, 2956), z(LANE - 14),
        s(0, 384), s(384, 768), s(1798, 2182), s(2182, 2566), s(2566, 2950)], axis=-1)


def unpermute_in_cols(g):
    s = lambda a, n: g[..., a:a + n]
    return jnp.concatenate([
        s(OFF_Z, 384), s(OFF_XS, 384), s(OFF_B, 256), s(OFF_C, 256), s(OFF_SM, 6),
        s(OFF_LX, 256), s(OFF_LG, 256), s(OFF_Q, 384), s(OFF_K, 384), s(OFF_V, 384),
        s(OFF_SM + FOX_LANE0, 6)], axis=-1)


def _pick(n, cands):
    for c in cands:
        if n % c == 0:
            return c
    return n


def mm(a, b, *, name, ta=False, tb=False, out_dtypes=(F32,), extras=(), col_params=(), partials=0, epilogue=None,
       tm=None, tn=None, tk=None, carried=None):
    bs = list(b) if isinstance(b, (list, tuple)) else [b]
    pair_sum = isinstance(a, (list, tuple))
    a_list = list(a) if pair_sum else [a]
    assert not pair_sum or len(a_list) == len(bs)
    a = a_list[0]
    n_a = len(a_list)
    n_acc = 1 if pair_sum else len(bs)
    extras = [e if isinstance(e, tuple) else (e, 0) for e in extras]
    M = a.shape[1] if ta else a.shape[0]
    K = a.shape[0] if ta else a.shape[1]
    N = bs[0].shape[0] if tb else bs[0].shape[1]
    tm = tm or _pick(M, (1024, 1408, 512, 256, 128))
    tn = tn or _pick(N, (1024, 1408, 768, 512, 256, 128))
    tk = tk or _pick(K, (1024, 1408, 512, 256, 128))
    nm, nn, nk = M // tm, N // tn, K // tk
    n_b, n_ex, n_cp, n_out = len(bs), len(extras), len(col_params), len(out_dtypes)
    a_bytes, b_bytes = n_a * M * K * a.dtype.itemsize, n_b * K * N * bs[0].dtype.itemsize
    rows_inner = a_bytes * nn + b_bytes <= a_bytes + b_bytes * nm

    def ij(g0, g1):
        return (g1, g0) if rows_inner else (g0, g1)

    def body(*rest):
        a_refs, rest = rest[:n_a], rest[n_a:]
        b_refs, rest = rest[:n_b], rest[n_b:]
        in_refs, rest = rest[:n_ex + n_cp], rest[n_ex + n_cp:]
        out_refs, accs = rest[:n_out + partials], rest[n_out + partials:]
        dn = (((0 if ta else 1,), (1 if tb else 0,)), ((), ()))
        dot = lambda x_ref, y_ref: lax.dot_general(x_ref[...].astype(BF16), y_ref[...].astype(BF16), dn,
                                                   preferred_element_type=F32)
        if pair_sum:
            parts = [functools.reduce(lambda u, v: u + v, [dot(x, y) for x, y in zip(a_refs, b_refs)])]
        else:
            parts = [dot(a_refs[0], b_ref) for b_ref in b_refs]

        def finish(rs):
            outs = epilogue(*rs, *[e[...] for e in in_refs]) if epilogue is not None else tuple(rs)
            for o_ref, o in zip(out_refs[:n_out], outs):
                o_ref[...] = o.astype(o_ref.dtype)
            for o_ref, o in zip(out_refs[n_out:], outs[n_out:]):
                o_ref[0] = o

        if nk == 1:
            finish(parts)
            return
        k = pl.program_id(2)

        @pl.when(k == 0)
        def _():
            for acc, part in zip(accs, parts):
                acc[...] = part

        @pl.when(k > 0)
        def _():
            for acc, part in zip(accs, parts):
                acc[...] += part

        @pl.when(k == nk - 1)
        def _():
            finish([acc[...] for acc in accs])

    def a_map(g0, g1, k):
        i, _ = ij(g0, g1)
        return (k, i) if ta else (i, k)

    def b_map(g0, g1, k):
        _, j = ij(g0, g1)
        return (j, k) if tb else (k, j)

    def ex_map(off, g0, g1, k):
        i, j = ij(g0, g1)
        return (i, j + off)

    a_spec = pl.BlockSpec((tk, tm) if ta else (tm, tk), a_map)
    b_spec = pl.BlockSpec((tn, tk) if tb else (tk, tn), b_map)
    mn_spec = pl.BlockSpec((tm, tn), functools.partial(ex_map, 0))
    comm = carried.take(name) if carried is not None else None
    call = _pc(body, name=name, grid=(nn, nm, nk) if rows_inner else (nm, nn, nk),
               in_specs=([a_spec] * n_a + [b_spec] * n_b
                         + [pl.BlockSpec((tm, tn), functools.partial(ex_map, off)) for _, off in extras]
                         + [pl.BlockSpec((1, tn), lambda g0, g1, k: (0, ij(g0, g1)[1]))] * n_cp),
               out_specs=([mn_spec] * n_out
                          + [pl.BlockSpec((1, 1, tn), lambda g0, g1, k: (ij(g0, g1)[0], 0, ij(g0, g1)[1]))] * partials),
               out_shape=[SDS((M, N), dt) for dt in out_dtypes] + [SDS((nm, 1, N), F32)] * partials,
               scratch=[pltpu.VMEM((tm, tn), F32)] * n_acc if nk > 1 else [],
               sem=("parallel", "parallel", "arbitrary"), comm=comm)
    outs = _run(call, (*a_list, *bs, *[e for e, _ in extras], *col_params), name, comm, carried)
    return outs[0] if len(outs) == 1 else outs


def rowwise(name, fn, rows, params, row_outs, acc_outs=(), tr=512):
    rows = [r if isinstance(r, tuple) else (r, 0, r.shape[1]) for r in rows]
    T = rows[0][0].shape[0]
    tr = min(tr, T)
    n_in, n_ro, n_ac = len(rows) + len(params), len(row_outs), len(acc_outs)

    def body(*refs):
        ins, outs = refs[:n_in], refs[n_in:]
        res = fn(*[r[...] for r in ins])
        if not isinstance(res, (tuple, list)):
            res = (res,)
        for k in range(n_ro):
            outs[k][...] = res[k].astype(outs[k].dtype)
        if n_ac:
            i = pl.program_id(0)

            @pl.when(i == 0)
            def _():
                for k in range(n_ac):
                    outs[n_ro + k][...] = res[n_ro + k]

            @pl.when(i > 0)
            def _():
                for k in range(n_ac):
                    outs[n_ro + k][...] += res[n_ro + k]

    in_specs = ([pl.BlockSpec((tr, w), functools.partial(lambda cb, i: (i, cb), cb)) for (_, cb, w) in rows]
                + [pl.BlockSpec(p.shape, lambda i: (0, 0)) for p in params])
    out_specs = ([pl.BlockSpec((tr, c), lambda i: (i, 0)) for (c, _) in row_outs]
                 + [pl.BlockSpec((1, c), lambda i: (0, 0)) for c in acc_outs])
    out_shape = [SDS((T, c), dt) for (c, dt) in row_outs] + [SDS((1, c), F32) for c in acc_outs]
    outs = _pc(body, name=name, grid=(T // tr,), in_specs=in_specs, out_specs=out_specs, out_shape=out_shape,
               sem=("arbitrary",) if n_ac else ("parallel",))(*[r[0] for r in rows], *params)
    return outs[0] if len(outs) == 1 else outs


def _rms(x, g):
    return x * lax.rsqrt(jnp.mean(x * x, axis=-1, keepdims=True) + EPS) * g


def _softplus(x):
    return jnp.maximum(x, 0.0) + jnp.log(1.0 + jnp.exp(-jnp.abs(x)))


def _silu(x):
    return x * jax.nn.sigmoid(x)


def _gelu(x):
    return 0.5 * x * (1.0 + jnp.tanh(math.sqrt(2.0 / math.pi) * (x + 0.044715 * (x * x * x))))


def _neg_expm1(x):
    series = x * (1 + x / 2 * (1 + x / 3 * (1 + x / 4 * (1 + x / 5 * (1 + x / 6 * (1 + x / 7))))))
    return -jnp.where(jnp.abs(x) < 0.3, series, jnp.exp(x) - 1.0)


def _swiglu(gu):
    return _silu(gu[:, :D_FF]) * gu[:, D_FF:]


def _ple(pg, pp, b):
    return jax.nn.sigmoid(pg + b) * pp


def _ssd_elt(small, xs_act, dtbias_row, alog_row):
    lane = _iota(small.shape, 1)
    dt = _softplus(small + dtbias_row)
    adt = jnp.where(lane < N_HEADS, -jnp.exp(alog_row) * dt, 0.0)
    head = _iota(xs_act.shape, 1) // HEAD_DIM
    dt_exp = jnp.zeros_like(xs_act)
    for h in range(N_HEADS):
        dth = jnp.sum(jnp.where(lane == h, dt, 0.0), axis=1, keepdims=True)
        dt_exp = dt_exp + jnp.where(head == h, dth, 0.0)
    return adt, xs_act * dt_exp


def _fox_elt(small, bf_row):
    lane = _iota(small.shape, 1)
    keep = (lane >= FOX_LANE0) & (lane < FOX_LANE0 + N_HEADS)
    return jnp.where(keep, -_softplus(-(small + bf_row)), 0.0)


def _lru_elt(xl, pre, b_ax, lam):
    r = jax.nn.sigmoid(pre[:, :LRU_W] + b_ax[:, :LRU_W])
    i = jax.nn.sigmoid(pre[:, LRU_W:] + b_ax[:, LRU_W:])
    log_a = -LRU_C * r * _softplus(-lam)
    a = jnp.exp(log_a)
    mult = jnp.sqrt(_neg_expm1(2.0 * log_a))
    return a, mult * (i * xl)


def _mix_post(yraw, xs_act, z, hl, lgate, yfox, dexp, g_ssd, g_lru, g_fox):
    y_ssd = _rms((yraw + xs_act * dexp) * _silu(z), g_ssd)
    y_lru = _rms(hl * _gelu(lgate), g_lru)
    y_fox = _rms(yfox, g_fox)
    return jnp.concatenate([y_ssd, y_lru, y_fox], axis=-1)


def _colsum(x):
    return jnp.sum(x, axis=0, keepdims=True)


def _shift_down(x, d):
    if d == 0:
        return x
    return jnp.where(_iota(x.shape, 0) >= d, pltpu.roll(x, d, 0), 0.0)


def _shift_up(x, d):
    if d == 0:
        return x
    s = x.shape[0]
    return jnp.where(_iota(x.shape, 0) < s - d, pltpu.roll(x, s - d, 0), 0.0)


def _conv_core(x, w, b):
    y = b + w[3:4, :] * x
    for k in range(3):
        y = y + w[k:k + 1, :] * _shift_down(x, 3 - k)
    return y


def seq_conv(name, src, col, width, w8, b, *, batch, silu, out_dtype):
    T = src.shape[0]
    S = T // batch
    c0 = col // LANE

    def body(x_ref, w_ref, b_ref, o_ref):
        y = _conv_core(x_ref[...], w_ref[...], b_ref[...])
        o_ref[...] = (_silu(y) if silu else y).astype(o_ref.dtype)

    return _pc(body, name=name, grid=(batch, width // LANE),
               in_specs=[pl.BlockSpec((S, LANE), lambda bi, ci: (bi, c0 + ci)),
                         pl.BlockSpec((8, LANE), lambda bi, ci: (0, ci)),
                         pl.BlockSpec((1, LANE), lambda bi, ci: (0, ci))],
               out_specs=pl.BlockSpec((S, LANE), lambda bi, ci: (bi, ci)),
               out_shape=SDS((T, width), out_dtype), sem=("parallel", "parallel"))(src, w8, b)


def seq_conv_bwd(name, src, col, width, w8, b, dy, *, batch, silu):
    T = src.shape[0]
    S = T // batch
    c0 = col // LANE

    def body(x_ref, w_ref, b_ref, dy_ref, dx_ref, dw_ref, db_ref):
        x, w = x_ref[...], w_ref[...]
        dpre = dy_ref[...].astype(F32)
        if silu:
            dpre = jax.vjp(_silu, _conv_core(x, w, b_ref[...]))[1](dpre)[0]
        dx = w[3:4, :] * dpre
        for k in range(3):
            dx = dx + w[k:k + 1, :] * _shift_up(dpre, 3 - k)
        dx_ref[...] = dx.astype(dx_ref.dtype)
        row8 = _iota((8, LANE), 0)
        dw = jnp.zeros((8, LANE), F32)
        for k in range(4):
            dw = dw + jnp.where(row8 == k, _colsum(dpre * _shift_down(x, 3 - k)), 0.0)
        db = _colsum(dpre)
        bi = pl.program_id(1)

        @pl.when(bi == 0)
        def _():
            dw_ref[...] = dw
            db_ref[...] = db

        @pl.when(bi > 0)
        def _():
            dw_ref[...] += dw
            db_ref[...] += db

    return _pc(body, name=name, grid=(width // LANE, batch),
               in_specs=[pl.BlockSpec((S, LANE), lambda ci, bi: (bi, c0 + ci)),
                         pl.BlockSpec((8, LANE), lambda ci, bi: (0, ci)),
                         pl.BlockSpec((1, LANE), lambda ci, bi: (0, ci)),
                         pl.BlockSpec((S, LANE), lambda ci, bi: (bi, ci))],
               out_specs=[pl.BlockSpec((S, LANE), lambda ci, bi: (bi, ci)),
                          pl.BlockSpec((8, LANE), lambda ci, bi: (0, ci)),
                          pl.BlockSpec((1, LANE), lambda ci, bi: (0, ci))],
               out_shape=[SDS((T, width), BF16), SDS((8, width), F32), SDS((1, width), F32)],
               sem=("parallel", "arbitrary"))(src, w8, b, dy)


def _split3_dot(tri, x):
    hi = x.astype(BF16)
    r1 = x - hi.astype(F32)
    mid = r1.astype(BF16)
    lo = (r1 - mid.astype(F32)).astype(BF16)
    d = lambda v: jnp.dot(tri, v, preferred_element_type=F32)
    return d(hi) + d(mid) + d(lo)


def seq_cumsum(name, x, *, batch, reverse=False, nsum=1, trow=None):
    T = x.shape[0]
    S = T // batch
    ch = min(256, S)
    nch = S // ch

    def body(x_ref, o_ref, *maybe_t):
        r, c = _iota((ch, ch), 0), _iota((ch, ch), 1)
        tri = jnp.where((c >= r) if reverse else (c <= r), 1.0, 0.0).astype(BF16)
        carry = jnp.zeros((1, LANE), F32)
        for k in (range(nch - 1, -1, -1) if reverse else range(nch)):
            xc = x_ref[k * ch:(k + 1) * ch, 0:LANE]
            for m in range(1, nsum):
                xc = xc + x_ref[k * ch:(k + 1) * ch, m * LANE:(m + 1) * LANE]
            o_ref[k * ch:(k + 1) * ch, :] = _split3_dot(tri, xc) + carry
            carry = carry + _colsum(xc)
        if trow is not None:
            maybe_t[0][...] = o_ref[...].T[trow:trow + 8, :]

    out_specs = [pl.BlockSpec((S, LANE), lambda bi: (bi, 0))]
    out_shape = [SDS((T, LANE), F32)]
    if trow is not None:
        out_specs.append(pl.BlockSpec((8, S), lambda bi: (bi, 0)))
        out_shape.append(SDS((batch * 8, S), F32))
    outs = _pc(body, name=name, grid=(batch,), in_specs=[pl.BlockSpec((S, LANE * nsum), lambda bi: (bi, 0))],
               out_specs=out_specs, out_shape=out_shape, sem=("parallel",))(x)
    return outs if trow is not None else outs[0]


def lru_scan(name, a, u, *, batch):
    T, W = a.shape
    S = T // batch

    def body(a_ref, u_ref, h_ref):
        row = _iota((8, W), 0)

        def step(g, h):
            off = pl.multiple_of(g * 8, 8)
            at, ut = a_ref[pl.ds(off, 8), :], u_ref[pl.ds(off, 8), :]
            acc = jnp.zeros((8, W), F32)
            for r in range(8):
                h = at[r:r + 1, :] * h + ut[r:r + 1, :]
                acc = jnp.where(row == r, jnp.broadcast_to(h, (8, W)), acc)
            h_ref[pl.ds(off, 8), :] = acc
            return h

        lax.fori_loop(0, S // 8, step, jnp.zeros((1, W), F32))

    spec = pl.BlockSpec((S, W), lambda bi: (bi, 0))
    return _pc(body, name=name, grid=(batch,), in_specs=[spec, spec], out_specs=spec,
               out_shape=SDS((T, W), F32), sem=("parallel",))(a, u)


def lru_scan_bwd(name, a, h, dh, *, batch):
    T, W = a.shape
    S = T // batch
    ng = S // 8

    def body(a_ref, h_ref, dh_ref, da_ref, du_ref):
        row = _iota((8, W), 0)

        def step(k, c):
            g_idx = ng - 1 - k
            off = pl.multiple_of(g_idx * 8, 8)
            offp = pl.multiple_of(jnp.maximum(g_idx - 1, 0) * 8, 8)
            at, ht, dt = a_ref[pl.ds(off, 8), :], h_ref[pl.ds(off, 8), :], dh_ref[pl.ds(off, 8), :]
            hp = jnp.where(g_idx > 0, h_ref[pl.ds(offp, 8), :], 0.0)
            da = jnp.zeros((8, W), F32)
            du = jnp.zeros((8, W), F32)
            for r in range(7, -1, -1):
                g = dt[r:r + 1, :] + c
                hprev = ht[r - 1:r, :] if r > 0 else hp[7:8, :]
                du = jnp.where(row == r, jnp.broadcast_to(g, (8, W)), du)
                da = jnp.where(row == r, jnp.broadcast_to(g * hprev, (8, W)), da)
                c = at[r:r + 1, :] * g
            da_ref[pl.ds(off, 8), :] = da
            du_ref[pl.ds(off, 8), :] = du
            return c

        lax.fori_loop(0, ng, step, jnp.zeros((1, W), F32))

    spec = pl.BlockSpec((S, W), lambda bi: (bi, 0))
    return _pc(body, name=name, grid=(batch,), in_specs=[spec] * 3, out_specs=[spec] * 2,
               out_shape=[SDS((T, W), F32)] * 2, sem=("parallel",))(a, h, dh)


def _nt(a, b):
    return lax.dot_general(a, b, (((1,), (1,)), ((), ())), preferred_element_type=F32)


def _tn(a, b):
    return lax.dot_general(a, b, (((0,), (0,)), ((), ())), preferred_element_type=F32)


def _tile(S, t=256):
    return min(t, S)


def ssd_attn_fwd(name, cm, bm, xd, cum, cum_t, *, batch):
    T = cm.shape[0]
    S = T // batch
    tq = tk = _tile(S)
    nq = S // tq

    def body(c_ref, b_ref, x_ref, cum_ref, cumt_ref, y_ref):
        i = pl.program_id(1)
        cq, cmq = cum_ref[...], c_ref[...]
        rowi, coli = _iota((tq, tk), 0), _iota((tq, tk), 1)
        half = _iota((tk, LANE), 1) // HEAD_DIM

        def step(j, accs, diag):
            off = pl.multiple_of(j * tk, tk)
            bj = b_ref[pl.ds(off, tk), :]
            gm = [_nt(cmq[:, g * LANE:(g + 1) * LANE], bj[:, g * LANE:(g + 1) * LANE]) for g in range(2)]
            ckt = cumt_ref[:, pl.ds(off, tk)]
            new = []
            for p in range(3):
                xp = x_ref[pl.ds(off, tk), p * LANE:(p + 1) * LANE]
                ws, xs = [], []
                for hh in range(2):
                    h = 2 * p + hh
                    seg = cq[:, h:h + 1] - ckt[h:h + 1, :]
                    e = jnp.exp(jnp.where(rowi >= coli, seg, -jnp.inf) if diag else seg)
                    ws.append((gm[h // 3] * e).astype(BF16))
                    xs.append(jnp.where(half == hh, xp, jnp.zeros_like(xp)))
                new.append(accs[p] + jnp.dot(jnp.concatenate(ws, axis=1), jnp.concatenate(xs, axis=0),
                                             preferred_element_type=F32))
            return tuple(new)

        accs = lax.fori_loop(0, i, functools.partial(step, diag=False),
                             tuple(jnp.zeros((tq, LANE), F32) for _ in range(3)))
        accs = step(i, accs, True)
        y_ref[...] = jnp.concatenate(accs, axis=1)

    return _pc(body, name=name, grid=(batch, nq),
               in_specs=[pl.BlockSpec((tq, 256), lambda b, i: (b * nq + i, 0)),
                         pl.BlockSpec((S, 256), lambda b, i: (b, 0)),
                         pl.BlockSpec((S, SSD_W), lambda b, i: (b, 0)),
                         pl.BlockSpec((tq, LANE), lambda b, i: (b * nq + i, 0)),
                         pl.BlockSpec((8, S), lambda b, i: (b, 0))],
               out_specs=pl.BlockSpec((tq, SSD_W), lambda b, i: (b * nq + i, 0)),
               out_shape=SDS((T, SSD_W), F32), sem=("parallel", "parallel"))(cm, bm, xd, cum, cum_t)


def ssd_attn_bwd(name, cm, bm, xd, cum, cum_t, dy, *, batch):
    T = cm.shape[0]
    S = T // batch
    tq = tk = _tile(S, 512)
    nq = S // tq

    def body(c_ref, b_ref, x_ref, cum_ref, cumt_ref, dy_ref, dx_ref, db_ref, dc_ref, dcum_ref, dcumt_ref):
        dx_ref[...] = jnp.zeros_like(dx_ref)
        db_ref[...] = jnp.zeros_like(db_ref)
        dcum_ref[...] = jnp.zeros_like(dcum_ref)
        dcumt_ref[...] = jnp.zeros_like(dcumt_ref)
        rowi, coli = _iota((tq, tk), 0), _iota((tq, tk), 1)
        halfq = _iota((tq, LANE), 1) // HEAD_DIM
        lane_q = _iota((tq, LANE), 1)

        def qblock(i, _):
            qoff = pl.multiple_of(i * tq, tq)
            cq = cum_ref[pl.ds(qoff, tq), :]
            cmq = c_ref[pl.ds(qoff, tq), :]
            dyq = dy_ref[pl.ds(qoff, tq), :]
            dyh = [[jnp.where(halfq == hh, dyq[:, p * LANE:(p + 1) * LANE], 0.0).astype(BF16) for hh in range(2)]
                   for p in range(3)]

            def step(j, carry, diag):
                dcq, rs_acc = carry
                off = pl.multiple_of(j * tk, tk)
                bj = b_ref[pl.ds(off, tk), :]
                gm = [_nt(cmq[:, g * LANE:(g + 1) * LANE], bj[:, g * LANE:(g + 1) * LANE]) for g in range(2)]
                ckt = cumt_ref[:, pl.ds(off, tk)]
                dgm = [jnp.zeros((tq, tk), F32), jnp.zeros((tq, tk), F32)]
                for p in range(3):
                    xp = x_ref[pl.ds(off, tk), p * LANE:(p + 1) * LANE]
                    ws = []
                    for hh in range(2):
                        h = 2 * p + hh
                        seg = cq[:, h:h + 1] - ckt[h:h + 1, :]
                        e = jnp.exp(jnp.where(rowi >= coli, seg, -jnp.inf) if diag else seg)
                        w = gm[h // 3] * e
                        dw = _nt(dyh[p][hh], xp)
                        zz = dw * w
                        rs_acc = rs_acc + jnp.where(lane_q == h, jnp.sum(zz, axis=1, keepdims=True), 0.0)
                        dcumt_ref[h:h + 1, pl.ds(off, tk)] += _colsum(zz)
                        dgm[h // 3] = dgm[h // 3] + dw * e
                        ws.append(w.astype(BF16))
                    dx_ref[pl.ds(off, tk), p * LANE:(p + 1) * LANE] += _tn(
                        jnp.concatenate(ws, axis=0), jnp.concatenate(dyh[p], axis=0))
                new_dcq = []
                for g in range(2):
                    dg = dgm[g].astype(BF16)
                    new_dcq.append(dcq[g] + jnp.dot(dg, bj[:, g * LANE:(g + 1) * LANE], preferred_element_type=F32))
                    db_ref[pl.ds(off, tk), g * LANE:(g + 1) * LANE] += _tn(dg, cmq[:, g * LANE:(g + 1) * LANE])
                return tuple(new_dcq), rs_acc

            carry = lax.fori_loop(
                0, i, functools.partial(step, diag=False),
                ((jnp.zeros((tq, LANE), F32), jnp.zeros((tq, LANE), F32)), jnp.zeros((tq, LANE), F32)))
            dcq, rs_acc = step(i, carry, True)
            dc_ref[pl.ds(qoff, tq), :] = jnp.concatenate(dcq, axis=1)
            dcum_ref[pl.ds(qoff, tq), :] += rs_acc
            return 0

        lax.fori_loop(0, nq, qblock, 0)
        dcum_ref[...] = dcum_ref[...] - dcumt_ref[...].T

    s256 = pl.BlockSpec((S, 256), lambda b: (b, 0))
    s384 = pl.BlockSpec((S, SSD_W), lambda b: (b, 0))
    s128 = pl.BlockSpec((S, LANE), lambda b: (b, 0))
    return _pc(body, name=name, grid=(batch,),
               in_specs=[s256, s256, s384, s128, pl.BlockSpec((8, S), lambda b: (b, 0)), s384],
               out_specs=[s384, s256, s256, s128],
               out_shape=[SDS((T, SSD_W), F32), SDS((T, 256), F32), SDS((T, 256), F32), SDS((T, LANE), F32)],
               scratch=[pltpu.VMEM((LANE, S), F32)], sem=("parallel",))(cm, bm, xd, cum, cum_t, dy)


NEG_BIG = -1e30


def fox_attn_fwd(name, proj, cum, cum_t, *, batch, carried=None):
    T = proj.shape[0]
    S = T // batch
    tq = tk = _tile(S, 512)
    nq = S // tq
    scale = HEAD_DIM ** -0.5
    qb, kb, vb = OFF_Q // LANE, OFF_K // LANE, OFF_V // LANE

    def body(q_ref, k_ref, v_ref, cum_ref, cumt_ref, o_ref, lse_ref):
        p, i = pl.program_id(1), pl.program_id(2)
        cq = cum_ref[...]
        lane_q = _iota((tq, LANE), 1)
        halfq, halfk = lane_q // HEAD_DIM, _iota((tk, LANE), 1) // HEAD_DIM
        qs = q_ref[...] * scale
        qh = [jnp.where(halfq == hh, qs, 0.0).astype(BF16) for hh in range(2)]
        rowi, coli = _iota((tq, tk), 0), _iota((tq, tk), 1)
        cqh = [jnp.sum(jnp.where(lane_q == FOX_LANE0 + 2 * p + hh, cq, 0.0), axis=1, keepdims=True) for hh in range(2)]
        row8 = _iota((8, tk), 0)

        def step(j, carry, diag):
            ms, ls, acc = carry
            off = pl.multiple_of(j * tk, tk)
            kj = k_ref[pl.ds(off, tk), :].astype(BF16)
            vj = v_ref[pl.ds(off, tk), :].astype(BF16)
            ckt = cumt_ref[:, pl.ds(off, tk)]
            ps, vs, new_m, new_l, alphas = [], [], [], [], []
            for hh in range(2):
                ck = jnp.sum(jnp.where(row8 == 2 * p + hh, ckt, 0.0), axis=0, keepdims=True)
                logits = _nt(qh[hh], kj) + (cqh[hh] - ck)
                if diag:
                    logits = jnp.where(rowi >= coli, logits, -jnp.inf)
                m = jnp.maximum(ms[hh], jnp.max(logits, axis=1, keepdims=True))
                alpha = jnp.exp(ms[hh] - m)
                pr = jnp.exp(logits - m)
                new_m.append(m)
                new_l.append(alpha * ls[hh] + jnp.sum(pr, axis=1, keepdims=True))
                alphas.append(alpha)
                ps.append(pr.astype(BF16))
                vs.append(jnp.where(halfk == hh, vj, jnp.zeros_like(vj)))
            acc = acc * jnp.where(halfq == 0, alphas[0], alphas[1]) + jnp.dot(
                jnp.concatenate(ps, axis=1), jnp.concatenate(vs, axis=0), preferred_element_type=F32)
            return tuple(new_m), tuple(new_l), acc

        init = ((jnp.full((tq, 1), NEG_BIG, F32),) * 2, (jnp.zeros((tq, 1), F32),) * 2, jnp.zeros((tq, LANE), F32))
        ms, ls, acc = step(i, lax.fori_loop(0, i, functools.partial(step, diag=False), init), True)
        o_ref[...] = acc / jnp.where(halfq == 0, ls[0], ls[1])
        lse_ref[...] = (jnp.where(lane_q == 0, ms[0] + jnp.log(ls[0]), 0.0)
                        + jnp.where(lane_q == 1, ms[1] + jnp.log(ls[1]), 0.0))

    comm = carried.take(name) if carried is not None else None
    call = _pc(body, name=name, grid=(batch, 3, nq),
               in_specs=[pl.BlockSpec((tq, LANE), lambda b, p, i: (b * nq + i, qb + p)),
                         pl.BlockSpec((S, LANE), lambda b, p, i: (b, kb + p)),
                         pl.BlockSpec((S, LANE), lambda b, p, i: (b, vb + p)),
                         pl.BlockSpec((tq, LANE), lambda b, p, i: (b * nq + i, 0)),
                         pl.BlockSpec((8, S), lambda b, p, i: (b, 0))],
               out_specs=[pl.BlockSpec((tq, LANE), lambda b, p, i: (b * nq + i, p))] * 2,
               out_shape=[SDS((T, FOX_W), F32)] * 2, sem=("parallel", "parallel", "parallel"), comm=comm)
    return _run(call, (proj, proj, proj, cum, cum_t), name, comm, carried)


def fox_attn_bwd(name, proj, o, do, lse, cum, cum_t, *, batch, carried=None):
    T = proj.shape[0]
    S = T // batch
    tq = tk = _tile(S, 512)
    nq = S // tq
    scale = HEAD_DIM ** -0.5
    qb, kb, vb = OFF_Q // LANE, OFF_K // LANE, OFF_V // LANE

    def body(q_ref, k_ref, v_ref, o_ref, do_ref, lse_ref, cum_ref, cumt_ref,
             dq_ref, dk_ref, dv_ref, dcum_ref, dk_acc, dv_acc, dcumt_ref):
        p = pl.program_id(1)
        dk_acc[...] = jnp.zeros_like(dk_acc)
        dv_acc[...] = jnp.zeros_like(dv_acc)
        dcum_ref[...] = jnp.zeros_like(dcum_ref)
        dcumt_ref[...] = jnp.zeros_like(dcumt_ref)
        lane_q = _iota((tq, LANE), 1)
        halfq, halfk = lane_q // HEAD_DIM, _iota((tk, LANE), 1) // HEAD_DIM
        rowi, coli = _iota((tq, tk), 0), _iota((tq, tk), 1)
        row8 = _iota((8, tk), 0)

        def qblock(i, _):
            qoff = pl.multiple_of(i * tq, tq)
            cq = cum_ref[pl.ds(qoff, tq), :]
            qs = q_ref[pl.ds(qoff, tq), :] * scale
            doq = do_ref[pl.ds(qoff, tq), :]
            lse = lse_ref[pl.ds(qoff, tq), :]
            delta = doq * o_ref[pl.ds(qoff, tq), :]
            qh, doh, cqh, lseh, dlt = [], [], [], [], []
            for hh in range(2):
                qh.append(jnp.where(halfq == hh, qs, 0.0).astype(BF16))
                doh.append(jnp.where(halfq == hh, doq, 0.0).astype(BF16))
                cqh.append(jnp.sum(jnp.where(lane_q == FOX_LANE0 + 2 * p + hh, cq, 0.0), axis=1, keepdims=True))
                lseh.append(jnp.sum(jnp.where(lane_q == hh, lse, 0.0), axis=1, keepdims=True))
                dlt.append(jnp.sum(jnp.where(halfq == hh, delta, 0.0), axis=1, keepdims=True))

            def step(j, carry, diag):
                dq, rs = carry
                off = pl.multiple_of(j * tk, tk)
                kj = k_ref[pl.ds(off, tk), :].astype(BF16)
                vj = v_ref[pl.ds(off, tk), :].astype(BF16)
                ckt = cumt_ref[:, pl.ds(off, tk)]
                dss, prs, ks = [], [], []
                for hh in range(2):
                    ck = jnp.sum(jnp.where(row8 == 2 * p + hh, ckt, 0.0), axis=0, keepdims=True)
                    logits = _nt(qh[hh], kj) + ((cqh[hh] - lseh[hh]) - ck)
                    if diag:
                        logits = jnp.where(rowi >= coli, logits, -jnp.inf)
                    pr = jnp.exp(logits)
                    ds = pr * (_nt(doh[hh], vj) - dlt[hh])
                    rs = rs + jnp.where(lane_q == FOX_LANE0 + 2 * p + hh, jnp.sum(ds, axis=1, keepdims=True), 0.0)
                    cs = _colsum(ds)
                    dcumt_ref[0:8, pl.ds(off, tk)] += jnp.where(row8 == 2 * p + hh, cs, 0.0)
                    dss.append(ds.astype(BF16))
                    prs.append(pr.astype(BF16))
                    ks.append(jnp.where(halfk == hh, kj, jnp.zeros_like(kj)))
                dq = dq + jnp.dot(jnp.concatenate(dss, axis=1), jnp.concatenate(ks, axis=0), preferred_element_type=F32)
                dk_acc[pl.ds(off, tk), :] += _tn(jnp.concatenate(dss, axis=0), jnp.concatenate(qh, axis=0))
                dv_acc[pl.ds(off, tk), :] += _tn(jnp.concatenate(prs, axis=0), jnp.concatenate(doh, axis=0))
                return dq, rs

            carry = lax.fori_loop(0, i, functools.partial(step, diag=False),
                                  (jnp.zeros((tq, LANE), F32), jnp.zeros((tq, LANE), F32)))
            dq, rs = step(i, carry, True)
            dq_ref[pl.ds(qoff, tq), :] = (dq * scale).astype(dq_ref.dtype)
            dcum_ref[pl.ds(qoff, tq), :] += rs
            return 0

        lax.fori_loop(0, nq, qblock, 0)
        dk_ref[...] = dk_acc[...].astype(dk_ref.dtype)
        dv_ref[...] = dv_acc[...].astype(dv_ref.dtype)
        dct = dcumt_ref[...].T
        dcum_ref[...] = dcum_ref[...] - pltpu.roll(dct, FOX_LANE0, 1)

    sp = lambda c0: pl.BlockSpec((S, LANE), lambda b, p: (b, c0 + p))
    s0 = pl.BlockSpec((S, LANE), lambda b, p: (b, 0))
    comm = carried.take(name) if carried is not None else None
    call = _pc(body, name=name, grid=(batch, 3),
               in_specs=[sp(qb), sp(kb), sp(vb), sp(0), sp(0), sp(0), s0, pl.BlockSpec((8, S), lambda b, p: (b, 0))],
               out_specs=[sp(0)] * 4,
               out_shape=[SDS((T, FOX_W), BF16)] * 3 + [SDS((T, FOX_W), F32)],
               scratch=[pltpu.VMEM((S, LANE), F32), pltpu.VMEM((S, LANE), F32), pltpu.VMEM((LANE, S), F32)],
               sem=("parallel", "parallel"), comm=comm)
    return _run(call, (proj, proj, proj, o, do, lse, cum, cum_t), name, comm, carried)


def _row(v, width=None, at=0):
    v = v.astype(F32)
    width = width or v.shape[0]
    return jnp.pad(v, (at, width - at - v.shape[0]))[None, :]


def _pad8(w4):
    return jnp.pad(w4.astype(F32), ((0, 4), (0, 0)))


def _block_diag(w):
    out = jnp.zeros((LRU_W, LRU_W), w.dtype)
    for g in range(4):
        out = lax.dynamic_update_slice(out, w[g], (g * 64, g * 64))
    return out


def prep_layer(f):
    cw, cb = f["ssd_conv_w"], f["ssd_conv_b"]
    return dict(
        win=permute_in_cols(f["w_in"]), wout=f["w_out"], wg=f["w_gate"], wu=f["w_up"], wd=f["w_down"], wpg=f["w_ple_gate"], wpp=f["w_ple_proj"],
        wax=jnp.concatenate([_block_diag(f["lru_w_a"]), _block_diag(f["lru_w_x"])], axis=1),
        g1=_row(f["norm1_g"]), g2=_row(f["norm2_g"]), g3=_row(f["norm3_g"]),
        cw_xs=_pad8(cw[:, :384]), cb_xs=_row(cb[:384]), cw_b=_pad8(cw[:, 384:640]), cb_b=_row(cb[384:640]),
        cw_c=_pad8(cw[:, 640:]), cb_c=_row(cb[640:]), cw_l=_pad8(f["lru_conv_w"]), cb_l=_row(f["lru_conv_b"]),
        dtbias_row=_row(f["ssd_dt_bias"], LANE), alog_row=_row(f["ssd_a_log"], LANE),
        dexp=jnp.repeat(f["ssd_d"].astype(F32), HEAD_DIM)[None, :], g_ssd=_row(f["ssd_norm_g"]),
        b_ax=_row(jnp.concatenate([f["lru_b_a"], f["lru_b_x"]])), lam=_row(f["lru_lambda"]), g_lru=_row(f["lru_norm_g"]),
        bf_row=_row(f["fox_b_f"], LANE, FOX_LANE0), g_fox=_row(f["fox_norm_g"]), b_pg=_row(f["b_ple_gate"]))


def unprep_grads(g):
    blocks = lambda m: jnp.stack([m[i * 64:(i + 1) * 64, i * 64:(i + 1) * 64] for i in range(4)])
    return dict(
        norm1_g=g["g1"][0], w_in=unpermute_in_cols(g["win"]),
        ssd_conv_w=jnp.concatenate([g["cw_xs"][:4], g["cw_b"][:4], g["cw_c"][:4]], axis=1),
        ssd_conv_b=jnp.concatenate([g["cb_xs"][0], g["cb_b"][0], g["cb_c"][0]]),
        ssd_dt_bias=g["dtbias_row"][0, :N_HEADS], ssd_a_log=g["alog_row"][0, :N_HEADS],
        ssd_d=jnp.sum(g["dexp"].reshape(N_HEADS, HEAD_DIM), axis=1), ssd_norm_g=g["g_ssd"][0],
        lru_conv_w=g["cw_l"][:4], lru_conv_b=g["cb_l"][0],
        lru_w_a=blocks(g["wax"][:, :LRU_W]), lru_b_a=g["b_ax"][0, :LRU_W],
        lru_w_x=blocks(g["wax"][:, LRU_W:]), lru_b_x=g["b_ax"][0, LRU_W:],
        lru_lambda=g["lam"][0], lru_norm_g=g["g_lru"][0],
        fox_b_f=g["bf_row"][0, FOX_LANE0:FOX_LANE0 + N_HEADS], fox_norm_g=g["g_fox"][0],
        w_out=g["wout"], norm2_g=g["g2"][0], w_gate=g["wg"], w_up=g["wu"], w_down=g["wd"],
        norm3_g=g["g3"][0], w_ple_gate=g["wpg"], b_ple_gate=g["b_pg"][0], w_ple_proj=g["wpp"])


def _view(a, off, width):
    return (a, off // width, width)


def _add_epilogue(acc, e):
    return (acc + e,)


def mixer_fwd(proj, w, batch, tag, carried=None):
    sm = _view(proj, OFF_SM, LANE)
    conv = functools.partial(seq_conv, batch=batch)
    cmc = conv(f"{tag}_conv_c", proj, OFF_C, 256, w["cw_c"], w["cb_c"], silu=True, out_dtype=BF16)
    bmc = conv(f"{tag}_conv_b", proj, OFF_B, 256, w["cw_b"], w["cb_b"], silu=True, out_dtype=BF16)
    xs_act = conv(f"{tag}_conv_xs", proj, OFF_XS, SSD_W, w["cw_xs"], w["cb_xs"], silu=True, out_dtype=F32)
    xl = conv(f"{tag}_conv_l", proj, OFF_LX, LRU_W, w["cw_l"], w["cb_l"], silu=False, out_dtype=F32)
    adt, xd = rowwise(f"{tag}_ssd_elt", _ssd_elt, [sm, xs_act], [w["dtbias_row"], w["alog_row"]],
                      [(LANE, F32), (SSD_W, BF16)])
    cum_a, cum_at = seq_cumsum(f"{tag}_cum_a", adt, batch=batch, trow=0)
    yraw = ssd_attn_fwd(f"{tag}_ssd_fwd", cmc, bmc, xd, cum_a, cum_at, batch=batch)
    logf = rowwise(f"{tag}_fox_elt", _fox_elt, [sm], [w["bf_row"]], [(LANE, F32)])
    cum_f, cum_ft = seq_cumsum(f"{tag}_cum_f", logf, batch=batch, trow=FOX_LANE0)
    o, lse = fox_attn_fwd(f"{tag}_fox_fwd", proj, cum_f, cum_ft, batch=batch, carried=carried)
    pre = mm(xl, w["wax"], name=f"{tag}_mm_lru_gates")
    a, u = rowwise(f"{tag}_lru_elt", _lru_elt, [xl, pre], [w["b_ax"], w["lam"]], [(LRU_W, F32), (LRU_W, F32)])
    hl = lru_scan(f"{tag}_lru_scan", a, u, batch=batch)
    ycat = rowwise(f"{tag}_mix_post", _mix_post,
                   [yraw, xs_act, _view(proj, OFF_Z, SSD_W), hl, _view(proj, OFF_LG, LRU_W), o],
                   [w["dexp"], w["g_ssd"], w["g_lru"], w["g_fox"]], [(D_MODEL, BF16)], tr=256)
    saved = dict(cmc=cmc, bmc=bmc, xs_act=xs_act, xl=xl, xd=xd, cum_a=cum_a, cum_at=cum_at, yraw=yraw,
                 cum_f=cum_f, cum_ft=cum_ft, o=o, lse=lse, pre=pre, a=a, hl=hl)
    return ycat, saved


def mixer_bwd(dycat, proj, w, s, batch, tag, carried=None):
    sm = _view(proj, OFF_SM, LANE)
    g = {}

    def post_bwd(yraw, xs_act, z, hl, lg, o, dyc, dexp, g_ssd, g_lru, g_fox):
        return jax.vjp(_mix_post, yraw, xs_act, z, hl, lg, o, dexp, g_ssd, g_lru, g_fox)[1](dyc)

    (dyraw, dxs1, dz, dhl, dlg, do, g["dexp"], g["g_ssd"], g["g_lru"], g["g_fox"]) = rowwise(
        f"{tag}_mix_post_bwd", post_bwd,
        [s["yraw"], s["xs_act"], _view(proj, OFF_Z, SSD_W), s["hl"], _view(proj, OFF_LG, LRU_W), s["o"], dycat],
        [w["dexp"], w["g_ssd"], w["g_lru"], w["g_fox"]],
        [(SSD_W, F32), (SSD_W, F32), (SSD_W, BF16), (LRU_W, F32), (LRU_W, BF16), (FOX_W, F32)],
        [SSD_W, SSD_W, LRU_W, FOX_W], tr=256)

    dq, dk, dv, dcum3 = fox_attn_bwd(f"{tag}_fox_bwd", proj, s["o"], do, s["lse"], s["cum_f"], s["cum_ft"], batch=batch,
                                     carried=carried)
    dlogf = seq_cumsum(f"{tag}_rcum_f", dcum3, batch=batch, reverse=True, nsum=3)

    dxd, dbm, dcm, dcum_a = ssd_attn_bwd(f"{tag}_ssd_bwd", s["cmc"], s["bmc"], s["xd"], s["cum_a"], s["cum_at"], dyraw,
                                         batch=batch)
    dadt = seq_cumsum(f"{tag}_rcum_a", dcum_a, batch=batch, reverse=True)

    def ssd_elt_bwd(small, xs_act, dadt_, dxd_, dxs1_, dtbias, alog):
        dsm, dxs, ddtb, dalog = jax.vjp(_ssd_elt, small, xs_act, dtbias, alog)[1]((dadt_, dxd_))
        return dsm, dxs + dxs1_, ddtb, dalog

    dsm_s, dxs_act, g["dtbias_row"], g["alog_row"] = rowwise(
        f"{tag}_ssd_elt_bwd", ssd_elt_bwd, [sm, s["xs_act"], dadt, dxd, dxs1], [w["dtbias_row"], w["alog_row"]],
        [(LANE, F32), (SSD_W, F32)], [LANE, LANE])

    def fox_elt_bwd(small, dlogf_, dsm_s_, bf_row):
        dsm, dbf = jax.vjp(_fox_elt, small, bf_row)[1](dlogf_)
        return dsm + dsm_s_, dbf

    dsm, g["bf_row"] = rowwise(f"{tag}_fox_elt_bwd", fox_elt_bwd, [sm, dlogf, dsm_s], [w["bf_row"]],
                               [(LANE, BF16)], [LANE])

    cbwd = functools.partial(seq_conv_bwd, batch=batch)
    dxs_raw, g["cw_xs"], g["cb_xs"] = cbwd(f"{tag}_conv_xs_bwd", proj, OFF_XS, SSD_W, w["cw_xs"], w["cb_xs"], dxs_act, silu=True)
    db_raw, g["cw_b"], g["cb_b"] = cbwd(f"{tag}_conv_b_bwd", proj, OFF_B, 256, w["cw_b"], w["cb_b"], dbm, silu=True)
    dc_raw, g["cw_c"], g["cb_c"] = cbwd(f"{tag}_conv_c_bwd", proj, OFF_C, 256, w["cw_c"], w["cb_c"], dcm, silu=True)

    da, du = lru_scan_bwd(f"{tag}_lru_scan_bwd", s["a"], s["hl"], dhl, batch=batch)

    def lru_elt_bwd(xl, pre, da_, du_, b_ax, lam):
        return jax.vjp(_lru_elt, xl, pre, b_ax, lam)[1]((da_, du_))

    dxl1, dpre, g["b_ax"], g["lam"] = rowwise(
        f"{tag}_lru_elt_bwd", lru_elt_bwd, [s["xl"], s["pre"], da, du], [w["b_ax"], w["lam"]],
        [(LRU_W, F32), (2 * LRU_W, BF16)], [2 * LRU_W, LRU_W])
    g["wax"] = mm(s["xl"], dpre, ta=True, name=f"{tag}_mm_dwax")
    dxl = mm(dpre, w["wax"], tb=True, extras=[dxl1], epilogue=_add_epilogue, name=f"{tag}_mm_dxl")
    dlx_raw, g["cw_l"], g["cb_l"] = cbwd(f"{tag}_conv_l_bwd", proj, OFF_LX, LRU_W, w["cw_l"], w["cb_l"], dxl, silu=False)

    dproj = jnp.concatenate([db_raw, dc_raw, dlx_raw, dlg, dsm, dz, dxs_raw, dq, dk, dv], axis=1)
    return dproj, g


def layer_fwd(h0, p_l, w, batch, tag, carried=None):
    u1 = rowwise(f"{tag}_rms1", _rms, [h0], [w["g1"]], [(D_MODEL, BF16)])
    proj = mm(u1, w["win"], name=f"{tag}_mm_in")
    ycat, ms = mixer_fwd(proj, w, batch, tag, carried)
    h1 = mm(ycat, w["wout"], extras=[h0], epilogue=_add_epilogue, name=f"{tag}_mm_out")
    u2 = rowwise(f"{tag}_rms2", _rms, [h1], [w["g2"]], [(D_MODEL, BF16)])
    gate, up, act = mm(u2, [w["wg"], w["wu"]], out_dtypes=(BF16, BF16, BF16), epilogue=_swiglu_epilogue,
                       tm=512, tn=D_FF // 2, name=f"{tag}_mm_gu", carried=carried)
    h2 = mm(act, w["wd"], extras=[h1], epilogue=_add_epilogue, tm=512, tk=D_FF, name=f"{tag}_mm_down")
    u3 = rowwise(f"{tag}_rms3", _rms, [h2], [w["g3"]], [(D_MODEL, BF16)])
    pp = mm(p_l, w["wpp"], name=f"{tag}_mm_pp")
    h3, pg = mm(u3, w["wpg"], extras=[pp, h2], col_params=[w["b_pg"]], epilogue=_ple_epilogue,
                out_dtypes=(F32, F32), tm=512, name=f"{tag}_mm_pg")
    saved = dict(h0=h0, u1=u1, proj=proj, ycat=ycat, h1=h1, u2=u2, gate=gate, up=up, act=act, h2=h2, u3=u3, pg=pg,
                 pp=pp, mixer=ms)
    return h3, saved


def _swiglu_epilogue(acc_g, acc_u):
    return acc_g, acc_u, _silu(acc_g) * acc_u


def _swiglu_bwd_epilogue(dact, gate, up):
    return jax.vjp(lambda g_, u_: _silu(g_) * u_, gate.astype(F32), up.astype(F32))[1](dact)


def _ple_epilogue(acc, pp, h2, b):
    return h2 + _ple(acc, pp, b), acc


def _rms_bwd_epilogue(du, h, dres, g):
    dh, dg = jax.vjp(_rms, h, g)[1](du)
    return dh + dres, dg


def layer_bwd(dh3, p_l, w, s, batch, tag, carried=None):
    def ple_bwd(pg, pp, dh, b):
        return jax.vjp(_ple, pg, pp, b)[1](dh)

    norm_bwd = dict(epilogue=_rms_bwd_epilogue, partials=1, tm=512, tn=D_MODEL, tb=True)

    d_pg, d_pp, g_bpg = rowwise(f"{tag}_ple_bwd", ple_bwd, [s["pg"], s["pp"], dh3], [w["b_pg"]],
                                [(D_MODEL, BF16), (D_MODEL, BF16)], [D_MODEL])
    g = dict(b_pg=g_bpg)
    g["wpp"] = mm(p_l, d_pp, ta=True, name=f"{tag}_mm_dwpp")
    g["wpg"] = mm(s["u3"], d_pg, ta=True, name=f"{tag}_mm_dwpg")
    dh2, dg3 = mm(d_pg, w["wpg"], extras=[s["h2"], dh3], col_params=[w["g3"]], name=f"{tag}_mm_du3", **norm_bwd)
    g["g3"] = sum_slices(f"{tag}_sum_dg3", dg3)

    d_gate, d_up = mm(dh2, w["wd"], tb=True, extras=[s["gate"], s["up"]], epilogue=_swiglu_bwd_epilogue,
                      out_dtypes=(BF16, BF16), tm=512, tn=D_FF // 2, name=f"{tag}_mm_dact")
    g["wd"] = mm(s["act"], dh2, ta=True, name=f"{tag}_mm_dwd")
    g["wg"] = mm(s["u2"], d_gate, ta=True, name=f"{tag}_mm_dwg")
    g["wu"] = mm(s["u2"], d_up, ta=True, name=f"{tag}_mm_dwu")
    dh1, dg2 = mm([d_gate, d_up], [w["wg"], w["wu"]], extras=[s["h1"], dh2], col_params=[w["g2"]],
                  name=f"{tag}_mm_du2", **norm_bwd)
    g["g2"] = sum_slices(f"{tag}_sum_dg2", dg2)

    dycat = mm(dh1, w["wout"], tb=True, name=f"{tag}_mm_dycat")
    g["wout"] = mm(s["ycat"], dh1, ta=True, name=f"{tag}_mm_dwout")
    dproj, gm = mixer_bwd(dycat, s["proj"], w, s["mixer"], batch, tag, carried)
    g.update(gm)
    g["win"] = mm(s["u1"], dproj, ta=True, name=f"{tag}_mm_dwin")
    dh0, dg1 = mm(dproj, w["win"], extras=[s["h0"], dh1], col_params=[w["g1"]], name=f"{tag}_mm_du1", **norm_bwd)
    g["g1"] = sum_slices(f"{tag}_sum_dg1", dg1)
    return dh0, g


def _loss_fwd_bwd(h, tgt, gf):
    def f(h_, gf_):
        e = _rms(h_, gf_) - tgt
        return 0.5 * jnp.sum(jnp.mean(e * e, axis=-1, keepdims=True), axis=0, keepdims=True)

    loss, vj = jax.vjp(f, h, gf)
    dh, dgf = vj(jnp.ones((1, 1), F32))
    return dh, jnp.broadcast_to(loss, (1, LANE)), dgf


def local_step(x, p, tgt, layers, final_g, carried=None, on_layer_grads=None):
    batch, S, _ = x.shape
    T = batch * S
    h = x.reshape(T, D_MODEL)
    saved, weights = [], []
    for l, w in enumerate(layers):
        w = w() if callable(w) else w
        weights.append(w)
        h, s = layer_fwd(h, p[l].reshape(T, PLE_DIM), w, batch, f"l{l}", carried)
        saved.append(s)
    dh, loss, dgf = rowwise("loss", _loss_fwd_bwd, [h, tgt.reshape(T, D_MODEL)], [_row(final_g)],
                            [(D_MODEL, F32)], [LANE, D_MODEL], tr=256)
    grads = [None] * len(layers)
    for l in reversed(range(len(layers))):
        dh, grads[l] = layer_bwd(dh, p[l].reshape(T, PLE_DIM), weights[l], saved[l], batch, f"l{l}", carried)
        if on_layer_grads is not None:
            on_layer_grads(l, grads[l])
    return loss[0, 0], dh.reshape(batch, S, D_MODEL), grads, dgf[0]


MESH = pl.DeviceIdType.MESH
N_DEV = 8
N_CHIP = 4
ANY = pl.BlockSpec(memory_space=pl.ANY)


def _pos():
    return lax.axis_index("x"), lax.axis_index("y"), lax.axis_index("c")


def _comm_call(body, name, out_shape, n_in, scratch):
    return pl.pallas_call(body, name=name, out_shape=out_shape, in_specs=[ANY] * n_in, out_specs=ANY,
                          scratch_shapes=scratch)


def all_gather8(name, blk):
    def body(x_ref, out_ref, send_sems, recv_sems, local_sem):
        x, y, c = _pos()
        me, sibling = (x, y, c), (x, y, 1 - c)
        chips = [(1 - x, y), (x, 1 - y), (1 - x, 1 - y)]

        def rows(px, py, pcore):
            return out_ref.at[4 * px + 2 * py + pcore]

        def copy(k, block, to, src=None):
            return pltpu.make_async_remote_copy(
                src_ref=rows(*block) if src is None else src, dst_ref=rows(*block),
                send_sem=send_sems.at[k], recv_sem=recv_sems.at[k], device_id=to, device_id_type=MESH)

        mine = pltpu.make_async_copy(x_ref, rows(*me), local_sem)
        mine.start()
        first = [copy(0, me, sibling, src=x_ref)]
        first += [copy(1 + j, me, (*chip, c), src=x_ref) for j, chip in enumerate(chips)]
        for cp in first:
            cp.start()
        passed = [copy(4 + j, (*chip, c), sibling) for j, chip in enumerate(chips)]
        for j, chip in enumerate(chips):
            copy(1 + j, (*chip, c), me).wait_recv()
            passed[j].start()
        copy(0, sibling, me).wait_recv()
        for j, chip in enumerate(chips):
            copy(4 + j, (*chip, 1 - c), me).wait_recv()
        for cp in first + passed:
            cp.wait_send()
        mine.wait()

    return _comm_call(body, name, SDS((N_DEV,) + blk.shape, blk.dtype), 1,
                      [pltpu.SemaphoreType.DMA((7,)), pltpu.SemaphoreType.DMA((7,)), pltpu.SemaphoreType.DMA])(blk)


def _comm_call_list(body, name, out_shapes, n_in, scratch):
    return pl.pallas_call(body, name=name, out_shape=out_shapes, in_specs=[ANY] * n_in, out_specs=[ANY] * len(out_shapes),
                          scratch_shapes=scratch)


class Exchange:
    def __init__(self, inputs, out_shapes, sems, start, wait, aliases=None):
        self.inputs, self.out_shapes, self.sems = list(inputs), list(out_shapes), list(sems)
        self.start, self.wait, self.aliases = start, wait, dict(aliases or {})


def run_exchange(name, ex):
    n_ci, n_co = len(ex.inputs), len(ex.out_shapes)

    def body(*refs):
        cins, couts, csems = refs[:n_ci], refs[n_ci:n_ci + n_co], refs[n_ci + n_co:]
        ex.start(cins, couts, csems)
        ex.wait(cins, couts, csems)

    return pl.pallas_call(body, name=name, out_shape=ex.out_shapes, in_specs=[ANY] * n_ci, out_specs=[ANY] * n_co,
                          scratch_shapes=ex.sems, input_output_aliases=ex.aliases)(*ex.inputs)


def _peers():
    x, y, c = _pos()
    return x, y, c, 2 * x + y, [(1 - x, y), (x, 1 - y), (1 - x, 1 - y)]


def _remote(src, dst, send_sem, recv_sem, to):
    return pltpu.make_async_remote_copy(src_ref=src, dst_ref=dst, send_sem=send_sem, recv_sem=recv_sem,
                                        device_id=to, device_id_type=MESH)


def gather_spread(shards, layer):
    n_t = len(shards)
    halves = [s.shape[1] // 2 for s in shards]

    def copies(cins, couts, sems):
        send_sems, recv_sems, local_sems = sems
        x, y, c, my_chip, chips = _peers()
        local, sends, recvs = [], [], []
        for t in range(n_t):
            h = halves[t]
            src = cins[t].at[layer, pl.ds(c * h, h)]
            mine = couts[t].at[my_chip, pl.ds(c * h, h)]
            local.append(pltpu.make_async_copy(src, mine, local_sems.at[t]))
            sends.append(_remote(src, mine, send_sems.at[0, t], recv_sems.at[0, t], (x, y, 1 - c)))
            recvs.append(_remote(src, couts[t].at[my_chip, pl.ds((1 - c) * h, h)], send_sems.at[0, t],
                                 recv_sems.at[0, t], (x, y, 1 - c)))
            for j, (px, py) in enumerate(chips):
                sends.append(_remote(src, mine, send_sems.at[1 + j, t], recv_sems.at[1 + j, t], (px, py, c)))
                recvs.append(_remote(src, couts[t].at[2 * px + py, pl.ds(c * h, h)], send_sems.at[1 + j, t],
                                     recv_sems.at[1 + j, t], (px, py, c)))
        return local, sends, recvs

    def start(cins, couts, sems):
        local, sends, _ = copies(cins, couts, sems)
        for cp in local + sends:
            cp.start()

    def wait(cins, couts, sems):
        local, sends, recvs = copies(cins, couts, sems)
        for cp in recvs:
            cp.wait_recv()
        for cp in sends:
            cp.wait_send()
        for cp in local:
            cp.wait()

    return Exchange(shards, [SDS((N_CHIP,) + s.shape[1:], s.dtype) for s in shards],
                    [pltpu.SemaphoreType.DMA((4, n_t)), pltpu.SemaphoreType.DMA((4, n_t)),
                     pltpu.SemaphoreType.DMA((n_t,))], start, wait)


def gather_pass_on(slots):
    n_t = len(slots)
    halves = [s.shape[1] // 2 for s in slots]

    def copies(cins, couts, sems):
        send_sems, recv_sems = sems
        x, y, c, my_chip, chips = _peers()
        sends, recvs = [], []
        for t in range(n_t):
            h = halves[t]
            for j, (px, py) in enumerate(chips):
                k = 2 * px + py
                sends.append(_remote(cins[t].at[k, pl.ds(c * h, h)], couts[t].at[k, pl.ds(c * h, h)],
                                     send_sems.at[j, t], recv_sems.at[j, t], (x, y, 1 - c)))
                recvs.append(_remote(cins[t].at[k, pl.ds(c * h, h)], couts[t].at[k, pl.ds((1 - c) * h, h)],
                                     send_sems.at[j, t], recv_sems.at[j, t], (x, y, 1 - c)))
        return sends, recvs

    def start(cins, couts, sems):
        for cp in copies(cins, couts, sems)[0]:
            cp.start()

    def wait(cins, couts, sems):
        sends, recvs = copies(cins, couts, sems)
        for cp in recvs:
            cp.wait_recv()
        for cp in sends:
            cp.wait_send()

    return Exchange(slots, [SDS(s.shape, s.dtype) for s in slots],
                    [pltpu.SemaphoreType.DMA((3, n_t)), pltpu.SemaphoreType.DMA((3, n_t))], start, wait,
                    aliases={t: t for t in range(n_t)})


def chips_exchange(vs):
    n_t = len(vs)

    def copies(cins, couts, sems):
        send_sems, recv_sems, local_sems = sems
        x, y, c, my_chip, chips = _peers()
        local = [pltpu.make_async_copy(cins[t].at[my_chip], couts[t].at[my_chip], local_sems.at[t]) for t in range(n_t)]
        sends, recvs = [], []
        for k, (px, py) in enumerate(chips):
            for t in range(n_t):
                sends.append(_remote(cins[t].at[2 * px + py], couts[t].at[my_chip], send_sems.at[k, t],
                                     recv_sems.at[k, t], (px, py, c)))
                recvs.append(_remote(cins[t].at[my_chip], couts[t].at[2 * px + py], send_sems.at[k, t],
                                     recv_sems.at[k, t], (px, py, c)))
        return local, sends, recvs

    def start(cins, couts, sems):
        local, sends, _ = copies(cins, couts, sems)
        for cp in local + sends:
            cp.start()

    def wait(cins, couts, sems):
        local, sends, recvs = copies(cins, couts, sems)
        for cp in recvs:
            cp.wait_recv()
        for cp in sends:
            cp.wait_send()
        for cp in local:
            cp.wait()

    return Exchange(vs, [SDS(v.shape, v.dtype) for v in vs],
                    [pltpu.SemaphoreType.DMA((3, n_t)), pltpu.SemaphoreType.DMA((3, n_t)),
                     pltpu.SemaphoreType.DMA((n_t,))], start, wait)


def swap_with_sibling(name, vs):
    n_t = len(vs)

    def body(*refs):
        v_refs, out_refs, send_sems, recv_sems = refs[:n_t], refs[n_t:2 * n_t], refs[-2], refs[-1]
        x, y, c = _pos()
        cps = [pltpu.make_async_remote_copy(src_ref=v_refs[t], dst_ref=out_refs[t], send_sem=send_sems.at[t],
                                            recv_sem=recv_sems.at[t], device_id=(x, y, 1 - c), device_id_type=MESH)
               for t in range(n_t)]
        for cp in cps:
            cp.start()
        for cp in cps:
            cp.wait()

    return _comm_call_list(body, name, [SDS(v.shape, v.dtype) for v in vs], n_t,
                           [pltpu.SemaphoreType.DMA((n_t,)), pltpu.SemaphoreType.DMA((n_t,))])(*vs)


_ROW_BLOCKS = (1024, 704, 512, 352, 256, 128, 64, 32, 16, 8)


def sum_slices(name, v, tr=512):
    n, R, C = v.shape
    tr = _pick(R, _ROW_BLOCKS)

    def body(v_ref, o_ref):
        acc = v_ref[0].astype(F32)
        for k in range(1, n):
            acc = acc + v_ref[k].astype(F32)
        o_ref[...] = acc

    return _pc(body, name=name, grid=(R // tr,), in_specs=[pl.BlockSpec((n, tr, C), lambda i: (0, i, 0))],
               out_specs=pl.BlockSpec((tr, C), lambda i: (i, 0)), out_shape=SDS((R, C), F32), sem=("parallel",))(v)


def add_slices(name, a, b, out_dtype):
    n, R, C = a.shape
    tr = _pick(R, _ROW_BLOCKS)

    def body(a_ref, b_ref, o_ref):
        o_ref[...] = (a_ref[...].astype(F32) + b_ref[...].astype(F32)).astype(o_ref.dtype)

    spec = pl.BlockSpec((1, tr, C), lambda k, i: (k, i, 0))
    return _pc(body, name=name, grid=(n, R // tr), in_specs=[spec, spec], out_specs=spec,
               out_shape=SDS(a.shape, out_dtype), sem=("parallel", "parallel"))(a, b)


def adamw(name, w, g, m, v):
    L, R, C = w.shape
    tr = _pick(R, (256, 128, 64, 32, 16, 8))
    c1 = 1.0 / (1.0 - ADAM_B1 ** ADAM_STEP)
    c2 = 1.0 / (1.0 - ADAM_B2 ** ADAM_STEP)

    def body(w_ref, g_ref, m_ref, v_ref, d_ref, nm_ref, nv_ref):
        gv = g_ref[...]
        nm = ADAM_B1 * m_ref[...] + (1.0 - ADAM_B1) * gv
        nv = ADAM_B2 * v_ref[...] + (1.0 - ADAM_B2) * (gv * gv)
        d_ref[...] = -ADAM_LR * ((nm * c1) / (jnp.sqrt(nv * c2) + ADAM_EPS) + ADAM_WD * w_ref[...])
        nm_ref[...] = nm
        nv_ref[...] = nv

    spec = pl.BlockSpec((1, tr, C), lambda l, i: (l, i, 0))
    return _pc(body, name=name, grid=(L, R // tr), in_specs=[spec] * 4, out_specs=[spec] * 3,
               out_shape=[SDS(w.shape, F32)] * 3, sem=("parallel", "parallel"))(w, g, m, v)


WEIGHTS = ["norm1_g", "w_in", "ssd_conv_w", "ssd_conv_b", "ssd_dt_bias", "ssd_a_log", "ssd_d", "ssd_norm_g",
           "lru_conv_w", "lru_conv_b", "lru_w_a", "lru_b_a", "lru_w_x", "lru_b_x", "lru_lambda", "lru_norm_g",
           "fox_b_f", "fox_norm_g", "w_out", "norm2_g", "w_gate", "w_up", "w_down", "norm3_g", "w_ple_gate",
           "b_ple_gate", "w_ple_proj", "final_norm_g"]
BIG = {"w_in": 2, "w_out": 1, "w_gate": 2, "w_up": 2, "w_down": 1, "w_ple_gate": 1, "w_ple_proj": 2}
SHARDED_SMALL = {"ssd_conv_w": 2, "lru_conv_w": 2}
SMALL = [n for n in WEIGHTS if n not in BIG]


def _pack(arrs, rows_multiple):
    flat = jnp.concatenate([a.reshape(-1) for a in arrs])
    per = rows_multiple * LANE
    n = -(-flat.shape[0] // per) * per
    return jnp.pad(flat, (0, n - flat.shape[0])).reshape(n // LANE, LANE)


def _unpack(flat2d, shapes):
    flat = flat2d.reshape(-1)
    out, off = [], 0
    for s in shapes:
        n = int(np.prod(s))
        out.append(flat[off:off + n].reshape(s))
        off += n
    return out


def _gather_shards(name, shards, axes, dtype):
    c = lax.axis_index("c")
    packed = _pack([s.astype(dtype) for s in shards], 32)
    half = packed.shape[0] // 2
    mine = lax.dynamic_slice_in_dim(packed, c * half, half, 0)
    got = all_gather8(name, mine).reshape(N_CHIP, 2 * half, LANE)
    per_chip = [_unpack(got[k], [s.shape for s in shards]) for k in range(N_CHIP)]
    return [jnp.concatenate([per_chip[k][i] for k in range(N_CHIP)], axis=ax) for i, ax in enumerate(axes)]


def kernel(x, p, norm1_g, w_in, ssd_conv_w, ssd_conv_b, ssd_dt_bias, ssd_a_log, ssd_d, ssd_norm_g, lru_conv_w, lru_conv_b, lru_w_a, lru_b_a, lru_w_x, lru_b_x, lru_lambda, lru_norm_g, fox_b_f, fox_norm_g, w_out, norm2_g, w_gate, w_up, w_down, norm3_g, w_ple_gate, b_ple_gate, w_ple_proj, final_norm_g, loss_target, m_norm1_g, m_w_in, m_ssd_conv_w, m_ssd_conv_b, m_ssd_dt_bias, m_ssd_a_log, m_ssd_d, m_ssd_norm_g, m_lru_conv_w, m_lru_conv_b, m_lru_w_a, m_lru_b_a, m_lru_w_x, m_lru_b_x, m_lru_lambda, m_lru_norm_g, m_fox_b_f, m_fox_norm_g, m_w_out, m_norm2_g, m_w_gate, m_w_up, m_w_down, m_norm3_g, m_w_ple_gate, m_b_ple_gate, m_w_ple_proj, m_final_norm_g, v_norm1_g, v_w_in, v_ssd_conv_w, v_ssd_conv_b, v_ssd_dt_bias, v_ssd_a_log, v_ssd_d, v_ssd_norm_g, v_lru_conv_w, v_lru_conv_b, v_lru_w_a, v_lru_b_a, v_lru_w_x, v_lru_b_x, v_lru_lambda, v_lru_norm_g, v_fox_b_f, v_fox_norm_g, v_w_out, v_norm2_g, v_w_gate, v_w_up, v_w_down, v_norm3_g, v_w_ple_gate, v_b_ple_gate, v_w_ple_proj, v_final_norm_g):
    args = (norm1_g, w_in, ssd_conv_w, ssd_conv_b, ssd_dt_bias, ssd_a_log, ssd_d, ssd_norm_g, lru_conv_w, lru_conv_b, lru_w_a, lru_b_a, lru_w_x, lru_b_x, lru_lambda, lru_norm_g, fox_b_f, fox_norm_g, w_out, norm2_g, w_gate, w_up, w_down, norm3_g, w_ple_gate, b_ple_gate, w_ple_proj, final_norm_g)
    m_args = (m_norm1_g, m_w_in, m_ssd_conv_w, m_ssd_conv_b, m_ssd_dt_bias, m_ssd_a_log, m_ssd_d, m_ssd_norm_g, m_lru_conv_w, m_lru_conv_b, m_lru_w_a, m_lru_b_a, m_lru_w_x, m_lru_b_x, m_lru_lambda, m_lru_norm_g, m_fox_b_f, m_fox_norm_g, m_w_out, m_norm2_g, m_w_gate, m_w_up, m_w_down, m_norm3_g, m_w_ple_gate, m_b_ple_gate, m_w_ple_proj, m_final_norm_g)
    v_args = (v_norm1_g, v_w_in, v_ssd_conv_w, v_ssd_conv_b, v_ssd_dt_bias, v_ssd_a_log, v_ssd_d, v_ssd_norm_g, v_lru_conv_w, v_lru_conv_b, v_lru_w_a, v_lru_b_a, v_lru_w_x, v_lru_b_x, v_lru_lambda, v_lru_norm_g, v_fox_b_f, v_fox_norm_g, v_w_out, v_norm2_g, v_w_gate, v_w_up, v_w_down, v_norm3_g, v_w_ple_gate, v_b_ple_gate, v_w_ple_proj, v_final_norm_g)
    w = dict(zip(WEIGHTS, args))
    mom = dict(zip(WEIGHTS, m_args))
    var = dict(zip(WEIGHTS, v_args))
    xi, yi, ci = _pos()
    chip = 2 * xi + yi

    big_names = list(BIG)
    wb = [w[n].astype(BF16) for n in big_names]
    conv_full = dict(zip(SHARDED_SMALL, _gather_shards("gather_conv", [w[n] for n in SHARDED_SMALL],
                                                       list(SHARDED_SMALL.values()), F32)))
    carried = Carried()

    def layer_weights(slots, l):
        f = {n: (conv_full[n][l] if n in conv_full else w[n][l]) for n in SMALL if n != "final_norm_g"}
        for n, s4 in zip(big_names, slots):
            f[n] = (s4.reshape(-1, s4.shape[-1]) if BIG[n] == 1
                    else jnp.concatenate([s4[k] for k in range(N_CHIP)], axis=1))
        return prep_layer(f)

    slots0 = run_exchange("gather0_pass_on", gather_pass_on(run_exchange("gather0_spread", gather_spread(wb, 0))))
    carried.offer("l0_fox_fwd", lambda: gather_spread(wb, 1))
    carried.offer("l0_mm_gu", lambda: gather_pass_on(carried.results["l0_fox_fwd"]))
    layers = [layer_weights(slots0, 0), lambda: layer_weights(carried.results["l0_mm_gu"], 1)]

    def chip_slices(a, n):
        return a.reshape(N_CHIP, -1, a.shape[1]) if BIG[n] == 1 else jnp.stack(jnp.split(a, N_CHIP, axis=1))

    gl, parts = [None] * DEPTH, [None] * DEPTH

    def on_layer_grads(l, g_layer):
        gl[l] = unprep_grads(g_layer)
        keep, give = [], []
        for n in big_names:
            s4 = chip_slices(gl[l][n], n)
            h = s4.shape[1] // 2
            keep.append(lax.dynamic_slice_in_dim(s4, ci * h, h, 1))
            give.append(lax.dynamic_slice_in_dim(s4, (1 - ci) * h, h, 1).astype(BF16))
        got = swap_with_sibling(f"swap_halves{l}", give)
        parts[l] = [add_slices(f"add_sibling{l}_{n}", k_, g_, BF16) for n, k_, g_ in zip(big_names, keep, got)]
        if l == 1:
            carried.offer("l0_fox_bwd", lambda: chips_exchange(parts[1]))

    loss, grad_x, grads, g_final = local_step(x, p, loss_target, layers, final_norm_g, carried, on_layer_grads)
    loss = lax.psum(loss, ("x", "y", "c"))
    arrived = [run_exchange("a2a_chips0", chips_exchange(parts[0])), carried.results["l0_fox_bwd"]]

    gsmall = {n: jnp.stack([gl[l][n] for l in range(DEPTH)]) for n in SMALL if n != "final_norm_g"}
    gsmall["final_norm_g"] = g_final
    small_shapes = [gsmall[n].shape for n in SMALL]
    gs = _pack([gsmall[n] for n in SMALL], 8)
    gs = sum_slices("sum_small", all_gather8("gather_small_grads", gs))
    gsum = dict(zip(SMALL, _unpack(gs, small_shapes)))
    for n, ax in SHARDED_SMALL.items():
        k = gsum[n].shape[ax] // N_CHIP
        gsum[n] = lax.dynamic_slice_in_dim(gsum[n], chip * k, k, ax)

    done = []
    for l in range(DEPTH):
        mine = [sum_slices(f"sum_chips{l}_{n}", a_) for n, a_ in zip(big_names, arrived[l])]
        other = swap_with_sibling(f"swap_results{l}", mine)
        done.append([jnp.concatenate([jnp.where(ci == 0, m_, o_), jnp.where(ci == 0, o_, m_)], axis=0)
                     for m_, o_ in zip(mine, other)])
    for t, n in enumerate(big_names):
        gsum[n] = jnp.stack([done[l][t] for l in range(DEPTH)])

    delta, new_m, new_v = {}, {}, {}
    for n in big_names:
        delta[n], new_m[n], new_v[n] = adamw(f"adamw_{n}", w[n], gsum[n], mom[n], var[n])
    shapes = [w[n].shape for n in SMALL]
    pk = lambda d: _pack([d[n] for n in SMALL], 8)[None]
    ds, ms, vs = adamw("adamw_small", pk(w), pk(gsum), pk(mom), pk(var))
    for d, packed in ((delta, ds), (new_m, ms), (new_v, vs)):
        d.update(zip(SMALL, _unpack(packed[0], shapes)))

    return (loss, grad_x, *[gsum[n] for n in WEIGHTS], *[delta[n] for n in WEIGHTS],
            *[new_m[n] for n in WEIGHTS], *[new_v[n] for n in WEIGHTS])
```

```python
import functools
import math

import jax
import jax.numpy as jnp
import numpy as np
from jax import lax
from jax.experimental import pallas as pl
from jax.experimental.pallas import tpu as pltpu

F32, BF16 = jnp.float32, jnp.bfloat16
SDS = jax.ShapeDtypeStruct

D_MODEL = 1024
DEPTH = 2
HEAD_DIM = 64
N_HEADS = 6
SSD_W, LRU_W, FOX_W = 384, 256, 384
D_FF = 2816
PLE_DIM = 256
IN_COLS = 2956
EPS = 1e-6
LRU_C = 8.0
LANE = 128
V7X_VMEM_LIMIT = 56 * 1024 * 1024

PW = 3072
OFF_B, OFF_C, OFF_LX, OFF_LG, OFF_SM, OFF_Z, OFF_XS, OFF_Q, OFF_K, OFF_V = (
    0, 256, 512, 768, 1024, 1152, 1536, 1920, 2304, 2688)
FOX_LANE0 = 8

ADAM_LR, ADAM_B1, ADAM_B2, ADAM_EPS, ADAM_WD, ADAM_STEP = 0.001, 0.9, 0.999, 1e-08, 0.01, 10


def _iota(shape, dim):
    return lax.broadcasted_iota(jnp.int32, shape, dim)


class Carried:
    def __init__(self):
        self.offers, self.results = {}, {}

    def offer(self, call_name, make_exchange):
        self.offers[call_name] = make_exchange

    def take(self, call_name):
        make = self.offers.pop(call_name, None)
        return None if make is None else make()

    def deliver(self, call_name, results):
        self.results[call_name] = results


class LazyDict(dict):
    def __getitem__(self, key):
        v = dict.__getitem__(self, key)
        if callable(v):
            v = v()
            dict.__setitem__(self, key, v)
        return v


def _run(call, args, name, comm, carried):
    if comm is None:
        return call(*args)
    own, brought = call(*args)
    carried.deliver(name, brought)
    return own


def _pc(body, *, name, grid, in_specs, out_specs, out_shape, scratch=(), sem=None, comm=None):
    if comm is None:
        return pl.pallas_call(
            body, name=name, grid=grid, in_specs=in_specs, out_specs=out_specs, out_shape=out_shape,
            scratch_shapes=list(scratch),
            compiler_params=pltpu.CompilerParams(dimension_semantics=sem, vmem_limit_bytes=V7X_VMEM_LIMIT))
    single = not isinstance(out_shape, (list, tuple))
    out_specs_l = [out_specs] if single else list(out_specs)
    out_shape_l = [out_shape] if single else list(out_shape)
    n_in, n_out, n_scr, n_ci, n_co = len(in_specs), len(out_shape_l), len(scratch), len(comm.inputs), len(comm.out_shapes)

    def hosted(*refs):
        ins, cins = refs[:n_in], refs[n_in:n_in + n_ci]
        outs, couts = refs[n_in + n_ci:n_in + n_ci + n_out], refs[n_in + n_ci + n_out:n_in + n_ci + n_out + n_co]
        rest = refs[n_in + n_ci + n_out + n_co:]
        scr, csems = rest[:n_scr], rest[n_scr:]
        ids = [pl.program_id(d) for d in range(len(grid))]
        first = functools.reduce(jnp.logical_and, [i == 0 for i in ids])
        last = functools.reduce(jnp.logical_and, [i == g - 1 for i, g in zip(ids, grid)])

        @pl.when(first)
        def _():
            comm.start(cins, couts, csems)

        body(*ins, *outs, *scr)

        @pl.when(last)
        def _():
            comm.wait(cins, couts, csems)

    call = pl.pallas_call(
        hosted, name=name, grid=grid, in_specs=list(in_specs) + [ANY] * n_ci,
        out_specs=out_specs_l + [ANY] * n_co, out_shape=out_shape_l + list(comm.out_shapes),
        scratch_shapes=list(scratch) + list(comm.sems),
        input_output_aliases={n_in + a: n_out + b for a, b in comm.aliases.items()},
        compiler_params=pltpu.CompilerParams(dimension_semantics=("arbitrary",) * len(grid),
                                             vmem_limit_bytes=V7X_VMEM_LIMIT))

    def run(*args):
        res = call(*args, *comm.inputs)
        own = res[:n_out]
        return (own[0] if single else own), list(res[n_out:])

    return run


def permute_in_cols(w):
    z = lambda n: jnp.zeros(w.shape[:-1] + (n,), w.dtype)
    s = lambda a, b: w[..., a:b]
    return jnp.concatenate([
        s(768, 1024), s(1024, 1280), s(1286, 1542), s(1542, 1798),
        s(1280, 1286), z(2), s(2950, 2956), z(LANE - 14),
        s(0, 384), s(384, 768), s(1798, 2182), s(2182, 2566), s(2566, 2950)], axis=-1)


def unpermute_in_cols(g):
    s = lambda a, n: g[..., a:a + n]
    return jnp.concatenate([
        s(OFF_Z, 384), s(OFF_XS, 384), s(OFF_B, 256), s(OFF_C, 256), s(OFF_SM, 6),
        s(OFF_LX, 256), s(OFF_LG, 256), s(OFF_Q, 384), s(OFF_K, 384), s(OFF_V, 384),
        s(OFF_SM + FOX_LANE0, 6)], axis=-1)


def _pick(n, cands):
    for c in cands:
        if n % c == 0:
            return c
    return n


def mm(a, b, *, name, ta=False, tb=False, out_dtypes=(F32,), extras=(), col_params=(), partials=0, epilogue=None,
       tm=None, tn=None, tk=None, carried=None):
    bs = list(b) if isinstance(b, (list, tuple)) else [b]
    pair_sum = isinstance(a, (list, tuple))
    a_list = list(a) if pair_sum else [a]
    assert not pair_sum or len(a_list) == len(bs)
    a = a_list[0]
    n_a = len(a_list)
    n_acc = 1 if pair_sum else len(bs)
    extras = [e if isinstance(e, tuple) else (e, 0) for e in extras]
    M = a.shape[1] if ta else a.shape[0]
    K = a.shape[0] if ta else a.shape[1]
    N = bs[0].shape[0] if tb else bs[0].shape[1]
    tm = tm or _pick(M, (1024, 1408, 512, 256, 128))
    tn = tn or _pick(N, (1024, 1408, 768, 512, 256, 128))
    tk = tk or _pick(K, (1024, 1408, 512, 256, 128))
    nm, nn, nk = M // tm, N // tn, K // tk
    n_b, n_ex, n_cp, n_out = len(bs), len(extras), len(col_params), len(out_dtypes)
    a_bytes, b_bytes = n_a * M * K * a.dtype.itemsize, n_b * K * N * bs[0].dtype.itemsize
    rows_inner = a_bytes * nn + b_bytes <= a_bytes + b_bytes * nm

    def ij(g0, g1):
        return (g1, g0) if rows_inner else (g0, g1)

    def body(*rest):
        a_refs, rest = rest[:n_a], rest[n_a:]
        b_refs, rest = rest[:n_b], rest[n_b:]
        in_refs, rest = rest[:n_ex + n_cp], rest[n_ex + n_cp:]
        out_refs, accs = rest[:n_out + partials], rest[n_out + partials:]
        dn = (((0 if ta else 1,), (1 if tb else 0,)), ((), ()))
        dot = lambda x_ref, y_ref: lax.dot_general(x_ref[...].astype(BF16), y_ref[...].astype(BF16), dn,
                                                   preferred_element_type=F32)
        if pair_sum:
            parts = [functools.reduce(lambda u, v: u + v, [dot(x, y) for x, y in zip(a_refs, b_refs)])]
        else:
            parts = [dot(a_refs[0], b_ref) for b_ref in b_refs]

        def finish(rs):
            outs = epilogue(*rs, *[e[...] for e in in_refs]) if epilogue is not None else tuple(rs)
            for o_ref, o in zip(out_refs[:n_out], outs):
                o_ref[...] = o.astype(o_ref.dtype)
            for o_ref, o in zip(out_refs[n_out:], outs[n_out:]):
                o_ref[0] = o

        if nk == 1:
            finish(parts)
            return
        k = pl.program_id(2)

        @pl.when(k == 0)
        def _():
            for acc, part in zip(accs, parts):
                acc[...] = part

        @pl.when(k > 0)
        def _():
            for acc, part in zip(accs, parts):
                acc[...] += part

        @pl.when(k == nk - 1)
        def _():
            finish([acc[...] for acc in accs])

    def a_map(g0, g1, k):
        i, _ = ij(g0, g1)
        return (k, i) if ta else (i, k)

    def b_map(g0, g1, k):
        _, j = ij(g0, g1)
        return (j, k) if tb else (k, j)

    def ex_map(off, g0, g1, k):
        i, j = ij(g0, g1)
        return (i, j + off)

    a_spec = pl.BlockSpec((tk, tm) if ta else (tm, tk), a_map)
    b_spec = pl.BlockSpec((tn, tk) if tb else (tk, tn), b_map)
    mn_spec = pl.BlockSpec((tm, tn), functools.partial(ex_map, 0))
    comm = carried.take(name) if carried is not None else None
    call = _pc(body, name=name, grid=(nn, nm, nk) if rows_inner else (nm, nn, nk),
               in_specs=([a_spec] * n_a + [b_spec] * n_b
                         + [pl.BlockSpec((tm, tn), functools.partial(ex_map, off)) for _, off in extras]
                         + [pl.BlockSpec((1, tn), lambda g0, g1, k: (0, ij(g0, g1)[1]))] * n_cp),
               out_specs=([mn_spec] * n_out
                          + [pl.BlockSpec((1, 1, tn), lambda g0, g1, k: (ij(g0, g1)[0], 0, ij(g0, g1)[1]))] * partials),
               out_shape=[SDS((M, N), dt) for dt in out_dtypes] + [SDS((nm, 1, N), F32)] * partials,
               scratch=[pltpu.VMEM((tm, tn), F32)] * n_acc if nk > 1 else [],
               sem=("parallel", "parallel", "arbitrary"), comm=comm)
    outs = _run(call, (*a_list, *bs, *[e for e, _ in extras], *col_params), name, comm, carried)
    return outs[0] if len(outs) == 1 else outs


def rowwise(name, fn, rows, params, row_outs, acc_outs=(), tr=512):
    rows = [r if isinstance(r, tuple) else (r, 0, r.shape[1]) for r in rows]
    T = rows[0][0].shape[0]
    tr = min(tr, T)
    n_in, n_ro, n_ac = len(rows) + len(params), len(row_outs), len(acc_outs)

    def body(*refs):
        ins, outs = refs[:n_in], refs[n_in:]
        res = fn(*[r[...] for r in ins])
        if not isinstance(res, (tuple, list)):
            res = (res,)
        for k in range(n_ro):
            outs[k][...] = res[k].astype(outs[k].dtype)
        if n_ac:
            i = pl.program_id(0)

            @pl.when(i == 0)
            def _():
                for k in range(n_ac):
                    outs[n_ro + k][...] = res[n_ro + k]

            @pl.when(i > 0)
            def _():
                for k in range(n_ac):
                    outs[n_ro + k][...] += res[n_ro + k]

    in_specs = ([pl.BlockSpec((tr, w), functools.partial(lambda cb, i: (i, cb), cb)) for (_, cb, w) in rows]
                + [pl.BlockSpec(p.shape, lambda i: (0, 0)) for p in params])
    out_specs = ([pl.BlockSpec((tr, c), lambda i: (i, 0)) for (c, _) in row_outs]
                 + [pl.BlockSpec((1, c), lambda i: (0, 0)) for c in acc_outs])
    out_shape = [SDS((T, c), dt) for (c, dt) in row_outs] + [SDS((1, c), F32) for c in acc_outs]
    outs = _pc(body, name=name, grid=(T // tr,), in_specs=in_specs, out_specs=out_specs, out_shape=out_shape,
               sem=("arbitrary",) if n_ac else ("parallel",))(*[r[0] for r in rows], *params)
    return outs[0] if len(outs) == 1 else outs


def _rms(x, g):
    return x * lax.rsqrt(jnp.mean(x * x, axis=-1, keepdims=True) + EPS) * g


def _softplus(x):
    return jnp.maximum(x, 0.0) + jnp.log(1.0 + jnp.exp(-jnp.abs(x)))


def _silu(x):
    return x * jax.nn.sigmoid(x)


def _gelu(x):
    return 0.5 * x * (1.0 + jnp.tanh(math.sqrt(2.0 / math.pi) * (x + 0.044715 * (x * x * x))))


def _neg_expm1(x):
    series = x * (1 + x / 2 * (1 + x / 3 * (1 + x / 4 * (1 + x / 5 * (1 + x / 6 * (1 + x / 7))))))
    return -jnp.where(jnp.abs(x) < 0.3, series, jnp.exp(x) - 1.0)


def _swiglu(gu):
    return _silu(gu[:, :D_FF]) * gu[:, D_FF:]


def _ple(pg, pp, b):
    return jax.nn.sigmoid(pg + b) * pp


def _ssd_elt(small, xs_act, dtbias_row, alog_row):
    lane = _iota(small.shape, 1)
    dt = _softplus(small + dtbias_row)
    adt = jnp.where(lane < N_HEADS, -jnp.exp(alog_row) * dt, 0.0)
    head = _iota(xs_act.shape, 1) // HEAD_DIM
    dt_exp = jnp.zeros_like(xs_act)
    for h in range(N_HEADS):
        dth = jnp.sum(jnp.where(lane == h, dt, 0.0), axis=1, keepdims=True)
        dt_exp = dt_exp + jnp.where(head == h, dth, 0.0)
    return adt, xs_act * dt_exp


def _fox_elt(small, bf_row):
    lane = _iota(small.shape, 1)
    keep = (lane >= FOX_LANE0) & (lane < FOX_LANE0 + N_HEADS)
    return jnp.where(keep, -_softplus(-(small + bf_row)), 0.0)


def _lru_elt(xl, pre, b_ax, lam):
    r = jax.nn.sigmoid(pre[:, :LRU_W] + b_ax[:, :LRU_W])
    i = jax.nn.sigmoid(pre[:, LRU_W:] + b_ax[:, LRU_W:])
    log_a = -LRU_C * r * _softplus(-lam)
    a = jnp.exp(log_a)
    mult = jnp.sqrt(_neg_expm1(2.0 * log_a))
    return a, mult * (i * xl)


def _mix_post(yraw, xs_act, z, hl, lgate, yfox, dexp, g_ssd, g_lru, g_fox):
    y_ssd = _rms((yraw + xs_act * dexp) * _silu(z), g_ssd)
    y_lru = _rms(hl * _gelu(lgate), g_lru)
    y_fox = _rms(yfox, g_fox)
    return jnp.concatenate([y_ssd, y_lru, y_fox], axis=-1)


def _colsum(x):
    return jnp.sum(x, axis=0, keepdims=True)


def _shift_down(x, d):
    if d == 0:
        return x
    return jnp.where(_iota(x.shape, 0) >= d, pltpu.roll(x, d, 0), 0.0)


def _shift_up(x, d):
    if d == 0:
        return x
    s = x.shape[0]
    return jnp.where(_iota(x.shape, 0) < s - d, pltpu.roll(x, s - d, 0), 0.0)


def _conv_core(x, w, b):
    y = b + w[3:4, :] * x
    for k in range(3):
        y = y + w[k:k + 1, :] * _shift_down(x, 3 - k)
    return y


def seq_conv(name, src, col, width, w8, b, *, batch, silu, out_dtype):
    T = src.shape[0]
    S = T // batch
    c0 = col // LANE

    def body(x_ref, w_ref, b_ref, o_ref):
        y = _conv_core(x_ref[...], w_ref[...], b_ref[...])
        o_ref[...] = (_silu(y) if silu else y).astype(o_ref.dtype)

    return _pc(body, name=name, grid=(batch, width // LANE),
               in_specs=[pl.BlockSpec((S, LANE), lambda bi, ci: (bi, c0 + ci)),
                         pl.BlockSpec((8, LANE), lambda bi, ci: (0, ci)),
                         pl.BlockSpec((1, LANE), lambda bi, ci: (0, ci))],
               out_specs=pl.BlockSpec((S, LANE), lambda bi, ci: (bi, ci)),
               out_shape=SDS((T, width), out_dtype), sem=("parallel", "parallel"))(src, w8, b)


def seq_conv_bwd(name, src, col, width, w8, b, dy, *, batch, silu):
    T = src.shape[0]
    S = T // batch
    c0 = col // LANE

    def body(x_ref, w_ref, b_ref, dy_ref, dx_ref, dw_ref, db_ref):
        x, w = x_ref[...], w_ref[...]
        dpre = dy_ref[...].astype(F32)
        if silu:
            dpre = jax.vjp(_silu, _conv_core(x, w, b_ref[...]))[1](dpre)[0]
        dx = w[3:4, :] * dpre
        for k in range(3):
            dx = dx + w[k:k + 1, :] * _shift_up(dpre, 3 - k)
        dx_ref[...] = dx.astype(dx_ref.dtype)
        row8 = _iota((8, LANE), 0)
        dw = jnp.zeros((8, LANE), F32)
        for k in range(4):
            dw = dw + jnp.where(row8 == k, _colsum(dpre * _shift_down(x, 3 - k)), 0.0)
        db = _colsum(dpre)
        bi = pl.program_id(1)

        @pl.when(bi == 0)
        def _():
            dw_ref[...] = dw
            db_ref[...] = db

        @pl.when(bi > 0)
        def _():
            dw_ref[...] += dw
            db_ref[...] += db

    return _pc(body, name=name, grid=(width // LANE, batch),
               in_specs=[pl.BlockSpec((S, LANE), lambda ci, bi: (bi, c0 + ci)),
                         pl.BlockSpec((8, LANE), lambda ci, bi: (0, ci)),
                         pl.BlockSpec((1, LANE), lambda ci, bi: (0, ci)),
                         pl.BlockSpec((S, LANE), lambda ci, bi: (bi, ci))],
               out_specs=[pl.BlockSpec((S, LANE), lambda ci, bi: (bi, ci)),
                          pl.BlockSpec((8, LANE), lambda ci, bi: (0, ci)),
                          pl.BlockSpec((1, LANE), lambda ci, bi: (0, ci))],
               out_shape=[SDS((T, width), BF16), SDS((8, width), F32), SDS((1, width), F32)],
               sem=("parallel", "arbitrary"))(src, w8, b, dy)


def _split3_dot(tri, x):
    hi = x.astype(BF16)
    r1 = x - hi.astype(F32)
    mid = r1.astype(BF16)
    lo = (r1 - mid.astype(F32)).astype(BF16)
    d = lambda v: jnp.dot(tri, v, preferred_element_type=F32)
    return d(hi) + d(mid) + d(lo)


def seq_cumsum(name, x, *, batch, reverse=False, nsum=1, trow=None):
    T = x.shape[0]
    S = T // batch
    ch = min(256, S)
    nch = S // ch

    def body(x_ref, o_ref, *maybe_t):
        r, c = _iota((ch, ch), 0), _iota((ch, ch), 1)
        tri = jnp.where((c >= r) if reverse else (c <= r), 1.0, 0.0).astype(BF16)
        carry = jnp.zeros((1, LANE), F32)
        for k in (range(nch - 1, -1, -1) if reverse else range(nch)):
            xc = x_ref[k * ch:(k + 1) * ch, 0:LANE]
            for m in range(1, nsum):
                xc = xc + x_ref[k * ch:(k + 1) * ch, m * LANE:(m + 1) * LANE]
            o_ref[k * ch:(k + 1) * ch, :] = _split3_dot(tri, xc) + carry
            carry = carry + _colsum(xc)
        if trow is not None:
            maybe_t[0][...] = o_ref[...].T[trow:trow + 8, :]

    out_specs = [pl.BlockSpec((S, LANE), lambda bi: (bi, 0))]
    out_shape = [SDS((T, LANE), F32)]
    if trow is not None:
        out_specs.append(pl.BlockSpec((8, S), lambda bi: (bi, 0)))
        out_shape.append(SDS((batch * 8, S), F32))
    outs = _pc(body, name=name, grid=(batch,), in_specs=[pl.BlockSpec((S, LANE * nsum), lambda bi: (bi, 0))],
               out_specs=out_specs, out_shape=out_shape, sem=("parallel",))(x)
    return outs if trow is not None else outs[0]


def lru_scan(name, a, u, *, batch):
    T, W = a.shape
    S = T // batch

    def body(a_ref, u_ref, h_ref):
        row = _iota((8, W), 0)

        def step(g, h):
            off = pl.multiple_of(g * 8, 8)
            at, ut = a_ref[pl.ds(off, 8), :], u_ref[pl.ds(off, 8), :]
            acc = jnp.zeros((8, W), F32)
            for r in range(8):
                h = at[r:r + 1, :] * h + ut[r:r + 1, :]
                acc = jnp.where(row == r, jnp.broadcast_to(h, (8, W)), acc)
            h_ref[pl.ds(off, 8), :] = acc
            return h

        lax.fori_loop(0, S // 8, step, jnp.zeros((1, W), F32))

    spec = pl.BlockSpec((S, W), lambda bi: (bi, 0))
    return _pc(body, name=name, grid=(batch,), in_specs=[spec, spec], out_specs=spec,
               out_shape=SDS((T, W), F32), sem=("parallel",))(a, u)


def lru_scan_bwd(name, a, h, dh, *, batch):
    T, W = a.shape
    S = T // batch
    ng = S // 8

    def body(a_ref, h_ref, dh_ref, da_ref, du_ref):
        row = _iota((8, W), 0)

        def step(k, c):
            g_idx = ng - 1 - k
            off = pl.multiple_of(g_idx * 8, 8)
            offp = pl.multiple_of(jnp.maximum(g_idx - 1, 0) * 8, 8)
            at, ht, dt = a_ref[pl.ds(off, 8), :], h_ref[pl.ds(off, 8), :], dh_ref[pl.ds(off, 8), :]
            hp = jnp.where(g_idx > 0, h_ref[pl.ds(offp, 8), :], 0.0)
            da = jnp.zeros((8, W), F32)
            du = jnp.zeros((8, W), F32)
            for r in range(7, -1, -1):
                g = dt[r:r + 1, :] + c
                hprev = ht[r - 1:r, :] if r > 0 else hp[7:8, :]
                du = jnp.where(row == r, jnp.broadcast_to(g, (8, W)), du)
                da = jnp.where(row == r, jnp.broadcast_to(g * hprev, (8, W)), da)
                c = at[r:r + 1, :] * g
            da_ref[pl.ds(off, 8), :] = da
            du_ref[pl.ds(off, 8), :] = du
            return c

        lax.fori_loop(0, ng, step, jnp.zeros((1, W), F32))

    spec = pl.BlockSpec((S, W), lambda bi: (bi, 0))
    return _pc(body, name=name, grid=(batch,), in_specs=[spec] * 3, out_specs=[spec] * 2,
               out_shape=[SDS((T, W), F32)] * 2, sem=("parallel",))(a, h, dh)


def _nt(a, b):
    return lax.dot_general(a, b, (((1,), (1,)), ((), ())), preferred_element_type=F32)


def _tn(a, b):
    return lax.dot_general(a, b, (((0,), (0,)), ((), ())), preferred_element_type=F32)


def _tile(S, t=256):
    return min(t, S)


def ssd_attn_fwd(name, cm, bm, xd, cum, cum_t, *, batch, carried=None):
    T = cm.shape[0]
    S = T // batch
    tq = tk = _tile(S)
    nq = S // tq

    def body(c_ref, b_ref, x_ref, cum_ref, cumt_ref, y_ref):
        i = pl.program_id(1)
        cq, cmq = cum_ref[...], c_ref[...]
        rowi, coli = _iota((tq, tk), 0), _iota((tq, tk), 1)
        half = _iota((tk, LANE), 1) // HEAD_DIM

        def step(j, accs, diag):
            off = pl.multiple_of(j * tk, tk)
            bj = b_ref[pl.ds(off, tk), :]
            gm = [_nt(cmq[:, g * LANE:(g + 1) * LANE], bj[:, g * LANE:(g + 1) * LANE]) for g in range(2)]
            ckt = cumt_ref[:, pl.ds(off, tk)]
            new = []
            for p in range(3):
                xp = x_ref[pl.ds(off, tk), p * LANE:(p + 1) * LANE]
                ws, xs = [], []
                for hh in range(2):
                    h = 2 * p + hh
                    seg = cq[:, h:h + 1] - ckt[h:h + 1, :]
                    e = jnp.exp(jnp.where(rowi >= coli, seg, -jnp.inf) if diag else seg)
                    ws.append((gm[h // 3] * e).astype(BF16))
                    xs.append(jnp.where(half == hh, xp, jnp.zeros_like(xp)))
                new.append(accs[p] + jnp.dot(jnp.concatenate(ws, axis=1), jnp.concatenate(xs, axis=0),
                                             preferred_element_type=F32))
            return tuple(new)

        accs = lax.fori_loop(0, i, functools.partial(step, diag=False),
                             tuple(jnp.zeros((tq, LANE), F32) for _ in range(3)))
        accs = step(i, accs, True)
        y_ref[...] = jnp.concatenate(accs, axis=1)

    comm = carried.take(name) if carried is not None else None
    call = _pc(body, name=name, grid=(batch, nq),
               in_specs=[pl.BlockSpec((tq, 256), lambda b, i: (b * nq + i, 0)),
                         pl.BlockSpec((S, 256), lambda b, i: (b, 0)),
                         pl.BlockSpec((S, SSD_W), lambda b, i: (b, 0)),
                         pl.BlockSpec((tq, LANE), lambda b, i: (b * nq + i, 0)),
                         pl.BlockSpec((8, S), lambda b, i: (b, 0))],
               out_specs=pl.BlockSpec((tq, SSD_W), lambda b, i: (b * nq + i, 0)),
               out_shape=SDS((T, SSD_W), F32), sem=("parallel", "parallel"), comm=comm)
    return _run(call, (cm, bm, xd, cum, cum_t), name, comm, carried)


def ssd_attn_bwd(name, cm, bm, xd, cum, cum_t, dy, *, batch, carried=None):
    T = cm.shape[0]
    S = T // batch
    tq = tk = _tile(S, 512)
    nq = S // tq

    def body(c_ref, b_ref, x_ref, cum_ref, cumt_ref, dy_ref, dx_ref, db_ref, dc_ref, dcum_ref, dcumt_ref):
        dx_ref[...] = jnp.zeros_like(dx_ref)
        db_ref[...] = jnp.zeros_like(db_ref)
        dcum_ref[...] = jnp.zeros_like(dcum_ref)
        dcumt_ref[...] = jnp.zeros_like(dcumt_ref)
        rowi, coli = _iota((tq, tk), 0), _iota((tq, tk), 1)
        halfq = _iota((tq, LANE), 1) // HEAD_DIM
        lane_q = _iota((tq, LANE), 1)

        def qblock(i, _):
            qoff = pl.multiple_of(i * tq, tq)
            cq = cum_ref[pl.ds(qoff, tq), :]
            cmq = c_ref[pl.ds(qoff, tq), :]
            dyq = dy_ref[pl.ds(qoff, tq), :]
            dyh = [[jnp.where(halfq == hh, dyq[:, p * LANE:(p + 1) * LANE], 0.0).astype(BF16) for hh in range(2)]
                   for p in range(3)]

            def step(j, carry, diag):
                dcq, rs_acc = carry
                off = pl.multiple_of(j * tk, tk)
                bj = b_ref[pl.ds(off, tk), :]
                gm = [_nt(cmq[:, g * LANE:(g + 1) * LANE], bj[:, g * LANE:(g + 1) * LANE]) for g in range(2)]
                ckt = cumt_ref[:, pl.ds(off, tk)]
                dgm = [jnp.zeros((tq, tk), F32), jnp.zeros((tq, tk), F32)]
                for p in range(3):
                    xp = x_ref[pl.ds(off, tk), p * LANE:(p + 1) * LANE]
                    ws = []
                    for hh in range(2):
                        h = 2 * p + hh
                        seg = cq[:, h:h + 1] - ckt[h:h + 1, :]
                        e = jnp.exp(jnp.where(rowi >= coli, seg, -jnp.inf) if diag else seg)
                        w = gm[h // 3] * e
                        dw = _nt(dyh[p][hh], xp)
                        zz = dw * w
                        rs_acc = rs_acc + jnp.where(lane_q == h, jnp.sum(zz, axis=1, keepdims=True), 0.0)
                        dcumt_ref[h:h + 1, pl.ds(off, tk)] += _colsum(zz)
                        dgm[h // 3] = dgm[h // 3] + dw * e
                        ws.append(w.astype(BF16))
                    dx_ref[pl.ds(off, tk), p * LANE:(p + 1) * LANE] += _tn(
                        jnp.concatenate(ws, axis=0), jnp.concatenate(dyh[p], axis=0))
                new_dcq = []
                for g in range(2):
                    dg = dgm[g].astype(BF16)
                    new_dcq.append(dcq[g] + jnp.dot(dg, bj[:, g * LANE:(g + 1) * LANE], preferred_element_type=F32))
                    db_ref[pl.ds(off, tk), g * LANE:(g + 1) * LANE] += _tn(dg, cmq[:, g * LANE:(g + 1) * LANE])
                return tuple(new_dcq), rs_acc

            carry = lax.fori_loop(
                0, i, functools.partial(step, diag=False),
                ((jnp.zeros((tq, LANE), F32), jnp.zeros((tq, LANE), F32)), jnp.zeros((tq, LANE), F32)))
            dcq, rs_acc = step(i, carry, True)
            dc_ref[pl.ds(qoff, tq), :] = jnp.concatenate(dcq, axis=1)
            dcum_ref[pl.ds(qoff, tq), :] += rs_acc
            return 0

        lax.fori_loop(0, nq, qblock, 0)
        dcum_ref[...] = dcum_ref[...] - dcumt_ref[...].T

    s256 = pl.BlockSpec((S, 256), lambda b: (b, 0))
    s384 = pl.BlockSpec((S, SSD_W), lambda b: (b, 0))
    s128 = pl.BlockSpec((S, LANE), lambda b: (b, 0))
    comm = carried.take(name) if carried is not None else None
    call = _pc(body, name=name, grid=(batch,),
               in_specs=[s256, s256, s384, s128, pl.BlockSpec((8, S), lambda b: (b, 0)), s384],
               out_specs=[s384, s256, s256, s128],
               out_shape=[SDS((T, SSD_W), F32), SDS((T, 256), F32), SDS((T, 256), F32), SDS((T, LANE), F32)],
               scratch=[pltpu.VMEM((LANE, S), F32)], sem=("parallel",), comm=comm)
    return _run(call, (cm, bm, xd, cum, cum_t, dy), name, comm, carried)


NEG_BIG = -1e30


def fox_attn_fwd(name, proj, cum, cum_t, *, batch, carried=None):
    T = proj.shape[0]
    S = T // batch
    tq = tk = _tile(S, 512)
    nq = S // tq
    scale = HEAD_DIM ** -0.5
    qb, kb, vb = OFF_Q // LANE, OFF_K // LANE, OFF_V // LANE

    def body(q_ref, k_ref, v_ref, cum_ref, cumt_ref, o_ref, lse_ref):
        p, i = pl.program_id(1), pl.program_id(2)
        cq = cum_ref[...]
        lane_q = _iota((tq, LANE), 1)
        halfq, halfk = lane_q // HEAD_DIM, _iota((tk, LANE), 1) // HEAD_DIM
        qs = q_ref[...] * scale
        qh = [jnp.where(halfq == hh, qs, 0.0).astype(BF16) for hh in range(2)]
        rowi, coli = _iota((tq, tk), 0), _iota((tq, tk), 1)
        cqh = [jnp.sum(jnp.where(lane_q == FOX_LANE0 + 2 * p + hh, cq, 0.0), axis=1, keepdims=True) for hh in range(2)]
        row8 = _iota((8, tk), 0)

        def step(j, carry, diag):
            ms, ls, acc = carry
            off = pl.multiple_of(j * tk, tk)
            kj = k_ref[pl.ds(off, tk), :].astype(BF16)
            vj = v_ref[pl.ds(off, tk), :].astype(BF16)
            ckt = cumt_ref[:, pl.ds(off, tk)]
            ps, vs, new_m, new_l, alphas = [], [], [], [], []
            for hh in range(2):
                ck = jnp.sum(jnp.where(row8 == 2 * p + hh, ckt, 0.0), axis=0, keepdims=True)
                logits = _nt(qh[hh], kj) + (cqh[hh] - ck)
                if diag:
                    logits = jnp.where(rowi >= coli, logits, -jnp.inf)
                m = jnp.maximum(ms[hh], jnp.max(logits, axis=1, keepdims=True))
                alpha = jnp.exp(ms[hh] - m)
                pr = jnp.exp(logits - m)
                new_m.append(m)
                new_l.append(alpha * ls[hh] + jnp.sum(pr, axis=1, keepdims=True))
                alphas.append(alpha)
                ps.append(pr.astype(BF16))
                vs.append(jnp.where(halfk == hh, vj, jnp.zeros_like(vj)))
            acc = acc * jnp.where(halfq == 0, alphas[0], alphas[1]) + jnp.dot(
                jnp.concatenate(ps, axis=1), jnp.concatenate(vs, axis=0), preferred_element_type=F32)
            return tuple(new_m), tuple(new_l), acc

        init = ((jnp.full((tq, 1), NEG_BIG, F32),) * 2, (jnp.zeros((tq, 1), F32),) * 2, jnp.zeros((tq, LANE), F32))
        ms, ls, acc = step(i, lax.fori_loop(0, i, functools.partial(step, diag=False), init), True)
        o_ref[...] = acc / jnp.where(halfq == 0, ls[0], ls[1])
        lse_ref[...] = (jnp.where(lane_q == 0, ms[0] + jnp.log(ls[0]), 0.0)
                        + jnp.where(lane_q == 1, ms[1] + jnp.log(ls[1]), 0.0))

    comm = carried.take(name) if carried is not None else None
    call = _pc(body, name=name, grid=(batch, 3, nq),
               in_specs=[pl.BlockSpec((tq, LANE), lambda b, p, i: (b * nq + i, qb + p)),
                         pl.BlockSpec((S, LANE), lambda b, p, i: (b, kb + p)),
                         pl.BlockSpec((S, LANE), lambda b, p, i: (b, vb + p)),
                         pl.BlockSpec((tq, LANE), lambda b, p, i: (b * nq + i, 0)),
                         pl.BlockSpec((8, S), lambda b, p, i: (b, 0))],
               out_specs=[pl.BlockSpec((tq, LANE), lambda b, p, i: (b * nq + i, p))] * 2,
               out_shape=[SDS((T, FOX_W), F32)] * 2, sem=("parallel", "parallel", "parallel"), comm=comm)
    return _run(call, (proj, proj, proj, cum, cum_t), name, comm, carried)


def fox_attn_bwd(name, proj, o, do, lse, cum, cum_t, *, batch, carried=None):
    T = proj.shape[0]
    S = T // batch
    tq = tk = _tile(S, 512)
    nq = S // tq
    scale = HEAD_DIM ** -0.5
    qb, kb, vb = OFF_Q // LANE, OFF_K // LANE, OFF_V // LANE

    def body(q_ref, k_ref, v_ref, o_ref, do_ref, lse_ref, cum_ref, cumt_ref,
             dq_ref, dk_ref, dv_ref, dcum_ref, dk_acc, dv_acc, dcumt_ref):
        p = pl.program_id(1)
        dk_acc[...] = jnp.zeros_like(dk_acc)
        dv_acc[...] = jnp.zeros_like(dv_acc)
        dcum_ref[...] = jnp.zeros_like(dcum_ref)
        dcumt_ref[...] = jnp.zeros_like(dcumt_ref)
        lane_q = _iota((tq, LANE), 1)
        halfq, halfk = lane_q // HEAD_DIM, _iota((tk, LANE), 1) // HEAD_DIM
        rowi, coli = _iota((tq, tk), 0), _iota((tq, tk), 1)
        row8 = _iota((8, tk), 0)

        def qblock(i, _):
            qoff = pl.multiple_of(i * tq, tq)
            cq = cum_ref[pl.ds(qoff, tq), :]
            qs = q_ref[pl.ds(qoff, tq), :] * scale
            doq = do_ref[pl.ds(qoff, tq), :]
            lse = lse_ref[pl.ds(qoff, tq), :]
            delta = doq * o_ref[pl.ds(qoff, tq), :]
            qh, doh, cqh, lseh, dlt = [], [], [], [], []
            for hh in range(2):
                qh.append(jnp.where(halfq == hh, qs, 0.0).astype(BF16))
                doh.append(jnp.where(halfq == hh, doq, 0.0).astype(BF16))
                cqh.append(jnp.sum(jnp.where(lane_q == FOX_LANE0 + 2 * p + hh, cq, 0.0), axis=1, keepdims=True))
                lseh.append(jnp.sum(jnp.where(lane_q == hh, lse, 0.0), axis=1, keepdims=True))
                dlt.append(jnp.sum(jnp.where(halfq == hh, delta, 0.0), axis=1, keepdims=True))

            def step(j, carry, diag):
                dq, rs = carry
                off = pl.multiple_of(j * tk, tk)
                kj = k_ref[pl.ds(off, tk), :].astype(BF16)
                vj = v_ref[pl.ds(off, tk), :].astype(BF16)
                ckt = cumt_ref[:, pl.ds(off, tk)]
                dss, prs, ks = [], [], []
                for hh in range(2):
                    ck = jnp.sum(jnp.where(row8 == 2 * p + hh, ckt, 0.0), axis=0, keepdims=True)
                    logits = _nt(qh[hh], kj) + ((cqh[hh] - lseh[hh]) - ck)
                    if diag:
                        logits = jnp.where(rowi >= coli, logits, -jnp.inf)
                    pr = jnp.exp(logits)
                    ds = pr * (_nt(doh[hh], vj) - dlt[hh])
                    rs = rs + jnp.where(lane_q == FOX_LANE0 + 2 * p + hh, jnp.sum(ds, axis=1, keepdims=True), 0.0)
                    cs = _colsum(ds)
                    dcumt_ref[0:8, pl.ds(off, tk)] += jnp.where(row8 == 2 * p + hh, cs, 0.0)
                    dss.append(ds.astype(BF16))
                    prs.append(pr.astype(BF16))
                    ks.append(jnp.where(halfk == hh, kj, jnp.zeros_like(kj)))
                dq = dq + jnp.dot(jnp.concatenate(dss, axis=1), jnp.concatenate(ks, axis=0), preferred_element_type=F32)
                dk_acc[pl.ds(off, tk), :] += _tn(jnp.concatenate(dss, axis=0), jnp.concatenate(qh, axis=0))
                dv_acc[pl.ds(off, tk), :] += _tn(jnp.concatenate(prs, axis=0), jnp.concatenate(doh, axis=0))
                return dq, rs

            carry = lax.fori_loop(0, i, functools.partial(step, diag=False),
                                  (jnp.zeros((tq, LANE), F32), jnp.zeros((tq, LANE), F32)))
            dq, rs = step(i, carry, True)
            dq_ref[pl.ds(qoff, tq), :] = (dq * scale).astype(dq_ref.dtype)
            dcum_ref[pl.ds(qoff, tq), :] += rs
            return 0

        lax.fori_loop(0, nq, qblock, 0)
        dk_ref[...] = dk_acc[...].astype(dk_ref.dtype)
        dv_ref[...] = dv_acc[...].astype(dv_ref.dtype)
        dct = dcumt_ref[...].T
        dcum_ref[...] = dcum_ref[...] - pltpu.roll(dct, FOX_LANE0, 1)

    sp = lambda c0: pl.BlockSpec((S, LANE), lambda b, p: (b, c0 + p))
    s0 = pl.BlockSpec((S, LANE), lambda b, p: (b, 0))
    comm = carried.take(name) if carried is not None else None
    call = _pc(body, name=name, grid=(batch, 3),
               in_specs=[sp(qb), sp(kb), sp(vb), sp(0), sp(0), sp(0), s0, pl.BlockSpec((8, S), lambda b, p: (b, 0))],
               out_specs=[sp(0)] * 4,
               out_shape=[SDS((T, FOX_W), BF16)] * 3 + [SDS((T, FOX_W), F32)],
               scratch=[pltpu.VMEM((S, LANE), F32), pltpu.VMEM((S, LANE), F32), pltpu.VMEM((LANE, S), F32)],
               sem=("parallel", "parallel"), comm=comm)
    return _run(call, (proj, proj, proj, o, do, lse, cum, cum_t), name, comm, carried)


def _row(v, width=None, at=0):
    v = v.astype(F32)
    width = width or v.shape[0]
    return jnp.pad(v, (at, width - at - v.shape[0]))[None, :]


def _pad8(w4):
    return jnp.pad(w4.astype(F32), ((0, 4), (0, 0)))


def _block_diag(w):
    out = jnp.zeros((LRU_W, LRU_W), w.dtype)
    for g in range(4):
        out = lax.dynamic_update_slice(out, w[g], (g * 64, g * 64))
    return out


def prep_layer(f):
    cw, cb = f["ssd_conv_w"], f["ssd_conv_b"]
    return LazyDict(
        win=lambda: permute_in_cols(f["w_in"]), wout=lambda: f["w_out"], wg=lambda: f["w_gate"], wu=lambda: f["w_up"],
        wd=lambda: f["w_down"], wpg=lambda: f["w_ple_gate"], wpp=lambda: f["w_ple_proj"],
        wax=jnp.concatenate([_block_diag(f["lru_w_a"]), _block_diag(f["lru_w_x"])], axis=1),
        g1=_row(f["norm1_g"]), g2=_row(f["norm2_g"]), g3=_row(f["norm3_g"]),
        cw_xs=_pad8(cw[:, :384]), cb_xs=_row(cb[:384]), cw_b=_pad8(cw[:, 384:640]), cb_b=_row(cb[384:640]),
        cw_c=_pad8(cw[:, 640:]), cb_c=_row(cb[640:]), cw_l=_pad8(f["lru_conv_w"]), cb_l=_row(f["lru_conv_b"]),
        dtbias_row=_row(f["ssd_dt_bias"], LANE), alog_row=_row(f["ssd_a_log"], LANE),
        dexp=jnp.repeat(f["ssd_d"].astype(F32), HEAD_DIM)[None, :], g_ssd=_row(f["ssd_norm_g"]),
        b_ax=_row(jnp.concatenate([f["lru_b_a"], f["lru_b_x"]])), lam=_row(f["lru_lambda"]), g_lru=_row(f["lru_norm_g"]),
        bf_row=_row(f["fox_b_f"], LANE, FOX_LANE0), g_fox=_row(f["fox_norm_g"]), b_pg=_row(f["b_ple_gate"]))


def unprep_grads(g):
    blocks = lambda m: jnp.stack([m[i * 64:(i + 1) * 64, i * 64:(i + 1) * 64] for i in range(4)])
    return dict(
        norm1_g=g["g1"][0], w_in=unpermute_in_cols(g["win"]),
        ssd_conv_w=jnp.concatenate([g["cw_xs"][:4], g["cw_b"][:4], g["cw_c"][:4]], axis=1),
        ssd_conv_b=jnp.concatenate([g["cb_xs"][0], g["cb_b"][0], g["cb_c"][0]]),
        ssd_dt_bias=g["dtbias_row"][0, :N_HEADS], ssd_a_log=g["alog_row"][0, :N_HEADS],
        ssd_d=jnp.sum(g["dexp"].reshape(N_HEADS, HEAD_DIM), axis=1), ssd_norm_g=g["g_ssd"][0],
        lru_conv_w=g["cw_l"][:4], lru_conv_b=g["cb_l"][0],
        lru_w_a=blocks(g["wax"][:, :LRU_W]), lru_b_a=g["b_ax"][0, :LRU_W],
        lru_w_x=blocks(g["wax"][:, LRU_W:]), lru_b_x=g["b_ax"][0, LRU_W:],
        lru_lambda=g["lam"][0], lru_norm_g=g["g_lru"][0],
        fox_b_f=g["bf_row"][0, FOX_LANE0:FOX_LANE0 + N_HEADS], fox_norm_g=g["g_fox"][0],
        w_out=g["wout"], norm2_g=g["g2"][0], w_gate=g["wg"], w_up=g["wu"], w_down=g["wd"],
        norm3_g=g["g3"][0], w_ple_gate=g["wpg"], b_ple_gate=g["b_pg"][0], w_ple_proj=g["wpp"])


def _view(a, off, width):
    return (a, off // width, width)


def _add_epilogue(acc, e):
    return (acc + e,)


def mixer_fwd(proj, w, batch, tag, carried=None):
    sm = _view(proj, OFF_SM, LANE)
    conv = functools.partial(seq_conv, batch=batch)
    cmc = conv(f"{tag}_conv_c", proj, OFF_C, 256, w["cw_c"], w["cb_c"], silu=True, out_dtype=BF16)
    bmc = conv(f"{tag}_conv_b", proj, OFF_B, 256, w["cw_b"], w["cb_b"], silu=True, out_dtype=BF16)
    xs_act = conv(f"{tag}_conv_xs", proj, OFF_XS, SSD_W, w["cw_xs"], w["cb_xs"], silu=True, out_dtype=F32)
    xl = conv(f"{tag}_conv_l", proj, OFF_LX, LRU_W, w["cw_l"], w["cb_l"], silu=False, out_dtype=F32)
    adt, xd = rowwise(f"{tag}_ssd_elt", _ssd_elt, [sm, xs_act], [w["dtbias_row"], w["alog_row"]],
                      [(LANE, F32), (SSD_W, BF16)])
    cum_a, cum_at = seq_cumsum(f"{tag}_cum_a", adt, batch=batch, trow=0)
    yraw = ssd_attn_fwd(f"{tag}_ssd_fwd", cmc, bmc, xd, cum_a, cum_at, batch=batch, carried=carried)
    logf = rowwise(f"{tag}_fox_elt", _fox_elt, [sm], [w["bf_row"]], [(LANE, F32)])
    cum_f, cum_ft = seq_cumsum(f"{tag}_cum_f", logf, batch=batch, trow=FOX_LANE0)
    o, lse = fox_attn_fwd(f"{tag}_fox_fwd", proj, cum_f, cum_ft, batch=batch, carried=carried)
    pre = mm(xl, w["wax"], name=f"{tag}_mm_lru_gates")
    a, u = rowwise(f"{tag}_lru_elt", _lru_elt, [xl, pre], [w["b_ax"], w["lam"]], [(LRU_W, F32), (LRU_W, F32)])
    hl = lru_scan(f"{tag}_lru_scan", a, u, batch=batch)
    ycat = rowwise(f"{tag}_mix_post", _mix_post,
                   [yraw, xs_act, _view(proj, OFF_Z, SSD_W), hl, _view(proj, OFF_LG, LRU_W), o],
                   [w["dexp"], w["g_ssd"], w["g_lru"], w["g_fox"]], [(D_MODEL, BF16)], tr=256)
    saved = dict(cmc=cmc, bmc=bmc, xs_act=xs_act, xl=xl, xd=xd, cum_a=cum_a, cum_at=cum_at, yraw=yraw,
                 cum_f=cum_f, cum_ft=cum_ft, o=o, lse=lse, pre=pre, a=a, hl=hl)
    return ycat, saved


def mixer_bwd(dycat, proj, w, s, batch, tag, carried=None):
    sm = _view(proj, OFF_SM, LANE)
    g = {}

    def post_bwd(yraw, xs_act, z, hl, lg, o, dyc, dexp, g_ssd, g_lru, g_fox):
        return jax.vjp(_mix_post, yraw, xs_act, z, hl, lg, o, dexp, g_ssd, g_lru, g_fox)[1](dyc)

    (dyraw, dxs1, dz, dhl, dlg, do, g["dexp"], g["g_ssd"], g["g_lru"], g["g_fox"]) = rowwise(
        f"{tag}_mix_post_bwd", post_bwd,
        [s["yraw"], s["xs_act"], _view(proj, OFF_Z, SSD_W), s["hl"], _view(proj, OFF_LG, LRU_W), s["o"], dycat],
        [w["dexp"], w["g_ssd"], w["g_lru"], w["g_fox"]],
        [(SSD_W, F32), (SSD_W, F32), (SSD_W, BF16), (LRU_W, F32), (LRU_W, BF16), (FOX_W, F32)],
        [SSD_W, SSD_W, LRU_W, FOX_W], tr=256)

    dq, dk, dv, dcum3 = fox_attn_bwd(f"{tag}_fox_bwd", proj, s["o"], do, s["lse"], s["cum_f"], s["cum_ft"], batch=batch,
                                     carried=carried)
    dlogf = seq_cumsum(f"{tag}_rcum_f", dcum3, batch=batch, reverse=True, nsum=3)

    dxd, dbm, dcm, dcum_a = ssd_attn_bwd(f"{tag}_ssd_bwd", s["cmc"], s["bmc"], s["xd"], s["cum_a"], s["cum_at"], dyraw,
                                         batch=batch, carried=carried)
    dadt = seq_cumsum(f"{tag}_rcum_a", dcum_a, batch=batch, reverse=True)

    def ssd_elt_bwd(small, xs_act, dadt_, dxd_, dxs1_, dtbias, alog):
        dsm, dxs, ddtb, dalog = jax.vjp(_ssd_elt, small, xs_act, dtbias, alog)[1]((dadt_, dxd_))
        return dsm, dxs + dxs1_, ddtb, dalog

    dsm_s, dxs_act, g["dtbias_row"], g["alog_row"] = rowwise(
        f"{tag}_ssd_elt_bwd", ssd_elt_bwd, [sm, s["xs_act"], dadt, dxd, dxs1], [w["dtbias_row"], w["alog_row"]],
        [(LANE, F32), (SSD_W, F32)], [LANE, LANE])

    def fox_elt_bwd(small, dlogf_, dsm_s_, bf_row):
        dsm, dbf = jax.vjp(_fox_elt, small, bf_row)[1](dlogf_)
        return dsm + dsm_s_, dbf

    dsm, g["bf_row"] = rowwise(f"{tag}_fox_elt_bwd", fox_elt_bwd, [sm, dlogf, dsm_s], [w["bf_row"]],
                               [(LANE, BF16)], [LANE])

    cbwd = functools.partial(seq_conv_bwd, batch=batch)
    dxs_raw, g["cw_xs"], g["cb_xs"] = cbwd(f"{tag}_conv_xs_bwd", proj, OFF_XS, SSD_W, w["cw_xs"], w["cb_xs"], dxs_act, silu=True)
    db_raw, g["cw_b"], g["cb_b"] = cbwd(f"{tag}_conv_b_bwd", proj, OFF_B, 256, w["cw_b"], w["cb_b"], dbm, silu=True)
    dc_raw, g["cw_c"], g["cb_c"] = cbwd(f"{tag}_conv_c_bwd", proj, OFF_C, 256, w["cw_c"], w["cb_c"], dcm, silu=True)

    da, du = lru_scan_bwd(f"{tag}_lru_scan_bwd", s["a"], s["hl"], dhl, batch=batch)

    def lru_elt_bwd(xl, pre, da_, du_, b_ax, lam):
        return jax.vjp(_lru_elt, xl, pre, b_ax, lam)[1]((da_, du_))

    dxl1, dpre, g["b_ax"], g["lam"] = rowwise(
        f"{tag}_lru_elt_bwd", lru_elt_bwd, [s["xl"], s["pre"], da, du], [w["b_ax"], w["lam"]],
        [(LRU_W, F32), (2 * LRU_W, BF16)], [2 * LRU_W, LRU_W])
    g["wax"] = mm(s["xl"], dpre, ta=True, name=f"{tag}_mm_dwax")
    dxl = mm(dpre, w["wax"], tb=True, extras=[dxl1], epilogue=_add_epilogue, name=f"{tag}_mm_dxl")
    dlx_raw, g["cw_l"], g["cb_l"] = cbwd(f"{tag}_conv_l_bwd", proj, OFF_LX, LRU_W, w["cw_l"], w["cb_l"], dxl, silu=False)

    dproj = jnp.concatenate([db_raw, dc_raw, dlx_raw, dlg, dsm, dz, dxs_raw, dq, dk, dv], axis=1)
    return dproj, g


def layer_fwd(h0, p_l, w, batch, tag, carried=None):
    u1 = rowwise(f"{tag}_rms1", _rms, [h0], [w["g1"]], [(D_MODEL, BF16)])
    proj = mm(u1, w["win"], name=f"{tag}_mm_in")
    ycat, ms = mixer_fwd(proj, w, batch, tag, carried)
    h1 = mm(ycat, w["wout"], extras=[h0], epilogue=_add_epilogue, name=f"{tag}_mm_out")
    u2 = rowwise(f"{tag}_rms2", _rms, [h1], [w["g2"]], [(D_MODEL, BF16)])
    gate, up, act = mm(u2, [w["wg"], w["wu"]], out_dtypes=(BF16, BF16, BF16), epilogue=_swiglu_epilogue,
                       tm=512, tn=D_FF // 2, name=f"{tag}_mm_gu", carried=carried)
    h2 = mm(act, w["wd"], extras=[h1], epilogue=_add_epilogue, tm=512, tk=D_FF, name=f"{tag}_mm_down")
    u3 = rowwise(f"{tag}_rms3", _rms, [h2], [w["g3"]], [(D_MODEL, BF16)])
    pp = mm(p_l, w["wpp"], name=f"{tag}_mm_pp")
    h3, pg = mm(u3, w["wpg"], extras=[pp, h2], col_params=[w["b_pg"]], epilogue=_ple_epilogue,
                out_dtypes=(F32, F32), tm=512, name=f"{tag}_mm_pg")
    saved = dict(h0=h0, u1=u1, proj=proj, ycat=ycat, h1=h1, u2=u2, gate=gate, up=up, act=act, h2=h2, u3=u3, pg=pg,
                 pp=pp, mixer=ms)
    return h3, saved


def _swiglu_epilogue(acc_g, acc_u):
    return acc_g, acc_u, _silu(acc_g) * acc_u


def _swiglu_bwd_epilogue(dact, gate, up):
    return jax.vjp(lambda g_, u_: _silu(g_) * u_, gate.astype(F32), up.astype(F32))[1](dact)


def _ple_epilogue(acc, pp, h2, b):
    return h2 + _ple(acc, pp, b), acc


def _rms_bwd_epilogue(du, h, dres, g):
    dh, dg = jax.vjp(_rms, h, g)[1](du)
    return dh + dres, dg


def layer_bwd(dh3, p_l, w, s, batch, tag, carried=None, on_early_grads=None):
    def ple_bwd(pg, pp, dh, b):
        return jax.vjp(_ple, pg, pp, b)[1](dh)

    norm_bwd = dict(epilogue=_rms_bwd_epilogue, partials=1, tm=512, tn=D_MODEL, tb=True)

    d_pg, d_pp, g_bpg = rowwise(f"{tag}_ple_bwd", ple_bwd, [s["pg"], s["pp"], dh3], [w["b_pg"]],
                                [(D_MODEL, BF16), (D_MODEL, BF16)], [D_MODEL])
    g = dict(b_pg=g_bpg)
    g["wpp"] = mm(p_l, d_pp, ta=True, name=f"{tag}_mm_dwpp")
    g["wpg"] = mm(s["u3"], d_pg, ta=True, name=f"{tag}_mm_dwpg")
    dh2, dg3 = mm(d_pg, w["wpg"], extras=[s["h2"], dh3], col_params=[w["g3"]], name=f"{tag}_mm_du3", **norm_bwd)
    g["g3"] = sum_slices(f"{tag}_sum_dg3", dg3)

    d_gate, d_up = mm(dh2, w["wd"], tb=True, extras=[s["gate"], s["up"]], epilogue=_swiglu_bwd_epilogue,
                      out_dtypes=(BF16, BF16), tm=512, tn=D_FF // 2, name=f"{tag}_mm_dact")
    g["wd"] = mm(s["act"], dh2, ta=True, name=f"{tag}_mm_dwd")
    g["wg"] = mm(s["u2"], d_gate, ta=True, name=f"{tag}_mm_dwg")
    g["wu"] = mm(s["u2"], d_up, ta=True, name=f"{tag}_mm_dwu")
    dh1, dg2 = mm([d_gate, d_up], [w["wg"], w["wu"]], extras=[s["h1"], dh2], col_params=[w["g2"]],
                  name=f"{tag}_mm_du2", **norm_bwd)
    g["g2"] = sum_slices(f"{tag}_sum_dg2", dg2)

    dycat = mm(dh1, w["wout"], tb=True, name=f"{tag}_mm_dycat")
    g["wout"] = mm(s["ycat"], dh1, ta=True, name=f"{tag}_mm_dwout")
    if on_early_grads is not None:
        on_early_grads(g)
    dproj, gm = mixer_bwd(dycat, s["proj"], w, s["mixer"], batch, tag, carried)
    g.update(gm)
    g["win"] = mm(s["u1"], dproj, ta=True, name=f"{tag}_mm_dwin")
    dh0, dg1 = mm(dproj, w["win"], extras=[s["h0"], dh1], col_params=[w["g1"]], name=f"{tag}_mm_du1", **norm_bwd)
    g["g1"] = sum_slices(f"{tag}_sum_dg1", dg1)
    return dh0, g


def _loss_fwd_bwd(h, tgt, gf):
    def f(h_, gf_):
        e = _rms(h_, gf_) - tgt
        return 0.5 * jnp.sum(jnp.mean(e * e, axis=-1, keepdims=True), axis=0, keepdims=True)

    loss, vj = jax.vjp(f, h, gf)
    dh, dgf = vj(jnp.ones((1, 1), F32))
    return dh, jnp.broadcast_to(loss, (1, LANE)), dgf


def local_step(x, p, tgt, layers, final_g, carried=None, on_layer_grads=None, on_early_grads=None):
    batch, S, _ = x.shape
    T = batch * S
    h = x.reshape(T, D_MODEL)
    saved, weights = [], []
    for l, w in enumerate(layers):
        w = w() if callable(w) else w
        weights.append(w)
        h, s = layer_fwd(h, p[l].reshape(T, PLE_DIM), w, batch, f"l{l}", carried)
        saved.append(s)
    dh, loss, dgf = rowwise("loss", _loss_fwd_bwd, [h, tgt.reshape(T, D_MODEL)], [_row(final_g)],
                            [(D_MODEL, F32)], [LANE, D_MODEL], tr=256)
    grads = [None] * len(layers)
    for l in reversed(range(len(layers))):
        early = functools.partial(on_early_grads, l) if on_early_grads is not None else None
        dh, grads[l] = layer_bwd(dh, p[l].reshape(T, PLE_DIM), weights[l], saved[l], batch, f"l{l}", carried, early)
        if on_layer_grads is not None:
            on_layer_grads(l, grads[l])
    return loss[0, 0], dh.reshape(batch, S, D_MODEL), grads, dgf[0]


MESH = pl.DeviceIdType.MESH
N_DEV = 8
N_CHIP = 4
ANY = pl.BlockSpec(memory_space=pl.ANY)


def _pos():
    return lax.axis_index("x"), lax.axis_index("y"), lax.axis_index("c")


def _comm_call(body, name, out_shape, n_in, scratch):
    return pl.pallas_call(body, name=name, out_shape=out_shape, in_specs=[ANY] * n_in, out_specs=ANY,
                          scratch_shapes=scratch)


def all_gather8(name, blk):
    def body(x_ref, out_ref, send_sems, recv_sems, local_sem):
        x, y, c = _pos()
        me, sibling = (x, y, c), (x, y, 1 - c)
        chips = [(1 - x, y), (x, 1 - y), (1 - x, 1 - y)]

        def rows(px, py, pcore):
            return out_ref.at[4 * px + 2 * py + pcore]

        def copy(k, block, to, src=None):
            return pltpu.make_async_remote_copy(
                src_ref=rows(*block) if src is None else src, dst_ref=rows(*block),
                send_sem=send_sems.at[k], recv_sem=recv_sems.at[k], device_id=to, device_id_type=MESH)

        mine = pltpu.make_async_copy(x_ref, rows(*me), local_sem)
        mine.start()
        first = [copy(0, me, sibling, src=x_ref)]
        first += [copy(1 + j, me, (*chip, c), src=x_ref) for j, chip in enumerate(chips)]
        for cp in first:
            cp.start()
        passed = [copy(4 + j, (*chip, c), sibling) for j, chip in enumerate(chips)]
        for j, chip in enumerate(chips):
            copy(1 + j, (*chip, c), me).wait_recv()
            passed[j].start()
        copy(0, sibling, me).wait_recv()
        for j, chip in enumerate(chips):
            copy(4 + j, (*chip, 1 - c), me).wait_recv()
        for cp in first + passed:
            cp.wait_send()
        mine.wait()

    return _comm_call(body, name, SDS((N_DEV,) + blk.shape, blk.dtype), 1,
                      [pltpu.SemaphoreType.DMA((7,)), pltpu.SemaphoreType.DMA((7,)), pltpu.SemaphoreType.DMA])(blk)


def _comm_call_list(body, name, out_shapes, n_in, scratch):
    return pl.pallas_call(body, name=name, out_shape=out_shapes, in_specs=[ANY] * n_in, out_specs=[ANY] * len(out_shapes),
                          scratch_shapes=scratch)


class Exchange:
    def __init__(self, inputs, out_shapes, sems, start, wait, aliases=None):
        self.inputs, self.out_shapes, self.sems = list(inputs), list(out_shapes), list(sems)
        self.start, self.wait, self.aliases = start, wait, dict(aliases or {})


def combine(a, b):
    ai, ao, as_ = len(a.inputs), len(a.out_shapes), len(a.sems)

    def split(cins, couts, sems):
        return (cins[:ai], couts[:ao], sems[:as_]), (cins[ai:], couts[ao:], sems[as_:])

    def start(cins, couts, sems):
        pa, pb = split(cins, couts, sems)
        a.start(*pa)
        b.start(*pb)

    def wait(cins, couts, sems):
        pa, pb = split(cins, couts, sems)
        a.wait(*pa)
        b.wait(*pb)

    aliases = dict(a.aliases)
    aliases.update({ai + i: ao + o for i, o in b.aliases.items()})
    return Exchange(a.inputs + b.inputs, a.out_shapes + b.out_shapes, a.sems + b.sems, start, wait, aliases)


def run_exchange(name, ex):
    n_ci, n_co = len(ex.inputs), len(ex.out_shapes)

    def body(*refs):
        cins, couts, csems = refs[:n_ci], refs[n_ci:n_ci + n_co], refs[n_ci + n_co:]
        ex.start(cins, couts, csems)
        ex.wait(cins, couts, csems)

    return pl.pallas_call(body, name=name, out_shape=ex.out_shapes, in_specs=[ANY] * n_ci, out_specs=[ANY] * n_co,
                          scratch_shapes=ex.sems, input_output_aliases=ex.aliases)(*ex.inputs)


def _peers():
    x, y, c = _pos()
    return x, y, c, 2 * x + y, [(1 - x, y), (x, 1 - y), (1 - x, 1 - y)]


def _remote(src, dst, send_sem, recv_sem, to):
    return pltpu.make_async_remote_copy(src_ref=src, dst_ref=dst, send_sem=send_sem, recv_sem=recv_sem,
                                        device_id=to, device_id_type=MESH)


def gather_spread(shards, layer):
    n_t = len(shards)
    halves = [s.shape[1] // 2 for s in shards]

    def copies(cins, couts, sems):
        send_sems, recv_sems, local_sems = sems
        x, y, c, my_chip, chips = _peers()
        local, sends, recvs = [], [], []
        for t in range(n_t):
            h = halves[t]
            src = cins[t].at[layer, pl.ds(c * h, h)]
            mine = couts[t].at[my_chip, pl.ds(c * h, h)]
            local.append(pltpu.make_async_copy(src, mine, local_sems.at[t]))
            sends.append(_remote(src, mine, send_sems.at[0, t], recv_sems.at[0, t], (x, y, 1 - c)))
            recvs.append(_remote(src, couts[t].at[my_chip, pl.ds((1 - c) * h, h)], send_sems.at[0, t],
                                 recv_sems.at[0, t], (x, y, 1 - c)))
            for j, (px, py) in enumerate(chips):
                sends.append(_remote(src, mine, send_sems.at[1 + j, t], recv_sems.at[1 + j, t], (px, py, c)))
                recvs.append(_remote(src, couts[t].at[2 * px + py, pl.ds(c * h, h)], send_sems.at[1 + j, t],
                                     recv_sems.at[1 + j, t], (px, py, c)))
        return local, sends, recvs

    def start(cins, couts, sems):
        local, sends, _ = copies(cins, couts, sems)
        for cp in local + sends:
            cp.start()

    def wait(cins, couts, sems):
        local, sends, recvs = copies(cins, couts, sems)
        for cp in recvs:
            cp.wait_recv()
        for cp in sends:
            cp.wait_send()
        for cp in local:
            cp.wait()

    return Exchange(shards, [SDS((N_CHIP,) + s.shape[1:], s.dtype) for s in shards],
                    [pltpu.SemaphoreType.DMA((4, n_t)), pltpu.SemaphoreType.DMA((4, n_t)),
                     pltpu.SemaphoreType.DMA((n_t,))], start, wait)


def gather_pass_on(slots):
    n_t = len(slots)
    halves = [s.shape[1] // 2 for s in slots]

    def copies(cins, couts, sems):
        send_sems, recv_sems = sems
        x, y, c, my_chip, chips = _peers()
        sends, recvs = [], []
        for t in range(n_t):
            h = halves[t]
            for j, (px, py) in enumerate(chips):
                k = 2 * px + py
                sends.append(_remote(cins[t].at[k, pl.ds(c * h, h)], couts[t].at[k, pl.ds(c * h, h)],
                                     send_sems.at[j, t], recv_sems.at[j, t], (x, y, 1 - c)))
                recvs.append(_remote(cins[t].at[k, pl.ds(c * h, h)], couts[t].at[k, pl.ds((1 - c) * h, h)],
                                     send_sems.at[j, t], recv_sems.at[j, t], (x, y, 1 - c)))
        return sends, recvs

    def start(cins, couts, sems):
        for cp in copies(cins, couts, sems)[0]:
            cp.start()

    def wait(cins, couts, sems):
        sends, recvs = copies(cins, couts, sems)
        for cp in recvs:
            cp.wait_recv()
        for cp in sends:
            cp.wait_send()

    return Exchange(slots, [SDS(s.shape, s.dtype) for s in slots],
                    [pltpu.SemaphoreType.DMA((3, n_t)), pltpu.SemaphoreType.DMA((3, n_t))], start, wait,
                    aliases={t: t for t in range(n_t)})


def chips_exchange(vs):
    n_t = len(vs)

    def copies(cins, couts, sems):
        send_sems, recv_sems, local_sems = sems
        x, y, c, my_chip, chips = _peers()
        local = [pltpu.make_async_copy(cins[t].at[my_chip], couts[t].at[my_chip], local_sems.at[t]) for t in range(n_t)]
        sends, recvs = [], []
        for k, (px, py) in enumerate(chips):
            for t in range(n_t):
                sends.append(_remote(cins[t].at[2 * px + py], couts[t].at[my_chip], send_sems.at[k, t],
                                     recv_sems.at[k, t], (px, py, c)))
                recvs.append(_remote(cins[t].at[my_chip], couts[t].at[2 * px + py], send_sems.at[k, t],
                                     recv_sems.at[k, t], (px, py, c)))
        return local, sends, recvs

    def start(cins, couts, sems):
        local, sends, _ = copies(cins, couts, sems)
        for cp in local + sends:
            cp.start()

    def wait(cins, couts, sems):
        local, sends, recvs = copies(cins, couts, sems)
        for cp in recvs:
            cp.wait_recv()
        for cp in sends:
            cp.wait_send()
        for cp in local:
            cp.wait()

    return Exchange(vs, [SDS(v.shape, v.dtype) for v in vs],
                    [pltpu.SemaphoreType.DMA((3, n_t)), pltpu.SemaphoreType.DMA((3, n_t)),
                     pltpu.SemaphoreType.DMA((n_t,))], start, wait)


def swap_with_sibling(name, vs):
    n_t = len(vs)

    def body(*refs):
        v_refs, out_refs, send_sems, recv_sems = refs[:n_t], refs[n_t:2 * n_t], refs[-2], refs[-1]
        x, y, c = _pos()
        cps = [pltpu.make_async_remote_copy(src_ref=v_refs[t], dst_ref=out_refs[t], send_sem=send_sems.at[t],
                                            recv_sem=recv_sems.at[t], device_id=(x, y, 1 - c), device_id_type=MESH)
               for t in range(n_t)]
        for cp in cps:
            cp.start()
        for cp in cps:
            cp.wait()

    return _comm_call_list(body, name, [SDS(v.shape, v.dtype) for v in vs], n_t,
                           [pltpu.SemaphoreType.DMA((n_t,)), pltpu.SemaphoreType.DMA((n_t,))])(*vs)


_ROW_BLOCKS = (1024, 704, 512, 352, 256, 128, 64, 32, 16, 8)


def sum_slices(name, v, tr=512):
    n, R, C = v.shape
    tr = _pick(R, _ROW_BLOCKS)

    def body(v_ref, o_ref):
        acc = v_ref[0].astype(F32)
        for k in range(1, n):
            acc = acc + v_ref[k].astype(F32)
        o_ref[...] = acc

    return _pc(body, name=name, grid=(R // tr,), in_specs=[pl.BlockSpec((n, tr, C), lambda i: (0, i, 0))],
               out_specs=pl.BlockSpec((tr, C), lambda i: (i, 0)), out_shape=SDS((R, C), F32), sem=("parallel",))(v)


def add_slices(name, a, b, out_dtype):
    n, R, C = a.shape
    tr = _pick(R, _ROW_BLOCKS)

    def body(a_ref, b_ref, o_ref):
        o_ref[...] = (a_ref[...].astype(F32) + b_ref[...].astype(F32)).astype(o_ref.dtype)

    spec = pl.BlockSpec((1, tr, C), lambda k, i: (k, i, 0))
    return _pc(body, name=name, grid=(n, R // tr), in_specs=[spec, spec], out_specs=spec,
               out_shape=SDS(a.shape, out_dtype), sem=("parallel", "parallel"))(a, b)


def adamw(name, w, g, m, v):
    L, R, C = w.shape
    tr = _pick(R, (256, 128, 64, 32, 16, 8))
    c1 = 1.0 / (1.0 - ADAM_B1 ** ADAM_STEP)
    c2 = 1.0 / (1.0 - ADAM_B2 ** ADAM_STEP)

    def body(w_ref, g_ref, m_ref, v_ref, d_ref, nm_ref, nv_ref):
        gv = g_ref[...]
        nm = ADAM_B1 * m_ref[...] + (1.0 - ADAM_B1) * gv
        nv = ADAM_B2 * v_ref[...] + (1.0 - ADAM_B2) * (gv * gv)
        d_ref[...] = -ADAM_LR * ((nm * c1) / (jnp.sqrt(nv * c2) + ADAM_EPS) + ADAM_WD * w_ref[...])
        nm_ref[...] = nm
        nv_ref[...] = nv

    spec = pl.BlockSpec((1, tr, C), lambda l, i: (l, i, 0))
    return _pc(body, name=name, grid=(L, R // tr), in_specs=[spec] * 4, out_specs=[spec] * 3,
               out_shape=[SDS(w.shape, F32)] * 3, sem=("parallel", "parallel"))(w, g, m, v)


WEIGHTS = ["norm1_g", "w_in", "ssd_conv_w", "ssd_conv_b", "ssd_dt_bias", "ssd_a_log", "ssd_d", "ssd_norm_g",
           "lru_conv_w", "lru_conv_b", "lru_w_a", "lru_b_a", "lru_w_x", "lru_b_x", "lru_lambda", "lru_norm_g",
           "fox_b_f", "fox_norm_g", "w_out", "norm2_g", "w_gate", "w_up", "w_down", "norm3_g", "w_ple_gate",
           "b_ple_gate", "w_ple_proj", "final_norm_g"]
BIG = {"w_in": 2, "w_out": 1, "w_gate": 2, "w_up": 2, "w_down": 1, "w_ple_gate": 1, "w_ple_proj": 2}
SHARDED_SMALL = {"ssd_conv_w": 2, "lru_conv_w": 2}
SMALL = [n for n in WEIGHTS if n not in BIG]


def _pack(arrs, rows_multiple):
    flat = jnp.concatenate([a.reshape(-1) for a in arrs])
    per = rows_multiple * LANE
    n = -(-flat.shape[0] // per) * per
    return jnp.pad(flat, (0, n - flat.shape[0])).reshape(n // LANE, LANE)


def _unpack(flat2d, shapes):
    flat = flat2d.reshape(-1)
    out, off = [], 0
    for s in shapes:
        n = int(np.prod(s))
        out.append(flat[off:off + n].reshape(s))
        off += n
    return out


def _gather_shards(name, shards, axes, dtype):
    c = lax.axis_index("c")
    packed = _pack([s.astype(dtype) for s in shards], 32)
    half = packed.shape[0] // 2
    mine = lax.dynamic_slice_in_dim(packed, c * half, half, 0)
    got = all_gather8(name, mine).reshape(N_CHIP, 2 * half, LANE)
    per_chip = [_unpack(got[k], [s.shape for s in shards]) for k in range(N_CHIP)]
    return [jnp.concatenate([per_chip[k][i] for k in range(N_CHIP)], axis=ax) for i, ax in enumerate(axes)]


def kernel(x, p, norm1_g, w_in, ssd_conv_w, ssd_conv_b, ssd_dt_bias, ssd_a_log, ssd_d, ssd_norm_g, lru_conv_w, lru_conv_b, lru_w_a, lru_b_a, lru_w_x, lru_b_x, lru_lambda, lru_norm_g, fox_b_f, fox_norm_g, w_out, norm2_g, w_gate, w_up, w_down, norm3_g, w_ple_gate, b_ple_gate, w_ple_proj, final_norm_g, loss_target, m_norm1_g, m_w_in, m_ssd_conv_w, m_ssd_conv_b, m_ssd_dt_bias, m_ssd_a_log, m_ssd_d, m_ssd_norm_g, m_lru_conv_w, m_lru_conv_b, m_lru_w_a, m_lru_b_a, m_lru_w_x, m_lru_b_x, m_lru_lambda, m_lru_norm_g, m_fox_b_f, m_fox_norm_g, m_w_out, m_norm2_g, m_w_gate, m_w_up, m_w_down, m_norm3_g, m_w_ple_gate, m_b_ple_gate, m_w_ple_proj, m_final_norm_g, v_norm1_g, v_w_in, v_ssd_conv_w, v_ssd_conv_b, v_ssd_dt_bias, v_ssd_a_log, v_ssd_d, v_ssd_norm_g, v_lru_conv_w, v_lru_conv_b, v_lru_w_a, v_lru_b_a, v_lru_w_x, v_lru_b_x, v_lru_lambda, v_lru_norm_g, v_fox_b_f, v_fox_norm_g, v_w_out, v_norm2_g, v_w_gate, v_w_up, v_w_down, v_norm3_g, v_w_ple_gate, v_b_ple_gate, v_w_ple_proj, v_final_norm_g):
    args = (norm1_g, w_in, ssd_conv_w, ssd_conv_b, ssd_dt_bias, ssd_a_log, ssd_d, ssd_norm_g, lru_conv_w, lru_conv_b, lru_w_a, lru_b_a, lru_w_x, lru_b_x, lru_lambda, lru_norm_g, fox_b_f, fox_norm_g, w_out, norm2_g, w_gate, w_up, w_down, norm3_g, w_ple_gate, b_ple_gate, w_ple_proj, final_norm_g)
    m_args = (m_norm1_g, m_w_in, m_ssd_conv_w, m_ssd_conv_b, m_ssd_dt_bias, m_ssd_a_log, m_ssd_d, m_ssd_norm_g, m_lru_conv_w, m_lru_conv_b, m_lru_w_a, m_lru_b_a, m_lru_w_x, m_lru_b_x, m_lru_lambda, m_lru_norm_g, m_fox_b_f, m_fox_norm_g, m_w_out, m_norm2_g, m_w_gate, m_w_up, m_w_down, m_norm3_g, m_w_ple_gate, m_b_ple_gate, m_w_ple_proj, m_final_norm_g)
    v_args = (v_norm1_g, v_w_in, v_ssd_conv_w, v_ssd_conv_b, v_ssd_dt_bias, v_ssd_a_log, v_ssd_d, v_ssd_norm_g, v_lru_conv_w, v_lru_conv_b, v_lru_w_a, v_lru_b_a, v_lru_w_x, v_lru_b_x, v_lru_lambda, v_lru_norm_g, v_fox_b_f, v_fox_norm_g, v_w_out, v_norm2_g, v_w_gate, v_w_up, v_w_down, v_norm3_g, v_w_ple_gate, v_b_ple_gate, v_w_ple_proj, v_final_norm_g)
    w = dict(zip(WEIGHTS, args))
    mom = dict(zip(WEIGHTS, m_args))
    var = dict(zip(WEIGHTS, v_args))
    xi, yi, ci = _pos()
    chip = 2 * xi + yi

    big_names = list(BIG)
    later = [n for n in big_names if n != "w_in"]
    wb = {n: w[n].astype(BF16) for n in big_names}
    conv_full = dict(zip(SHARDED_SMALL, _gather_shards("gather_conv", [w[n] for n in SHARDED_SMALL],
                                                       list(SHARDED_SMALL.values()), F32)))
    carried = Carried()

    def layer_weights(l, slots_of):
        def assemble(n):
            s4 = slots_of(n)
            return (s4.reshape(-1, s4.shape[-1]) if BIG[n] == 1
                    else jnp.concatenate([s4[k] for k in range(N_CHIP)], axis=1))

        f = LazyDict({n: (conv_full[n][l] if n in conv_full else w[n][l]) for n in SMALL if n != "final_norm_g"})
        f.update({n: functools.partial(assemble, n) for n in big_names})
        return prep_layer(f)

    w_in0 = run_exchange("gather0_in_pass_on", gather_pass_on(
        run_exchange("gather0_in_spread", gather_spread([wb["w_in"]], 0))))
    carried.offer("l0_ssd_fwd", lambda: gather_spread([wb[n] for n in later], 0))
    carried.offer("l0_fox_fwd", lambda: combine(gather_pass_on(carried.results["l0_ssd_fwd"]),
                                                gather_spread([wb[n] for n in big_names], 1)))
    carried.offer("l0_mm_gu", lambda: gather_pass_on(carried.results["l0_fox_fwd"][len(later):]))
    layers = [
        layer_weights(0, lambda n: w_in0[0] if n == "w_in" else carried.results["l0_fox_fwd"][later.index(n)]),
        lambda: layer_weights(1, lambda n: carried.results["l0_mm_gu"][big_names.index(n)])]

    def chip_slices(a, n):
        return a.reshape(N_CHIP, -1, a.shape[1]) if BIG[n] == 1 else jnp.stack(jnp.split(a, N_CHIP, axis=1))

    def reduce_start(tag, names, full_grads):
        keep, give = [], []
        for n, a in zip(names, full_grads):
            s4 = chip_slices(a, n)
            h = s4.shape[1] // 2
            keep.append(lax.dynamic_slice_in_dim(s4, ci * h, h, 1))
            give.append(lax.dynamic_slice_in_dim(s4, (1 - ci) * h, h, 1).astype(BF16))
        got = swap_with_sibling(f"swap_halves{tag}", give)
        return [add_slices(f"add_sibling{tag}_{n}", k_, g_, BF16) for n, k_, g_ in zip(names, keep, got)]

    kernel_key = dict(w_out="wout", w_gate="wg", w_up="wu", w_down="wd", w_ple_gate="wpg", w_ple_proj="wpp")
    gl, parts = [None] * DEPTH, {}

    def on_early_grads(l, g_layer):
        if l == 0:
            parts["0_later"] = reduce_start("0_later", later, [g_layer[kernel_key[n]] for n in later])
            carried.offer("l0_ssd_bwd", lambda: chips_exchange(parts["0_later"]))

    def on_layer_grads(l, g_layer):
        gl[l] = unprep_grads(g_layer)
        if l == 1:
            parts["1"] = reduce_start("1", big_names, [gl[1][n] for n in big_names])
            carried.offer("l0_fox_bwd", lambda: chips_exchange(parts["1"]))
        else:
            parts["0_in"] = reduce_start("0_in", ["w_in"], [gl[0]["w_in"]])

    loss, grad_x, grads, g_final = local_step(x, p, loss_target, layers, final_norm_g, carried, on_layer_grads,
                                              on_early_grads)
    loss = lax.psum(loss, ("x", "y", "c"))
    arrived0 = dict(zip(later, carried.results["l0_ssd_bwd"]))
    arrived0["w_in"] = run_exchange("a2a_chips0_in", chips_exchange(parts["0_in"]))[0]
    arrived = [[arrived0[n] for n in big_names], carried.results["l0_fox_bwd"]]

    gsmall = {n: jnp.stack([gl[l][n] for l in range(DEPTH)]) for n in SMALL if n != "final_norm_g"}
    gsmall["final_norm_g"] = g_final
    small_shapes = [gsmall[n].shape for n in SMALL]
    gs = _pack([gsmall[n] for n in SMALL], 8)
    gs = sum_slices("sum_small", all_gather8("gather_small_grads", gs))
    gsum = dict(zip(SMALL, _unpack(gs, small_shapes)))
    for n, ax in SHARDED_SMALL.items():
        k = gsum[n].shape[ax] // N_CHIP
        gsum[n] = lax.dynamic_slice_in_dim(gsum[n], chip * k, k, ax)

    done = []
    for l in range(DEPTH):
        mine = [sum_slices(f"sum_chips{l}_{n}", a_) for n, a_ in zip(big_names, arrived[l])]
        other = swap_with_sibling(f"swap_results{l}", mine)
        done.append([jnp.concatenate([jnp.where(ci == 0, m_, o_), jnp.where(ci == 0, o_, m_)], axis=0)
                     for m_, o_ in zip(mine, other)])
    for t, n in enumerate(big_names):
        gsum[n] = jnp.stack([done[l][t] for l in range(DEPTH)])

    delta, new_m, new_v = {}, {}, {}
    for n in big_names:
        delta[n], new_m[n], new_v[n] = adamw(f"adamw_{n}", w[n], gsum[n], mom[n], var[n])
    shapes = [w[n].shape for n in SMALL]
    pk = lambda d: _pack([d[n] for n in SMALL], 8)[None]
    ds, ms, vs = adamw("adamw_small", pk(w), pk(gsum), pk(mom), pk(var))
    for d, packed in ((delta, ds), (new_m, ms), (new_v, vs)):
        d.update(zip(SMALL, _unpack(packed[0], shapes)))

    return (loss, grad_x, *[gsum[n] for n in WEIGHTS], *[delta[n] for n in WEIGHTS],
            *[new_m[n] for n in WEIGHTS], *[new_v[n] for n in WEIGHTS])
```

```python
import functools
import math

import jax
import jax.numpy as jnp
import numpy as np
from jax import lax
from jax.experimental import pallas as pl
from jax.experimental.pallas import tpu as pltpu

F32, BF16 = jnp.float32, jnp.bfloat16
SDS = jax.ShapeDtypeStruct

D_MODEL = 1024
DEPTH = 2
HEAD_DIM = 64
N_HEADS = 6
SSD_W, LRU_W, FOX_W = 384, 256, 384
D_FF = 2816
PLE_DIM = 256
IN_COLS = 2956
EPS = 1e-6
LRU_C = 8.0
LANE = 128
V7X_VMEM_LIMIT = 56 * 1024 * 1024

PW = 3072
OFF_B, OFF_C, OFF_LX, OFF_LG, OFF_SM, OFF_Z, OFF_XS, OFF_Q, OFF_K, OFF_V = (
    0, 256, 512, 768, 1024, 1152, 1536, 1920, 2304, 2688)
FOX_LANE0 = 8

ADAM_LR, ADAM_B1, ADAM_B2, ADAM_EPS, ADAM_WD, ADAM_STEP = 0.001, 0.9, 0.999, 1e-08, 0.01, 10


def _iota(shape, dim):
    return lax.broadcasted_iota(jnp.int32, shape, dim)


class Carried:
    def __init__(self):
        self.offers, self.results = {}, {}

    def offer(self, call_name, make_exchange):
        self.offers[call_name] = make_exchange

    def take(self, call_name):
        make = self.offers.pop(call_name, None)
        return None if make is None else make()

    def deliver(self, call_name, results):
        self.results[call_name] = results


class LazyDict(dict):
    def __getitem__(self, key):
        v = dict.__getitem__(self, key)
        if callable(v):
            v = v()
            dict.__setitem__(self, key, v)
        return v


def _run(call, args, name, comm, carried):
    if comm is None:
        return call(*args)
    own, brought = call(*args)
    carried.deliver(name, brought)
    return own


def _pc(body, *, name, grid, in_specs, out_specs, out_shape, scratch=(), sem=None, comm=None):
    if comm is None:
        return pl.pallas_call(
            body, name=name, grid=grid, in_specs=in_specs, out_specs=out_specs, out_shape=out_shape,
            scratch_shapes=list(scratch),
            compiler_params=pltpu.CompilerParams(dimension_semantics=sem, vmem_limit_bytes=V7X_VMEM_LIMIT))
    single = not isinstance(out_shape, (list, tuple))
    out_specs_l = [out_specs] if single else list(out_specs)
    out_shape_l = [out_shape] if single else list(out_shape)
    n_in, n_out, n_scr, n_ci, n_co = len(in_specs), len(out_shape_l), len(scratch), len(comm.inputs), len(comm.out_shapes)

    def hosted(*refs):
        ins, cins = refs[:n_in], refs[n_in:n_in + n_ci]
        outs, couts = refs[n_in + n_ci:n_in + n_ci + n_out], refs[n_in + n_ci + n_out:n_in + n_ci + n_out + n_co]
        rest = refs[n_in + n_ci + n_out + n_co:]
        scr, csems = rest[:n_scr], rest[n_scr:]
        ids = [pl.program_id(d) for d in range(len(grid))]
        first = functools.reduce(jnp.logical_and, [i == 0 for i in ids])
        last = functools.reduce(jnp.logical_and, [i == g - 1 for i, g in zip(ids, grid)])

        @pl.when(first)
        def _():
            comm.start(cins, couts, csems)

        body(*ins, *outs, *scr)

        @pl.when(last)
        def _():
            comm.wait(cins, couts, csems)

    call = pl.pallas_call(
        hosted, name=name, grid=grid, in_specs=list(in_specs) + [ANY] * n_ci,
        out_specs=out_specs_l + [ANY] * n_co, out_shape=out_shape_l + list(comm.out_shapes),
        scratch_shapes=list(scratch) + list(comm.sems),
        input_output_aliases={n_in + a: n_out + b for a, b in comm.aliases.items()},
        compiler_params=pltpu.CompilerParams(dimension_semantics=("arbitrary",) * len(grid),
                                             vmem_limit_bytes=V7X_VMEM_LIMIT))

    def run(*args):
        res = call(*args, *comm.inputs)
        own = res[:n_out]
        return (own[0] if single else own), list(res[n_out:])

    return run


def permute_in_cols(w):
    z = lambda n: jnp.zeros(w.shape[:-1] + (n,), w.dtype)
    s = lambda a, b: w[..., a:b]
    return jnp.concatenate([
        s(768, 1024), s(1024, 1280), s(1286, 1542), s(1542, 1798),
        s(1280, 1286), z(2), s(2950, 2956), z(LANE - 14),
        s(0, 384), s(384, 768), s(1798, 2182), s(2182, 2566), s(2566, 2950)], axis=-1)


def unpermute_in_cols(g):
    s = lambda a, n: g[..., a:a + n]
    return jnp.concatenate([
        s(OFF_Z, 384), s(OFF_XS, 384), s(OFF_B, 256), s(OFF_C, 256), s(OFF_SM, 6),
        s(OFF_LX, 256), s(OFF_LG, 256), s(OFF_Q, 384), s(OFF_K, 384), s(OFF_V, 384),
        s(OFF_SM + FOX_LANE0, 6)], axis=-1)


def _pick(n, cands):
    for c in cands:
        if n % c == 0:
            return c
    return n


def mm(a, b, *, name, ta=False, tb=False, out_dtypes=(F32,), extras=(), col_params=(), partials=0, epilogue=None,
       tm=None, tn=None, tk=None, carried=None):
    bs = list(b) if isinstance(b, (list, tuple)) else [b]
    pair_sum = isinstance(a, (list, tuple))
    a_list = list(a) if pair_sum else [a]
    assert not pair_sum or len(a_list) == len(bs)
    a = a_list[0]
    n_a = len(a_list)
    n_acc = 1 if pair_sum else len(bs)
    extras = [e if isinstance(e, tuple) else (e, 0) for e in extras]
    M = a.shape[1] if ta else a.shape[0]
    K = a.shape[0] if ta else a.shape[1]
    N = bs[0].shape[0] if tb else bs[0].shape[1]
    tm = tm or _pick(M, (1024, 1408, 512, 256, 128))
    tn = tn or _pick(N, (1024, 1408, 768, 512, 256, 128))
    tk = tk or _pick(K, (1024, 1408, 512, 256, 128))
    nm, nn, nk = M // tm, N // tn, K // tk
    n_b, n_ex, n_cp, n_out = len(bs), len(extras), len(col_params), len(out_dtypes)
    a_bytes, b_bytes = n_a * M * K * a.dtype.itemsize, n_b * K * N * bs[0].dtype.itemsize
    rows_inner = a_bytes * nn + b_bytes <= a_bytes + b_bytes * nm

    def ij(g0, g1):
        return (g1, g0) if rows_inner else (g0, g1)

    def body(*rest):
        a_refs, rest = rest[:n_a], rest[n_a:]
        b_refs, rest = rest[:n_b], rest[n_b:]
        in_refs, rest = rest[:n_ex + n_cp], rest[n_ex + n_cp:]
        out_refs, accs = rest[:n_out + partials], rest[n_out + partials:]
        dn = (((0 if ta else 1,), (1 if tb else 0,)), ((), ()))
        dot = lambda x_ref, y_ref: lax.dot_general(x_ref[...].astype(BF16), y_ref[...].astype(BF16), dn,
                                                   preferred_element_type=F32)
        if pair_sum:
            parts = [functools.reduce(lambda u, v: u + v, [dot(x, y) for x, y in zip(a_refs, b_refs)])]
        else:
            parts = [dot(a_refs[0], b_ref) for b_ref in b_refs]

        def finish(rs):
            outs = epilogue(*rs, *[e[...] for e in in_refs]) if epilogue is not None else tuple(rs)
            for o_ref, o in zip(out_refs[:n_out], outs):
                o_ref[...] = o.astype(o_ref.dtype)
            for o_ref, o in zip(out_refs[n_out:], outs[n_out:]):
                o_ref[0] = o

        if nk == 1:
            finish(parts)
            return
        k = pl.program_id(2)

        @pl.when(k == 0)
        def _():
            for acc, part in zip(accs, parts):
                acc[...] = part

        @pl.when(k > 0)
        def _():
            for acc, part in zip(accs, parts):
                acc[...] += part

        @pl.when(k == nk - 1)
        def _():
            finish([acc[...] for acc in accs])

    def a_map(g0, g1, k):
        i, _ = ij(g0, g1)
        return (k, i) if ta else (i, k)

    def b_map(g0, g1, k):
        _, j = ij(g0, g1)
        return (j, k) if tb else (k, j)

    def ex_map(off, g0, g1, k):
        i, j = ij(g0, g1)
        return (i, j + off)

    a_spec = pl.BlockSpec((tk, tm) if ta else (tm, tk), a_map)
    b_spec = pl.BlockSpec((tn, tk) if tb else (tk, tn), b_map)
    mn_spec = pl.BlockSpec((tm, tn), functools.partial(ex_map, 0))
    comm = carried.take(name) if carried is not None else None
    call = _pc(body, name=name, grid=(nn, nm, nk) if rows_inner else (nm, nn, nk),
               in_specs=([a_spec] * n_a + [b_spec] * n_b
                         + [pl.BlockSpec((tm, tn), functools.partial(ex_map, off)) for _, off in extras]
                         + [pl.BlockSpec((1, tn), lambda g0, g1, k: (0, ij(g0, g1)[1]))] * n_cp),
               out_specs=([mn_spec] * n_out
                          + [pl.BlockSpec((1, 1, tn), lambda g0, g1, k: (ij(g0, g1)[0], 0, ij(g0, g1)[1]))] * partials),
               out_shape=[SDS((M, N), dt) for dt in out_dtypes] + [SDS((nm, 1, N), F32)] * partials,
               scratch=[pltpu.VMEM((tm, tn), F32)] * n_acc if nk > 1 else [],
               sem=("parallel", "parallel", "arbitrary"), comm=comm)
    outs = _run(call, (*a_list, *bs, *[e for e, _ in extras], *col_params), name, comm, carried)
    return outs[0] if len(outs) == 1 else outs


def rowwise(name, fn, rows, params, row_outs, acc_outs=(), tr=512):
    rows = [r if isinstance(r, tuple) else (r, 0, r.shape[1]) for r in rows]
    T = rows[0][0].shape[0]
    tr = min(tr, T)
    n_in, n_ro, n_ac = len(rows) + len(params), len(row_outs), len(acc_outs)

    def body(*refs):
        ins, outs = refs[:n_in], refs[n_in:]
        res = fn(*[r[...] for r in ins])
        if not isinstance(res, (tuple, list)):
            res = (res,)
        for k in range(n_ro):
            outs[k][...] = res[k].astype(outs[k].dtype)
        if n_ac:
            i = pl.program_id(0)

            @pl.when(i == 0)
            def _():
                for k in range(n_ac):
                    outs[n_ro + k][...] = res[n_ro + k]

            @pl.when(i > 0)
            def _():
                for k in range(n_ac):
                    outs[n_ro + k][...] += res[n_ro + k]

    in_specs = ([pl.BlockSpec((tr, w), functools.partial(lambda cb, i: (i, cb), cb)) for (_, cb, w) in rows]
                + [pl.BlockSpec(p.shape, lambda i: (0, 0)) for p in params])
    out_specs = ([pl.BlockSpec((tr, c), lambda i: (i, 0)) for (c, _) in row_outs]
                 + [pl.BlockSpec((1, c), lambda i: (0, 0)) for c in acc_outs])
    out_shape = [SDS((T, c), dt) for (c, dt) in row_outs] + [SDS((1, c), F32) for c in acc_outs]
    outs = _pc(body, name=name, grid=(T // tr,), in_specs=in_specs, out_specs=out_specs, out_shape=out_shape,
               sem=("arbitrary",) if n_ac else ("parallel",))(*[r[0] for r in rows], *params)
    return outs[0] if len(outs) == 1 else outs


def _rms(x, g):
    return x * lax.rsqrt(jnp.mean(x * x, axis=-1, keepdims=True) + EPS) * g


def _softplus(x):
    return jnp.maximum(x, 0.0) + jnp.log(1.0 + jnp.exp(-jnp.abs(x)))


def _silu(x):
    return x * jax.nn.sigmoid(x)


def _gelu(x):
    return 0.5 * x * (1.0 + jnp.tanh(math.sqrt(2.0 / math.pi) * (x + 0.044715 * (x * x * x))))


def _neg_expm1(x):
    series = x * (1 + x / 2 * (1 + x / 3 * (1 + x / 4 * (1 + x / 5 * (1 + x / 6 * (1 + x / 7))))))
    return -jnp.where(jnp.abs(x) < 0.3, series, jnp.exp(x) - 1.0)


def _swiglu(gu):
    return _silu(gu[:, :D_FF]) * gu[:, D_FF:]


def _ple(pg, pp, b):
    return jax.nn.sigmoid(pg + b) * pp


def _ssd_elt(small, xs_act, dtbias_row, alog_row):
    lane = _iota(small.shape, 1)
    dt = _softplus(small + dtbias_row)
    adt = jnp.where(lane < N_HEADS, -jnp.exp(alog_row) * dt, 0.0)
    head = _iota(xs_act.shape, 1) // HEAD_DIM
    dt_exp = jnp.zeros_like(xs_act)
    for h in range(N_HEADS):
        dth = jnp.sum(jnp.where(lane == h, dt, 0.0), axis=1, keepdims=True)
        dt_exp = dt_exp + jnp.where(head == h, dth, 0.0)
    return adt, xs_act * dt_exp


def _fox_elt(small, bf_row):
    lane = _iota(small.shape, 1)
    keep = (lane >= FOX_LANE0) & (lane < FOX_LANE0 + N_HEADS)
    return jnp.where(keep, -_softplus(-(small + bf_row)), 0.0)


def _lru_elt(xl, pre, b_ax, lam):
    r = jax.nn.sigmoid(pre[:, :LRU_W] + b_ax[:, :LRU_W])
    i = jax.nn.sigmoid(pre[:, LRU_W:] + b_ax[:, LRU_W:])
    log_a = -LRU_C * r * _softplus(-lam)
    a = jnp.exp(log_a)
    mult = jnp.sqrt(_neg_expm1(2.0 * log_a))
    return a, mult * (i * xl)


def _mix_post(yraw, xs_act, z, hl, lgate, yfox, dexp, g_ssd, g_lru, g_fox):
    y_ssd = _rms((yraw + xs_act * dexp) * _silu(z), g_ssd)
    y_lru = _rms(hl * _gelu(lgate), g_lru)
    y_fox = _rms(yfox, g_fox)
    return jnp.concatenate([y_ssd, y_lru, y_fox], axis=-1)


def _colsum(x):
    return jnp.sum(x, axis=0, keepdims=True)


def _shift_down(x, d):
    if d == 0:
        return x
    return jnp.where(_iota(x.shape, 0) >= d, pltpu.roll(x, d, 0), 0.0)


def _shift_up(x, d):
    if d == 0:
        return x
    s = x.shape[0]
    return jnp.where(_iota(x.shape, 0) < s - d, pltpu.roll(x, s - d, 0), 0.0)


def _conv_core(x, w, b):
    y = b + w[3:4, :] * x
    for k in range(3):
        y = y + w[k:k + 1, :] * _shift_down(x, 3 - k)
    return y


def seq_conv(name, src, col, width, w8, b, *, batch, silu, out_dtype):
    T = src.shape[0]
    S = T // batch
    c0 = col // LANE

    def body(x_ref, w_ref, b_ref, o_ref):
        y = _conv_core(x_ref[...], w_ref[...], b_ref[...])
        o_ref[...] = (_silu(y) if silu else y).astype(o_ref.dtype)

    return _pc(body, name=name, grid=(batch, width // LANE),
               in_specs=[pl.BlockSpec((S, LANE), lambda bi, ci: (bi, c0 + ci)),
                         pl.BlockSpec((8, LANE), lambda bi, ci: (0, ci)),
                         pl.BlockSpec((1, LANE), lambda bi, ci: (0, ci))],
               out_specs=pl.BlockSpec((S, LANE), lambda bi, ci: (bi, ci)),
               out_shape=SDS((T, width), out_dtype), sem=("parallel", "parallel"))(src, w8, b)


def seq_conv_bwd(name, src, col, width, w8, b, dy, *, batch, silu):
    T = src.shape[0]
    S = T // batch
    c0 = col // LANE

    def body(x_ref, w_ref, b_ref, dy_ref, dx_ref, dw_ref, db_ref):
        x, w = x_ref[...], w_ref[...]
        dpre = dy_ref[...].astype(F32)
        if silu:
            dpre = jax.vjp(_silu, _conv_core(x, w, b_ref[...]))[1](dpre)[0]
        dx = w[3:4, :] * dpre
        for k in range(3):
            dx = dx + w[k:k + 1, :] * _shift_up(dpre, 3 - k)
        dx_ref[...] = dx.astype(dx_ref.dtype)
        row8 = _iota((8, LANE), 0)
        dw = jnp.zeros((8, LANE), F32)
        for k in range(4):
            dw = dw + jnp.where(row8 == k, _colsum(dpre * _shift_down(x, 3 - k)), 0.0)
        db = _colsum(dpre)
        bi = pl.program_id(1)

        @pl.when(bi == 0)
        def _():
            dw_ref[...] = dw
            db_ref[...] = db

        @pl.when(bi > 0)
        def _():
            dw_ref[...] += dw
            db_ref[...] += db

    return _pc(body, name=name, grid=(width // LANE, batch),
               in_specs=[pl.BlockSpec((S, LANE), lambda ci, bi: (bi, c0 + ci)),
                         pl.BlockSpec((8, LANE), lambda ci, bi: (0, ci)),
                         pl.BlockSpec((1, LANE), lambda ci, bi: (0, ci)),
                         pl.BlockSpec((S, LANE), lambda ci, bi: (bi, ci))],
               out_specs=[pl.BlockSpec((S, LANE), lambda ci, bi: (bi, ci)),
                          pl.BlockSpec((8, LANE), lambda ci, bi: (0, ci)),
                          pl.BlockSpec((1, LANE), lambda ci, bi: (0, ci))],
               out_shape=[SDS((T, width), BF16), SDS((8, width), F32), SDS((1, width), F32)],
               sem=("parallel", "arbitrary"))(src, w8, b, dy)


def _split3_dot(tri, x):
    hi = x.astype(BF16)
    r1 = x - hi.astype(F32)
    mid = r1.astype(BF16)
    lo = (r1 - mid.astype(F32)).astype(BF16)
    d = lambda v: jnp.dot(tri, v, preferred_element_type=F32)
    return d(hi) + d(mid) + d(lo)


def seq_cumsum(name, x, *, batch, reverse=False, nsum=1, trow=None):
    T = x.shape[0]
    S = T // batch
    ch = min(256, S)
    nch = S // ch

    def body(x_ref, o_ref, *maybe_t):
        r, c = _iota((ch, ch), 0), _iota((ch, ch), 1)
        tri = jnp.where((c >= r) if reverse else (c <= r), 1.0, 0.0).astype(BF16)
        carry = jnp.zeros((1, LANE), F32)
        for k in (range(nch - 1, -1, -1) if reverse else range(nch)):
            xc = x_ref[k * ch:(k + 1) * ch, 0:LANE]
            for m in range(1, nsum):
                xc = xc + x_ref[k * ch:(k + 1) * ch, m * LANE:(m + 1) * LANE]
            o_ref[k * ch:(k + 1) * ch, :] = _split3_dot(tri, xc) + carry
            carry = carry + _colsum(xc)
        if trow is not None:
            maybe_t[0][...] = o_ref[...].T[trow:trow + 8, :]

    out_specs = [pl.BlockSpec((S, LANE), lambda bi: (bi, 0))]
    out_shape = [SDS((T, LANE), F32)]
    if trow is not None:
        out_specs.append(pl.BlockSpec((8, S), lambda bi: (bi, 0)))
        out_shape.append(SDS((batch * 8, S), F32))
    outs = _pc(body, name=name, grid=(batch,), in_specs=[pl.BlockSpec((S, LANE * nsum), lambda bi: (bi, 0))],
               out_specs=out_specs, out_shape=out_shape, sem=("parallel",))(x)
    return outs if trow is not None else outs[0]


def lru_scan(name, a, u, *, batch):
    T, W = a.shape
    S = T // batch

    def body(a_ref, u_ref, h_ref):
        row = _iota((8, W), 0)

        def step(g, h):
            off = pl.multiple_of(g * 8, 8)
            at, ut = a_ref[pl.ds(off, 8), :], u_ref[pl.ds(off, 8), :]
            acc = jnp.zeros((8, W), F32)
            for r in range(8):
                h = at[r:r + 1, :] * h + ut[r:r + 1, :]
                acc = jnp.where(row == r, jnp.broadcast_to(h, (8, W)), acc)
            h_ref[pl.ds(off, 8), :] = acc
            return h

        lax.fori_loop(0, S // 8, step, jnp.zeros((1, W), F32))

    spec = pl.BlockSpec((S, W), lambda bi: (bi, 0))
    return _pc(body, name=name, grid=(batch,), in_specs=[spec, spec], out_specs=spec,
               out_shape=SDS((T, W), F32), sem=("parallel",))(a, u)


def lru_scan_bwd(name, a, h, dh, *, batch):
    T, W = a.shape
    S = T // batch
    ng = S // 8

    def body(a_ref, h_ref, dh_ref, da_ref, du_ref):
        row = _iota((8, W), 0)

        def step(k, c):
            g_idx = ng - 1 - k
            off = pl.multiple_of(g_idx * 8, 8)
            offp = pl.multiple_of(jnp.maximum(g_idx - 1, 0) * 8, 8)
            at, ht, dt = a_ref[pl.ds(off, 8), :], h_ref[pl.ds(off, 8), :], dh_ref[pl.ds(off, 8), :]
            hp = jnp.where(g_idx > 0, h_ref[pl.ds(offp, 8), :], 0.0)
            da = jnp.zeros((8, W), F32)
            du = jnp.zeros((8, W), F32)
            for r in range(7, -1, -1):
                g = dt[r:r + 1, :] + c
                hprev = ht[r - 1:r, :] if r > 0 else hp[7:8, :]
                du = jnp.where(row == r, jnp.broadcast_to(g, (8, W)), du)
                da = jnp.where(row == r, jnp.broadcast_to(g * hprev, (8, W)), da)
                c = at[r:r + 1, :] * g
            da_ref[pl.ds(off, 8), :] = da
            du_ref[pl.ds(off, 8), :] = du
            return c

        lax.fori_loop(0, ng, step, jnp.zeros((1, W), F32))

    spec = pl.BlockSpec((S, W), lambda bi: (bi, 0))
    return _pc(body, name=name, grid=(batch,), in_specs=[spec] * 3, out_specs=[spec] * 2,
               out_shape=[SDS((T, W), F32)] * 2, sem=("parallel",))(a, h, dh)


def _nt(a, b):
    return lax.dot_general(a, b, (((1,), (1,)), ((), ())), preferred_element_type=F32)


def _tn(a, b):
    return lax.dot_general(a, b, (((0,), (0,)), ((), ())), preferred_element_type=F32)


def _tile(S, t=256):
    return min(t, S)


def ssd_attn_fwd(name, cm, bm, xd, cum, cum_t, *, batch, carried=None):
    T = cm.shape[0]
    S = T // batch
    tq = tk = _tile(S)
    nq = S // tq

    def body(c_ref, b_ref, x_ref, cum_ref, cumt_ref, y_ref):
        i = pl.program_id(1)
        cq, cmq = cum_ref[...], c_ref[...]
        rowi, coli = _iota((tq, tk), 0), _iota((tq, tk), 1)
        half = _iota((tk, LANE), 1) // HEAD_DIM

        def step(j, accs, diag):
            off = pl.multiple_of(j * tk, tk)
            bj = b_ref[pl.ds(off, tk), :]
            gm = [_nt(cmq[:, g * LANE:(g + 1) * LANE], bj[:, g * LANE:(g + 1) * LANE]) for g in range(2)]
            ckt = cumt_ref[:, pl.ds(off, tk)]
            new = []
            for p in range(3):
                xp = x_ref[pl.ds(off, tk), p * LANE:(p + 1) * LANE]
                ws, xs = [], []
                for hh in range(2):
                    h = 2 * p + hh
                    seg = cq[:, h:h + 1] - ckt[h:h + 1, :]
                    e = jnp.exp(jnp.where(rowi >= coli, seg, -jnp.inf) if diag else seg)
                    ws.append((gm[h // 3] * e).astype(BF16))
                    xs.append(jnp.where(half == hh, xp, jnp.zeros_like(xp)))
                new.append(accs[p] + jnp.dot(jnp.concatenate(ws, axis=1), jnp.concatenate(xs, axis=0),
                                             preferred_element_type=F32))
            return tuple(new)

        accs = lax.fori_loop(0, i, functools.partial(step, diag=False),
                             tuple(jnp.zeros((tq, LANE), F32) for _ in range(3)))
        accs = step(i, accs, True)
        y_ref[...] = jnp.concatenate(accs, axis=1)

    comm = carried.take(name) if carried is not None else None
    call = _pc(body, name=name, grid=(batch, nq),
               in_specs=[pl.BlockSpec((tq, 256), lambda b, i: (b * nq + i, 0)),
                         pl.BlockSpec((S, 256), lambda b, i: (b, 0)),
                         pl.BlockSpec((S, SSD_W), lambda b, i: (b, 0)),
                         pl.BlockSpec((tq, LANE), lambda b, i: (b * nq + i, 0)),
                         pl.BlockSpec((8, S), lambda b, i: (b, 0))],
               out_specs=pl.BlockSpec((tq, SSD_W), lambda b, i: (b * nq + i, 0)),
               out_shape=SDS((T, SSD_W), F32), sem=("parallel", "parallel"), comm=comm)
    return _run(call, (cm, bm, xd, cum, cum_t), name, comm, carried)


def ssd_attn_bwd(name, cm, bm, xd, cum, cum_t, dy, *, batch, carried=None):
    T = cm.shape[0]
    S = T // batch
    tq = tk = _tile(S, 512)
    nq = S // tq

    def body(c_ref, b_ref, x_ref, cum_ref, cumt_ref, dy_ref, dx_ref, db_ref, dc_ref, dcum_ref, dcumt_ref):
        dx_ref[...] = jnp.zeros_like(dx_ref)
        db_ref[...] = jnp.zeros_like(db_ref)
        dcum_ref[...] = jnp.zeros_like(dcum_ref)
        dcumt_ref[...] = jnp.zeros_like(dcumt_ref)
        rowi, coli = _iota((tq, tk), 0), _iota((tq, tk), 1)
        halfq = _iota((tq, LANE), 1) // HEAD_DIM
        lane_q = _iota((tq, LANE), 1)

        def qblock(i, _):
            qoff = pl.multiple_of(i * tq, tq)
            cq = cum_ref[pl.ds(qoff, tq), :]
            cmq = c_ref[pl.ds(qoff, tq), :]
            dyq = dy_ref[pl.ds(qoff, tq), :]
            dyh = [[jnp.where(halfq == hh, dyq[:, p * LANE:(p + 1) * LANE], 0.0).astype(BF16) for hh in range(2)]
                   for p in range(3)]

            def step(j, carry, diag):
                dcq, rs_acc = carry
                off = pl.multiple_of(j * tk, tk)
                bj = b_ref[pl.ds(off, tk), :]
                gm = [_nt(cmq[:, g * LANE:(g + 1) * LANE], bj[:, g * LANE:(g + 1) * LANE]) for g in range(2)]
                ckt = cumt_ref[:, pl.ds(off, tk)]
                dgm = [jnp.zeros((tq, tk), F32), jnp.zeros((tq, tk), F32)]
                for p in range(3):
                    xp = x_ref[pl.ds(off, tk), p * LANE:(p + 1) * LANE]
                    ws = []
                    for hh in range(2):
                        h = 2 * p + hh
                        seg = cq[:, h:h + 1] - ckt[h:h + 1, :]
                        e = jnp.exp(jnp.where(rowi >= coli, seg, -jnp.inf) if diag else seg)
                        w = gm[h // 3] * e
                        dw = _nt(dyh[p][hh], xp)
                        zz = dw * w
                        rs_acc = rs_acc + jnp.where(lane_q == h, jnp.sum(zz, axis=1, keepdims=True), 0.0)
                        dcumt_ref[h:h + 1, pl.ds(off, tk)] += _colsum(zz)
                        dgm[h // 3] = dgm[h // 3] + dw * e
                        ws.append(w.astype(BF16))
                    dx_ref[pl.ds(off, tk), p * LANE:(p + 1) * LANE] += _tn(
                        jnp.concatenate(ws, axis=0), jnp.concatenate(dyh[p], axis=0))
                new_dcq = []
                for g in range(2):
                    dg = dgm[g].astype(BF16)
                    new_dcq.append(dcq[g] + jnp.dot(dg, bj[:, g * LANE:(g + 1) * LANE], preferred_element_type=F32))
                    db_ref[pl.ds(off, tk), g * LANE:(g + 1) * LANE] += _tn(dg, cmq[:, g * LANE:(g + 1) * LANE])
                return tuple(new_dcq), rs_acc

            carry = lax.fori_loop(
                0, i, functools.partial(step, diag=False),
                ((jnp.zeros((tq, LANE), F32), jnp.zeros((tq, LANE), F32)), jnp.zeros((tq, LANE), F32)))
            dcq, rs_acc = step(i, carry, True)
            dc_ref[pl.ds(qoff, tq), :] = jnp.concatenate(dcq, axis=1)
            dcum_ref[pl.ds(qoff, tq), :] += rs_acc
            return 0

        lax.fori_loop(0, nq, qblock, 0)
        dcum_ref[...] = dcum_ref[...] - dcumt_ref[...].T

    s256 = pl.BlockSpec((S, 256), lambda b: (b, 0))
    s384 = pl.BlockSpec((S, SSD_W), lambda b: (b, 0))
    s128 = pl.BlockSpec((S, LANE), lambda b: (b, 0))
    comm = carried.take(name) if carried is not None else None
    call = _pc(body, name=name, grid=(batch,),
               in_specs=[s256, s256, s384, s128, pl.BlockSpec((8, S), lambda b: (b, 0)), s384],
               out_specs=[s384, s256, s256, s128],
               out_shape=[SDS((T, SSD_W), F32), SDS((T, 256), F32), SDS((T, 256), F32), SDS((T, LANE), F32)],
               scratch=[pltpu.VMEM((LANE, S), F32)], sem=("parallel",), comm=comm)
    return _run(call, (cm, bm, xd, cum, cum_t, dy), name, comm, carried)


NEG_BIG = -1e30


def fox_attn_fwd(name, proj, cum, cum_t, *, batch, carried=None):
    T = proj.shape[0]
    S = T // batch
    tq = tk = _tile(S, 512)
    nq = S // tq
    scale = HEAD_DIM ** -0.5
    qb, kb, vb = OFF_Q // LANE, OFF_K // LANE, OFF_V // LANE

    def body(q_ref, k_ref, v_ref, cum_ref, cumt_ref, o_ref, lse_ref):
        p, i = pl.program_id(1), pl.program_id(2)
        cq = cum_ref[...]
        lane_q = _iota((tq, LANE), 1)
        halfq, halfk = lane_q // HEAD_DIM, _iota((tk, LANE), 1) // HEAD_DIM
        qs = q_ref[...] * scale
        qh = [jnp.where(halfq == hh, qs, 0.0).astype(BF16) for hh in range(2)]
        rowi, coli = _iota((tq, tk), 0), _iota((tq, tk), 1)
        cqh = [jnp.sum(jnp.where(lane_q == FOX_LANE0 + 2 * p + hh, cq, 0.0), axis=1, keepdims=True) for hh in range(2)]
        row8 = _iota((8, tk), 0)

        def step(j, carry, diag):
            ms, ls, acc = carry
            off = pl.multiple_of(j * tk, tk)
            kj = k_ref[pl.ds(off, tk), :].astype(BF16)
            vj = v_ref[pl.ds(off, tk), :].astype(BF16)
            ckt = cumt_ref[:, pl.ds(off, tk)]
            ps, vs, new_m, new_l, alphas = [], [], [], [], []
            for hh in range(2):
                ck = jnp.sum(jnp.where(row8 == 2 * p + hh, ckt, 0.0), axis=0, keepdims=True)
                logits = _nt(qh[hh], kj) + (cqh[hh] - ck)
                if diag:
                    logits = jnp.where(rowi >= coli, logits, -jnp.inf)
                m = jnp.maximum(ms[hh], jnp.max(logits, axis=1, keepdims=True))
                alpha = jnp.exp(ms[hh] - m)
                pr = jnp.exp(logits - m)
                new_m.append(m)
                new_l.append(alpha * ls[hh] + jnp.sum(pr, axis=1, keepdims=True))
                alphas.append(alpha)
                ps.append(pr.astype(BF16))
                vs.append(jnp.where(halfk == hh, vj, jnp.zeros_like(vj)))
            acc = acc * jnp.where(halfq == 0, alphas[0], alphas[1]) + jnp.dot(
                jnp.concatenate(ps, axis=1), jnp.concatenate(vs, axis=0), preferred_element_type=F32)
            return tuple(new_m), tuple(new_l), acc

        init = ((jnp.full((tq, 1), NEG_BIG, F32),) * 2, (jnp.zeros((tq, 1), F32),) * 2, jnp.zeros((tq, LANE), F32))
        ms, ls, acc = step(i, lax.fori_loop(0, i, functools.partial(step, diag=False), init), True)
        o_ref[...] = acc / jnp.where(halfq == 0, ls[0], ls[1])
        lse_ref[...] = (jnp.where(lane_q == 0, ms[0] + jnp.log(ls[0]), 0.0)
                        + jnp.where(lane_q == 1, ms[1] + jnp.log(ls[1]), 0.0))

    comm = carried.take(name) if carried is not None else None
    call = _pc(body, name=name, grid=(batch, 3, nq),
               in_specs=[pl.BlockSpec((tq, LANE), lambda b, p, i: (b * nq + i, qb + p)),
                         pl.BlockSpec((S, LANE), lambda b, p, i: (b, kb + p)),
                         pl.BlockSpec((S, LANE), lambda b, p, i: (b, vb + p)),
                         pl.BlockSpec((tq, LANE), lambda b, p, i: (b * nq + i, 0)),
                         pl.BlockSpec((8, S), lambda b, p, i: (b, 0))],
               out_specs=[pl.BlockSpec((tq, LANE), lambda b, p, i: (b * nq + i, p))] * 2,
               out_shape=[SDS((T, FOX_W), F32)] * 2, sem=("parallel", "parallel", "parallel"), comm=comm)
    return _run(call, (proj, proj, proj, cum, cum_t), name, comm, carried)


def fox_attn_bwd(name, proj, o, do, lse, cum, cum_t, *, batch, carried=None):
    T = proj.shape[0]
    S = T // batch
    tq = tk = _tile(S, 512)
    nq = S // tq
    scale = HEAD_DIM ** -0.5
    qb, kb, vb = OFF_Q // LANE, OFF_K // LANE, OFF_V // LANE

    def body(q_ref, k_ref, v_ref, o_ref, do_ref, lse_ref, cum_ref, cumt_ref,
             dq_ref, dk_ref, dv_ref, dcum_ref, dk_acc, dv_acc, dcumt_ref):
        p = pl.program_id(1)
        dk_acc[...] = jnp.zeros_like(dk_acc)
        dv_acc[...] = jnp.zeros_like(dv_acc)
        dcum_ref[...] = jnp.zeros_like(dcum_ref)
        dcumt_ref[...] = jnp.zeros_like(dcumt_ref)
        lane_q = _iota((tq, LANE), 1)
        halfq, halfk = lane_q // HEAD_DIM, _iota((tk, LANE), 1) // HEAD_DIM
        rowi, coli = _iota((tq, tk), 0), _iota((tq, tk), 1)
        row8 = _iota((8, tk), 0)

        def qblock(i, _):
            qoff = pl.multiple_of(i * tq, tq)
            cq = cum_ref[pl.ds(qoff, tq), :]
            qs = q_ref[pl.ds(qoff, tq), :] * scale
            doq = do_ref[pl.ds(qoff, tq), :]
            lse = lse_ref[pl.ds(qoff, tq), :]
            delta = doq * o_ref[pl.ds(qoff, tq), :]
            qh, doh, cqh, lseh, dlt = [], [], [], [], []
            for hh in range(2):
                qh.append(jnp.where(halfq == hh, qs, 0.0).astype(BF16))
                doh.append(jnp.where(halfq == hh, doq, 0.0).astype(BF16))
                cqh.append(jnp.sum(jnp.where(lane_q == FOX_LANE0 + 2 * p + hh, cq, 0.0), axis=1, keepdims=True))
                lseh.append(jnp.sum(jnp.where(lane_q == hh, lse, 0.0), axis=1, keepdims=True))
                dlt.append(jnp.sum(jnp.where(halfq == hh, delta, 0.0), axis=1, keepdims=True))

            def step(j, carry, diag):
                dq, rs = carry
                off = pl.multiple_of(j * tk, tk)
                kj = k_ref[pl.ds(off, tk), :].astype(BF16)
                vj = v_ref[pl.ds(off, tk), :].astype(BF16)
                ckt = cumt_ref[:, pl.ds(off, tk)]
                dss, prs, ks = [], [], []
                for hh in range(2):
                    ck = jnp.sum(jnp.where(row8 == 2 * p + hh, ckt, 0.0), axis=0, keepdims=True)
                    logits = _nt(qh[hh], kj) + ((cqh[hh] - lseh[hh]) - ck)
                    if diag:
                        logits = jnp.where(rowi >= coli, logits, -jnp.inf)
                    pr = jnp.exp(logits)
                    ds = pr * (_nt(doh[hh], vj) - dlt[hh])
                    rs = rs + jnp.where(lane_q == FOX_LANE0 + 2 * p + hh, jnp.sum(ds, axis=1, keepdims=True), 0.0)
                    cs = _colsum(ds)
                    dcumt_ref[0:8, pl.ds(off, tk)] += jnp.where(row8 == 2 * p + hh, cs, 0.0)
                    dss.append(ds.astype(BF16))
                    prs.append(pr.astype(BF16))
                    ks.append(jnp.where(halfk == hh, kj, jnp.zeros_like(kj)))
                dq = dq + jnp.dot(jnp.concatenate(dss, axis=1), jnp.concatenate(ks, axis=0), preferred_element_type=F32)
                dk_acc[pl.ds(off, tk), :] += _tn(jnp.concatenate(dss, axis=0), jnp.concatenate(qh, axis=0))
                dv_acc[pl.ds(off, tk), :] += _tn(jnp.concatenate(prs, axis=0), jnp.concatenate(doh, axis=0))
                return dq, rs

            carry = lax.fori_loop(0, i, functools.partial(step, diag=False),
                                  (jnp.zeros((tq, LANE), F32), jnp.zeros((tq, LANE), F32)))
            dq, rs = step(i, carry, True)
            dq_ref[pl.ds(qoff, tq), :] = (dq * scale).astype(dq_ref.dtype)
            dcum_ref[pl.ds(qoff, tq), :] += rs
            return 0

        lax.fori_loop(0, nq, qblock, 0)
        dk_ref[...] = dk_acc[...].astype(dk_ref.dtype)
        dv_ref[...] = dv_acc[...].astype(dv_ref.dtype)
        dct = dcumt_ref[...].T
        dcum_ref[...] = dcum_ref[...] - pltpu.roll(dct, FOX_LANE0, 1)

    sp = lambda c0: pl.BlockSpec((S, LANE), lambda b, p: (b, c0 + p))
    s0 = pl.BlockSpec((S, LANE), lambda b, p: (b, 0))
    comm = carried.take(name) if carried is not None else None
    call = _pc(body, name=name, grid=(batch, 3),
               in_specs=[sp(qb), sp(kb), sp(vb), sp(0), sp(0), sp(0), s0, pl.BlockSpec((8, S), lambda b, p: (b, 0))],
               out_specs=[sp(0)] * 4,
               out_shape=[SDS((T, FOX_W), BF16)] * 3 + [SDS((T, FOX_W), F32)],
               scratch=[pltpu.VMEM((S, LANE), F32), pltpu.VMEM((S, LANE), F32), pltpu.VMEM((LANE, S), F32)],
               sem=("parallel", "parallel"), comm=comm)
    return _run(call, (proj, proj, proj, o, do, lse, cum, cum_t), name, comm, carried)


def _row(v, width=None, at=0):
    v = v.astype(F32)
    width = width or v.shape[0]
    return jnp.pad(v, (at, width - at - v.shape[0]))[None, :]


def _pad8(w4):
    return jnp.pad(w4.astype(F32), ((0, 4), (0, 0)))


def _block_diag(w):
    eye = jnp.eye(w.shape[0], dtype=w.dtype)
    return (w[:, :, None, :] * eye[:, None, :, None]).reshape(LRU_W, LRU_W)


def prep_layer(f):
    cw, cb = f["ssd_conv_w"], f["ssd_conv_b"]
    return LazyDict(
        win=lambda: permute_in_cols(f["w_in"]), wout=lambda: f["w_out"], wg=lambda: f["w_gate"], wu=lambda: f["w_up"],
        wd=lambda: f["w_down"], wpg=lambda: f["w_ple_gate"], wpp=lambda: f["w_ple_proj"],
        wax=jnp.concatenate([_block_diag(f["lru_w_a"]), _block_diag(f["lru_w_x"])], axis=1),
        g1=_row(f["norm1_g"]), g2=_row(f["norm2_g"]), g3=_row(f["norm3_g"]),
        cw_xs=_pad8(cw[:, :384]), cb_xs=_row(cb[:384]), cw_b=_pad8(cw[:, 384:640]), cb_b=_row(cb[384:640]),
        cw_c=_pad8(cw[:, 640:]), cb_c=_row(cb[640:]), cw_l=_pad8(f["lru_conv_w"]), cb_l=_row(f["lru_conv_b"]),
        dtbias_row=_row(f["ssd_dt_bias"], LANE), alog_row=_row(f["ssd_a_log"], LANE),
        dexp=jnp.repeat(f["ssd_d"].astype(F32), HEAD_DIM)[None, :], g_ssd=_row(f["ssd_norm_g"]),
        b_ax=_row(jnp.concatenate([f["lru_b_a"], f["lru_b_x"]])), lam=_row(f["lru_lambda"]), g_lru=_row(f["lru_norm_g"]),
        bf_row=_row(f["fox_b_f"], LANE, FOX_LANE0), g_fox=_row(f["fox_norm_g"]), b_pg=_row(f["b_ple_gate"]))


def unprep_grads(g):
    blocks = lambda m: jnp.stack([m[i * 64:(i + 1) * 64, i * 64:(i + 1) * 64] for i in range(4)])
    return dict(
        norm1_g=g["g1"][0], w_in=unpermute_in_cols(g["win"]),
        ssd_conv_w=jnp.concatenate([g["cw_xs"][:4], g["cw_b"][:4], g["cw_c"][:4]], axis=1),
        ssd_conv_b=jnp.concatenate([g["cb_xs"][0], g["cb_b"][0], g["cb_c"][0]]),
        ssd_dt_bias=g["dtbias_row"][0, :N_HEADS], ssd_a_log=g["alog_row"][0, :N_HEADS],
        ssd_d=jnp.sum(g["dexp"].reshape(N_HEADS, HEAD_DIM), axis=1), ssd_norm_g=g["g_ssd"][0],
        lru_conv_w=g["cw_l"][:4], lru_conv_b=g["cb_l"][0],
        lru_w_a=blocks(g["wax"][:, :LRU_W]), lru_b_a=g["b_ax"][0, :LRU_W],
        lru_w_x=blocks(g["wax"][:, LRU_W:]), lru_b_x=g["b_ax"][0, LRU_W:],
        lru_lambda=g["lam"][0], lru_norm_g=g["g_lru"][0],
        fox_b_f=g["bf_row"][0, FOX_LANE0:FOX_LANE0 + N_HEADS], fox_norm_g=g["g_fox"][0],
        w_out=g["wout"], norm2_g=g["g2"][0], w_gate=g["wg"], w_up=g["wu"], w_down=g["wd"],
        norm3_g=g["g3"][0], w_ple_gate=g["wpg"], b_ple_gate=g["b_pg"][0], w_ple_proj=g["wpp"])


def _view(a, off, width):
    return (a, off // width, width)


def _add_epilogue(acc, e):
    return (acc + e,)


def mixer_fwd(proj, w, batch, tag, carried=None):
    sm = _view(proj, OFF_SM, LANE)
    conv = functools.partial(seq_conv, batch=batch)
    cmc = conv(f"{tag}_conv_c", proj, OFF_C, 256, w["cw_c"], w["cb_c"], silu=True, out_dtype=BF16)
    bmc = conv(f"{tag}_conv_b", proj, OFF_B, 256, w["cw_b"], w["cb_b"], silu=True, out_dtype=BF16)
    xs_act = conv(f"{tag}_conv_xs", proj, OFF_XS, SSD_W, w["cw_xs"], w["cb_xs"], silu=True, out_dtype=F32)
    xl = conv(f"{tag}_conv_l", proj, OFF_LX, LRU_W, w["cw_l"], w["cb_l"], silu=False, out_dtype=F32)
    adt, xd = rowwise(f"{tag}_ssd_elt", _ssd_elt, [sm, xs_act], [w["dtbias_row"], w["alog_row"]],
                      [(LANE, F32), (SSD_W, BF16)])
    cum_a, cum_at = seq_cumsum(f"{tag}_cum_a", adt, batch=batch, trow=0)
    yraw = ssd_attn_fwd(f"{tag}_ssd_fwd", cmc, bmc, xd, cum_a, cum_at, batch=batch, carried=carried)
    logf = rowwise(f"{tag}_fox_elt", _fox_elt, [sm], [w["bf_row"]], [(LANE, F32)])
    cum_f, cum_ft = seq_cumsum(f"{tag}_cum_f", logf, batch=batch, trow=FOX_LANE0)
    o, lse = fox_attn_fwd(f"{tag}_fox_fwd", proj, cum_f, cum_ft, batch=batch, carried=carried)
    pre = mm(xl, w["wax"], name=f"{tag}_mm_lru_gates")
    a, u = rowwise(f"{tag}_lru_elt", _lru_elt, [xl, pre], [w["b_ax"], w["lam"]], [(LRU_W, F32), (LRU_W, F32)])
    hl = lru_scan(f"{tag}_lru_scan", a, u, batch=batch)
    ycat = rowwise(f"{tag}_mix_post", _mix_post,
                   [yraw, xs_act, _view(proj, OFF_Z, SSD_W), hl, _view(proj, OFF_LG, LRU_W), o],
                   [w["dexp"], w["g_ssd"], w["g_lru"], w["g_fox"]], [(D_MODEL, BF16)], tr=256)
    saved = dict(cmc=cmc, bmc=bmc, xs_act=xs_act, xl=xl, xd=xd, cum_a=cum_a, cum_at=cum_at, yraw=yraw,
                 cum_f=cum_f, cum_ft=cum_ft, o=o, lse=lse, pre=pre, a=a, hl=hl)
    return ycat, saved


def mixer_bwd(dycat, proj, w, s, batch, tag, carried=None):
    sm = _view(proj, OFF_SM, LANE)
    g = {}

    def post_bwd(yraw, xs_act, z, hl, lg, o, dyc, dexp, g_ssd, g_lru, g_fox):
        return jax.vjp(_mix_post, yraw, xs_act, z, hl, lg, o, dexp, g_ssd, g_lru, g_fox)[1](dyc)

    (dyraw, dxs1, dz, dhl, dlg, do, g["dexp"], g["g_ssd"], g["g_lru"], g["g_fox"]) = rowwise(
        f"{tag}_mix_post_bwd", post_bwd,
        [s["yraw"], s["xs_act"], _view(proj, OFF_Z, SSD_W), s["hl"], _view(proj, OFF_LG, LRU_W), s["o"], dycat],
        [w["dexp"], w["g_ssd"], w["g_lru"], w["g_fox"]],
        [(SSD_W, F32), (SSD_W, F32), (SSD_W, BF16), (LRU_W, F32), (LRU_W, BF16), (FOX_W, F32)],
        [SSD_W, SSD_W, LRU_W, FOX_W], tr=256)

    dq, dk, dv, dcum3 = fox_attn_bwd(f"{tag}_fox_bwd", proj, s["o"], do, s["lse"], s["cum_f"], s["cum_ft"], batch=batch,
                                     carried=carried)
    dlogf = seq_cumsum(f"{tag}_rcum_f", dcum3, batch=batch, reverse=True, nsum=3)

    dxd, dbm, dcm, dcum_a = ssd_attn_bwd(f"{tag}_ssd_bwd", s["cmc"], s["bmc"], s["xd"], s["cum_a"], s["cum_at"], dyraw,
                                         batch=batch, carried=carried)
    dadt = seq_cumsum(f"{tag}_rcum_a", dcum_a, batch=batch, reverse=True)

    def ssd_elt_bwd(small, xs_act, dadt_, dxd_, dxs1_, dtbias, alog):
        dsm, dxs, ddtb, dalog = jax.vjp(_ssd_elt, small, xs_act, dtbias, alog)[1]((dadt_, dxd_))
        return dsm, dxs + dxs1_, ddtb, dalog

    dsm_s, dxs_act, g["dtbias_row"], g["alog_row"] = rowwise(
        f"{tag}_ssd_elt_bwd", ssd_elt_bwd, [sm, s["xs_act"], dadt, dxd, dxs1], [w["dtbias_row"], w["alog_row"]],
        [(LANE, F32), (SSD_W, F32)], [LANE, LANE])

    def fox_elt_bwd(small, dlogf_, dsm_s_, bf_row):
        dsm, dbf = jax.vjp(_fox_elt, small, bf_row)[1](dlogf_)
        return dsm + dsm_s_, dbf

    dsm, g["bf_row"] = rowwise(f"{tag}_fox_elt_bwd", fox_elt_bwd, [sm, dlogf, dsm_s], [w["bf_row"]],
                               [(LANE, BF16)], [LANE])

    cbwd = functools.partial(seq_conv_bwd, batch=batch)
    dxs_raw, g["cw_xs"], g["cb_xs"] = cbwd(f"{tag}_conv_xs_bwd", proj, OFF_XS, SSD_W, w["cw_xs"], w["cb_xs"], dxs_act, silu=True)
    db_raw, g["cw_b"], g["cb_b"] = cbwd(f"{tag}_conv_b_bwd", proj, OFF_B, 256, w["cw_b"], w["cb_b"], dbm, silu=True)
    dc_raw, g["cw_c"], g["cb_c"] = cbwd(f"{tag}_conv_c_bwd", proj, OFF_C, 256, w["cw_c"], w["cb_c"], dcm, silu=True)

    da, du = lru_scan_bwd(f"{tag}_lru_scan_bwd", s["a"], s["hl"], dhl, batch=batch)

    def lru_elt_bwd(xl, pre, da_, du_, b_ax, lam):
        return jax.vjp(_lru_elt, xl, pre, b_ax, lam)[1]((da_, du_))

    dxl1, dpre, g["b_ax"], g["lam"] = rowwise(
        f"{tag}_lru_elt_bwd", lru_elt_bwd, [s["xl"], s["pre"], da, du], [w["b_ax"], w["lam"]],
        [(LRU_W, F32), (2 * LRU_W, BF16)], [2 * LRU_W, LRU_W])
    g["wax"] = mm(s["xl"], dpre, ta=True, name=f"{tag}_mm_dwax")
    dxl = mm(dpre, w["wax"], tb=True, extras=[dxl1], epilogue=_add_epilogue, name=f"{tag}_mm_dxl")
    dlx_raw, g["cw_l"], g["cb_l"] = cbwd(f"{tag}_conv_l_bwd", proj, OFF_LX, LRU_W, w["cw_l"], w["cb_l"], dxl, silu=False)

    dproj = jnp.concatenate([db_raw, dc_raw, dlx_raw, dlg, dsm, dz, dxs_raw, dq, dk, dv], axis=1)
    return dproj, g


def layer_fwd(h0, p_l, w, batch, tag, carried=None):
    u1 = rowwise(f"{tag}_rms1", _rms, [h0], [w["g1"]], [(D_MODEL, BF16)])
    proj = mm(u1, w["win"], name=f"{tag}_mm_in", carried=carried)
    ycat, ms = mixer_fwd(proj, w, batch, tag, carried)
    h1 = mm(ycat, w["wout"], extras=[h0], epilogue=_add_epilogue, name=f"{tag}_mm_out")
    u2 = rowwise(f"{tag}_rms2", _rms, [h1], [w["g2"]], [(D_MODEL, BF16)])
    gate, up, act = mm(u2, [w["wg"], w["wu"]], out_dtypes=(BF16, BF16, BF16), epilogue=_swiglu_epilogue,
                       tm=512, tn=D_FF // 2, name=f"{tag}_mm_gu", carried=carried)
    h2 = mm(act, w["wd"], extras=[h1], epilogue=_add_epilogue, tm=512, tk=D_FF, name=f"{tag}_mm_down")
    u3 = rowwise(f"{tag}_rms3", _rms, [h2], [w["g3"]], [(D_MODEL, BF16)])
    pp = mm(p_l, w["wpp"], name=f"{tag}_mm_pp")
    h3, pg = mm(u3, w["wpg"], extras=[pp, h2], col_params=[w["b_pg"]], epilogue=_ple_epilogue,
                out_dtypes=(F32, F32), tm=512, name=f"{tag}_mm_pg")
    saved = dict(h0=h0, u1=u1, proj=proj, ycat=ycat, h1=h1, u2=u2, gate=gate, up=up, act=act, h2=h2, u3=u3, pg=pg,
                 pp=pp, mixer=ms)
    return h3, saved


def _swiglu_epilogue(acc_g, acc_u):
    return acc_g, acc_u, _silu(acc_g) * acc_u


def _swiglu_bwd_epilogue(dact, gate, up):
    return jax.vjp(lambda g_, u_: _silu(g_) * u_, gate.astype(F32), up.astype(F32))[1](dact)


def _ple_epilogue(acc, pp, h2, b):
    return h2 + _ple(acc, pp, b), acc


def _rms_bwd_epilogue(du, h, dres, g):
    dh, dg = jax.vjp(_rms, h, g)[1](du)
    return dh + dres, dg


def layer_bwd(dh3, p_l, w, s, batch, tag, carried=None, on_early_grads=None):
    def ple_bwd(pg, pp, dh, b):
        return jax.vjp(_ple, pg, pp, b)[1](dh)

    norm_bwd = dict(epilogue=_rms_bwd_epilogue, partials=1, tm=512, tn=D_MODEL, tb=True)

    d_pg, d_pp, g_bpg = rowwise(f"{tag}_ple_bwd", ple_bwd, [s["pg"], s["pp"], dh3], [w["b_pg"]],
                                [(D_MODEL, BF16), (D_MODEL, BF16)], [D_MODEL])
    g = dict(b_pg=g_bpg)
    g["wpp"] = mm(p_l, d_pp, ta=True, name=f"{tag}_mm_dwpp")
    g["wpg"] = mm(s["u3"], d_pg, ta=True, name=f"{tag}_mm_dwpg")
    dh2, dg3 = mm(d_pg, w["wpg"], extras=[s["h2"], dh3], col_params=[w["g3"]], name=f"{tag}_mm_du3", **norm_bwd)
    g["g3"] = sum_slices(f"{tag}_sum_dg3", dg3)

    d_gate, d_up = mm(dh2, w["wd"], tb=True, extras=[s["gate"], s["up"]], epilogue=_swiglu_bwd_epilogue,
                      out_dtypes=(BF16, BF16), tm=512, tn=D_FF // 2, name=f"{tag}_mm_dact")
    g["wd"] = mm(s["act"], dh2, ta=True, name=f"{tag}_mm_dwd")
    g["wg"] = mm(s["u2"], d_gate, ta=True, name=f"{tag}_mm_dwg")
    g["wu"] = mm(s["u2"], d_up, ta=True, name=f"{tag}_mm_dwu")
    dh1, dg2 = mm([d_gate, d_up], [w["wg"], w["wu"]], extras=[s["h1"], dh2], col_params=[w["g2"]],
                  name=f"{tag}_mm_du2", **norm_bwd)
    g["g2"] = sum_slices(f"{tag}_sum_dg2", dg2)

    dycat = mm(dh1, w["wout"], tb=True, name=f"{tag}_mm_dycat")
    g["wout"] = mm(s["ycat"], dh1, ta=True, name=f"{tag}_mm_dwout")
    if on_early_grads is not None:
        on_early_grads(g)
    dproj, gm = mixer_bwd(dycat, s["proj"], w, s["mixer"], batch, tag, carried)
    g.update(gm)
    g["win"] = mm(s["u1"], dproj, ta=True, name=f"{tag}_mm_dwin")
    dh0, dg1 = mm(dproj, w["win"], extras=[s["h0"], dh1], col_params=[w["g1"]], name=f"{tag}_mm_du1", **norm_bwd)
    g["g1"] = sum_slices(f"{tag}_sum_dg1", dg1)
    return dh0, g


def _loss_fwd_bwd(h, tgt, gf):
    def f(h_, gf_):
        e = _rms(h_, gf_) - tgt
        return 0.5 * jnp.sum(jnp.mean(e * e, axis=-1, keepdims=True), axis=0, keepdims=True)

    loss, vj = jax.vjp(f, h, gf)
    dh, dgf = vj(jnp.ones((1, 1), F32))
    return dh, jnp.broadcast_to(loss, (1, LANE)), dgf


def local_step(x, p, tgt, layers, final_g, carried=None, on_layer_grads=None, on_early_grads=None):
    batch, S, _ = x.shape
    T = batch * S
    h = x.reshape(T, D_MODEL)
    saved, weights = [], []
    for l, w in enumerate(layers):
        w = w() if callable(w) else w
        weights.append(w)
        h, s = layer_fwd(h, p[l].reshape(T, PLE_DIM), w, batch, f"l{l}", carried)
        saved.append(s)
    dh, loss, dgf = rowwise("loss", _loss_fwd_bwd, [h, tgt.reshape(T, D_MODEL)], [_row(final_g)],
                            [(D_MODEL, F32)], [LANE, D_MODEL], tr=256)
    grads = [None] * len(layers)
    for l in reversed(range(len(layers))):
        early = functools.partial(on_early_grads, l) if on_early_grads is not None else None
        dh, grads[l] = layer_bwd(dh, p[l].reshape(T, PLE_DIM), weights[l], saved[l], batch, f"l{l}", carried, early)
        if on_layer_grads is not None:
            on_layer_grads(l, grads[l])
    return loss[0, 0], dh.reshape(batch, S, D_MODEL), grads, dgf[0]


MESH = pl.DeviceIdType.MESH
N_DEV = 8
N_CHIP = 4
ANY = pl.BlockSpec(memory_space=pl.ANY)


def _pos():
    return lax.axis_index("x"), lax.axis_index("y"), lax.axis_index("c")


def _comm_call(body, name, out_shape, n_in, scratch):
    return pl.pallas_call(body, name=name, out_shape=out_shape, in_specs=[ANY] * n_in, out_specs=ANY,
                          scratch_shapes=scratch)


def all_gather8(name, blk):
    def body(x_ref, out_ref, send_sems, recv_sems, local_sem):
        x, y, c = _pos()
        me, sibling = (x, y, c), (x, y, 1 - c)
        chips = [(1 - x, y), (x, 1 - y), (1 - x, 1 - y)]

        def rows(px, py, pcore):
            return out_ref.at[4 * px + 2 * py + pcore]

        def copy(k, block, to, src=None):
            return pltpu.make_async_remote_copy(
                src_ref=rows(*block) if src is None else src, dst_ref=rows(*block),
                send_sem=send_sems.at[k], recv_sem=recv_sems.at[k], device_id=to, device_id_type=MESH)

        mine = pltpu.make_async_copy(x_ref, rows(*me), local_sem)
        mine.start()
        first = [copy(0, me, sibling, src=x_ref)]
        first += [copy(1 + j, me, (*chip, c), src=x_ref) for j, chip in enumerate(chips)]
        for cp in first:
            cp.start()
        passed = [copy(4 + j, (*chip, c), sibling) for j, chip in enumerate(chips)]
        for j, chip in enumerate(chips):
            copy(1 + j, (*chip, c), me).wait_recv()
            passed[j].start()
        copy(0, sibling, me).wait_recv()
        for j, chip in enumerate(chips):
            copy(4 + j, (*chip, 1 - c), me).wait_recv()
        for cp in first + passed:
            cp.wait_send()
        mine.wait()

    return _comm_call(body, name, SDS((N_DEV,) + blk.shape, blk.dtype), 1,
                      [pltpu.SemaphoreType.DMA((7,)), pltpu.SemaphoreType.DMA((7,)), pltpu.SemaphoreType.DMA])(blk)


def _comm_call_list(body, name, out_shapes, n_in, scratch):
    return pl.pallas_call(body, name=name, out_shape=out_shapes, in_specs=[ANY] * n_in, out_specs=[ANY] * len(out_shapes),
                          scratch_shapes=scratch)


class Exchange:
    def __init__(self, inputs, out_shapes, sems, start, wait, aliases=None):
        self.inputs, self.out_shapes, self.sems = list(inputs), list(out_shapes), list(sems)
        self.start, self.wait, self.aliases = start, wait, dict(aliases or {})


def combine(a, b):
    ai, ao, as_ = len(a.inputs), len(a.out_shapes), len(a.sems)

    def split(cins, couts, sems):
        return (cins[:ai], couts[:ao], sems[:as_]), (cins[ai:], couts[ao:], sems[as_:])

    def start(cins, couts, sems):
        pa, pb = split(cins, couts, sems)
        a.start(*pa)
        b.start(*pb)

    def wait(cins, couts, sems):
        pa, pb = split(cins, couts, sems)
        a.wait(*pa)
        b.wait(*pb)

    aliases = dict(a.aliases)
    aliases.update({ai + i: ao + o for i, o in b.aliases.items()})
    return Exchange(a.inputs + b.inputs, a.out_shapes + b.out_shapes, a.sems + b.sems, start, wait, aliases)


def run_exchange(name, ex):
    n_ci, n_co = len(ex.inputs), len(ex.out_shapes)

    def body(*refs):
        cins, couts, csems = refs[:n_ci], refs[n_ci:n_ci + n_co], refs[n_ci + n_co:]
        ex.start(cins, couts, csems)
        ex.wait(cins, couts, csems)

    return pl.pallas_call(body, name=name, out_shape=ex.out_shapes, in_specs=[ANY] * n_ci, out_specs=[ANY] * n_co,
                          scratch_shapes=ex.sems, input_output_aliases=ex.aliases)(*ex.inputs)


def _peers():
    x, y, c = _pos()
    return x, y, c, 2 * x + y, [(1 - x, y), (x, 1 - y), (1 - x, 1 - y)]


def _remote(src, dst, send_sem, recv_sem, to):
    return pltpu.make_async_remote_copy(src_ref=src, dst_ref=dst, send_sem=send_sem, recv_sem=recv_sem,
                                        device_id=to, device_id_type=MESH)


def gather_spread(shards, layer):
    n_t = len(shards)
    halves = [s.shape[1] // 2 for s in shards]

    def copies(cins, couts, sems):
        send_sems, recv_sems, local_sems = sems
        x, y, c, my_chip, chips = _peers()
        local, sends, recvs = [], [], []
        for t in range(n_t):
            h = halves[t]
            src = cins[t].at[layer, pl.ds(c * h, h)]
            mine = couts[t].at[my_chip, pl.ds(c * h, h)]
            local.append(pltpu.make_async_copy(src, mine, local_sems.at[t]))
            sends.append(_remote(src, mine, send_sems.at[0, t], recv_sems.at[0, t], (x, y, 1 - c)))
            recvs.append(_remote(src, couts[t].at[my_chip, pl.ds((1 - c) * h, h)], send_sems.at[0, t],
                                 recv_sems.at[0, t], (x, y, 1 - c)))
            for j, (px, py) in enumerate(chips):
                sends.append(_remote(src, mine, send_sems.at[1 + j, t], recv_sems.at[1 + j, t], (px, py, c)))
                recvs.append(_remote(src, couts[t].at[2 * px + py, pl.ds(c * h, h)], send_sems.at[1 + j, t],
                                     recv_sems.at[1 + j, t], (px, py, c)))
        return local, sends, recvs

    def start(cins, couts, sems):
        local, sends, _ = copies(cins, couts, sems)
        for cp in local + sends:
            cp.start()

    def wait(cins, couts, sems):
        local, sends, recvs = copies(cins, couts, sems)
        for cp in recvs:
            cp.wait_recv()
        for cp in sends:
            cp.wait_send()
        for cp in local:
            cp.wait()

    return Exchange(shards, [SDS((N_CHIP,) + s.shape[1:], s.dtype) for s in shards],
                    [pltpu.SemaphoreType.DMA((4, n_t)), pltpu.SemaphoreType.DMA((4, n_t)),
                     pltpu.SemaphoreType.DMA((n_t,))], start, wait)


def gather_pass_on(slots):
    n_t = len(slots)
    halves = [s.shape[1] // 2 for s in slots]

    def copies(cins, couts, sems):
        send_sems, recv_sems = sems
        x, y, c, my_chip, chips = _peers()
        sends, recvs = [], []
        for t in range(n_t):
            h = halves[t]
            for j, (px, py) in enumerate(chips):
                k = 2 * px + py
                sends.append(_remote(cins[t].at[k, pl.ds(c * h, h)], couts[t].at[k, pl.ds(c * h, h)],
                                     send_sems.at[j, t], recv_sems.at[j, t], (x, y, 1 - c)))
                recvs.append(_remote(cins[t].at[k, pl.ds(c * h, h)], couts[t].at[k, pl.ds((1 - c) * h, h)],
                                     send_sems.at[j, t], recv_sems.at[j, t], (x, y, 1 - c)))
        return sends, recvs

    def start(cins, couts, sems):
        for cp in copies(cins, couts, sems)[0]:
            cp.start()

    def wait(cins, couts, sems):
        sends, recvs = copies(cins, couts, sems)
        for cp in recvs:
            cp.wait_recv()
        for cp in sends:
            cp.wait_send()

    return Exchange(slots, [SDS(s.shape, s.dtype) for s in slots],
                    [pltpu.SemaphoreType.DMA((3, n_t)), pltpu.SemaphoreType.DMA((3, n_t))], start, wait,
                    aliases={t: t for t in range(n_t)})


def chips_exchange(vs):
    n_t = len(vs)

    def copies(cins, couts, sems):
        send_sems, recv_sems, local_sems = sems
        x, y, c, my_chip, chips = _peers()
        local = [pltpu.make_async_copy(cins[t].at[my_chip], couts[t].at[my_chip], local_sems.at[t]) for t in range(n_t)]
        sends, recvs = [], []
        for k, (px, py) in enumerate(chips):
            for t in range(n_t):
                sends.append(_remote(cins[t].at[2 * px + py], couts[t].at[my_chip], send_sems.at[k, t],
                                     recv_sems.at[k, t], (px, py, c)))
                recvs.append(_remote(cins[t].at[my_chip], couts[t].at[2 * px + py], send_sems.at[k, t],
                                     recv_sems.at[k, t], (px, py, c)))
        return local, sends, recvs

    def start(cins, couts, sems):
        local, sends, _ = copies(cins, couts, sems)
        for cp in local + sends:
            cp.start()

    def wait(cins, couts, sems):
        local, sends, recvs = copies(cins, couts, sems)
        for cp in recvs:
            cp.wait_recv()
        for cp in sends:
            cp.wait_send()
        for cp in local:
            cp.wait()

    return Exchange(vs, [SDS(v.shape, v.dtype) for v in vs],
                    [pltpu.SemaphoreType.DMA((3, n_t)), pltpu.SemaphoreType.DMA((3, n_t)),
                     pltpu.SemaphoreType.DMA((n_t,))], start, wait)


def swap_with_sibling(name, vs):
    n_t = len(vs)

    def body(*refs):
        v_refs, out_refs, send_sems, recv_sems = refs[:n_t], refs[n_t:2 * n_t], refs[-2], refs[-1]
        x, y, c = _pos()
        cps = [pltpu.make_async_remote_copy(src_ref=v_refs[t], dst_ref=out_refs[t], send_sem=send_sems.at[t],
                                            recv_sem=recv_sems.at[t], device_id=(x, y, 1 - c), device_id_type=MESH)
               for t in range(n_t)]
        for cp in cps:
            cp.start()
        for cp in cps:
            cp.wait()

    return _comm_call_list(body, name, [SDS(v.shape, v.dtype) for v in vs], n_t,
                           [pltpu.SemaphoreType.DMA((n_t,)), pltpu.SemaphoreType.DMA((n_t,))])(*vs)


_ROW_BLOCKS = (1024, 704, 512, 352, 256, 128, 64, 32, 16, 8)


def sum_slices(name, v, tr=512):
    n, R, C = v.shape
    tr = _pick(R, _ROW_BLOCKS)

    def body(v_ref, o_ref):
        acc = v_ref[0].astype(F32)
        for k in range(1, n):
            acc = acc + v_ref[k].astype(F32)
        o_ref[...] = acc

    return _pc(body, name=name, grid=(R // tr,), in_specs=[pl.BlockSpec((n, tr, C), lambda i: (0, i, 0))],
               out_specs=pl.BlockSpec((tr, C), lambda i: (i, 0)), out_shape=SDS((R, C), F32), sem=("parallel",))(v)


def add_slices(name, a, b, out_dtype):
    n, R, C = a.shape
    tr = _pick(R, _ROW_BLOCKS)

    def body(a_ref, b_ref, o_ref):
        o_ref[...] = (a_ref[...].astype(F32) + b_ref[...].astype(F32)).astype(o_ref.dtype)

    spec = pl.BlockSpec((1, tr, C), lambda k, i: (k, i, 0))
    return _pc(body, name=name, grid=(n, R // tr), in_specs=[spec, spec], out_specs=spec,
               out_shape=SDS(a.shape, out_dtype), sem=("parallel", "parallel"))(a, b)


def adamw(name, w, g, m, v):
    L, R, C = w.shape
    tr = _pick(R, (512, 352, 256, 128, 64, 32, 16, 8))
    c1 = 1.0 / (1.0 - ADAM_B1 ** ADAM_STEP)
    c2 = 1.0 / (1.0 - ADAM_B2 ** ADAM_STEP)

    def body(w_ref, g_ref, m_ref, v_ref, d_ref, nm_ref, nv_ref):
        gv = g_ref[...]
        nm = ADAM_B1 * m_ref[...] + (1.0 - ADAM_B1) * gv
        nv = ADAM_B2 * v_ref[...] + (1.0 - ADAM_B2) * (gv * gv)
        d_ref[...] = -ADAM_LR * ((nm * c1) / (jnp.sqrt(nv * c2) + ADAM_EPS) + ADAM_WD * w_ref[...])
        nm_ref[...] = nm
        nv_ref[...] = nv

    spec = pl.BlockSpec((1, tr, C), lambda l, i: (l, i, 0))
    return _pc(body, name=name, grid=(L, R // tr), in_specs=[spec] * 4, out_specs=[spec] * 3,
               out_shape=[SDS(w.shape, F32)] * 3, sem=("parallel", "parallel"))(w, g, m, v)


WEIGHTS = ["norm1_g", "w_in", "ssd_conv_w", "ssd_conv_b", "ssd_dt_bias", "ssd_a_log", "ssd_d", "ssd_norm_g",
           "lru_conv_w", "lru_conv_b", "lru_w_a", "lru_b_a", "lru_w_x", "lru_b_x", "lru_lambda", "lru_norm_g",
           "fox_b_f", "fox_norm_g", "w_out", "norm2_g", "w_gate", "w_up", "w_down", "norm3_g", "w_ple_gate",
           "b_ple_gate", "w_ple_proj", "final_norm_g"]
BIG = {"w_in": 2, "w_out": 1, "w_gate": 2, "w_up": 2, "w_down": 1, "w_ple_gate": 1, "w_ple_proj": 2}
SHARDED_SMALL = {"ssd_conv_w": 2, "lru_conv_w": 2}
SMALL = [n for n in WEIGHTS if n not in BIG]


def _pack(arrs, rows_multiple):
    flat = jnp.concatenate([a.reshape(-1) for a in arrs])
    per = rows_multiple * LANE
    n = -(-flat.shape[0] // per) * per
    return jnp.pad(flat, (0, n - flat.shape[0])).reshape(n // LANE, LANE)


def _unpack(flat2d, shapes):
    flat = flat2d.reshape(-1)
    out, off = [], 0
    for s in shapes:
        n = int(np.prod(s))
        out.append(flat[off:off + n].reshape(s))
        off += n
    return out


def _gather_shards(name, shards, axes, dtype):
    c = lax.axis_index("c")
    packed = _pack([s.astype(dtype) for s in shards], 32)
    half = packed.shape[0] // 2
    mine = lax.dynamic_slice_in_dim(packed, c * half, half, 0)
    got = all_gather8(name, mine).reshape(N_CHIP, 2 * half, LANE)
    per_chip = [_unpack(got[k], [s.shape for s in shards]) for k in range(N_CHIP)]
    return [jnp.concatenate([per_chip[k][i] for k in range(N_CHIP)], axis=ax) for i, ax in enumerate(axes)]


def kernel(x, p, norm1_g, w_in, ssd_conv_w, ssd_conv_b, ssd_dt_bias, ssd_a_log, ssd_d, ssd_norm_g, lru_conv_w, lru_conv_b, lru_w_a, lru_b_a, lru_w_x, lru_b_x, lru_lambda, lru_norm_g, fox_b_f, fox_norm_g, w_out, norm2_g, w_gate, w_up, w_down, norm3_g, w_ple_gate, b_ple_gate, w_ple_proj, final_norm_g, loss_target, m_norm1_g, m_w_in, m_ssd_conv_w, m_ssd_conv_b, m_ssd_dt_bias, m_ssd_a_log, m_ssd_d, m_ssd_norm_g, m_lru_conv_w, m_lru_conv_b, m_lru_w_a, m_lru_b_a, m_lru_w_x, m_lru_b_x, m_lru_lambda, m_lru_norm_g, m_fox_b_f, m_fox_norm_g, m_w_out, m_norm2_g, m_w_gate, m_w_up, m_w_down, m_norm3_g, m_w_ple_gate, m_b_ple_gate, m_w_ple_proj, m_final_norm_g, v_norm1_g, v_w_in, v_ssd_conv_w, v_ssd_conv_b, v_ssd_dt_bias, v_ssd_a_log, v_ssd_d, v_ssd_norm_g, v_lru_conv_w, v_lru_conv_b, v_lru_w_a, v_lru_b_a, v_lru_w_x, v_lru_b_x, v_lru_lambda, v_lru_norm_g, v_fox_b_f, v_fox_norm_g, v_w_out, v_norm2_g, v_w_gate, v_w_up, v_w_down, v_norm3_g, v_w_ple_gate, v_b_ple_gate, v_w_ple_proj, v_final_norm_g):
    args = (norm1_g, w_in, ssd_conv_w, ssd_conv_b, ssd_dt_bias, ssd_a_log, ssd_d, ssd_norm_g, lru_conv_w, lru_conv_b, lru_w_a, lru_b_a, lru_w_x, lru_b_x, lru_lambda, lru_norm_g, fox_b_f, fox_norm_g, w_out, norm2_g, w_gate, w_up, w_down, norm3_g, w_ple_gate, b_ple_gate, w_ple_proj, final_norm_g)
    m_args = (m_norm1_g, m_w_in, m_ssd_conv_w, m_ssd_conv_b, m_ssd_dt_bias, m_ssd_a_log, m_ssd_d, m_ssd_norm_g, m_lru_conv_w, m_lru_conv_b, m_lru_w_a, m_lru_b_a, m_lru_w_x, m_lru_b_x, m_lru_lambda, m_lru_norm_g, m_fox_b_f, m_fox_norm_g, m_w_out, m_norm2_g, m_w_gate, m_w_up, m_w_down, m_norm3_g, m_w_ple_gate, m_b_ple_gate, m_w_ple_proj, m_final_norm_g)
    v_args = (v_norm1_g, v_w_in, v_ssd_conv_w, v_ssd_conv_b, v_ssd_dt_bias, v_ssd_a_log, v_ssd_d, v_ssd_norm_g, v_lru_conv_w, v_lru_conv_b, v_lru_w_a, v_lru_b_a, v_lru_w_x, v_lru_b_x, v_lru_lambda, v_lru_norm_g, v_fox_b_f, v_fox_norm_g, v_w_out, v_norm2_g, v_w_gate, v_w_up, v_w_down, v_norm3_g, v_w_ple_gate, v_b_ple_gate, v_w_ple_proj, v_final_norm_g)
    w = dict(zip(WEIGHTS, args))
    mom = dict(zip(WEIGHTS, m_args))
    var = dict(zip(WEIGHTS, v_args))
    xi, yi, ci = _pos()
    chip = 2 * xi + yi

    big_names = list(BIG)
    later = [n for n in big_names if n != "w_in"]
    wb = {n: w[n].astype(BF16) for n in big_names}
    conv_full = dict(zip(SHARDED_SMALL, _gather_shards("gather_conv", [w[n] for n in SHARDED_SMALL],
                                                       list(SHARDED_SMALL.values()), F32)))
    carried = Carried()

    def layer_weights(l, slots_of):
        def assemble(n):
            s4 = slots_of(n)
            return (s4.reshape(-1, s4.shape[-1]) if BIG[n] == 1
                    else jnp.concatenate([s4[k] for k in range(N_CHIP)], axis=1))

        f = LazyDict({n: (conv_full[n][l] if n in conv_full else w[n][l]) for n in SMALL if n != "final_norm_g"})
        f.update({n: functools.partial(assemble, n) for n in big_names})
        return prep_layer(f)

    w_in0 = run_exchange("gather0_in_pass_on", gather_pass_on(
        run_exchange("gather0_in_spread", gather_spread([wb["w_in"]], 0))))
    n_early = 2
    carried.offer("l0_mm_in", lambda: gather_spread([wb[n] for n in later[:n_early]], 0))
    carried.offer("l0_ssd_fwd", lambda: gather_spread([wb[n] for n in later[n_early:]], 0))
    carried.offer("l0_fox_fwd", lambda: combine(
        gather_pass_on(carried.results["l0_mm_in"] + carried.results["l0_ssd_fwd"]),
        gather_spread([wb[n] for n in big_names], 1)))
    carried.offer("l0_mm_gu", lambda: gather_pass_on(carried.results["l0_fox_fwd"][len(later):]))
    layers = [
        layer_weights(0, lambda n: w_in0[0] if n == "w_in" else carried.results["l0_fox_fwd"][later.index(n)]),
        lambda: layer_weights(1, lambda n: carried.results["l0_mm_gu"][big_names.index(n)])]

    def chip_slices(a, n):
        return a.reshape(N_CHIP, -1, a.shape[1]) if BIG[n] == 1 else jnp.stack(jnp.split(a, N_CHIP, axis=1))

    def reduce_start(tag, names, full_grads):
        keep, give = [], []
        for n, a in zip(names, full_grads):
            s4 = chip_slices(a, n)
            h = s4.shape[1] // 2
            keep.append(lax.dynamic_slice_in_dim(s4, ci * h, h, 1))
            give.append(lax.dynamic_slice_in_dim(s4, (1 - ci) * h, h, 1).astype(BF16))
        got = swap_with_sibling(f"swap_halves{tag}", give)
        return [add_slices(f"add_sibling{tag}_{n}", k_, g_, BF16) for n, k_, g_ in zip(names, keep, got)]

    kernel_key = dict(w_out="wout", w_gate="wg", w_up="wu", w_down="wd", w_ple_gate="wpg", w_ple_proj="wpp")
    gl, parts = [None] * DEPTH, {}

    def on_early_grads(l, g_layer):
        if l == 0:
            parts["0_later"] = reduce_start("0_later", later, [g_layer[kernel_key[n]] for n in later])
            carried.offer("l0_ssd_bwd", lambda: chips_exchange(parts["0_later"]))

    def on_layer_grads(l, g_layer):
        gl[l] = unprep_grads(g_layer)
        if l == 1:
            parts["1"] = reduce_start("1", big_names, [gl[1][n] for n in big_names])
            carried.offer("l0_fox_bwd", lambda: chips_exchange(parts["1"]))
        else:
            parts["0_in"] = reduce_start("0_in", ["w_in"], [gl[0]["w_in"]])

    loss, grad_x, grads, g_final = local_step(x, p, loss_target, layers, final_norm_g, carried, on_layer_grads,
                                              on_early_grads)
    loss = lax.psum(loss, ("x", "y", "c"))
    arrived0 = dict(zip(later, carried.results["l0_ssd_bwd"]))
    arrived0["w_in"] = run_exchange("a2a_chips0_in", chips_exchange(parts["0_in"]))[0]
    arrived = [[arrived0[n] for n in big_names], carried.results["l0_fox_bwd"]]

    gsmall = {n: jnp.stack([gl[l][n] for l in range(DEPTH)]) for n in SMALL if n != "final_norm_g"}
    gsmall["final_norm_g"] = g_final
    small_shapes = [gsmall[n].shape for n in SMALL]
    gs = _pack([gsmall[n] for n in SMALL], 8)
    gs = sum_slices("sum_small", all_gather8("gather_small_grads", gs))
    gsum = dict(zip(SMALL, _unpack(gs, small_shapes)))
    for n, ax in SHARDED_SMALL.items():
        k = gsum[n].shape[ax] // N_CHIP
        gsum[n] = lax.dynamic_slice_in_dim(gsum[n], chip * k, k, ax)

    done = []
    for l in range(DEPTH):
        mine = [sum_slices(f"sum_chips{l}_{n}", a_) for n, a_ in zip(big_names, arrived[l])]
        other = swap_with_sibling(f"swap_results{l}", mine)
        done.append([jnp.concatenate([jnp.where(ci == 0, m_, o_), jnp.where(ci == 0, o_, m_)], axis=0)
                     for m_, o_ in zip(mine, other)])
    for t, n in enumerate(big_names):
        gsum[n] = jnp.stack([done[l][t] for l in range(DEPTH)])

    delta, new_m, new_v = {}, {}, {}
    for n in big_names:
        delta[n], new_m[n], new_v[n] = adamw(f"adamw_{n}", w[n], gsum[n], mom[n], var[n])
    shapes = [w[n].shape for n in SMALL]
    pk = lambda d: _pack([d[n] for n in SMALL], 8)[None]
    ds, ms, vs = adamw("adamw_small", pk(w), pk(gsum), pk(mom), pk(var))
    for d, packed in ((delta, ds), (new_m, ms), (new_v, vs)):
        d.update(zip(SMALL, _unpack(packed[0], shapes)))

    return (loss, grad_x, *[gsum[n] for n in WEIGHTS], *[delta[n] for n in WEIGHTS],
            *[new_m[n] for n in WEIGHTS], *[new_v[n] for n in WEIGHTS])
```

```python
import functools
import math

import jax
import jax.numpy as jnp
import numpy as np
from jax import lax
from jax.experimental import pallas as pl
from jax.experimental.pallas import tpu as pltpu

F32, BF16 = jnp.float32, jnp.bfloat16
SDS = jax.ShapeDtypeStruct

D_MODEL = 1024
DEPTH = 2
HEAD_DIM = 64
N_HEADS = 6
SSD_W, LRU_W, FOX_W = 384, 256, 384
D_FF = 2816
PLE_DIM = 256
IN_COLS = 2956
EPS = 1e-6
LRU_C = 8.0
LANE = 128
V7X_VMEM_LIMIT = 56 * 1024 * 1024

PW = 3072
OFF_B, OFF_C, OFF_LX, OFF_LG, OFF_SM, OFF_Z, OFF_XS, OFF_Q, OFF_K, OFF_V = (
    0, 256, 512, 768, 1024, 1152, 1536, 1920, 2304, 2688)
FOX_LANE0 = 8

ADAM_LR, ADAM_B1, ADAM_B2, ADAM_EPS, ADAM_WD, ADAM_STEP = 0.001, 0.9, 0.999, 1e-08, 0.01, 10


def _iota(shape, dim):
    return lax.broadcasted_iota(jnp.int32, shape, dim)


class Carried:
    def __init__(self):
        self.offers, self.results = {}, {}

    def offer(self, call_name, make_exchange):
        self.offers[call_name] = make_exchange

    def take(self, call_name):
        make = self.offers.pop(call_name, None)
        return None if make is None else make()

    def deliver(self, call_name, results):
        self.results[call_name] = results


class LazyDict(dict):
    def __getitem__(self, key):
        v = dict.__getitem__(self, key)
        if callable(v):
            v = v()
            dict.__setitem__(self, key, v)
        return v


def _run(call, args, name, comm, carried):
    if comm is None:
        return call(*args)
    own, brought = call(*args)
    carried.deliver(name, brought)
    return own


def _pc(body, *, name, grid, in_specs, out_specs, out_shape, scratch=(), sem=None, comm=None):
    if comm is None:
        return pl.pallas_call(
            body, name=name, grid=grid, in_specs=in_specs, out_specs=out_specs, out_shape=out_shape,
            scratch_shapes=list(scratch),
            compiler_params=pltpu.CompilerParams(dimension_semantics=sem, vmem_limit_bytes=V7X_VMEM_LIMIT))
    single = not isinstance(out_shape, (list, tuple))
    out_specs_l = [out_specs] if single else list(out_specs)
    out_shape_l = [out_shape] if single else list(out_shape)
    n_in, n_out, n_scr, n_ci, n_co = len(in_specs), len(out_shape_l), len(scratch), len(comm.inputs), len(comm.out_shapes)

    def hosted(*refs):
        ins, cins = refs[:n_in], refs[n_in:n_in + n_ci]
        outs, couts = refs[n_in + n_ci:n_in + n_ci + n_out], refs[n_in + n_ci + n_out:n_in + n_ci + n_out + n_co]
        rest = refs[n_in + n_ci + n_out + n_co:]
        scr, csems = rest[:n_scr], rest[n_scr:]
        ids = [pl.program_id(d) for d in range(len(grid))]
        first = functools.reduce(jnp.logical_and, [i == 0 for i in ids])
        last = functools.reduce(jnp.logical_and, [i == g - 1 for i, g in zip(ids, grid)])

        @pl.when(first)
        def _():
            comm.start(cins, couts, csems)

        body(*ins, *outs, *scr)

        @pl.when(last)
        def _():
            comm.wait(cins, couts, csems)

    call = pl.pallas_call(
        hosted, name=name, grid=grid, in_specs=list(in_specs) + [ANY] * n_ci,
        out_specs=out_specs_l + [ANY] * n_co, out_shape=out_shape_l + list(comm.out_shapes),
        scratch_shapes=list(scratch) + list(comm.sems),
        input_output_aliases={n_in + a: n_out + b for a, b in comm.aliases.items()},
        compiler_params=pltpu.CompilerParams(dimension_semantics=("arbitrary",) * len(grid),
                                             vmem_limit_bytes=V7X_VMEM_LIMIT))

    def run(*args):
        res = call(*args, *comm.inputs)
        own = res[:n_out]
        return (own[0] if single else own), list(res[n_out:])

    return run


def permute_in_cols(w):
    z = lambda n: jnp.zeros(w.shape[:-1] + (n,), w.dtype)
    s = lambda a, b: w[..., a:b]
    return jnp.concatenate([
        s(768, 1024), s(1024, 1280), s(1286, 1542), s(1542, 1798),
        s(1280, 1286), z(2), s(2950, 2956), z(LANE - 14),
        s(0, 384), s(384, 768), s(1798, 2182), s(2182, 2566), s(2566, 2950)], axis=-1)


def unpermute_in_cols(g):
    s = lambda a, n: g[..., a:a + n]
    return jnp.concatenate([
        s(OFF_Z, 384), s(OFF_XS, 384), s(OFF_B, 256), s(OFF_C, 256), s(OFF_SM, 6),
        s(OFF_LX, 256), s(OFF_LG, 256), s(OFF_Q, 384), s(OFF_K, 384), s(OFF_V, 384),
        s(OFF_SM + FOX_LANE0, 6)], axis=-1)


def _pick(n, cands):
    for c in cands:
        if n % c == 0:
            return c
    return n


def mm(a, b, *, name, ta=False, tb=False, out_dtypes=(F32,), extras=(), col_params=(), partials=0, epilogue=None,
       tm=None, tn=None, tk=None, carried=None):
    bs = list(b) if isinstance(b, (list, tuple)) else [b]
    pair_sum = isinstance(a, (list, tuple))
    a_list = list(a) if pair_sum else [a]
    assert not pair_sum or len(a_list) == len(bs)
    a = a_list[0]
    n_a = len(a_list)
    n_acc = 1 if pair_sum else len(bs)
    extras = [e if isinstance(e, tuple) else (e, 0) for e in extras]
    M = a.shape[1] if ta else a.shape[0]
    K = a.shape[0] if ta else a.shape[1]
    N = bs[0].shape[0] if tb else bs[0].shape[1]
    tm = tm or _pick(M, (1024, 1408, 512, 256, 128))
    tn = tn or _pick(N, (1024, 1408, 768, 512, 256, 128))
    tk = tk or _pick(K, (1024, 1408, 512, 256, 128))
    nm, nn, nk = M // tm, N // tn, K // tk
    n_b, n_ex, n_cp, n_out = len(bs), len(extras), len(col_params), len(out_dtypes)
    a_bytes, b_bytes = n_a * M * K * a.dtype.itemsize, n_b * K * N * bs[0].dtype.itemsize
    rows_inner = a_bytes * nn + b_bytes <= a_bytes + b_bytes * nm

    def ij(g0, g1):
        return (g1, g0) if rows_inner else (g0, g1)

    def body(*rest):
        a_refs, rest = rest[:n_a], rest[n_a:]
        b_refs, rest = rest[:n_b], rest[n_b:]
        in_refs, rest = rest[:n_ex + n_cp], rest[n_ex + n_cp:]
        out_refs, accs = rest[:n_out + partials], rest[n_out + partials:]
        dn = (((0 if ta else 1,), (1 if tb else 0,)), ((), ()))
        dot = lambda x_ref, y_ref: lax.dot_general(x_ref[...].astype(BF16), y_ref[...].astype(BF16), dn,
                                                   preferred_element_type=F32)
        if pair_sum:
            parts = [functools.reduce(lambda u, v: u + v, [dot(x, y) for x, y in zip(a_refs, b_refs)])]
        else:
            parts = [dot(a_refs[0], b_ref) for b_ref in b_refs]

        def finish(rs):
            outs = epilogue(*rs, *[e[...] for e in in_refs]) if epilogue is not None else tuple(rs)
            for o_ref, o in zip(out_refs[:n_out], outs):
                o_ref[...] = o.astype(o_ref.dtype)
            for o_ref, o in zip(out_refs[n_out:], outs[n_out:]):
                o_ref[0] = o

        if nk == 1:
            finish(parts)
            return
        k = pl.program_id(2)

        @pl.when(k == 0)
        def _():
            for acc, part in zip(accs, parts):
                acc[...] = part

        @pl.when(k > 0)
        def _():
            for acc, part in zip(accs, parts):
                acc[...] += part

        @pl.when(k == nk - 1)
        def _():
            finish([acc[...] for acc in accs])

    def a_map(g0, g1, k):
        i, _ = ij(g0, g1)
        return (k, i) if ta else (i, k)

    def b_map(g0, g1, k):
        _, j = ij(g0, g1)
        return (j, k) if tb else (k, j)

    def ex_map(off, g0, g1, k):
        i, j = ij(g0, g1)
        return (i, j + off)

    a_spec = pl.BlockSpec((tk, tm) if ta else (tm, tk), a_map)
    b_spec = pl.BlockSpec((tn, tk) if tb else (tk, tn), b_map)
    mn_spec = pl.BlockSpec((tm, tn), functools.partial(ex_map, 0))
    comm = carried.take(name) if carried is not None else None
    call = _pc(body, name=name, grid=(nn, nm, nk) if rows_inner else (nm, nn, nk),
               in_specs=([a_spec] * n_a + [b_spec] * n_b
                         + [pl.BlockSpec((tm, tn), functools.partial(ex_map, off)) for _, off in extras]
                         + [pl.BlockSpec((1, tn), lambda g0, g1, k: (0, ij(g0, g1)[1]))] * n_cp),
               out_specs=([mn_spec] * n_out
                          + [pl.BlockSpec((1, 1, tn), lambda g0, g1, k: (ij(g0, g1)[0], 0, ij(g0, g1)[1]))] * partials),
               out_shape=[SDS((M, N), dt) for dt in out_dtypes] + [SDS((nm, 1, N), F32)] * partials,
               scratch=[pltpu.VMEM((tm, tn), F32)] * n_acc if nk > 1 else [],
               sem=("parallel", "parallel", "arbitrary"), comm=comm)
    outs = _run(call, (*a_list, *bs, *[e for e, _ in extras], *col_params), name, comm, carried)
    return outs[0] if len(outs) == 1 else outs


def rowwise(name, fn, rows, params, row_outs, acc_outs=(), tr=512):
    rows = [r if isinstance(r, tuple) else (r, 0, r.shape[1]) for r in rows]
    T = rows[0][0].shape[0]
    tr = min(tr, T)
    n_in, n_ro, n_ac = len(rows) + len(params), len(row_outs), len(acc_outs)

    def body(*refs):
        ins, outs = refs[:n_in], refs[n_in:]
        res = fn(*[r[...] for r in ins])
        if not isinstance(res, (tuple, list)):
            res = (res,)
        for k in range(n_ro):
            outs[k][...] = res[k].astype(outs[k].dtype)
        if n_ac:
            i = pl.program_id(0)

            @pl.when(i == 0)
            def _():
                for k in range(n_ac):
                    outs[n_ro + k][...] = res[n_ro + k]

            @pl.when(i > 0)
            def _():
                for k in range(n_ac):
                    outs[n_ro + k][...] += res[n_ro + k]

    in_specs = ([pl.BlockSpec((tr, w), functools.partial(lambda cb, i: (i, cb), cb)) for (_, cb, w) in rows]
                + [pl.BlockSpec(p.shape, lambda i: (0, 0)) for p in params])
    out_specs = ([pl.BlockSpec((tr, c), lambda i: (i, 0)) for (c, _) in row_outs]
                 + [pl.BlockSpec((1, c), lambda i: (0, 0)) for c in acc_outs])
    out_shape = [SDS((T, c), dt) for (c, dt) in row_outs] + [SDS((1, c), F32) for c in acc_outs]
    outs = _pc(body, name=name, grid=(T // tr,), in_specs=in_specs, out_specs=out_specs, out_shape=out_shape,
               sem=("arbitrary",) if n_ac else ("parallel",))(*[r[0] for r in rows], *params)
    return outs[0] if len(outs) == 1 else outs


def _rms(x, g):
    return x * lax.rsqrt(jnp.mean(x * x, axis=-1, keepdims=True) + EPS) * g


def _softplus(x):
    return jnp.maximum(x, 0.0) + jnp.log(1.0 + jnp.exp(-jnp.abs(x)))


def _silu(x):
    return x * jax.nn.sigmoid(x)


def _gelu(x):
    return 0.5 * x * (1.0 + jnp.tanh(math.sqrt(2.0 / math.pi) * (x + 0.044715 * (x * x * x))))


def _neg_expm1(x):
    series = x * (1 + x / 2 * (1 + x / 3 * (1 + x / 4 * (1 + x / 5 * (1 + x / 6 * (1 + x / 7))))))
    return -jnp.where(jnp.abs(x) < 0.3, series, jnp.exp(x) - 1.0)


def _swiglu(gu):
    return _silu(gu[:, :D_FF]) * gu[:, D_FF:]


def _ple(pg, pp, b):
    return jax.nn.sigmoid(pg + b) * pp


def _ssd_elt(small, xs_act, dtbias_row, alog_row):
    lane = _iota(small.shape, 1)
    dt = _softplus(small + dtbias_row)
    adt = jnp.where(lane < N_HEADS, -jnp.exp(alog_row) * dt, 0.0)
    head = _iota(xs_act.shape, 1) // HEAD_DIM
    dt_exp = jnp.zeros_like(xs_act)
    for h in range(N_HEADS):
        dth = jnp.sum(jnp.where(lane == h, dt, 0.0), axis=1, keepdims=True)
        dt_exp = dt_exp + jnp.where(head == h, dth, 0.0)
    return adt, xs_act * dt_exp


def _fox_elt(small, bf_row):
    lane = _iota(small.shape, 1)
    keep = (lane >= FOX_LANE0) & (lane < FOX_LANE0 + N_HEADS)
    return jnp.where(keep, -_softplus(-(small + bf_row)), 0.0)


def _lru_elt(xl, pre, b_ax, lam):
    r = jax.nn.sigmoid(pre[:, :LRU_W] + b_ax[:, :LRU_W])
    i = jax.nn.sigmoid(pre[:, LRU_W:] + b_ax[:, LRU_W:])
    log_a = -LRU_C * r * _softplus(-lam)
    a = jnp.exp(log_a)
    mult = jnp.sqrt(_neg_expm1(2.0 * log_a))
    return a, mult * (i * xl)


def _mix_post(yraw, xs_act, z, hl, lgate, yfox, dexp, g_ssd, g_lru, g_fox):
    y_ssd = _rms((yraw + xs_act * dexp) * _silu(z), g_ssd)
    y_lru = _rms(hl * _gelu(lgate), g_lru)
    y_fox = _rms(yfox, g_fox)
    return jnp.concatenate([y_ssd, y_lru, y_fox], axis=-1)


def _colsum(x):
    return jnp.sum(x, axis=0, keepdims=True)


def _shift_down(x, d):
    if d == 0:
        return x
    return jnp.where(_iota(x.shape, 0) >= d, pltpu.roll(x, d, 0), 0.0)


def _shift_up(x, d):
    if d == 0:
        return x
    s = x.shape[0]
    return jnp.where(_iota(x.shape, 0) < s - d, pltpu.roll(x, s - d, 0), 0.0)


def _conv_core(x, w, b):
    y = b + w[3:4, :] * x
    for k in range(3):
        y = y + w[k:k + 1, :] * _shift_down(x, 3 - k)
    return y


def seq_conv(name, src, col, width, w8, b, *, batch, silu, out_dtype):
    T = src.shape[0]
    S = T // batch
    c0 = col // LANE

    def body(x_ref, w_ref, b_ref, o_ref):
        y = _conv_core(x_ref[...], w_ref[...], b_ref[...])
        o_ref[...] = (_silu(y) if silu else y).astype(o_ref.dtype)

    return _pc(body, name=name, grid=(batch, width // LANE),
               in_specs=[pl.BlockSpec((S, LANE), lambda bi, ci: (bi, c0 + ci)),
                         pl.BlockSpec((8, LANE), lambda bi, ci: (0, ci)),
                         pl.BlockSpec((1, LANE), lambda bi, ci: (0, ci))],
               out_specs=pl.BlockSpec((S, LANE), lambda bi, ci: (bi, ci)),
               out_shape=SDS((T, width), out_dtype), sem=("parallel", "parallel"))(src, w8, b)


def seq_conv_bwd(name, src, col, width, w8, b, dy, *, batch, silu):
    T = src.shape[0]
    S = T // batch
    c0 = col // LANE

    def body(x_ref, w_ref, b_ref, dy_ref, dx_ref, dw_ref, db_ref):
        x, w = x_ref[...], w_ref[...]
        dpre = dy_ref[...].astype(F32)
        if silu:
            dpre = jax.vjp(_silu, _conv_core(x, w, b_ref[...]))[1](dpre)[0]
        dx = w[3:4, :] * dpre
        for k in range(3):
            dx = dx + w[k:k + 1, :] * _shift_up(dpre, 3 - k)
        dx_ref[...] = dx.astype(dx_ref.dtype)
        row8 = _iota((8, LANE), 0)
        dw = jnp.zeros((8, LANE), F32)
        for k in range(4):
            dw = dw + jnp.where(row8 == k, _colsum(dpre * _shift_down(x, 3 - k)), 0.0)
        db = _colsum(dpre)
        bi = pl.program_id(1)

        @pl.when(bi == 0)
        def _():
            dw_ref[...] = dw
            db_ref[...] = db

        @pl.when(bi > 0)
        def _():
            dw_ref[...] += dw
            db_ref[...] += db

    return _pc(body, name=name, grid=(width // LANE, batch),
               in_specs=[pl.BlockSpec((S, LANE), lambda ci, bi: (bi, c0 + ci)),
                         pl.BlockSpec((8, LANE), lambda ci, bi: (0, ci)),
                         pl.BlockSpec((1, LANE), lambda ci, bi: (0, ci)),
                         pl.BlockSpec((S, LANE), lambda ci, bi: (bi, ci))],
               out_specs=[pl.BlockSpec((S, LANE), lambda ci, bi: (bi, ci)),
                          pl.BlockSpec((8, LANE), lambda ci, bi: (0, ci)),
                          pl.BlockSpec((1, LANE), lambda ci, bi: (0, ci))],
               out_shape=[SDS((T, width), BF16), SDS((8, width), F32), SDS((1, width), F32)],
               sem=("parallel", "arbitrary"))(src, w8, b, dy)


def _split3_dot(tri, x):
    hi = x.astype(BF16)
    r1 = x - hi.astype(F32)
    mid = r1.astype(BF16)
    lo = (r1 - mid.astype(F32)).astype(BF16)
    d = lambda v: jnp.dot(tri, v, preferred_element_type=F32)
    return d(hi) + d(mid) + d(lo)


def seq_cumsum(name, x, *, batch, reverse=False, nsum=1, trow=None):
    T = x.shape[0]
    S = T // batch
    ch = min(256, S)
    nch = S // ch

    def body(x_ref, o_ref, *maybe_t):
        r, c = _iota((ch, ch), 0), _iota((ch, ch), 1)
        tri = jnp.where((c >= r) if reverse else (c <= r), 1.0, 0.0).astype(BF16)
        carry = jnp.zeros((1, LANE), F32)
        for k in (range(nch - 1, -1, -1) if reverse else range(nch)):
            xc = x_ref[k * ch:(k + 1) * ch, 0:LANE]
            for m in range(1, nsum):
                xc = xc + x_ref[k * ch:(k + 1) * ch, m * LANE:(m + 1) * LANE]
            o_ref[k * ch:(k + 1) * ch, :] = _split3_dot(tri, xc) + carry
            carry = carry + _colsum(xc)
        if trow is not None:
            maybe_t[0][...] = o_ref[...].T[trow:trow + 8, :]

    out_specs = [pl.BlockSpec((S, LANE), lambda bi: (bi, 0))]
    out_shape = [SDS((T, LANE), F32)]
    if trow is not None:
        out_specs.append(pl.BlockSpec((8, S), lambda bi: (bi, 0)))
        out_shape.append(SDS((batch * 8, S), F32))
    outs = _pc(body, name=name, grid=(batch,), in_specs=[pl.BlockSpec((S, LANE * nsum), lambda bi: (bi, 0))],
               out_specs=out_specs, out_shape=out_shape, sem=("parallel",))(x)
    return outs if trow is not None else outs[0]


def lru_scan(name, a, u, *, batch):
    T, W = a.shape
    S = T // batch

    def body(a_ref, u_ref, h_ref):
        row = _iota((8, W), 0)

        def step(g, h):
            off = pl.multiple_of(g * 8, 8)
            at, ut = a_ref[pl.ds(off, 8), :], u_ref[pl.ds(off, 8), :]
            acc = jnp.zeros((8, W), F32)
            for r in range(8):
                h = at[r:r + 1, :] * h + ut[r:r + 1, :]
                acc = jnp.where(row == r, jnp.broadcast_to(h, (8, W)), acc)
            h_ref[pl.ds(off, 8), :] = acc
            return h

        lax.fori_loop(0, S // 8, step, jnp.zeros((1, W), F32))

    spec = pl.BlockSpec((S, W), lambda bi: (bi, 0))
    return _pc(body, name=name, grid=(batch,), in_specs=[spec, spec], out_specs=spec,
               out_shape=SDS((T, W), F32), sem=("parallel",))(a, u)


def lru_scan_bwd(name, a, h, dh, *, batch):
    T, W = a.shape
    S = T // batch
    ng = S // 8

    def body(a_ref, h_ref, dh_ref, da_ref, du_ref):
        row = _iota((8, W), 0)

        def step(k, c):
            g_idx = ng - 1 - k
            off = pl.multiple_of(g_idx * 8, 8)
            offp = pl.multiple_of(jnp.maximum(g_idx - 1, 0) * 8, 8)
            at, ht, dt = a_ref[pl.ds(off, 8), :], h_ref[pl.ds(off, 8), :], dh_ref[pl.ds(off, 8), :]
            hp = jnp.where(g_idx > 0, h_ref[pl.ds(offp, 8), :], 0.0)
            da = jnp.zeros((8, W), F32)
            du = jnp.zeros((8, W), F32)
            for r in range(7, -1, -1):
                g = dt[r:r + 1, :] + c
                hprev = ht[r - 1:r, :] if r > 0 else hp[7:8, :]
                du = jnp.where(row == r, jnp.broadcast_to(g, (8, W)), du)
                da = jnp.where(row == r, jnp.broadcast_to(g * hprev, (8, W)), da)
                c = at[r:r + 1, :] * g
            da_ref[pl.ds(off, 8), :] = da
            du_ref[pl.ds(off, 8), :] = du
            return c

        lax.fori_loop(0, ng, step, jnp.zeros((1, W), F32))

    spec = pl.BlockSpec((S, W), lambda bi: (bi, 0))
    return _pc(body, name=name, grid=(batch,), in_specs=[spec] * 3, out_specs=[spec] * 2,
               out_shape=[SDS((T, W), F32)] * 2, sem=("parallel",))(a, h, dh)


def _nt(a, b):
    return lax.dot_general(a, b, (((1,), (1,)), ((), ())), preferred_element_type=F32)


def _tn(a, b):
    return lax.dot_general(a, b, (((0,), (0,)), ((), ())), preferred_element_type=F32)


def _tile(S, t=256):
    return min(t, S)


def ssd_attn_fwd(name, cm, bm, xd, cum, cum_t, *, batch, carried=None):
    T = cm.shape[0]
    S = T // batch
    tq = tk = _tile(S)
    nq = S // tq

    def body(c_ref, b_ref, x_ref, cum_ref, cumt_ref, y_ref):
        i = pl.program_id(1)
        cq, cmq = cum_ref[...], c_ref[...]
        rowi, coli = _iota((tq, tk), 0), _iota((tq, tk), 1)
        half = _iota((tk, LANE), 1) // HEAD_DIM

        def step(j, accs, diag):
            off = pl.multiple_of(j * tk, tk)
            bj = b_ref[pl.ds(off, tk), :]
            gm = [_nt(cmq[:, g * LANE:(g + 1) * LANE], bj[:, g * LANE:(g + 1) * LANE]) for g in range(2)]
            ckt = cumt_ref[:, pl.ds(off, tk)]
            new = []
            for p in range(3):
                xp = x_ref[pl.ds(off, tk), p * LANE:(p + 1) * LANE]
                ws, xs = [], []
                for hh in range(2):
                    h = 2 * p + hh
                    seg = cq[:, h:h + 1] - ckt[h:h + 1, :]
                    e = jnp.exp(jnp.where(rowi >= coli, seg, -jnp.inf) if diag else seg)
                    ws.append((gm[h // 3] * e).astype(BF16))
                    xs.append(jnp.where(half == hh, xp, jnp.zeros_like(xp)))
                new.append(accs[p] + jnp.dot(jnp.concatenate(ws, axis=1), jnp.concatenate(xs, axis=0),
                                             preferred_element_type=F32))
            return tuple(new)

        accs = lax.fori_loop(0, i, functools.partial(step, diag=False),
                             tuple(jnp.zeros((tq, LANE), F32) for _ in range(3)))
        accs = step(i, accs, True)
        y_ref[...] = jnp.concatenate(accs, axis=1)

    comm = carried.take(name) if carried is not None else None
    call = _pc(body, name=name, grid=(batch, nq),
               in_specs=[pl.BlockSpec((tq, 256), lambda b, i: (b * nq + i, 0)),
                         pl.BlockSpec((S, 256), lambda b, i: (b, 0)),
                         pl.BlockSpec((S, SSD_W), lambda b, i: (b, 0)),
                         pl.BlockSpec((tq, LANE), lambda b, i: (b * nq + i, 0)),
                         pl.BlockSpec((8, S), lambda b, i: (b, 0))],
               out_specs=pl.BlockSpec((tq, SSD_W), lambda b, i: (b * nq + i, 0)),
               out_shape=SDS((T, SSD_W), F32), sem=("parallel", "parallel"), comm=comm)
    return _run(call, (cm, bm, xd, cum, cum_t), name, comm, carried)


def ssd_attn_bwd(name, cm, bm, xd, cum, cum_t, dy, *, batch, carried=None):
    T = cm.shape[0]
    S = T // batch
    tq = tk = _tile(S, 512)
    nq = S // tq

    def body(c_ref, b_ref, x_ref, cum_ref, cumt_ref, dy_ref, dx_ref, db_ref, dc_ref, dcum_ref, dcumt_ref):
        dx_ref[...] = jnp.zeros_like(dx_ref)
        db_ref[...] = jnp.zeros_like(db_ref)
        dcum_ref[...] = jnp.zeros_like(dcum_ref)
        dcumt_ref[...] = jnp.zeros_like(dcumt_ref)
        rowi, coli = _iota((tq, tk), 0), _iota((tq, tk), 1)
        halfq = _iota((tq, LANE), 1) // HEAD_DIM
        lane_q = _iota((tq, LANE), 1)

        def qblock(i, _):
            qoff = pl.multiple_of(i * tq, tq)
            cq = cum_ref[pl.ds(qoff, tq), :]
            cmq = c_ref[pl.ds(qoff, tq), :]
            dyq = dy_ref[pl.ds(qoff, tq), :]
            dyh = [[jnp.where(halfq == hh, dyq[:, p * LANE:(p + 1) * LANE], 0.0).astype(BF16) for hh in range(2)]
                   for p in range(3)]

            def step(j, carry, diag):
                dcq, rs_acc = carry
                off = pl.multiple_of(j * tk, tk)
                bj = b_ref[pl.ds(off, tk), :]
                gm = [_nt(cmq[:, g * LANE:(g + 1) * LANE], bj[:, g * LANE:(g + 1) * LANE]) for g in range(2)]
                ckt = cumt_ref[:, pl.ds(off, tk)]
                dgm = [jnp.zeros((tq, tk), F32), jnp.zeros((tq, tk), F32)]
                for p in range(3):
                    xp = x_ref[pl.ds(off, tk), p * LANE:(p + 1) * LANE]
                    ws = []
                    for hh in range(2):
                        h = 2 * p + hh
                        seg = cq[:, h:h + 1] - ckt[h:h + 1, :]
                        e = jnp.exp(jnp.where(rowi >= coli, seg, -jnp.inf) if diag else seg)
                        w = gm[h // 3] * e
                        dw = _nt(dyh[p][hh], xp)
                        zz = dw * w
                        rs_acc = rs_acc + jnp.where(lane_q == h, jnp.sum(zz, axis=1, keepdims=True), 0.0)
                        dcumt_ref[h:h + 1, pl.ds(off, tk)] += _colsum(zz)
                        dgm[h // 3] = dgm[h // 3] + dw * e
                        ws.append(w.astype(BF16))
                    dx_ref[pl.ds(off, tk), p * LANE:(p + 1) * LANE] += _tn(
                        jnp.concatenate(ws, axis=0), jnp.concatenate(dyh[p], axis=0))
                new_dcq = []
                for g in range(2):
                    dg = dgm[g].astype(BF16)
                    new_dcq.append(dcq[g] + jnp.dot(dg, bj[:, g * LANE:(g + 1) * LANE], preferred_element_type=F32))
                    db_ref[pl.ds(off, tk), g * LANE:(g + 1) * LANE] += _tn(dg, cmq[:, g * LANE:(g + 1) * LANE])
                return tuple(new_dcq), rs_acc

            carry = lax.fori_loop(
                0, i, functools.partial(step, diag=False),
                ((jnp.zeros((tq, LANE), F32), jnp.zeros((tq, LANE), F32)), jnp.zeros((tq, LANE), F32)))
            dcq, rs_acc = step(i, carry, True)
            dc_ref[pl.ds(qoff, tq), :] = jnp.concatenate(dcq, axis=1)
            dcum_ref[pl.ds(qoff, tq), :] += rs_acc
            return 0

        lax.fori_loop(0, nq, qblock, 0)
        dcum_ref[...] = dcum_ref[...] - dcumt_ref[...].T

    s256 = pl.BlockSpec((S, 256), lambda b: (b, 0))
    s384 = pl.BlockSpec((S, SSD_W), lambda b: (b, 0))
    s128 = pl.BlockSpec((S, LANE), lambda b: (b, 0))
    comm = carried.take(name) if carried is not None else None
    call = _pc(body, name=name, grid=(batch,),
               in_specs=[s256, s256, s384, s128, pl.BlockSpec((8, S), lambda b: (b, 0)), s384],
               out_specs=[s384, s256, s256, s128],
               out_shape=[SDS((T, SSD_W), F32), SDS((T, 256), F32), SDS((T, 256), F32), SDS((T, LANE), F32)],
               scratch=[pltpu.VMEM((LANE, S), F32)], sem=("parallel",), comm=comm)
    return _run(call, (cm, bm, xd, cum, cum_t, dy), name, comm, carried)


NEG_BIG = -1e30


def fox_attn_fwd(name, proj, cum, cum_t, *, batch, carried=None):
    T = proj.shape[0]
    S = T // batch
    tq = tk = _tile(S, 512)
    nq = S // tq
    scale = HEAD_DIM ** -0.5
    qb, kb, vb = OFF_Q // LANE, OFF_K // LANE, OFF_V // LANE

    def body(q_ref, k_ref, v_ref, cum_ref, cumt_ref, o_ref, lse_ref):
        p, i = pl.program_id(1), pl.program_id(2)
        cq = cum_ref[...]
        lane_q = _iota((tq, LANE), 1)
        halfq, halfk = lane_q // HEAD_DIM, _iota((tk, LANE), 1) // HEAD_DIM
        qs = q_ref[...] * scale
        qh = [jnp.where(halfq == hh, qs, 0.0).astype(BF16) for hh in range(2)]
        rowi, coli = _iota((tq, tk), 0), _iota((tq, tk), 1)
        cqh = [jnp.sum(jnp.where(lane_q == FOX_LANE0 + 2 * p + hh, cq, 0.0), axis=1, keepdims=True) for hh in range(2)]
        row8 = _iota((8, tk), 0)

        def step(j, carry, diag):
            ms, ls, acc = carry
            off = pl.multiple_of(j * tk, tk)
            kj = k_ref[pl.ds(off, tk), :].astype(BF16)
            vj = v_ref[pl.ds(off, tk), :].astype(BF16)
            ckt = cumt_ref[:, pl.ds(off, tk)]
            ps, vs, new_m, new_l, alphas = [], [], [], [], []
            for hh in range(2):
                ck = jnp.sum(jnp.where(row8 == 2 * p + hh, ckt, 0.0), axis=0, keepdims=True)
                logits = _nt(qh[hh], kj) + (cqh[hh] - ck)
                if diag:
                    logits = jnp.where(rowi >= coli, logits, -jnp.inf)
                m = jnp.maximum(ms[hh], jnp.max(logits, axis=1, keepdims=True))
                alpha = jnp.exp(ms[hh] - m)
                pr = jnp.exp(logits - m)
                new_m.append(m)
                new_l.append(alpha * ls[hh] + jnp.sum(pr, axis=1, keepdims=True))
                alphas.append(alpha)
                ps.append(pr.astype(BF16))
                vs.append(jnp.where(halfk == hh, vj, jnp.zeros_like(vj)))
            acc = acc * jnp.where(halfq == 0, alphas[0], alphas[1]) + jnp.dot(
                jnp.concatenate(ps, axis=1), jnp.concatenate(vs, axis=0), preferred_element_type=F32)
            return tuple(new_m), tuple(new_l), acc

        init = ((jnp.full((tq, 1), NEG_BIG, F32),) * 2, (jnp.zeros((tq, 1), F32),) * 2, jnp.zeros((tq, LANE), F32))
        ms, ls, acc = step(i, lax.fori_loop(0, i, functools.partial(step, diag=False), init), True)
        o_ref[...] = acc / jnp.where(halfq == 0, ls[0], ls[1])
        lse_ref[...] = (jnp.where(lane_q == 0, ms[0] + jnp.log(ls[0]), 0.0)
                        + jnp.where(lane_q == 1, ms[1] + jnp.log(ls[1]), 0.0))

    comm = carried.take(name) if carried is not None else None
    call = _pc(body, name=name, grid=(batch, 3, nq),
               in_specs=[pl.BlockSpec((tq, LANE), lambda b, p, i: (b * nq + i, qb + p)),
                         pl.BlockSpec((S, LANE), lambda b, p, i: (b, kb + p)),
                         pl.BlockSpec((S, LANE), lambda b, p, i: (b, vb + p)),
                         pl.BlockSpec((tq, LANE), lambda b, p, i: (b * nq + i, 0)),
                         pl.BlockSpec((8, S), lambda b, p, i: (b, 0))],
               out_specs=[pl.BlockSpec((tq, LANE), lambda b, p, i: (b * nq + i, p))] * 2,
               out_shape=[SDS((T, FOX_W), F32)] * 2, sem=("parallel", "parallel", "parallel"), comm=comm)
    return _run(call, (proj, proj, proj, cum, cum_t), name, comm, carried)


def fox_attn_bwd(name, proj, o, do, lse, cum, cum_t, *, batch, carried=None):
    T = proj.shape[0]
    S = T // batch
    tq = tk = _tile(S, 512)
    nq = S // tq
    scale = HEAD_DIM ** -0.5
    qb, kb, vb = OFF_Q // LANE, OFF_K // LANE, OFF_V // LANE

    def body(q_ref, k_ref, v_ref, o_ref, do_ref, lse_ref, cum_ref, cumt_ref,
             dq_ref, dk_ref, dv_ref, dcum_ref, dk_acc, dv_acc, dcumt_ref):
        p = pl.program_id(1)
        dk_acc[...] = jnp.zeros_like(dk_acc)
        dv_acc[...] = jnp.zeros_like(dv_acc)
        dcum_ref[...] = jnp.zeros_like(dcum_ref)
        dcumt_ref[...] = jnp.zeros_like(dcumt_ref)
        lane_q = _iota((tq, LANE), 1)
        halfq, halfk = lane_q // HEAD_DIM, _iota((tk, LANE), 1) // HEAD_DIM
        rowi, coli = _iota((tq, tk), 0), _iota((tq, tk), 1)
        row8 = _iota((8, tk), 0)

        def qblock(i, _):
            qoff = pl.multiple_of(i * tq, tq)
            cq = cum_ref[pl.ds(qoff, tq), :]
            qs = q_ref[pl.ds(qoff, tq), :] * scale
            doq = do_ref[pl.ds(qoff, tq), :]
            lse = lse_ref[pl.ds(qoff, tq), :]
            delta = doq * o_ref[pl.ds(qoff, tq), :]
            qh, doh, cqh, lseh, dlt = [], [], [], [], []
            for hh in range(2):
                qh.append(jnp.where(halfq == hh, qs, 0.0).astype(BF16))
                doh.append(jnp.where(halfq == hh, doq, 0.0).astype(BF16))
                cqh.append(jnp.sum(jnp.where(lane_q == FOX_LANE0 + 2 * p + hh, cq, 0.0), axis=1, keepdims=True))
                lseh.append(jnp.sum(jnp.where(lane_q == hh, lse, 0.0), axis=1, keepdims=True))
                dlt.append(jnp.sum(jnp.where(halfq == hh, delta, 0.0), axis=1, keepdims=True))

            def step(j, carry, diag):
                dq, rs = carry
                off = pl.multiple_of(j * tk, tk)
                kj = k_ref[pl.ds(off, tk), :].astype(BF16)
                vj = v_ref[pl.ds(off, tk), :].astype(BF16)
                ckt = cumt_ref[:, pl.ds(off, tk)]
                dss, prs, ks = [], [], []
                for hh in range(2):
                    ck = jnp.sum(jnp.where(row8 == 2 * p + hh, ckt, 0.0), axis=0, keepdims=True)
                    logits = _nt(qh[hh], kj) + ((cqh[hh] - lseh[hh]) - ck)
                    if diag:
                        logits = jnp.where(rowi >= coli, logits, -jnp.inf)
                    pr = jnp.exp(logits)
                    ds = pr * (_nt(doh[hh], vj) - dlt[hh])
                    rs = rs + jnp.where(lane_q == FOX_LANE0 + 2 * p + hh, jnp.sum(ds, axis=1, keepdims=True), 0.0)
                    cs = _colsum(ds)
                    dcumt_ref[0:8, pl.ds(off, tk)] += jnp.where(row8 == 2 * p + hh, cs, 0.0)
                    dss.append(ds.astype(BF16))
                    prs.append(pr.astype(BF16))
                    ks.append(jnp.where(halfk == hh, kj, jnp.zeros_like(kj)))
                dq = dq + jnp.dot(jnp.concatenate(dss, axis=1), jnp.concatenate(ks, axis=0), preferred_element_type=F32)
                dk_acc[pl.ds(off, tk), :] += _tn(jnp.concatenate(dss, axis=0), jnp.concatenate(qh, axis=0))
                dv_acc[pl.ds(off, tk), :] += _tn(jnp.concatenate(prs, axis=0), jnp.concatenate(doh, axis=0))
                return dq, rs

            carry = lax.fori_loop(0, i, functools.partial(step, diag=False),
                                  (jnp.zeros((tq, LANE), F32), jnp.zeros((tq, LANE), F32)))
            dq, rs = step(i, carry, True)
            dq_ref[pl.ds(qoff, tq), :] = (dq * scale).astype(dq_ref.dtype)
            dcum_ref[pl.ds(qoff, tq), :] += rs
            return 0

        lax.fori_loop(0, nq, qblock, 0)
        dk_ref[...] = dk_acc[...].astype(dk_ref.dtype)
        dv_ref[...] = dv_acc[...].astype(dv_ref.dtype)
        dct = dcumt_ref[...].T
        dcum_ref[...] = dcum_ref[...] - pltpu.roll(dct, FOX_LANE0, 1)

    sp = lambda c0: pl.BlockSpec((S, LANE), lambda b, p: (b, c0 + p))
    s0 = pl.BlockSpec((S, LANE), lambda b, p: (b, 0))
    comm = carried.take(name) if carried is not None else None
    call = _pc(body, name=name, grid=(batch, 3),
               in_specs=[sp(qb), sp(kb), sp(vb), sp(0), sp(0), sp(0), s0, pl.BlockSpec((8, S), lambda b, p: (b, 0))],
               out_specs=[sp(0)] * 4,
               out_shape=[SDS((T, FOX_W), BF16)] * 3 + [SDS((T, FOX_W), F32)],
               scratch=[pltpu.VMEM((S, LANE), F32), pltpu.VMEM((S, LANE), F32), pltpu.VMEM((LANE, S), F32)],
               sem=("parallel", "parallel"), comm=comm)
    return _run(call, (proj, proj, proj, o, do, lse, cum, cum_t), name, comm, carried)


def _row(v, width=None, at=0):
    v = v.astype(F32)
    width = width or v.shape[0]
    return jnp.pad(v, (at, width - at - v.shape[0]))[None, :]


def _pad8(w4):
    return jnp.pad(w4.astype(F32), ((0, 4), (0, 0)))


def _block_diag(w):
    eye = jnp.eye(w.shape[0], dtype=w.dtype)
    return (w[:, :, None, :] * eye[:, None, :, None]).reshape(LRU_W, LRU_W)


def prep_layer(f):
    cw, cb = f["ssd_conv_w"], f["ssd_conv_b"]
    return LazyDict(
        win=lambda: permute_in_cols(f["w_in"]), wout=lambda: f["w_out"],
        wg=lambda: f["w_gate_t"] if "w_gate_t" in f else f["w_gate"].T,
        wu=lambda: f["w_up_t"] if "w_up_t" in f else f["w_up"].T,
        wd=lambda: f["w_down"], wpg=lambda: f["w_ple_gate"], wpp=lambda: f["w_ple_proj"],
        wax=jnp.concatenate([_block_diag(f["lru_w_a"]), _block_diag(f["lru_w_x"])], axis=1),
        g1=_row(f["norm1_g"]), g2=_row(f["norm2_g"]), g3=_row(f["norm3_g"]),
        cw_xs=_pad8(cw[:, :384]), cb_xs=_row(cb[:384]), cw_b=_pad8(cw[:, 384:640]), cb_b=_row(cb[384:640]),
        cw_c=_pad8(cw[:, 640:]), cb_c=_row(cb[640:]), cw_l=_pad8(f["lru_conv_w"]), cb_l=_row(f["lru_conv_b"]),
        dtbias_row=_row(f["ssd_dt_bias"], LANE), alog_row=_row(f["ssd_a_log"], LANE),
        dexp=jnp.repeat(f["ssd_d"].astype(F32), HEAD_DIM)[None, :], g_ssd=_row(f["ssd_norm_g"]),
        b_ax=_row(jnp.concatenate([f["lru_b_a"], f["lru_b_x"]])), lam=_row(f["lru_lambda"]), g_lru=_row(f["lru_norm_g"]),
        bf_row=_row(f["fox_b_f"], LANE, FOX_LANE0), g_fox=_row(f["fox_norm_g"]), b_pg=_row(f["b_ple_gate"]))


def unprep_grads(g):
    blocks = lambda m: jnp.stack([m[i * 64:(i + 1) * 64, i * 64:(i + 1) * 64] for i in range(4)])
    return dict(
        norm1_g=g["g1"][0], w_in=unpermute_in_cols(g["win"]),
        ssd_conv_w=jnp.concatenate([g["cw_xs"][:4], g["cw_b"][:4], g["cw_c"][:4]], axis=1),
        ssd_conv_b=jnp.concatenate([g["cb_xs"][0], g["cb_b"][0], g["cb_c"][0]]),
        ssd_dt_bias=g["dtbias_row"][0, :N_HEADS], ssd_a_log=g["alog_row"][0, :N_HEADS],
        ssd_d=jnp.sum(g["dexp"].reshape(N_HEADS, HEAD_DIM), axis=1), ssd_norm_g=g["g_ssd"][0],
        lru_conv_w=g["cw_l"][:4], lru_conv_b=g["cb_l"][0],
        lru_w_a=blocks(g["wax"][:, :LRU_W]), lru_b_a=g["b_ax"][0, :LRU_W],
        lru_w_x=blocks(g["wax"][:, LRU_W:]), lru_b_x=g["b_ax"][0, LRU_W:],
        lru_lambda=g["lam"][0], lru_norm_g=g["g_lru"][0],
        fox_b_f=g["bf_row"][0, FOX_LANE0:FOX_LANE0 + N_HEADS], fox_norm_g=g["g_fox"][0],
        w_out=g["wout"], norm2_g=g["g2"][0], w_gate=g["wg"].T, w_up=g["wu"].T, w_down=g["wd"],
        norm3_g=g["g3"][0], w_ple_gate=g["wpg"], b_ple_gate=g["b_pg"][0], w_ple_proj=g["wpp"])


def _view(a, off, width):
    return (a, off // width, width)


def _add_epilogue(acc, e):
    return (acc + e,)


def mixer_fwd(proj, w, batch, tag, carried=None):
    sm = _view(proj, OFF_SM, LANE)
    conv = functools.partial(seq_conv, batch=batch)
    cmc = conv(f"{tag}_conv_c", proj, OFF_C, 256, w["cw_c"], w["cb_c"], silu=True, out_dtype=BF16)
    bmc = conv(f"{tag}_conv_b", proj, OFF_B, 256, w["cw_b"], w["cb_b"], silu=True, out_dtype=BF16)
    xs_act = conv(f"{tag}_conv_xs", proj, OFF_XS, SSD_W, w["cw_xs"], w["cb_xs"], silu=True, out_dtype=F32)
    xl = conv(f"{tag}_conv_l", proj, OFF_LX, LRU_W, w["cw_l"], w["cb_l"], silu=False, out_dtype=F32)
    adt, xd = rowwise(f"{tag}_ssd_elt", _ssd_elt, [sm, xs_act], [w["dtbias_row"], w["alog_row"]],
                      [(LANE, F32), (SSD_W, BF16)])
    cum_a, cum_at = seq_cumsum(f"{tag}_cum_a", adt, batch=batch, trow=0)
    yraw = ssd_attn_fwd(f"{tag}_ssd_fwd", cmc, bmc, xd, cum_a, cum_at, batch=batch, carried=carried)
    logf = rowwise(f"{tag}_fox_elt", _fox_elt, [sm], [w["bf_row"]], [(LANE, F32)])
    cum_f, cum_ft = seq_cumsum(f"{tag}_cum_f", logf, batch=batch, trow=FOX_LANE0)
    o, lse = fox_attn_fwd(f"{tag}_fox_fwd", proj, cum_f, cum_ft, batch=batch, carried=carried)
    pre = mm(xl, w["wax"], name=f"{tag}_mm_lru_gates")
    a, u = rowwise(f"{tag}_lru_elt", _lru_elt, [xl, pre], [w["b_ax"], w["lam"]], [(LRU_W, F32), (LRU_W, F32)])
    hl = lru_scan(f"{tag}_lru_scan", a, u, batch=batch)
    ycat = rowwise(f"{tag}_mix_post", _mix_post,
                   [yraw, xs_act, _view(proj, OFF_Z, SSD_W), hl, _view(proj, OFF_LG, LRU_W), o],
                   [w["dexp"], w["g_ssd"], w["g_lru"], w["g_fox"]], [(D_MODEL, BF16)], tr=256)
    saved = dict(cmc=cmc, bmc=bmc, xs_act=xs_act, xl=xl, xd=xd, cum_a=cum_a, cum_at=cum_at, yraw=yraw,
                 cum_f=cum_f, cum_ft=cum_ft, o=o, lse=lse, pre=pre, a=a, hl=hl)
    return ycat, saved


def mixer_bwd(dycat, proj, w, s, batch, tag, carried=None):
    sm = _view(proj, OFF_SM, LANE)
    g = {}

    def post_bwd(yraw, xs_act, z, hl, lg, o, dyc, dexp, g_ssd, g_lru, g_fox):
        return jax.vjp(_mix_post, yraw, xs_act, z, hl, lg, o, dexp, g_ssd, g_lru, g_fox)[1](dyc)

    (dyraw, dxs1, dz, dhl, dlg, do, g["dexp"], g["g_ssd"], g["g_lru"], g["g_fox"]) = rowwise(
        f"{tag}_mix_post_bwd", post_bwd,
        [s["yraw"], s["xs_act"], _view(proj, OFF_Z, SSD_W), s["hl"], _view(proj, OFF_LG, LRU_W), s["o"], dycat],
        [w["dexp"], w["g_ssd"], w["g_lru"], w["g_fox"]],
        [(SSD_W, F32), (SSD_W, F32), (SSD_W, BF16), (LRU_W, F32), (LRU_W, BF16), (FOX_W, F32)],
        [SSD_W, SSD_W, LRU_W, FOX_W], tr=256)

    dq, dk, dv, dcum3 = fox_attn_bwd(f"{tag}_fox_bwd", proj, s["o"], do, s["lse"], s["cum_f"], s["cum_ft"], batch=batch,
                                     carried=carried)
    dlogf = seq_cumsum(f"{tag}_rcum_f", dcum3, batch=batch, reverse=True, nsum=3)

    dxd, dbm, dcm, dcum_a = ssd_attn_bwd(f"{tag}_ssd_bwd", s["cmc"], s["bmc"], s["xd"], s["cum_a"], s["cum_at"], dyraw,
                                         batch=batch, carried=carried)
    dadt = seq_cumsum(f"{tag}_rcum_a", dcum_a, batch=batch, reverse=True)

    def ssd_elt_bwd(small, xs_act, dadt_, dxd_, dxs1_, dtbias, alog):
        dsm, dxs, ddtb, dalog = jax.vjp(_ssd_elt, small, xs_act, dtbias, alog)[1]((dadt_, dxd_))
        return dsm, dxs + dxs1_, ddtb, dalog

    dsm_s, dxs_act, g["dtbias_row"], g["alog_row"] = rowwise(
        f"{tag}_ssd_elt_bwd", ssd_elt_bwd, [sm, s["xs_act"], dadt, dxd, dxs1], [w["dtbias_row"], w["alog_row"]],
        [(LANE, F32), (SSD_W, F32)], [LANE, LANE])

    def fox_elt_bwd(small, dlogf_, dsm_s_, bf_row):
        dsm, dbf = jax.vjp(_fox_elt, small, bf_row)[1](dlogf_)
        return dsm + dsm_s_, dbf

    dsm, g["bf_row"] = rowwise(f"{tag}_fox_elt_bwd", fox_elt_bwd, [sm, dlogf, dsm_s], [w["bf_row"]],
                               [(LANE, BF16)], [LANE])

    cbwd = functools.partial(seq_conv_bwd, batch=batch)
    dxs_raw, g["cw_xs"], g["cb_xs"] = cbwd(f"{tag}_conv_xs_bwd", proj, OFF_XS, SSD_W, w["cw_xs"], w["cb_xs"], dxs_act, silu=True)
    db_raw, g["cw_b"], g["cb_b"] = cbwd(f"{tag}_conv_b_bwd", proj, OFF_B, 256, w["cw_b"], w["cb_b"], dbm, silu=True)
    dc_raw, g["cw_c"], g["cb_c"] = cbwd(f"{tag}_conv_c_bwd", proj, OFF_C, 256, w["cw_c"], w["cb_c"], dcm, silu=True)

    da, du = lru_scan_bwd(f"{tag}_lru_scan_bwd", s["a"], s["hl"], dhl, batch=batch)

    def lru_elt_bwd(xl, pre, da_, du_, b_ax, lam):
        return jax.vjp(_lru_elt, xl, pre, b_ax, lam)[1]((da_, du_))

    dxl1, dpre, g["b_ax"], g["lam"] = rowwise(
        f"{tag}_lru_elt_bwd", lru_elt_bwd, [s["xl"], s["pre"], da, du], [w["b_ax"], w["lam"]],
        [(LRU_W, F32), (2 * LRU_W, BF16)], [2 * LRU_W, LRU_W])
    g["wax"] = mm(s["xl"], dpre, ta=True, name=f"{tag}_mm_dwax")
    dxl = mm(dpre, w["wax"], tb=True, extras=[dxl1], epilogue=_add_epilogue, name=f"{tag}_mm_dxl")
    dlx_raw, g["cw_l"], g["cb_l"] = cbwd(f"{tag}_conv_l_bwd", proj, OFF_LX, LRU_W, w["cw_l"], w["cb_l"], dxl, silu=False)

    dproj = jnp.concatenate([db_raw, dc_raw, dlx_raw, dlg, dsm, dz, dxs_raw, dq, dk, dv], axis=1)
    return dproj, g


def layer_fwd(h0, p_l, w, batch, tag, carried=None):
    u1 = rowwise(f"{tag}_rms1", _rms, [h0], [w["g1"]], [(D_MODEL, BF16)])
    proj = mm(u1, w["win"], name=f"{tag}_mm_in", carried=carried)
    ycat, ms = mixer_fwd(proj, w, batch, tag, carried)
    h1 = mm(ycat, w["wout"], extras=[h0], epilogue=_add_epilogue, name=f"{tag}_mm_out")
    u2 = rowwise(f"{tag}_rms2", _rms, [h1], [w["g2"]], [(D_MODEL, BF16)])
    gate, up, act = mm(u2, [w["wg"], w["wu"]], tb=True, out_dtypes=(BF16, BF16, BF16), epilogue=_swiglu_epilogue,
                       tm=512, tn=D_FF // 2, name=f"{tag}_mm_gu", carried=carried)
    h2 = mm(act, w["wd"], extras=[h1], epilogue=_add_epilogue, tm=512, tk=D_FF, name=f"{tag}_mm_down")
    u3 = rowwise(f"{tag}_rms3", _rms, [h2], [w["g3"]], [(D_MODEL, BF16)])
    pp = mm(p_l, w["wpp"], name=f"{tag}_mm_pp")
    h3, pg = mm(u3, w["wpg"], extras=[pp, h2], col_params=[w["b_pg"]], epilogue=_ple_epilogue,
                out_dtypes=(F32, F32), tm=512, name=f"{tag}_mm_pg")
    saved = dict(h0=h0, u1=u1, proj=proj, ycat=ycat, h1=h1, u2=u2, gate=gate, up=up, act=act, h2=h2, u3=u3, pg=pg,
                 pp=pp, mixer=ms)
    return h3, saved


def _swiglu_epilogue(acc_g, acc_u):
    return acc_g, acc_u, _silu(acc_g) * acc_u


def _swiglu_bwd_epilogue(dact, gate, up):
    return jax.vjp(lambda g_, u_: _silu(g_) * u_, gate.astype(F32), up.astype(F32))[1](dact)


def _ple_epilogue(acc, pp, h2, b):
    return h2 + _ple(acc, pp, b), acc


def _rms_bwd_epilogue(du, h, dres, g):
    dh, dg = jax.vjp(_rms, h, g)[1](du)
    return dh + dres, dg


def layer_bwd(dh3, p_l, w, s, batch, tag, carried=None, on_early_grads=None):
    def ple_bwd(pg, pp, dh, b):
        return jax.vjp(_ple, pg, pp, b)[1](dh)

    norm_bwd = dict(epilogue=_rms_bwd_epilogue, partials=1, tm=512, tn=D_MODEL, tb=True)

    d_pg, d_pp, g_bpg = rowwise(f"{tag}_ple_bwd", ple_bwd, [s["pg"], s["pp"], dh3], [w["b_pg"]],
                                [(D_MODEL, BF16), (D_MODEL, BF16)], [D_MODEL])
    g = dict(b_pg=g_bpg)
    g["wpp"] = mm(p_l, d_pp, ta=True, name=f"{tag}_mm_dwpp")
    g["wpg"] = mm(s["u3"], d_pg, ta=True, name=f"{tag}_mm_dwpg")
    dh2, dg3 = mm(d_pg, w["wpg"], extras=[s["h2"], dh3], col_params=[w["g3"]], name=f"{tag}_mm_du3", **norm_bwd)
    g["g3"] = sum_slices(f"{tag}_sum_dg3", dg3)

    d_gate, d_up = mm(dh2, w["wd"], tb=True, extras=[s["gate"], s["up"]], epilogue=_swiglu_bwd_epilogue,
                      out_dtypes=(BF16, BF16), tm=512, tn=D_FF // 2, name=f"{tag}_mm_dact")
    g["wd"] = mm(s["act"], dh2, ta=True, name=f"{tag}_mm_dwd")
    g["wg"] = mm(d_gate, s["u2"], ta=True, name=f"{tag}_mm_dwg")
    g["wu"] = mm(d_up, s["u2"], ta=True, name=f"{tag}_mm_dwu")
    dh1, dg2 = mm([d_gate, d_up], [w["wg"], w["wu"]], extras=[s["h1"], dh2], col_params=[w["g2"]],
                  name=f"{tag}_mm_du2", **{**norm_bwd, "tb": False})
    g["g2"] = sum_slices(f"{tag}_sum_dg2", dg2)

    dycat = mm(dh1, w["wout"], tb=True, name=f"{tag}_mm_dycat")
    g["wout"] = mm(s["ycat"], dh1, ta=True, name=f"{tag}_mm_dwout")
    if on_early_grads is not None:
        on_early_grads(g)
    dproj, gm = mixer_bwd(dycat, s["proj"], w, s["mixer"], batch, tag, carried)
    g.update(gm)
    g["win"] = mm(s["u1"], dproj, ta=True, name=f"{tag}_mm_dwin")
    dh0, dg1 = mm(dproj, w["win"], extras=[s["h0"], dh1], col_params=[w["g1"]], name=f"{tag}_mm_du1", **norm_bwd)
    g["g1"] = sum_slices(f"{tag}_sum_dg1", dg1)
    return dh0, g


def _loss_fwd_bwd(h, tgt, gf):
    def f(h_, gf_):
        e = _rms(h_, gf_) - tgt
        return 0.5 * jnp.sum(jnp.mean(e * e, axis=-1, keepdims=True), axis=0, keepdims=True)

    loss, vj = jax.vjp(f, h, gf)
    dh, dgf = vj(jnp.ones((1, 1), F32))
    return dh, jnp.broadcast_to(loss, (1, LANE)), dgf


def local_step(x, p, tgt, layers, final_g, carried=None, on_layer_grads=None, on_early_grads=None):
    batch, S, _ = x.shape
    T = batch * S
    h = x.reshape(T, D_MODEL)
    saved, weights = [], []
    for l, w in enumerate(layers):
        w = w() if callable(w) else w
        weights.append(w)
        h, s = layer_fwd(h, p[l].reshape(T, PLE_DIM), w, batch, f"l{l}", carried)
        saved.append(s)
    dh, loss, dgf = rowwise("loss", _loss_fwd_bwd, [h, tgt.reshape(T, D_MODEL)], [_row(final_g)],
                            [(D_MODEL, F32)], [LANE, D_MODEL], tr=256)
    grads = [None] * len(layers)
    for l in reversed(range(len(layers))):
        early = functools.partial(on_early_grads, l) if on_early_grads is not None else None
        dh, grads[l] = layer_bwd(dh, p[l].reshape(T, PLE_DIM), weights[l], saved[l], batch, f"l{l}", carried, early)
        if on_layer_grads is not None:
            on_layer_grads(l, grads[l])
    return loss[0, 0], dh.reshape(batch, S, D_MODEL), grads, dgf[0]


MESH = pl.DeviceIdType.MESH
N_DEV = 8
N_CHIP = 4
ANY = pl.BlockSpec(memory_space=pl.ANY)


def _pos():
    return lax.axis_index("x"), lax.axis_index("y"), lax.axis_index("c")


def _comm_call(body, name, out_shape, n_in, scratch):
    return pl.pallas_call(body, name=name, out_shape=out_shape, in_specs=[ANY] * n_in, out_specs=ANY,
                          scratch_shapes=scratch)


def all_gather8(name, blk):
    def body(x_ref, out_ref, send_sems, recv_sems, local_sem):
        x, y, c = _pos()
        me, sibling = (x, y, c), (x, y, 1 - c)
        chips = [(1 - x, y), (x, 1 - y), (1 - x, 1 - y)]

        def rows(px, py, pcore):
            return out_ref.at[4 * px + 2 * py + pcore]

        def copy(k, block, to, src=None):
            return pltpu.make_async_remote_copy(
                src_ref=rows(*block) if src is None else src, dst_ref=rows(*block),
                send_sem=send_sems.at[k], recv_sem=recv_sems.at[k], device_id=to, device_id_type=MESH)

        mine = pltpu.make_async_copy(x_ref, rows(*me), local_sem)
        mine.start()
        first = [copy(0, me, sibling, src=x_ref)]
        first += [copy(1 + j, me, (*chip, c), src=x_ref) for j, chip in enumerate(chips)]
        for cp in first:
            cp.start()
        passed = [copy(4 + j, (*chip, c), sibling) for j, chip in enumerate(chips)]
        for j, chip in enumerate(chips):
            copy(1 + j, (*chip, c), me).wait_recv()
            passed[j].start()
        copy(0, sibling, me).wait_recv()
        for j, chip in enumerate(chips):
            copy(4 + j, (*chip, 1 - c), me).wait_recv()
        for cp in first + passed:
            cp.wait_send()
        mine.wait()

    return _comm_call(body, name, SDS((N_DEV,) + blk.shape, blk.dtype), 1,
                      [pltpu.SemaphoreType.DMA((7,)), pltpu.SemaphoreType.DMA((7,)), pltpu.SemaphoreType.DMA])(blk)


def _comm_call_list(body, name, out_shapes, n_in, scratch):
    return pl.pallas_call(body, name=name, out_shape=out_shapes, in_specs=[ANY] * n_in, out_specs=[ANY] * len(out_shapes),
                          scratch_shapes=scratch)


class Exchange:
    def __init__(self, inputs, out_shapes, sems, start, wait, aliases=None):
        self.inputs, self.out_shapes, self.sems = list(inputs), list(out_shapes), list(sems)
        self.start, self.wait, self.aliases = start, wait, dict(aliases or {})


def combine(a, b):
    ai, ao, as_ = len(a.inputs), len(a.out_shapes), len(a.sems)

    def split(cins, couts, sems):
        return (cins[:ai], couts[:ao], sems[:as_]), (cins[ai:], couts[ao:], sems[as_:])

    def start(cins, couts, sems):
        pa, pb = split(cins, couts, sems)
        a.start(*pa)
        b.start(*pb)

    def wait(cins, couts, sems):
        pa, pb = split(cins, couts, sems)
        a.wait(*pa)
        b.wait(*pb)

    aliases = dict(a.aliases)
    aliases.update({ai + i: ao + o for i, o in b.aliases.items()})
    return Exchange(a.inputs + b.inputs, a.out_shapes + b.out_shapes, a.sems + b.sems, start, wait, aliases)


def run_exchange(name, ex):
    n_ci, n_co = len(ex.inputs), len(ex.out_shapes)

    def body(*refs):
        cins, couts, csems = refs[:n_ci], refs[n_ci:n_ci + n_co], refs[n_ci + n_co:]
        ex.start(cins, couts, csems)
        ex.wait(cins, couts, csems)

    return pl.pallas_call(body, name=name, out_shape=ex.out_shapes, in_specs=[ANY] * n_ci, out_specs=[ANY] * n_co,
                          scratch_shapes=ex.sems, input_output_aliases=ex.aliases)(*ex.inputs)


def _peers():
    x, y, c = _pos()
    return x, y, c, 2 * x + y, [(1 - x, y), (x, 1 - y), (1 - x, 1 - y)]


def _remote(src, dst, send_sem, recv_sem, to):
    return pltpu.make_async_remote_copy(src_ref=src, dst_ref=dst, send_sem=send_sem, recv_sem=recv_sem,
                                        device_id=to, device_id_type=MESH)


def gather_spread(shards, layer):
    n_t = len(shards)
    halves = [s.shape[1] // 2 for s in shards]

    def copies(cins, couts, sems):
        send_sems, recv_sems, local_sems = sems
        x, y, c, my_chip, chips = _peers()
        local, sends, recvs = [], [], []
        for t in range(n_t):
            h = halves[t]
            src = cins[t].at[layer, pl.ds(c * h, h)]
            mine = couts[t].at[my_chip, pl.ds(c * h, h)]
            local.append(pltpu.make_async_copy(src, mine, local_sems.at[t]))
            sends.append(_remote(src, mine, send_sems.at[0, t], recv_sems.at[0, t], (x, y, 1 - c)))
            recvs.append(_remote(src, couts[t].at[my_chip, pl.ds((1 - c) * h, h)], send_sems.at[0, t],
                                 recv_sems.at[0, t], (x, y, 1 - c)))
            for j, (px, py) in enumerate(chips):
                sends.append(_remote(src, mine, send_sems.at[1 + j, t], recv_sems.at[1 + j, t], (px, py, c)))
                recvs.append(_remote(src, couts[t].at[2 * px + py, pl.ds(c * h, h)], send_sems.at[1 + j, t],
                                     recv_sems.at[1 + j, t], (px, py, c)))
        return local, sends, recvs

    def start(cins, couts, sems):
        local, sends, _ = copies(cins, couts, sems)
        for cp in local + sends:
            cp.start()

    def wait(cins, couts, sems):
        local, sends, recvs = copies(cins, couts, sems)
        for cp in recvs:
            cp.wait_recv()
        for cp in sends:
            cp.wait_send()
        for cp in local:
            cp.wait()

    return Exchange(shards, [SDS((N_CHIP,) + s.shape[1:], s.dtype) for s in shards],
                    [pltpu.SemaphoreType.DMA((4, n_t)), pltpu.SemaphoreType.DMA((4, n_t)),
                     pltpu.SemaphoreType.DMA((n_t,))], start, wait)


def gather_pass_on(slots):
    n_t = len(slots)
    halves = [s.shape[1] // 2 for s in slots]

    def copies(cins, couts, sems):
        send_sems, recv_sems = sems
        x, y, c, my_chip, chips = _peers()
        sends, recvs = [], []
        for t in range(n_t):
            h = halves[t]
            for j, (px, py) in enumerate(chips):
                k = 2 * px + py
                sends.append(_remote(cins[t].at[k, pl.ds(c * h, h)], couts[t].at[k, pl.ds(c * h, h)],
                                     send_sems.at[j, t], recv_sems.at[j, t], (x, y, 1 - c)))
                recvs.append(_remote(cins[t].at[k, pl.ds(c * h, h)], couts[t].at[k, pl.ds((1 - c) * h, h)],
                                     send_sems.at[j, t], recv_sems.at[j, t], (x, y, 1 - c)))
        return sends, recvs

    def start(cins, couts, sems):
        for cp in copies(cins, couts, sems)[0]:
            cp.start()

    def wait(cins, couts, sems):
        sends, recvs = copies(cins, couts, sems)
        for cp in recvs:
            cp.wait_recv()
        for cp in sends:
            cp.wait_send()

    return Exchange(slots, [SDS(s.shape, s.dtype) for s in slots],
                    [pltpu.SemaphoreType.DMA((3, n_t)), pltpu.SemaphoreType.DMA((3, n_t))], start, wait,
                    aliases={t: t for t in range(n_t)})


def chips_exchange(vs):
    n_t = len(vs)

    def copies(cins, couts, sems):
        send_sems, recv_sems, local_sems = sems
        x, y, c, my_chip, chips = _peers()
        local = [pltpu.make_async_copy(cins[t].at[my_chip], couts[t].at[my_chip], local_sems.at[t]) for t in range(n_t)]
        sends, recvs = [], []
        for k, (px, py) in enumerate(chips):
            for t in range(n_t):
                sends.append(_remote(cins[t].at[2 * px + py], couts[t].at[my_chip], send_sems.at[k, t],
                                     recv_sems.at[k, t], (px, py, c)))
                recvs.append(_remote(cins[t].at[my_chip], couts[t].at[2 * px + py], send_sems.at[k, t],
                                     recv_sems.at[k, t], (px, py, c)))
        return local, sends, recvs

    def start(cins, couts, sems):
        local, sends, _ = copies(cins, couts, sems)
        for cp in local + sends:
            cp.start()

    def wait(cins, couts, sems):
        local, sends, recvs = copies(cins, couts, sems)
        for cp in recvs:
            cp.wait_recv()
        for cp in sends:
            cp.wait_send()
        for cp in local:
            cp.wait()

    return Exchange(vs, [SDS(v.shape, v.dtype) for v in vs],
                    [pltpu.SemaphoreType.DMA((3, n_t)), pltpu.SemaphoreType.DMA((3, n_t)),
                     pltpu.SemaphoreType.DMA((n_t,))], start, wait)


def swap_with_sibling(name, vs):
    n_t = len(vs)

    def body(*refs):
        v_refs, out_refs, send_sems, recv_sems = refs[:n_t], refs[n_t:2 * n_t], refs[-2], refs[-1]
        x, y, c = _pos()
        cps = [pltpu.make_async_remote_copy(src_ref=v_refs[t], dst_ref=out_refs[t], send_sem=send_sems.at[t],
                                            recv_sem=recv_sems.at[t], device_id=(x, y, 1 - c), device_id_type=MESH)
               for t in range(n_t)]
        for cp in cps:
            cp.start()
        for cp in cps:
            cp.wait()

    return _comm_call_list(body, name, [SDS(v.shape, v.dtype) for v in vs], n_t,
                           [pltpu.SemaphoreType.DMA((n_t,)), pltpu.SemaphoreType.DMA((n_t,))])(*vs)


_ROW_BLOCKS = (1024, 704, 512, 352, 256, 128, 64, 32, 16, 8)


def sum_slices(name, v, tr=512):
    n, R, C = v.shape
    tr = _pick(R, _ROW_BLOCKS)

    def body(v_ref, o_ref):
        acc = v_ref[0].astype(F32)
        for k in range(1, n):
            acc = acc + v_ref[k].astype(F32)
        o_ref[...] = acc

    return _pc(body, name=name, grid=(R // tr,), in_specs=[pl.BlockSpec((n, tr, C), lambda i: (0, i, 0))],
               out_specs=pl.BlockSpec((tr, C), lambda i: (i, 0)), out_shape=SDS((R, C), F32), sem=("parallel",))(v)


def add_slices(name, a, b, out_dtype):
    n, R, C = a.shape
    tr = _pick(R, _ROW_BLOCKS)

    def body(a_ref, b_ref, o_ref):
        o_ref[...] = (a_ref[...].astype(F32) + b_ref[...].astype(F32)).astype(o_ref.dtype)

    spec = pl.BlockSpec((1, tr, C), lambda k, i: (k, i, 0))
    return _pc(body, name=name, grid=(n, R // tr), in_specs=[spec, spec], out_specs=spec,
               out_shape=SDS(a.shape, out_dtype), sem=("parallel", "parallel"))(a, b)


def adamw(name, w, g, m, v):
    L, R, C = w.shape
    tr = _pick(R, (512, 352, 256, 128, 64, 32, 16, 8))
    c1 = 1.0 / (1.0 - ADAM_B1 ** ADAM_STEP)
    c2 = 1.0 / (1.0 - ADAM_B2 ** ADAM_STEP)

    def body(w_ref, g_ref, m_ref, v_ref, d_ref, nm_ref, nv_ref):
        gv = g_ref[...]
        nm = ADAM_B1 * m_ref[...] + (1.0 - ADAM_B1) * gv
        nv = ADAM_B2 * v_ref[...] + (1.0 - ADAM_B2) * (gv * gv)
        d_ref[...] = -ADAM_LR * ((nm * c1) / (jnp.sqrt(nv * c2) + ADAM_EPS) + ADAM_WD * w_ref[...])
        nm_ref[...] = nm
        nv_ref[...] = nv

    if R < 8:
        tl = 32
        spec = pl.BlockSpec((tl, R, C), lambda i, _: (i, 0, 0))
        grid = (pl.cdiv(L, tl), 1)
    else:
        spec = pl.BlockSpec((1, tr, C), lambda l, i: (l, i, 0))
        grid = (L, R // tr)
    return _pc(body, name=name, grid=grid, in_specs=[spec] * 4, out_specs=[spec] * 3,
               out_shape=[SDS(w.shape, F32)] * 3, sem=("parallel", "parallel"))(w, g, m, v)


WEIGHTS = ["norm1_g", "w_in", "ssd_conv_w", "ssd_conv_b", "ssd_dt_bias", "ssd_a_log", "ssd_d", "ssd_norm_g",
           "lru_conv_w", "lru_conv_b", "lru_w_a", "lru_b_a", "lru_w_x", "lru_b_x", "lru_lambda", "lru_norm_g",
           "fox_b_f", "fox_norm_g", "w_out", "norm2_g", "w_gate", "w_up", "w_down", "norm3_g", "w_ple_gate",
           "b_ple_gate", "w_ple_proj", "final_norm_g"]
BIG = {"w_in": 2, "w_out": 1, "w_gate": 2, "w_up": 2, "w_down": 1, "w_ple_gate": 1, "w_ple_proj": 2}
SHARDED_SMALL = {"ssd_conv_w": 2, "lru_conv_w": 2}
ADAM_VIEW = {"w_in": ((2, 0, 1), (1, 2, 0)), "w_gate": ((0, 2, 1), (0, 2, 1)), "w_up": ((0, 2, 1), (0, 2, 1))}
TRANSPOSED = ("w_gate", "w_up")
SMALL = [n for n in WEIGHTS if n not in BIG]


def _pack(arrs, rows_multiple):
    flat = jnp.concatenate([a.reshape(-1) for a in arrs])
    per = rows_multiple * LANE
    n = -(-flat.shape[0] // per) * per
    return jnp.pad(flat, (0, n - flat.shape[0])).reshape(n // LANE, LANE)


def _unpack(flat2d, shapes):
    flat = flat2d.reshape(-1)
    out, off = [], 0
    for s in shapes:
        n = int(np.prod(s))
        out.append(flat[off:off + n].reshape(s))
        off += n
    return out


def _gather_shards(name, shards, axes, dtype):
    c = lax.axis_index("c")
    packed = _pack([s.astype(dtype) for s in shards], 32)
    half = packed.shape[0] // 2
    mine = lax.dynamic_slice_in_dim(packed, c * half, half, 0)
    got = all_gather8(name, mine).reshape(N_CHIP, 2 * half, LANE)
    per_chip = [_unpack(got[k], [s.shape for s in shards]) for k in range(N_CHIP)]
    return [jnp.concatenate([per_chip[k][i] for k in range(N_CHIP)], axis=ax) for i, ax in enumerate(axes)]


def kernel(x, p, norm1_g, w_in, ssd_conv_w, ssd_conv_b, ssd_dt_bias, ssd_a_log, ssd_d, ssd_norm_g, lru_conv_w, lru_conv_b, lru_w_a, lru_b_a, lru_w_x, lru_b_x, lru_lambda, lru_norm_g, fox_b_f, fox_norm_g, w_out, norm2_g, w_gate, w_up, w_down, norm3_g, w_ple_gate, b_ple_gate, w_ple_proj, final_norm_g, loss_target, m_norm1_g, m_w_in, m_ssd_conv_w, m_ssd_conv_b, m_ssd_dt_bias, m_ssd_a_log, m_ssd_d, m_ssd_norm_g, m_lru_conv_w, m_lru_conv_b, m_lru_w_a, m_lru_b_a, m_lru_w_x, m_lru_b_x, m_lru_lambda, m_lru_norm_g, m_fox_b_f, m_fox_norm_g, m_w_out, m_norm2_g, m_w_gate, m_w_up, m_w_down, m_norm3_g, m_w_ple_gate, m_b_ple_gate, m_w_ple_proj, m_final_norm_g, v_norm1_g, v_w_in, v_ssd_conv_w, v_ssd_conv_b, v_ssd_dt_bias, v_ssd_a_log, v_ssd_d, v_ssd_norm_g, v_lru_conv_w, v_lru_conv_b, v_lru_w_a, v_lru_b_a, v_lru_w_x, v_lru_b_x, v_lru_lambda, v_lru_norm_g, v_fox_b_f, v_fox_norm_g, v_w_out, v_norm2_g, v_w_gate, v_w_up, v_w_down, v_norm3_g, v_w_ple_gate, v_b_ple_gate, v_w_ple_proj, v_final_norm_g):
    args = (norm1_g, w_in, ssd_conv_w, ssd_conv_b, ssd_dt_bias, ssd_a_log, ssd_d, ssd_norm_g, lru_conv_w, lru_conv_b, lru_w_a, lru_b_a, lru_w_x, lru_b_x, lru_lambda, lru_norm_g, fox_b_f, fox_norm_g, w_out, norm2_g, w_gate, w_up, w_down, norm3_g, w_ple_gate, b_ple_gate, w_ple_proj, final_norm_g)
    m_args = (m_norm1_g, m_w_in, m_ssd_conv_w, m_ssd_conv_b, m_ssd_dt_bias, m_ssd_a_log, m_ssd_d, m_ssd_norm_g, m_lru_conv_w, m_lru_conv_b, m_lru_w_a, m_lru_b_a, m_lru_w_x, m_lru_b_x, m_lru_lambda, m_lru_norm_g, m_fox_b_f, m_fox_norm_g, m_w_out, m_norm2_g, m_w_gate, m_w_up, m_w_down, m_norm3_g, m_w_ple_gate, m_b_ple_gate, m_w_ple_proj, m_final_norm_g)
    v_args = (v_norm1_g, v_w_in, v_ssd_conv_w, v_ssd_conv_b, v_ssd_dt_bias, v_ssd_a_log, v_ssd_d, v_ssd_norm_g, v_lru_conv_w, v_lru_conv_b, v_lru_w_a, v_lru_b_a, v_lru_w_x, v_lru_b_x, v_lru_lambda, v_lru_norm_g, v_fox_b_f, v_fox_norm_g, v_w_out, v_norm2_g, v_w_gate, v_w_up, v_w_down, v_norm3_g, v_w_ple_gate, v_b_ple_gate, v_w_ple_proj, v_final_norm_g)
    w = dict(zip(WEIGHTS, args))
    mom = dict(zip(WEIGHTS, m_args))
    var = dict(zip(WEIGHTS, v_args))
    xi, yi, ci = _pos()
    chip = 2 * xi + yi

    big_names = list(BIG)
    later = [n for n in big_names if n != "w_in"]
    by_rows = {n: BIG[n] == 1 or n in TRANSPOSED for n in big_names}
    wb = {n: (jnp.transpose(w[n], (0, 2, 1)) if n in TRANSPOSED else w[n]).astype(BF16) for n in big_names}
    conv_full = dict(zip(SHARDED_SMALL, _gather_shards("gather_conv", [w[n] for n in SHARDED_SMALL],
                                                       list(SHARDED_SMALL.values()), F32)))
    carried = Carried()

    def layer_weights(l, slots_of):
        def assemble(n):
            s4 = slots_of(n)
            return (s4.reshape(-1, s4.shape[-1]) if by_rows[n]
                    else jnp.concatenate([s4[k] for k in range(N_CHIP)], axis=1))

        f = LazyDict({n: (conv_full[n][l] if n in conv_full else w[n][l]) for n in SMALL if n != "final_norm_g"})
        f.update({(n + "_t" if n in TRANSPOSED else n): functools.partial(assemble, n) for n in big_names})
        return prep_layer(f)

    w_in0 = run_exchange("gather0_in_pass_on", gather_pass_on(
        run_exchange("gather0_in_spread", gather_spread([wb["w_in"]], 0))))
    n_early = 2
    carried.offer("l0_mm_in", lambda: gather_spread([wb[n] for n in later[:n_early]], 0))
    carried.offer("l0_ssd_fwd", lambda: gather_spread([wb[n] for n in later[n_early:]], 0))
    carried.offer("l0_fox_fwd", lambda: combine(
        gather_pass_on(carried.results["l0_mm_in"] + carried.results["l0_ssd_fwd"]),
        gather_spread([wb[n] for n in big_names], 1)))
    carried.offer("l0_mm_gu", lambda: gather_pass_on(carried.results["l0_fox_fwd"][len(later):]))
    layers = [
        layer_weights(0, lambda n: w_in0[0] if n == "w_in" else carried.results["l0_fox_fwd"][later.index(n)]),
        lambda: layer_weights(1, lambda n: carried.results["l0_mm_gu"][big_names.index(n)])]

    def chip_slices(a, n):
        return a.reshape(N_CHIP, -1, a.shape[1]) if by_rows[n] else jnp.stack(jnp.split(a, N_CHIP, axis=1))

    def reduce_start(tag, names, full_grads):
        keep, give = [], []
        for n, a in zip(names, full_grads):
            s4 = chip_slices(a, n)
            h = s4.shape[1] // 2
            keep.append(lax.dynamic_slice_in_dim(s4, ci * h, h, 1))
            give.append(lax.dynamic_slice_in_dim(s4, (1 - ci) * h, h, 1).astype(BF16))
        got = swap_with_sibling(f"swap_halves{tag}", give)
        return [add_slices(f"add_sibling{tag}_{n}", k_, g_, BF16) for n, k_, g_ in zip(names, keep, got)]

    kernel_key = dict(w_out="wout", w_gate="wg", w_up="wu", w_down="wd", w_ple_gate="wpg", w_ple_proj="wpp")
    gl, parts = [None] * DEPTH, {}

    def on_early_grads(l, g_layer):
        if l == 0:
            parts["0_later"] = reduce_start("0_later", later, [g_layer[kernel_key[n]] for n in later])
            carried.offer("l0_ssd_bwd", lambda: chips_exchange(parts["0_later"]))

    def on_layer_grads(l, g_layer):
        gl[l] = unprep_grads(g_layer)
        if l == 1:
            parts["1"] = reduce_start("1", big_names, [gl[1]["w_in"]] + [g_layer[kernel_key[n]] for n in later])
            carried.offer("l0_fox_bwd", lambda: chips_exchange(parts["1"]))
        else:
            parts["0_in"] = reduce_start("0_in", ["w_in"], [gl[0]["w_in"]])

    loss, grad_x, grads, g_final = local_step(x, p, loss_target, layers, final_norm_g, carried, on_layer_grads,
                                              on_early_grads)
    loss = lax.psum(loss, ("x", "y", "c"))
    arrived0 = dict(zip(later, carried.results["l0_ssd_bwd"]))
    arrived0["w_in"] = run_exchange("a2a_chips0_in", chips_exchange(parts["0_in"]))[0]
    arrived = [[arrived0[n] for n in big_names], carried.results["l0_fox_bwd"]]

    gsmall = {n: jnp.stack([gl[l][n] for l in range(DEPTH)]) for n in SMALL if n != "final_norm_g"}
    gsmall["final_norm_g"] = g_final
    small_shapes = [gsmall[n].shape for n in SMALL]
    gs = _pack([gsmall[n] for n in SMALL], 8)
    gs = sum_slices("sum_small", all_gather8("gather_small_grads", gs))
    gsum = dict(zip(SMALL, _unpack(gs, small_shapes)))
    for n, ax in SHARDED_SMALL.items():
        k = gsum[n].shape[ax] // N_CHIP
        gsum[n] = lax.dynamic_slice_in_dim(gsum[n], chip * k, k, ax)

    done = []
    for l in range(DEPTH):
        mine = [sum_slices(f"sum_chips{l}_{n}", a_) for n, a_ in zip(big_names, arrived[l])]
        other = swap_with_sibling(f"swap_results{l}", mine)
        done.append([jnp.concatenate([jnp.where(ci == 0, m_, o_), jnp.where(ci == 0, o_, m_)], axis=0)
                     for m_, o_ in zip(mine, other)])
    gview = {}
    for t, n in enumerate(big_names):
        g2 = jnp.stack([done[l][t] for l in range(DEPTH)])
        if n in TRANSPOSED:
            gview[n], gsum[n] = g2, jnp.transpose(g2, (0, 2, 1))
        else:
            gsum[n] = g2

    delta, new_m, new_v = {}, {}, {}
    for n in big_names:
        if n in ADAM_VIEW:
            to_view, back = ADAM_VIEW[n]
            gv = gview[n] if n in gview else jnp.transpose(gsum[n], to_view)
            outs = adamw(f"adamw_{n}", jnp.transpose(w[n], to_view), gv, jnp.transpose(mom[n], to_view),
                         jnp.transpose(var[n], to_view))
            delta[n], new_m[n], new_v[n] = [jnp.transpose(o, back) for o in outs]
        else:
            delta[n], new_m[n], new_v[n] = adamw(f"adamw_{n}", w[n], gsum[n], mom[n], var[n])
    shapes = [w[n].shape for n in SMALL]
    pk = lambda d: _pack([d[n] for n in SMALL], 8)[None]
    ds, ms, vs = adamw("adamw_small", pk(w), pk(gsum), pk(mom), pk(var))
    for d, packed in ((delta, ds), (new_m, ms), (new_v, vs)):
        d.update(zip(SMALL, _unpack(packed[0], shapes)))

    return (loss, grad_x, *[gsum[n] for n in WEIGHTS], *[delta[n] for n in WEIGHTS],
            *[new_m[n] for n in WEIGHTS], *[new_v[n] for n in WEIGHTS])
```

```python
import functools
import math

import jax
import jax.numpy as jnp
import numpy as np
from jax import lax
from jax.experimental import pallas as pl
from jax.experimental.pallas import tpu as pltpu

F32, BF16 = jnp.float32, jnp.bfloat16
SDS = jax.ShapeDtypeStruct

D_MODEL = 1024
DEPTH = 2
HEAD_DIM = 64
N_HEADS = 6
SSD_W, LRU_W, FOX_W = 384, 256, 384
D_FF = 2816
PLE_DIM = 256
IN_COLS = 2956
EPS = 1e-6
LRU_C = 8.0
LANE = 128
V7X_VMEM_LIMIT = 56 * 1024 * 1024

PW = 3072
OFF_B, OFF_C, OFF_LX, OFF_LG, OFF_SM, OFF_Z, OFF_XS, OFF_Q, OFF_K, OFF_V = (
    0, 256, 512, 768, 1024, 1152, 1536, 1920, 2304, 2688)
FOX_LANE0 = 8

ADAM_LR, ADAM_B1, ADAM_B2, ADAM_EPS, ADAM_WD, ADAM_STEP = 0.001, 0.9, 0.999, 1e-08, 0.01, 10


def _iota(shape, dim):
    return lax.broadcasted_iota(jnp.int32, shape, dim)


class Carried:
    def __init__(self):
        self.offers, self.results = {}, {}

    def offer(self, call_name, make_exchange):
        self.offers[call_name] = make_exchange

    def take(self, call_name):
        make = self.offers.pop(call_name, None)
        return None if make is None else make()

    def deliver(self, call_name, results):
        self.results[call_name] = results


class LazyDict(dict):
    def __getitem__(self, key):
        v = dict.__getitem__(self, key)
        if callable(v):
            v = v()
            dict.__setitem__(self, key, v)
        return v


def _run(call, args, name, comm, carried):
    if comm is None:
        return call(*args)
    own, brought = call(*args)
    carried.deliver(name, brought)
    return own


def _pc(body, *, name, grid, in_specs, out_specs, out_shape, scratch=(), sem=None, comm=None):
    if comm is None:
        return pl.pallas_call(
            body, name=name, grid=grid, in_specs=in_specs, out_specs=out_specs, out_shape=out_shape,
            scratch_shapes=list(scratch),
            compiler_params=pltpu.CompilerParams(dimension_semantics=sem, vmem_limit_bytes=V7X_VMEM_LIMIT))
    single = not isinstance(out_shape, (list, tuple))
    out_specs_l = [out_specs] if single else list(out_specs)
    out_shape_l = [out_shape] if single else list(out_shape)
    n_in, n_out, n_scr, n_ci, n_co = len(in_specs), len(out_shape_l), len(scratch), len(comm.inputs), len(comm.out_shapes)

    def hosted(*refs):
        ins, cins = refs[:n_in], refs[n_in:n_in + n_ci]
        outs, couts = refs[n_in + n_ci:n_in + n_ci + n_out], refs[n_in + n_ci + n_out:n_in + n_ci + n_out + n_co]
        rest = refs[n_in + n_ci + n_out + n_co:]
        scr, csems = rest[:n_scr], rest[n_scr:]
        ids = [pl.program_id(d) for d in range(len(grid))]
        first = functools.reduce(jnp.logical_and, [i == 0 for i in ids])
        last = functools.reduce(jnp.logical_and, [i == g - 1 for i, g in zip(ids, grid)])

        @pl.when(first)
        def _():
            comm.start(cins, couts, csems)

        body(*ins, *outs, *scr)

        @pl.when(last)
        def _():
            comm.wait(cins, couts, csems)

    call = pl.pallas_call(
        hosted, name=name, grid=grid, in_specs=list(in_specs) + [ANY] * n_ci,
        out_specs=out_specs_l + [ANY] * n_co, out_shape=out_shape_l + list(comm.out_shapes),
        scratch_shapes=list(scratch) + list(comm.sems),
        input_output_aliases={n_in + a: n_out + b for a, b in comm.aliases.items()},
        compiler_params=pltpu.CompilerParams(dimension_semantics=("arbitrary",) * len(grid),
                                             vmem_limit_bytes=V7X_VMEM_LIMIT))

    def run(*args):
        res = call(*args, *comm.inputs)
        own = res[:n_out]
        return (own[0] if single else own), list(res[n_out:])

    return run


def permute_in_cols(w):
    z = lambda n: jnp.zeros(w.shape[:-1] + (n,), w.dtype)
    s = lambda a, b: w[..., a:b]
    return jnp.concatenate([
        s(768, 1024), s(1024, 1280), s(1286, 1542), s(1542, 1798),
        s(1280, 1286), z(2), s(2950, 2956), z(LANE - 14),
        s(0, 384), s(384, 768), s(1798, 2182), s(2182, 2566), s(2566, 2950)], axis=-1)


def unpermute_in_cols(g):
    s = lambda a, n: g[..., a:a + n]
    return jnp.concatenate([
        s(OFF_Z, 384), s(OFF_XS, 384), s(OFF_B, 256), s(OFF_C, 256), s(OFF_SM, 6),
        s(OFF_LX, 256), s(OFF_LG, 256), s(OFF_Q, 384), s(OFF_K, 384), s(OFF_V, 384),
        s(OFF_SM + FOX_LANE0, 6)], axis=-1)


def _pick(n, cands):
    for c in cands:
        if n % c == 0:
            return c
    return n


def mm(a, b, *, name, ta=False, tb=False, out_dtypes=(F32,), extras=(), col_params=(), partials=0, epilogue=None,
       tm=None, tn=None, tk=None, carried=None):
    bs = list(b) if isinstance(b, (list, tuple)) else [b]
    pair_sum = isinstance(a, (list, tuple))
    a_list = list(a) if pair_sum else [a]
    assert not pair_sum or len(a_list) == len(bs)
    a = a_list[0]
    n_a = len(a_list)
    n_acc = 1 if pair_sum else len(bs)
    extras = [e if isinstance(e, tuple) else (e, 0) for e in extras]
    M = a.shape[1] if ta else a.shape[0]
    K = a.shape[0] if ta else a.shape[1]
    N = bs[0].shape[0] if tb else bs[0].shape[1]
    tm = tm or _pick(M, (1024, 1408, 512, 256, 128))
    tn = tn or _pick(N, (1024, 1408, 768, 512, 256, 128))
    tk = tk or _pick(K, (1024, 1408, 512, 256, 128))
    nm, nn, nk = M // tm, N // tn, K // tk
    n_b, n_ex, n_cp, n_out = len(bs), len(extras), len(col_params), len(out_dtypes)
    a_bytes, b_bytes = n_a * M * K * a.dtype.itemsize, n_b * K * N * bs[0].dtype.itemsize
    rows_inner = a_bytes * nn + b_bytes <= a_bytes + b_bytes * nm

    def ij(g0, g1):
        return (g1, g0) if rows_inner else (g0, g1)

    def body(*rest):
        a_refs, rest = rest[:n_a], rest[n_a:]
        b_refs, rest = rest[:n_b], rest[n_b:]
        in_refs, rest = rest[:n_ex + n_cp], rest[n_ex + n_cp:]
        out_refs, accs = rest[:n_out + partials], rest[n_out + partials:]
        dn = (((0 if ta else 1,), (1 if tb else 0,)), ((), ()))
        dot = lambda x_ref, y_ref: lax.dot_general(x_ref[...].astype(BF16), y_ref[...].astype(BF16), dn,
                                                   preferred_element_type=F32)
        if pair_sum:
            parts = [functools.reduce(lambda u, v: u + v, [dot(x, y) for x, y in zip(a_refs, b_refs)])]
        else:
            parts = [dot(a_refs[0], b_ref) for b_ref in b_refs]

        def finish(rs):
            outs = epilogue(*rs, *[e[...] for e in in_refs]) if epilogue is not None else tuple(rs)
            for o_ref, o in zip(out_refs[:n_out], outs):
                o_ref[...] = o.astype(o_ref.dtype)
            for o_ref, o in zip(out_refs[n_out:], outs[n_out:]):
                o_ref[0] = o

        if nk == 1:
            finish(parts)
            return
        k = pl.program_id(2)

        @pl.when(k == 0)
        def _():
            for acc, part in zip(accs, parts):
                acc[...] = part

        @pl.when(k > 0)
        def _():
            for acc, part in zip(accs, parts):
                acc[...] += part

        @pl.when(k == nk - 1)
        def _():
            finish([acc[...] for acc in accs])

    def a_map(g0, g1, k):
        i, _ = ij(g0, g1)
        return (k, i) if ta else (i, k)

    def b_map(g0, g1, k):
        _, j = ij(g0, g1)
        return (j, k) if tb else (k, j)

    def ex_map(off, g0, g1, k):
        i, j = ij(g0, g1)
        return (i, j + off)

    a_spec = pl.BlockSpec((tk, tm) if ta else (tm, tk), a_map)
    b_spec = pl.BlockSpec((tn, tk) if tb else (tk, tn), b_map)
    mn_spec = pl.BlockSpec((tm, tn), functools.partial(ex_map, 0))
    comm = carried.take(name) if carried is not None else None
    call = _pc(body, name=name, grid=(nn, nm, nk) if rows_inner else (nm, nn, nk),
               in_specs=([a_spec] * n_a + [b_spec] * n_b
                         + [pl.BlockSpec((tm, tn), functools.partial(ex_map, off)) for _, off in extras]
                         + [pl.BlockSpec((1, tn), lambda g0, g1, k: (0, ij(g0, g1)[1]))] * n_cp),
               out_specs=([mn_spec] * n_out
                          + [pl.BlockSpec((1, 1, tn), lambda g0, g1, k: (ij(g0, g1)[0], 0, ij(g0, g1)[1]))] * partials),
               out_shape=[SDS((M, N), dt) for dt in out_dtypes] + [SDS((nm, 1, N), F32)] * partials,
               scratch=[pltpu.VMEM((tm, tn), F32)] * n_acc if nk > 1 else [],
               sem=("parallel", "parallel", "arbitrary"), comm=comm)
    outs = _run(call, (*a_list, *bs, *[e for e, _ in extras], *col_params), name, comm, carried)
    return outs[0] if len(outs) == 1 else outs


def rowwise(name, fn, rows, params, row_outs, acc_outs=(), tr=512):
    rows = [r if isinstance(r, tuple) else (r, 0, r.shape[1]) for r in rows]
    T = rows[0][0].shape[0]
    tr = min(tr, T)
    n_in, n_ro, n_ac = len(rows) + len(params), len(row_outs), len(acc_outs)

    def body(*refs):
        ins, outs = refs[:n_in], refs[n_in:]
        res = fn(*[r[...] for r in ins])
        if not isinstance(res, (tuple, list)):
            res = (res,)
        for k in range(n_ro):
            outs[k][...] = res[k].astype(outs[k].dtype)
        if n_ac:
            i = pl.program_id(0)

            @pl.when(i == 0)
            def _():
                for k in range(n_ac):
                    outs[n_ro + k][...] = res[n_ro + k]

            @pl.when(i > 0)
            def _():
                for k in range(n_ac):
                    outs[n_ro + k][...] += res[n_ro + k]

    in_specs = ([pl.BlockSpec((tr, w), functools.partial(lambda cb, i: (i, cb), cb)) for (_, cb, w) in rows]
                + [pl.BlockSpec(p.shape, lambda i: (0, 0)) for p in params])
    out_specs = ([pl.BlockSpec((tr, c), lambda i: (i, 0)) for (c, _) in row_outs]
                 + [pl.BlockSpec((1, c), lambda i: (0, 0)) for c in acc_outs])
    out_shape = [SDS((T, c), dt) for (c, dt) in row_outs] + [SDS((1, c), F32) for c in acc_outs]
    outs = _pc(body, name=name, grid=(T // tr,), in_specs=in_specs, out_specs=out_specs, out_shape=out_shape,
               sem=("arbitrary",) if n_ac else ("parallel",))(*[r[0] for r in rows], *params)
    return outs[0] if len(outs) == 1 else outs


def _rms(x, g):
    return x * lax.rsqrt(jnp.mean(x * x, axis=-1, keepdims=True) + EPS) * g


def _softplus(x):
    return jnp.maximum(x, 0.0) + jnp.log(1.0 + jnp.exp(-jnp.abs(x)))


def _silu(x):
    return x * jax.nn.sigmoid(x)


def _gelu(x):
    return 0.5 * x * (1.0 + jnp.tanh(math.sqrt(2.0 / math.pi) * (x + 0.044715 * (x * x * x))))


def _neg_expm1(x):
    series = x * (1 + x / 2 * (1 + x / 3 * (1 + x / 4 * (1 + x / 5 * (1 + x / 6 * (1 + x / 7))))))
    return -jnp.where(jnp.abs(x) < 0.3, series, jnp.exp(x) - 1.0)


def _swiglu(gu):
    return _silu(gu[:, :D_FF]) * gu[:, D_FF:]


def _ple(pg, pp, b):
    return jax.nn.sigmoid(pg + b) * pp


def _ssd_elt(small, xs_act, dtbias_row, alog_row):
    lane = _iota(small.shape, 1)
    dt = _softplus(small + dtbias_row)
    adt = jnp.where(lane < N_HEADS, -jnp.exp(alog_row) * dt, 0.0)
    head = _iota(xs_act.shape, 1) // HEAD_DIM
    dt_exp = jnp.zeros_like(xs_act)
    for h in range(N_HEADS):
        dth = jnp.sum(jnp.where(lane == h, dt, 0.0), axis=1, keepdims=True)
        dt_exp = dt_exp + jnp.where(head == h, dth, 0.0)
    return adt, xs_act * dt_exp


def _fox_elt(small, bf_row):
    lane = _iota(small.shape, 1)
    keep = (lane >= FOX_LANE0) & (lane < FOX_LANE0 + N_HEADS)
    return jnp.where(keep, -_softplus(-(small + bf_row)), 0.0)


def _lru_elt(xl, pre, b_ax, lam):
    r = jax.nn.sigmoid(pre[:, :LRU_W] + b_ax[:, :LRU_W])
    i = jax.nn.sigmoid(pre[:, LRU_W:] + b_ax[:, LRU_W:])
    log_a = -LRU_C * r * _softplus(-lam)
    a = jnp.exp(log_a)
    mult = jnp.sqrt(_neg_expm1(2.0 * log_a))
    return a, mult * (i * xl)


def _mix_post(yraw, xs_act, z, hl, lgate, yfox, dexp, g_ssd, g_lru, g_fox):
    y_ssd = _rms((yraw + xs_act * dexp) * _silu(z), g_ssd)
    y_lru = _rms(hl * _gelu(lgate), g_lru)
    y_fox = _rms(yfox, g_fox)
    return jnp.concatenate([y_ssd, y_lru, y_fox], axis=-1)


def _colsum(x):
    return jnp.sum(x, axis=0, keepdims=True)


def _shift_down(x, d):
    if d == 0:
        return x
    return jnp.where(_iota(x.shape, 0) >= d, pltpu.roll(x, d, 0), 0.0)


def _shift_up(x, d):
    if d == 0:
        return x
    s = x.shape[0]
    return jnp.where(_iota(x.shape, 0) < s - d, pltpu.roll(x, s - d, 0), 0.0)


def _conv_core(x, w, b):
    y = b + w[3:4, :] * x
    for k in range(3):
        y = y + w[k:k + 1, :] * _shift_down(x, 3 - k)
    return y


def seq_conv(name, src, col, width, w8, b, *, batch, silu, out_dtype):
    T = src.shape[0]
    S = T // batch
    c0 = col // LANE

    def body(x_ref, w_ref, b_ref, o_ref):
        y = _conv_core(x_ref[...], w_ref[...], b_ref[...])
        o_ref[...] = (_silu(y) if silu else y).astype(o_ref.dtype)

    return _pc(body, name=name, grid=(batch, width // LANE),
               in_specs=[pl.BlockSpec((S, LANE), lambda bi, ci: (bi, c0 + ci)),
                         pl.BlockSpec((8, LANE), lambda bi, ci: (0, ci)),
                         pl.BlockSpec((1, LANE), lambda bi, ci: (0, ci))],
               out_specs=pl.BlockSpec((S, LANE), lambda bi, ci: (bi, ci)),
               out_shape=SDS((T, width), out_dtype), sem=("parallel", "parallel"))(src, w8, b)


def seq_conv_bwd(name, src, col, width, w8, b, dy, *, batch, silu):
    T = src.shape[0]
    S = T // batch
    c0 = col // LANE

    def body(x_ref, w_ref, b_ref, dy_ref, dx_ref, dw_ref, db_ref):
        x, w = x_ref[...], w_ref[...]
        dpre = dy_ref[...].astype(F32)
        if silu:
            dpre = jax.vjp(_silu, _conv_core(x, w, b_ref[...]))[1](dpre)[0]
        dx = w[3:4, :] * dpre
        for k in range(3):
            dx = dx + w[k:k + 1, :] * _shift_up(dpre, 3 - k)
        dx_ref[...] = dx.astype(dx_ref.dtype)
        row8 = _iota((8, LANE), 0)
        dw = jnp.zeros((8, LANE), F32)
        for k in range(4):
            dw = dw + jnp.where(row8 == k, _colsum(dpre * _shift_down(x, 3 - k)), 0.0)
        db = _colsum(dpre)
        bi = pl.program_id(1)

        @pl.when(bi == 0)
        def _():
            dw_ref[...] = dw
            db_ref[...] = db

        @pl.when(bi > 0)
        def _():
            dw_ref[...] += dw
            db_ref[...] += db

    return _pc(body, name=name, grid=(width // LANE, batch),
               in_specs=[pl.BlockSpec((S, LANE), lambda ci, bi: (bi, c0 + ci)),
                         pl.BlockSpec((8, LANE), lambda ci, bi: (0, ci)),
                         pl.BlockSpec((1, LANE), lambda ci, bi: (0, ci)),
                         pl.BlockSpec((S, LANE), lambda ci, bi: (bi, ci))],
               out_specs=[pl.BlockSpec((S, LANE), lambda ci, bi: (bi, ci)),
                          pl.BlockSpec((8, LANE), lambda ci, bi: (0, ci)),
                          pl.BlockSpec((1, LANE), lambda ci, bi: (0, ci))],
               out_shape=[SDS((T, width), BF16), SDS((8, width), F32), SDS((1, width), F32)],
               sem=("parallel", "arbitrary"))(src, w8, b, dy)


def _split3_dot(tri, x):
    hi = x.astype(BF16)
    r1 = x - hi.astype(F32)
    mid = r1.astype(BF16)
    lo = (r1 - mid.astype(F32)).astype(BF16)
    d = lambda v: jnp.dot(tri, v, preferred_element_type=F32)
    return d(hi) + d(mid) + d(lo)


def seq_cumsum(name, x, *, batch, reverse=False, nsum=1, trow=None):
    T = x.shape[0]
    S = T // batch
    ch = min(256, S)
    nch = S // ch

    def body(x_ref, o_ref, *maybe_t):
        r, c = _iota((ch, ch), 0), _iota((ch, ch), 1)
        tri = jnp.where((c >= r) if reverse else (c <= r), 1.0, 0.0).astype(BF16)
        carry = jnp.zeros((1, LANE), F32)
        for k in (range(nch - 1, -1, -1) if reverse else range(nch)):
            xc = x_ref[k * ch:(k + 1) * ch, 0:LANE]
            for m in range(1, nsum):
                xc = xc + x_ref[k * ch:(k + 1) * ch, m * LANE:(m + 1) * LANE]
            o_ref[k * ch:(k + 1) * ch, :] = _split3_dot(tri, xc) + carry
            carry = carry + _colsum(xc)
        if trow is not None:
            maybe_t[0][...] = o_ref[...].T[trow:trow + 8, :]

    out_specs = [pl.BlockSpec((S, LANE), lambda bi: (bi, 0))]
    out_shape = [SDS((T, LANE), F32)]
    if trow is not None:
        out_specs.append(pl.BlockSpec((8, S), lambda bi: (bi, 0)))
        out_shape.append(SDS((batch * 8, S), F32))
    outs = _pc(body, name=name, grid=(batch,), in_specs=[pl.BlockSpec((S, LANE * nsum), lambda bi: (bi, 0))],
               out_specs=out_specs, out_shape=out_shape, sem=("parallel",))(x)
    return outs if trow is not None else outs[0]


_SCAN_SEQS = 2


def lru_scan(name, a, u, *, batch):
    T, W = a.shape
    S = T // batch
    nb = _SCAN_SEQS if batch % _SCAN_SEQS == 0 else 1

    def body(a_ref, u_ref, h_ref):
        row = _iota((8, W), 0)

        def step(g, hs):
            new = []
            for q, h in enumerate(hs):
                off = pl.multiple_of(q * S + g * 8, 8)
                at, ut = a_ref[pl.ds(off, 8), :], u_ref[pl.ds(off, 8), :]
                acc = jnp.zeros((8, W), F32)
                for r in range(8):
                    h = at[r:r + 1, :] * h + ut[r:r + 1, :]
                    acc = jnp.where(row == r, jnp.broadcast_to(h, (8, W)), acc)
                h_ref[pl.ds(off, 8), :] = acc
                new.append(h)
            return tuple(new)

        lax.fori_loop(0, S // 8, step, tuple(jnp.zeros((1, W), F32) for _ in range(nb)))

    spec = pl.BlockSpec((nb * S, W), lambda bi: (bi, 0))
    return _pc(body, name=name, grid=(batch // nb,), in_specs=[spec, spec], out_specs=spec,
               out_shape=SDS((T, W), F32), sem=("parallel",))(a, u)


def lru_scan_bwd(name, a, h, dh, *, batch):
    T, W = a.shape
    S = T // batch
    ng = S // 8
    nb = _SCAN_SEQS if batch % _SCAN_SEQS == 0 else 1

    def body(a_ref, h_ref, dh_ref, da_ref, du_ref):
        row = _iota((8, W), 0)

        def step(k, cs):
            g_idx = ng - 1 - k
            new = []
            for q, c in enumerate(cs):
                off = pl.multiple_of(q * S + g_idx * 8, 8)
                offp = pl.multiple_of(q * S + jnp.maximum(g_idx - 1, 0) * 8, 8)
                at, ht, dt = a_ref[pl.ds(off, 8), :], h_ref[pl.ds(off, 8), :], dh_ref[pl.ds(off, 8), :]
                hp = jnp.where(g_idx > 0, h_ref[pl.ds(offp, 8), :], 0.0)
                da = jnp.zeros((8, W), F32)
                du = jnp.zeros((8, W), F32)
                for r in range(7, -1, -1):
                    g = dt[r:r + 1, :] + c
                    hprev = ht[r - 1:r, :] if r > 0 else hp[7:8, :]
                    du = jnp.where(row == r, jnp.broadcast_to(g, (8, W)), du)
                    da = jnp.where(row == r, jnp.broadcast_to(g * hprev, (8, W)), da)
                    c = at[r:r + 1, :] * g
                da_ref[pl.ds(off, 8), :] = da
                du_ref[pl.ds(off, 8), :] = du
                new.append(c)
            return tuple(new)

        lax.fori_loop(0, ng, step, tuple(jnp.zeros((1, W), F32) for _ in range(nb)))

    spec = pl.BlockSpec((nb * S, W), lambda bi: (bi, 0))
    return _pc(body, name=name, grid=(batch // nb,), in_specs=[spec] * 3, out_specs=[spec] * 2,
               out_shape=[SDS((T, W), F32)] * 2, sem=("parallel",))(a, h, dh)


def _nt(a, b):
    return lax.dot_general(a, b, (((1,), (1,)), ((), ())), preferred_element_type=F32)


def _tn(a, b):
    return lax.dot_general(a, b, (((0,), (0,)), ((), ())), preferred_element_type=F32)


def _tile(S, t=256):
    return min(t, S)


def ssd_attn_fwd(name, cm, bm, xd, cum, cum_t, *, batch, carried=None):
    T = cm.shape[0]
    S = T // batch
    tq = tk = _tile(S)
    nq = S // tq

    def body(c_ref, b_ref, x_ref, cum_ref, cumt_ref, y_ref):
        i = pl.program_id(1)
        cq, cmq = cum_ref[...], c_ref[...]
        rowi, coli = _iota((tq, tk), 0), _iota((tq, tk), 1)
        half = _iota((tk, LANE), 1) // HEAD_DIM

        def step(j, accs, diag):
            off = pl.multiple_of(j * tk, tk)
            bj = b_ref[pl.ds(off, tk), :]
            gm = [_nt(cmq[:, g * LANE:(g + 1) * LANE], bj[:, g * LANE:(g + 1) * LANE]) for g in range(2)]
            ckt = cumt_ref[:, pl.ds(off, tk)]
            new = []
            for p in range(3):
                xp = x_ref[pl.ds(off, tk), p * LANE:(p + 1) * LANE]
                ws, xs = [], []
                for hh in range(2):
                    h = 2 * p + hh
                    seg = cq[:, h:h + 1] - ckt[h:h + 1, :]
                    e = jnp.exp(jnp.where(rowi >= coli, seg, -jnp.inf) if diag else seg)
                    ws.append((gm[h // 3] * e).astype(BF16))
                    xs.append(jnp.where(half == hh, xp, jnp.zeros_like(xp)))
                new.append(accs[p] + jnp.dot(jnp.concatenate(ws, axis=1), jnp.concatenate(xs, axis=0),
                                             preferred_element_type=F32))
            return tuple(new)

        accs = lax.fori_loop(0, i, functools.partial(step, diag=False),
                             tuple(jnp.zeros((tq, LANE), F32) for _ in range(3)))
        accs = step(i, accs, True)
        y_ref[...] = jnp.concatenate(accs, axis=1)

    comm = carried.take(name) if carried is not None else None
    call = _pc(body, name=name, grid=(batch, nq),
               in_specs=[pl.BlockSpec((tq, 256), lambda b, i: (b * nq + i, 0)),
                         pl.BlockSpec((S, 256), lambda b, i: (b, 0)),
                         pl.BlockSpec((S, SSD_W), lambda b, i: (b, 0)),
                         pl.BlockSpec((tq, LANE), lambda b, i: (b * nq + i, 0)),
                         pl.BlockSpec((8, S), lambda b, i: (b, 0))],
               out_specs=pl.BlockSpec((tq, SSD_W), lambda b, i: (b * nq + i, 0)),
               out_shape=SDS((T, SSD_W), F32), sem=("parallel", "parallel"), comm=comm)
    return _run(call, (cm, bm, xd, cum, cum_t), name, comm, carried)


def ssd_attn_bwd(name, cm, bm, xd, cum, cum_t, dy, *, batch, carried=None):
    T = cm.shape[0]
    S = T // batch
    tq = tk = _tile(S, 512)
    nq = S // tq

    def body(c_ref, b_ref, x_ref, cum_ref, cumt_ref, dy_ref, dx_ref, db_ref, dc_ref, dcum_ref, dcumt_ref):
        dx_ref[...] = jnp.zeros_like(dx_ref)
        db_ref[...] = jnp.zeros_like(db_ref)
        dcum_ref[...] = jnp.zeros_like(dcum_ref)
        dcumt_ref[...] = jnp.zeros_like(dcumt_ref)
        rowi, coli = _iota((tq, tk), 0), _iota((tq, tk), 1)
        halfq = _iota((tq, LANE), 1) // HEAD_DIM
        lane_q = _iota((tq, LANE), 1)

        def qblock(i, _):
            qoff = pl.multiple_of(i * tq, tq)
            cq = cum_ref[pl.ds(qoff, tq), :]
            cmq = c_ref[pl.ds(qoff, tq), :]
            dyq = dy_ref[pl.ds(qoff, tq), :]
            dyh = [[jnp.where(halfq == hh, dyq[:, p * LANE:(p + 1) * LANE], 0.0).astype(BF16) for hh in range(2)]
                   for p in range(3)]

            def step(j, carry, diag):
                dcq, rs_acc = carry
                off = pl.multiple_of(j * tk, tk)
                bj = b_ref[pl.ds(off, tk), :]
                gm = [_nt(cmq[:, g * LANE:(g + 1) * LANE], bj[:, g * LANE:(g + 1) * LANE]) for g in range(2)]
                ckt = cumt_ref[:, pl.ds(off, tk)]
                dgm = [jnp.zeros((tq, tk), F32), jnp.zeros((tq, tk), F32)]
                for p in range(3):
                    xp = x_ref[pl.ds(off, tk), p * LANE:(p + 1) * LANE]
                    ws = []
                    for hh in range(2):
                        h = 2 * p + hh
                        seg = cq[:, h:h + 1] - ckt[h:h + 1, :]
                        e = jnp.exp(jnp.where(rowi >= coli, seg, -jnp.inf) if diag else seg)
                        w = gm[h // 3] * e
                        dw = _nt(dyh[p][hh], xp)
                        zz = dw * w
                        rs_acc = rs_acc + jnp.where(lane_q == h, jnp.sum(zz, axis=1, keepdims=True), 0.0)
                        dcumt_ref[h:h + 1, pl.ds(off, tk)] += _colsum(zz)
                        dgm[h // 3] = dgm[h // 3] + dw * e
                        ws.append(w.astype(BF16))
                    dx_ref[pl.ds(off, tk), p * LANE:(p + 1) * LANE] += _tn(
                        jnp.concatenate(ws, axis=0), jnp.concatenate(dyh[p], axis=0))
                new_dcq = []
                for g in range(2):
                    dg = dgm[g].astype(BF16)
                    new_dcq.append(dcq[g] + jnp.dot(dg, bj[:, g * LANE:(g + 1) * LANE], preferred_element_type=F32))
                    db_ref[pl.ds(off, tk), g * LANE:(g + 1) * LANE] += _tn(dg, cmq[:, g * LANE:(g + 1) * LANE])
                return tuple(new_dcq), rs_acc

            carry = lax.fori_loop(
                0, i, functools.partial(step, diag=False),
                ((jnp.zeros((tq, LANE), F32), jnp.zeros((tq, LANE), F32)), jnp.zeros((tq, LANE), F32)))
            dcq, rs_acc = step(i, carry, True)
            dc_ref[pl.ds(qoff, tq), :] = jnp.concatenate(dcq, axis=1)
            dcum_ref[pl.ds(qoff, tq), :] += rs_acc
            return 0

        lax.fori_loop(0, nq, qblock, 0)
        dcum_ref[...] = dcum_ref[...] - dcumt_ref[...].T

    s256 = pl.BlockSpec((S, 256), lambda b: (b, 0))
    s384 = pl.BlockSpec((S, SSD_W), lambda b: (b, 0))
    s128 = pl.BlockSpec((S, LANE), lambda b: (b, 0))
    comm = carried.take(name) if carried is not None else None
    call = _pc(body, name=name, grid=(batch,),
               in_specs=[s256, s256, s384, s128, pl.BlockSpec((8, S), lambda b: (b, 0)), s384],
               out_specs=[s384, s256, s256, s128],
               out_shape=[SDS((T, SSD_W), F32), SDS((T, 256), F32), SDS((T, 256), F32), SDS((T, LANE), F32)],
               scratch=[pltpu.VMEM((LANE, S), F32)], sem=("parallel",), comm=comm)
    return _run(call, (cm, bm, xd, cum, cum_t, dy), name, comm, carried)


NEG_BIG = -1e30


def fox_attn_fwd(name, proj, cum, cum_t, *, batch, carried=None):
    T = proj.shape[0]
    S = T // batch
    tq = tk = _tile(S, 512)
    nq = S // tq
    scale = HEAD_DIM ** -0.5
    qb, kb, vb = OFF_Q // LANE, OFF_K // LANE, OFF_V // LANE

    def body(q_ref, k_ref, v_ref, cum_ref, cumt_ref, o_ref, lse_ref):
        p, i = pl.program_id(1), pl.program_id(2)
        cq = cum_ref[...]
        lane_q = _iota((tq, LANE), 1)
        halfq, halfk = lane_q // HEAD_DIM, _iota((tk, LANE), 1) // HEAD_DIM
        qs = q_ref[...] * scale
        qh = [jnp.where(halfq == hh, qs, 0.0).astype(BF16) for hh in range(2)]
        rowi, coli = _iota((tq, tk), 0), _iota((tq, tk), 1)
        cqh = [jnp.sum(jnp.where(lane_q == FOX_LANE0 + 2 * p + hh, cq, 0.0), axis=1, keepdims=True) for hh in range(2)]
        row8 = _iota((8, tk), 0)

        def step(j, carry, diag):
            ms, ls, acc = carry
            off = pl.multiple_of(j * tk, tk)
            kj = k_ref[pl.ds(off, tk), :].astype(BF16)
            vj = v_ref[pl.ds(off, tk), :].astype(BF16)
            ckt = cumt_ref[:, pl.ds(off, tk)]
            ps, vs, new_m, new_l, alphas = [], [], [], [], []
            for hh in range(2):
                ck = jnp.sum(jnp.where(row8 == 2 * p + hh, ckt, 0.0), axis=0, keepdims=True)
                logits = _nt(qh[hh], kj) + (cqh[hh] - ck)
                if diag:
                    logits = jnp.where(rowi >= coli, logits, -jnp.inf)
                m = jnp.maximum(ms[hh], jnp.max(logits, axis=1, keepdims=True))
                alpha = jnp.exp(ms[hh] - m)
                pr = jnp.exp(logits - m)
                new_m.append(m)
                new_l.append(alpha * ls[hh] + jnp.sum(pr, axis=1, keepdims=True))
                alphas.append(alpha)
                ps.append(pr.astype(BF16))
                vs.append(jnp.where(halfk == hh, vj, jnp.zeros_like(vj)))
            acc = acc * jnp.where(halfq == 0, alphas[0], alphas[1]) + jnp.dot(
                jnp.concatenate(ps, axis=1), jnp.concatenate(vs, axis=0), preferred_element_type=F32)
            return tuple(new_m), tuple(new_l), acc

        init = ((jnp.full((tq, 1), NEG_BIG, F32),) * 2, (jnp.zeros((tq, 1), F32),) * 2, jnp.zeros((tq, LANE), F32))
        ms, ls, acc = step(i, lax.fori_loop(0, i, functools.partial(step, diag=False), init), True)
        o_ref[...] = acc / jnp.where(halfq == 0, ls[0], ls[1])
        lse_ref[...] = (jnp.where(lane_q == 0, ms[0] + jnp.log(ls[0]), 0.0)
                        + jnp.where(lane_q == 1, ms[1] + jnp.log(ls[1]), 0.0))

    comm = carried.take(name) if carried is not None else None
    call = _pc(body, name=name, grid=(batch, 3, nq),
               in_specs=[pl.BlockSpec((tq, LANE), lambda b, p, i: (b * nq + i, qb + p)),
                         pl.BlockSpec((S, LANE), lambda b, p, i: (b, kb + p)),
                         pl.BlockSpec((S, LANE), lambda b, p, i: (b, vb + p)),
                         pl.BlockSpec((tq, LANE), lambda b, p, i: (b * nq + i, 0)),
                         pl.BlockSpec((8, S), lambda b, p, i: (b, 0))],
               out_specs=[pl.BlockSpec((tq, LANE), lambda b, p, i: (b * nq + i, p))] * 2,
               out_shape=[SDS((T, FOX_W), F32)] * 2, sem=("parallel", "parallel", "parallel"), comm=comm)
    return _run(call, (proj, proj, proj, cum, cum_t), name, comm, carried)


def fox_attn_bwd(name, proj, o, do, lse, cum, cum_t, *, batch, carried=None):
    T = proj.shape[0]
    S = T // batch
    tq = tk = _tile(S, 512)
    nq = S // tq
    scale = HEAD_DIM ** -0.5
    qb, kb, vb = OFF_Q // LANE, OFF_K // LANE, OFF_V // LANE

    def body(q_ref, k_ref, v_ref, o_ref, do_ref, lse_ref, cum_ref, cumt_ref,
             dq_ref, dk_ref, dv_ref, dcum_ref, dk_acc, dv_acc, dcumt_ref):
        p = pl.program_id(1)
        dk_acc[...] = jnp.zeros_like(dk_acc)
        dv_acc[...] = jnp.zeros_like(dv_acc)
        dcum_ref[...] = jnp.zeros_like(dcum_ref)
        dcumt_ref[...] = jnp.zeros_like(dcumt_ref)
        lane_q = _iota((tq, LANE), 1)
        halfq, halfk = lane_q // HEAD_DIM, _iota((tk, LANE), 1) // HEAD_DIM
        rowi, coli = _iota((tq, tk), 0), _iota((tq, tk), 1)
        row8 = _iota((8, tk), 0)

        def qblock(i, _):
            qoff = pl.multiple_of(i * tq, tq)
            cq = cum_ref[pl.ds(qoff, tq), :]
            qs = q_ref[pl.ds(qoff, tq), :] * scale
            doq = do_ref[pl.ds(qoff, tq), :]
            lse = lse_ref[pl.ds(qoff, tq), :]
            delta = doq * o_ref[pl.ds(qoff, tq), :]
            qh, doh, cqh, lseh, dlt = [], [], [], [], []
            for hh in range(2):
                qh.append(jnp.where(halfq == hh, qs, 0.0).astype(BF16))
                doh.append(jnp.where(halfq == hh, doq, 0.0).astype(BF16))
                cqh.append(jnp.sum(jnp.where(lane_q == FOX_LANE0 + 2 * p + hh, cq, 0.0), axis=1, keepdims=True))
                lseh.append(jnp.sum(jnp.where(lane_q == hh, lse, 0.0), axis=1, keepdims=True))
                dlt.append(jnp.sum(jnp.where(halfq == hh, delta, 0.0), axis=1, keepdims=True))

            def step(j, carry, diag):
                dq, rs = carry
                off = pl.multiple_of(j * tk, tk)
                kj = k_ref[pl.ds(off, tk), :].astype(BF16)
                vj = v_ref[pl.ds(off, tk), :].astype(BF16)
                ckt = cumt_ref[:, pl.ds(off, tk)]
                dss, prs, ks = [], [], []
                for hh in range(2):
                    ck = jnp.sum(jnp.where(row8 == 2 * p + hh, ckt, 0.0), axis=0, keepdims=True)
                    logits = _nt(qh[hh], kj) + ((cqh[hh] - lseh[hh]) - ck)
                    if diag:
                        logits = jnp.where(rowi >= coli, logits, -jnp.inf)
                    pr = jnp.exp(logits)
                    ds = pr * (_nt(doh[hh], vj) - dlt[hh])
                    rs = rs + jnp.where(lane_q == FOX_LANE0 + 2 * p + hh, jnp.sum(ds, axis=1, keepdims=True), 0.0)
                    cs = _colsum(ds)
                    dcumt_ref[0:8, pl.ds(off, tk)] += jnp.where(row8 == 2 * p + hh, cs, 0.0)
                    dss.append(ds.astype(BF16))
                    prs.append(pr.astype(BF16))
                    ks.append(jnp.where(halfk == hh, kj, jnp.zeros_like(kj)))
                dq = dq + jnp.dot(jnp.concatenate(dss, axis=1), jnp.concatenate(ks, axis=0), preferred_element_type=F32)
                dk_acc[pl.ds(off, tk), :] += _tn(jnp.concatenate(dss, axis=0), jnp.concatenate(qh, axis=0))
                dv_acc[pl.ds(off, tk), :] += _tn(jnp.concatenate(prs, axis=0), jnp.concatenate(doh, axis=0))
                return dq, rs

            carry = lax.fori_loop(0, i, functools.partial(step, diag=False),
                                  (jnp.zeros((tq, LANE), F32), jnp.zeros((tq, LANE), F32)))
            dq, rs = step(i, carry, True)
            dq_ref[pl.ds(qoff, tq), :] = (dq * scale).astype(dq_ref.dtype)
            dcum_ref[pl.ds(qoff, tq), :] += rs
            return 0

        lax.fori_loop(0, nq, qblock, 0)
        dk_ref[...] = dk_acc[...].astype(dk_ref.dtype)
        dv_ref[...] = dv_acc[...].astype(dv_ref.dtype)
        dct = dcumt_ref[...].T
        dcum_ref[...] = dcum_ref[...] - pltpu.roll(dct, FOX_LANE0, 1)

    sp = lambda c0: pl.BlockSpec((S, LANE), lambda b, p: (b, c0 + p))
    s0 = pl.BlockSpec((S, LANE), lambda b, p: (b, 0))
    comm = carried.take(name) if carried is not None else None
    call = _pc(body, name=name, grid=(batch, 3),
               in_specs=[sp(qb), sp(kb), sp(vb), sp(0), sp(0), sp(0), s0, pl.BlockSpec((8, S), lambda b, p: (b, 0))],
               out_specs=[sp(0)] * 4,
               out_shape=[SDS((T, FOX_W), BF16)] * 3 + [SDS((T, FOX_W), F32)],
               scratch=[pltpu.VMEM((S, LANE), F32), pltpu.VMEM((S, LANE), F32), pltpu.VMEM((LANE, S), F32)],
               sem=("parallel", "parallel"), comm=comm)
    return _run(call, (proj, proj, proj, o, do, lse, cum, cum_t), name, comm, carried)


def _row(v, width=None, at=0):
    v = v.astype(F32)
    width = width or v.shape[0]
    return jnp.pad(v, (at, width - at - v.shape[0]))[None, :]


def _pad8(w4):
    return jnp.pad(w4.astype(F32), ((0, 4), (0, 0)))


def _block_diag(w):
    eye = jnp.eye(w.shape[0], dtype=w.dtype)
    return (w[:, :, None, :] * eye[:, None, :, None]).reshape(LRU_W, LRU_W)


def prep_layer(f):
    cw, cb = f["ssd_conv_w"], f["ssd_conv_b"]
    return LazyDict(
        win=lambda: permute_in_cols(f["w_in"]), wout=lambda: f["w_out"],
        wg=lambda: f["w_gate_t"] if "w_gate_t" in f else f["w_gate"].T,
        wu=lambda: f["w_up_t"] if "w_up_t" in f else f["w_up"].T,
        wd=lambda: f["w_down"], wpg=lambda: f["w_ple_gate"], wpp=lambda: f["w_ple_proj"],
        wax=jnp.concatenate([_block_diag(f["lru_w_a"]), _block_diag(f["lru_w_x"])], axis=1),
        g1=_row(f["norm1_g"]), g2=_row(f["norm2_g"]), g3=_row(f["norm3_g"]),
        cw_xs=_pad8(cw[:, :384]), cb_xs=_row(cb[:384]), cw_b=_pad8(cw[:, 384:640]), cb_b=_row(cb[384:640]),
        cw_c=_pad8(cw[:, 640:]), cb_c=_row(cb[640:]), cw_l=_pad8(f["lru_conv_w"]), cb_l=_row(f["lru_conv_b"]),
        dtbias_row=_row(f["ssd_dt_bias"], LANE), alog_row=_row(f["ssd_a_log"], LANE),
        dexp=jnp.repeat(f["ssd_d"].astype(F32), HEAD_DIM)[None, :], g_ssd=_row(f["ssd_norm_g"]),
        b_ax=_row(jnp.concatenate([f["lru_b_a"], f["lru_b_x"]])), lam=_row(f["lru_lambda"]), g_lru=_row(f["lru_norm_g"]),
        bf_row=_row(f["fox_b_f"], LANE, FOX_LANE0), g_fox=_row(f["fox_norm_g"]), b_pg=_row(f["b_ple_gate"]))


def unprep_grads(g):
    blocks = lambda m: jnp.stack([m[i * 64:(i + 1) * 64, i * 64:(i + 1) * 64] for i in range(4)])
    return dict(
        norm1_g=g["g1"][0], w_in=unpermute_in_cols(g["win"]),
        ssd_conv_w=jnp.concatenate([g["cw_xs"][:4], g["cw_b"][:4], g["cw_c"][:4]], axis=1),
        ssd_conv_b=jnp.concatenate([g["cb_xs"][0], g["cb_b"][0], g["cb_c"][0]]),
        ssd_dt_bias=g["dtbias_row"][0, :N_HEADS], ssd_a_log=g["alog_row"][0, :N_HEADS],
        ssd_d=jnp.sum(g["dexp"].reshape(N_HEADS, HEAD_DIM), axis=1), ssd_norm_g=g["g_ssd"][0],
        lru_conv_w=g["cw_l"][:4], lru_conv_b=g["cb_l"][0],
        lru_w_a=blocks(g["wax"][:, :LRU_W]), lru_b_a=g["b_ax"][0, :LRU_W],
        lru_w_x=blocks(g["wax"][:, LRU_W:]), lru_b_x=g["b_ax"][0, LRU_W:],
        lru_lambda=g["lam"][0], lru_norm_g=g["g_lru"][0],
        fox_b_f=g["bf_row"][0, FOX_LANE0:FOX_LANE0 + N_HEADS], fox_norm_g=g["g_fox"][0],
        w_out=g["wout"], norm2_g=g["g2"][0], w_gate=g["wg"].T, w_up=g["wu"].T, w_down=g["wd"],
        norm3_g=g["g3"][0], w_ple_gate=g["wpg"], b_ple_gate=g["b_pg"][0], w_ple_proj=g["wpp"])


def _view(a, off, width):
    return (a, off // width, width)


def _add_epilogue(acc, e):
    return (acc + e,)


def mixer_fwd(proj, w, batch, tag, carried=None):
    sm = _view(proj, OFF_SM, LANE)
    conv = functools.partial(seq_conv, batch=batch)
    cmc = conv(f"{tag}_conv_c", proj, OFF_C, 256, w["cw_c"], w["cb_c"], silu=True, out_dtype=BF16)
    bmc = conv(f"{tag}_conv_b", proj, OFF_B, 256, w["cw_b"], w["cb_b"], silu=True, out_dtype=BF16)
    xs_act = conv(f"{tag}_conv_xs", proj, OFF_XS, SSD_W, w["cw_xs"], w["cb_xs"], silu=True, out_dtype=F32)
    xl = conv(f"{tag}_conv_l", proj, OFF_LX, LRU_W, w["cw_l"], w["cb_l"], silu=False, out_dtype=F32)
    adt, xd = rowwise(f"{tag}_ssd_elt", _ssd_elt, [sm, xs_act], [w["dtbias_row"], w["alog_row"]],
                      [(LANE, F32), (SSD_W, BF16)])
    cum_a, cum_at = seq_cumsum(f"{tag}_cum_a", adt, batch=batch, trow=0)
    yraw = ssd_attn_fwd(f"{tag}_ssd_fwd", cmc, bmc, xd, cum_a, cum_at, batch=batch, carried=carried)
    logf = rowwise(f"{tag}_fox_elt", _fox_elt, [sm], [w["bf_row"]], [(LANE, F32)])
    cum_f, cum_ft = seq_cumsum(f"{tag}_cum_f", logf, batch=batch, trow=FOX_LANE0)
    o, lse = fox_attn_fwd(f"{tag}_fox_fwd", proj, cum_f, cum_ft, batch=batch, carried=carried)
    pre = mm(xl, w["wax"], name=f"{tag}_mm_lru_gates")
    a, u = rowwise(f"{tag}_lru_elt", _lru_elt, [xl, pre], [w["b_ax"], w["lam"]], [(LRU_W, F32), (LRU_W, F32)])
    hl = lru_scan(f"{tag}_lru_scan", a, u, batch=batch)
    ycat = rowwise(f"{tag}_mix_post", _mix_post,
                   [yraw, xs_act, _view(proj, OFF_Z, SSD_W), hl, _view(proj, OFF_LG, LRU_W), o],
                   [w["dexp"], w["g_ssd"], w["g_lru"], w["g_fox"]], [(D_MODEL, BF16)], tr=256)
    saved = dict(cmc=cmc, bmc=bmc, xs_act=xs_act, xl=xl, xd=xd, cum_a=cum_a, cum_at=cum_at, yraw=yraw,
                 cum_f=cum_f, cum_ft=cum_ft, o=o, lse=lse, pre=pre, a=a, hl=hl)
    return ycat, saved


def mixer_bwd(dycat, proj, w, s, batch, tag, carried=None):
    sm = _view(proj, OFF_SM, LANE)
    g = {}

    def post_bwd(yraw, xs_act, z, hl, lg, o, dyc, dexp, g_ssd, g_lru, g_fox):
        return jax.vjp(_mix_post, yraw, xs_act, z, hl, lg, o, dexp, g_ssd, g_lru, g_fox)[1](dyc)

    (dyraw, dxs1, dz, dhl, dlg, do, g["dexp"], g["g_ssd"], g["g_lru"], g["g_fox"]) = rowwise(
        f"{tag}_mix_post_bwd", post_bwd,
        [s["yraw"], s["xs_act"], _view(proj, OFF_Z, SSD_W), s["hl"], _view(proj, OFF_LG, LRU_W), s["o"], dycat],
        [w["dexp"], w["g_ssd"], w["g_lru"], w["g_fox"]],
        [(SSD_W, F32), (SSD_W, F32), (SSD_W, BF16), (LRU_W, F32), (LRU_W, BF16), (FOX_W, F32)],
        [SSD_W, SSD_W, LRU_W, FOX_W], tr=256)

    dq, dk, dv, dcum3 = fox_attn_bwd(f"{tag}_fox_bwd", proj, s["o"], do, s["lse"], s["cum_f"], s["cum_ft"], batch=batch,
                                     carried=carried)
    dlogf = seq_cumsum(f"{tag}_rcum_f", dcum3, batch=batch, reverse=True, nsum=3)

    dxd, dbm, dcm, dcum_a = ssd_attn_bwd(f"{tag}_ssd_bwd", s["cmc"], s["bmc"], s["xd"], s["cum_a"], s["cum_at"], dyraw,
                                         batch=batch, carried=carried)
    dadt = seq_cumsum(f"{tag}_rcum_a", dcum_a, batch=batch, reverse=True)

    def ssd_elt_bwd(small, xs_act, dadt_, dxd_, dxs1_, dtbias, alog):
        dsm, dxs, ddtb, dalog = jax.vjp(_ssd_elt, small, xs_act, dtbias, alog)[1]((dadt_, dxd_))
        return dsm, dxs + dxs1_, ddtb, dalog

    dsm_s, dxs_act, g["dtbias_row"], g["alog_row"] = rowwise(
        f"{tag}_ssd_elt_bwd", ssd_elt_bwd, [sm, s["xs_act"], dadt, dxd, dxs1], [w["dtbias_row"], w["alog_row"]],
        [(LANE, F32), (SSD_W, F32)], [LANE, LANE])

    def fox_elt_bwd(small, dlogf_, dsm_s_, bf_row):
        dsm, dbf = jax.vjp(_fox_elt, small, bf_row)[1](dlogf_)
        return dsm + dsm_s_, dbf

    dsm, g["bf_row"] = rowwise(f"{tag}_fox_elt_bwd", fox_elt_bwd, [sm, dlogf, dsm_s], [w["bf_row"]],
                               [(LANE, BF16)], [LANE])

    cbwd = functools.partial(seq_conv_bwd, batch=batch)
    dxs_raw, g["cw_xs"], g["cb_xs"] = cbwd(f"{tag}_conv_xs_bwd", proj, OFF_XS, SSD_W, w["cw_xs"], w["cb_xs"], dxs_act, silu=True)
    db_raw, g["cw_b"], g["cb_b"] = cbwd(f"{tag}_conv_b_bwd", proj, OFF_B, 256, w["cw_b"], w["cb_b"], dbm, silu=True)
    dc_raw, g["cw_c"], g["cb_c"] = cbwd(f"{tag}_conv_c_bwd", proj, OFF_C, 256, w["cw_c"], w["cb_c"], dcm, silu=True)

    da, du = lru_scan_bwd(f"{tag}_lru_scan_bwd", s["a"], s["hl"], dhl, batch=batch)

    def lru_elt_bwd(xl, pre, da_, du_, b_ax, lam):
        return jax.vjp(_lru_elt, xl, pre, b_ax, lam)[1]((da_, du_))

    dxl1, dpre, g["b_ax"], g["lam"] = rowwise(
        f"{tag}_lru_elt_bwd", lru_elt_bwd, [s["xl"], s["pre"], da, du], [w["b_ax"], w["lam"]],
        [(LRU_W, F32), (2 * LRU_W, BF16)], [2 * LRU_W, LRU_W])
    g["wax"] = mm(s["xl"], dpre, ta=True, name=f"{tag}_mm_dwax")
    dxl = mm(dpre, w["wax"], tb=True, extras=[dxl1], epilogue=_add_epilogue, name=f"{tag}_mm_dxl")
    dlx_raw, g["cw_l"], g["cb_l"] = cbwd(f"{tag}_conv_l_bwd", proj, OFF_LX, LRU_W, w["cw_l"], w["cb_l"], dxl, silu=False)

    dproj = jnp.concatenate([db_raw, dc_raw, dlx_raw, dlg, dsm, dz, dxs_raw, dq, dk, dv], axis=1)
    return dproj, g


def layer_fwd(h0, p_l, w, batch, tag, carried=None):
    u1 = rowwise(f"{tag}_rms1", _rms, [h0], [w["g1"]], [(D_MODEL, BF16)])
    proj = mm(u1, w["win"], name=f"{tag}_mm_in", carried=carried)
    ycat, ms = mixer_fwd(proj, w, batch, tag, carried)
    add_norm = dict(epilogue=_add_rms_epilogue, out_dtypes=(F32, BF16), tm=512, tn=D_MODEL)
    h1, u2 = mm(ycat, w["wout"], extras=[h0], col_params=[w["g2"]], name=f"{tag}_mm_out", **add_norm)
    gate, up, act = mm(u2, [w["wg"], w["wu"]], tb=True, out_dtypes=(BF16, BF16, BF16), epilogue=_swiglu_epilogue,
                       tm=512, tn=D_FF // 2, name=f"{tag}_mm_gu", carried=carried)
    h2, u3 = mm(act, w["wd"], extras=[h1], col_params=[w["g3"]], tk=D_FF, name=f"{tag}_mm_down", **add_norm)
    pp = mm(p_l, w["wpp"], name=f"{tag}_mm_pp")
    h3, pg = mm(u3, w["wpg"], extras=[pp, h2], col_params=[w["b_pg"]], epilogue=_ple_epilogue,
                out_dtypes=(F32, F32), tm=512, name=f"{tag}_mm_pg")
    saved = dict(h0=h0, u1=u1, proj=proj, ycat=ycat, h1=h1, u2=u2, gate=gate, up=up, act=act, h2=h2, u3=u3, pg=pg,
                 pp=pp, mixer=ms)
    return h3, saved


def _add_rms_epilogue(acc, res, g):
    h = res + acc
    return h, _rms(h, g)


def _swiglu_epilogue(acc_g, acc_u):
    return acc_g, acc_u, _silu(acc_g) * acc_u


def _swiglu_bwd_epilogue(dact, gate, up):
    return jax.vjp(lambda g_, u_: _silu(g_) * u_, gate.astype(F32), up.astype(F32))[1](dact)


def _ple_epilogue(acc, pp, h2, b):
    return h2 + _ple(acc, pp, b), acc


def _rms_bwd_epilogue(du, h, dres, g):
    r = lax.rsqrt(jnp.mean(h * h, axis=-1, keepdims=True) + EPS)
    n = h * r
    t = du * n
    dh = r * (du * g - n * jnp.mean(t * g, axis=-1, keepdims=True))
    return dh + dres, _colsum(t)


def layer_bwd(dh3, p_l, w, s, batch, tag, carried=None, on_early_grads=None):
    def ple_bwd(pg, pp, dh, b):
        return jax.vjp(_ple, pg, pp, b)[1](dh)

    norm_bwd = dict(epilogue=_rms_bwd_epilogue, partials=1, tm=512, tn=D_MODEL, tb=True)

    d_pg, d_pp, g_bpg = rowwise(f"{tag}_ple_bwd", ple_bwd, [s["pg"], s["pp"], dh3], [w["b_pg"]],
                                [(D_MODEL, BF16), (D_MODEL, BF16)], [D_MODEL])
    g = dict(b_pg=g_bpg)
    g["wpp"] = mm(p_l, d_pp, ta=True, name=f"{tag}_mm_dwpp")
    g["wpg"] = mm(s["u3"], d_pg, ta=True, name=f"{tag}_mm_dwpg")
    dh2, dg3 = mm(d_pg, w["wpg"], extras=[s["h2"], dh3], col_params=[w["g3"]], name=f"{tag}_mm_du3", **norm_bwd)
    g["g3"] = sum_slices(f"{tag}_sum_dg3", dg3)

    d_gate, d_up = mm(dh2, w["wd"], tb=True, extras=[s["gate"], s["up"]], epilogue=_swiglu_bwd_epilogue,
                      out_dtypes=(BF16, BF16), tm=512, tn=D_FF // 2, name=f"{tag}_mm_dact")
    g["wd"] = mm(s["act"], dh2, ta=True, name=f"{tag}_mm_dwd")
    g["wg"] = mm(d_gate, s["u2"], ta=True, name=f"{tag}_mm_dwg")
    g["wu"] = mm(d_up, s["u2"], ta=True, name=f"{tag}_mm_dwu")
    dh1, dg2 = mm([d_gate, d_up], [w["wg"], w["wu"]], extras=[s["h1"], dh2], col_params=[w["g2"]],
                  name=f"{tag}_mm_du2", **{**norm_bwd, "tb": False})
    g["g2"] = sum_slices(f"{tag}_sum_dg2", dg2)

    dycat = mm(dh1, w["wout"], tb=True, name=f"{tag}_mm_dycat")
    g["wout"] = mm(s["ycat"], dh1, ta=True, name=f"{tag}_mm_dwout")
    if on_early_grads is not None:
        on_early_grads(g)
    dproj, gm = mixer_bwd(dycat, s["proj"], w, s["mixer"], batch, tag, carried)
    g.update(gm)
    g["win"] = mm(s["u1"], dproj, ta=True, name=f"{tag}_mm_dwin")
    dh0, dg1 = mm(dproj, w["win"], extras=[s["h0"], dh1], col_params=[w["g1"]], name=f"{tag}_mm_du1", **norm_bwd)
    g["g1"] = sum_slices(f"{tag}_sum_dg1", dg1)
    return dh0, g


def _loss_fwd_bwd(h, tgt, gf):
    def f(h_, gf_):
        e = _rms(h_, gf_) - tgt
        return 0.5 * jnp.sum(jnp.mean(e * e, axis=-1, keepdims=True), axis=0, keepdims=True)

    loss, vj = jax.vjp(f, h, gf)
    dh, dgf = vj(jnp.ones((1, 1), F32))
    return dh, jnp.broadcast_to(loss, (1, LANE)), dgf


def local_step(x, p, tgt, layers, final_g, carried=None, on_layer_grads=None, on_early_grads=None):
    batch, S, _ = x.shape
    T = batch * S
    h = x.reshape(T, D_MODEL)
    saved, weights = [], []
    for l, w in enumerate(layers):
        w = w() if callable(w) else w
        weights.append(w)
        h, s = layer_fwd(h, p[l].reshape(T, PLE_DIM), w, batch, f"l{l}", carried)
        saved.append(s)
    dh, loss, dgf = rowwise("loss", _loss_fwd_bwd, [h, tgt.reshape(T, D_MODEL)], [_row(final_g)],
                            [(D_MODEL, F32)], [LANE, D_MODEL], tr=256)
    grads = [None] * len(layers)
    for l in reversed(range(len(layers))):
        early = functools.partial(on_early_grads, l) if on_early_grads is not None else None
        dh, grads[l] = layer_bwd(dh, p[l].reshape(T, PLE_DIM), weights[l], saved[l], batch, f"l{l}", carried, early)
        if on_layer_grads is not None:
            on_layer_grads(l, grads[l])
    return loss[0, 0], dh.reshape(batch, S, D_MODEL), grads, dgf[0]


MESH = pl.DeviceIdType.MESH
N_DEV = 8
N_CHIP = 4
ANY = pl.BlockSpec(memory_space=pl.ANY)


def _pos():
    return lax.axis_index("x"), lax.axis_index("y"), lax.axis_index("c")


def _comm_call(body, name, out_shape, n_in, scratch):
    return pl.pallas_call(body, name=name, out_shape=out_shape, in_specs=[ANY] * n_in, out_specs=ANY,
                          scratch_shapes=scratch)


def all_gather8(name, blk):
    def body(x_ref, out_ref, send_sems, recv_sems, local_sem):
        x, y, c = _pos()
        me, sibling = (x, y, c), (x, y, 1 - c)
        chips = [(1 - x, y), (x, 1 - y), (1 - x, 1 - y)]

        def rows(px, py, pcore):
            return out_ref.at[4 * px + 2 * py + pcore]

        def copy(k, block, to, src=None):
            return pltpu.make_async_remote_copy(
                src_ref=rows(*block) if src is None else src, dst_ref=rows(*block),
                send_sem=send_sems.at[k], recv_sem=recv_sems.at[k], device_id=to, device_id_type=MESH)

        mine = pltpu.make_async_copy(x_ref, rows(*me), local_sem)
        mine.start()
        first = [copy(0, me, sibling, src=x_ref)]
        first += [copy(1 + j, me, (*chip, c), src=x_ref) for j, chip in enumerate(chips)]
        for cp in first:
            cp.start()
        passed = [copy(4 + j, (*chip, c), sibling) for j, chip in enumerate(chips)]
        for j, chip in enumerate(chips):
            copy(1 + j, (*chip, c), me).wait_recv()
            passed[j].start()
        copy(0, sibling, me).wait_recv()
        for j, chip in enumerate(chips):
            copy(4 + j, (*chip, 1 - c), me).wait_recv()
        for cp in first + passed:
            cp.wait_send()
        mine.wait()

    return _comm_call(body, name, SDS((N_DEV,) + blk.shape, blk.dtype), 1,
                      [pltpu.SemaphoreType.DMA((7,)), pltpu.SemaphoreType.DMA((7,)), pltpu.SemaphoreType.DMA])(blk)


def _comm_call_list(body, name, out_shapes, n_in, scratch):
    return pl.pallas_call(body, name=name, out_shape=out_shapes, in_specs=[ANY] * n_in, out_specs=[ANY] * len(out_shapes),
                          scratch_shapes=scratch)


class Exchange:
    def __init__(self, inputs, out_shapes, sems, start, wait, aliases=None):
        self.inputs, self.out_shapes, self.sems = list(inputs), list(out_shapes), list(sems)
        self.start, self.wait, self.aliases = start, wait, dict(aliases or {})


def combine(a, b):
    ai, ao, as_ = len(a.inputs), len(a.out_shapes), len(a.sems)

    def split(cins, couts, sems):
        return (cins[:ai], couts[:ao], sems[:as_]), (cins[ai:], couts[ao:], sems[as_:])

    def start(cins, couts, sems):
        pa, pb = split(cins, couts, sems)
        a.start(*pa)
        b.start(*pb)

    def wait(cins, couts, sems):
        pa, pb = split(cins, couts, sems)
        a.wait(*pa)
        b.wait(*pb)

    aliases = dict(a.aliases)
    aliases.update({ai + i: ao + o for i, o in b.aliases.items()})
    return Exchange(a.inputs + b.inputs, a.out_shapes + b.out_shapes, a.sems + b.sems, start, wait, aliases)


def run_exchange(name, ex):
    n_ci, n_co = len(ex.inputs), len(ex.out_shapes)

    def body(*refs):
        cins, couts, csems = refs[:n_ci], refs[n_ci:n_ci + n_co], refs[n_ci + n_co:]
        ex.start(cins, couts, csems)
        ex.wait(cins, couts, csems)

    return pl.pallas_call(body, name=name, out_shape=ex.out_shapes, in_specs=[ANY] * n_ci, out_specs=[ANY] * n_co,
                          scratch_shapes=ex.sems, input_output_aliases=ex.aliases)(*ex.inputs)


def _peers():
    x, y, c = _pos()
    return x, y, c, 2 * x + y, [(1 - x, y), (x, 1 - y), (1 - x, 1 - y)]


def _remote(src, dst, send_sem, recv_sem, to):
    return pltpu.make_async_remote_copy(src_ref=src, dst_ref=dst, send_sem=send_sem, recv_sem=recv_sem,
                                        device_id=to, device_id_type=MESH)


def gather_spread(shards, layer):
    n_t = len(shards)
    halves = [s.shape[1] // 2 for s in shards]

    def copies(cins, couts, sems):
        send_sems, recv_sems, local_sems = sems
        x, y, c, my_chip, chips = _peers()
        local, sends, recvs = [], [], []
        for t in range(n_t):
            h = halves[t]
            src = cins[t].at[layer, pl.ds(c * h, h)]
            mine = couts[t].at[my_chip, pl.ds(c * h, h)]
            local.append(pltpu.make_async_copy(src, mine, local_sems.at[t]))
            sends.append(_remote(src, mine, send_sems.at[0, t], recv_sems.at[0, t], (x, y, 1 - c)))
            recvs.append(_remote(src, couts[t].at[my_chip, pl.ds((1 - c) * h, h)], send_sems.at[0, t],
                                 recv_sems.at[0, t], (x, y, 1 - c)))
            for j, (px, py) in enumerate(chips):
                sends.append(_remote(src, mine, send_sems.at[1 + j, t], recv_sems.at[1 + j, t], (px, py, c)))
                recvs.append(_remote(src, couts[t].at[2 * px + py, pl.ds(c * h, h)], send_sems.at[1 + j, t],
                                     recv_sems.at[1 + j, t], (px, py, c)))
        return local, sends, recvs

    def start(cins, couts, sems):
        local, sends, _ = copies(cins, couts, sems)
        for cp in local + sends:
            cp.start()

    def wait(cins, couts, sems):
        local, sends, recvs = copies(cins, couts, sems)
        for cp in recvs:
            cp.wait_recv()
        for cp in sends:
            cp.wait_send()
        for cp in local:
            cp.wait()

    return Exchange(shards, [SDS((N_CHIP,) + s.shape[1:], s.dtype) for s in shards],
                    [pltpu.SemaphoreType.DMA((4, n_t)), pltpu.SemaphoreType.DMA((4, n_t)),
                     pltpu.SemaphoreType.DMA((n_t,))], start, wait)


def gather_pass_on(slots):
    n_t = len(slots)
    halves = [s.shape[1] // 2 for s in slots]

    def copies(cins, couts, sems):
        send_sems, recv_sems = sems
        x, y, c, my_chip, chips = _peers()
        sends, recvs = [], []
        for t in range(n_t):
            h = halves[t]
            for j, (px, py) in enumerate(chips):
                k = 2 * px + py
                sends.append(_remote(cins[t].at[k, pl.ds(c * h, h)], couts[t].at[k, pl.ds(c * h, h)],
                                     send_sems.at[j, t], recv_sems.at[j, t], (x, y, 1 - c)))
                recvs.append(_remote(cins[t].at[k, pl.ds(c * h, h)], couts[t].at[k, pl.ds((1 - c) * h, h)],
                                     send_sems.at[j, t], recv_sems.at[j, t], (x, y, 1 - c)))
        return sends, recvs

    def start(cins, couts, sems):
        for cp in copies(cins, couts, sems)[0]:
            cp.start()

    def wait(cins, couts, sems):
        sends, recvs = copies(cins, couts, sems)
        for cp in recvs:
            cp.wait_recv()
        for cp in sends:
            cp.wait_send()

    return Exchange(slots, [SDS(s.shape, s.dtype) for s in slots],
                    [pltpu.SemaphoreType.DMA((3, n_t)), pltpu.SemaphoreType.DMA((3, n_t))], start, wait,
                    aliases={t: t for t in range(n_t)})


def chips_exchange(vs):
    n_t = len(vs)

    def copies(cins, couts, sems):
        send_sems, recv_sems, local_sems = sems
        x, y, c, my_chip, chips = _peers()
        local = [pltpu.make_async_copy(cins[t].at[my_chip], couts[t].at[my_chip], local_sems.at[t]) for t in range(n_t)]
        sends, recvs = [], []
        for k, (px, py) in enumerate(chips):
            for t in range(n_t):
                sends.append(_remote(cins[t].at[2 * px + py], couts[t].at[my_chip], send_sems.at[k, t],
                                     recv_sems.at[k, t], (px, py, c)))
                recvs.append(_remote(cins[t].at[my_chip], couts[t].at[2 * px + py], send_sems.at[k, t],
                                     recv_sems.at[k, t], (px, py, c)))
        return local, sends, recvs

    def start(cins, couts, sems):
        local, sends, _ = copies(cins, couts, sems)
        for cp in local + sends:
            cp.start()

    def wait(cins, couts, sems):
        local, sends, recvs = copies(cins, couts, sems)
        for cp in recvs:
            cp.wait_recv()
        for cp in sends:
            cp.wait_send()
        for cp in local:
            cp.wait()

    return Exchange(vs, [SDS(v.shape, v.dtype) for v in vs],
                    [pltpu.SemaphoreType.DMA((3, n_t)), pltpu.SemaphoreType.DMA((3, n_t)),
                     pltpu.SemaphoreType.DMA((n_t,))], start, wait)


def swap_with_sibling(name, vs):
    n_t = len(vs)

    def body(*refs):
        v_refs, out_refs, send_sems, recv_sems = refs[:n_t], refs[n_t:2 * n_t], refs[-2], refs[-1]
        x, y, c = _pos()
        cps = [pltpu.make_async_remote_copy(src_ref=v_refs[t], dst_ref=out_refs[t], send_sem=send_sems.at[t],
                                            recv_sem=recv_sems.at[t], device_id=(x, y, 1 - c), device_id_type=MESH)
               for t in range(n_t)]
        for cp in cps:
            cp.start()
        for cp in cps:
            cp.wait()

    return _comm_call_list(body, name, [SDS(v.shape, v.dtype) for v in vs], n_t,
                           [pltpu.SemaphoreType.DMA((n_t,)), pltpu.SemaphoreType.DMA((n_t,))])(*vs)


_ROW_BLOCKS = (1024, 704, 512, 352, 256, 128, 64, 32, 16, 8)


def sum_slices(name, v, tr=512):
    n, R, C = v.shape
    tr = _pick(R, _ROW_BLOCKS)

    def body(v_ref, o_ref):
        acc = v_ref[0].astype(F32)
        for k in range(1, n):
            acc = acc + v_ref[k].astype(F32)
        o_ref[...] = acc

    return _pc(body, name=name, grid=(R // tr,), in_specs=[pl.BlockSpec((n, tr, C), lambda i: (0, i, 0))],
               out_specs=pl.BlockSpec((tr, C), lambda i: (i, 0)), out_shape=SDS((R, C), F32), sem=("parallel",))(v)


def add_slices(name, a, b, out_dtype):
    n, R, C = a.shape
    tr = _pick(R, _ROW_BLOCKS)

    def body(a_ref, b_ref, o_ref):
        o_ref[...] = (a_ref[...].astype(F32) + b_ref[...].astype(F32)).astype(o_ref.dtype)

    spec = pl.BlockSpec((1, tr, C), lambda k, i: (k, i, 0))
    return _pc(body, name=name, grid=(n, R // tr), in_specs=[spec, spec], out_specs=spec,
               out_shape=SDS(a.shape, out_dtype), sem=("parallel", "parallel"))(a, b)


def adamw(name, w, g, m, v):
    L, R, C = w.shape
    tr = _pick(R, (512, 352, 256, 128, 64, 32, 16, 8))
    c1 = 1.0 / (1.0 - ADAM_B1 ** ADAM_STEP)
    c2 = 1.0 / (1.0 - ADAM_B2 ** ADAM_STEP)

    def body(w_ref, g_ref, m_ref, v_ref, d_ref, nm_ref, nv_ref):
        gv = g_ref[...]
        nm = ADAM_B1 * m_ref[...] + (1.0 - ADAM_B1) * gv
        nv = ADAM_B2 * v_ref[...] + (1.0 - ADAM_B2) * (gv * gv)
        d_ref[...] = -ADAM_LR * ((nm * c1) / (jnp.sqrt(nv * c2) + ADAM_EPS) + ADAM_WD * w_ref[...])
        nm_ref[...] = nm
        nv_ref[...] = nv

    if R < 8:
        tl = 32
        spec = pl.BlockSpec((tl, R, C), lambda i, _: (i, 0, 0))
        grid = (pl.cdiv(L, tl), 1)
    else:
        spec = pl.BlockSpec((1, tr, C), lambda l, i: (l, i, 0))
        grid = (L, R // tr)
    return _pc(body, name=name, grid=grid, in_specs=[spec] * 4, out_specs=[spec] * 3,
               out_shape=[SDS(w.shape, F32)] * 3, sem=("parallel", "parallel"))(w, g, m, v)


WEIGHTS = ["norm1_g", "w_in", "ssd_conv_w", "ssd_conv_b", "ssd_dt_bias", "ssd_a_log", "ssd_d", "ssd_norm_g",
           "lru_conv_w", "lru_conv_b", "lru_w_a", "lru_b_a", "lru_w_x", "lru_b_x", "lru_lambda", "lru_norm_g",
           "fox_b_f", "fox_norm_g", "w_out", "norm2_g", "w_gate", "w_up", "w_down", "norm3_g", "w_ple_gate",
           "b_ple_gate", "w_ple_proj", "final_norm_g"]
BIG = {"w_in": 2, "w_out": 1, "w_gate": 2, "w_up": 2, "w_down": 1, "w_ple_gate": 1, "w_ple_proj": 2}
SHARDED_SMALL = {"ssd_conv_w": 2, "lru_conv_w": 2}
ADAM_VIEW = {"w_in": ((2, 0, 1), (1, 2, 0)), "w_gate": ((0, 2, 1), (0, 2, 1)), "w_up": ((0, 2, 1), (0, 2, 1))}
TRANSPOSED = ("w_gate", "w_up")
SMALL = [n for n in WEIGHTS if n not in BIG]


def _pack(arrs, rows_multiple):
    flat = jnp.concatenate([a.reshape(-1) for a in arrs])
    per = rows_multiple * LANE
    n = -(-flat.shape[0] // per) * per
    return jnp.pad(flat, (0, n - flat.shape[0])).reshape(n // LANE, LANE)


def _unpack(flat2d, shapes):
    flat = flat2d.reshape(-1)
    out, off = [], 0
    for s in shapes:
        n = int(np.prod(s))
        out.append(flat[off:off + n].reshape(s))
        off += n
    return out


def _gather_shards(name, shards, axes, dtype):
    c = lax.axis_index("c")
    packed = _pack([s.astype(dtype) for s in shards], 32)
    half = packed.shape[0] // 2
    mine = lax.dynamic_slice_in_dim(packed, c * half, half, 0)
    got = all_gather8(name, mine).reshape(N_CHIP, 2 * half, LANE)
    per_chip = [_unpack(got[k], [s.shape for s in shards]) for k in range(N_CHIP)]
    return [jnp.concatenate([per_chip[k][i] for k in range(N_CHIP)], axis=ax) for i, ax in enumerate(axes)]


def kernel(x, p, norm1_g, w_in, ssd_conv_w, ssd_conv_b, ssd_dt_bias, ssd_a_log, ssd_d, ssd_norm_g, lru_conv_w, lru_conv_b, lru_w_a, lru_b_a, lru_w_x, lru_b_x, lru_lambda, lru_norm_g, fox_b_f, fox_norm_g, w_out, norm2_g, w_gate, w_up, w_down, norm3_g, w_ple_gate, b_ple_gate, w_ple_proj, final_norm_g, loss_target, m_norm1_g, m_w_in, m_ssd_conv_w, m_ssd_conv_b, m_ssd_dt_bias, m_ssd_a_log, m_ssd_d, m_ssd_norm_g, m_lru_conv_w, m_lru_conv_b, m_lru_w_a, m_lru_b_a, m_lru_w_x, m_lru_b_x, m_lru_lambda, m_lru_norm_g, m_fox_b_f, m_fox_norm_g, m_w_out, m_norm2_g, m_w_gate, m_w_up, m_w_down, m_norm3_g, m_w_ple_gate, m_b_ple_gate, m_w_ple_proj, m_final_norm_g, v_norm1_g, v_w_in, v_ssd_conv_w, v_ssd_conv_b, v_ssd_dt_bias, v_ssd_a_log, v_ssd_d, v_ssd_norm_g, v_lru_conv_w, v_lru_conv_b, v_lru_w_a, v_lru_b_a, v_lru_w_x, v_lru_b_x, v_lru_lambda, v_lru_norm_g, v_fox_b_f, v_fox_norm_g, v_w_out, v_norm2_g, v_w_gate, v_w_up, v_w_down, v_norm3_g, v_w_ple_gate, v_b_ple_gate, v_w_ple_proj, v_final_norm_g):
    args = (norm1_g, w_in, ssd_conv_w, ssd_conv_b, ssd_dt_bias, ssd_a_log, ssd_d, ssd_norm_g, lru_conv_w, lru_conv_b, lru_w_a, lru_b_a, lru_w_x, lru_b_x, lru_lambda, lru_norm_g, fox_b_f, fox_norm_g, w_out, norm2_g, w_gate, w_up, w_down, norm3_g, w_ple_gate, b_ple_gate, w_ple_proj, final_norm_g)
    m_args = (m_norm1_g, m_w_in, m_ssd_conv_w, m_ssd_conv_b, m_ssd_dt_bias, m_ssd_a_log, m_ssd_d, m_ssd_norm_g, m_lru_conv_w, m_lru_conv_b, m_lru_w_a, m_lru_b_a, m_lru_w_x, m_lru_b_x, m_lru_lambda, m_lru_norm_g, m_fox_b_f, m_fox_norm_g, m_w_out, m_norm2_g, m_w_gate, m_w_up, m_w_down, m_norm3_g, m_w_ple_gate, m_b_ple_gate, m_w_ple_proj, m_final_norm_g)
    v_args = (v_norm1_g, v_w_in, v_ssd_conv_w, v_ssd_conv_b, v_ssd_dt_bias, v_ssd_a_log, v_ssd_d, v_ssd_norm_g, v_lru_conv_w, v_lru_conv_b, v_lru_w_a, v_lru_b_a, v_lru_w_x, v_lru_b_x, v_lru_lambda, v_lru_norm_g, v_fox_b_f, v_fox_norm_g, v_w_out, v_norm2_g, v_w_gate, v_w_up, v_w_down, v_norm3_g, v_w_ple_gate, v_b_ple_gate, v_w_ple_proj, v_final_norm_g)
    w = dict(zip(WEIGHTS, args))
    mom = dict(zip(WEIGHTS, m_args))
    var = dict(zip(WEIGHTS, v_args))
    xi, yi, ci = _pos()
    chip = 2 * xi + yi

    big_names = list(BIG)
    later = [n for n in big_names if n != "w_in"]
    by_rows = {n: BIG[n] == 1 or n in TRANSPOSED for n in big_names}
    wb = {n: (jnp.transpose(w[n], (0, 2, 1)) if n in TRANSPOSED else w[n]).astype(BF16) for n in big_names}
    conv_full = dict(zip(SHARDED_SMALL, _gather_shards("gather_conv", [w[n] for n in SHARDED_SMALL],
                                                       list(SHARDED_SMALL.values()), F32)))
    carried = Carried()

    def layer_weights(l, slots_of):
        def assemble(n):
            s4 = slots_of(n)
            return (s4.reshape(-1, s4.shape[-1]) if by_rows[n]
                    else jnp.concatenate([s4[k] for k in range(N_CHIP)], axis=1))

        f = LazyDict({n: (conv_full[n][l] if n in conv_full else w[n][l]) for n in SMALL if n != "final_norm_g"})
        f.update({(n + "_t" if n in TRANSPOSED else n): functools.partial(assemble, n) for n in big_names})
        return prep_layer(f)

    w_in0 = run_exchange("gather0_in_pass_on", gather_pass_on(
        run_exchange("gather0_in_spread", gather_spread([wb["w_in"]], 0))))
    n_early = 2
    carried.offer("l0_mm_in", lambda: gather_spread([wb[n] for n in later[:n_early]], 0))
    carried.offer("l0_ssd_fwd", lambda: gather_spread([wb[n] for n in later[n_early:]], 0))
    carried.offer("l0_fox_fwd", lambda: combine(
        gather_pass_on(carried.results["l0_mm_in"] + carried.results["l0_ssd_fwd"]),
        gather_spread([wb[n] for n in big_names], 1)))
    carried.offer("l0_mm_gu", lambda: gather_pass_on(carried.results["l0_fox_fwd"][len(later):]))
    layers = [
        layer_weights(0, lambda n: w_in0[0] if n == "w_in" else carried.results["l0_fox_fwd"][later.index(n)]),
        lambda: layer_weights(1, lambda n: carried.results["l0_mm_gu"][big_names.index(n)])]

    def chip_slices(a, n):
        return a.reshape(N_CHIP, -1, a.shape[1]) if by_rows[n] else jnp.stack(jnp.split(a, N_CHIP, axis=1))

    def reduce_start(tag, names, full_grads):
        keep, give = [], []
        for n, a in zip(names, full_grads):
            s4 = chip_slices(a, n)
            h = s4.shape[1] // 2
            keep.append(lax.dynamic_slice_in_dim(s4, ci * h, h, 1))
            give.append(lax.dynamic_slice_in_dim(s4, (1 - ci) * h, h, 1).astype(BF16))
        got = swap_with_sibling(f"swap_halves{tag}", give)
        return [add_slices(f"add_sibling{tag}_{n}", k_, g_, BF16) for n, k_, g_ in zip(names, keep, got)]

    kernel_key = dict(w_out="wout", w_gate="wg", w_up="wu", w_down="wd", w_ple_gate="wpg", w_ple_proj="wpp")
    gl, parts = [None] * DEPTH, {}

    def on_early_grads(l, g_layer):
        if l == 0:
            parts["0_later"] = reduce_start("0_later", later, [g_layer[kernel_key[n]] for n in later])
            carried.offer("l0_ssd_bwd", lambda: chips_exchange(parts["0_later"]))

    def on_layer_grads(l, g_layer):
        gl[l] = unprep_grads(g_layer)
        if l == 1:
            parts["1"] = reduce_start("1", big_names, [gl[1]["w_in"]] + [g_layer[kernel_key[n]] for n in later])
            carried.offer("l0_fox_bwd", lambda: chips_exchange(parts["1"]))
        else:
            parts["0_in"] = reduce_start("0_in", ["w_in"], [gl[0]["w_in"]])

    loss, grad_x, grads, g_final = local_step(x, p, loss_target, layers, final_norm_g, carried, on_layer_grads,
                                              on_early_grads)
    loss = lax.psum(loss, ("x", "y", "c"))
    arrived0 = dict(zip(later, carried.results["l0_ssd_bwd"]))
    arrived0["w_in"] = run_exchange("a2a_chips0_in", chips_exchange(parts["0_in"]))[0]
    arrived = [[arrived0[n] for n in big_names], carried.results["l0_fox_bwd"]]

    gsmall = {n: jnp.stack([gl[l][n] for l in range(DEPTH)]) for n in SMALL if n != "final_norm_g"}
    gsmall["final_norm_g"] = g_final
    small_shapes = [gsmall[n].shape for n in SMALL]
    gs = _pack([gsmall[n] for n in SMALL], 8)
    gs = sum_slices("sum_small", all_gather8("gather_small_grads", gs))
    gsum = dict(zip(SMALL, _unpack(gs, small_shapes)))
    for n, ax in SHARDED_SMALL.items():
        k = gsum[n].shape[ax] // N_CHIP
        gsum[n] = lax.dynamic_slice_in_dim(gsum[n], chip * k, k, ax)

    done = []
    for l in range(DEPTH):
        mine = [sum_slices(f"sum_chips{l}_{n}", a_) for n, a_ in zip(big_names, arrived[l])]
        other = swap_with_sibling(f"swap_results{l}", mine)
        done.append([jnp.concatenate([jnp.where(ci == 0, m_, o_), jnp.where(ci == 0, o_, m_)], axis=0)
                     for m_, o_ in zip(mine, other)])
    gview = {}
    for t, n in enumerate(big_names):
        g2 = jnp.stack([done[l][t] for l in range(DEPTH)])
        if n in TRANSPOSED:
            gview[n], gsum[n] = g2, jnp.transpose(g2, (0, 2, 1))
        else:
            gsum[n] = g2

    delta, new_m, new_v = {}, {}, {}
    for n in big_names:
        if n in ADAM_VIEW:
            to_view, back = ADAM_VIEW[n]
            gv = gview[n] if n in gview else jnp.transpose(gsum[n], to_view)
            outs = adamw(f"adamw_{n}", jnp.transpose(w[n], to_view), gv, jnp.transpose(mom[n], to_view),
                         jnp.transpose(var[n], to_view))
            delta[n], new_m[n], new_v[n] = [jnp.transpose(o, back) for o in outs]
        else:
            delta[n], new_m[n], new_v[n] = adamw(f"adamw_{n}", w[n], gsum[n], mom[n], var[n])
    shapes = [w[n].shape for n in SMALL]
    pk = lambda d: _pack([d[n] for n in SMALL], 8)[None]
    ds, ms, vs = adamw("adamw_small", pk(w), pk(gsum), pk(mom), pk(var))
    for d, packed in ((delta, ds), (new_m, ms), (new_v, vs)):
        d.update(zip(SMALL, _unpack(packed[0], shapes)))

    return (loss, grad_x, *[gsum[n] for n in WEIGHTS], *[delta[n] for n in WEIGHTS],
            *[new_m[n] for n in WEIGHTS], *[new_v[n] for n in WEIGHTS])
```

```python
import functools
import math

import jax
import jax.numpy as jnp
import numpy as np
from jax import lax
from jax.experimental import pallas as pl
from jax.experimental.pallas import tpu as pltpu

F32, BF16 = jnp.float32, jnp.bfloat16
SDS = jax.ShapeDtypeStruct

D_MODEL = 1024
DEPTH = 2
HEAD_DIM = 64
N_HEADS = 6
SSD_W, LRU_W, FOX_W = 384, 256, 384
D_FF = 2816
PLE_DIM = 256
IN_COLS = 2956
EPS = 1e-6
LRU_C = 8.0
LANE = 128
V7X_VMEM_LIMIT = 56 * 1024 * 1024

PW = 3072
OFF_B, OFF_C, OFF_LX, OFF_LG, OFF_SM, OFF_Z, OFF_XS, OFF_Q, OFF_K, OFF_V = (
    0, 256, 512, 768, 1024, 1152, 1536, 1920, 2304, 2688)
FOX_LANE0 = 8

ADAM_LR, ADAM_B1, ADAM_B2, ADAM_EPS, ADAM_WD, ADAM_STEP = 0.001, 0.9, 0.999, 1e-08, 0.01, 10


def _iota(shape, dim):
    return lax.broadcasted_iota(jnp.int32, shape, dim)


class Carried:
    def __init__(self):
        self.offers, self.results = {}, {}

    def offer(self, call_name, make_exchange):
        self.offers[call_name] = make_exchange

    def take(self, call_name):
        make = self.offers.pop(call_name, None)
        return None if make is None else make()

    def deliver(self, call_name, results):
        self.results[call_name] = results


class LazyDict(dict):
    def __getitem__(self, key):
        v = dict.__getitem__(self, key)
        if callable(v):
            v = v()
            dict.__setitem__(self, key, v)
        return v


def _run(call, args, name, comm, carried):
    if comm is None:
        return call(*args)
    own, brought = call(*args)
    carried.deliver(name, brought)
    return own


def _pc(body, *, name, grid, in_specs, out_specs, out_shape, scratch=(), sem=None, comm=None):
    if comm is None:
        return pl.pallas_call(
            body, name=name, grid=grid, in_specs=in_specs, out_specs=out_specs, out_shape=out_shape,
            scratch_shapes=list(scratch),
            compiler_params=pltpu.CompilerParams(dimension_semantics=sem, vmem_limit_bytes=V7X_VMEM_LIMIT))
    single = not isinstance(out_shape, (list, tuple))
    out_specs_l = [out_specs] if single else list(out_specs)
    out_shape_l = [out_shape] if single else list(out_shape)
    n_in, n_out, n_scr, n_ci, n_co = len(in_specs), len(out_shape_l), len(scratch), len(comm.inputs), len(comm.out_shapes)

    def hosted(*refs):
        ins, cins = refs[:n_in], refs[n_in:n_in + n_ci]
        outs, couts = refs[n_in + n_ci:n_in + n_ci + n_out], refs[n_in + n_ci + n_out:n_in + n_ci + n_out + n_co]
        rest = refs[n_in + n_ci + n_out + n_co:]
        scr, csems = rest[:n_scr], rest[n_scr:]
        ids = [pl.program_id(d) for d in range(len(grid))]
        first = functools.reduce(jnp.logical_and, [i == 0 for i in ids])
        last = functools.reduce(jnp.logical_and, [i == g - 1 for i, g in zip(ids, grid)])

        @pl.when(first)
        def _():
            comm.start(cins, couts, csems)

        body(*ins, *outs, *scr)

        @pl.when(last)
        def _():
            comm.wait(cins, couts, csems)

    call = pl.pallas_call(
        hosted, name=name, grid=grid, in_specs=list(in_specs) + [ANY] * n_ci,
        out_specs=out_specs_l + [ANY] * n_co, out_shape=out_shape_l + list(comm.out_shapes),
        scratch_shapes=list(scratch) + list(comm.sems),
        input_output_aliases={n_in + a: n_out + b for a, b in comm.aliases.items()},
        compiler_params=pltpu.CompilerParams(dimension_semantics=("arbitrary",) * len(grid),
                                             vmem_limit_bytes=V7X_VMEM_LIMIT))

    def run(*args):
        res = call(*args, *comm.inputs)
        own = res[:n_out]
        return (own[0] if single else own), list(res[n_out:])

    return run


def permute_in_cols(w):
    z = lambda n: jnp.zeros(w.shape[:-1] + (n,), w.dtype)
    s = lambda a, b: w[..., a:b]
    return jnp.concatenate([
        s(768, 1024), s(1024, 1280), s(1286, 1542), s(1542, 1798),
        s(1280, 1286), z(2), s(2950, 2956), z(LANE - 14),
        s(0, 384), s(384, 768), s(1798, 2182), s(2182, 2566), s(2566, 2950)], axis=-1)


def unpermute_in_cols(g):
    s = lambda a, n: g[..., a:a + n]
    return jnp.concatenate([
        s(OFF_Z, 384), s(OFF_XS, 384), s(OFF_B, 256), s(OFF_C, 256), s(OFF_SM, 6),
        s(OFF_LX, 256), s(OFF_LG, 256), s(OFF_Q, 384), s(OFF_K, 384), s(OFF_V, 384),
        s(OFF_SM + FOX_LANE0, 6)], axis=-1)


def _pick(n, cands):
    for c in cands:
        if n % c == 0:
            return c
    return n


def mm(a, b, *, name, ta=False, tb=False, out_dtypes=(F32,), extras=(), col_params=(), partials=0, epilogue=None,
       tm=None, tn=None, tk=None, carried=None):
    bs = list(b) if isinstance(b, (list, tuple)) else [b]
    pair_sum = isinstance(a, (list, tuple))
    a_list = list(a) if pair_sum else [a]
    assert not pair_sum or len(a_list) == len(bs)
    a = a_list[0]
    n_a = len(a_list)
    n_acc = 1 if pair_sum else len(bs)
    extras = [e if isinstance(e, tuple) else (e, 0) for e in extras]
    M = a.shape[1] if ta else a.shape[0]
    K = a.shape[0] if ta else a.shape[1]
    N = bs[0].shape[0] if tb else bs[0].shape[1]
    tm = tm or _pick(M, (1024, 1408, 512, 256, 128))
    tn = tn or _pick(N, (1024, 1408, 768, 512, 256, 128))
    tk = tk or _pick(K, (1024, 1408, 512, 256, 128))
    nm, nn, nk = M // tm, N // tn, K // tk
    n_b, n_ex, n_cp, n_out = len(bs), len(extras), len(col_params), len(out_dtypes)
    a_bytes, b_bytes = n_a * M * K * a.dtype.itemsize, n_b * K * N * bs[0].dtype.itemsize
    rows_inner = a_bytes * nn + b_bytes <= a_bytes + b_bytes * nm

    def ij(g0, g1):
        return (g1, g0) if rows_inner else (g0, g1)

    def body(*rest):
        a_refs, rest = rest[:n_a], rest[n_a:]
        b_refs, rest = rest[:n_b], rest[n_b:]
        in_refs, rest = rest[:n_ex + n_cp], rest[n_ex + n_cp:]
        out_refs, accs = rest[:n_out + partials], rest[n_out + partials:]
        dn = (((0 if ta else 1,), (1 if tb else 0,)), ((), ()))
        dot = lambda x_ref, y_ref: lax.dot_general(x_ref[...].astype(BF16), y_ref[...].astype(BF16), dn,
                                                   preferred_element_type=F32)
        if pair_sum:
            parts = [functools.reduce(lambda u, v: u + v, [dot(x, y) for x, y in zip(a_refs, b_refs)])]
        else:
            parts = [dot(a_refs[0], b_ref) for b_ref in b_refs]

        def finish(rs):
            outs = epilogue(*rs, *[e[...] for e in in_refs]) if epilogue is not None else tuple(rs)
            for o_ref, o in zip(out_refs[:n_out], outs):
                o_ref[...] = o.astype(o_ref.dtype)
            for o_ref, o in zip(out_refs[n_out:], outs[n_out:]):
                o_ref[0] = o

        if nk == 1:
            finish(parts)
            return
        k = pl.program_id(2)

        @pl.when(k == 0)
        def _():
            for acc, part in zip(accs, parts):
                acc[...] = part

        @pl.when(k > 0)
        def _():
            for acc, part in zip(accs, parts):
                acc[...] += part

        @pl.when(k == nk - 1)
        def _():
            finish([acc[...] for acc in accs])

    def a_map(g0, g1, k):
        i, _ = ij(g0, g1)
        return (k, i) if ta else (i, k)

    def b_map(g0, g1, k):
        _, j = ij(g0, g1)
        return (j, k) if tb else (k, j)

    def ex_map(off, g0, g1, k):
        i, j = ij(g0, g1)
        return (i, j + off)

    a_spec = pl.BlockSpec((tk, tm) if ta else (tm, tk), a_map)
    b_spec = pl.BlockSpec((tn, tk) if tb else (tk, tn), b_map)
    mn_spec = pl.BlockSpec((tm, tn), functools.partial(ex_map, 0))
    comm = carried.take(name) if carried is not None else None
    call = _pc(body, name=name, grid=(nn, nm, nk) if rows_inner else (nm, nn, nk),
               in_specs=([a_spec] * n_a + [b_spec] * n_b
                         + [pl.BlockSpec((tm, tn), functools.partial(ex_map, off)) for _, off in extras]
                         + [pl.BlockSpec((1, tn), lambda g0, g1, k: (0, ij(g0, g1)[1]))] * n_cp),
               out_specs=([mn_spec] * n_out
                          + [pl.BlockSpec((1, 1, tn), lambda g0, g1, k: (ij(g0, g1)[0], 0, ij(g0, g1)[1]))] * partials),
               out_shape=[SDS((M, N), dt) for dt in out_dtypes] + [SDS((nm, 1, N), F32)] * partials,
               scratch=[pltpu.VMEM((tm, tn), F32)] * n_acc if nk > 1 else [],
               sem=("parallel", "parallel", "arbitrary"), comm=comm)
    outs = _run(call, (*a_list, *bs, *[e for e, _ in extras], *col_params), name, comm, carried)
    return outs[0] if len(outs) == 1 else outs


def rowwise(name, fn, rows, params, row_outs, acc_outs=(), tr=512, carried=None):
    rows = [r if isinstance(r, tuple) else (r, 0, r.shape[1]) for r in rows]
    T = rows[0][0].shape[0]
    tr = min(tr, T)
    n_in, n_ro, n_ac = len(rows) + len(params), len(row_outs), len(acc_outs)

    def body(*refs):
        ins, outs = refs[:n_in], refs[n_in:]
        res = fn(*[r[...] for r in ins])
        if not isinstance(res, (tuple, list)):
            res = (res,)
        for k in range(n_ro):
            outs[k][...] = res[k].astype(outs[k].dtype)
        if n_ac:
            i = pl.program_id(0)

            @pl.when(i == 0)
            def _():
                for k in range(n_ac):
                    outs[n_ro + k][...] = res[n_ro + k]

            @pl.when(i > 0)
            def _():
                for k in range(n_ac):
                    outs[n_ro + k][...] += res[n_ro + k]

    in_specs = ([pl.BlockSpec((tr, w), functools.partial(lambda cb, i: (i, cb), cb)) for (_, cb, w) in rows]
                + [pl.BlockSpec(p.shape, lambda i: (0, 0)) for p in params])
    out_specs = ([pl.BlockSpec((tr, c), lambda i: (i, 0)) for (c, _) in row_outs]
                 + [pl.BlockSpec((1, c), lambda i: (0, 0)) for c in acc_outs])
    out_shape = [SDS((T, c), dt) for (c, dt) in row_outs] + [SDS((1, c), F32) for c in acc_outs]
    comm = carried.take(name) if carried is not None else None
    call = _pc(body, name=name, grid=(T // tr,), in_specs=in_specs, out_specs=out_specs, out_shape=out_shape,
               sem=("arbitrary",) if n_ac else ("parallel",), comm=comm)
    outs = _run(call, (*[r[0] for r in rows], *params), name, comm, carried)
    return outs[0] if len(outs) == 1 else outs


def _rms(x, g):
    return x * lax.rsqrt(jnp.mean(x * x, axis=-1, keepdims=True) + EPS) * g


def _softplus(x):
    return jnp.maximum(x, 0.0) + jnp.log(1.0 + jnp.exp(-jnp.abs(x)))


def _silu(x):
    return x * jax.nn.sigmoid(x)


def _gelu(x):
    return 0.5 * x * (1.0 + jnp.tanh(math.sqrt(2.0 / math.pi) * (x + 0.044715 * (x * x * x))))


def _neg_expm1(x):
    series = x * (1 + x / 2 * (1 + x / 3 * (1 + x / 4 * (1 + x / 5 * (1 + x / 6 * (1 + x / 7))))))
    return -jnp.where(jnp.abs(x) < 0.3, series, jnp.exp(x) - 1.0)


def _swiglu(gu):
    return _silu(gu[:, :D_FF]) * gu[:, D_FF:]


def _ple(pg, pp, b):
    return jax.nn.sigmoid(pg + b) * pp


def _ssd_elt(small, xs_act, dtbias_row, alog_row):
    lane = _iota(small.shape, 1)
    dt = _softplus(small + dtbias_row)
    adt = jnp.where(lane < N_HEADS, -jnp.exp(alog_row) * dt, 0.0)
    head = _iota(xs_act.shape, 1) // HEAD_DIM
    dt_exp = jnp.zeros_like(xs_act)
    for h in range(N_HEADS):
        dth = jnp.sum(jnp.where(lane == h, dt, 0.0), axis=1, keepdims=True)
        dt_exp = dt_exp + jnp.where(head == h, dth, 0.0)
    return adt, xs_act * dt_exp


def _fox_elt(small, bf_row):
    lane = _iota(small.shape, 1)
    keep = (lane >= FOX_LANE0) & (lane < FOX_LANE0 + N_HEADS)
    return jnp.where(keep, -_softplus(-(small + bf_row)), 0.0)


def _lru_elt(xl, pre, b_ax, lam):
    r = jax.nn.sigmoid(pre[:, :LRU_W] + b_ax[:, :LRU_W])
    i = jax.nn.sigmoid(pre[:, LRU_W:] + b_ax[:, LRU_W:])
    log_a = -LRU_C * r * _softplus(-lam)
    a = jnp.exp(log_a)
    mult = jnp.sqrt(_neg_expm1(2.0 * log_a))
    return a, mult * (i * xl)


def _mix_post(yraw, xs_act, z, hl, lgate, yfox, dexp, g_ssd, g_lru, g_fox):
    y_ssd = _rms((yraw + xs_act * dexp) * _silu(z), g_ssd)
    y_lru = _rms(hl * _gelu(lgate), g_lru)
    y_fox = _rms(yfox, g_fox)
    return jnp.concatenate([y_ssd, y_lru, y_fox], axis=-1)


def _colsum(x):
    return jnp.sum(x, axis=0, keepdims=True)


def _shift_down(x, d):
    if d == 0:
        return x
    return jnp.where(_iota(x.shape, 0) >= d, pltpu.roll(x, d, 0), 0.0)


def _shift_up(x, d):
    if d == 0:
        return x
    s = x.shape[0]
    return jnp.where(_iota(x.shape, 0) < s - d, pltpu.roll(x, s - d, 0), 0.0)


def _conv_core(x, w, b):
    y = b + w[3:4, :] * x
    for k in range(3):
        y = y + w[k:k + 1, :] * _shift_down(x, 3 - k)
    return y


def seq_conv(name, src, col, width, w8, b, *, batch, silu, out_dtype):
    T = src.shape[0]
    S = T // batch
    c0 = col // LANE

    def body(x_ref, w_ref, b_ref, o_ref):
        y = _conv_core(x_ref[...], w_ref[...], b_ref[...])
        o_ref[...] = (_silu(y) if silu else y).astype(o_ref.dtype)

    return _pc(body, name=name, grid=(batch, width // LANE),
               in_specs=[pl.BlockSpec((S, LANE), lambda bi, ci: (bi, c0 + ci)),
                         pl.BlockSpec((8, LANE), lambda bi, ci: (0, ci)),
                         pl.BlockSpec((1, LANE), lambda bi, ci: (0, ci))],
               out_specs=pl.BlockSpec((S, LANE), lambda bi, ci: (bi, ci)),
               out_shape=SDS((T, width), out_dtype), sem=("parallel", "parallel"))(src, w8, b)


def seq_conv_bwd(name, src, col, width, w8, b, dy, *, batch, silu):
    T = src.shape[0]
    S = T // batch
    c0 = col // LANE

    def body(x_ref, w_ref, b_ref, dy_ref, dx_ref, dw_ref, db_ref):
        x, w = x_ref[...], w_ref[...]
        dpre = dy_ref[...].astype(F32)
        if silu:
            dpre = jax.vjp(_silu, _conv_core(x, w, b_ref[...]))[1](dpre)[0]
        dx = w[3:4, :] * dpre
        for k in range(3):
            dx = dx + w[k:k + 1, :] * _shift_up(dpre, 3 - k)
        dx_ref[...] = dx.astype(dx_ref.dtype)
        row8 = _iota((8, LANE), 0)
        dw = jnp.zeros((8, LANE), F32)
        for k in range(4):
            dw = dw + jnp.where(row8 == k, _colsum(dpre * _shift_down(x, 3 - k)), 0.0)
        db = _colsum(dpre)
        bi = pl.program_id(1)

        @pl.when(bi == 0)
        def _():
            dw_ref[...] = dw
            db_ref[...] = db

        @pl.when(bi > 0)
        def _():
            dw_ref[...] += dw
            db_ref[...] += db

    return _pc(body, name=name, grid=(width // LANE, batch),
               in_specs=[pl.BlockSpec((S, LANE), lambda ci, bi: (bi, c0 + ci)),
                         pl.BlockSpec((8, LANE), lambda ci, bi: (0, ci)),
                         pl.BlockSpec((1, LANE), lambda ci, bi: (0, ci)),
                         pl.BlockSpec((S, LANE), lambda ci, bi: (bi, ci))],
               out_specs=[pl.BlockSpec((S, LANE), lambda ci, bi: (bi, ci)),
                          pl.BlockSpec((8, LANE), lambda ci, bi: (0, ci)),
                          pl.BlockSpec((1, LANE), lambda ci, bi: (0, ci))],
               out_shape=[SDS((T, width), BF16), SDS((8, width), F32), SDS((1, width), F32)],
               sem=("parallel", "arbitrary"))(src, w8, b, dy)


def _split3_dot(tri, x):
    hi = x.astype(BF16)
    r1 = x - hi.astype(F32)
    mid = r1.astype(BF16)
    lo = (r1 - mid.astype(F32)).astype(BF16)
    d = lambda v: jnp.dot(tri, v, preferred_element_type=F32)
    return d(hi) + d(mid) + d(lo)


def seq_cumsum(name, x, *, batch, reverse=False, nsum=1, trow=None):
    T = x.shape[0]
    S = T // batch
    ch = min(256, S)
    nch = S // ch

    def body(x_ref, o_ref, *maybe_t):
        r, c = _iota((ch, ch), 0), _iota((ch, ch), 1)
        tri = jnp.where((c >= r) if reverse else (c <= r), 1.0, 0.0).astype(BF16)
        carry = jnp.zeros((1, LANE), F32)
        for k in (range(nch - 1, -1, -1) if reverse else range(nch)):
            xc = x_ref[k * ch:(k + 1) * ch, 0:LANE]
            for m in range(1, nsum):
                xc = xc + x_ref[k * ch:(k + 1) * ch, m * LANE:(m + 1) * LANE]
            o_ref[k * ch:(k + 1) * ch, :] = _split3_dot(tri, xc) + carry
            carry = carry + _colsum(xc)
        if trow is not None:
            maybe_t[0][...] = o_ref[...].T[trow:trow + 8, :]

    out_specs = [pl.BlockSpec((S, LANE), lambda bi: (bi, 0))]
    out_shape = [SDS((T, LANE), F32)]
    if trow is not None:
        out_specs.append(pl.BlockSpec((8, S), lambda bi: (bi, 0)))
        out_shape.append(SDS((batch * 8, S), F32))
    outs = _pc(body, name=name, grid=(batch,), in_specs=[pl.BlockSpec((S, LANE * nsum), lambda bi: (bi, 0))],
               out_specs=out_specs, out_shape=out_shape, sem=("parallel",))(x)
    return outs if trow is not None else outs[0]


_SCAN_SEQS = 2


def lru_scan(name, a, u, *, batch):
    T, W = a.shape
    S = T // batch
    nb = _SCAN_SEQS if batch % _SCAN_SEQS == 0 else 1

    def body(a_ref, u_ref, h_ref):
        row = _iota((8, W), 0)

        def step(g, hs):
            new = []
            for q, h in enumerate(hs):
                off = pl.multiple_of(q * S + g * 8, 8)
                at, ut = a_ref[pl.ds(off, 8), :], u_ref[pl.ds(off, 8), :]
                acc = jnp.zeros((8, W), F32)
                for r in range(8):
                    h = at[r:r + 1, :] * h + ut[r:r + 1, :]
                    acc = jnp.where(row == r, jnp.broadcast_to(h, (8, W)), acc)
                h_ref[pl.ds(off, 8), :] = acc
                new.append(h)
            return tuple(new)

        lax.fori_loop(0, S // 8, step, tuple(jnp.zeros((1, W), F32) for _ in range(nb)))

    spec = pl.BlockSpec((nb * S, W), lambda bi: (bi, 0))
    return _pc(body, name=name, grid=(batch // nb,), in_specs=[spec, spec], out_specs=spec,
               out_shape=SDS((T, W), F32), sem=("parallel",))(a, u)


def lru_scan_bwd(name, a, h, dh, *, batch):
    T, W = a.shape
    S = T // batch
    ng = S // 8
    nb = _SCAN_SEQS if batch % _SCAN_SEQS == 0 else 1

    def body(a_ref, h_ref, dh_ref, da_ref, du_ref):
        row = _iota((8, W), 0)

        def step(k, cs):
            g_idx = ng - 1 - k
            new = []
            for q, c in enumerate(cs):
                off = pl.multiple_of(q * S + g_idx * 8, 8)
                offp = pl.multiple_of(q * S + jnp.maximum(g_idx - 1, 0) * 8, 8)
                at, ht, dt = a_ref[pl.ds(off, 8), :], h_ref[pl.ds(off, 8), :], dh_ref[pl.ds(off, 8), :]
                hp = jnp.where(g_idx > 0, h_ref[pl.ds(offp, 8), :], 0.0)
                da = jnp.zeros((8, W), F32)
                du = jnp.zeros((8, W), F32)
                for r in range(7, -1, -1):
                    g = dt[r:r + 1, :] + c
                    hprev = ht[r - 1:r, :] if r > 0 else hp[7:8, :]
                    du = jnp.where(row == r, jnp.broadcast_to(g, (8, W)), du)
                    da = jnp.where(row == r, jnp.broadcast_to(g * hprev, (8, W)), da)
                    c = at[r:r + 1, :] * g
                da_ref[pl.ds(off, 8), :] = da
                du_ref[pl.ds(off, 8), :] = du
                new.append(c)
            return tuple(new)

        lax.fori_loop(0, ng, step, tuple(jnp.zeros((1, W), F32) for _ in range(nb)))

    spec = pl.BlockSpec((nb * S, W), lambda bi: (bi, 0))
    return _pc(body, name=name, grid=(batch // nb,), in_specs=[spec] * 3, out_specs=[spec] * 2,
               out_shape=[SDS((T, W), F32)] * 2, sem=("parallel",))(a, h, dh)


def _nt(a, b):
    return lax.dot_general(a, b, (((1,), (1,)), ((), ())), preferred_element_type=F32)


def _tn(a, b):
    return lax.dot_general(a, b, (((0,), (0,)), ((), ())), preferred_element_type=F32)


def _tile(S, t=256):
    return min(t, S)


def ssd_attn_fwd(name, cm, bm, xd, cum, cum_t, *, batch, carried=None):
    T = cm.shape[0]
    S = T // batch
    tq = tk = _tile(S)
    nq = S // tq

    def body(c_ref, b_ref, x_ref, cum_ref, cumt_ref, y_ref):
        i = pl.program_id(1)
        cq, cmq = cum_ref[...], c_ref[...]
        rowi, coli = _iota((tq, tk), 0), _iota((tq, tk), 1)
        half = _iota((tk, LANE), 1) // HEAD_DIM

        def step(j, accs, diag):
            off = pl.multiple_of(j * tk, tk)
            bj = b_ref[pl.ds(off, tk), :]
            gm = [_nt(cmq[:, g * LANE:(g + 1) * LANE], bj[:, g * LANE:(g + 1) * LANE]) for g in range(2)]
            ckt = cumt_ref[:, pl.ds(off, tk)]
            new = []
            for p in range(3):
                xp = x_ref[pl.ds(off, tk), p * LANE:(p + 1) * LANE]
                ws, xs = [], []
                for hh in range(2):
                    h = 2 * p + hh
                    seg = cq[:, h:h + 1] - ckt[h:h + 1, :]
                    e = jnp.exp(jnp.where(rowi >= coli, seg, -jnp.inf) if diag else seg)
                    ws.append((gm[h // 3] * e).astype(BF16))
                    xs.append(jnp.where(half == hh, xp, jnp.zeros_like(xp)))
                new.append(accs[p] + jnp.dot(jnp.concatenate(ws, axis=1), jnp.concatenate(xs, axis=0),
                                             preferred_element_type=F32))
            return tuple(new)

        accs = lax.fori_loop(0, i, functools.partial(step, diag=False),
                             tuple(jnp.zeros((tq, LANE), F32) for _ in range(3)))
        accs = step(i, accs, True)
        y_ref[...] = jnp.concatenate(accs, axis=1)

    comm = carried.take(name) if carried is not None else None
    call = _pc(body, name=name, grid=(batch, nq),
               in_specs=[pl.BlockSpec((tq, 256), lambda b, i: (b * nq + i, 0)),
                         pl.BlockSpec((S, 256), lambda b, i: (b, 0)),
                         pl.BlockSpec((S, SSD_W), lambda b, i: (b, 0)),
                         pl.BlockSpec((tq, LANE), lambda b, i: (b * nq + i, 0)),
                         pl.BlockSpec((8, S), lambda b, i: (b, 0))],
               out_specs=pl.BlockSpec((tq, SSD_W), lambda b, i: (b * nq + i, 0)),
               out_shape=SDS((T, SSD_W), F32), sem=("parallel", "parallel"), comm=comm)
    return _run(call, (cm, bm, xd, cum, cum_t), name, comm, carried)


def ssd_attn_bwd(name, cm, bm, xd, cum, cum_t, dy, *, batch, carried=None):
    T = cm.shape[0]
    S = T // batch
    tq = tk = _tile(S, 512)
    nq = S // tq

    def body(c_ref, b_ref, x_ref, cum_ref, cumt_ref, dy_ref, dx_ref, db_ref, dc_ref, dcum_ref, dcumt_ref):
        dx_ref[...] = jnp.zeros_like(dx_ref)
        db_ref[...] = jnp.zeros_like(db_ref)
        dcum_ref[...] = jnp.zeros_like(dcum_ref)
        dcumt_ref[...] = jnp.zeros_like(dcumt_ref)
        rowi, coli = _iota((tq, tk), 0), _iota((tq, tk), 1)
        halfq = _iota((tq, LANE), 1) // HEAD_DIM
        lane_q = _iota((tq, LANE), 1)

        def qblock(i, _):
            qoff = pl.multiple_of(i * tq, tq)
            cq = cum_ref[pl.ds(qoff, tq), :]
            cmq = c_ref[pl.ds(qoff, tq), :]
            dyq = dy_ref[pl.ds(qoff, tq), :]
            dyh = [[jnp.where(halfq == hh, dyq[:, p * LANE:(p + 1) * LANE], 0.0).astype(BF16) for hh in range(2)]
                   for p in range(3)]

            def step(j, carry, diag):
                dcq, rs_acc = carry
                off = pl.multiple_of(j * tk, tk)
                bj = b_ref[pl.ds(off, tk), :]
                gm = [_nt(cmq[:, g * LANE:(g + 1) * LANE], bj[:, g * LANE:(g + 1) * LANE]) for g in range(2)]
                ckt = cumt_ref[:, pl.ds(off, tk)]
                dgm = [jnp.zeros((tq, tk), F32), jnp.zeros((tq, tk), F32)]
                for p in range(3):
                    xp = x_ref[pl.ds(off, tk), p * LANE:(p + 1) * LANE]
                    ws = []
                    for hh in range(2):
                        h = 2 * p + hh
                        seg = cq[:, h:h + 1] - ckt[h:h + 1, :]
                        e = jnp.exp(jnp.where(rowi >= coli, seg, -jnp.inf) if diag else seg)
                        w = gm[h // 3] * e
                        dw = _nt(dyh[p][hh], xp)
                        zz = dw * w
                        rs_acc = rs_acc + jnp.where(lane_q == h, jnp.sum(zz, axis=1, keepdims=True), 0.0)
                        dcumt_ref[h:h + 1, pl.ds(off, tk)] += _colsum(zz)
                        dgm[h // 3] = dgm[h // 3] + dw * e
                        ws.append(w.astype(BF16))
                    dx_ref[pl.ds(off, tk), p * LANE:(p + 1) * LANE] += _tn(
                        jnp.concatenate(ws, axis=0), jnp.concatenate(dyh[p], axis=0))
                new_dcq = []
                for g in range(2):
                    dg = dgm[g].astype(BF16)
                    new_dcq.append(dcq[g] + jnp.dot(dg, bj[:, g * LANE:(g + 1) * LANE], preferred_element_type=F32))
                    db_ref[pl.ds(off, tk), g * LANE:(g + 1) * LANE] += _tn(dg, cmq[:, g * LANE:(g + 1) * LANE])
                return tuple(new_dcq), rs_acc

            carry = lax.fori_loop(
                0, i, functools.partial(step, diag=False),
                ((jnp.zeros((tq, LANE), F32), jnp.zeros((tq, LANE), F32)), jnp.zeros((tq, LANE), F32)))
            dcq, rs_acc = step(i, carry, True)
            dc_ref[pl.ds(qoff, tq), :] = jnp.concatenate(dcq, axis=1)
            dcum_ref[pl.ds(qoff, tq), :] += rs_acc
            return 0

        lax.fori_loop(0, nq, qblock, 0)
        dcum_ref[...] = dcum_ref[...] - dcumt_ref[...].T

    s256 = pl.BlockSpec((S, 256), lambda b: (b, 0))
    s384 = pl.BlockSpec((S, SSD_W), lambda b: (b, 0))
    s128 = pl.BlockSpec((S, LANE), lambda b: (b, 0))
    comm = carried.take(name) if carried is not None else None
    call = _pc(body, name=name, grid=(batch,),
               in_specs=[s256, s256, s384, s128, pl.BlockSpec((8, S), lambda b: (b, 0)), s384],
               out_specs=[s384, s256, s256, s128],
               out_shape=[SDS((T, SSD_W), F32), SDS((T, 256), F32), SDS((T, 256), F32), SDS((T, LANE), F32)],
               scratch=[pltpu.VMEM((LANE, S), F32)], sem=("parallel",), comm=comm)
    return _run(call, (cm, bm, xd, cum, cum_t, dy), name, comm, carried)


NEG_BIG = -1e30


def fox_attn_fwd(name, proj, cum, cum_t, *, batch, carried=None):
    T = proj.shape[0]
    S = T // batch
    tq = tk = _tile(S, 512)
    nq = S // tq
    scale = HEAD_DIM ** -0.5
    qb, kb, vb = OFF_Q // LANE, OFF_K // LANE, OFF_V // LANE

    def body(q_ref, k_ref, v_ref, cum_ref, cumt_ref, o_ref, lse_ref):
        p, i = pl.program_id(1), pl.program_id(2)
        cq = cum_ref[...]
        lane_q = _iota((tq, LANE), 1)
        halfq, halfk = lane_q // HEAD_DIM, _iota((tk, LANE), 1) // HEAD_DIM
        qs = q_ref[...] * scale
        qh = [jnp.where(halfq == hh, qs, 0.0).astype(BF16) for hh in range(2)]
        rowi, coli = _iota((tq, tk), 0), _iota((tq, tk), 1)
        cqh = [jnp.sum(jnp.where(lane_q == FOX_LANE0 + 2 * p + hh, cq, 0.0), axis=1, keepdims=True) for hh in range(2)]
        row8 = _iota((8, tk), 0)

        def step(j, carry, diag):
            ms, ls, acc = carry
            off = pl.multiple_of(j * tk, tk)
            kj = k_ref[pl.ds(off, tk), :].astype(BF16)
            vj = v_ref[pl.ds(off, tk), :].astype(BF16)
            ckt = cumt_ref[:, pl.ds(off, tk)]
            ps, vs, new_m, new_l, alphas = [], [], [], [], []
            for hh in range(2):
                ck = jnp.sum(jnp.where(row8 == 2 * p + hh, ckt, 0.0), axis=0, keepdims=True)
                logits = _nt(qh[hh], kj) + (cqh[hh] - ck)
                if diag:
                    logits = jnp.where(rowi >= coli, logits, -jnp.inf)
                m = jnp.maximum(ms[hh], jnp.max(logits, axis=1, keepdims=True))
                alpha = jnp.exp(ms[hh] - m)
                pr = jnp.exp(logits - m)
                new_m.append(m)
                new_l.append(alpha * ls[hh] + jnp.sum(pr, axis=1, keepdims=True))
                alphas.append(alpha)
                ps.append(pr.astype(BF16))
                vs.append(jnp.where(halfk == hh, vj, jnp.zeros_like(vj)))
            acc = acc * jnp.where(halfq == 0, alphas[0], alphas[1]) + jnp.dot(
                jnp.concatenate(ps, axis=1), jnp.concatenate(vs, axis=0), preferred_element_type=F32)
            return tuple(new_m), tuple(new_l), acc

        init = ((jnp.full((tq, 1), NEG_BIG, F32),) * 2, (jnp.zeros((tq, 1), F32),) * 2, jnp.zeros((tq, LANE), F32))
        ms, ls, acc = step(i, lax.fori_loop(0, i, functools.partial(step, diag=False), init), True)
        o_ref[...] = acc / jnp.where(halfq == 0, ls[0], ls[1])
        lse_ref[...] = (jnp.where(lane_q == 0, ms[0] + jnp.log(ls[0]), 0.0)
                        + jnp.where(lane_q == 1, ms[1] + jnp.log(ls[1]), 0.0))

    comm = carried.take(name) if carried is not None else None
    call = _pc(body, name=name, grid=(batch, 3, nq),
               in_specs=[pl.BlockSpec((tq, LANE), lambda b, p, i: (b * nq + i, qb + p)),
                         pl.BlockSpec((S, LANE), lambda b, p, i: (b, kb + p)),
                         pl.BlockSpec((S, LANE), lambda b, p, i: (b, vb + p)),
                         pl.BlockSpec((tq, LANE), lambda b, p, i: (b * nq + i, 0)),
                         pl.BlockSpec((8, S), lambda b, p, i: (b, 0))],
               out_specs=[pl.BlockSpec((tq, LANE), lambda b, p, i: (b * nq + i, p))] * 2,
               out_shape=[SDS((T, FOX_W), F32)] * 2, sem=("parallel", "parallel", "parallel"), comm=comm)
    return _run(call, (proj, proj, proj, cum, cum_t), name, comm, carried)


def fox_attn_bwd(name, proj, o, do, lse, cum, cum_t, *, batch, carried=None):
    T = proj.shape[0]
    S = T // batch
    tq = tk = _tile(S, 512)
    nq = S // tq
    scale = HEAD_DIM ** -0.5
    qb, kb, vb = OFF_Q // LANE, OFF_K // LANE, OFF_V // LANE

    def body(q_ref, k_ref, v_ref, o_ref, do_ref, lse_ref, cum_ref, cumt_ref,
             dq_ref, dk_ref, dv_ref, dcum_ref, dk_acc, dv_acc, dcumt_ref):
        p = pl.program_id(1)
        dk_acc[...] = jnp.zeros_like(dk_acc)
        dv_acc[...] = jnp.zeros_like(dv_acc)
        dcum_ref[...] = jnp.zeros_like(dcum_ref)
        dcumt_ref[...] = jnp.zeros_like(dcumt_ref)
        lane_q = _iota((tq, LANE), 1)
        halfq, halfk = lane_q // HEAD_DIM, _iota((tk, LANE), 1) // HEAD_DIM
        rowi, coli = _iota((tq, tk), 0), _iota((tq, tk), 1)
        row8 = _iota((8, tk), 0)

        def qblock(i, _):
            qoff = pl.multiple_of(i * tq, tq)
            cq = cum_ref[pl.ds(qoff, tq), :]
            qs = q_ref[pl.ds(qoff, tq), :] * scale
            doq = do_ref[pl.ds(qoff, tq), :]
            lse = lse_ref[pl.ds(qoff, tq), :]
            delta = doq * o_ref[pl.ds(qoff, tq), :]
            qh, doh, cqh, lseh, dlt = [], [], [], [], []
            for hh in range(2):
                qh.append(jnp.where(halfq == hh, qs, 0.0).astype(BF16))
                doh.append(jnp.where(halfq == hh, doq, 0.0).astype(BF16))
                cqh.append(jnp.sum(jnp.where(lane_q == FOX_LANE0 + 2 * p + hh, cq, 0.0), axis=1, keepdims=True))
                lseh.append(jnp.sum(jnp.where(lane_q == hh, lse, 0.0), axis=1, keepdims=True))
                dlt.append(jnp.sum(jnp.where(halfq == hh, delta, 0.0), axis=1, keepdims=True))

            def step(j, carry, diag):
                dq, rs = carry
                off = pl.multiple_of(j * tk, tk)
                kj = k_ref[pl.ds(off, tk), :].astype(BF16)
                vj = v_ref[pl.ds(off, tk), :].astype(BF16)
                ckt = cumt_ref[:, pl.ds(off, tk)]
                dss, prs, ks = [], [], []
                for hh in range(2):
                    ck = jnp.sum(jnp.where(row8 == 2 * p + hh, ckt, 0.0), axis=0, keepdims=True)
                    logits = _nt(qh[hh], kj) + ((cqh[hh] - lseh[hh]) - ck)
                    if diag:
                        logits = jnp.where(rowi >= coli, logits, -jnp.inf)
                    pr = jnp.exp(logits)
                    ds = pr * (_nt(doh[hh], vj) - dlt[hh])
                    rs = rs + jnp.where(lane_q == FOX_LANE0 + 2 * p + hh, jnp.sum(ds, axis=1, keepdims=True), 0.0)
                    cs = _colsum(ds)
                    dcumt_ref[0:8, pl.ds(off, tk)] += jnp.where(row8 == 2 * p + hh, cs, 0.0)
                    dss.append(ds.astype(BF16))
                    prs.append(pr.astype(BF16))
                    ks.append(jnp.where(halfk == hh, kj, jnp.zeros_like(kj)))
                dq = dq + jnp.dot(jnp.concatenate(dss, axis=1), jnp.concatenate(ks, axis=0), preferred_element_type=F32)
                dk_acc[pl.ds(off, tk), :] += _tn(jnp.concatenate(dss, axis=0), jnp.concatenate(qh, axis=0))
                dv_acc[pl.ds(off, tk), :] += _tn(jnp.concatenate(prs, axis=0), jnp.concatenate(doh, axis=0))
                return dq, rs

            carry = lax.fori_loop(0, i, functools.partial(step, diag=False),
                                  (jnp.zeros((tq, LANE), F32), jnp.zeros((tq, LANE), F32)))
            dq, rs = step(i, carry, True)
            dq_ref[pl.ds(qoff, tq), :] = (dq * scale).astype(dq_ref.dtype)
            dcum_ref[pl.ds(qoff, tq), :] += rs
            return 0

        lax.fori_loop(0, nq, qblock, 0)
        dk_ref[...] = dk_acc[...].astype(dk_ref.dtype)
        dv_ref[...] = dv_acc[...].astype(dv_ref.dtype)
        dct = dcumt_ref[...].T
        dcum_ref[...] = dcum_ref[...] - pltpu.roll(dct, FOX_LANE0, 1)

    sp = lambda c0: pl.BlockSpec((S, LANE), lambda b, p: (b, c0 + p))
    s0 = pl.BlockSpec((S, LANE), lambda b, p: (b, 0))
    comm = carried.take(name) if carried is not None else None
    call = _pc(body, name=name, grid=(batch, 3),
               in_specs=[sp(qb), sp(kb), sp(vb), sp(0), sp(0), sp(0), s0, pl.BlockSpec((8, S), lambda b, p: (b, 0))],
               out_specs=[sp(0)] * 4,
               out_shape=[SDS((T, FOX_W), BF16)] * 3 + [SDS((T, FOX_W), F32)],
               scratch=[pltpu.VMEM((S, LANE), F32), pltpu.VMEM((S, LANE), F32), pltpu.VMEM((LANE, S), F32)],
               sem=("parallel", "parallel"), comm=comm)
    return _run(call, (proj, proj, proj, o, do, lse, cum, cum_t), name, comm, carried)


def _row(v, width=None, at=0):
    v = v.astype(F32)
    width = width or v.shape[0]
    return jnp.pad(v, (at, width - at - v.shape[0]))[None, :]


def _pad8(w4):
    return jnp.pad(w4.astype(F32), ((0, 4), (0, 0)))


def _block_diag(w):
    eye = jnp.eye(w.shape[0], dtype=w.dtype)
    return (w[:, :, None, :] * eye[:, None, :, None]).reshape(LRU_W, LRU_W)


def prep_layer(f):
    cw, cb = f["ssd_conv_w"], f["ssd_conv_b"]
    return LazyDict(
        win=lambda: permute_in_cols(f["w_in"]), wout=lambda: f["w_out"],
        wg=lambda: f["w_gate_t"] if "w_gate_t" in f else f["w_gate"].T,
        wu=lambda: f["w_up_t"] if "w_up_t" in f else f["w_up"].T,
        wd=lambda: f["w_down"], wpg=lambda: f["w_ple_gate"], wpp=lambda: f["w_ple_proj"],
        wax=jnp.concatenate([_block_diag(f["lru_w_a"]), _block_diag(f["lru_w_x"])], axis=1),
        g1=_row(f["norm1_g"]), g2=_row(f["norm2_g"]), g3=_row(f["norm3_g"]),
        cw_xs=_pad8(cw[:, :384]), cb_xs=_row(cb[:384]), cw_b=_pad8(cw[:, 384:640]), cb_b=_row(cb[384:640]),
        cw_c=_pad8(cw[:, 640:]), cb_c=_row(cb[640:]), cw_l=_pad8(f["lru_conv_w"]), cb_l=_row(f["lru_conv_b"]),
        dtbias_row=_row(f["ssd_dt_bias"], LANE), alog_row=_row(f["ssd_a_log"], LANE),
        dexp=jnp.repeat(f["ssd_d"].astype(F32), HEAD_DIM)[None, :], g_ssd=_row(f["ssd_norm_g"]),
        b_ax=_row(jnp.concatenate([f["lru_b_a"], f["lru_b_x"]])), lam=_row(f["lru_lambda"]), g_lru=_row(f["lru_norm_g"]),
        bf_row=_row(f["fox_b_f"], LANE, FOX_LANE0), g_fox=_row(f["fox_norm_g"]), b_pg=_row(f["b_ple_gate"]))


def unprep_grads(g):
    blocks = lambda m: jnp.stack([m[i * 64:(i + 1) * 64, i * 64:(i + 1) * 64] for i in range(4)])
    return dict(
        norm1_g=g["g1"][0], w_in=unpermute_in_cols(g["win"]),
        ssd_conv_w=jnp.concatenate([g["cw_xs"][:4], g["cw_b"][:4], g["cw_c"][:4]], axis=1),
        ssd_conv_b=jnp.concatenate([g["cb_xs"][0], g["cb_b"][0], g["cb_c"][0]]),
        ssd_dt_bias=g["dtbias_row"][0, :N_HEADS], ssd_a_log=g["alog_row"][0, :N_HEADS],
        ssd_d=jnp.sum(g["dexp"].reshape(N_HEADS, HEAD_DIM), axis=1), ssd_norm_g=g["g_ssd"][0],
        lru_conv_w=g["cw_l"][:4], lru_conv_b=g["cb_l"][0],
        lru_w_a=blocks(g["wax"][:, :LRU_W]), lru_b_a=g["b_ax"][0, :LRU_W],
        lru_w_x=blocks(g["wax"][:, LRU_W:]), lru_b_x=g["b_ax"][0, LRU_W:],
        lru_lambda=g["lam"][0], lru_norm_g=g["g_lru"][0],
        fox_b_f=g["bf_row"][0, FOX_LANE0:FOX_LANE0 + N_HEADS], fox_norm_g=g["g_fox"][0],
        w_out=g["wout"], norm2_g=g["g2"][0], w_gate=g["wg"].T, w_up=g["wu"].T, w_down=g["wd"],
        norm3_g=g["g3"][0], w_ple_gate=g["wpg"], b_ple_gate=g["b_pg"][0], w_ple_proj=g["wpp"])


def _view(a, off, width):
    return (a, off // width, width)


def _add_epilogue(acc, e):
    return (acc + e,)


def mixer_fwd(proj, w, batch, tag, carried=None):
    sm = _view(proj, OFF_SM, LANE)
    conv = functools.partial(seq_conv, batch=batch)
    cmc = conv(f"{tag}_conv_c", proj, OFF_C, 256, w["cw_c"], w["cb_c"], silu=True, out_dtype=BF16)
    bmc = conv(f"{tag}_conv_b", proj, OFF_B, 256, w["cw_b"], w["cb_b"], silu=True, out_dtype=BF16)
    xs_act = conv(f"{tag}_conv_xs", proj, OFF_XS, SSD_W, w["cw_xs"], w["cb_xs"], silu=True, out_dtype=F32)
    xl = conv(f"{tag}_conv_l", proj, OFF_LX, LRU_W, w["cw_l"], w["cb_l"], silu=False, out_dtype=F32)
    adt, xd = rowwise(f"{tag}_ssd_elt", _ssd_elt, [sm, xs_act], [w["dtbias_row"], w["alog_row"]],
                      [(LANE, F32), (SSD_W, BF16)])
    cum_a, cum_at = seq_cumsum(f"{tag}_cum_a", adt, batch=batch, trow=0)
    yraw = ssd_attn_fwd(f"{tag}_ssd_fwd", cmc, bmc, xd, cum_a, cum_at, batch=batch, carried=carried)
    logf = rowwise(f"{tag}_fox_elt", _fox_elt, [sm], [w["bf_row"]], [(LANE, F32)])
    cum_f, cum_ft = seq_cumsum(f"{tag}_cum_f", logf, batch=batch, trow=FOX_LANE0)
    o, lse = fox_attn_fwd(f"{tag}_fox_fwd", proj, cum_f, cum_ft, batch=batch, carried=carried)
    pre = mm(xl, w["wax"], name=f"{tag}_mm_lru_gates")
    a, u = rowwise(f"{tag}_lru_elt", _lru_elt, [xl, pre], [w["b_ax"], w["lam"]], [(LRU_W, F32), (LRU_W, F32)])
    hl = lru_scan(f"{tag}_lru_scan", a, u, batch=batch)
    ycat = rowwise(f"{tag}_mix_post", _mix_post,
                   [yraw, xs_act, _view(proj, OFF_Z, SSD_W), hl, _view(proj, OFF_LG, LRU_W), o],
                   [w["dexp"], w["g_ssd"], w["g_lru"], w["g_fox"]], [(D_MODEL, BF16)], tr=256)
    saved = dict(cmc=cmc, bmc=bmc, xs_act=xs_act, xl=xl, xd=xd, cum_a=cum_a, cum_at=cum_at, yraw=yraw,
                 cum_f=cum_f, cum_ft=cum_ft, o=o, lse=lse, pre=pre, a=a, hl=hl)
    return ycat, saved


def mixer_bwd(dycat, proj, w, s, batch, tag, carried=None):
    sm = _view(proj, OFF_SM, LANE)
    g = {}

    def post_bwd(yraw, xs_act, z, hl, lg, o, dyc, dexp, g_ssd, g_lru, g_fox):
        return jax.vjp(_mix_post, yraw, xs_act, z, hl, lg, o, dexp, g_ssd, g_lru, g_fox)[1](dyc)

    (dyraw, dxs1, dz, dhl, dlg, do, g["dexp"], g["g_ssd"], g["g_lru"], g["g_fox"]) = rowwise(
        f"{tag}_mix_post_bwd", post_bwd,
        [s["yraw"], s["xs_act"], _view(proj, OFF_Z, SSD_W), s["hl"], _view(proj, OFF_LG, LRU_W), s["o"], dycat],
        [w["dexp"], w["g_ssd"], w["g_lru"], w["g_fox"]],
        [(SSD_W, F32), (SSD_W, F32), (SSD_W, BF16), (LRU_W, F32), (LRU_W, BF16), (FOX_W, F32)],
        [SSD_W, SSD_W, LRU_W, FOX_W], tr=256)

    dq, dk, dv, dcum3 = fox_attn_bwd(f"{tag}_fox_bwd", proj, s["o"], do, s["lse"], s["cum_f"], s["cum_ft"], batch=batch,
                                     carried=carried)
    dlogf = seq_cumsum(f"{tag}_rcum_f", dcum3, batch=batch, reverse=True, nsum=3)

    dxd, dbm, dcm, dcum_a = ssd_attn_bwd(f"{tag}_ssd_bwd", s["cmc"], s["bmc"], s["xd"], s["cum_a"], s["cum_at"], dyraw,
                                         batch=batch, carried=carried)
    dadt = seq_cumsum(f"{tag}_rcum_a", dcum_a, batch=batch, reverse=True)

    def ssd_elt_bwd(small, xs_act, dadt_, dxd_, dxs1_, dtbias, alog):
        dsm, dxs, ddtb, dalog = jax.vjp(_ssd_elt, small, xs_act, dtbias, alog)[1]((dadt_, dxd_))
        return dsm, dxs + dxs1_, ddtb, dalog

    dsm_s, dxs_act, g["dtbias_row"], g["alog_row"] = rowwise(
        f"{tag}_ssd_elt_bwd", ssd_elt_bwd, [sm, s["xs_act"], dadt, dxd, dxs1], [w["dtbias_row"], w["alog_row"]],
        [(LANE, F32), (SSD_W, F32)], [LANE, LANE])

    def fox_elt_bwd(small, dlogf_, dsm_s_, bf_row):
        dsm, dbf = jax.vjp(_fox_elt, small, bf_row)[1](dlogf_)
        return dsm + dsm_s_, dbf

    dsm, g["bf_row"] = rowwise(f"{tag}_fox_elt_bwd", fox_elt_bwd, [sm, dlogf, dsm_s], [w["bf_row"]],
                               [(LANE, BF16)], [LANE])

    cbwd = functools.partial(seq_conv_bwd, batch=batch)
    dxs_raw, g["cw_xs"], g["cb_xs"] = cbwd(f"{tag}_conv_xs_bwd", proj, OFF_XS, SSD_W, w["cw_xs"], w["cb_xs"], dxs_act, silu=True)
    db_raw, g["cw_b"], g["cb_b"] = cbwd(f"{tag}_conv_b_bwd", proj, OFF_B, 256, w["cw_b"], w["cb_b"], dbm, silu=True)
    dc_raw, g["cw_c"], g["cb_c"] = cbwd(f"{tag}_conv_c_bwd", proj, OFF_C, 256, w["cw_c"], w["cb_c"], dcm, silu=True)

    da, du = lru_scan_bwd(f"{tag}_lru_scan_bwd", s["a"], s["hl"], dhl, batch=batch)

    def lru_elt_bwd(xl, pre, da_, du_, b_ax, lam):
        return jax.vjp(_lru_elt, xl, pre, b_ax, lam)[1]((da_, du_))

    dxl1, dpre, g["b_ax"], g["lam"] = rowwise(
        f"{tag}_lru_elt_bwd", lru_elt_bwd, [s["xl"], s["pre"], da, du], [w["b_ax"], w["lam"]],
        [(LRU_W, F32), (2 * LRU_W, BF16)], [2 * LRU_W, LRU_W])
    g["wax"] = mm(s["xl"], dpre, ta=True, name=f"{tag}_mm_dwax")
    dxl = mm(dpre, w["wax"], tb=True, extras=[dxl1], epilogue=_add_epilogue, name=f"{tag}_mm_dxl")
    dlx_raw, g["cw_l"], g["cb_l"] = cbwd(f"{tag}_conv_l_bwd", proj, OFF_LX, LRU_W, w["cw_l"], w["cb_l"], dxl, silu=False)

    dproj = jnp.concatenate([db_raw, dc_raw, dlx_raw, dlg, dsm, dz, dxs_raw, dq, dk, dv], axis=1)
    return dproj, g


def layer_fwd(h0, p_l, w, batch, tag, carried=None):
    u1 = rowwise(f"{tag}_rms1", _rms, [h0], [w["g1"]], [(D_MODEL, BF16)], carried=carried)
    proj = mm(u1, w["win"], name=f"{tag}_mm_in", carried=carried)
    ycat, ms = mixer_fwd(proj, w, batch, tag, carried)
    add_norm = dict(epilogue=_add_rms_epilogue, out_dtypes=(F32, BF16), tm=512, tn=D_MODEL)
    h1, u2 = mm(ycat, w["wout"], extras=[h0], col_params=[w["g2"]], name=f"{tag}_mm_out", **add_norm)
    gate, up, act = mm(u2, [w["wg"], w["wu"]], tb=True, out_dtypes=(BF16, BF16, BF16), epilogue=_swiglu_epilogue,
                       tm=512, tn=D_FF // 2, name=f"{tag}_mm_gu", carried=carried)
    h2, u3 = mm(act, w["wd"], extras=[h1], col_params=[w["g3"]], tk=D_FF, name=f"{tag}_mm_down", **add_norm)
    pp = mm(p_l, w["wpp"], name=f"{tag}_mm_pp")
    h3, pg = mm(u3, w["wpg"], extras=[pp, h2], col_params=[w["b_pg"]], epilogue=_ple_epilogue,
                out_dtypes=(F32, F32), tm=512, name=f"{tag}_mm_pg")
    saved = dict(h0=h0, u1=u1, proj=proj, ycat=ycat, h1=h1, u2=u2, gate=gate, up=up, act=act, h2=h2, u3=u3, pg=pg,
                 pp=pp, mixer=ms)
    return h3, saved


def _add_rms_epilogue(acc, res, g):
    h = res + acc
    return h, _rms(h, g)


def _swiglu_epilogue(acc_g, acc_u):
    return acc_g, acc_u, _silu(acc_g) * acc_u


def _swiglu_bwd_epilogue(dact, gate, up):
    return jax.vjp(lambda g_, u_: _silu(g_) * u_, gate.astype(F32), up.astype(F32))[1](dact)


def _ple_epilogue(acc, pp, h2, b):
    return h2 + _ple(acc, pp, b), acc


def _rms_bwd_epilogue(du, h, dres, g):
    r = lax.rsqrt(jnp.mean(h * h, axis=-1, keepdims=True) + EPS)
    n = h * r
    t = du * n
    dh = r * (du * g - n * jnp.mean(t * g, axis=-1, keepdims=True))
    return dh + dres, _colsum(t)


def layer_bwd(dh3, p_l, w, s, batch, tag, carried=None, on_early_grads=None, on_w_in_grad=None):
    def ple_bwd(pg, pp, dh, b):
        return jax.vjp(_ple, pg, pp, b)[1](dh)

    norm_bwd = dict(epilogue=_rms_bwd_epilogue, partials=1, tm=512, tn=D_MODEL, tb=True)

    d_pg, d_pp, g_bpg = rowwise(f"{tag}_ple_bwd", ple_bwd, [s["pg"], s["pp"], dh3], [w["b_pg"]],
                                [(D_MODEL, BF16), (D_MODEL, BF16)], [D_MODEL])
    g = dict(b_pg=g_bpg)
    g["wpp"] = mm(p_l, d_pp, ta=True, name=f"{tag}_mm_dwpp")
    g["wpg"] = mm(s["u3"], d_pg, ta=True, name=f"{tag}_mm_dwpg", carried=carried)
    dh2, dg3 = mm(d_pg, w["wpg"], extras=[s["h2"], dh3], col_params=[w["g3"]], name=f"{tag}_mm_du3", **norm_bwd)
    g["g3"] = sum_slices(f"{tag}_sum_dg3", dg3)

    d_gate, d_up = mm(dh2, w["wd"], tb=True, extras=[s["gate"], s["up"]], epilogue=_swiglu_bwd_epilogue,
                      out_dtypes=(BF16, BF16), tm=512, tn=D_FF // 2, name=f"{tag}_mm_dact")
    g["wd"] = mm(s["act"], dh2, ta=True, name=f"{tag}_mm_dwd")
    g["wg"] = mm(d_gate, s["u2"], ta=True, name=f"{tag}_mm_dwg")
    g["wu"] = mm(d_up, s["u2"], ta=True, name=f"{tag}_mm_dwu")
    dh1, dg2 = mm([d_gate, d_up], [w["wg"], w["wu"]], extras=[s["h1"], dh2], col_params=[w["g2"]],
                  name=f"{tag}_mm_du2", **{**norm_bwd, "tb": False})
    g["g2"] = sum_slices(f"{tag}_sum_dg2", dg2)

    dycat = mm(dh1, w["wout"], tb=True, name=f"{tag}_mm_dycat")
    g["wout"] = mm(s["ycat"], dh1, ta=True, name=f"{tag}_mm_dwout")
    if on_early_grads is not None:
        on_early_grads(g)
    dproj, gm = mixer_bwd(dycat, s["proj"], w, s["mixer"], batch, tag, carried)
    g.update(gm)
    g["win"] = mm(s["u1"], dproj, ta=True, name=f"{tag}_mm_dwin")
    if on_w_in_grad is not None:
        on_w_in_grad(g["win"])
    dh0, dg1 = mm(dproj, w["win"], extras=[s["h0"], dh1], col_params=[w["g1"]], name=f"{tag}_mm_du1",
                  carried=carried, **norm_bwd)
    g["g1"] = sum_slices(f"{tag}_sum_dg1", dg1)
    return dh0, g


def _loss_fwd_bwd(h, tgt, gf):
    def f(h_, gf_):
        e = _rms(h_, gf_) - tgt
        return 0.5 * jnp.sum(jnp.mean(e * e, axis=-1, keepdims=True), axis=0, keepdims=True)

    loss, vj = jax.vjp(f, h, gf)
    dh, dgf = vj(jnp.ones((1, 1), F32))
    return dh, jnp.broadcast_to(loss, (1, LANE)), dgf


def local_step(x, p, tgt, layers, final_g, carried=None, on_layer_grads=None, on_early_grads=None, on_w_in_grad=None):
    batch, S, _ = x.shape
    T = batch * S
    h = x.reshape(T, D_MODEL)
    saved, weights = [], []
    for l, w in enumerate(layers):
        w = w() if callable(w) else w
        weights.append(w)
        h, s = layer_fwd(h, p[l].reshape(T, PLE_DIM), w, batch, f"l{l}", carried)
        saved.append(s)
    dh, loss, dgf = rowwise("loss", _loss_fwd_bwd, [h, tgt.reshape(T, D_MODEL)], [_row(final_g)],
                            [(D_MODEL, F32)], [LANE, D_MODEL], tr=256)
    grads = [None] * len(layers)
    for l in reversed(range(len(layers))):
        early = functools.partial(on_early_grads, l) if on_early_grads is not None else None
        w_in_hook = functools.partial(on_w_in_grad, l) if on_w_in_grad is not None else None
        dh, grads[l] = layer_bwd(dh, p[l].reshape(T, PLE_DIM), weights[l], saved[l], batch, f"l{l}", carried, early,
                                 w_in_hook)
        if on_layer_grads is not None:
            on_layer_grads(l, grads[l])
    return loss[0, 0], dh.reshape(batch, S, D_MODEL), grads, dgf[0]


MESH = pl.DeviceIdType.MESH
N_DEV = 8
N_CHIP = 4
ANY = pl.BlockSpec(memory_space=pl.ANY)


def _pos():
    return lax.axis_index("x"), lax.axis_index("y"), lax.axis_index("c")


def _comm_call(body, name, out_shape, n_in, scratch):
    return pl.pallas_call(body, name=name, out_shape=out_shape, in_specs=[ANY] * n_in, out_specs=ANY,
                          scratch_shapes=scratch)


def all_gather8(name, blk):
    def body(x_ref, out_ref, send_sems, recv_sems, local_sem):
        x, y, c = _pos()
        me, sibling = (x, y, c), (x, y, 1 - c)
        chips = [(1 - x, y), (x, 1 - y), (1 - x, 1 - y)]

        def rows(px, py, pcore):
            return out_ref.at[4 * px + 2 * py + pcore]

        def copy(k, block, to, src=None):
            return pltpu.make_async_remote_copy(
                src_ref=rows(*block) if src is None else src, dst_ref=rows(*block),
                send_sem=send_sems.at[k], recv_sem=recv_sems.at[k], device_id=to, device_id_type=MESH)

        mine = pltpu.make_async_copy(x_ref, rows(*me), local_sem)
        mine.start()
        first = [copy(0, me, sibling, src=x_ref)]
        first += [copy(1 + j, me, (*chip, c), src=x_ref) for j, chip in enumerate(chips)]
        for cp in first:
            cp.start()
        passed = [copy(4 + j, (*chip, c), sibling) for j, chip in enumerate(chips)]
        for j, chip in enumerate(chips):
            copy(1 + j, (*chip, c), me).wait_recv()
            passed[j].start()
        copy(0, sibling, me).wait_recv()
        for j, chip in enumerate(chips):
            copy(4 + j, (*chip, 1 - c), me).wait_recv()
        for cp in first + passed:
            cp.wait_send()
        mine.wait()

    return _comm_call(body, name, SDS((N_DEV,) + blk.shape, blk.dtype), 1,
                      [pltpu.SemaphoreType.DMA((7,)), pltpu.SemaphoreType.DMA((7,)), pltpu.SemaphoreType.DMA])(blk)


class Exchange:
    def __init__(self, inputs, out_shapes, sems, start, wait, aliases=None):
        self.inputs, self.out_shapes, self.sems = list(inputs), list(out_shapes), list(sems)
        self.start, self.wait, self.aliases = start, wait, dict(aliases or {})


def combine(a, b):
    ai, ao, as_ = len(a.inputs), len(a.out_shapes), len(a.sems)

    def split(cins, couts, sems):
        return (cins[:ai], couts[:ao], sems[:as_]), (cins[ai:], couts[ao:], sems[as_:])

    def start(cins, couts, sems):
        pa, pb = split(cins, couts, sems)
        a.start(*pa)
        b.start(*pb)

    def wait(cins, couts, sems):
        pa, pb = split(cins, couts, sems)
        a.wait(*pa)
        b.wait(*pb)

    aliases = dict(a.aliases)
    aliases.update({ai + i: ao + o for i, o in b.aliases.items()})
    return Exchange(a.inputs + b.inputs, a.out_shapes + b.out_shapes, a.sems + b.sems, start, wait, aliases)


def run_exchange(name, ex):
    n_ci, n_co = len(ex.inputs), len(ex.out_shapes)

    def body(*refs):
        cins, couts, csems = refs[:n_ci], refs[n_ci:n_ci + n_co], refs[n_ci + n_co:]
        ex.start(cins, couts, csems)
        ex.wait(cins, couts, csems)

    return pl.pallas_call(body, name=name, out_shape=ex.out_shapes, in_specs=[ANY] * n_ci, out_specs=[ANY] * n_co,
                          scratch_shapes=ex.sems, input_output_aliases=ex.aliases)(*ex.inputs)


def _peers():
    x, y, c = _pos()
    return x, y, c, 2 * x + y, [(1 - x, y), (x, 1 - y), (1 - x, 1 - y)]


def _remote(src, dst, send_sem, recv_sem, to):
    return pltpu.make_async_remote_copy(src_ref=src, dst_ref=dst, send_sem=send_sem, recv_sem=recv_sem,
                                        device_id=to, device_id_type=MESH)


def gather_spread(shards, layer):
    n_t = len(shards)
    halves = [s.shape[1] // 2 for s in shards]

    def copies(cins, couts, sems):
        send_sems, recv_sems, local_sems = sems
        x, y, c, my_chip, chips = _peers()
        local, sends, recvs = [], [], []
        for t in range(n_t):
            h = halves[t]
            src = cins[t].at[layer, pl.ds(c * h, h)]
            mine = couts[t].at[my_chip, pl.ds(c * h, h)]
            local.append(pltpu.make_async_copy(src, mine, local_sems.at[t]))
            sends.append(_remote(src, mine, send_sems.at[0, t], recv_sems.at[0, t], (x, y, 1 - c)))
            recvs.append(_remote(src, couts[t].at[my_chip, pl.ds((1 - c) * h, h)], send_sems.at[0, t],
                                 recv_sems.at[0, t], (x, y, 1 - c)))
            for j, (px, py) in enumerate(chips):
                sends.append(_remote(src, mine, send_sems.at[1 + j, t], recv_sems.at[1 + j, t], (px, py, c)))
                recvs.append(_remote(src, couts[t].at[2 * px + py, pl.ds(c * h, h)], send_sems.at[1 + j, t],
                                     recv_sems.at[1 + j, t], (px, py, c)))
        return local, sends, recvs

    def start(cins, couts, sems):
        local, sends, _ = copies(cins, couts, sems)
        for cp in local + sends:
            cp.start()

    def wait(cins, couts, sems):
        local, sends, recvs = copies(cins, couts, sems)
        for cp in recvs:
            cp.wait_recv()
        for cp in sends:
            cp.wait_send()
        for cp in local:
            cp.wait()

    return Exchange(shards, [SDS((N_CHIP,) + s.shape[1:], s.dtype) for s in shards],
                    [pltpu.SemaphoreType.DMA((4, n_t)), pltpu.SemaphoreType.DMA((4, n_t)),
                     pltpu.SemaphoreType.DMA((n_t,))], start, wait)


def gather_pass_on(slots):
    n_t = len(slots)
    halves = [s.shape[1] // 2 for s in slots]

    def copies(cins, couts, sems):
        send_sems, recv_sems = sems
        x, y, c, my_chip, chips = _peers()
        sends, recvs = [], []
        for t in range(n_t):
            h = halves[t]
            for j, (px, py) in enumerate(chips):
                k = 2 * px + py
                sends.append(_remote(cins[t].at[k, pl.ds(c * h, h)], couts[t].at[k, pl.ds(c * h, h)],
                                     send_sems.at[j, t], recv_sems.at[j, t], (x, y, 1 - c)))
                recvs.append(_remote(cins[t].at[k, pl.ds(c * h, h)], couts[t].at[k, pl.ds((1 - c) * h, h)],
                                     send_sems.at[j, t], recv_sems.at[j, t], (x, y, 1 - c)))
        return sends, recvs

    def start(cins, couts, sems):
        for cp in copies(cins, couts, sems)[0]:
            cp.start()

    def wait(cins, couts, sems):
        sends, recvs = copies(cins, couts, sems)
        for cp in recvs:
            cp.wait_recv()
        for cp in sends:
            cp.wait_send()

    return Exchange(slots, [SDS(s.shape, s.dtype) for s in slots],
                    [pltpu.SemaphoreType.DMA((3, n_t)), pltpu.SemaphoreType.DMA((3, n_t))], start, wait,
                    aliases={t: t for t in range(n_t)})


def chips_exchange(vs):
    n_t = len(vs)

    def copies(cins, couts, sems):
        send_sems, recv_sems, local_sems = sems
        x, y, c, my_chip, chips = _peers()
        local = [pltpu.make_async_copy(cins[t].at[my_chip], couts[t].at[my_chip], local_sems.at[t]) for t in range(n_t)]
        sends, recvs = [], []
        for k, (px, py) in enumerate(chips):
            for t in range(n_t):
                sends.append(_remote(cins[t].at[2 * px + py], couts[t].at[my_chip], send_sems.at[k, t],
                                     recv_sems.at[k, t], (px, py, c)))
                recvs.append(_remote(cins[t].at[my_chip], couts[t].at[2 * px + py], send_sems.at[k, t],
                                     recv_sems.at[k, t], (px, py, c)))
        return local, sends, recvs

    def start(cins, couts, sems):
        local, sends, _ = copies(cins, couts, sems)
        for cp in local + sends:
            cp.start()

    def wait(cins, couts, sems):
        local, sends, recvs = copies(cins, couts, sems)
        for cp in recvs:
            cp.wait_recv()
        for cp in sends:
            cp.wait_send()
        for cp in local:
            cp.wait()

    return Exchange(vs, [SDS(v.shape, v.dtype) for v in vs],
                    [pltpu.SemaphoreType.DMA((3, n_t)), pltpu.SemaphoreType.DMA((3, n_t)),
                     pltpu.SemaphoreType.DMA((n_t,))], start, wait)


def sibling_exchange(vs):
    n_t = len(vs)

    def copies(cins, couts, sems):
        x, y, c = _pos()
        return [_remote(cins[t], couts[t], sems[0].at[t], sems[1].at[t], (x, y, 1 - c)) for t in range(n_t)]

    def start(cins, couts, sems):
        for cp in copies(cins, couts, sems):
            cp.start()

    def wait(cins, couts, sems):
        for cp in copies(cins, couts, sems):
            cp.wait()

    return Exchange(vs, [SDS(v.shape, v.dtype) for v in vs],
                    [pltpu.SemaphoreType.DMA((n_t,)), pltpu.SemaphoreType.DMA((n_t,))], start, wait)


def swap_with_sibling(name, vs):
    return run_exchange(name, sibling_exchange(vs))


_ROW_BLOCKS = (1024, 704, 512, 352, 256, 128, 64, 32, 16, 8)


def sum_slices(name, v, tr=512):
    n, R, C = v.shape
    tr = _pick(R, _ROW_BLOCKS)

    def body(v_ref, o_ref):
        acc = v_ref[0].astype(F32)
        for k in range(1, n):
            acc = acc + v_ref[k].astype(F32)
        o_ref[...] = acc

    return _pc(body, name=name, grid=(R // tr,), in_specs=[pl.BlockSpec((n, tr, C), lambda i: (0, i, 0))],
               out_specs=pl.BlockSpec((tr, C), lambda i: (i, 0)), out_shape=SDS((R, C), F32), sem=("parallel",))(v)


def add_slices(name, a, b, out_dtype):
    n, R, C = a.shape
    tr = _pick(R, _ROW_BLOCKS)

    def body(a_ref, b_ref, o_ref):
        o_ref[...] = (a_ref[...].astype(F32) + b_ref[...].astype(F32)).astype(o_ref.dtype)

    spec = pl.BlockSpec((1, tr, C), lambda k, i: (k, i, 0))
    return _pc(body, name=name, grid=(n, R // tr), in_specs=[spec, spec], out_specs=spec,
               out_shape=SDS(a.shape, out_dtype), sem=("parallel", "parallel"))(a, b)


def adamw(name, w, g, m, v):
    L, R, C = w.shape
    tr = _pick(R, (512, 352, 256, 128, 64, 32, 16, 8))
    c1 = 1.0 / (1.0 - ADAM_B1 ** ADAM_STEP)
    c2 = 1.0 / (1.0 - ADAM_B2 ** ADAM_STEP)

    def body(w_ref, g_ref, m_ref, v_ref, d_ref, nm_ref, nv_ref):
        gv = g_ref[...]
        nm = ADAM_B1 * m_ref[...] + (1.0 - ADAM_B1) * gv
        nv = ADAM_B2 * v_ref[...] + (1.0 - ADAM_B2) * (gv * gv)
        d_ref[...] = -ADAM_LR * ((nm * c1) / (jnp.sqrt(nv * c2) + ADAM_EPS) + ADAM_WD * w_ref[...])
        nm_ref[...] = nm
        nv_ref[...] = nv

    if R < 8:
        tl = 32
        spec = pl.BlockSpec((tl, R, C), lambda i, _: (i, 0, 0))
        grid = (pl.cdiv(L, tl), 1)
    else:
        spec = pl.BlockSpec((1, tr, C), lambda l, i: (l, i, 0))
        grid = (L, R // tr)
    return _pc(body, name=name, grid=grid, in_specs=[spec] * 4, out_specs=[spec] * 3,
               out_shape=[SDS(w.shape, F32)] * 3, sem=("parallel", "parallel"))(w, g, m, v)


WEIGHTS = ["norm1_g", "w_in", "ssd_conv_w", "ssd_conv_b", "ssd_dt_bias", "ssd_a_log", "ssd_d", "ssd_norm_g",
           "lru_conv_w", "lru_conv_b", "lru_w_a", "lru_b_a", "lru_w_x", "lru_b_x", "lru_lambda", "lru_norm_g",
           "fox_b_f", "fox_norm_g", "w_out", "norm2_g", "w_gate", "w_up", "w_down", "norm3_g", "w_ple_gate",
           "b_ple_gate", "w_ple_proj", "final_norm_g"]
BIG = {"w_in": 2, "w_out": 1, "w_gate": 2, "w_up": 2, "w_down": 1, "w_ple_gate": 1, "w_ple_proj": 2}
SHARDED_SMALL = {"ssd_conv_w": 2, "lru_conv_w": 2}
ADAM_VIEW = {"w_in": ((2, 0, 1), (1, 2, 0)), "w_gate": ((0, 2, 1), (0, 2, 1)), "w_up": ((0, 2, 1), (0, 2, 1))}
TRANSPOSED = ("w_gate", "w_up")
SMALL = [n for n in WEIGHTS if n not in BIG]


def _pack(arrs, rows_multiple):
    flat = jnp.concatenate([a.reshape(-1) for a in arrs])
    per = rows_multiple * LANE
    n = -(-flat.shape[0] // per) * per
    return jnp.pad(flat, (0, n - flat.shape[0])).reshape(n // LANE, LANE)


def _unpack(flat2d, shapes):
    flat = flat2d.reshape(-1)
    out, off = [], 0
    for s in shapes:
        n = int(np.prod(s))
        out.append(flat[off:off + n].reshape(s))
        off += n
    return out


def _gather_shards(name, shards, axes, dtype):
    c = lax.axis_index("c")
    packed = _pack([s.astype(dtype) for s in shards], 32)
    half = packed.shape[0] // 2
    mine = lax.dynamic_slice_in_dim(packed, c * half, half, 0)
    got = all_gather8(name, mine).reshape(N_CHIP, 2 * half, LANE)
    per_chip = [_unpack(got[k], [s.shape for s in shards]) for k in range(N_CHIP)]
    return [jnp.concatenate([per_chip[k][i] for k in range(N_CHIP)], axis=ax) for i, ax in enumerate(axes)]


def kernel(x, p, norm1_g, w_in, ssd_conv_w, ssd_conv_b, ssd_dt_bias, ssd_a_log, ssd_d, ssd_norm_g, lru_conv_w, lru_conv_b, lru_w_a, lru_b_a, lru_w_x, lru_b_x, lru_lambda, lru_norm_g, fox_b_f, fox_norm_g, w_out, norm2_g, w_gate, w_up, w_down, norm3_g, w_ple_gate, b_ple_gate, w_ple_proj, final_norm_g, loss_target, m_norm1_g, m_w_in, m_ssd_conv_w, m_ssd_conv_b, m_ssd_dt_bias, m_ssd_a_log, m_ssd_d, m_ssd_norm_g, m_lru_conv_w, m_lru_conv_b, m_lru_w_a, m_lru_b_a, m_lru_w_x, m_lru_b_x, m_lru_lambda, m_lru_norm_g, m_fox_b_f, m_fox_norm_g, m_w_out, m_norm2_g, m_w_gate, m_w_up, m_w_down, m_norm3_g, m_w_ple_gate, m_b_ple_gate, m_w_ple_proj, m_final_norm_g, v_norm1_g, v_w_in, v_ssd_conv_w, v_ssd_conv_b, v_ssd_dt_bias, v_ssd_a_log, v_ssd_d, v_ssd_norm_g, v_lru_conv_w, v_lru_conv_b, v_lru_w_a, v_lru_b_a, v_lru_w_x, v_lru_b_x, v_lru_lambda, v_lru_norm_g, v_fox_b_f, v_fox_norm_g, v_w_out, v_norm2_g, v_w_gate, v_w_up, v_w_down, v_norm3_g, v_w_ple_gate, v_b_ple_gate, v_w_ple_proj, v_final_norm_g):
    args = (norm1_g, w_in, ssd_conv_w, ssd_conv_b, ssd_dt_bias, ssd_a_log, ssd_d, ssd_norm_g, lru_conv_w, lru_conv_b, lru_w_a, lru_b_a, lru_w_x, lru_b_x, lru_lambda, lru_norm_g, fox_b_f, fox_norm_g, w_out, norm2_g, w_gate, w_up, w_down, norm3_g, w_ple_gate, b_ple_gate, w_ple_proj, final_norm_g)
    m_args = (m_norm1_g, m_w_in, m_ssd_conv_w, m_ssd_conv_b, m_ssd_dt_bias, m_ssd_a_log, m_ssd_d, m_ssd_norm_g, m_lru_conv_w, m_lru_conv_b, m_lru_w_a, m_lru_b_a, m_lru_w_x, m_lru_b_x, m_lru_lambda, m_lru_norm_g, m_fox_b_f, m_fox_norm_g, m_w_out, m_norm2_g, m_w_gate, m_w_up, m_w_down, m_norm3_g, m_w_ple_gate, m_b_ple_gate, m_w_ple_proj, m_final_norm_g)
    v_args = (v_norm1_g, v_w_in, v_ssd_conv_w, v_ssd_conv_b, v_ssd_dt_bias, v_ssd_a_log, v_ssd_d, v_ssd_norm_g, v_lru_conv_w, v_lru_conv_b, v_lru_w_a, v_lru_b_a, v_lru_w_x, v_lru_b_x, v_lru_lambda, v_lru_norm_g, v_fox_b_f, v_fox_norm_g, v_w_out, v_norm2_g, v_w_gate, v_w_up, v_w_down, v_norm3_g, v_w_ple_gate, v_b_ple_gate, v_w_ple_proj, v_final_norm_g)
    w = dict(zip(WEIGHTS, args))
    mom = dict(zip(WEIGHTS, m_args))
    var = dict(zip(WEIGHTS, v_args))
    xi, yi, ci = _pos()
    chip = 2 * xi + yi

    big_names = list(BIG)
    later = [n for n in big_names if n != "w_in"]
    by_rows = {n: BIG[n] == 1 or n in TRANSPOSED for n in big_names}
    wb = {n: (jnp.transpose(w[n], (0, 2, 1)) if n in TRANSPOSED else w[n]).astype(BF16) for n in big_names}
    conv_full = dict(zip(SHARDED_SMALL, _gather_shards("gather_conv", [w[n] for n in SHARDED_SMALL],
                                                       list(SHARDED_SMALL.values()), F32)))
    carried = Carried()

    def layer_weights(l, slots_of):
        def assemble(n):
            s4 = slots_of(n)
            return (s4.reshape(-1, s4.shape[-1]) if by_rows[n]
                    else jnp.concatenate([s4[k] for k in range(N_CHIP)], axis=1))

        f = LazyDict({n: (conv_full[n][l] if n in conv_full else w[n][l]) for n in SMALL if n != "final_norm_g"})
        f.update({(n + "_t" if n in TRANSPOSED else n): functools.partial(assemble, n) for n in big_names})
        return prep_layer(f)

    carried.offer("l0_rms1", lambda: gather_spread([wb["w_in"]], 0))

    def w_in0():
        return run_exchange("gather0_in_pass_on", gather_pass_on(carried.results["l0_rms1"]))[0]

    n_early = 2
    carried.offer("l0_mm_in", lambda: gather_spread([wb[n] for n in later[:n_early]], 0))
    carried.offer("l0_ssd_fwd", lambda: gather_spread([wb[n] for n in later[n_early:]], 0))
    carried.offer("l0_fox_fwd", lambda: combine(
        gather_pass_on(carried.results["l0_mm_in"] + carried.results["l0_ssd_fwd"]),
        gather_spread([wb[n] for n in big_names], 1)))
    carried.offer("l0_mm_gu", lambda: gather_pass_on(carried.results["l0_fox_fwd"][len(later):]))
    layers = [
        layer_weights(0, lambda n: w_in0() if n == "w_in" else carried.results["l0_fox_fwd"][later.index(n)]),
        lambda: layer_weights(1, lambda n: carried.results["l0_mm_gu"][big_names.index(n)])]

    def chip_slices(a, n):
        return a.reshape(N_CHIP, -1, a.shape[1]) if by_rows[n] else jnp.stack(jnp.split(a, N_CHIP, axis=1))

    def halves(names, full_grads):
        keep, give = [], []
        for n, a in zip(names, full_grads):
            s4 = chip_slices(a, n)
            h = s4.shape[1] // 2
            keep.append(lax.dynamic_slice_in_dim(s4, ci * h, h, 1))
            give.append(lax.dynamic_slice_in_dim(s4, (1 - ci) * h, h, 1).astype(BF16))
        return keep, give

    def add_halves(tag, names, keep, got):
        return [add_slices(f"add_sibling{tag}_{n}", k_, g_, BF16) for n, k_, g_ in zip(names, keep, got)]

    kernel_key = dict(w_out="wout", w_gate="wg", w_up="wu", w_down="wd", w_ple_gate="wpg", w_ple_proj="wpp")
    n_big = len(big_names)
    gl = [None] * DEPTH

    def on_layer_grads(l, g_layer):
        gl[l] = unprep_grads(g_layer)
        if l == 1:
            keep, give = halves(big_names, [gl[1]["w_in"]] + [g_layer[kernel_key[n]] for n in later])
            carried.offer("l0_mm_dwpg", lambda: sibling_exchange(give))
            carried.offer("l0_fox_bwd", lambda: chips_exchange(
                add_halves("1", big_names, keep, carried.results["l0_mm_dwpg"])))

    def on_early_grads(l, g_layer):
        if l == 0:
            keep, give = halves(later, [g_layer[kernel_key[n]] for n in later])
            layer1_exchange = carried.offers.pop("l0_fox_bwd")
            carried.offer("l0_fox_bwd", lambda: combine(layer1_exchange(), sibling_exchange(give)))
            carried.offer("l0_ssd_bwd", lambda: chips_exchange(
                add_halves("0_later", later, keep, carried.results["l0_fox_bwd"][n_big:])))

    def on_w_in_grad(l, g_win):
        if l == 0:
            keep, give = halves(["w_in"], [unpermute_in_cols(g_win)])
            part = add_halves("0_in", ["w_in"], keep, swap_with_sibling("swap_halves0_in", give))
            carried.offer("l0_mm_du1", lambda: chips_exchange(part))

    loss, grad_x, grads, g_final = local_step(x, p, loss_target, layers, final_norm_g, carried, on_layer_grads,
                                              on_early_grads, on_w_in_grad)
    loss = lax.psum(loss, ("x", "y", "c"))
    arrived0 = dict(zip(later, carried.results["l0_ssd_bwd"]))
    arrived0["w_in"] = carried.results["l0_mm_du1"][0]
    arrived = [[arrived0[n] for n in big_names], carried.results["l0_fox_bwd"][:n_big]]

    gsmall = {n: jnp.stack([gl[l][n] for l in range(DEPTH)]) for n in SMALL if n != "final_norm_g"}
    gsmall["final_norm_g"] = g_final
    small_shapes = [gsmall[n].shape for n in SMALL]
    gs = _pack([gsmall[n] for n in SMALL], 8)
    gs = sum_slices("sum_small", all_gather8("gather_small_grads", gs))
    gsum = dict(zip(SMALL, _unpack(gs, small_shapes)))
    for n, ax in SHARDED_SMALL.items():
        k = gsum[n].shape[ax] // N_CHIP
        gsum[n] = lax.dynamic_slice_in_dim(gsum[n], chip * k, k, ax)

    done = []
    for l in range(DEPTH):
        mine = [sum_slices(f"sum_chips{l}_{n}", a_) for n, a_ in zip(big_names, arrived[l])]
        other = swap_with_sibling(f"swap_results{l}", mine)
        done.append([jnp.concatenate([jnp.where(ci == 0, m_, o_), jnp.where(ci == 0, o_, m_)], axis=0)
                     for m_, o_ in zip(mine, other)])
    gview = {}
    for t, n in enumerate(big_names):
        g2 = jnp.stack([done[l][t] for l in range(DEPTH)])
        if n in TRANSPOSED:
            gview[n], gsum[n] = g2, jnp.transpose(g2, (0, 2, 1))
        else:
            gsum[n] = g2

    delta, new_m, new_v = {}, {}, {}
    for n in big_names:
        if n in ADAM_VIEW:
            to_view, back = ADAM_VIEW[n]
            gv = gview[n] if n in gview else jnp.transpose(gsum[n], to_view)
            outs = adamw(f"adamw_{n}", jnp.transpose(w[n], to_view), gv, jnp.transpose(mom[n], to_view),
                         jnp.transpose(var[n], to_view))
            delta[n], new_m[n], new_v[n] = [jnp.transpose(o, back) for o in outs]
        else:
            delta[n], new_m[n], new_v[n] = adamw(f"adamw_{n}", w[n], gsum[n], mom[n], var[n])
    shapes = [w[n].shape for n in SMALL]
    pk = lambda d: _pack([d[n] for n in SMALL], 8)[None]
    ds, ms, vs = adamw("adamw_small", pk(w), pk(gsum), pk(mom), pk(var))
    for d, packed in ((delta, ds), (new_m, ms), (new_v, vs)):
        d.update(zip(SMALL, _unpack(packed[0], shapes)))

    return (loss, grad_x, *[gsum[n] for n in WEIGHTS], *[delta[n] for n in WEIGHTS],
            *[new_m[n] for n in WEIGHTS], *[new_v[n] for n in WEIGHTS])
```

```python
import functools
import math

import jax
import jax.numpy as jnp
import numpy as np
from jax import lax
from jax.experimental import pallas as pl
from jax.experimental.pallas import tpu as pltpu

F32, BF16 = jnp.float32, jnp.bfloat16
SDS = jax.ShapeDtypeStruct

D_MODEL = 1024
DEPTH = 2
HEAD_DIM = 64
N_HEADS = 6
SSD_W, LRU_W, FOX_W = 384, 256, 384
D_FF = 2816
PLE_DIM = 256
IN_COLS = 2956
EPS = 1e-6
LRU_C = 8.0
LANE = 128
V7X_VMEM_LIMIT = 56 * 1024 * 1024

PW = 3072
OFF_B, OFF_C, OFF_LX, OFF_LG, OFF_SM, OFF_Z, OFF_XS, OFF_Q, OFF_K, OFF_V = (
    0, 256, 512, 768, 1024, 1152, 1536, 1920, 2304, 2688)
FOX_LANE0 = 8

ADAM_LR, ADAM_B1, ADAM_B2, ADAM_EPS, ADAM_WD, ADAM_STEP = 0.001, 0.9, 0.999, 1e-08, 0.01, 10


def _iota(shape, dim):
    return lax.broadcasted_iota(jnp.int32, shape, dim)


class Carried:
    def __init__(self):
        self.offers, self.results = {}, {}

    def offer(self, call_name, make_exchange):
        self.offers[call_name] = make_exchange

    def take(self, call_name):
        make = self.offers.pop(call_name, None)
        return None if make is None else make()

    def deliver(self, call_name, results):
        self.results[call_name] = results


class LazyDict(dict):
    def __getitem__(self, key):
        v = dict.__getitem__(self, key)
        if callable(v):
            v = v()
            dict.__setitem__(self, key, v)
        return v


def _run(call, args, name, comm, carried):
    if comm is None:
        return call(*args)
    own, brought = call(*args)
    carried.deliver(name, brought)
    return own


def _pc(body, *, name, grid, in_specs, out_specs, out_shape, scratch=(), sem=None, comm=None):
    if comm is None:
        return pl.pallas_call(
            body, name=name, grid=grid, in_specs=in_specs, out_specs=out_specs, out_shape=out_shape,
            scratch_shapes=list(scratch),
            compiler_params=pltpu.CompilerParams(dimension_semantics=sem, vmem_limit_bytes=V7X_VMEM_LIMIT))
    single = not isinstance(out_shape, (list, tuple))
    out_specs_l = [out_specs] if single else list(out_specs)
    out_shape_l = [out_shape] if single else list(out_shape)
    n_in, n_out, n_scr, n_ci, n_co = len(in_specs), len(out_shape_l), len(scratch), len(comm.inputs), len(comm.out_shapes)

    def hosted(*refs):
        ins, cins = refs[:n_in], refs[n_in:n_in + n_ci]
        outs, couts = refs[n_in + n_ci:n_in + n_ci + n_out], refs[n_in + n_ci + n_out:n_in + n_ci + n_out + n_co]
        rest = refs[n_in + n_ci + n_out + n_co:]
        scr, csems = rest[:n_scr], rest[n_scr:]
        ids = [pl.program_id(d) for d in range(len(grid))]
        first = functools.reduce(jnp.logical_and, [i == 0 for i in ids])
        last = functools.reduce(jnp.logical_and, [i == g - 1 for i, g in zip(ids, grid)])

        @pl.when(first)
        def _():
            comm.start(cins, couts, csems)

        body(*ins, *outs, *scr)

        @pl.when(last)
        def _():
            comm.wait(cins, couts, csems)

    call = pl.pallas_call(
        hosted, name=name, grid=grid, in_specs=list(in_specs) + [ANY] * n_ci,
        out_specs=out_specs_l + [ANY] * n_co, out_shape=out_shape_l + list(comm.out_shapes),
        scratch_shapes=list(scratch) + list(comm.sems),
        input_output_aliases={n_in + a: n_out + b for a, b in comm.aliases.items()},
        compiler_params=pltpu.CompilerParams(dimension_semantics=("arbitrary",) * len(grid),
                                             vmem_limit_bytes=V7X_VMEM_LIMIT))

    def run(*args):
        res = call(*args, *comm.inputs)
        own = res[:n_out]
        return (own[0] if single else own), list(res[n_out:])

    return run


def permute_in_cols(w):
    z = lambda n: jnp.zeros(w.shape[:-1] + (n,), w.dtype)
    s = lambda a, b: w[..., a:b]
    return jnp.concatenate([
        s(768, 1024), s(1024, 1280), s(1286, 1542), s(1542, 1798),
        s(1280, 1286), z(2), s(2950, 2956), z(LANE - 14),
        s(0, 384), s(384, 768), s(1798, 2182), s(2182, 2566), s(2566, 2950)], axis=-1)


def unpermute_in_cols(g):
    s = lambda a, n: g[..., a:a + n]
    return jnp.concatenate([
        s(OFF_Z, 384), s(OFF_XS, 384), s(OFF_B, 256), s(OFF_C, 256), s(OFF_SM, 6),
        s(OFF_LX, 256), s(OFF_LG, 256), s(OFF_Q, 384), s(OFF_K, 384), s(OFF_V, 384),
        s(OFF_SM + FOX_LANE0, 6)], axis=-1)


def _pick(n, cands):
    for c in cands:
        if n % c == 0:
            return c
    return n


def mm(a, b, *, name, ta=False, tb=False, out_dtypes=(F32,), extras=(), col_params=(), partials=0, epilogue=None,
       tm=None, tn=None, tk=None, carried=None):
    bs = list(b) if isinstance(b, (list, tuple)) else [b]
    pair_sum = isinstance(a, (list, tuple))
    a_list = list(a) if pair_sum else [a]
    assert not pair_sum or len(a_list) == len(bs)
    a = a_list[0]
    n_a = len(a_list)
    n_acc = 1 if pair_sum else len(bs)
    extras = [e if isinstance(e, tuple) else (e, 0) for e in extras]
    M = a.shape[1] if ta else a.shape[0]
    K = a.shape[0] if ta else a.shape[1]
    N = bs[0].shape[0] if tb else bs[0].shape[1]
    tm = tm or _pick(M, (1024, 1408, 512, 256, 128))
    tn = tn or _pick(N, (1024, 1408, 768, 512, 256, 128))
    tk = tk or _pick(K, (1024, 1408, 512, 256, 128))
    nm, nn, nk = M // tm, N // tn, K // tk
    n_b, n_ex, n_cp, n_out = len(bs), len(extras), len(col_params), len(out_dtypes)
    a_bytes, b_bytes = n_a * M * K * a.dtype.itemsize, n_b * K * N * bs[0].dtype.itemsize
    rows_inner = a_bytes * nn + b_bytes <= a_bytes + b_bytes * nm

    def ij(g0, g1):
        return (g1, g0) if rows_inner else (g0, g1)

    def body(*rest):
        a_refs, rest = rest[:n_a], rest[n_a:]
        b_refs, rest = rest[:n_b], rest[n_b:]
        in_refs, rest = rest[:n_ex + n_cp], rest[n_ex + n_cp:]
        out_refs, accs = rest[:n_out + partials], rest[n_out + partials:]
        dn = (((0 if ta else 1,), (1 if tb else 0,)), ((), ()))
        dot = lambda x_ref, y_ref: lax.dot_general(x_ref[...].astype(BF16), y_ref[...].astype(BF16), dn,
                                                   preferred_element_type=F32)
        if pair_sum:
            parts = [functools.reduce(lambda u, v: u + v, [dot(x, y) for x, y in zip(a_refs, b_refs)])]
        else:
            parts = [dot(a_refs[0], b_ref) for b_ref in b_refs]

        def finish(rs):
            outs = epilogue(*rs, *[e[...] for e in in_refs]) if epilogue is not None else tuple(rs)
            for o_ref, o in zip(out_refs[:n_out], outs):
                o_ref[...] = o.astype(o_ref.dtype)
            for o_ref, o in zip(out_refs[n_out:], outs[n_out:]):
                o_ref[0] = o

        if nk == 1:
            finish(parts)
            return
        k = pl.program_id(2)

        @pl.when(k == 0)
        def _():
            for acc, part in zip(accs, parts):
                acc[...] = part

        @pl.when(k > 0)
        def _():
            for acc, part in zip(accs, parts):
                acc[...] += part

        @pl.when(k == nk - 1)
        def _():
            finish([acc[...] for acc in accs])

    def a_map(g0, g1, k):
        i, _ = ij(g0, g1)
        return (k, i) if ta else (i, k)

    def b_map(g0, g1, k):
        _, j = ij(g0, g1)
        return (j, k) if tb else (k, j)

    def ex_map(off, g0, g1, k):
        i, j = ij(g0, g1)
        return (i, j + off)

    a_spec = pl.BlockSpec((tk, tm) if ta else (tm, tk), a_map)
    b_spec = pl.BlockSpec((tn, tk) if tb else (tk, tn), b_map)
    mn_spec = pl.BlockSpec((tm, tn), functools.partial(ex_map, 0))
    comm = carried.take(name) if carried is not None else None
    call = _pc(body, name=name, grid=(nn, nm, nk) if rows_inner else (nm, nn, nk),
               in_specs=([a_spec] * n_a + [b_spec] * n_b
                         + [pl.BlockSpec((tm, tn), functools.partial(ex_map, off)) for _, off in extras]
                         + [pl.BlockSpec((1, tn), lambda g0, g1, k: (0, ij(g0, g1)[1]))] * n_cp),
               out_specs=([mn_spec] * n_out
                          + [pl.BlockSpec((1, 1, tn), lambda g0, g1, k: (ij(g0, g1)[0], 0, ij(g0, g1)[1]))] * partials),
               out_shape=[SDS((M, N), dt) for dt in out_dtypes] + [SDS((nm, 1, N), F32)] * partials,
               scratch=[pltpu.VMEM((tm, tn), F32)] * n_acc if nk > 1 else [],
               sem=("parallel", "parallel", "arbitrary"), comm=comm)
    outs = _run(call, (*a_list, *bs, *[e for e, _ in extras], *col_params), name, comm, carried)
    return outs[0] if len(outs) == 1 else outs


def rowwise(name, fn, rows, params, row_outs, acc_outs=(), tr=512, carried=None):
    rows = [r if isinstance(r, tuple) else (r, 0, r.shape[1]) for r in rows]
    T = rows[0][0].shape[0]
    tr = min(tr, T)
    n_in, n_ro, n_ac = len(rows) + len(params), len(row_outs), len(acc_outs)

    def body(*refs):
        ins, outs = refs[:n_in], refs[n_in:]
        res = fn(*[r[...] for r in ins])
        if not isinstance(res, (tuple, list)):
            res = (res,)
        for k in range(n_ro):
            outs[k][...] = res[k].astype(outs[k].dtype)
        if n_ac:
            i = pl.program_id(0)

            @pl.when(i == 0)
            def _():
                for k in range(n_ac):
                    outs[n_ro + k][...] = res[n_ro + k]

            @pl.when(i > 0)
            def _():
                for k in range(n_ac):
                    outs[n_ro + k][...] += res[n_ro + k]

    in_specs = ([pl.BlockSpec((tr, w), functools.partial(lambda cb, i: (i, cb), cb)) for (_, cb, w) in rows]
                + [pl.BlockSpec(p.shape, lambda i: (0, 0)) for p in params])
    out_specs = ([pl.BlockSpec((tr, c), lambda i: (i, 0)) for (c, _) in row_outs]
                 + [pl.BlockSpec((1, c), lambda i: (0, 0)) for c in acc_outs])
    out_shape = [SDS((T, c), dt) for (c, dt) in row_outs] + [SDS((1, c), F32) for c in acc_outs]
    comm = carried.take(name) if carried is not None else None
    call = _pc(body, name=name, grid=(T // tr,), in_specs=in_specs, out_specs=out_specs, out_shape=out_shape,
               sem=("arbitrary",) if n_ac else ("parallel",), comm=comm)
    outs = _run(call, (*[r[0] for r in rows], *params), name, comm, carried)
    return outs[0] if len(outs) == 1 else outs


def _rms(x, g):
    return x * lax.rsqrt(jnp.mean(x * x, axis=-1, keepdims=True) + EPS) * g


def _softplus(x):
    return jnp.maximum(x, 0.0) + jnp.log(1.0 + jnp.exp(-jnp.abs(x)))


def _silu(x):
    return x * jax.nn.sigmoid(x)


def _gelu(x):
    return 0.5 * x * (1.0 + jnp.tanh(math.sqrt(2.0 / math.pi) * (x + 0.044715 * (x * x * x))))


def _neg_expm1(x):
    series = x * (1 + x / 2 * (1 + x / 3 * (1 + x / 4 * (1 + x / 5 * (1 + x / 6 * (1 + x / 7))))))
    return -jnp.where(jnp.abs(x) < 0.3, series, jnp.exp(x) - 1.0)


def _swiglu(gu):
    return _silu(gu[:, :D_FF]) * gu[:, D_FF:]


def _ple(pg, pp, b):
    return jax.nn.sigmoid(pg + b) * pp


def _ssd_elt(small, xs_act, dtbias_row, alog_row):
    lane = _iota(small.shape, 1)
    dt = _softplus(small + dtbias_row)
    adt = jnp.where(lane < N_HEADS, -jnp.exp(alog_row) * dt, 0.0)
    head = _iota(xs_act.shape, 1) // HEAD_DIM
    dt_exp = jnp.zeros_like(xs_act)
    for h in range(N_HEADS):
        dth = jnp.sum(jnp.where(lane == h, dt, 0.0), axis=1, keepdims=True)
        dt_exp = dt_exp + jnp.where(head == h, dth, 0.0)
    return adt, xs_act * dt_exp


def _fox_elt(small, bf_row):
    lane = _iota(small.shape, 1)
    keep = (lane >= FOX_LANE0) & (lane < FOX_LANE0 + N_HEADS)
    return jnp.where(keep, -_softplus(-(small + bf_row)), 0.0)


def _lru_elt(xl, pre, b_ax, lam):
    r = jax.nn.sigmoid(pre[:, :LRU_W] + b_ax[:, :LRU_W])
    i = jax.nn.sigmoid(pre[:, LRU_W:] + b_ax[:, LRU_W:])
    log_a = -LRU_C * r * _softplus(-lam)
    a = jnp.exp(log_a)
    mult = jnp.sqrt(_neg_expm1(2.0 * log_a))
    return a, mult * (i * xl)


def _mix_post(yraw, xs_act, z, hl, lgate, yfox, dexp, g_ssd, g_lru, g_fox):
    y_ssd = _rms((yraw + xs_act * dexp) * _silu(z), g_ssd)
    y_lru = _rms(hl * _gelu(lgate), g_lru)
    y_fox = _rms(yfox, g_fox)
    return jnp.concatenate([y_ssd, y_lru, y_fox], axis=-1)


def _colsum(x):
    return jnp.sum(x, axis=0, keepdims=True)


def _shift_down(x, d):
    if d == 0:
        return x
    return jnp.where(_iota(x.shape, 0) >= d, pltpu.roll(x, d, 0), 0.0)


def _shift_up(x, d):
    if d == 0:
        return x
    s = x.shape[0]
    return jnp.where(_iota(x.shape, 0) < s - d, pltpu.roll(x, s - d, 0), 0.0)


def _conv_core(x, w, b):
    y = b + w[3:4, :] * x
    for k in range(3):
        y = y + w[k:k + 1, :] * _shift_down(x, 3 - k)
    return y


def seq_conv(name, src, col, width, w8, b, *, batch, silu, out_dtype):
    T = src.shape[0]
    S = T // batch
    c0 = col // LANE

    def body(x_ref, w_ref, b_ref, o_ref):
        y = _conv_core(x_ref[...], w_ref[...], b_ref[...])
        o_ref[...] = (_silu(y) if silu else y).astype(o_ref.dtype)

    return _pc(body, name=name, grid=(batch, width // LANE),
               in_specs=[pl.BlockSpec((S, LANE), lambda bi, ci: (bi, c0 + ci)),
                         pl.BlockSpec((8, LANE), lambda bi, ci: (0, ci)),
                         pl.BlockSpec((1, LANE), lambda bi, ci: (0, ci))],
               out_specs=pl.BlockSpec((S, LANE), lambda bi, ci: (bi, ci)),
               out_shape=SDS((T, width), out_dtype), sem=("parallel", "parallel"))(src, w8, b)


def seq_conv_bwd(name, src, col, width, w8, b, dy, *, batch, silu):
    T = src.shape[0]
    S = T // batch
    c0 = col // LANE

    def body(x_ref, w_ref, b_ref, dy_ref, dx_ref, dw_ref, db_ref):
        x, w = x_ref[...], w_ref[...]
        dpre = dy_ref[...].astype(F32)
        if silu:
            dpre = jax.vjp(_silu, _conv_core(x, w, b_ref[...]))[1](dpre)[0]
        dx = w[3:4, :] * dpre
        for k in range(3):
            dx = dx + w[k:k + 1, :] * _shift_up(dpre, 3 - k)
        dx_ref[...] = dx.astype(dx_ref.dtype)
        row8 = _iota((8, LANE), 0)
        dw = jnp.zeros((8, LANE), F32)
        for k in range(4):
            dw = dw + jnp.where(row8 == k, _colsum(dpre * _shift_down(x, 3 - k)), 0.0)
        db = _colsum(dpre)
        bi = pl.program_id(1)

        @pl.when(bi == 0)
        def _():
            dw_ref[...] = dw
            db_ref[...] = db

        @pl.when(bi > 0)
        def _():
            dw_ref[...] += dw
            db_ref[...] += db

    return _pc(body, name=name, grid=(width // LANE, batch),
               in_specs=[pl.BlockSpec((S, LANE), lambda ci, bi: (bi, c0 + ci)),
                         pl.BlockSpec((8, LANE), lambda ci, bi: (0, ci)),
                         pl.BlockSpec((1, LANE), lambda ci, bi: (0, ci)),
                         pl.BlockSpec((S, LANE), lambda ci, bi: (bi, ci))],
               out_specs=[pl.BlockSpec((S, LANE), lambda ci, bi: (bi, ci)),
                          pl.BlockSpec((8, LANE), lambda ci, bi: (0, ci)),
                          pl.BlockSpec((1, LANE), lambda ci, bi: (0, ci))],
               out_shape=[SDS((T, width), BF16), SDS((8, width), F32), SDS((1, width), F32)],
               sem=("parallel", "arbitrary"))(src, w8, b, dy)


def _split3_dot(tri, x):
    hi = x.astype(BF16)
    r1 = x - hi.astype(F32)
    mid = r1.astype(BF16)
    lo = (r1 - mid.astype(F32)).astype(BF16)
    d = lambda v: jnp.dot(tri, v, preferred_element_type=F32)
    return d(hi) + d(mid) + d(lo)


def seq_cumsum(name, x, *, batch, reverse=False, nsum=1, trow=None):
    T = x.shape[0]
    S = T // batch
    ch = min(256, S)
    nch = S // ch

    def body(x_ref, o_ref, *maybe_t):
        r, c = _iota((ch, ch), 0), _iota((ch, ch), 1)
        tri = jnp.where((c >= r) if reverse else (c <= r), 1.0, 0.0).astype(BF16)
        carry = jnp.zeros((1, LANE), F32)
        for k in (range(nch - 1, -1, -1) if reverse else range(nch)):
            xc = x_ref[k * ch:(k + 1) * ch, 0:LANE]
            for m in range(1, nsum):
                xc = xc + x_ref[k * ch:(k + 1) * ch, m * LANE:(m + 1) * LANE]
            o_ref[k * ch:(k + 1) * ch, :] = _split3_dot(tri, xc) + carry
            carry = carry + _colsum(xc)
        if trow is not None:
            maybe_t[0][...] = o_ref[...].T[trow:trow + 8, :]

    out_specs = [pl.BlockSpec((S, LANE), lambda bi: (bi, 0))]
    out_shape = [SDS((T, LANE), F32)]
    if trow is not None:
        out_specs.append(pl.BlockSpec((8, S), lambda bi: (bi, 0)))
        out_shape.append(SDS((batch * 8, S), F32))
    outs = _pc(body, name=name, grid=(batch,), in_specs=[pl.BlockSpec((S, LANE * nsum), lambda bi: (bi, 0))],
               out_specs=out_specs, out_shape=out_shape, sem=("parallel",))(x)
    return outs if trow is not None else outs[0]


_SCAN_SEQS = 2


def lru_scan(name, a, u, *, batch):
    T, W = a.shape
    S = T // batch
    nb = _SCAN_SEQS if batch % _SCAN_SEQS == 0 else 1

    def body(a_ref, u_ref, h_ref):
        row = _iota((8, W), 0)

        def step(g, hs):
            new = []
            for q, h in enumerate(hs):
                off = pl.multiple_of(q * S + g * 8, 8)
                at, ut = a_ref[pl.ds(off, 8), :], u_ref[pl.ds(off, 8), :]
                acc = jnp.zeros((8, W), F32)
                for r in range(8):
                    h = at[r:r + 1, :] * h + ut[r:r + 1, :]
                    acc = jnp.where(row == r, jnp.broadcast_to(h, (8, W)), acc)
                h_ref[pl.ds(off, 8), :] = acc
                new.append(h)
            return tuple(new)

        lax.fori_loop(0, S // 8, step, tuple(jnp.zeros((1, W), F32) for _ in range(nb)))

    spec = pl.BlockSpec((nb * S, W), lambda bi: (bi, 0))
    return _pc(body, name=name, grid=(batch // nb,), in_specs=[spec, spec], out_specs=spec,
               out_shape=SDS((T, W), F32), sem=("parallel",))(a, u)


def lru_scan_bwd(name, a, h, dh, *, batch):
    T, W = a.shape
    S = T // batch
    ng = S // 8
    nb = _SCAN_SEQS if batch % _SCAN_SEQS == 0 else 1

    def body(a_ref, h_ref, dh_ref, da_ref, du_ref):
        row = _iota((8, W), 0)

        def step(k, cs):
            g_idx = ng - 1 - k
            new = []
            for q, c in enumerate(cs):
                off = pl.multiple_of(q * S + g_idx * 8, 8)
                offp = pl.multiple_of(q * S + jnp.maximum(g_idx - 1, 0) * 8, 8)
                at, ht, dt = a_ref[pl.ds(off, 8), :], h_ref[pl.ds(off, 8), :], dh_ref[pl.ds(off, 8), :]
                hp = jnp.where(g_idx > 0, h_ref[pl.ds(offp, 8), :], 0.0)
                da = jnp.zeros((8, W), F32)
                du = jnp.zeros((8, W), F32)
                for r in range(7, -1, -1):
                    g = dt[r:r + 1, :] + c
                    hprev = ht[r - 1:r, :] if r > 0 else hp[7:8, :]
                    du = jnp.where(row == r, jnp.broadcast_to(g, (8, W)), du)
                    da = jnp.where(row == r, jnp.broadcast_to(g * hprev, (8, W)), da)
                    c = at[r:r + 1, :] * g
                da_ref[pl.ds(off, 8), :] = da
                du_ref[pl.ds(off, 8), :] = du
                new.append(c)
            return tuple(new)

        lax.fori_loop(0, ng, step, tuple(jnp.zeros((1, W), F32) for _ in range(nb)))

    spec = pl.BlockSpec((nb * S, W), lambda bi: (bi, 0))
    return _pc(body, name=name, grid=(batch // nb,), in_specs=[spec] * 3, out_specs=[spec] * 2,
               out_shape=[SDS((T, W), F32)] * 2, sem=("parallel",))(a, h, dh)


def _nt(a, b):
    return lax.dot_general(a, b, (((1,), (1,)), ((), ())), preferred_element_type=F32)


def _tn(a, b):
    return lax.dot_general(a, b, (((0,), (0,)), ((), ())), preferred_element_type=F32)


def _tile(S, t=256):
    return min(t, S)


def ssd_attn_fwd(name, cm, bm, xd, cum, cum_t, *, batch, carried=None):
    T = cm.shape[0]
    S = T // batch
    tq = tk = _tile(S)
    nq = S // tq

    def body(c_ref, b_ref, x_ref, cum_ref, cumt_ref, y_ref):
        i = pl.program_id(1)
        cq, cmq = cum_ref[...], c_ref[...]
        rowi, coli = _iota((tq, tk), 0), _iota((tq, tk), 1)
        half = _iota((tk, LANE), 1) // HEAD_DIM

        def step(j, accs, diag):
            off = pl.multiple_of(j * tk, tk)
            bj = b_ref[pl.ds(off, tk), :]
            gm = [_nt(cmq[:, g * LANE:(g + 1) * LANE], bj[:, g * LANE:(g + 1) * LANE]) for g in range(2)]
            ckt = cumt_ref[:, pl.ds(off, tk)]
            new = []
            for p in range(3):
                xp = x_ref[pl.ds(off, tk), p * LANE:(p + 1) * LANE]
                ws, xs = [], []
                for hh in range(2):
                    h = 2 * p + hh
                    seg = cq[:, h:h + 1] - ckt[h:h + 1, :]
                    e = jnp.exp(jnp.where(rowi >= coli, seg, -jnp.inf) if diag else seg)
                    ws.append((gm[h // 3] * e).astype(BF16))
                    xs.append(jnp.where(half == hh, xp, jnp.zeros_like(xp)))
                new.append(accs[p] + jnp.dot(jnp.concatenate(ws, axis=1), jnp.concatenate(xs, axis=0),
                                             preferred_element_type=F32))
            return tuple(new)

        accs = lax.fori_loop(0, i, functools.partial(step, diag=False),
                             tuple(jnp.zeros((tq, LANE), F32) for _ in range(3)))
        accs = step(i, accs, True)
        y_ref[...] = jnp.concatenate(accs, axis=1)

    comm = carried.take(name) if carried is not None else None
    call = _pc(body, name=name, grid=(batch, nq),
               in_specs=[pl.BlockSpec((tq, 256), lambda b, i: (b * nq + i, 0)),
                         pl.BlockSpec((S, 256), lambda b, i: (b, 0)),
                         pl.BlockSpec((S, SSD_W), lambda b, i: (b, 0)),
                         pl.BlockSpec((tq, LANE), lambda b, i: (b * nq + i, 0)),
                         pl.BlockSpec((8, S), lambda b, i: (b, 0))],
               out_specs=pl.BlockSpec((tq, SSD_W), lambda b, i: (b * nq + i, 0)),
               out_shape=SDS((T, SSD_W), F32), sem=("parallel", "parallel"), comm=comm)
    return _run(call, (cm, bm, xd, cum, cum_t), name, comm, carried)


def ssd_attn_bwd(name, cm, bm, xd, cum, cum_t, dy, *, batch, carried=None):
    T = cm.shape[0]
    S = T // batch
    tq = tk = _tile(S, 512)
    nq = S // tq

    def body(c_ref, b_ref, x_ref, cum_ref, cumt_ref, dy_ref, dx_ref, db_ref, dc_ref, dcum_ref, dcumt_ref):
        dx_ref[...] = jnp.zeros_like(dx_ref)
        db_ref[...] = jnp.zeros_like(db_ref)
        dcum_ref[...] = jnp.zeros_like(dcum_ref)
        dcumt_ref[...] = jnp.zeros_like(dcumt_ref)
        rowi, coli = _iota((tq, tk), 0), _iota((tq, tk), 1)
        halfq = _iota((tq, LANE), 1) // HEAD_DIM
        lane_q = _iota((tq, LANE), 1)

        def qblock(i, _):
            qoff = pl.multiple_of(i * tq, tq)
            cq = cum_ref[pl.ds(qoff, tq), :]
            cmq = c_ref[pl.ds(qoff, tq), :]
            dyq = dy_ref[pl.ds(qoff, tq), :]
            dyh = [[jnp.where(halfq == hh, dyq[:, p * LANE:(p + 1) * LANE], 0.0).astype(BF16) for hh in range(2)]
                   for p in range(3)]

            def step(j, carry, diag):
                dcq, rs_acc = carry
                off = pl.multiple_of(j * tk, tk)
                bj = b_ref[pl.ds(off, tk), :]
                gm = [_nt(cmq[:, g * LANE:(g + 1) * LANE], bj[:, g * LANE:(g + 1) * LANE]) for g in range(2)]
                ckt = cumt_ref[:, pl.ds(off, tk)]
                dgm = [jnp.zeros((tq, tk), F32), jnp.zeros((tq, tk), F32)]
                for p in range(3):
                    xp = x_ref[pl.ds(off, tk), p * LANE:(p + 1) * LANE]
                    ws = []
                    for hh in range(2):
                        h = 2 * p + hh
                        seg = cq[:, h:h + 1] - ckt[h:h + 1, :]
                        e = jnp.exp(jnp.where(rowi >= coli, seg, -jnp.inf) if diag else seg)
                        w = gm[h // 3] * e
                        dw = _nt(dyh[p][hh], xp)
                        zz = dw * w
                        rs_acc = rs_acc + jnp.where(lane_q == h, jnp.sum(zz, axis=1, keepdims=True), 0.0)
                        dcumt_ref[h:h + 1, pl.ds(off, tk)] += _colsum(zz)
                        dgm[h // 3] = dgm[h // 3] + dw * e
                        ws.append(w.astype(BF16))
                    dx_ref[pl.ds(off, tk), p * LANE:(p + 1) * LANE] += _tn(
                        jnp.concatenate(ws, axis=0), jnp.concatenate(dyh[p], axis=0))
                new_dcq = []
                for g in range(2):
                    dg = dgm[g].astype(BF16)
                    new_dcq.append(dcq[g] + jnp.dot(dg, bj[:, g * LANE:(g + 1) * LANE], preferred_element_type=F32))
                    db_ref[pl.ds(off, tk), g * LANE:(g + 1) * LANE] += _tn(dg, cmq[:, g * LANE:(g + 1) * LANE])
                return tuple(new_dcq), rs_acc

            carry = lax.fori_loop(
                0, i, functools.partial(step, diag=False),
                ((jnp.zeros((tq, LANE), F32), jnp.zeros((tq, LANE), F32)), jnp.zeros((tq, LANE), F32)))
            dcq, rs_acc = step(i, carry, True)
            dc_ref[pl.ds(qoff, tq), :] = jnp.concatenate(dcq, axis=1)
            dcum_ref[pl.ds(qoff, tq), :] += rs_acc
            return 0

        lax.fori_loop(0, nq, qblock, 0)
        dcum_ref[...] = dcum_ref[...] - dcumt_ref[...].T

    s256 = pl.BlockSpec((S, 256), lambda b: (b, 0))
    s384 = pl.BlockSpec((S, SSD_W), lambda b: (b, 0))
    s128 = pl.BlockSpec((S, LANE), lambda b: (b, 0))
    comm = carried.take(name) if carried is not None else None
    call = _pc(body, name=name, grid=(batch,),
               in_specs=[s256, s256, s384, s128, pl.BlockSpec((8, S), lambda b: (b, 0)), s384],
               out_specs=[s384, s256, s256, s128],
               out_shape=[SDS((T, SSD_W), F32), SDS((T, 256), F32), SDS((T, 256), F32), SDS((T, LANE), F32)],
               scratch=[pltpu.VMEM((LANE, S), F32)], sem=("parallel",), comm=comm)
    return _run(call, (cm, bm, xd, cum, cum_t, dy), name, comm, carried)


NEG_BIG = -1e30


def fox_attn_fwd(name, proj, cum, cum_t, *, batch, carried=None):
    T = proj.shape[0]
    S = T // batch
    tq = tk = _tile(S, 512)
    nq = S // tq
    scale = HEAD_DIM ** -0.5
    qb, kb, vb = OFF_Q // LANE, OFF_K // LANE, OFF_V // LANE

    def body(q_ref, k_ref, v_ref, cum_ref, cumt_ref, o_ref, lse_ref):
        p, i = pl.program_id(1), pl.program_id(2)
        cq = cum_ref[...]
        lane_q = _iota((tq, LANE), 1)
        halfq, halfk = lane_q // HEAD_DIM, _iota((tk, LANE), 1) // HEAD_DIM
        qs = q_ref[...] * scale
        qh = [jnp.where(halfq == hh, qs, 0.0).astype(BF16) for hh in range(2)]
        rowi, coli = _iota((tq, tk), 0), _iota((tq, tk), 1)
        cqh = [jnp.sum(jnp.where(lane_q == FOX_LANE0 + 2 * p + hh, cq, 0.0), axis=1, keepdims=True) for hh in range(2)]
        row8 = _iota((8, tk), 0)

        def step(j, carry, diag):
            ms, ls, acc = carry
            off = pl.multiple_of(j * tk, tk)
            kj = k_ref[pl.ds(off, tk), :].astype(BF16)
            vj = v_ref[pl.ds(off, tk), :].astype(BF16)
            ckt = cumt_ref[:, pl.ds(off, tk)]
            ps, vs, new_m, new_l, alphas = [], [], [], [], []
            for hh in range(2):
                ck = jnp.sum(jnp.where(row8 == 2 * p + hh, ckt, 0.0), axis=0, keepdims=True)
                logits = _nt(qh[hh], kj) + (cqh[hh] - ck)
                if diag:
                    logits = jnp.where(rowi >= coli, logits, -jnp.inf)
                m = jnp.maximum(ms[hh], jnp.max(logits, axis=1, keepdims=True))
                alpha = jnp.exp(ms[hh] - m)
                pr = jnp.exp(logits - m)
                new_m.append(m)
                new_l.append(alpha * ls[hh] + jnp.sum(pr, axis=1, keepdims=True))
                alphas.append(alpha)
                ps.append(pr.astype(BF16))
                vs.append(jnp.where(halfk == hh, vj, jnp.zeros_like(vj)))
            acc = acc * jnp.where(halfq == 0, alphas[0], alphas[1]) + jnp.dot(
                jnp.concatenate(ps, axis=1), jnp.concatenate(vs, axis=0), preferred_element_type=F32)
            return tuple(new_m), tuple(new_l), acc

        init = ((jnp.full((tq, 1), NEG_BIG, F32),) * 2, (jnp.zeros((tq, 1), F32),) * 2, jnp.zeros((tq, LANE), F32))
        ms, ls, acc = step(i, lax.fori_loop(0, i, functools.partial(step, diag=False), init), True)
        o_ref[...] = acc / jnp.where(halfq == 0, ls[0], ls[1])
        lse_ref[...] = (jnp.where(lane_q == 0, ms[0] + jnp.log(ls[0]), 0.0)
                        + jnp.where(lane_q == 1, ms[1] + jnp.log(ls[1]), 0.0))

    comm = carried.take(name) if carried is not None else None
    call = _pc(body, name=name, grid=(batch, 3, nq),
               in_specs=[pl.BlockSpec((tq, LANE), lambda b, p, i: (b * nq + i, qb + p)),
                         pl.BlockSpec((S, LANE), lambda b, p, i: (b, kb + p)),
                         pl.BlockSpec((S, LANE), lambda b, p, i: (b, vb + p)),
                         pl.BlockSpec((tq, LANE), lambda b, p, i: (b * nq + i, 0)),
                         pl.BlockSpec((8, S), lambda b, p, i: (b, 0))],
               out_specs=[pl.BlockSpec((tq, LANE), lambda b, p, i: (b * nq + i, p))] * 2,
               out_shape=[SDS((T, FOX_W), F32)] * 2, sem=("parallel", "parallel", "parallel"), comm=comm)
    return _run(call, (proj, proj, proj, cum, cum_t), name, comm, carried)


def fox_attn_bwd(name, proj, o, do, lse, cum, cum_t, *, batch, carried=None):
    T = proj.shape[0]
    S = T // batch
    tq = tk = _tile(S, 512)
    nq = S // tq
    scale = HEAD_DIM ** -0.5
    qb, kb, vb = OFF_Q // LANE, OFF_K // LANE, OFF_V // LANE

    def body(q_ref, k_ref, v_ref, o_ref, do_ref, lse_ref, cum_ref, cumt_ref,
             dq_ref, dk_ref, dv_ref, dcum_ref, dk_acc, dv_acc, dcumt_ref):
        p = pl.program_id(1)
        dk_acc[...] = jnp.zeros_like(dk_acc)
        dv_acc[...] = jnp.zeros_like(dv_acc)
        dcum_ref[...] = jnp.zeros_like(dcum_ref)
        dcumt_ref[...] = jnp.zeros_like(dcumt_ref)
        lane_q = _iota((tq, LANE), 1)
        halfq, halfk = lane_q // HEAD_DIM, _iota((tk, LANE), 1) // HEAD_DIM
        rowi, coli = _iota((tq, tk), 0), _iota((tq, tk), 1)
        row8 = _iota((8, tk), 0)

        def qblock(i, _):
            qoff = pl.multiple_of(i * tq, tq)
            cq = cum_ref[pl.ds(qoff, tq), :]
            qs = q_ref[pl.ds(qoff, tq), :] * scale
            doq = do_ref[pl.ds(qoff, tq), :]
            lse = lse_ref[pl.ds(qoff, tq), :]
            delta = doq * o_ref[pl.ds(qoff, tq), :]
            qh, doh, cqh, lseh, dlt = [], [], [], [], []
            for hh in range(2):
                qh.append(jnp.where(halfq == hh, qs, 0.0).astype(BF16))
                doh.append(jnp.where(halfq == hh, doq, 0.0).astype(BF16))
                cqh.append(jnp.sum(jnp.where(lane_q == FOX_LANE0 + 2 * p + hh, cq, 0.0), axis=1, keepdims=True))
                lseh.append(jnp.sum(jnp.where(lane_q == hh, lse, 0.0), axis=1, keepdims=True))
                dlt.append(jnp.sum(jnp.where(halfq == hh, delta, 0.0), axis=1, keepdims=True))

            def step(j, carry, diag):
                dq, rs = carry
                off = pl.multiple_of(j * tk, tk)
                kj = k_ref[pl.ds(off, tk), :].astype(BF16)
                vj = v_ref[pl.ds(off, tk), :].astype(BF16)
                ckt = cumt_ref[:, pl.ds(off, tk)]
                dss, prs, ks = [], [], []
                for hh in range(2):
                    ck = jnp.sum(jnp.where(row8 == 2 * p + hh, ckt, 0.0), axis=0, keepdims=True)
                    logits = _nt(qh[hh], kj) + ((cqh[hh] - lseh[hh]) - ck)
                    if diag:
                        logits = jnp.where(rowi >= coli, logits, -jnp.inf)
                    pr = jnp.exp(logits)
                    ds = pr * (_nt(doh[hh], vj) - dlt[hh])
                    rs = rs + jnp.where(lane_q == FOX_LANE0 + 2 * p + hh, jnp.sum(ds, axis=1, keepdims=True), 0.0)
                    cs = _colsum(ds)
                    dcumt_ref[0:8, pl.ds(off, tk)] += jnp.where(row8 == 2 * p + hh, cs, 0.0)
                    dss.append(ds.astype(BF16))
                    prs.append(pr.astype(BF16))
                    ks.append(jnp.where(halfk == hh, kj, jnp.zeros_like(kj)))
                dq = dq + jnp.dot(jnp.concatenate(dss, axis=1), jnp.concatenate(ks, axis=0), preferred_element_type=F32)
                dk_acc[pl.ds(off, tk), :] += _tn(jnp.concatenate(dss, axis=0), jnp.concatenate(qh, axis=0))
                dv_acc[pl.ds(off, tk), :] += _tn(jnp.concatenate(prs, axis=0), jnp.concatenate(doh, axis=0))
                return dq, rs

            carry = lax.fori_loop(0, i, functools.partial(step, diag=False),
                                  (jnp.zeros((tq, LANE), F32), jnp.zeros((tq, LANE), F32)))
            dq, rs = step(i, carry, True)
            dq_ref[pl.ds(qoff, tq), :] = (dq * scale).astype(dq_ref.dtype)
            dcum_ref[pl.ds(qoff, tq), :] += rs
            return 0

        lax.fori_loop(0, nq, qblock, 0)
        dk_ref[...] = dk_acc[...].astype(dk_ref.dtype)
        dv_ref[...] = dv_acc[...].astype(dv_ref.dtype)
        dct = dcumt_ref[...].T
        dcum_ref[...] = dcum_ref[...] - pltpu.roll(dct, FOX_LANE0, 1)

    sp = lambda c0: pl.BlockSpec((S, LANE), lambda b, p: (b, c0 + p))
    s0 = pl.BlockSpec((S, LANE), lambda b, p: (b, 0))
    comm = carried.take(name) if carried is not None else None
    call = _pc(body, name=name, grid=(batch, 3),
               in_specs=[sp(qb), sp(kb), sp(vb), sp(0), sp(0), sp(0), s0, pl.BlockSpec((8, S), lambda b, p: (b, 0))],
               out_specs=[sp(0)] * 4,
               out_shape=[SDS((T, FOX_W), BF16)] * 3 + [SDS((T, FOX_W), F32)],
               scratch=[pltpu.VMEM((S, LANE), F32), pltpu.VMEM((S, LANE), F32), pltpu.VMEM((LANE, S), F32)],
               sem=("parallel", "parallel"), comm=comm)
    return _run(call, (proj, proj, proj, o, do, lse, cum, cum_t), name, comm, carried)


def _row(v, width=None, at=0):
    v = v.astype(F32)
    width = width or v.shape[0]
    return jnp.pad(v, (at, width - at - v.shape[0]))[None, :]


def _pad8(w4):
    return jnp.pad(w4.astype(F32), ((0, 4), (0, 0)))


def _block_diag(w):
    eye = jnp.eye(w.shape[0], dtype=w.dtype)
    return (w[:, :, None, :] * eye[:, None, :, None]).reshape(LRU_W, LRU_W)


def prep_layer(f):
    cw, cb = f["ssd_conv_w"], f["ssd_conv_b"]
    return LazyDict(
        win=lambda: permute_in_cols(f["w_in"]), wout=lambda: f["w_out"],
        wg=lambda: f["w_gate_t"] if "w_gate_t" in f else f["w_gate"].T,
        wu=lambda: f["w_up_t"] if "w_up_t" in f else f["w_up"].T,
        wd=lambda: f["w_down"], wpg=lambda: f["w_ple_gate"], wpp=lambda: f["w_ple_proj"],
        wax=jnp.concatenate([_block_diag(f["lru_w_a"]), _block_diag(f["lru_w_x"])], axis=1),
        g1=_row(f["norm1_g"]), g2=_row(f["norm2_g"]), g3=_row(f["norm3_g"]),
        cw_xs=_pad8(cw[:, :384]), cb_xs=_row(cb[:384]), cw_b=_pad8(cw[:, 384:640]), cb_b=_row(cb[384:640]),
        cw_c=_pad8(cw[:, 640:]), cb_c=_row(cb[640:]), cw_l=_pad8(f["lru_conv_w"]), cb_l=_row(f["lru_conv_b"]),
        dtbias_row=_row(f["ssd_dt_bias"], LANE), alog_row=_row(f["ssd_a_log"], LANE),
        dexp=jnp.repeat(f["ssd_d"].astype(F32), HEAD_DIM)[None, :], g_ssd=_row(f["ssd_norm_g"]),
        b_ax=_row(jnp.concatenate([f["lru_b_a"], f["lru_b_x"]])), lam=_row(f["lru_lambda"]), g_lru=_row(f["lru_norm_g"]),
        bf_row=_row(f["fox_b_f"], LANE, FOX_LANE0), g_fox=_row(f["fox_norm_g"]), b_pg=_row(f["b_ple_gate"]))


def unprep_grads(g):
    blocks = lambda m: jnp.stack([m[i * 64:(i + 1) * 64, i * 64:(i + 1) * 64] for i in range(4)])
    return dict(
        norm1_g=g["g1"][0], w_in=unpermute_in_cols(g["win"]),
        ssd_conv_w=jnp.concatenate([g["cw_xs"][:4], g["cw_b"][:4], g["cw_c"][:4]], axis=1),
        ssd_conv_b=jnp.concatenate([g["cb_xs"][0], g["cb_b"][0], g["cb_c"][0]]),
        ssd_dt_bias=g["dtbias_row"][0, :N_HEADS], ssd_a_log=g["alog_row"][0, :N_HEADS],
        ssd_d=jnp.sum(g["dexp"].reshape(N_HEADS, HEAD_DIM), axis=1), ssd_norm_g=g["g_ssd"][0],
        lru_conv_w=g["cw_l"][:4], lru_conv_b=g["cb_l"][0],
        lru_w_a=blocks(g["wax"][:, :LRU_W]), lru_b_a=g["b_ax"][0, :LRU_W],
        lru_w_x=blocks(g["wax"][:, LRU_W:]), lru_b_x=g["b_ax"][0, LRU_W:],
        lru_lambda=g["lam"][0], lru_norm_g=g["g_lru"][0],
        fox_b_f=g["bf_row"][0, FOX_LANE0:FOX_LANE0 + N_HEADS], fox_norm_g=g["g_fox"][0],
        w_out=g["wout"], norm2_g=g["g2"][0], w_gate=g["wg"].T, w_up=g["wu"].T, w_down=g["wd"],
        norm3_g=g["g3"][0], w_ple_gate=g["wpg"], b_ple_gate=g["b_pg"][0], w_ple_proj=g["wpp"])


def _view(a, off, width):
    return (a, off // width, width)


def _add_epilogue(acc, e):
    return (acc + e,)


def mixer_fwd(proj, w, batch, tag, carried=None):
    sm = _view(proj, OFF_SM, LANE)
    conv = functools.partial(seq_conv, batch=batch)
    cmc = conv(f"{tag}_conv_c", proj, OFF_C, 256, w["cw_c"], w["cb_c"], silu=True, out_dtype=BF16)
    bmc = conv(f"{tag}_conv_b", proj, OFF_B, 256, w["cw_b"], w["cb_b"], silu=True, out_dtype=BF16)
    xs_act = conv(f"{tag}_conv_xs", proj, OFF_XS, SSD_W, w["cw_xs"], w["cb_xs"], silu=True, out_dtype=F32)
    xl = conv(f"{tag}_conv_l", proj, OFF_LX, LRU_W, w["cw_l"], w["cb_l"], silu=False, out_dtype=F32)
    adt, xd = rowwise(f"{tag}_ssd_elt", _ssd_elt, [sm, xs_act], [w["dtbias_row"], w["alog_row"]],
                      [(LANE, F32), (SSD_W, BF16)])
    cum_a, cum_at = seq_cumsum(f"{tag}_cum_a", adt, batch=batch, trow=0)
    yraw = ssd_attn_fwd(f"{tag}_ssd_fwd", cmc, bmc, xd, cum_a, cum_at, batch=batch, carried=carried)
    logf = rowwise(f"{tag}_fox_elt", _fox_elt, [sm], [w["bf_row"]], [(LANE, F32)])
    cum_f, cum_ft = seq_cumsum(f"{tag}_cum_f", logf, batch=batch, trow=FOX_LANE0)
    o, lse = fox_attn_fwd(f"{tag}_fox_fwd", proj, cum_f, cum_ft, batch=batch, carried=carried)
    pre = mm(xl, w["wax"], name=f"{tag}_mm_lru_gates")
    a, u = rowwise(f"{tag}_lru_elt", _lru_elt, [xl, pre], [w["b_ax"], w["lam"]], [(LRU_W, F32), (LRU_W, F32)])
    hl = lru_scan(f"{tag}_lru_scan", a, u, batch=batch)
    ycat = rowwise(f"{tag}_mix_post", _mix_post,
                   [yraw, xs_act, _view(proj, OFF_Z, SSD_W), hl, _view(proj, OFF_LG, LRU_W), o],
                   [w["dexp"], w["g_ssd"], w["g_lru"], w["g_fox"]], [(D_MODEL, BF16)], tr=256)
    saved = dict(cmc=cmc, bmc=bmc, xs_act=xs_act, xl=xl, xd=xd, cum_a=cum_a, cum_at=cum_at, yraw=yraw,
                 cum_f=cum_f, cum_ft=cum_ft, o=o, lse=lse, pre=pre, a=a, hl=hl)
    return ycat, saved


def mixer_bwd(dycat, proj, w, s, batch, tag, carried=None):
    sm = _view(proj, OFF_SM, LANE)
    g = {}

    def post_bwd(yraw, xs_act, z, hl, lg, o, dyc, dexp, g_ssd, g_lru, g_fox):
        return jax.vjp(_mix_post, yraw, xs_act, z, hl, lg, o, dexp, g_ssd, g_lru, g_fox)[1](dyc)

    (dyraw, dxs1, dz, dhl, dlg, do, g["dexp"], g["g_ssd"], g["g_lru"], g["g_fox"]) = rowwise(
        f"{tag}_mix_post_bwd", post_bwd,
        [s["yraw"], s["xs_act"], _view(proj, OFF_Z, SSD_W), s["hl"], _view(proj, OFF_LG, LRU_W), s["o"], dycat],
        [w["dexp"], w["g_ssd"], w["g_lru"], w["g_fox"]],
        [(SSD_W, F32), (SSD_W, F32), (SSD_W, BF16), (LRU_W, F32), (LRU_W, BF16), (FOX_W, F32)],
        [SSD_W, SSD_W, LRU_W, FOX_W], tr=256)

    dq, dk, dv, dcum3 = fox_attn_bwd(f"{tag}_fox_bwd", proj, s["o"], do, s["lse"], s["cum_f"], s["cum_ft"], batch=batch,
                                     carried=carried)
    dlogf = seq_cumsum(f"{tag}_rcum_f", dcum3, batch=batch, reverse=True, nsum=3)

    dxd, dbm, dcm, dcum_a = ssd_attn_bwd(f"{tag}_ssd_bwd", s["cmc"], s["bmc"], s["xd"], s["cum_a"], s["cum_at"], dyraw,
                                         batch=batch, carried=carried)
    dadt = seq_cumsum(f"{tag}_rcum_a", dcum_a, batch=batch, reverse=True)

    def ssd_elt_bwd(small, xs_act, dadt_, dxd_, dxs1_, dtbias, alog):
        dsm, dxs, ddtb, dalog = jax.vjp(_ssd_elt, small, xs_act, dtbias, alog)[1]((dadt_, dxd_))
        return dsm, dxs + dxs1_, ddtb, dalog

    dsm_s, dxs_act, g["dtbias_row"], g["alog_row"] = rowwise(
        f"{tag}_ssd_elt_bwd", ssd_elt_bwd, [sm, s["xs_act"], dadt, dxd, dxs1], [w["dtbias_row"], w["alog_row"]],
        [(LANE, F32), (SSD_W, F32)], [LANE, LANE])

    def fox_elt_bwd(small, dlogf_, dsm_s_, bf_row):
        dsm, dbf = jax.vjp(_fox_elt, small, bf_row)[1](dlogf_)
        return dsm + dsm_s_, dbf

    dsm, g["bf_row"] = rowwise(f"{tag}_fox_elt_bwd", fox_elt_bwd, [sm, dlogf, dsm_s], [w["bf_row"]],
                               [(LANE, BF16)], [LANE])

    cbwd = functools.partial(seq_conv_bwd, batch=batch)
    dxs_raw, g["cw_xs"], g["cb_xs"] = cbwd(f"{tag}_conv_xs_bwd", proj, OFF_XS, SSD_W, w["cw_xs"], w["cb_xs"], dxs_act, silu=True)
    db_raw, g["cw_b"], g["cb_b"] = cbwd(f"{tag}_conv_b_bwd", proj, OFF_B, 256, w["cw_b"], w["cb_b"], dbm, silu=True)
    dc_raw, g["cw_c"], g["cb_c"] = cbwd(f"{tag}_conv_c_bwd", proj, OFF_C, 256, w["cw_c"], w["cb_c"], dcm, silu=True)

    da, du = lru_scan_bwd(f"{tag}_lru_scan_bwd", s["a"], s["hl"], dhl, batch=batch)

    def lru_elt_bwd(xl, pre, da_, du_, b_ax, lam):
        return jax.vjp(_lru_elt, xl, pre, b_ax, lam)[1]((da_, du_))

    dxl1, dpre, g["b_ax"], g["lam"] = rowwise(
        f"{tag}_lru_elt_bwd", lru_elt_bwd, [s["xl"], s["pre"], da, du], [w["b_ax"], w["lam"]],
        [(LRU_W, F32), (2 * LRU_W, BF16)], [2 * LRU_W, LRU_W])
    g["wax"] = mm(s["xl"], dpre, ta=True, name=f"{tag}_mm_dwax")
    dxl = mm(dpre, w["wax"], tb=True, extras=[dxl1], epilogue=_add_epilogue, name=f"{tag}_mm_dxl")
    dlx_raw, g["cw_l"], g["cb_l"] = cbwd(f"{tag}_conv_l_bwd", proj, OFF_LX, LRU_W, w["cw_l"], w["cb_l"], dxl, silu=False)

    dproj = jnp.concatenate([db_raw, dc_raw, dlx_raw, dlg, dsm, dz, dxs_raw, dq, dk, dv], axis=1)
    return dproj, g


def layer_fwd(h0, p_l, w, batch, tag, carried=None):
    u1 = rowwise(f"{tag}_rms1", _rms, [h0], [w["g1"]], [(D_MODEL, BF16)], carried=carried)
    proj = mm(u1, w["win"], name=f"{tag}_mm_in", carried=carried)
    ycat, ms = mixer_fwd(proj, w, batch, tag, carried)
    add_norm = dict(epilogue=_add_rms_epilogue, out_dtypes=(F32, BF16), tm=512, tn=D_MODEL)
    h1, u2 = mm(ycat, w["wout"], extras=[h0], col_params=[w["g2"]], name=f"{tag}_mm_out", **add_norm)
    gate, up, act = mm(u2, [w["wg"], w["wu"]], tb=True, out_dtypes=(BF16, BF16, BF16), epilogue=_swiglu_epilogue,
                       tm=512, tn=D_FF // 2, name=f"{tag}_mm_gu", carried=carried)
    h2, u3 = mm(act, w["wd"], extras=[h1], col_params=[w["g3"]], tk=D_FF, name=f"{tag}_mm_down", **add_norm)
    pp = mm(p_l, w["wpp"], name=f"{tag}_mm_pp")
    h3, pg = mm(u3, w["wpg"], extras=[pp, h2], col_params=[w["b_pg"]], epilogue=_ple_epilogue,
                out_dtypes=(F32, F32), tm=512, name=f"{tag}_mm_pg")
    saved = dict(h0=h0, u1=u1, proj=proj, ycat=ycat, h1=h1, u2=u2, gate=gate, up=up, act=act, h2=h2, u3=u3, pg=pg,
                 pp=pp, mixer=ms)
    return h3, saved


def _add_rms_epilogue(acc, res, g):
    h = res + acc
    return h, _rms(h, g)


def _swiglu_epilogue(acc_g, acc_u):
    return acc_g, acc_u, _silu(acc_g) * acc_u


def _swiglu_bwd_epilogue(dact, gate, up):
    g, u = gate.astype(F32), up.astype(F32)
    s = jax.nn.sigmoid(g)
    silu = g * s
    return dact * u * (s + silu * (1.0 - s)), dact * silu


def _ple_epilogue(acc, pp, h2, b):
    return h2 + _ple(acc, pp, b), acc


def _rms_bwd_epilogue(du, h, dres, g):
    r = lax.rsqrt(jnp.mean(h * h, axis=-1, keepdims=True) + EPS)
    n = h * r
    t = du * n
    dh = r * (du * g - n * jnp.mean(t * g, axis=-1, keepdims=True))
    return dh + dres, _colsum(t)


def layer_bwd(dh3, p_l, w, s, batch, tag, carried=None, on_early_grads=None, on_w_in_grad=None):
    def ple_bwd(pg, pp, dh, b):
        return jax.vjp(_ple, pg, pp, b)[1](dh)

    norm_bwd = dict(epilogue=_rms_bwd_epilogue, partials=1, tm=512, tn=D_MODEL, tb=True)

    d_pg, d_pp, g_bpg = rowwise(f"{tag}_ple_bwd", ple_bwd, [s["pg"], s["pp"], dh3], [w["b_pg"]],
                                [(D_MODEL, BF16), (D_MODEL, BF16)], [D_MODEL])
    g = dict(b_pg=g_bpg)
    g["wpp"] = mm(p_l, d_pp, ta=True, name=f"{tag}_mm_dwpp")
    g["wpg"] = mm(s["u3"], d_pg, ta=True, name=f"{tag}_mm_dwpg", carried=carried)
    dh2, dg3 = mm(d_pg, w["wpg"], extras=[s["h2"], dh3], col_params=[w["g3"]], name=f"{tag}_mm_du3", **norm_bwd)
    g["g3"] = sum_slices(f"{tag}_sum_dg3", dg3)

    d_gate, d_up = mm(dh2, w["wd"], tb=True, extras=[s["gate"], s["up"]], epilogue=_swiglu_bwd_epilogue,
                      out_dtypes=(BF16, BF16), tm=512, tn=D_FF // 2, name=f"{tag}_mm_dact")
    g["wd"] = mm(s["act"], dh2, ta=True, name=f"{tag}_mm_dwd")
    g["wg"] = mm(d_gate, s["u2"], ta=True, name=f"{tag}_mm_dwg")
    g["wu"] = mm(d_up, s["u2"], ta=True, name=f"{tag}_mm_dwu")
    dh1, dg2 = mm([d_gate, d_up], [w["wg"], w["wu"]], extras=[s["h1"], dh2], col_params=[w["g2"]],
                  name=f"{tag}_mm_du2", **{**norm_bwd, "tb": False})
    g["g2"] = sum_slices(f"{tag}_sum_dg2", dg2)

    dycat = mm(dh1, w["wout"], tb=True, name=f"{tag}_mm_dycat")
    g["wout"] = mm(s["ycat"], dh1, ta=True, name=f"{tag}_mm_dwout")
    if on_early_grads is not None:
        on_early_grads(g)
    dproj, gm = mixer_bwd(dycat, s["proj"], w, s["mixer"], batch, tag, carried)
    g.update(gm)
    g["win"] = mm(s["u1"], dproj, ta=True, name=f"{tag}_mm_dwin")
    if on_w_in_grad is not None:
        on_w_in_grad(g["win"])
    dh0, dg1 = mm(dproj, w["win"], extras=[s["h0"], dh1], col_params=[w["g1"]], name=f"{tag}_mm_du1",
                  carried=carried, **norm_bwd)
    g["g1"] = sum_slices(f"{tag}_sum_dg1", dg1)
    return dh0, g


def _loss_fwd_bwd(h, tgt, gf):
    def f(h_, gf_):
        e = _rms(h_, gf_) - tgt
        return 0.5 * jnp.sum(jnp.mean(e * e, axis=-1, keepdims=True), axis=0, keepdims=True)

    loss, vj = jax.vjp(f, h, gf)
    dh, dgf = vj(jnp.ones((1, 1), F32))
    return dh, jnp.broadcast_to(loss, (1, LANE)), dgf


def local_step(x, p, tgt, layers, final_g, carried=None, on_layer_grads=None, on_early_grads=None, on_w_in_grad=None):
    batch, S, _ = x.shape
    T = batch * S
    h = x.reshape(T, D_MODEL)
    saved, weights = [], []
    for l, w in enumerate(layers):
        w = w() if callable(w) else w
        weights.append(w)
        h, s = layer_fwd(h, p[l].reshape(T, PLE_DIM), w, batch, f"l{l}", carried)
        saved.append(s)
    dh, loss, dgf = rowwise("loss", _loss_fwd_bwd, [h, tgt.reshape(T, D_MODEL)], [_row(final_g)],
                            [(D_MODEL, F32)], [LANE, D_MODEL], tr=256)
    grads = [None] * len(layers)
    for l in reversed(range(len(layers))):
        early = functools.partial(on_early_grads, l) if on_early_grads is not None else None
        w_in_hook = functools.partial(on_w_in_grad, l) if on_w_in_grad is not None else None
        dh, grads[l] = layer_bwd(dh, p[l].reshape(T, PLE_DIM), weights[l], saved[l], batch, f"l{l}", carried, early,
                                 w_in_hook)
        if on_layer_grads is not None:
            on_layer_grads(l, grads[l])
    return loss[0, 0], dh.reshape(batch, S, D_MODEL), grads, dgf[0]


MESH = pl.DeviceIdType.MESH
N_DEV = 8
N_CHIP = 4
ANY = pl.BlockSpec(memory_space=pl.ANY)


def _pos():
    return lax.axis_index("x"), lax.axis_index("y"), lax.axis_index("c")


def _comm_call(body, name, out_shape, n_in, scratch):
    return pl.pallas_call(body, name=name, out_shape=out_shape, in_specs=[ANY] * n_in, out_specs=ANY,
                          scratch_shapes=scratch)


def all_gather8(name, blk):
    def body(x_ref, out_ref, send_sems, recv_sems, local_sem):
        x, y, c = _pos()
        me, sibling = (x, y, c), (x, y, 1 - c)
        chips = [(1 - x, y), (x, 1 - y), (1 - x, 1 - y)]

        def rows(px, py, pcore):
            return out_ref.at[4 * px + 2 * py + pcore]

        def copy(k, block, to, src=None):
            return pltpu.make_async_remote_copy(
                src_ref=rows(*block) if src is None else src, dst_ref=rows(*block),
                send_sem=send_sems.at[k], recv_sem=recv_sems.at[k], device_id=to, device_id_type=MESH)

        mine = pltpu.make_async_copy(x_ref, rows(*me), local_sem)
        mine.start()
        first = [copy(0, me, sibling, src=x_ref)]
        first += [copy(1 + j, me, (*chip, c), src=x_ref) for j, chip in enumerate(chips)]
        for cp in first:
            cp.start()
        passed = [copy(4 + j, (*chip, c), sibling) for j, chip in enumerate(chips)]
        for j, chip in enumerate(chips):
            copy(1 + j, (*chip, c), me).wait_recv()
            passed[j].start()
        copy(0, sibling, me).wait_recv()
        for j, chip in enumerate(chips):
            copy(4 + j, (*chip, 1 - c), me).wait_recv()
        for cp in first + passed:
            cp.wait_send()
        mine.wait()

    return _comm_call(body, name, SDS((N_DEV,) + blk.shape, blk.dtype), 1,
                      [pltpu.SemaphoreType.DMA((7,)), pltpu.SemaphoreType.DMA((7,)), pltpu.SemaphoreType.DMA])(blk)


class Exchange:
    def __init__(self, inputs, out_shapes, sems, start, wait, aliases=None):
        self.inputs, self.out_shapes, self.sems = list(inputs), list(out_shapes), list(sems)
        self.start, self.wait, self.aliases = start, wait, dict(aliases or {})


def combine(a, b):
    ai, ao, as_ = len(a.inputs), len(a.out_shapes), len(a.sems)

    def split(cins, couts, sems):
        return (cins[:ai], couts[:ao], sems[:as_]), (cins[ai:], couts[ao:], sems[as_:])

    def start(cins, couts, sems):
        pa, pb = split(cins, couts, sems)
        a.start(*pa)
        b.start(*pb)

    def wait(cins, couts, sems):
        pa, pb = split(cins, couts, sems)
        a.wait(*pa)
        b.wait(*pb)

    aliases = dict(a.aliases)
    aliases.update({ai + i: ao + o for i, o in b.aliases.items()})
    return Exchange(a.inputs + b.inputs, a.out_shapes + b.out_shapes, a.sems + b.sems, start, wait, aliases)


def run_exchange(name, ex):
    n_ci, n_co = len(ex.inputs), len(ex.out_shapes)

    def body(*refs):
        cins, couts, csems = refs[:n_ci], refs[n_ci:n_ci + n_co], refs[n_ci + n_co:]
        ex.start(cins, couts, csems)
        ex.wait(cins, couts, csems)

    return pl.pallas_call(body, name=name, out_shape=ex.out_shapes, in_specs=[ANY] * n_ci, out_specs=[ANY] * n_co,
                          scratch_shapes=ex.sems, input_output_aliases=ex.aliases)(*ex.inputs)


def _peers():
    x, y, c = _pos()
    return x, y, c, 2 * x + y, [(1 - x, y), (x, 1 - y), (1 - x, 1 - y)]


def _remote(src, dst, send_sem, recv_sem, to):
    return pltpu.make_async_remote_copy(src_ref=src, dst_ref=dst, send_sem=send_sem, recv_sem=recv_sem,
                                        device_id=to, device_id_type=MESH)


def gather_spread(shards, layer):
    n_t = len(shards)
    halves = [s.shape[1] // 2 for s in shards]

    def copies(cins, couts, sems):
        send_sems, recv_sems, local_sems = sems
        x, y, c, my_chip, chips = _peers()
        local, sends, recvs = [], [], []
        for t in range(n_t):
            h = halves[t]
            src = cins[t].at[layer, pl.ds(c * h, h)]
            mine = couts[t].at[my_chip, pl.ds(c * h, h)]
            local.append(pltpu.make_async_copy(src, mine, local_sems.at[t]))
            sends.append(_remote(src, mine, send_sems.at[0, t], recv_sems.at[0, t], (x, y, 1 - c)))
            recvs.append(_remote(src, couts[t].at[my_chip, pl.ds((1 - c) * h, h)], send_sems.at[0, t],
                                 recv_sems.at[0, t], (x, y, 1 - c)))
            for j, (px, py) in enumerate(chips):
                sends.append(_remote(src, mine, send_sems.at[1 + j, t], recv_sems.at[1 + j, t], (px, py, c)))
                recvs.append(_remote(src, couts[t].at[2 * px + py, pl.ds(c * h, h)], send_sems.at[1 + j, t],
                                     recv_sems.at[1 + j, t], (px, py, c)))
        return local, sends, recvs

    def start(cins, couts, sems):
        local, sends, _ = copies(cins, couts, sems)
        for cp in local + sends:
            cp.start()

    def wait(cins, couts, sems):
        local, sends, recvs = copies(cins, couts, sems)
        for cp in recvs:
            cp.wait_recv()
        for cp in sends:
            cp.wait_send()
        for cp in local:
            cp.wait()

    return Exchange(shards, [SDS((N_CHIP,) + s.shape[1:], s.dtype) for s in shards],
                    [pltpu.SemaphoreType.DMA((4, n_t)), pltpu.SemaphoreType.DMA((4, n_t)),
                     pltpu.SemaphoreType.DMA((n_t,))], start, wait)


def gather_pass_on(slots):
    n_t = len(slots)
    halves = [s.shape[1] // 2 for s in slots]

    def copies(cins, couts, sems):
        send_sems, recv_sems = sems
        x, y, c, my_chip, chips = _peers()
        sends, recvs = [], []
        for t in range(n_t):
            h = halves[t]
            for j, (px, py) in enumerate(chips):
                k = 2 * px + py
                sends.append(_remote(cins[t].at[k, pl.ds(c * h, h)], couts[t].at[k, pl.ds(c * h, h)],
                                     send_sems.at[j, t], recv_sems.at[j, t], (x, y, 1 - c)))
                recvs.append(_remote(cins[t].at[k, pl.ds(c * h, h)], couts[t].at[k, pl.ds((1 - c) * h, h)],
                                     send_sems.at[j, t], recv_sems.at[j, t], (x, y, 1 - c)))
        return sends, recvs

    def start(cins, couts, sems):
        for cp in copies(cins, couts, sems)[0]:
            cp.start()

    def wait(cins, couts, sems):
        sends, recvs = copies(cins, couts, sems)
        for cp in recvs:
            cp.wait_recv()
        for cp in sends:
            cp.wait_send()

    return Exchange(slots, [SDS(s.shape, s.dtype) for s in slots],
                    [pltpu.SemaphoreType.DMA((3, n_t)), pltpu.SemaphoreType.DMA((3, n_t))], start, wait,
                    aliases={t: t for t in range(n_t)})


def chips_exchange(vs):
    n_t = len(vs)

    def copies(cins, couts, sems):
        send_sems, recv_sems, local_sems = sems
        x, y, c, my_chip, chips = _peers()
        local = [pltpu.make_async_copy(cins[t].at[my_chip], couts[t].at[my_chip], local_sems.at[t]) for t in range(n_t)]
        sends, recvs = [], []
        for k, (px, py) in enumerate(chips):
            for t in range(n_t):
                sends.append(_remote(cins[t].at[2 * px + py], couts[t].at[my_chip], send_sems.at[k, t],
                                     recv_sems.at[k, t], (px, py, c)))
                recvs.append(_remote(cins[t].at[my_chip], couts[t].at[2 * px + py], send_sems.at[k, t],
                                     recv_sems.at[k, t], (px, py, c)))
        return local, sends, recvs

    def start(cins, couts, sems):
        local, sends, _ = copies(cins, couts, sems)
        for cp in local + sends:
            cp.start()

    def wait(cins, couts, sems):
        local, sends, recvs = copies(cins, couts, sems)
        for cp in recvs:
            cp.wait_recv()
        for cp in sends:
            cp.wait_send()
        for cp in local:
            cp.wait()

    return Exchange(vs, [SDS(v.shape, v.dtype) for v in vs],
                    [pltpu.SemaphoreType.DMA((3, n_t)), pltpu.SemaphoreType.DMA((3, n_t)),
                     pltpu.SemaphoreType.DMA((n_t,))], start, wait)


def sibling_exchange(vs):
    n_t = len(vs)

    def copies(cins, couts, sems):
        x, y, c = _pos()
        return [_remote(cins[t], couts[t], sems[0].at[t], sems[1].at[t], (x, y, 1 - c)) for t in range(n_t)]

    def start(cins, couts, sems):
        for cp in copies(cins, couts, sems):
            cp.start()

    def wait(cins, couts, sems):
        for cp in copies(cins, couts, sems):
            cp.wait()

    return Exchange(vs, [SDS(v.shape, v.dtype) for v in vs],
                    [pltpu.SemaphoreType.DMA((n_t,)), pltpu.SemaphoreType.DMA((n_t,))], start, wait)


def swap_with_sibling(name, vs):
    return run_exchange(name, sibling_exchange(vs))


_ROW_BLOCKS = (1024, 704, 512, 352, 256, 128, 64, 32, 16, 8)


def sum_slices(name, v, tr=512):
    n, R, C = v.shape
    tr = _pick(R, _ROW_BLOCKS)

    def body(v_ref, o_ref):
        acc = v_ref[0].astype(F32)
        for k in range(1, n):
            acc = acc + v_ref[k].astype(F32)
        o_ref[...] = acc

    return _pc(body, name=name, grid=(R // tr,), in_specs=[pl.BlockSpec((n, tr, C), lambda i: (0, i, 0))],
               out_specs=pl.BlockSpec((tr, C), lambda i: (i, 0)), out_shape=SDS((R, C), F32), sem=("parallel",))(v)


def add_slices(name, a, b, out_dtype):
    n, R, C = a.shape
    tr = _pick(R, _ROW_BLOCKS)

    def body(a_ref, b_ref, o_ref):
        o_ref[...] = (a_ref[...].astype(F32) + b_ref[...].astype(F32)).astype(o_ref.dtype)

    spec = pl.BlockSpec((1, tr, C), lambda k, i: (k, i, 0))
    return _pc(body, name=name, grid=(n, R // tr), in_specs=[spec, spec], out_specs=spec,
               out_shape=SDS(a.shape, out_dtype), sem=("parallel", "parallel"))(a, b)


def adamw(name, w, g, m, v):
    L, R, C = w.shape
    tr = _pick(R, (512, 352, 256, 128, 64, 32, 16, 8))
    c1 = 1.0 / (1.0 - ADAM_B1 ** ADAM_STEP)
    c2 = 1.0 / (1.0 - ADAM_B2 ** ADAM_STEP)

    def body(w_ref, g_ref, m_ref, v_ref, d_ref, nm_ref, nv_ref):
        gv = g_ref[...]
        nm = ADAM_B1 * m_ref[...] + (1.0 - ADAM_B1) * gv
        nv = ADAM_B2 * v_ref[...] + (1.0 - ADAM_B2) * (gv * gv)
        d_ref[...] = -ADAM_LR * ((nm * c1) / (jnp.sqrt(nv * c2) + ADAM_EPS) + ADAM_WD * w_ref[...])
        nm_ref[...] = nm
        nv_ref[...] = nv

    if R < 8:
        tl = 64
        spec = pl.BlockSpec((tl, R, C), lambda i, _: (i, 0, 0))
        grid = (pl.cdiv(L, tl), 1)
    else:
        spec = pl.BlockSpec((1, tr, C), lambda l, i: (l, i, 0))
        grid = (L, R // tr)
    return _pc(body, name=name, grid=grid, in_specs=[spec] * 4, out_specs=[spec] * 3,
               out_shape=[SDS(w.shape, F32)] * 3, sem=("parallel", "parallel"))(w, g, m, v)


WEIGHTS = ["norm1_g", "w_in", "ssd_conv_w", "ssd_conv_b", "ssd_dt_bias", "ssd_a_log", "ssd_d", "ssd_norm_g",
           "lru_conv_w", "lru_conv_b", "lru_w_a", "lru_b_a", "lru_w_x", "lru_b_x", "lru_lambda", "lru_norm_g",
           "fox_b_f", "fox_norm_g", "w_out", "norm2_g", "w_gate", "w_up", "w_down", "norm3_g", "w_ple_gate",
           "b_ple_gate", "w_ple_proj", "final_norm_g"]
BIG = {"w_in": 2, "w_out": 1, "w_gate": 2, "w_up": 2, "w_down": 1, "w_ple_gate": 1, "w_ple_proj": 2}
SHARDED_SMALL = {"ssd_conv_w": 2, "lru_conv_w": 2}
ADAM_VIEW = {"w_in": ((2, 0, 1), (1, 2, 0)), "w_gate": ((0, 2, 1), (0, 2, 1)), "w_up": ((0, 2, 1), (0, 2, 1))}
TRANSPOSED = ("w_gate", "w_up")
SMALL = [n for n in WEIGHTS if n not in BIG]


def _pack(arrs, rows_multiple):
    flat = jnp.concatenate([a.reshape(-1) for a in arrs])
    per = rows_multiple * LANE
    n = -(-flat.shape[0] // per) * per
    return jnp.pad(flat, (0, n - flat.shape[0])).reshape(n // LANE, LANE)


def _unpack(flat2d, shapes):
    flat = flat2d.reshape(-1)
    out, off = [], 0
    for s in shapes:
        n = int(np.prod(s))
        out.append(flat[off:off + n].reshape(s))
        off += n
    return out


def _gather_shards(name, shards, axes, dtype):
    c = lax.axis_index("c")
    packed = _pack([s.astype(dtype) for s in shards], 32)
    half = packed.shape[0] // 2
    mine = lax.dynamic_slice_in_dim(packed, c * half, half, 0)
    got = all_gather8(name, mine).reshape(N_CHIP, 2 * half, LANE)
    per_chip = [_unpack(got[k], [s.shape for s in shards]) for k in range(N_CHIP)]
    return [jnp.concatenate([per_chip[k][i] for k in range(N_CHIP)], axis=ax) for i, ax in enumerate(axes)]


def kernel(x, p, norm1_g, w_in, ssd_conv_w, ssd_conv_b, ssd_dt_bias, ssd_a_log, ssd_d, ssd_norm_g, lru_conv_w, lru_conv_b, lru_w_a, lru_b_a, lru_w_x, lru_b_x, lru_lambda, lru_norm_g, fox_b_f, fox_norm_g, w_out, norm2_g, w_gate, w_up, w_down, norm3_g, w_ple_gate, b_ple_gate, w_ple_proj, final_norm_g, loss_target, m_norm1_g, m_w_in, m_ssd_conv_w, m_ssd_conv_b, m_ssd_dt_bias, m_ssd_a_log, m_ssd_d, m_ssd_norm_g, m_lru_conv_w, m_lru_conv_b, m_lru_w_a, m_lru_b_a, m_lru_w_x, m_lru_b_x, m_lru_lambda, m_lru_norm_g, m_fox_b_f, m_fox_norm_g, m_w_out, m_norm2_g, m_w_gate, m_w_up, m_w_down, m_norm3_g, m_w_ple_gate, m_b_ple_gate, m_w_ple_proj, m_final_norm_g, v_norm1_g, v_w_in, v_ssd_conv_w, v_ssd_conv_b, v_ssd_dt_bias, v_ssd_a_log, v_ssd_d, v_ssd_norm_g, v_lru_conv_w, v_lru_conv_b, v_lru_w_a, v_lru_b_a, v_lru_w_x, v_lru_b_x, v_lru_lambda, v_lru_norm_g, v_fox_b_f, v_fox_norm_g, v_w_out, v_norm2_g, v_w_gate, v_w_up, v_w_down, v_norm3_g, v_w_ple_gate, v_b_ple_gate, v_w_ple_proj, v_final_norm_g):
    args = (norm1_g, w_in, ssd_conv_w, ssd_conv_b, ssd_dt_bias, ssd_a_log, ssd_d, ssd_norm_g, lru_conv_w, lru_conv_b, lru_w_a, lru_b_a, lru_w_x, lru_b_x, lru_lambda, lru_norm_g, fox_b_f, fox_norm_g, w_out, norm2_g, w_gate, w_up, w_down, norm3_g, w_ple_gate, b_ple_gate, w_ple_proj, final_norm_g)
    m_args = (m_norm1_g, m_w_in, m_ssd_conv_w, m_ssd_conv_b, m_ssd_dt_bias, m_ssd_a_log, m_ssd_d, m_ssd_norm_g, m_lru_conv_w, m_lru_conv_b, m_lru_w_a, m_lru_b_a, m_lru_w_x, m_lru_b_x, m_lru_lambda, m_lru_norm_g, m_fox_b_f, m_fox_norm_g, m_w_out, m_norm2_g, m_w_gate, m_w_up, m_w_down, m_norm3_g, m_w_ple_gate, m_b_ple_gate, m_w_ple_proj, m_final_norm_g)
    v_args = (v_norm1_g, v_w_in, v_ssd_conv_w, v_ssd_conv_b, v_ssd_dt_bias, v_ssd_a_log, v_ssd_d, v_ssd_norm_g, v_lru_conv_w, v_lru_conv_b, v_lru_w_a, v_lru_b_a, v_lru_w_x, v_lru_b_x, v_lru_lambda, v_lru_norm_g, v_fox_b_f, v_fox_norm_g, v_w_out, v_norm2_g, v_w_gate, v_w_up, v_w_down, v_norm3_g, v_w_ple_gate, v_b_ple_gate, v_w_ple_proj, v_final_norm_g)
    w = dict(zip(WEIGHTS, args))
    mom = dict(zip(WEIGHTS, m_args))
    var = dict(zip(WEIGHTS, v_args))
    xi, yi, ci = _pos()
    chip = 2 * xi + yi

    big_names = list(BIG)
    later = [n for n in big_names if n != "w_in"]
    by_rows = {n: BIG[n] == 1 or n in TRANSPOSED for n in big_names}
    wb = {n: (jnp.transpose(w[n], (0, 2, 1)) if n in TRANSPOSED else w[n]).astype(BF16) for n in big_names}
    conv_full = dict(zip(SHARDED_SMALL, _gather_shards("gather_conv", [w[n] for n in SHARDED_SMALL],
                                                       list(SHARDED_SMALL.values()), F32)))
    carried = Carried()

    def layer_weights(l, slots_of):
        def assemble(n):
            s4 = slots_of(n)
            return (s4.reshape(-1, s4.shape[-1]) if by_rows[n]
                    else jnp.concatenate([s4[k] for k in range(N_CHIP)], axis=1))

        f = LazyDict({n: (conv_full[n][l] if n in conv_full else w[n][l]) for n in SMALL if n != "final_norm_g"})
        f.update({(n + "_t" if n in TRANSPOSED else n): functools.partial(assemble, n) for n in big_names})
        return prep_layer(f)

    carried.offer("l0_rms1", lambda: gather_spread([wb["w_in"]], 0))

    def w_in0():
        return run_exchange("gather0_in_pass_on", gather_pass_on(carried.results["l0_rms1"]))[0]

    n_early = 2
    carried.offer("l0_mm_in", lambda: gather_spread([wb[n] for n in later[:n_early]], 0))
    carried.offer("l0_ssd_fwd", lambda: gather_spread([wb[n] for n in later[n_early:]], 0))
    carried.offer("l0_fox_fwd", lambda: combine(
        gather_pass_on(carried.results["l0_mm_in"] + carried.results["l0_ssd_fwd"]),
        gather_spread([wb[n] for n in big_names], 1)))
    carried.offer("l0_mm_gu", lambda: gather_pass_on(carried.results["l0_fox_fwd"][len(later):]))
    layers = [
        layer_weights(0, lambda n: w_in0() if n == "w_in" else carried.results["l0_fox_fwd"][later.index(n)]),
        lambda: layer_weights(1, lambda n: carried.results["l0_mm_gu"][big_names.index(n)])]

    def chip_slices(a, n):
        return a.reshape(N_CHIP, -1, a.shape[1]) if by_rows[n] else jnp.stack(jnp.split(a, N_CHIP, axis=1))

    def halves(names, full_grads):
        keep, give = [], []
        for n, a in zip(names, full_grads):
            s4 = chip_slices(a, n)
            h = s4.shape[1] // 2
            keep.append(lax.dynamic_slice_in_dim(s4, ci * h, h, 1))
            give.append(lax.dynamic_slice_in_dim(s4, (1 - ci) * h, h, 1).astype(BF16))
        return keep, give

    def add_halves(tag, names, keep, got):
        return [add_slices(f"add_sibling{tag}_{n}", k_, g_, BF16) for n, k_, g_ in zip(names, keep, got)]

    kernel_key = dict(w_out="wout", w_gate="wg", w_up="wu", w_down="wd", w_ple_gate="wpg", w_ple_proj="wpp")
    n_big = len(big_names)
    gl, summed = [None] * DEPTH, [None] * DEPTH

    def on_layer_grads(l, g_layer):
        gl[l] = unprep_grads(g_layer)
        if l == 1:
            keep, give = halves(big_names, [gl[1]["w_in"]] + [g_layer[kernel_key[n]] for n in later])
            carried.offer("l0_mm_dwpg", lambda: sibling_exchange(give))
            carried.offer("l0_fox_bwd", lambda: chips_exchange(
                add_halves("1", big_names, keep, carried.results["l0_mm_dwpg"])))

    def on_early_grads(l, g_layer):
        if l == 0:
            keep, give = halves(later, [g_layer[kernel_key[n]] for n in later])
            layer1_exchange = carried.offers.pop("l0_fox_bwd")
            carried.offer("l0_fox_bwd", lambda: combine(layer1_exchange(), sibling_exchange(give)))
            def ride_with_ssd_bwd():
                summed[1] = [sum_slices(f"sum_chips1_{n}", a_)
                             for n, a_ in zip(big_names, carried.results["l0_fox_bwd"][:n_big])]
                return combine(
                    chips_exchange(add_halves("0_later", later, keep, carried.results["l0_fox_bwd"][n_big:])),
                    sibling_exchange(summed[1]))

            carried.offer("l0_ssd_bwd", ride_with_ssd_bwd)

    def on_w_in_grad(l, g_win):
        if l == 0:
            keep, give = halves(["w_in"], [unpermute_in_cols(g_win)])
            part = add_halves("0_in", ["w_in"], keep, swap_with_sibling("swap_halves0_in", give))
            carried.offer("l0_mm_du1", lambda: chips_exchange(part))

    loss, grad_x, grads, g_final = local_step(x, p, loss_target, layers, final_norm_g, carried, on_layer_grads,
                                              on_early_grads, on_w_in_grad)
    loss = lax.psum(loss, ("x", "y", "c"))
    arrived0 = dict(zip(later, carried.results["l0_ssd_bwd"][:len(later)]))
    arrived0["w_in"] = carried.results["l0_mm_du1"][0]

    gsmall = {n: jnp.stack([gl[l][n] for l in range(DEPTH)]) for n in SMALL if n != "final_norm_g"}
    gsmall["final_norm_g"] = g_final
    small_shapes = [gsmall[n].shape for n in SMALL]
    gs = _pack([gsmall[n] for n in SMALL], 8)
    gs = sum_slices("sum_small", all_gather8("gather_small_grads", gs))
    gsum = dict(zip(SMALL, _unpack(gs, small_shapes)))
    for n, ax in SHARDED_SMALL.items():
        k = gsum[n].shape[ax] // N_CHIP
        gsum[n] = lax.dynamic_slice_in_dim(gsum[n], chip * k, k, ax)

    summed[0] = [sum_slices(f"sum_chips0_{n}", arrived0[n]) for n in big_names]
    others = [swap_with_sibling("swap_results0", summed[0]), carried.results["l0_ssd_bwd"][len(later):]]
    done = [[jnp.concatenate([jnp.where(ci == 0, m_, o_), jnp.where(ci == 0, o_, m_)], axis=0)
             for m_, o_ in zip(summed[l], others[l])] for l in range(DEPTH)]
    gview = {}
    for t, n in enumerate(big_names):
        g2 = jnp.stack([done[l][t] for l in range(DEPTH)])
        if n in TRANSPOSED:
            gview[n], gsum[n] = g2, jnp.transpose(g2, (0, 2, 1))
        else:
            gsum[n] = g2

    delta, new_m, new_v = {}, {}, {}
    for n in big_names:
        if n in ADAM_VIEW:
            to_view, back = ADAM_VIEW[n]
            gv = gview[n] if n in gview else jnp.transpose(gsum[n], to_view)
            outs = adamw(f"adamw_{n}", jnp.transpose(w[n], to_view), gv, jnp.transpose(mom[n], to_view),
                         jnp.transpose(var[n], to_view))
            delta[n], new_m[n], new_v[n] = [jnp.transpose(o, back) for o in outs]
        else:
            delta[n], new_m[n], new_v[n] = adamw(f"adamw_{n}", w[n], gsum[n], mom[n], var[n])
    shapes = [w[n].shape for n in SMALL]
    pk = lambda d: _pack([d[n] for n in SMALL], 8)[None]
    ds, ms, vs = adamw("adamw_small", pk(w), pk(gsum), pk(mom), pk(var))
    for d, packed in ((delta, ds), (new_m, ms), (new_v, vs)):
        d.update(zip(SMALL, _unpack(packed[0], shapes)))

    return (loss, grad_x, *[gsum[n] for n in WEIGHTS], *[delta[n] for n in WEIGHTS],
            *[new_m[n] for n in WEIGHTS], *[new_v[n] for n in WEIGHTS])
```

```python
import functools
import math

import jax
import jax.numpy as jnp
import numpy as np
from jax import lax
from jax.experimental import pallas as pl
from jax.experimental.pallas import tpu as pltpu

F32, BF16 = jnp.float32, jnp.bfloat16
SDS = jax.ShapeDtypeStruct

D_MODEL = 1024
DEPTH = 2
HEAD_DIM = 64
N_HEADS = 6
SSD_W, LRU_W, FOX_W = 384, 256, 384
D_FF = 2816
PLE_DIM = 256
EPS = 1e-6
LRU_C = 8.0
LANE = 128
V7X_VMEM_LIMIT = 56 * 1024 * 1024

PW = 3072
OFF_B, OFF_C, OFF_LX, OFF_LG, OFF_SM, OFF_Z, OFF_XS, OFF_Q, OFF_K, OFF_V = (
    0, 256, 512, 768, 1024, 1152, 1536, 1920, 2304, 2688)
FOX_LANE0 = 8

ADAM_LR, ADAM_B1, ADAM_B2, ADAM_EPS, ADAM_WD, ADAM_STEP = 0.001, 0.9, 0.999, 1e-08, 0.01, 10


def _iota(shape, dim):
    return lax.broadcasted_iota(jnp.int32, shape, dim)


class Carried:
    def __init__(self):
        self.offers, self.results = {}, {}

    def offer(self, call_name, make_exchange):
        self.offers[call_name] = make_exchange

    def take(self, call_name):
        make = self.offers.pop(call_name, None)
        return None if make is None else make()

    def deliver(self, call_name, results):
        self.results[call_name] = results


class LazyDict(dict):
    def __getitem__(self, key):
        v = dict.__getitem__(self, key)
        if callable(v):
            v = v()
            dict.__setitem__(self, key, v)
        return v


def _run(call, args, name, comm, carried):
    if comm is None:
        return call(*args)
    own, brought = call(*args)
    carried.deliver(name, brought)
    return own


def _pc(body, *, name, grid, in_specs, out_specs, out_shape, scratch=(), sem=None, comm=None):
    if comm is None:
        return pl.pallas_call(
            body, name=name, grid=grid, in_specs=in_specs, out_specs=out_specs, out_shape=out_shape,
            scratch_shapes=list(scratch),
            compiler_params=pltpu.CompilerParams(dimension_semantics=sem, vmem_limit_bytes=V7X_VMEM_LIMIT))
    single = not isinstance(out_shape, (list, tuple))
    out_specs_l = [out_specs] if single else list(out_specs)
    out_shape_l = [out_shape] if single else list(out_shape)
    n_in, n_out, n_scr, n_ci, n_co = len(in_specs), len(out_shape_l), len(scratch), len(comm.inputs), len(comm.out_shapes)

    def hosted(*refs):
        ins, cins = refs[:n_in], refs[n_in:n_in + n_ci]
        outs, couts = refs[n_in + n_ci:n_in + n_ci + n_out], refs[n_in + n_ci + n_out:n_in + n_ci + n_out + n_co]
        rest = refs[n_in + n_ci + n_out + n_co:]
        scr, csems = rest[:n_scr], rest[n_scr:]
        ids = [pl.program_id(d) for d in range(len(grid))]
        first = functools.reduce(jnp.logical_and, [i == 0 for i in ids])
        last = functools.reduce(jnp.logical_and, [i == g - 1 for i, g in zip(ids, grid)])

        @pl.when(first)
        def _():
            comm.start(cins, couts, csems)

        body(*ins, *outs, *scr)

        @pl.when(last)
        def _():
            comm.wait(cins, couts, csems)

    call = pl.pallas_call(
        hosted, name=name, grid=grid, in_specs=list(in_specs) + [ANY] * n_ci,
        out_specs=out_specs_l + [ANY] * n_co, out_shape=out_shape_l + list(comm.out_shapes),
        scratch_shapes=list(scratch) + list(comm.sems),
        input_output_aliases={n_in + a: n_out + b for a, b in comm.aliases.items()},
        compiler_params=pltpu.CompilerParams(dimension_semantics=("arbitrary",) * len(grid),
                                             vmem_limit_bytes=V7X_VMEM_LIMIT))

    def run(*args):
        res = call(*args, *comm.inputs)
        own = res[:n_out]
        return (own[0] if single else own), list(res[n_out:])

    return run


def permute_in_cols(w):
    z = lambda n: jnp.zeros(w.shape[:-1] + (n,), w.dtype)
    s = lambda a, b: w[..., a:b]
    return jnp.concatenate([
        s(768, 1024), s(1024, 1280), s(1286, 1542), s(1542, 1798),
        s(1280, 1286), z(2), s(2950, 2956), z(LANE - 14),
        s(0, 384), s(384, 768), s(1798, 2182), s(2182, 2566), s(2566, 2950)], axis=-1)


def unpermute_in_cols(g):
    s = lambda a, n: g[..., a:a + n]
    return jnp.concatenate([
        s(OFF_Z, 384), s(OFF_XS, 384), s(OFF_B, 256), s(OFF_C, 256), s(OFF_SM, 6),
        s(OFF_LX, 256), s(OFF_LG, 256), s(OFF_Q, 384), s(OFF_K, 384), s(OFF_V, 384),
        s(OFF_SM + FOX_LANE0, 6)], axis=-1)


def _pick(n, cands):
    for c in cands:
        if n % c == 0:
            return c
    return n


def mm(a, b, *, name, ta=False, tb=False, out_dtypes=(F32,), extras=(), col_params=(), partials=0, epilogue=None,
       tm=None, tn=None, tk=None, carried=None):
    bs = list(b) if isinstance(b, (list, tuple)) else [b]
    pair_sum = isinstance(a, (list, tuple))
    a_list = list(a) if pair_sum else [a]
    assert not pair_sum or len(a_list) == len(bs)
    a = a_list[0]
    n_a = len(a_list)
    n_acc = 1 if pair_sum else len(bs)
    extras = [e if isinstance(e, tuple) else (e, 0) for e in extras]
    M = a.shape[1] if ta else a.shape[0]
    K = a.shape[0] if ta else a.shape[1]
    N = bs[0].shape[0] if tb else bs[0].shape[1]
    tm = tm or _pick(M, (1024, 1408, 512, 256, 128))
    tn = tn or _pick(N, (1024, 1408, 768, 512, 256, 128))
    if tk is None and ta and K % 2048 == 0 and 2048 * (tm * a.dtype.itemsize + tn * bs[0].dtype.itemsize) <= 12 << 20:
        tk = 2048
    tk = tk or _pick(K, (1024, 1408, 512, 256, 128))
    nm, nn, nk = M // tm, N // tn, K // tk
    n_b, n_ex, n_cp, n_out = len(bs), len(extras), len(col_params), len(out_dtypes)
    a_bytes, b_bytes = n_a * M * K * a.dtype.itemsize, n_b * K * N * bs[0].dtype.itemsize
    rows_inner = a_bytes * nn + b_bytes <= a_bytes + b_bytes * nm

    def ij(g0, g1):
        return (g1, g0) if rows_inner else (g0, g1)

    def body(*rest):
        a_refs, rest = rest[:n_a], rest[n_a:]
        b_refs, rest = rest[:n_b], rest[n_b:]
        in_refs, rest = rest[:n_ex + n_cp], rest[n_ex + n_cp:]
        out_refs, accs = rest[:n_out + partials], rest[n_out + partials:]
        dn = (((0 if ta else 1,), (1 if tb else 0,)), ((), ()))
        dot = lambda x_ref, y_ref: lax.dot_general(x_ref[...].astype(BF16), y_ref[...].astype(BF16), dn,
                                                   preferred_element_type=F32)
        if pair_sum:
            parts = [functools.reduce(lambda u, v: u + v, [dot(x, y) for x, y in zip(a_refs, b_refs)])]
        else:
            parts = [dot(a_refs[0], b_ref) for b_ref in b_refs]

        def finish(rs):
            outs = epilogue(*rs, *[e[...] for e in in_refs]) if epilogue is not None else tuple(rs)
            for o_ref, o in zip(out_refs[:n_out], outs):
                o_ref[...] = o.astype(o_ref.dtype)
            for o_ref, o in zip(out_refs[n_out:], outs[n_out:]):
                o_ref[0] = o

        if nk == 1:
            finish(parts)
            return
        k = pl.program_id(2)

        @pl.when(k == 0)
        def _():
            for acc, part in zip(accs, parts):
                acc[...] = part

        @pl.when(k > 0)
        def _():
            for acc, part in zip(accs, parts):
                acc[...] += part

        @pl.when(k == nk - 1)
        def _():
            finish([acc[...] for acc in accs])

    def a_map(g0, g1, k):
        i, _ = ij(g0, g1)
        return (k, i) if ta else (i, k)

    def b_map(g0, g1, k):
        _, j = ij(g0, g1)
        return (j, k) if tb else (k, j)

    def ex_map(off, g0, g1, k):
        i, j = ij(g0, g1)
        return (i, j + off)

    a_spec = pl.BlockSpec((tk, tm) if ta else (tm, tk), a_map)
    b_spec = pl.BlockSpec((tn, tk) if tb else (tk, tn), b_map)
    mn_spec = pl.BlockSpec((tm, tn), functools.partial(ex_map, 0))
    comm = carried.take(name) if carried is not None else None
    call = _pc(body, name=name, grid=(nn, nm, nk) if rows_inner else (nm, nn, nk),
               in_specs=([a_spec] * n_a + [b_spec] * n_b
                         + [pl.BlockSpec((tm, tn), functools.partial(ex_map, off)) for _, off in extras]
                         + [pl.BlockSpec((1, tn), lambda g0, g1, k: (0, ij(g0, g1)[1]))] * n_cp),
               out_specs=([mn_spec] * n_out
                          + [pl.BlockSpec((1, 1, tn), lambda g0, g1, k: (ij(g0, g1)[0], 0, ij(g0, g1)[1]))] * partials),
               out_shape=[SDS((M, N), dt) for dt in out_dtypes] + [SDS((nm, 1, N), F32)] * partials,
               scratch=[pltpu.VMEM((tm, tn), F32)] * n_acc if nk > 1 else [],
               sem=("parallel", "parallel", "arbitrary"), comm=comm)
    outs = _run(call, (*a_list, *bs, *[e for e, _ in extras], *col_params), name, comm, carried)
    return outs[0] if len(outs) == 1 else outs


def rowwise(name, fn, rows, params, row_outs, acc_outs=(), tr=512, carried=None):
    rows = [r if isinstance(r, tuple) else (r, 0, r.shape[1]) for r in rows]
    T = rows[0][0].shape[0]
    tr = min(tr, T)
    n_in, n_ro, n_ac = len(rows) + len(params), len(row_outs), len(acc_outs)

    def body(*refs):
        ins, outs = refs[:n_in], refs[n_in:]
        res = fn(*[r[...] for r in ins])
        if not isinstance(res, (tuple, list)):
            res = (res,)
        for k in range(n_ro):
            outs[k][...] = res[k].astype(outs[k].dtype)
        if n_ac:
            i = pl.program_id(0)

            @pl.when(i == 0)
            def _():
                for k in range(n_ac):
                    outs[n_ro + k][...] = res[n_ro + k]

            @pl.when(i > 0)
            def _():
                for k in range(n_ac):
                    outs[n_ro + k][...] += res[n_ro + k]

    in_specs = ([pl.BlockSpec((tr, w), functools.partial(lambda cb, i: (i, cb), cb)) for (_, cb, w) in rows]
                + [pl.BlockSpec(p.shape, lambda i: (0, 0)) for p in params])
    out_specs = ([pl.BlockSpec((tr, c), lambda i: (i, 0)) for (c, _) in row_outs]
                 + [pl.BlockSpec((1, c), lambda i: (0, 0)) for c in acc_outs])
    out_shape = [SDS((T, c), dt) for (c, dt) in row_outs] + [SDS((1, c), F32) for c in acc_outs]
    comm = carried.take(name) if carried is not None else None
    call = _pc(body, name=name, grid=(T // tr,), in_specs=in_specs, out_specs=out_specs, out_shape=out_shape,
               sem=("arbitrary",) if n_ac else ("parallel",), comm=comm)
    outs = _run(call, (*[r[0] for r in rows], *params), name, comm, carried)
    return outs[0] if len(outs) == 1 else outs


def _rms(x, g):
    return x * lax.rsqrt(jnp.mean(x * x, axis=-1, keepdims=True) + EPS) * g


def _softplus(x):
    return jnp.maximum(x, 0.0) + jnp.log(1.0 + jnp.exp(-jnp.abs(x)))


def _silu(x):
    return x * jax.nn.sigmoid(x)


def _gelu(x):
    return 0.5 * x * (1.0 + jnp.tanh(math.sqrt(2.0 / math.pi) * (x + 0.044715 * (x * x * x))))


def _neg_expm1(x):
    series = x * (1 + x / 2 * (1 + x / 3 * (1 + x / 4 * (1 + x / 5 * (1 + x / 6 * (1 + x / 7))))))
    return -jnp.where(jnp.abs(x) < 0.3, series, jnp.exp(x) - 1.0)


def _ple(pg, pp, b):
    return jax.nn.sigmoid(pg + b) * pp


def _ssd_elt(small, xs_act, dtbias_row, alog_row):
    lane = _iota(small.shape, 1)
    dt = _softplus(small + dtbias_row)
    adt = jnp.where(lane < N_HEADS, -jnp.exp(alog_row) * dt, 0.0)
    head = _iota(xs_act.shape, 1) // HEAD_DIM
    dt_exp = jnp.zeros_like(xs_act)
    for h in range(N_HEADS):
        dth = jnp.sum(jnp.where(lane == h, dt, 0.0), axis=1, keepdims=True)
        dt_exp = dt_exp + jnp.where(head == h, dth, 0.0)
    return adt, xs_act * dt_exp


def _fox_elt(small, bf_row):
    lane = _iota(small.shape, 1)
    keep = (lane >= FOX_LANE0) & (lane < FOX_LANE0 + N_HEADS)
    return jnp.where(keep, -_softplus(-(small + bf_row)), 0.0)


def _lru_elt(xl, pre, b_ax, lam):
    r = jax.nn.sigmoid(pre[:, :LRU_W] + b_ax[:, :LRU_W])
    i = jax.nn.sigmoid(pre[:, LRU_W:] + b_ax[:, LRU_W:])
    log_a = -LRU_C * r * _softplus(-lam)
    a = jnp.exp(log_a)
    mult = jnp.sqrt(_neg_expm1(2.0 * log_a))
    return a, mult * (i * xl)


def _mix_post(yraw, xs_act, z, hl, lgate, yfox, dexp, g_ssd, g_lru, g_fox):
    y_ssd = _rms((yraw + xs_act * dexp) * _silu(z), g_ssd)
    y_lru = _rms(hl * _gelu(lgate), g_lru)
    y_fox = _rms(yfox, g_fox)
    return jnp.concatenate([y_ssd, y_lru, y_fox], axis=-1)


def _colsum(x):
    return jnp.sum(x, axis=0, keepdims=True)


def _shift_down(x, d):
    if d == 0:
        return x
    return jnp.where(_iota(x.shape, 0) >= d, pltpu.roll(x, d, 0), 0.0)


def _shift_up(x, d):
    if d == 0:
        return x
    s = x.shape[0]
    return jnp.where(_iota(x.shape, 0) < s - d, pltpu.roll(x, s - d, 0), 0.0)


def _conv_core(x, w, b):
    y = b + w[3:4, :] * x
    for k in range(3):
        y = y + w[k:k + 1, :] * _shift_down(x, 3 - k)
    return y


def seq_conv(name, src, col, width, w8, b, *, batch, silu, out_dtype):
    T = src.shape[0]
    S = T // batch
    c0 = col // LANE

    def body(x_ref, w_ref, b_ref, o_ref):
        y = _conv_core(x_ref[...], w_ref[...], b_ref[...])
        o_ref[...] = (_silu(y) if silu else y).astype(o_ref.dtype)

    return _pc(body, name=name, grid=(batch, width // LANE),
               in_specs=[pl.BlockSpec((S, LANE), lambda bi, ci: (bi, c0 + ci)),
                         pl.BlockSpec((8, LANE), lambda bi, ci: (0, ci)),
                         pl.BlockSpec((1, LANE), lambda bi, ci: (0, ci))],
               out_specs=pl.BlockSpec((S, LANE), lambda bi, ci: (bi, ci)),
               out_shape=SDS((T, width), out_dtype), sem=("parallel", "parallel"))(src, w8, b)


def seq_conv_bwd(name, src, col, width, w8, b, dy, *, batch, silu):
    T = src.shape[0]
    S = T // batch
    c0 = col // LANE

    def body(x_ref, w_ref, b_ref, dy_ref, dx_ref, dw_ref, db_ref):
        x, w = x_ref[...], w_ref[...]
        dpre = dy_ref[...].astype(F32)
        if silu:
            dpre = jax.vjp(_silu, _conv_core(x, w, b_ref[...]))[1](dpre)[0]
        dx = w[3:4, :] * dpre
        for k in range(3):
            dx = dx + w[k:k + 1, :] * _shift_up(dpre, 3 - k)
        dx_ref[...] = dx.astype(dx_ref.dtype)
        row8 = _iota((8, LANE), 0)
        dw = jnp.zeros((8, LANE), F32)
        for k in range(4):
            dw = dw + jnp.where(row8 == k, _colsum(dpre * _shift_down(x, 3 - k)), 0.0)
        db = _colsum(dpre)
        bi = pl.program_id(1)

        @pl.when(bi == 0)
        def _():
            dw_ref[...] = dw
            db_ref[...] = db

        @pl.when(bi > 0)
        def _():
            dw_ref[...] += dw
            db_ref[...] += db

    return _pc(body, name=name, grid=(width // LANE, batch),
               in_specs=[pl.BlockSpec((S, LANE), lambda ci, bi: (bi, c0 + ci)),
                         pl.BlockSpec((8, LANE), lambda ci, bi: (0, ci)),
                         pl.BlockSpec((1, LANE), lambda ci, bi: (0, ci)),
                         pl.BlockSpec((S, LANE), lambda ci, bi: (bi, ci))],
               out_specs=[pl.BlockSpec((S, LANE), lambda ci, bi: (bi, ci)),
                          pl.BlockSpec((8, LANE), lambda ci, bi: (0, ci)),
                          pl.BlockSpec((1, LANE), lambda ci, bi: (0, ci))],
               out_shape=[SDS((T, width), BF16), SDS((8, width), F32), SDS((1, width), F32)],
               sem=("parallel", "arbitrary"))(src, w8, b, dy)


def _split3_dot(tri, x):
    hi = x.astype(BF16)
    r1 = x - hi.astype(F32)
    mid = r1.astype(BF16)
    lo = (r1 - mid.astype(F32)).astype(BF16)
    d = lambda v: jnp.dot(tri, v, preferred_element_type=F32)
    return d(hi) + d(mid) + d(lo)


def seq_cumsum(name, x, *, batch, reverse=False, nsum=1, trow=None):
    T = x.shape[0]
    S = T // batch
    ch = min(256, S)
    nch = S // ch

    def body(x_ref, o_ref, *maybe_t):
        r, c = _iota((ch, ch), 0), _iota((ch, ch), 1)
        tri = jnp.where((c >= r) if reverse else (c <= r), 1.0, 0.0).astype(BF16)
        carry = jnp.zeros((1, LANE), F32)
        for k in (range(nch - 1, -1, -1) if reverse else range(nch)):
            xc = x_ref[k * ch:(k + 1) * ch, 0:LANE]
            for m in range(1, nsum):
                xc = xc + x_ref[k * ch:(k + 1) * ch, m * LANE:(m + 1) * LANE]
            o_ref[k * ch:(k + 1) * ch, :] = _split3_dot(tri, xc) + carry
            carry = carry + _colsum(xc)
        if trow is not None:
            maybe_t[0][...] = o_ref[...].T[trow:trow + 8, :]

    out_specs = [pl.BlockSpec((S, LANE), lambda bi: (bi, 0))]
    out_shape = [SDS((T, LANE), F32)]
    if trow is not None:
        out_specs.append(pl.BlockSpec((8, S), lambda bi: (bi, 0)))
        out_shape.append(SDS((batch * 8, S), F32))
    outs = _pc(body, name=name, grid=(batch,), in_specs=[pl.BlockSpec((S, LANE * nsum), lambda bi: (bi, 0))],
               out_specs=out_specs, out_shape=out_shape, sem=("parallel",))(x)
    return outs if trow is not None else outs[0]


_SCAN_SEQS = 2


def lru_scan(name, a, u, *, batch):
    T, W = a.shape
    S = T // batch
    nb = _SCAN_SEQS if batch % _SCAN_SEQS == 0 else 1

    def body(a_ref, u_ref, h_ref):
        row = _iota((8, W), 0)

        def step(g, hs):
            new = []
            for q, h in enumerate(hs):
                off = pl.multiple_of(q * S + g * 8, 8)
                at, ut = a_ref[pl.ds(off, 8), :], u_ref[pl.ds(off, 8), :]
                acc = jnp.zeros((8, W), F32)
                for r in range(8):
                    h = at[r:r + 1, :] * h + ut[r:r + 1, :]
                    acc = jnp.where(row == r, jnp.broadcast_to(h, (8, W)), acc)
                h_ref[pl.ds(off, 8), :] = acc
                new.append(h)
            return tuple(new)

        lax.fori_loop(0, S // 8, step, tuple(jnp.zeros((1, W), F32) for _ in range(nb)))

    spec = pl.BlockSpec((nb * S, W), lambda bi: (bi, 0))
    return _pc(body, name=name, grid=(batch // nb,), in_specs=[spec, spec], out_specs=spec,
               out_shape=SDS((T, W), F32), sem=("parallel",))(a, u)


def lru_scan_bwd(name, a, h, dh, *, batch):
    T, W = a.shape
    S = T // batch
    ng = S // 8
    nb = _SCAN_SEQS if batch % _SCAN_SEQS == 0 else 1

    def body(a_ref, h_ref, dh_ref, da_ref, du_ref):
        row = _iota((8, W), 0)

        def step(k, cs):
            g_idx = ng - 1 - k
            new = []
            for q, c in enumerate(cs):
                off = pl.multiple_of(q * S + g_idx * 8, 8)
                offp = pl.multiple_of(q * S + jnp.maximum(g_idx - 1, 0) * 8, 8)
                at, ht, dt = a_ref[pl.ds(off, 8), :], h_ref[pl.ds(off, 8), :], dh_ref[pl.ds(off, 8), :]
                hp = jnp.where(g_idx > 0, h_ref[pl.ds(offp, 8), :], 0.0)
                da = jnp.zeros((8, W), F32)
                du = jnp.zeros((8, W), F32)
                for r in range(7, -1, -1):
                    g = dt[r:r + 1, :] + c
                    hprev = ht[r - 1:r, :] if r > 0 else hp[7:8, :]
                    du = jnp.where(row == r, jnp.broadcast_to(g, (8, W)), du)
                    da = jnp.where(row == r, jnp.broadcast_to(g * hprev, (8, W)), da)
                    c = at[r:r + 1, :] * g
                da_ref[pl.ds(off, 8), :] = da
                du_ref[pl.ds(off, 8), :] = du
                new.append(c)
            return tuple(new)

        lax.fori_loop(0, ng, step, tuple(jnp.zeros((1, W), F32) for _ in range(nb)))

    spec = pl.BlockSpec((nb * S, W), lambda bi: (bi, 0))
    return _pc(body, name=name, grid=(batch // nb,), in_specs=[spec] * 3, out_specs=[spec] * 2,
               out_shape=[SDS((T, W), F32)] * 2, sem=("parallel",))(a, h, dh)


def _nt(a, b):
    return lax.dot_general(a, b, (((1,), (1,)), ((), ())), preferred_element_type=F32)


def _tn(a, b):
    return lax.dot_general(a, b, (((0,), (0,)), ((), ())), preferred_element_type=F32)


def _tile(S, t=256):
    return min(t, S)


def ssd_attn_fwd(name, cm, bm, xd, cum, cum_t, *, batch, carried=None):
    T = cm.shape[0]
    S = T // batch
    tq = tk = _tile(S)
    nq = S // tq

    def body(c_ref, b_ref, x_ref, cum_ref, cumt_ref, y_ref):
        i = pl.program_id(1)
        cq, cmq = cum_ref[...], c_ref[...]
        rowi, coli = _iota((tq, tk), 0), _iota((tq, tk), 1)
        half = _iota((tk, LANE), 1) // HEAD_DIM

        def step(j, accs, diag):
            off = pl.multiple_of(j * tk, tk)
            bj = b_ref[pl.ds(off, tk), :]
            gm = [_nt(cmq[:, g * LANE:(g + 1) * LANE], bj[:, g * LANE:(g + 1) * LANE]) for g in range(2)]
            ckt = cumt_ref[:, pl.ds(off, tk)]
            new = []
            for p in range(3):
                xp = x_ref[pl.ds(off, tk), p * LANE:(p + 1) * LANE]
                ws, xs = [], []
                for hh in range(2):
                    h = 2 * p + hh
                    seg = cq[:, h:h + 1] - ckt[h:h + 1, :]
                    e = jnp.exp(jnp.where(rowi >= coli, seg, -jnp.inf) if diag else seg)
                    ws.append((gm[h // 3] * e).astype(BF16))
                    xs.append(jnp.where(half == hh, xp, jnp.zeros_like(xp)))
                new.append(accs[p] + jnp.dot(jnp.concatenate(ws, axis=1), jnp.concatenate(xs, axis=0),
                                             preferred_element_type=F32))
            return tuple(new)

        accs = lax.fori_loop(0, i, functools.partial(step, diag=False),
                             tuple(jnp.zeros((tq, LANE), F32) for _ in range(3)))
        accs = step(i, accs, True)
        y_ref[...] = jnp.concatenate(accs, axis=1)

    comm = carried.take(name) if carried is not None else None
    call = _pc(body, name=name, grid=(batch, nq),
               in_specs=[pl.BlockSpec((tq, 256), lambda b, i: (b * nq + i, 0)),
                         pl.BlockSpec((S, 256), lambda b, i: (b, 0)),
                         pl.BlockSpec((S, SSD_W), lambda b, i: (b, 0)),
                         pl.BlockSpec((tq, LANE), lambda b, i: (b * nq + i, 0)),
                         pl.BlockSpec((8, S), lambda b, i: (b, 0))],
               out_specs=pl.BlockSpec((tq, SSD_W), lambda b, i: (b * nq + i, 0)),
               out_shape=SDS((T, SSD_W), F32), sem=("parallel", "parallel"), comm=comm)
    return _run(call, (cm, bm, xd, cum, cum_t), name, comm, carried)


def ssd_attn_bwd(name, cm, bm, xd, cum, cum_t, dy, *, batch, carried=None):
    T = cm.shape[0]
    S = T // batch
    tq = tk = _tile(S, 512)
    nq = S // tq

    def body(c_ref, b_ref, x_ref, cum_ref, cumt_ref, dy_ref, dx_ref, db_ref, dc_ref, dcum_ref, dcumt_ref):
        dx_ref[...] = jnp.zeros_like(dx_ref)
        db_ref[...] = jnp.zeros_like(db_ref)
        dcum_ref[...] = jnp.zeros_like(dcum_ref)
        dcumt_ref[...] = jnp.zeros_like(dcumt_ref)
        rowi, coli = _iota((tq, tk), 0), _iota((tq, tk), 1)
        halfq = _iota((tq, LANE), 1) // HEAD_DIM
        lane_q = _iota((tq, LANE), 1)

        def qblock(i, _):
            qoff = pl.multiple_of(i * tq, tq)
            cq = cum_ref[pl.ds(qoff, tq), :]
            cmq = c_ref[pl.ds(qoff, tq), :]
            dyq = dy_ref[pl.ds(qoff, tq), :]
            dyh = [[jnp.where(halfq == hh, dyq[:, p * LANE:(p + 1) * LANE], 0.0).astype(BF16) for hh in range(2)]
                   for p in range(3)]

            def step(j, carry, diag):
                dcq, rs_acc = carry
                off = pl.multiple_of(j * tk, tk)
                bj = b_ref[pl.ds(off, tk), :]
                gm = [_nt(cmq[:, g * LANE:(g + 1) * LANE], bj[:, g * LANE:(g + 1) * LANE]) for g in range(2)]
                ckt = cumt_ref[:, pl.ds(off, tk)]
                dgm = [jnp.zeros((tq, tk), F32), jnp.zeros((tq, tk), F32)]
                for p in range(3):
                    xp = x_ref[pl.ds(off, tk), p * LANE:(p + 1) * LANE]
                    ws = []
                    for hh in range(2):
                        h = 2 * p + hh
                        seg = cq[:, h:h + 1] - ckt[h:h + 1, :]
                        e = jnp.exp(jnp.where(rowi >= coli, seg, -jnp.inf) if diag else seg)
                        w = gm[h // 3] * e
                        dw = _nt(dyh[p][hh], xp)
                        zz = dw * w
                        rs_acc = rs_acc + jnp.where(lane_q == h, jnp.sum(zz, axis=1, keepdims=True), 0.0)
                        dcumt_ref[h:h + 1, pl.ds(off, tk)] += _colsum(zz)
                        dgm[h // 3] = dgm[h // 3] + dw * e
                        ws.append(w.astype(BF16))
                    dx_ref[pl.ds(off, tk), p * LANE:(p + 1) * LANE] += _tn(
                        jnp.concatenate(ws, axis=0), jnp.concatenate(dyh[p], axis=0))
                new_dcq = []
                for g in range(2):
                    dg = dgm[g].astype(BF16)
                    new_dcq.append(dcq[g] + jnp.dot(dg, bj[:, g * LANE:(g + 1) * LANE], preferred_element_type=F32))
                    db_ref[pl.ds(off, tk), g * LANE:(g + 1) * LANE] += _tn(dg, cmq[:, g * LANE:(g + 1) * LANE])
                return tuple(new_dcq), rs_acc

            carry = lax.fori_loop(
                0, i, functools.partial(step, diag=False),
                ((jnp.zeros((tq, LANE), F32), jnp.zeros((tq, LANE), F32)), jnp.zeros((tq, LANE), F32)))
            dcq, rs_acc = step(i, carry, True)
            dc_ref[pl.ds(qoff, tq), :] = jnp.concatenate(dcq, axis=1)
            dcum_ref[pl.ds(qoff, tq), :] += rs_acc
            return 0

        lax.fori_loop(0, nq, qblock, 0)
        dcum_ref[...] = dcum_ref[...] - dcumt_ref[...].T

    s256 = pl.BlockSpec((S, 256), lambda b: (b, 0))
    s384 = pl.BlockSpec((S, SSD_W), lambda b: (b, 0))
    s128 = pl.BlockSpec((S, LANE), lambda b: (b, 0))
    comm = carried.take(name) if carried is not None else None
    call = _pc(body, name=name, grid=(batch,),
               in_specs=[s256, s256, s384, s128, pl.BlockSpec((8, S), lambda b: (b, 0)), s384],
               out_specs=[s384, s256, s256, s128],
               out_shape=[SDS((T, SSD_W), F32), SDS((T, 256), F32), SDS((T, 256), F32), SDS((T, LANE), F32)],
               scratch=[pltpu.VMEM((LANE, S), F32)], sem=("parallel",), comm=comm)
    return _run(call, (cm, bm, xd, cum, cum_t, dy), name, comm, carried)


NEG_BIG = -1e30


def fox_attn_fwd(name, proj, cum, cum_t, *, batch, carried=None):
    T = proj.shape[0]
    S = T // batch
    tq = tk = _tile(S, 512)
    nq = S // tq
    scale = HEAD_DIM ** -0.5
    qb, kb, vb = OFF_Q // LANE, OFF_K // LANE, OFF_V // LANE

    def body(q_ref, k_ref, v_ref, cum_ref, cumt_ref, o_ref, lse_ref):
        p, i = pl.program_id(1), pl.program_id(2)
        cq = cum_ref[...]
        lane_q = _iota((tq, LANE), 1)
        halfq, halfk = lane_q // HEAD_DIM, _iota((tk, LANE), 1) // HEAD_DIM
        qs = q_ref[...] * scale
        qh = [jnp.where(halfq == hh, qs, 0.0).astype(BF16) for hh in range(2)]
        rowi, coli = _iota((tq, tk), 0), _iota((tq, tk), 1)
        cqh = [jnp.sum(jnp.where(lane_q == FOX_LANE0 + 2 * p + hh, cq, 0.0), axis=1, keepdims=True) for hh in range(2)]
        row8 = _iota((8, tk), 0)

        def step(j, carry, diag):
            ms, ls, acc = carry
            off = pl.multiple_of(j * tk, tk)
            kj = k_ref[pl.ds(off, tk), :].astype(BF16)
            vj = v_ref[pl.ds(off, tk), :].astype(BF16)
            ckt = cumt_ref[:, pl.ds(off, tk)]
            ps, vs, new_m, new_l, alphas = [], [], [], [], []
            for hh in range(2):
                ck = jnp.sum(jnp.where(row8 == 2 * p + hh, ckt, 0.0), axis=0, keepdims=True)
                logits = _nt(qh[hh], kj) + (cqh[hh] - ck)
                if diag:
                    logits = jnp.where(rowi >= coli, logits, -jnp.inf)
                m = jnp.maximum(ms[hh], jnp.max(logits, axis=1, keepdims=True))
                alpha = jnp.exp(ms[hh] - m)
                pr = jnp.exp(logits - m)
                new_m.append(m)
                new_l.append(alpha * ls[hh] + jnp.sum(pr, axis=1, keepdims=True))
                alphas.append(alpha)
                ps.append(pr.astype(BF16))
                vs.append(jnp.where(halfk == hh, vj, jnp.zeros_like(vj)))
            acc = acc * jnp.where(halfq == 0, alphas[0], alphas[1]) + jnp.dot(
                jnp.concatenate(ps, axis=1), jnp.concatenate(vs, axis=0), preferred_element_type=F32)
            return tuple(new_m), tuple(new_l), acc

        init = ((jnp.full((tq, 1), NEG_BIG, F32),) * 2, (jnp.zeros((tq, 1), F32),) * 2, jnp.zeros((tq, LANE), F32))
        ms, ls, acc = step(i, lax.fori_loop(0, i, functools.partial(step, diag=False), init), True)
        o_ref[...] = acc / jnp.where(halfq == 0, ls[0], ls[1])
        lse_ref[...] = (jnp.where(lane_q == 0, ms[0] + jnp.log(ls[0]), 0.0)
                        + jnp.where(lane_q == 1, ms[1] + jnp.log(ls[1]), 0.0))

    comm = carried.take(name) if carried is not None else None
    call = _pc(body, name=name, grid=(batch, 3, nq),
               in_specs=[pl.BlockSpec((tq, LANE), lambda b, p, i: (b * nq + i, qb + p)),
                         pl.BlockSpec((S, LANE), lambda b, p, i: (b, kb + p)),
                         pl.BlockSpec((S, LANE), lambda b, p, i: (b, vb + p)),
                         pl.BlockSpec((tq, LANE), lambda b, p, i: (b * nq + i, 0)),
                         pl.BlockSpec((8, S), lambda b, p, i: (b, 0))],
               out_specs=[pl.BlockSpec((tq, LANE), lambda b, p, i: (b * nq + i, p))] * 2,
               out_shape=[SDS((T, FOX_W), F32)] * 2, sem=("parallel", "parallel", "parallel"), comm=comm)
    return _run(call, (proj, proj, proj, cum, cum_t), name, comm, carried)


def fox_attn_bwd(name, proj, o, do, lse, cum, cum_t, *, batch, carried=None):
    T = proj.shape[0]
    S = T // batch
    tq = tk = _tile(S, 512)
    nq = S // tq
    scale = HEAD_DIM ** -0.5
    qb, kb, vb = OFF_Q // LANE, OFF_K // LANE, OFF_V // LANE

    def body(q_ref, k_ref, v_ref, o_ref, do_ref, lse_ref, cum_ref, cumt_ref,
             dq_ref, dk_ref, dv_ref, dcum_ref, dk_acc, dv_acc, dcumt_ref):
        p = pl.program_id(1)
        dk_acc[...] = jnp.zeros_like(dk_acc)
        dv_acc[...] = jnp.zeros_like(dv_acc)
        dcum_ref[...] = jnp.zeros_like(dcum_ref)
        dcumt_ref[...] = jnp.zeros_like(dcumt_ref)
        lane_q = _iota((tq, LANE), 1)
        halfq, halfk = lane_q // HEAD_DIM, _iota((tk, LANE), 1) // HEAD_DIM
        rowi, coli = _iota((tq, tk), 0), _iota((tq, tk), 1)
        row8 = _iota((8, tk), 0)

        def qblock(i, _):
            qoff = pl.multiple_of(i * tq, tq)
            cq = cum_ref[pl.ds(qoff, tq), :]
            qs = q_ref[pl.ds(qoff, tq), :] * scale
            doq = do_ref[pl.ds(qoff, tq), :]
            lse = lse_ref[pl.ds(qoff, tq), :]
            delta = doq * o_ref[pl.ds(qoff, tq), :]
            qh, doh, cqh, lseh, dlt = [], [], [], [], []
            for hh in range(2):
                qh.append(jnp.where(halfq == hh, qs, 0.0).astype(BF16))
                doh.append(jnp.where(halfq == hh, doq, 0.0).astype(BF16))
                cqh.append(jnp.sum(jnp.where(lane_q == FOX_LANE0 + 2 * p + hh, cq, 0.0), axis=1, keepdims=True))
                lseh.append(jnp.sum(jnp.where(lane_q == hh, lse, 0.0), axis=1, keepdims=True))
                dlt.append(jnp.sum(jnp.where(halfq == hh, delta, 0.0), axis=1, keepdims=True))

            def step(j, carry, diag):
                dq, rs = carry
                off = pl.multiple_of(j * tk, tk)
                kj = k_ref[pl.ds(off, tk), :].astype(BF16)
                vj = v_ref[pl.ds(off, tk), :].astype(BF16)
                ckt = cumt_ref[:, pl.ds(off, tk)]
                dss, prs, ks = [], [], []
                for hh in range(2):
                    ck = jnp.sum(jnp.where(row8 == 2 * p + hh, ckt, 0.0), axis=0, keepdims=True)
                    logits = _nt(qh[hh], kj) + ((cqh[hh] - lseh[hh]) - ck)
                    if diag:
                        logits = jnp.where(rowi >= coli, logits, -jnp.inf)
                    pr = jnp.exp(logits)
                    ds = pr * (_nt(doh[hh], vj) - dlt[hh])
                    rs = rs + jnp.where(lane_q == FOX_LANE0 + 2 * p + hh, jnp.sum(ds, axis=1, keepdims=True), 0.0)
                    cs = _colsum(ds)
                    dcumt_ref[0:8, pl.ds(off, tk)] += jnp.where(row8 == 2 * p + hh, cs, 0.0)
                    dss.append(ds.astype(BF16))
                    prs.append(pr.astype(BF16))
                    ks.append(jnp.where(halfk == hh, kj, jnp.zeros_like(kj)))
                dq = dq + jnp.dot(jnp.concatenate(dss, axis=1), jnp.concatenate(ks, axis=0), preferred_element_type=F32)
                dk_acc[pl.ds(off, tk), :] += _tn(jnp.concatenate(dss, axis=0), jnp.concatenate(qh, axis=0))
                dv_acc[pl.ds(off, tk), :] += _tn(jnp.concatenate(prs, axis=0), jnp.concatenate(doh, axis=0))
                return dq, rs

            carry = lax.fori_loop(0, i, functools.partial(step, diag=False),
                                  (jnp.zeros((tq, LANE), F32), jnp.zeros((tq, LANE), F32)))
            dq, rs = step(i, carry, True)
            dq_ref[pl.ds(qoff, tq), :] = (dq * scale).astype(dq_ref.dtype)
            dcum_ref[pl.ds(qoff, tq), :] += rs
            return 0

        lax.fori_loop(0, nq, qblock, 0)
        dk_ref[...] = dk_acc[...].astype(dk_ref.dtype)
        dv_ref[...] = dv_acc[...].astype(dv_ref.dtype)
        dct = dcumt_ref[...].T
        dcum_ref[...] = dcum_ref[...] - pltpu.roll(dct, FOX_LANE0, 1)

    sp = lambda c0: pl.BlockSpec((S, LANE), lambda b, p: (b, c0 + p))
    s0 = pl.BlockSpec((S, LANE), lambda b, p: (b, 0))
    comm = carried.take(name) if carried is not None else None
    call = _pc(body, name=name, grid=(batch, 3),
               in_specs=[sp(qb), sp(kb), sp(vb), sp(0), sp(0), sp(0), s0, pl.BlockSpec((8, S), lambda b, p: (b, 0))],
               out_specs=[sp(0)] * 4,
               out_shape=[SDS((T, FOX_W), BF16)] * 3 + [SDS((T, FOX_W), F32)],
               scratch=[pltpu.VMEM((S, LANE), F32), pltpu.VMEM((S, LANE), F32), pltpu.VMEM((LANE, S), F32)],
               sem=("parallel", "parallel"), comm=comm)
    return _run(call, (proj, proj, proj, o, do, lse, cum, cum_t), name, comm, carried)


def _row(v, width=None, at=0):
    v = v.astype(F32)
    width = width or v.shape[0]
    return jnp.pad(v, (at, width - at - v.shape[0]))[None, :]


def _pad8(w4):
    return jnp.pad(w4.astype(F32), ((0, 4), (0, 0)))


def _block_diag(w):
    eye = jnp.eye(w.shape[0], dtype=w.dtype)
    return (w[:, :, None, :] * eye[:, None, :, None]).reshape(LRU_W, LRU_W)


def prep_layer(f):
    cw, cb = f["ssd_conv_w"], f["ssd_conv_b"]
    return LazyDict(
        win=lambda: permute_in_cols(f["w_in"]), wout=lambda: f["w_out"],
        wg=lambda: f["w_gate_t"] if "w_gate_t" in f else f["w_gate"].T,
        wu=lambda: f["w_up_t"] if "w_up_t" in f else f["w_up"].T,
        wd=lambda: f["w_down"], wpg=lambda: f["w_ple_gate"], wpp=lambda: f["w_ple_proj"],
        wax=jnp.concatenate([_block_diag(f["lru_w_a"]), _block_diag(f["lru_w_x"])], axis=1),
        g1=_row(f["norm1_g"]), g2=_row(f["norm2_g"]), g3=_row(f["norm3_g"]),
        cw_xs=_pad8(cw[:, :384]), cb_xs=_row(cb[:384]), cw_b=_pad8(cw[:, 384:640]), cb_b=_row(cb[384:640]),
        cw_c=_pad8(cw[:, 640:]), cb_c=_row(cb[640:]), cw_l=_pad8(f["lru_conv_w"]), cb_l=_row(f["lru_conv_b"]),
        dtbias_row=_row(f["ssd_dt_bias"], LANE), alog_row=_row(f["ssd_a_log"], LANE),
        dexp=jnp.repeat(f["ssd_d"].astype(F32), HEAD_DIM)[None, :], g_ssd=_row(f["ssd_norm_g"]),
        b_ax=_row(jnp.concatenate([f["lru_b_a"], f["lru_b_x"]])), lam=_row(f["lru_lambda"]), g_lru=_row(f["lru_norm_g"]),
        bf_row=_row(f["fox_b_f"], LANE, FOX_LANE0), g_fox=_row(f["fox_norm_g"]), b_pg=_row(f["b_ple_gate"]))


def unprep_grads(g):
    blocks = lambda m: jnp.stack([m[i * 64:(i + 1) * 64, i * 64:(i + 1) * 64] for i in range(4)])
    return dict(
        norm1_g=g["g1"][0], w_in=unpermute_in_cols(g["win"]),
        ssd_conv_w=jnp.concatenate([g["cw_xs"][:4], g["cw_b"][:4], g["cw_c"][:4]], axis=1),
        ssd_conv_b=jnp.concatenate([g["cb_xs"][0], g["cb_b"][0], g["cb_c"][0]]),
        ssd_dt_bias=g["dtbias_row"][0, :N_HEADS], ssd_a_log=g["alog_row"][0, :N_HEADS],
        ssd_d=jnp.sum(g["dexp"].reshape(N_HEADS, HEAD_DIM), axis=1), ssd_norm_g=g["g_ssd"][0],
        lru_conv_w=g["cw_l"][:4], lru_conv_b=g["cb_l"][0],
        lru_w_a=blocks(g["wax"][:, :LRU_W]), lru_b_a=g["b_ax"][0, :LRU_W],
        lru_w_x=blocks(g["wax"][:, LRU_W:]), lru_b_x=g["b_ax"][0, LRU_W:],
        lru_lambda=g["lam"][0], lru_norm_g=g["g_lru"][0],
        fox_b_f=g["bf_row"][0, FOX_LANE0:FOX_LANE0 + N_HEADS], fox_norm_g=g["g_fox"][0],
        w_out=g["wout"], norm2_g=g["g2"][0], w_gate=g["wg"].T, w_up=g["wu"].T, w_down=g["wd"],
        norm3_g=g["g3"][0], w_ple_gate=g["wpg"], b_ple_gate=g["b_pg"][0], w_ple_proj=g["wpp"])


def _view(a, off, width):
    return (a, off // width, width)


def _add_epilogue(acc, e):
    return (acc + e,)


def mixer_fwd(proj, w, batch, tag, carried=None):
    sm = _view(proj, OFF_SM, LANE)
    conv = functools.partial(seq_conv, batch=batch)
    cmc = conv(f"{tag}_conv_c", proj, OFF_C, 256, w["cw_c"], w["cb_c"], silu=True, out_dtype=BF16)
    bmc = conv(f"{tag}_conv_b", proj, OFF_B, 256, w["cw_b"], w["cb_b"], silu=True, out_dtype=BF16)
    xs_act = conv(f"{tag}_conv_xs", proj, OFF_XS, SSD_W, w["cw_xs"], w["cb_xs"], silu=True, out_dtype=F32)
    xl = conv(f"{tag}_conv_l", proj, OFF_LX, LRU_W, w["cw_l"], w["cb_l"], silu=False, out_dtype=F32)
    adt, xd = rowwise(f"{tag}_ssd_elt", _ssd_elt, [sm, xs_act], [w["dtbias_row"], w["alog_row"]],
                      [(LANE, F32), (SSD_W, BF16)])
    cum_a, cum_at = seq_cumsum(f"{tag}_cum_a", adt, batch=batch, trow=0)
    yraw = ssd_attn_fwd(f"{tag}_ssd_fwd", cmc, bmc, xd, cum_a, cum_at, batch=batch, carried=carried)
    logf = rowwise(f"{tag}_fox_elt", _fox_elt, [sm], [w["bf_row"]], [(LANE, F32)])
    cum_f, cum_ft = seq_cumsum(f"{tag}_cum_f", logf, batch=batch, trow=FOX_LANE0)
    o, lse = fox_attn_fwd(f"{tag}_fox_fwd", proj, cum_f, cum_ft, batch=batch, carried=carried)
    pre = mm(xl, w["wax"], name=f"{tag}_mm_lru_gates")
    a, u = rowwise(f"{tag}_lru_elt", _lru_elt, [xl, pre], [w["b_ax"], w["lam"]], [(LRU_W, F32), (LRU_W, F32)])
    hl = lru_scan(f"{tag}_lru_scan", a, u, batch=batch)
    ycat = rowwise(f"{tag}_mix_post", _mix_post,
                   [yraw, xs_act, _view(proj, OFF_Z, SSD_W), hl, _view(proj, OFF_LG, LRU_W), o],
                   [w["dexp"], w["g_ssd"], w["g_lru"], w["g_fox"]], [(D_MODEL, BF16)], tr=256)
    saved = dict(cmc=cmc, bmc=bmc, xs_act=xs_act, xl=xl, xd=xd, cum_a=cum_a, cum_at=cum_at, yraw=yraw,
                 cum_f=cum_f, cum_ft=cum_ft, o=o, lse=lse, pre=pre, a=a, hl=hl)
    return ycat, saved


def mixer_bwd(dycat, proj, w, s, batch, tag, carried=None):
    sm = _view(proj, OFF_SM, LANE)
    g = {}

    def post_bwd(yraw, xs_act, z, hl, lg, o, dyc, dexp, g_ssd, g_lru, g_fox):
        return jax.vjp(_mix_post, yraw, xs_act, z, hl, lg, o, dexp, g_ssd, g_lru, g_fox)[1](dyc)

    (dyraw, dxs1, dz, dhl, dlg, do, g["dexp"], g["g_ssd"], g["g_lru"], g["g_fox"]) = rowwise(
        f"{tag}_mix_post_bwd", post_bwd,
        [s["yraw"], s["xs_act"], _view(proj, OFF_Z, SSD_W), s["hl"], _view(proj, OFF_LG, LRU_W), s["o"], dycat],
        [w["dexp"], w["g_ssd"], w["g_lru"], w["g_fox"]],
        [(SSD_W, F32), (SSD_W, F32), (SSD_W, BF16), (LRU_W, F32), (LRU_W, BF16), (FOX_W, F32)],
        [SSD_W, SSD_W, LRU_W, FOX_W], tr=256)

    dq, dk, dv, dcum3 = fox_attn_bwd(f"{tag}_fox_bwd", proj, s["o"], do, s["lse"], s["cum_f"], s["cum_ft"], batch=batch,
                                     carried=carried)
    dlogf = seq_cumsum(f"{tag}_rcum_f", dcum3, batch=batch, reverse=True, nsum=3)

    dxd, dbm, dcm, dcum_a = ssd_attn_bwd(f"{tag}_ssd_bwd", s["cmc"], s["bmc"], s["xd"], s["cum_a"], s["cum_at"], dyraw,
                                         batch=batch, carried=carried)
    dadt = seq_cumsum(f"{tag}_rcum_a", dcum_a, batch=batch, reverse=True)

    def ssd_elt_bwd(small, xs_act, dadt_, dxd_, dxs1_, dtbias, alog):
        dsm, dxs, ddtb, dalog = jax.vjp(_ssd_elt, small, xs_act, dtbias, alog)[1]((dadt_, dxd_))
        return dsm, dxs + dxs1_, ddtb, dalog

    dsm_s, dxs_act, g["dtbias_row"], g["alog_row"] = rowwise(
        f"{tag}_ssd_elt_bwd", ssd_elt_bwd, [sm, s["xs_act"], dadt, dxd, dxs1], [w["dtbias_row"], w["alog_row"]],
        [(LANE, F32), (SSD_W, F32)], [LANE, LANE])

    def fox_elt_bwd(small, dlogf_, dsm_s_, bf_row):
        dsm, dbf = jax.vjp(_fox_elt, small, bf_row)[1](dlogf_)
        return dsm + dsm_s_, dbf

    dsm, g["bf_row"] = rowwise(f"{tag}_fox_elt_bwd", fox_elt_bwd, [sm, dlogf, dsm_s], [w["bf_row"]],
                               [(LANE, BF16)], [LANE])

    cbwd = functools.partial(seq_conv_bwd, batch=batch)
    dxs_raw, g["cw_xs"], g["cb_xs"] = cbwd(f"{tag}_conv_xs_bwd", proj, OFF_XS, SSD_W, w["cw_xs"], w["cb_xs"], dxs_act, silu=True)
    db_raw, g["cw_b"], g["cb_b"] = cbwd(f"{tag}_conv_b_bwd", proj, OFF_B, 256, w["cw_b"], w["cb_b"], dbm, silu=True)
    dc_raw, g["cw_c"], g["cb_c"] = cbwd(f"{tag}_conv_c_bwd", proj, OFF_C, 256, w["cw_c"], w["cb_c"], dcm, silu=True)

    da, du = lru_scan_bwd(f"{tag}_lru_scan_bwd", s["a"], s["hl"], dhl, batch=batch)

    def lru_elt_bwd(xl, pre, da_, du_, b_ax, lam):
        return jax.vjp(_lru_elt, xl, pre, b_ax, lam)[1]((da_, du_))

    dxl1, dpre, g["b_ax"], g["lam"] = rowwise(
        f"{tag}_lru_elt_bwd", lru_elt_bwd, [s["xl"], s["pre"], da, du], [w["b_ax"], w["lam"]],
        [(LRU_W, F32), (2 * LRU_W, BF16)], [2 * LRU_W, LRU_W])
    g["wax"] = mm(s["xl"], dpre, ta=True, name=f"{tag}_mm_dwax")
    dxl = mm(dpre, w["wax"], tb=True, extras=[dxl1], epilogue=_add_epilogue, name=f"{tag}_mm_dxl")
    dlx_raw, g["cw_l"], g["cb_l"] = cbwd(f"{tag}_conv_l_bwd", proj, OFF_LX, LRU_W, w["cw_l"], w["cb_l"], dxl, silu=False)

    dproj = jnp.concatenate([db_raw, dc_raw, dlx_raw, dlg, dsm, dz, dxs_raw, dq, dk, dv], axis=1)
    return dproj, g


def layer_fwd(h0, p_l, w, batch, tag, carried=None):
    u1 = rowwise(f"{tag}_rms1", _rms, [h0], [w["g1"]], [(D_MODEL, BF16)], carried=carried)
    proj = mm(u1, w["win"], name=f"{tag}_mm_in", carried=carried)
    ycat, ms = mixer_fwd(proj, w, batch, tag, carried)
    add_norm = dict(epilogue=_add_rms_epilogue, out_dtypes=(F32, BF16), tm=512, tn=D_MODEL)
    h1, u2 = mm(ycat, w["wout"], extras=[h0], col_params=[w["g2"]], name=f"{tag}_mm_out", **add_norm)
    gate, up, act = mm(u2, [w["wg"], w["wu"]], tb=True, out_dtypes=(BF16, BF16, BF16), epilogue=_swiglu_epilogue,
                       tm=512, tn=D_FF // 2, name=f"{tag}_mm_gu", carried=carried)
    h2, u3 = mm(act, w["wd"], extras=[h1], col_params=[w["g3"]], tk=D_FF, name=f"{tag}_mm_down", **add_norm)
    pp = mm(p_l, w["wpp"], name=f"{tag}_mm_pp")
    h3, pg = mm(u3, w["wpg"], extras=[pp, h2], col_params=[w["b_pg"]], epilogue=_ple_epilogue,
                out_dtypes=(F32, F32), tm=512, name=f"{tag}_mm_pg")
    saved = dict(h0=h0, u1=u1, proj=proj, ycat=ycat, h1=h1, u2=u2, gate=gate, up=up, act=act, h2=h2, u3=u3, pg=pg,
                 pp=pp, mixer=ms)
    return h3, saved


def _add_rms_epilogue(acc, res, g):
    h = res + acc
    return h, _rms(h, g)


def _swiglu_epilogue(acc_g, acc_u):
    return acc_g, acc_u, _silu(acc_g) * acc_u


def _swiglu_bwd_epilogue(dact, gate, up):
    g, u = gate.astype(F32), up.astype(F32)
    s = jax.nn.sigmoid(g)
    silu = g * s
    return dact * u * (s + silu * (1.0 - s)), dact * silu


def _ple_epilogue(acc, pp, h2, b):
    return h2 + _ple(acc, pp, b), acc


def _rms_bwd_epilogue(du, h, dres, g):
    r = lax.rsqrt(jnp.mean(h * h, axis=-1, keepdims=True) + EPS)
    n = h * r
    t = du * n
    dh = r * (du * g - n * jnp.mean(t * g, axis=-1, keepdims=True))
    return dh + dres, _colsum(t)


def layer_bwd(dh3, p_l, w, s, batch, tag, carried=None, on_early_grads=None, on_w_in_grad=None):
    def ple_bwd(pg, pp, dh, b):
        return jax.vjp(_ple, pg, pp, b)[1](dh)

    norm_bwd = dict(epilogue=_rms_bwd_epilogue, partials=1, tm=512, tn=D_MODEL, tb=True)

    d_pg, d_pp, g_bpg = rowwise(f"{tag}_ple_bwd", ple_bwd, [s["pg"], s["pp"], dh3], [w["b_pg"]],
                                [(D_MODEL, BF16), (D_MODEL, BF16)], [D_MODEL])
    g = dict(b_pg=g_bpg)
    g["wpp"] = mm(p_l, d_pp, ta=True, name=f"{tag}_mm_dwpp")
    g["wpg"] = mm(s["u3"], d_pg, ta=True, name=f"{tag}_mm_dwpg", carried=carried)
    dh2, dg3 = mm(d_pg, w["wpg"], extras=[s["h2"], dh3], col_params=[w["g3"]], name=f"{tag}_mm_du3", **norm_bwd)
    g["g3"] = sum_slices(f"{tag}_sum_dg3", dg3)

    d_gate, d_up = mm(dh2, w["wd"], tb=True, extras=[s["gate"], s["up"]], epilogue=_swiglu_bwd_epilogue,
                      out_dtypes=(BF16, BF16), tm=512, tn=D_FF // 2, name=f"{tag}_mm_dact")
    g["wd"] = mm(s["act"], dh2, ta=True, name=f"{tag}_mm_dwd")
    g["wg"] = mm(d_gate, s["u2"], ta=True, name=f"{tag}_mm_dwg")
    g["wu"] = mm(d_up, s["u2"], ta=True, name=f"{tag}_mm_dwu")
    dh1, dg2 = mm([d_gate, d_up], [w["wg"], w["wu"]], extras=[s["h1"], dh2], col_params=[w["g2"]],
                  name=f"{tag}_mm_du2", **{**norm_bwd, "tb": False, "tm": 256, "tk": D_FF})
    g["g2"] = sum_slices(f"{tag}_sum_dg2", dg2)

    dycat = mm(dh1, w["wout"], tb=True, name=f"{tag}_mm_dycat")
    g["wout"] = mm(s["ycat"], dh1, ta=True, name=f"{tag}_mm_dwout")
    if on_early_grads is not None:
        on_early_grads(g)
    dproj, gm = mixer_bwd(dycat, s["proj"], w, s["mixer"], batch, tag, carried)
    g.update(gm)
    g["win"] = mm(s["u1"], dproj, ta=True, name=f"{tag}_mm_dwin")
    if on_w_in_grad is not None:
        on_w_in_grad(g["win"])
    dh0, dg1 = mm(dproj, w["win"], extras=[s["h0"], dh1], col_params=[w["g1"]], name=f"{tag}_mm_du1",
                  carried=carried, tk=PW, **norm_bwd)
    g["g1"] = sum_slices(f"{tag}_sum_dg1", dg1)
    return dh0, g


def _loss_fwd_bwd(h, tgt, gf):
    def f(h_, gf_):
        e = _rms(h_, gf_) - tgt
        return 0.5 * jnp.sum(jnp.mean(e * e, axis=-1, keepdims=True), axis=0, keepdims=True)

    loss, vj = jax.vjp(f, h, gf)
    dh, dgf = vj(jnp.ones((1, 1), F32))
    return dh, jnp.broadcast_to(loss, (1, LANE)), dgf


def local_step(x, p, tgt, layers, final_g, carried=None, on_layer_grads=None, on_early_grads=None, on_w_in_grad=None):
    batch, S, _ = x.shape
    T = batch * S
    h = x.reshape(T, D_MODEL)
    saved, weights = [], []
    for l, w in enumerate(layers):
        w = w() if callable(w) else w
        weights.append(w)
        h, s = layer_fwd(h, p[l].reshape(T, PLE_DIM), w, batch, f"l{l}", carried)
        saved.append(s)
    dh, loss, dgf = rowwise("loss", _loss_fwd_bwd, [h, tgt.reshape(T, D_MODEL)], [_row(final_g)],
                            [(D_MODEL, F32)], [LANE, D_MODEL], tr=256)
    grads = [None] * len(layers)
    for l in reversed(range(len(layers))):
        early = functools.partial(on_early_grads, l) if on_early_grads is not None else None
        w_in_hook = functools.partial(on_w_in_grad, l) if on_w_in_grad is not None else None
        dh, grads[l] = layer_bwd(dh, p[l].reshape(T, PLE_DIM), weights[l], saved[l], batch, f"l{l}", carried, early,
                                 w_in_hook)
        if on_layer_grads is not None:
            on_layer_grads(l, grads[l])
    return loss[0, 0], dh.reshape(batch, S, D_MODEL), grads, dgf[0]


MESH = pl.DeviceIdType.MESH
N_DEV = 8
N_CHIP = 4
ANY = pl.BlockSpec(memory_space=pl.ANY)


def _pos():
    return lax.axis_index("x"), lax.axis_index("y"), lax.axis_index("c")


def _comm_call(body, name, out_shape, n_in, scratch):
    return pl.pallas_call(body, name=name, out_shape=out_shape, in_specs=[ANY] * n_in, out_specs=ANY,
                          scratch_shapes=scratch)


def all_gather8(name, blk):
    def body(x_ref, out_ref, send_sems, recv_sems, local_sem):
        x, y, c = _pos()
        me, sibling = (x, y, c), (x, y, 1 - c)
        chips = [(1 - x, y), (x, 1 - y), (1 - x, 1 - y)]

        def rows(px, py, pcore):
            return out_ref.at[4 * px + 2 * py + pcore]

        def copy(k, block, to, src=None):
            return pltpu.make_async_remote_copy(
                src_ref=rows(*block) if src is None else src, dst_ref=rows(*block),
                send_sem=send_sems.at[k], recv_sem=recv_sems.at[k], device_id=to, device_id_type=MESH)

        mine = pltpu.make_async_copy(x_ref, rows(*me), local_sem)
        mine.start()
        first = [copy(0, me, sibling, src=x_ref)]
        first += [copy(1 + j, me, (*chip, c), src=x_ref) for j, chip in enumerate(chips)]
        for cp in first:
            cp.start()
        passed = [copy(4 + j, (*chip, c), sibling) for j, chip in enumerate(chips)]
        for j, chip in enumerate(chips):
            copy(1 + j, (*chip, c), me).wait_recv()
            passed[j].start()
        copy(0, sibling, me).wait_recv()
        for j, chip in enumerate(chips):
            copy(4 + j, (*chip, 1 - c), me).wait_recv()
        for cp in first + passed:
            cp.wait_send()
        mine.wait()

    return _comm_call(body, name, SDS((N_DEV,) + blk.shape, blk.dtype), 1,
                      [pltpu.SemaphoreType.DMA((7,)), pltpu.SemaphoreType.DMA((7,)), pltpu.SemaphoreType.DMA])(blk)


class Exchange:
    def __init__(self, inputs, out_shapes, sems, start, wait, aliases=None):
        self.inputs, self.out_shapes, self.sems = list(inputs), list(out_shapes), list(sems)
        self.start, self.wait, self.aliases = start, wait, dict(aliases or {})


def combine(a, b):
    ai, ao, as_ = len(a.inputs), len(a.out_shapes), len(a.sems)

    def split(cins, couts, sems):
        return (cins[:ai], couts[:ao], sems[:as_]), (cins[ai:], couts[ao:], sems[as_:])

    def start(cins, couts, sems):
        pa, pb = split(cins, couts, sems)
        a.start(*pa)
        b.start(*pb)

    def wait(cins, couts, sems):
        pa, pb = split(cins, couts, sems)
        a.wait(*pa)
        b.wait(*pb)

    aliases = dict(a.aliases)
    aliases.update({ai + i: ao + o for i, o in b.aliases.items()})
    return Exchange(a.inputs + b.inputs, a.out_shapes + b.out_shapes, a.sems + b.sems, start, wait, aliases)


def run_exchange(name, ex):
    n_ci, n_co = len(ex.inputs), len(ex.out_shapes)

    def body(*refs):
        cins, couts, csems = refs[:n_ci], refs[n_ci:n_ci + n_co], refs[n_ci + n_co:]
        ex.start(cins, couts, csems)
        ex.wait(cins, couts, csems)

    return pl.pallas_call(body, name=name, out_shape=ex.out_shapes, in_specs=[ANY] * n_ci, out_specs=[ANY] * n_co,
                          scratch_shapes=ex.sems, input_output_aliases=ex.aliases)(*ex.inputs)


def _peers():
    x, y, c = _pos()
    return x, y, c, 2 * x + y, [(1 - x, y), (x, 1 - y), (1 - x, 1 - y)]


def _remote(src, dst, send_sem, recv_sem, to):
    return pltpu.make_async_remote_copy(src_ref=src, dst_ref=dst, send_sem=send_sem, recv_sem=recv_sem,
                                        device_id=to, device_id_type=MESH)


def gather_spread(shards, layer):
    n_t = len(shards)
    halves = [s.shape[1] // 2 for s in shards]

    def copies(cins, couts, sems):
        send_sems, recv_sems, local_sems = sems
        x, y, c, my_chip, chips = _peers()
        local, sends, recvs = [], [], []
        for t in range(n_t):
            h = halves[t]
            src = cins[t].at[layer, pl.ds(c * h, h)]
            mine = couts[t].at[my_chip, pl.ds(c * h, h)]
            local.append(pltpu.make_async_copy(src, mine, local_sems.at[t]))
            sends.append(_remote(src, mine, send_sems.at[0, t], recv_sems.at[0, t], (x, y, 1 - c)))
            recvs.append(_remote(src, couts[t].at[my_chip, pl.ds((1 - c) * h, h)], send_sems.at[0, t],
                                 recv_sems.at[0, t], (x, y, 1 - c)))
            for j, (px, py) in enumerate(chips):
                sends.append(_remote(src, mine, send_sems.at[1 + j, t], recv_sems.at[1 + j, t], (px, py, c)))
                recvs.append(_remote(src, couts[t].at[2 * px + py, pl.ds(c * h, h)], send_sems.at[1 + j, t],
                                     recv_sems.at[1 + j, t], (px, py, c)))
        return local, sends, recvs

    def start(cins, couts, sems):
        local, sends, _ = copies(cins, couts, sems)
        for cp in local + sends:
            cp.start()

    def wait(cins, couts, sems):
        local, sends, recvs = copies(cins, couts, sems)
        for cp in recvs:
            cp.wait_recv()
        for cp in sends:
            cp.wait_send()
        for cp in local:
            cp.wait()

    return Exchange(shards, [SDS((N_CHIP,) + s.shape[1:], s.dtype) for s in shards],
                    [pltpu.SemaphoreType.DMA((4, n_t)), pltpu.SemaphoreType.DMA((4, n_t)),
                     pltpu.SemaphoreType.DMA((n_t,))], start, wait)


def gather_pass_on(slots):
    n_t = len(slots)
    halves = [s.shape[1] // 2 for s in slots]

    def copies(cins, couts, sems):
        send_sems, recv_sems = sems
        x, y, c, my_chip, chips = _peers()
        sends, recvs = [], []
        for t in range(n_t):
            h = halves[t]
            for j, (px, py) in enumerate(chips):
                k = 2 * px + py
                sends.append(_remote(cins[t].at[k, pl.ds(c * h, h)], couts[t].at[k, pl.ds(c * h, h)],
                                     send_sems.at[j, t], recv_sems.at[j, t], (x, y, 1 - c)))
                recvs.append(_remote(cins[t].at[k, pl.ds(c * h, h)], couts[t].at[k, pl.ds((1 - c) * h, h)],
                                     send_sems.at[j, t], recv_sems.at[j, t], (x, y, 1 - c)))
        return sends, recvs

    def start(cins, couts, sems):
        for cp in copies(cins, couts, sems)[0]:
            cp.start()

    def wait(cins, couts, sems):
        sends, recvs = copies(cins, couts, sems)
        for cp in recvs:
            cp.wait_recv()
        for cp in sends:
            cp.wait_send()

    return Exchange(slots, [SDS(s.shape, s.dtype) for s in slots],
                    [pltpu.SemaphoreType.DMA((3, n_t)), pltpu.SemaphoreType.DMA((3, n_t))], start, wait,
                    aliases={t: t for t in range(n_t)})


def chips_exchange(vs):
    n_t = len(vs)

    def copies(cins, couts, sems):
        send_sems, recv_sems, local_sems = sems
        x, y, c, my_chip, chips = _peers()
        local = [pltpu.make_async_copy(cins[t].at[my_chip], couts[t].at[my_chip], local_sems.at[t]) for t in range(n_t)]
        sends, recvs = [], []
        for k, (px, py) in enumerate(chips):
            for t in range(n_t):
                sends.append(_remote(cins[t].at[2 * px + py], couts[t].at[my_chip], send_sems.at[k, t],
                                     recv_sems.at[k, t], (px, py, c)))
                recvs.append(_remote(cins[t].at[my_chip], couts[t].at[2 * px + py], send_sems.at[k, t],
                                     recv_sems.at[k, t], (px, py, c)))
        return local, sends, recvs

    def start(cins, couts, sems):
        local, sends, _ = copies(cins, couts, sems)
        for cp in local + sends:
            cp.start()

    def wait(cins, couts, sems):
        local, sends, recvs = copies(cins, couts, sems)
        for cp in recvs:
            cp.wait_recv()
        for cp in sends:
            cp.wait_send()
        for cp in local:
            cp.wait()

    return Exchange(vs, [SDS(v.shape, v.dtype) for v in vs],
                    [pltpu.SemaphoreType.DMA((3, n_t)), pltpu.SemaphoreType.DMA((3, n_t)),
                     pltpu.SemaphoreType.DMA((n_t,))], start, wait)


def sibling_exchange(vs):
    n_t = len(vs)

    def copies(cins, couts, sems):
        x, y, c = _pos()
        return [_remote(cins[t], couts[t], sems[0].at[t], sems[1].at[t], (x, y, 1 - c)) for t in range(n_t)]

    def start(cins, couts, sems):
        for cp in copies(cins, couts, sems):
            cp.start()

    def wait(cins, couts, sems):
        for cp in copies(cins, couts, sems):
            cp.wait()

    return Exchange(vs, [SDS(v.shape, v.dtype) for v in vs],
                    [pltpu.SemaphoreType.DMA((n_t,)), pltpu.SemaphoreType.DMA((n_t,))], start, wait)


def swap_with_sibling(name, vs):
    return run_exchange(name, sibling_exchange(vs))


_ROW_BLOCKS = (1024, 704, 512, 352, 256, 128, 64, 32, 16, 8)


def sum_slices(name, v, tr=512):
    n, R, C = v.shape
    tr = _pick(R, _ROW_BLOCKS)

    def body(v_ref, o_ref):
        acc = v_ref[0].astype(F32)
        for k in range(1, n):
            acc = acc + v_ref[k].astype(F32)
        o_ref[...] = acc

    return _pc(body, name=name, grid=(R // tr,), in_specs=[pl.BlockSpec((n, tr, C), lambda i: (0, i, 0))],
               out_specs=pl.BlockSpec((tr, C), lambda i: (i, 0)), out_shape=SDS((R, C), F32), sem=("parallel",))(v)


def add_slices(name, a, b, out_dtype):
    n, R, C = a.shape
    tr = _pick(R, _ROW_BLOCKS)

    def body(a_ref, b_ref, o_ref):
        o_ref[...] = (a_ref[...].astype(F32) + b_ref[...].astype(F32)).astype(o_ref.dtype)

    spec = pl.BlockSpec((1, tr, C), lambda k, i: (k, i, 0))
    return _pc(body, name=name, grid=(n, R // tr), in_specs=[spec, spec], out_specs=spec,
               out_shape=SDS(a.shape, out_dtype), sem=("parallel", "parallel"))(a, b)


def adamw(name, w, g, m, v):
    L, R, C = w.shape
    tr = _pick(R, (512, 352, 256, 128, 64, 32, 16, 8))
    c1 = 1.0 / (1.0 - ADAM_B1 ** ADAM_STEP)
    c2 = 1.0 / (1.0 - ADAM_B2 ** ADAM_STEP)

    def body(w_ref, g_ref, m_ref, v_ref, d_ref, nm_ref, nv_ref):
        gv = g_ref[...]
        nm = ADAM_B1 * m_ref[...] + (1.0 - ADAM_B1) * gv
        nv = ADAM_B2 * v_ref[...] + (1.0 - ADAM_B2) * (gv * gv)
        d_ref[...] = -ADAM_LR * ((nm * c1) / (jnp.sqrt(nv * c2) + ADAM_EPS) + ADAM_WD * w_ref[...])
        nm_ref[...] = nm
        nv_ref[...] = nv

    if R < 8:
        tl = 64
        spec = pl.BlockSpec((tl, R, C), lambda i, _: (i, 0, 0))
        grid = (pl.cdiv(L, tl), 1)
    else:
        spec = pl.BlockSpec((1, tr, C), lambda l, i: (l, i, 0))
        grid = (L, R // tr)
    return _pc(body, name=name, grid=grid, in_specs=[spec] * 4, out_specs=[spec] * 3,
               out_shape=[SDS(w.shape, F32)] * 3, sem=("parallel", "parallel"))(w, g, m, v)


WEIGHTS = ["norm1_g", "w_in", "ssd_conv_w", "ssd_conv_b", "ssd_dt_bias", "ssd_a_log", "ssd_d", "ssd_norm_g",
           "lru_conv_w", "lru_conv_b", "lru_w_a", "lru_b_a", "lru_w_x", "lru_b_x", "lru_lambda", "lru_norm_g",
           "fox_b_f", "fox_norm_g", "w_out", "norm2_g", "w_gate", "w_up", "w_down", "norm3_g", "w_ple_gate",
           "b_ple_gate", "w_ple_proj", "final_norm_g"]
BIG = {"w_in": 2, "w_out": 1, "w_gate": 2, "w_up": 2, "w_down": 1, "w_ple_gate": 1, "w_ple_proj": 2}
SHARDED_SMALL = {"ssd_conv_w": 2, "lru_conv_w": 2}
ADAM_VIEW = {"w_in": ((2, 0, 1), (1, 2, 0)), "w_gate": ((0, 2, 1), (0, 2, 1)), "w_up": ((0, 2, 1), (0, 2, 1))}
TRANSPOSED = ("w_gate", "w_up")
SMALL = [n for n in WEIGHTS if n not in BIG]


def _pack(arrs, rows_multiple):
    flat = jnp.concatenate([a.reshape(-1) for a in arrs])
    per = rows_multiple * LANE
    n = -(-flat.shape[0] // per) * per
    return jnp.pad(flat, (0, n - flat.shape[0])).reshape(n // LANE, LANE)


def _unpack(flat2d, shapes):
    flat = flat2d.reshape(-1)
    out, off = [], 0
    for s in shapes:
        n = int(np.prod(s))
        out.append(flat[off:off + n].reshape(s))
        off += n
    return out


def _gather_shards(name, shards, axes, dtype):
    c = lax.axis_index("c")
    packed = _pack([s.astype(dtype) for s in shards], 32)
    half = packed.shape[0] // 2
    mine = lax.dynamic_slice_in_dim(packed, c * half, half, 0)
    got = all_gather8(name, mine).reshape(N_CHIP, 2 * half, LANE)
    per_chip = [_unpack(got[k], [s.shape for s in shards]) for k in range(N_CHIP)]
    return [jnp.concatenate([per_chip[k][i] for k in range(N_CHIP)], axis=ax) for i, ax in enumerate(axes)]


def kernel(x, p, norm1_g, w_in, ssd_conv_w, ssd_conv_b, ssd_dt_bias, ssd_a_log, ssd_d, ssd_norm_g, lru_conv_w, lru_conv_b, lru_w_a, lru_b_a, lru_w_x, lru_b_x, lru_lambda, lru_norm_g, fox_b_f, fox_norm_g, w_out, norm2_g, w_gate, w_up, w_down, norm3_g, w_ple_gate, b_ple_gate, w_ple_proj, final_norm_g, loss_target, m_norm1_g, m_w_in, m_ssd_conv_w, m_ssd_conv_b, m_ssd_dt_bias, m_ssd_a_log, m_ssd_d, m_ssd_norm_g, m_lru_conv_w, m_lru_conv_b, m_lru_w_a, m_lru_b_a, m_lru_w_x, m_lru_b_x, m_lru_lambda, m_lru_norm_g, m_fox_b_f, m_fox_norm_g, m_w_out, m_norm2_g, m_w_gate, m_w_up, m_w_down, m_norm3_g, m_w_ple_gate, m_b_ple_gate, m_w_ple_proj, m_final_norm_g, v_norm1_g, v_w_in, v_ssd_conv_w, v_ssd_conv_b, v_ssd_dt_bias, v_ssd_a_log, v_ssd_d, v_ssd_norm_g, v_lru_conv_w, v_lru_conv_b, v_lru_w_a, v_lru_b_a, v_lru_w_x, v_lru_b_x, v_lru_lambda, v_lru_norm_g, v_fox_b_f, v_fox_norm_g, v_w_out, v_norm2_g, v_w_gate, v_w_up, v_w_down, v_norm3_g, v_w_ple_gate, v_b_ple_gate, v_w_ple_proj, v_final_norm_g):
    args = (norm1_g, w_in, ssd_conv_w, ssd_conv_b, ssd_dt_bias, ssd_a_log, ssd_d, ssd_norm_g, lru_conv_w, lru_conv_b, lru_w_a, lru_b_a, lru_w_x, lru_b_x, lru_lambda, lru_norm_g, fox_b_f, fox_norm_g, w_out, norm2_g, w_gate, w_up, w_down, norm3_g, w_ple_gate, b_ple_gate, w_ple_proj, final_norm_g)
    m_args = (m_norm1_g, m_w_in, m_ssd_conv_w, m_ssd_conv_b, m_ssd_dt_bias, m_ssd_a_log, m_ssd_d, m_ssd_norm_g, m_lru_conv_w, m_lru_conv_b, m_lru_w_a, m_lru_b_a, m_lru_w_x, m_lru_b_x, m_lru_lambda, m_lru_norm_g, m_fox_b_f, m_fox_norm_g, m_w_out, m_norm2_g, m_w_gate, m_w_up, m_w_down, m_norm3_g, m_w_ple_gate, m_b_ple_gate, m_w_ple_proj, m_final_norm_g)
    v_args = (v_norm1_g, v_w_in, v_ssd_conv_w, v_ssd_conv_b, v_ssd_dt_bias, v_ssd_a_log, v_ssd_d, v_ssd_norm_g, v_lru_conv_w, v_lru_conv_b, v_lru_w_a, v_lru_b_a, v_lru_w_x, v_lru_b_x, v_lru_lambda, v_lru_norm_g, v_fox_b_f, v_fox_norm_g, v_w_out, v_norm2_g, v_w_gate, v_w_up, v_w_down, v_norm3_g, v_w_ple_gate, v_b_ple_gate, v_w_ple_proj, v_final_norm_g)
    w = dict(zip(WEIGHTS, args))
    mom = dict(zip(WEIGHTS, m_args))
    var = dict(zip(WEIGHTS, v_args))
    xi, yi, ci = _pos()
    chip = 2 * xi + yi

    big_names = list(BIG)
    later = [n for n in big_names if n != "w_in"]
    by_rows = {n: BIG[n] == 1 or n in TRANSPOSED for n in big_names}
    wb = {n: (jnp.transpose(w[n], (0, 2, 1)) if n in TRANSPOSED else w[n]).astype(BF16) for n in big_names}
    conv_full = dict(zip(SHARDED_SMALL, _gather_shards("gather_conv", [w[n] for n in SHARDED_SMALL],
                                                       list(SHARDED_SMALL.values()), F32)))
    carried = Carried()

    def layer_weights(l, slots_of):
        def assemble(n):
            s4 = slots_of(n)
            return (s4.reshape(-1, s4.shape[-1]) if by_rows[n]
                    else jnp.concatenate([s4[k] for k in range(N_CHIP)], axis=1))

        f = LazyDict({n: (conv_full[n][l] if n in conv_full else w[n][l]) for n in SMALL if n != "final_norm_g"})
        f.update({(n + "_t" if n in TRANSPOSED else n): functools.partial(assemble, n) for n in big_names})
        return prep_layer(f)

    carried.offer("l0_rms1", lambda: gather_spread([wb["w_in"]], 0))

    def w_in0():
        return run_exchange("gather0_in_pass_on", gather_pass_on(carried.results["l0_rms1"]))[0]

    n_early = 2
    carried.offer("l0_mm_in", lambda: gather_spread([wb[n] for n in later[:n_early]], 0))
    carried.offer("l0_ssd_fwd", lambda: gather_spread([wb[n] for n in later[n_early:]], 0))
    carried.offer("l0_fox_fwd", lambda: combine(
        gather_pass_on(carried.results["l0_mm_in"] + carried.results["l0_ssd_fwd"]),
        gather_spread([wb[n] for n in big_names], 1)))
    carried.offer("l0_mm_gu", lambda: gather_pass_on(carried.results["l0_fox_fwd"][len(later):]))
    layers = [
        layer_weights(0, lambda n: w_in0() if n == "w_in" else carried.results["l0_fox_fwd"][later.index(n)]),
        lambda: layer_weights(1, lambda n: carried.results["l0_mm_gu"][big_names.index(n)])]

    def chip_slices(a, n):
        return a.reshape(N_CHIP, -1, a.shape[1]) if by_rows[n] else jnp.stack(jnp.split(a, N_CHIP, axis=1))

    def halves(names, full_grads):
        keep, give = [], []
        for n, a in zip(names, full_grads):
            s4 = chip_slices(a, n)
            h = s4.shape[1] // 2
            keep.append(lax.dynamic_slice_in_dim(s4, ci * h, h, 1))
            give.append(lax.dynamic_slice_in_dim(s4, (1 - ci) * h, h, 1).astype(BF16))
        return keep, give

    def add_halves(tag, names, keep, got):
        return [add_slices(f"add_sibling{tag}_{n}", k_, g_, BF16) for n, k_, g_ in zip(names, keep, got)]

    kernel_key = dict(w_out="wout", w_gate="wg", w_up="wu", w_down="wd", w_ple_gate="wpg", w_ple_proj="wpp")
    n_big = len(big_names)
    gl, summed = [None] * DEPTH, [None] * DEPTH

    def on_layer_grads(l, g_layer):
        gl[l] = unprep_grads(g_layer)
        if l == 1:
            keep, give = halves(big_names, [gl[1]["w_in"]] + [g_layer[kernel_key[n]] for n in later])
            carried.offer("l0_mm_dwpg", lambda: sibling_exchange(give))
            carried.offer("l0_fox_bwd", lambda: chips_exchange(
                add_halves("1", big_names, keep, carried.results["l0_mm_dwpg"])))

    def on_early_grads(l, g_layer):
        if l == 0:
            keep, give = halves(later, [g_layer[kernel_key[n]] for n in later])
            layer1_exchange = carried.offers.pop("l0_fox_bwd")
            carried.offer("l0_fox_bwd", lambda: combine(layer1_exchange(), sibling_exchange(give)))
            def ride_with_ssd_bwd():
                summed[1] = [sum_slices(f"sum_chips1_{n}", a_)
                             for n, a_ in zip(big_names, carried.results["l0_fox_bwd"][:n_big])]
                return combine(
                    chips_exchange(add_halves("0_later", later, keep, carried.results["l0_fox_bwd"][n_big:])),
                    sibling_exchange(summed[1]))

            carried.offer("l0_ssd_bwd", ride_with_ssd_bwd)

    def on_w_in_grad(l, g_win):
        if l == 0:
            keep, give = halves(["w_in"], [unpermute_in_cols(g_win)])
            part = add_halves("0_in", ["w_in"], keep, swap_with_sibling("swap_halves0_in", give))
            carried.offer("l0_mm_du1", lambda: chips_exchange(part))

    loss, grad_x, grads, g_final = local_step(x, p, loss_target, layers, final_norm_g, carried, on_layer_grads,
                                              on_early_grads, on_w_in_grad)
    loss = lax.psum(loss, ("x", "y", "c"))
    arrived0 = dict(zip(later, carried.results["l0_ssd_bwd"][:len(later)]))
    arrived0["w_in"] = carried.results["l0_mm_du1"][0]

    gsmall = {n: jnp.stack([gl[l][n] for l in range(DEPTH)]) for n in SMALL if n != "final_norm_g"}
    gsmall["final_norm_g"] = g_final
    small_shapes = [gsmall[n].shape for n in SMALL]
    gs = _pack([gsmall[n] for n in SMALL], 8)
    gs = sum_slices("sum_small", all_gather8("gather_small_grads", gs))
    gsum = dict(zip(SMALL, _unpack(gs, small_shapes)))
    for n, ax in SHARDED_SMALL.items():
        k = gsum[n].shape[ax] // N_CHIP
        gsum[n] = lax.dynamic_slice_in_dim(gsum[n], chip * k, k, ax)

    summed[0] = [sum_slices(f"sum_chips0_{n}", arrived0[n]) for n in big_names]
    others = [swap_with_sibling("swap_results0", summed[0]), carried.results["l0_ssd_bwd"][len(later):]]
    done = [[jnp.concatenate([jnp.where(ci == 0, m_, o_), jnp.where(ci == 0, o_, m_)], axis=0)
             for m_, o_ in zip(summed[l], others[l])] for l in range(DEPTH)]
    gview = {}
    for t, n in enumerate(big_names):
        g2 = jnp.stack([done[l][t] for l in range(DEPTH)])
        if n in TRANSPOSED:
            gview[n], gsum[n] = g2, jnp.transpose(g2, (0, 2, 1))
        else:
            gsum[n] = g2

    delta, new_m, new_v = {}, {}, {}
    for n in big_names:
        if n in ADAM_VIEW:
            to_view, back = ADAM_VIEW[n]
            gv = gview[n] if n in gview else jnp.transpose(gsum[n], to_view)
            outs = adamw(f"adamw_{n}", jnp.transpose(w[n], to_view), gv, jnp.transpose(mom[n], to_view),
                         jnp.transpose(var[n], to_view))
            delta[n], new_m[n], new_v[n] = [jnp.transpose(o, back) for o in outs]
        else:
            delta[n], new_m[n], new_v[n] = adamw(f"adamw_{n}", w[n], gsum[n], mom[n], var[n])
    shapes = [w[n].shape for n in SMALL]
    pk = lambda d: _pack([d[n] for n in SMALL], 8)[None]
    ds, ms, vs = adamw("adamw_small", pk(w), pk(gsum), pk(mom), pk(var))
    for d, packed in ((delta, ds), (new_m, ms), (new_v, vs)):
        d.update(zip(SMALL, _unpack(packed[0], shapes)))

    return (loss, grad_x, *[gsum[n] for n in WEIGHTS], *[delta[n] for n in WEIGHTS],
            *[new_m[n] for n in WEIGHTS], *[new_v[n] for n in WEIGHTS])
```

```python
import functools
import math

import jax
import jax.numpy as jnp
import numpy as np
from jax import lax
from jax.experimental import pallas as pl
from jax.experimental.pallas import tpu as pltpu

F32, BF16 = jnp.float32, jnp.bfloat16
SDS = jax.ShapeDtypeStruct

D_MODEL = 1024
DEPTH = 2
HEAD_DIM = 64
N_HEADS = 6
SSD_W, LRU_W, FOX_W = 384, 256, 384
D_FF = 2816
PLE_DIM = 256
EPS = 1e-6
LRU_C = 8.0
LANE = 128
V7X_VMEM_LIMIT = 56 * 1024 * 1024

PW = 3072
OFF_B, OFF_C, OFF_LX, OFF_LG, OFF_SM, OFF_Z, OFF_XS, OFF_Q, OFF_K, OFF_V = (
    0, 256, 512, 768, 1024, 1152, 1536, 1920, 2304, 2688)
FOX_LANE0 = 8

ADAM_LR, ADAM_B1, ADAM_B2, ADAM_EPS, ADAM_WD, ADAM_STEP = 0.001, 0.9, 0.999, 1e-08, 0.01, 10


def _iota(shape, dim):
    return lax.broadcasted_iota(jnp.int32, shape, dim)


class Carried:
    def __init__(self):
        self.offers, self.results = {}, {}

    def offer(self, call_name, make_exchange):
        self.offers[call_name] = make_exchange

    def take(self, call_name):
        make = self.offers.pop(call_name, None)
        return None if make is None else make()

    def deliver(self, call_name, results):
        self.results[call_name] = results


class LazyDict(dict):
    def __getitem__(self, key):
        v = dict.__getitem__(self, key)
        if callable(v):
            v = v()
            dict.__setitem__(self, key, v)
        return v


def _run(call, args, name, comm, carried):
    if comm is None:
        return call(*args)
    own, brought = call(*args)
    carried.deliver(name, brought)
    return own


def _pc(body, *, name, grid, in_specs, out_specs, out_shape, scratch=(), sem=None, comm=None):
    if comm is None:
        return pl.pallas_call(
            body, name=name, grid=grid, in_specs=in_specs, out_specs=out_specs, out_shape=out_shape,
            scratch_shapes=list(scratch),
            compiler_params=pltpu.CompilerParams(dimension_semantics=sem, vmem_limit_bytes=V7X_VMEM_LIMIT))
    single = not isinstance(out_shape, (list, tuple))
    out_specs_l = [out_specs] if single else list(out_specs)
    out_shape_l = [out_shape] if single else list(out_shape)
    n_in, n_out, n_scr, n_ci, n_co = len(in_specs), len(out_shape_l), len(scratch), len(comm.inputs), len(comm.out_shapes)

    def hosted(*refs):
        ins, cins = refs[:n_in], refs[n_in:n_in + n_ci]
        outs, couts = refs[n_in + n_ci:n_in + n_ci + n_out], refs[n_in + n_ci + n_out:n_in + n_ci + n_out + n_co]
        rest = refs[n_in + n_ci + n_out + n_co:]
        scr, csems = rest[:n_scr], rest[n_scr:]
        ids = [pl.program_id(d) for d in range(len(grid))]
        first = functools.reduce(jnp.logical_and, [i == 0 for i in ids])
        last = functools.reduce(jnp.logical_and, [i == g - 1 for i, g in zip(ids, grid)])

        @pl.when(first)
        def _():
            comm.start(cins, couts, csems)

        body(*ins, *outs, *scr)

        @pl.when(last)
        def _():
            comm.wait(cins, couts, csems)

    call = pl.pallas_call(
        hosted, name=name, grid=grid, in_specs=list(in_specs) + [ANY] * n_ci,
        out_specs=out_specs_l + [ANY] * n_co, out_shape=out_shape_l + list(comm.out_shapes),
        scratch_shapes=list(scratch) + list(comm.sems),
        input_output_aliases={n_in + a: n_out + b for a, b in comm.aliases.items()},
        compiler_params=pltpu.CompilerParams(dimension_semantics=("arbitrary",) * len(grid),
                                             vmem_limit_bytes=V7X_VMEM_LIMIT))

    def run(*args):
        res = call(*args, *comm.inputs)
        own = res[:n_out]
        return (own[0] if single else own), list(res[n_out:])

    return run


def permute_in_cols(w):
    z = lambda n: jnp.zeros(w.shape[:-1] + (n,), w.dtype)
    s = lambda a, b: w[..., a:b]
    return jnp.concatenate([
        s(768, 1024), s(1024, 1280), s(1286, 1542), s(1542, 1798),
        s(1280, 1286), z(2), s(2950, 2956), z(LANE - 14),
        s(0, 384), s(384, 768), s(1798, 2182), s(2182, 2566), s(2566, 2950)], axis=-1)


def unpermute_in_cols(g):
    s = lambda a, n: g[..., a:a + n]
    return jnp.concatenate([
        s(OFF_Z, 384), s(OFF_XS, 384), s(OFF_B, 256), s(OFF_C, 256), s(OFF_SM, 6),
        s(OFF_LX, 256), s(OFF_LG, 256), s(OFF_Q, 384), s(OFF_K, 384), s(OFF_V, 384),
        s(OFF_SM + FOX_LANE0, 6)], axis=-1)


def _pick(n, cands):
    for c in cands:
        if n % c == 0:
            return c
    return n


def mm(a, b, *, name, ta=False, tb=False, out_dtypes=(F32,), extras=(), col_params=(), partials=0, epilogue=None,
       tm=None, tn=None, tk=None, carried=None):
    bs = list(b) if isinstance(b, (list, tuple)) else [b]
    pair_sum = isinstance(a, (list, tuple))
    a_list = list(a) if pair_sum else [a]
    assert not pair_sum or len(a_list) == len(bs)
    a = a_list[0]
    n_a = len(a_list)
    n_acc = 1 if pair_sum else len(bs)
    extras = [e if isinstance(e, tuple) else (e, 0) for e in extras]
    M = a.shape[1] if ta else a.shape[0]
    K = a.shape[0] if ta else a.shape[1]
    N = bs[0].shape[0] if tb else bs[0].shape[1]
    tn_auto = tn is None
    tm = tm or _pick(M, (1024, 1408, 512, 256, 128))
    tn = tn or _pick(N, (1024, 1408, 768, 512, 256, 128))
    if tk is None and ta and K % 2048 == 0:
        small = lambda tn_: 2048 * (tm * a.dtype.itemsize + tn_ * bs[0].dtype.itemsize) <= 12 << 20
        if small(tn):
            tk = 2048
        elif tn_auto and N % 512 == 0 and small(512):
            tn, tk = 512, 2048
    tk = tk or _pick(K, (1024, 1408, 512, 256, 128))
    nm, nn, nk = M // tm, N // tn, K // tk
    n_b, n_ex, n_cp, n_out = len(bs), len(extras), len(col_params), len(out_dtypes)
    a_bytes, b_bytes = n_a * M * K * a.dtype.itemsize, n_b * K * N * bs[0].dtype.itemsize
    rows_inner = a_bytes * nn + b_bytes <= a_bytes + b_bytes * nm

    def ij(g0, g1):
        return (g1, g0) if rows_inner else (g0, g1)

    def body(*rest):
        a_refs, rest = rest[:n_a], rest[n_a:]
        b_refs, rest = rest[:n_b], rest[n_b:]
        in_refs, rest = rest[:n_ex + n_cp], rest[n_ex + n_cp:]
        out_refs, accs = rest[:n_out + partials], rest[n_out + partials:]
        dn = (((0 if ta else 1,), (1 if tb else 0,)), ((), ()))
        dot = lambda x_ref, y_ref: lax.dot_general(x_ref[...].astype(BF16), y_ref[...].astype(BF16), dn,
                                                   preferred_element_type=F32)
        if pair_sum:
            parts = [functools.reduce(lambda u, v: u + v, [dot(x, y) for x, y in zip(a_refs, b_refs)])]
        else:
            parts = [dot(a_refs[0], b_ref) for b_ref in b_refs]

        def finish(rs):
            outs = epilogue(*rs, *[e[...] for e in in_refs]) if epilogue is not None else tuple(rs)
            for o_ref, o in zip(out_refs[:n_out], outs):
                o_ref[...] = o.astype(o_ref.dtype)
            for o_ref, o in zip(out_refs[n_out:], outs[n_out:]):
                o_ref[0] = o

        if nk == 1:
            finish(parts)
            return
        k = pl.program_id(2)

        @pl.when(k == 0)
        def _():
            for acc, part in zip(accs, parts):
                acc[...] = part

        @pl.when(k > 0)
        def _():
            for acc, part in zip(accs, parts):
                acc[...] += part

        @pl.when(k == nk - 1)
        def _():
            finish([acc[...] for acc in accs])

    def a_map(g0, g1, k):
        i, _ = ij(g0, g1)
        return (k, i) if ta else (i, k)

    def b_map(g0, g1, k):
        _, j = ij(g0, g1)
        return (j, k) if tb else (k, j)

    def ex_map(off, g0, g1, k):
        i, j = ij(g0, g1)
        return (i, j + off)

    a_spec = pl.BlockSpec((tk, tm) if ta else (tm, tk), a_map)
    b_spec = pl.BlockSpec((tn, tk) if tb else (tk, tn), b_map)
    mn_spec = pl.BlockSpec((tm, tn), functools.partial(ex_map, 0))
    comm = carried.take(name) if carried is not None else None
    call = _pc(body, name=name, grid=(nn, nm, nk) if rows_inner else (nm, nn, nk),
               in_specs=([a_spec] * n_a + [b_spec] * n_b
                         + [pl.BlockSpec((tm, tn), functools.partial(ex_map, off)) for _, off in extras]
                         + [pl.BlockSpec((1, tn), lambda g0, g1, k: (0, ij(g0, g1)[1]))] * n_cp),
               out_specs=([mn_spec] * n_out
                          + [pl.BlockSpec((1, 1, tn), lambda g0, g1, k: (ij(g0, g1)[0], 0, ij(g0, g1)[1]))] * partials),
               out_shape=[SDS((M, N), dt) for dt in out_dtypes] + [SDS((nm, 1, N), F32)] * partials,
               scratch=[pltpu.VMEM((tm, tn), F32)] * n_acc if nk > 1 else [],
               sem=("parallel", "parallel", "arbitrary"), comm=comm)
    outs = _run(call, (*a_list, *bs, *[e for e, _ in extras], *col_params), name, comm, carried)
    return outs[0] if len(outs) == 1 else outs


def rowwise(name, fn, rows, params, row_outs, acc_outs=(), tr=512, carried=None):
    rows = [r if isinstance(r, tuple) else (r, 0, r.shape[1]) for r in rows]
    T = rows[0][0].shape[0]
    tr = min(tr, T)
    n_in, n_ro, n_ac = len(rows) + len(params), len(row_outs), len(acc_outs)

    def body(*refs):
        ins, outs = refs[:n_in], refs[n_in:]
        res = fn(*[r[...] for r in ins])
        if not isinstance(res, (tuple, list)):
            res = (res,)
        for k in range(n_ro):
            outs[k][...] = res[k].astype(outs[k].dtype)
        if n_ac:
            i = pl.program_id(0)

            @pl.when(i == 0)
            def _():
                for k in range(n_ac):
                    outs[n_ro + k][...] = res[n_ro + k]

            @pl.when(i > 0)
            def _():
                for k in range(n_ac):
                    outs[n_ro + k][...] += res[n_ro + k]

    in_specs = ([pl.BlockSpec((tr, w), functools.partial(lambda cb, i: (i, cb), cb)) for (_, cb, w) in rows]
                + [pl.BlockSpec(p.shape, lambda i: (0, 0)) for p in params])
    out_specs = ([pl.BlockSpec((tr, c), lambda i: (i, 0)) for (c, _) in row_outs]
                 + [pl.BlockSpec((1, c), lambda i: (0, 0)) for c in acc_outs])
    out_shape = [SDS((T, c), dt) for (c, dt) in row_outs] + [SDS((1, c), F32) for c in acc_outs]
    comm = carried.take(name) if carried is not None else None
    call = _pc(body, name=name, grid=(T // tr,), in_specs=in_specs, out_specs=out_specs, out_shape=out_shape,
               sem=("arbitrary",) if n_ac else ("parallel",), comm=comm)
    outs = _run(call, (*[r[0] for r in rows], *params), name, comm, carried)
    return outs[0] if len(outs) == 1 else outs


def _rms(x, g):
    return x * lax.rsqrt(jnp.mean(x * x, axis=-1, keepdims=True) + EPS) * g


def _softplus(x):
    return jnp.maximum(x, 0.0) + jnp.log(1.0 + jnp.exp(-jnp.abs(x)))


def _silu(x):
    return x * jax.nn.sigmoid(x)


def _gelu(x):
    return 0.5 * x * (1.0 + jnp.tanh(math.sqrt(2.0 / math.pi) * (x + 0.044715 * (x * x * x))))


def _neg_expm1(x):
    series = x * (1 + x / 2 * (1 + x / 3 * (1 + x / 4 * (1 + x / 5 * (1 + x / 6 * (1 + x / 7))))))
    return -jnp.where(jnp.abs(x) < 0.3, series, jnp.exp(x) - 1.0)


def _ple(pg, pp, b):
    return jax.nn.sigmoid(pg + b) * pp


def _ssd_elt(small, xs_act, dtbias_row, alog_row):
    lane = _iota(small.shape, 1)
    dt = _softplus(small + dtbias_row)
    adt = jnp.where(lane < N_HEADS, -jnp.exp(alog_row) * dt, 0.0)
    head = _iota(xs_act.shape, 1) // HEAD_DIM
    dt_exp = jnp.zeros_like(xs_act)
    for h in range(N_HEADS):
        dth = jnp.sum(jnp.where(lane == h, dt, 0.0), axis=1, keepdims=True)
        dt_exp = dt_exp + jnp.where(head == h, dth, 0.0)
    return adt, xs_act * dt_exp


def _fox_elt(small, bf_row):
    lane = _iota(small.shape, 1)
    keep = (lane >= FOX_LANE0) & (lane < FOX_LANE0 + N_HEADS)
    return jnp.where(keep, -_softplus(-(small + bf_row)), 0.0)


def _lru_elt(xl, pre, b_ax, lam):
    r = jax.nn.sigmoid(pre[:, :LRU_W] + b_ax[:, :LRU_W])
    i = jax.nn.sigmoid(pre[:, LRU_W:] + b_ax[:, LRU_W:])
    log_a = -LRU_C * r * _softplus(-lam)
    a = jnp.exp(log_a)
    mult = jnp.sqrt(_neg_expm1(2.0 * log_a))
    return a, mult * (i * xl)


def _mix_post(yraw, xs_act, z, hl, lgate, yfox, dexp, g_ssd, g_lru, g_fox):
    y_ssd = _rms((yraw + xs_act * dexp) * _silu(z), g_ssd)
    y_lru = _rms(hl * _gelu(lgate), g_lru)
    y_fox = _rms(yfox, g_fox)
    return jnp.concatenate([y_ssd, y_lru, y_fox], axis=-1)


def _colsum(x):
    return jnp.sum(x, axis=0, keepdims=True)


def _shift_down(x, d):
    if d == 0:
        return x
    return jnp.where(_iota(x.shape, 0) >= d, pltpu.roll(x, d, 0), 0.0)


def _shift_up(x, d):
    if d == 0:
        return x
    s = x.shape[0]
    return jnp.where(_iota(x.shape, 0) < s - d, pltpu.roll(x, s - d, 0), 0.0)


def _conv_core(x, w, b):
    y = b + w[3:4, :] * x
    for k in range(3):
        y = y + w[k:k + 1, :] * _shift_down(x, 3 - k)
    return y


def seq_conv(name, src, col, width, w8, b, *, batch, silu, out_dtype):
    T = src.shape[0]
    S = T // batch
    c0 = col // LANE

    def body(x_ref, w_ref, b_ref, o_ref):
        y = _conv_core(x_ref[...], w_ref[...], b_ref[...])
        o_ref[...] = (_silu(y) if silu else y).astype(o_ref.dtype)

    return _pc(body, name=name, grid=(batch, width // LANE),
               in_specs=[pl.BlockSpec((S, LANE), lambda bi, ci: (bi, c0 + ci)),
                         pl.BlockSpec((8, LANE), lambda bi, ci: (0, ci)),
                         pl.BlockSpec((1, LANE), lambda bi, ci: (0, ci))],
               out_specs=pl.BlockSpec((S, LANE), lambda bi, ci: (bi, ci)),
               out_shape=SDS((T, width), out_dtype), sem=("parallel", "parallel"))(src, w8, b)


def seq_conv_bwd(name, src, col, width, w8, b, dy, *, batch, silu):
    T = src.shape[0]
    S = T // batch
    c0 = col // LANE

    def body(x_ref, w_ref, b_ref, dy_ref, dx_ref, dw_ref, db_ref):
        x, w = x_ref[...], w_ref[...]
        dpre = dy_ref[...].astype(F32)
        if silu:
            dpre = jax.vjp(_silu, _conv_core(x, w, b_ref[...]))[1](dpre)[0]
        dx = w[3:4, :] * dpre
        for k in range(3):
            dx = dx + w[k:k + 1, :] * _shift_up(dpre, 3 - k)
        dx_ref[...] = dx.astype(dx_ref.dtype)
        row8 = _iota((8, LANE), 0)
        dw = jnp.zeros((8, LANE), F32)
        for k in range(4):
            dw = dw + jnp.where(row8 == k, _colsum(dpre * _shift_down(x, 3 - k)), 0.0)
        db = _colsum(dpre)
        bi = pl.program_id(1)

        @pl.when(bi == 0)
        def _():
            dw_ref[...] = dw
            db_ref[...] = db

        @pl.when(bi > 0)
        def _():
            dw_ref[...] += dw
            db_ref[...] += db

    return _pc(body, name=name, grid=(width // LANE, batch),
               in_specs=[pl.BlockSpec((S, LANE), lambda ci, bi: (bi, c0 + ci)),
                         pl.BlockSpec((8, LANE), lambda ci, bi: (0, ci)),
                         pl.BlockSpec((1, LANE), lambda ci, bi: (0, ci)),
                         pl.BlockSpec((S, LANE), lambda ci, bi: (bi, ci))],
               out_specs=[pl.BlockSpec((S, LANE), lambda ci, bi: (bi, ci)),
                          pl.BlockSpec((8, LANE), lambda ci, bi: (0, ci)),
                          pl.BlockSpec((1, LANE), lambda ci, bi: (0, ci))],
               out_shape=[SDS((T, width), BF16), SDS((8, width), F32), SDS((1, width), F32)],
               sem=("parallel", "arbitrary"))(src, w8, b, dy)


def _split3_dot(tri, x):
    hi = x.astype(BF16)
    r1 = x - hi.astype(F32)
    mid = r1.astype(BF16)
    lo = (r1 - mid.astype(F32)).astype(BF16)
    d = lambda v: jnp.dot(tri, v, preferred_element_type=F32)
    return d(hi) + d(mid) + d(lo)


def seq_cumsum(name, x, *, batch, reverse=False, nsum=1, trow=None):
    T = x.shape[0]
    S = T // batch
    ch = min(256, S)
    nch = S // ch

    def body(x_ref, o_ref, *maybe_t):
        r, c = _iota((ch, ch), 0), _iota((ch, ch), 1)
        tri = jnp.where((c >= r) if reverse else (c <= r), 1.0, 0.0).astype(BF16)
        carry = jnp.zeros((1, LANE), F32)
        for k in (range(nch - 1, -1, -1) if reverse else range(nch)):
            xc = x_ref[k * ch:(k + 1) * ch, 0:LANE]
            for m in range(1, nsum):
                xc = xc + x_ref[k * ch:(k + 1) * ch, m * LANE:(m + 1) * LANE]
            o_ref[k * ch:(k + 1) * ch, :] = _split3_dot(tri, xc) + carry
            carry = carry + _colsum(xc)
        if trow is not None:
            maybe_t[0][...] = o_ref[...].T[trow:trow + 8, :]

    out_specs = [pl.BlockSpec((S, LANE), lambda bi: (bi, 0))]
    out_shape = [SDS((T, LANE), F32)]
    if trow is not None:
        out_specs.append(pl.BlockSpec((8, S), lambda bi: (bi, 0)))
        out_shape.append(SDS((batch * 8, S), F32))
    outs = _pc(body, name=name, grid=(batch,), in_specs=[pl.BlockSpec((S, LANE * nsum), lambda bi: (bi, 0))],
               out_specs=out_specs, out_shape=out_shape, sem=("parallel",))(x)
    return outs if trow is not None else outs[0]


_SCAN_SEQS = 2


def lru_scan(name, a, u, *, batch):
    T, W = a.shape
    S = T // batch
    nb = _SCAN_SEQS if batch % _SCAN_SEQS == 0 else 1

    def body(a_ref, u_ref, h_ref):
        row = _iota((8, W), 0)

        def step(g, hs):
            new = []
            for q, h in enumerate(hs):
                off = pl.multiple_of(q * S + g * 8, 8)
                at, ut = a_ref[pl.ds(off, 8), :], u_ref[pl.ds(off, 8), :]
                acc = jnp.zeros((8, W), F32)
                for r in range(8):
                    h = at[r:r + 1, :] * h + ut[r:r + 1, :]
                    acc = jnp.where(row == r, jnp.broadcast_to(h, (8, W)), acc)
                h_ref[pl.ds(off, 8), :] = acc
                new.append(h)
            return tuple(new)

        lax.fori_loop(0, S // 8, step, tuple(jnp.zeros((1, W), F32) for _ in range(nb)))

    spec = pl.BlockSpec((nb * S, W), lambda bi: (bi, 0))
    return _pc(body, name=name, grid=(batch // nb,), in_specs=[spec, spec], out_specs=spec,
               out_shape=SDS((T, W), F32), sem=("parallel",))(a, u)


def lru_scan_bwd(name, a, h, dh, *, batch):
    T, W = a.shape
    S = T // batch
    ng = S // 8
    nb = _SCAN_SEQS if batch % _SCAN_SEQS == 0 else 1

    def body(a_ref, h_ref, dh_ref, da_ref, du_ref):
        row = _iota((8, W), 0)

        def step(k, cs):
            g_idx = ng - 1 - k
            new = []
            for q, c in enumerate(cs):
                off = pl.multiple_of(q * S + g_idx * 8, 8)
                offp = pl.multiple_of(q * S + jnp.maximum(g_idx - 1, 0) * 8, 8)
                at, ht, dt = a_ref[pl.ds(off, 8), :], h_ref[pl.ds(off, 8), :], dh_ref[pl.ds(off, 8), :]
                hp = jnp.where(g_idx > 0, h_ref[pl.ds(offp, 8), :], 0.0)
                da = jnp.zeros((8, W), F32)
                du = jnp.zeros((8, W), F32)
                for r in range(7, -1, -1):
                    g = dt[r:r + 1, :] + c
                    hprev = ht[r - 1:r, :] if r > 0 else hp[7:8, :]
                    du = jnp.where(row == r, jnp.broadcast_to(g, (8, W)), du)
                    da = jnp.where(row == r, jnp.broadcast_to(g * hprev, (8, W)), da)
                    c = at[r:r + 1, :] * g
                da_ref[pl.ds(off, 8), :] = da
                du_ref[pl.ds(off, 8), :] = du
                new.append(c)
            return tuple(new)

        lax.fori_loop(0, ng, step, tuple(jnp.zeros((1, W), F32) for _ in range(nb)))

    spec = pl.BlockSpec((nb * S, W), lambda bi: (bi, 0))
    return _pc(body, name=name, grid=(batch // nb,), in_specs=[spec] * 3, out_specs=[spec] * 2,
               out_shape=[SDS((T, W), F32)] * 2, sem=("parallel",))(a, h, dh)


def _nt(a, b):
    return lax.dot_general(a, b, (((1,), (1,)), ((), ())), preferred_element_type=F32)


def _tn(a, b):
    return lax.dot_general(a, b, (((0,), (0,)), ((), ())), preferred_element_type=F32)


def _tile(S, t=256):
    return min(t, S)


def ssd_attn_fwd(name, cm, bm, xd, cum, cum_t, *, batch, carried=None):
    T = cm.shape[0]
    S = T // batch
    tq = tk = _tile(S)
    nq = S // tq

    def body(c_ref, b_ref, x_ref, cum_ref, cumt_ref, y_ref):
        i = pl.program_id(1)
        cq, cmq = cum_ref[...], c_ref[...]
        rowi, coli = _iota((tq, tk), 0), _iota((tq, tk), 1)
        half = _iota((tk, LANE), 1) // HEAD_DIM

        def step(j, accs, diag):
            off = pl.multiple_of(j * tk, tk)
            bj = b_ref[pl.ds(off, tk), :]
            gm = [_nt(cmq[:, g * LANE:(g + 1) * LANE], bj[:, g * LANE:(g + 1) * LANE]) for g in range(2)]
            ckt = cumt_ref[:, pl.ds(off, tk)]
            new = []
            for p in range(3):
                xp = x_ref[pl.ds(off, tk), p * LANE:(p + 1) * LANE]
                ws, xs = [], []
                for hh in range(2):
                    h = 2 * p + hh
                    seg = cq[:, h:h + 1] - ckt[h:h + 1, :]
                    e = jnp.exp(jnp.where(rowi >= coli, seg, -jnp.inf) if diag else seg)
                    ws.append((gm[h // 3] * e).astype(BF16))
                    xs.append(jnp.where(half == hh, xp, jnp.zeros_like(xp)))
                new.append(accs[p] + jnp.dot(jnp.concatenate(ws, axis=1), jnp.concatenate(xs, axis=0),
                                             preferred_element_type=F32))
            return tuple(new)

        accs = lax.fori_loop(0, i, functools.partial(step, diag=False),
                             tuple(jnp.zeros((tq, LANE), F32) for _ in range(3)))
        accs = step(i, accs, True)
        y_ref[...] = jnp.concatenate(accs, axis=1)

    comm = carried.take(name) if carried is not None else None
    call = _pc(body, name=name, grid=(batch, nq),
               in_specs=[pl.BlockSpec((tq, 256), lambda b, i: (b * nq + i, 0)),
                         pl.BlockSpec((S, 256), lambda b, i: (b, 0)),
                         pl.BlockSpec((S, SSD_W), lambda b, i: (b, 0)),
                         pl.BlockSpec((tq, LANE), lambda b, i: (b * nq + i, 0)),
                         pl.BlockSpec((8, S), lambda b, i: (b, 0))],
               out_specs=pl.BlockSpec((tq, SSD_W), lambda b, i: (b * nq + i, 0)),
               out_shape=SDS((T, SSD_W), F32), sem=("parallel", "parallel"), comm=comm)
    return _run(call, (cm, bm, xd, cum, cum_t), name, comm, carried)


def ssd_attn_bwd(name, cm, bm, xd, cum, cum_t, dy, *, batch, carried=None):
    T = cm.shape[0]
    S = T // batch
    tq = tk = _tile(S, 512)
    nq = S // tq

    def body(c_ref, b_ref, x_ref, cum_ref, cumt_ref, dy_ref, dx_ref, db_ref, dc_ref, dcum_ref, dcumt_ref):
        dx_ref[...] = jnp.zeros_like(dx_ref)
        db_ref[...] = jnp.zeros_like(db_ref)
        dcum_ref[...] = jnp.zeros_like(dcum_ref)
        dcumt_ref[...] = jnp.zeros_like(dcumt_ref)
        rowi, coli = _iota((tq, tk), 0), _iota((tq, tk), 1)
        halfq = _iota((tq, LANE), 1) // HEAD_DIM
        lane_q = _iota((tq, LANE), 1)

        def qblock(i, _):
            qoff = pl.multiple_of(i * tq, tq)
            cq = cum_ref[pl.ds(qoff, tq), :]
            cmq = c_ref[pl.ds(qoff, tq), :]
            dyq = dy_ref[pl.ds(qoff, tq), :]
            dyh = [[jnp.where(halfq == hh, dyq[:, p * LANE:(p + 1) * LANE], 0.0).astype(BF16) for hh in range(2)]
                   for p in range(3)]

            def step(j, carry, diag):
                dcq, rs_acc = carry
                off = pl.multiple_of(j * tk, tk)
                bj = b_ref[pl.ds(off, tk), :]
                gm = [_nt(cmq[:, g * LANE:(g + 1) * LANE], bj[:, g * LANE:(g + 1) * LANE]) for g in range(2)]
                ckt = cumt_ref[:, pl.ds(off, tk)]
                dgm = [jnp.zeros((tq, tk), F32), jnp.zeros((tq, tk), F32)]
                for p in range(3):
                    xp = x_ref[pl.ds(off, tk), p * LANE:(p + 1) * LANE]
                    ws = []
                    for hh in range(2):
                        h = 2 * p + hh
                        seg = cq[:, h:h + 1] - ckt[h:h + 1, :]
                        e = jnp.exp(jnp.where(rowi >= coli, seg, -jnp.inf) if diag else seg)
                        w = gm[h // 3] * e
                        dw = _nt(dyh[p][hh], xp)
                        zz = dw * w
                        rs_acc = rs_acc + jnp.where(lane_q == h, jnp.sum(zz, axis=1, keepdims=True), 0.0)
                        dcumt_ref[h:h + 1, pl.ds(off, tk)] += _colsum(zz)
                        dgm[h // 3] = dgm[h // 3] + dw * e
                        ws.append(w.astype(BF16))
                    dx_ref[pl.ds(off, tk), p * LANE:(p + 1) * LANE] += _tn(
                        jnp.concatenate(ws, axis=0), jnp.concatenate(dyh[p], axis=0))
                new_dcq = []
                for g in range(2):
                    dg = dgm[g].astype(BF16)
                    new_dcq.append(dcq[g] + jnp.dot(dg, bj[:, g * LANE:(g + 1) * LANE], preferred_element_type=F32))
                    db_ref[pl.ds(off, tk), g * LANE:(g + 1) * LANE] += _tn(dg, cmq[:, g * LANE:(g + 1) * LANE])
                return tuple(new_dcq), rs_acc

            carry = lax.fori_loop(
                0, i, functools.partial(step, diag=False),
                ((jnp.zeros((tq, LANE), F32), jnp.zeros((tq, LANE), F32)), jnp.zeros((tq, LANE), F32)))
            dcq, rs_acc = step(i, carry, True)
            dc_ref[pl.ds(qoff, tq), :] = jnp.concatenate(dcq, axis=1)
            dcum_ref[pl.ds(qoff, tq), :] += rs_acc
            return 0

        lax.fori_loop(0, nq, qblock, 0)
        dcum_ref[...] = dcum_ref[...] - dcumt_ref[...].T

    s256 = pl.BlockSpec((S, 256), lambda b: (b, 0))
    s384 = pl.BlockSpec((S, SSD_W), lambda b: (b, 0))
    s128 = pl.BlockSpec((S, LANE), lambda b: (b, 0))
    comm = carried.take(name) if carried is not None else None
    call = _pc(body, name=name, grid=(batch,),
               in_specs=[s256, s256, s384, s128, pl.BlockSpec((8, S), lambda b: (b, 0)), s384],
               out_specs=[s384, s256, s256, s128],
               out_shape=[SDS((T, SSD_W), F32), SDS((T, 256), F32), SDS((T, 256), F32), SDS((T, LANE), F32)],
               scratch=[pltpu.VMEM((LANE, S), F32)], sem=("parallel",), comm=comm)
    return _run(call, (cm, bm, xd, cum, cum_t, dy), name, comm, carried)


NEG_BIG = -1e30


def fox_attn_fwd(name, proj, cum, cum_t, *, batch, carried=None):
    T = proj.shape[0]
    S = T // batch
    tq = tk = _tile(S, 512)
    nq = S // tq
    scale = HEAD_DIM ** -0.5
    qb, kb, vb = OFF_Q // LANE, OFF_K // LANE, OFF_V // LANE

    def body(q_ref, k_ref, v_ref, cum_ref, cumt_ref, o_ref, lse_ref):
        p, i = pl.program_id(1), pl.program_id(2)
        cq = cum_ref[...]
        lane_q = _iota((tq, LANE), 1)
        halfq, halfk = lane_q // HEAD_DIM, _iota((tk, LANE), 1) // HEAD_DIM
        qs = q_ref[...] * scale
        qh = [jnp.where(halfq == hh, qs, 0.0).astype(BF16) for hh in range(2)]
        rowi, coli = _iota((tq, tk), 0), _iota((tq, tk), 1)
        cqh = [jnp.sum(jnp.where(lane_q == FOX_LANE0 + 2 * p + hh, cq, 0.0), axis=1, keepdims=True) for hh in range(2)]
        row8 = _iota((8, tk), 0)

        def step(j, carry, diag):
            ms, ls, acc = carry
            off = pl.multiple_of(j * tk, tk)
            kj = k_ref[pl.ds(off, tk), :].astype(BF16)
            vj = v_ref[pl.ds(off, tk), :].astype(BF16)
            ckt = cumt_ref[:, pl.ds(off, tk)]
            ps, vs, new_m, new_l, alphas = [], [], [], [], []
            for hh in range(2):
                ck = jnp.sum(jnp.where(row8 == 2 * p + hh, ckt, 0.0), axis=0, keepdims=True)
                logits = _nt(qh[hh], kj) + (cqh[hh] - ck)
                if diag:
                    logits = jnp.where(rowi >= coli, logits, -jnp.inf)
                m = jnp.maximum(ms[hh], jnp.max(logits, axis=1, keepdims=True))
                alpha = jnp.exp(ms[hh] - m)
                pr = jnp.exp(logits - m)
                new_m.append(m)
                new_l.append(alpha * ls[hh] + jnp.sum(pr, axis=1, keepdims=True))
                alphas.append(alpha)
                ps.append(pr.astype(BF16))
                vs.append(jnp.where(halfk == hh, vj, jnp.zeros_like(vj)))
            acc = acc * jnp.where(halfq == 0, alphas[0], alphas[1]) + jnp.dot(
                jnp.concatenate(ps, axis=1), jnp.concatenate(vs, axis=0), preferred_element_type=F32)
            return tuple(new_m), tuple(new_l), acc

        init = ((jnp.full((tq, 1), NEG_BIG, F32),) * 2, (jnp.zeros((tq, 1), F32),) * 2, jnp.zeros((tq, LANE), F32))
        ms, ls, acc = step(i, lax.fori_loop(0, i, functools.partial(step, diag=False), init), True)
        o_ref[...] = acc / jnp.where(halfq == 0, ls[0], ls[1])
        lse_ref[...] = (jnp.where(lane_q == 0, ms[0] + jnp.log(ls[0]), 0.0)
                        + jnp.where(lane_q == 1, ms[1] + jnp.log(ls[1]), 0.0))

    comm = carried.take(name) if carried is not None else None
    call = _pc(body, name=name, grid=(batch, 3, nq),
               in_specs=[pl.BlockSpec((tq, LANE), lambda b, p, i: (b * nq + i, qb + p)),
                         pl.BlockSpec((S, LANE), lambda b, p, i: (b, kb + p)),
                         pl.BlockSpec((S, LANE), lambda b, p, i: (b, vb + p)),
                         pl.BlockSpec((tq, LANE), lambda b, p, i: (b * nq + i, 0)),
                         pl.BlockSpec((8, S), lambda b, p, i: (b, 0))],
               out_specs=[pl.BlockSpec((tq, LANE), lambda b, p, i: (b * nq + i, p))] * 2,
               out_shape=[SDS((T, FOX_W), F32)] * 2, sem=("parallel", "parallel", "parallel"), comm=comm)
    return _run(call, (proj, proj, proj, cum, cum_t), name, comm, carried)


def fox_attn_bwd(name, proj, o, do, lse, cum, cum_t, *, batch, carried=None):
    T = proj.shape[0]
    S = T // batch
    tq = tk = _tile(S, 512)
    nq = S // tq
    scale = HEAD_DIM ** -0.5
    qb, kb, vb = OFF_Q // LANE, OFF_K // LANE, OFF_V // LANE

    def body(q_ref, k_ref, v_ref, o_ref, do_ref, lse_ref, cum_ref, cumt_ref,
             dq_ref, dk_ref, dv_ref, dcum_ref, dk_acc, dv_acc, dcumt_ref):
        p = pl.program_id(1)
        dk_acc[...] = jnp.zeros_like(dk_acc)
        dv_acc[...] = jnp.zeros_like(dv_acc)
        dcum_ref[...] = jnp.zeros_like(dcum_ref)
        dcumt_ref[...] = jnp.zeros_like(dcumt_ref)
        lane_q = _iota((tq, LANE), 1)
        halfq, halfk = lane_q // HEAD_DIM, _iota((tk, LANE), 1) // HEAD_DIM
        rowi, coli = _iota((tq, tk), 0), _iota((tq, tk), 1)
        row8 = _iota((8, tk), 0)

        def qblock(i, _):
            qoff = pl.multiple_of(i * tq, tq)
            cq = cum_ref[pl.ds(qoff, tq), :]
            qs = q_ref[pl.ds(qoff, tq), :] * scale
            doq = do_ref[pl.ds(qoff, tq), :]
            lse = lse_ref[pl.ds(qoff, tq), :]
            delta = doq * o_ref[pl.ds(qoff, tq), :]
            qh, doh, cqh, lseh, dlt = [], [], [], [], []
            for hh in range(2):
                qh.append(jnp.where(halfq == hh, qs, 0.0).astype(BF16))
                doh.append(jnp.where(halfq == hh, doq, 0.0).astype(BF16))
                cqh.append(jnp.sum(jnp.where(lane_q == FOX_LANE0 + 2 * p + hh, cq, 0.0), axis=1, keepdims=True))
                lseh.append(jnp.sum(jnp.where(lane_q == hh, lse, 0.0), axis=1, keepdims=True))
                dlt.append(jnp.sum(jnp.where(halfq == hh, delta, 0.0), axis=1, keepdims=True))

            def step(j, carry, diag):
                dq, rs = carry
                off = pl.multiple_of(j * tk, tk)
                kj = k_ref[pl.ds(off, tk), :].astype(BF16)
                vj = v_ref[pl.ds(off, tk), :].astype(BF16)
                ckt = cumt_ref[:, pl.ds(off, tk)]
                dss, prs, ks = [], [], []
                for hh in range(2):
                    ck = jnp.sum(jnp.where(row8 == 2 * p + hh, ckt, 0.0), axis=0, keepdims=True)
                    logits = _nt(qh[hh], kj) + ((cqh[hh] - lseh[hh]) - ck)
                    if diag:
                        logits = jnp.where(rowi >= coli, logits, -jnp.inf)
                    pr = jnp.exp(logits)
                    ds = pr * (_nt(doh[hh], vj) - dlt[hh])
                    rs = rs + jnp.where(lane_q == FOX_LANE0 + 2 * p + hh, jnp.sum(ds, axis=1, keepdims=True), 0.0)
                    cs = _colsum(ds)
                    dcumt_ref[0:8, pl.ds(off, tk)] += jnp.where(row8 == 2 * p + hh, cs, 0.0)
                    dss.append(ds.astype(BF16))
                    prs.append(pr.astype(BF16))
                    ks.append(jnp.where(halfk == hh, kj, jnp.zeros_like(kj)))
                dq = dq + jnp.dot(jnp.concatenate(dss, axis=1), jnp.concatenate(ks, axis=0), preferred_element_type=F32)
                dk_acc[pl.ds(off, tk), :] += _tn(jnp.concatenate(dss, axis=0), jnp.concatenate(qh, axis=0))
                dv_acc[pl.ds(off, tk), :] += _tn(jnp.concatenate(prs, axis=0), jnp.concatenate(doh, axis=0))
                return dq, rs

            carry = lax.fori_loop(0, i, functools.partial(step, diag=False),
                                  (jnp.zeros((tq, LANE), F32), jnp.zeros((tq, LANE), F32)))
            dq, rs = step(i, carry, True)
            dq_ref[pl.ds(qoff, tq), :] = (dq * scale).astype(dq_ref.dtype)
            dcum_ref[pl.ds(qoff, tq), :] += rs
            return 0

        lax.fori_loop(0, nq, qblock, 0)
        dk_ref[...] = dk_acc[...].astype(dk_ref.dtype)
        dv_ref[...] = dv_acc[...].astype(dv_ref.dtype)
        dct = dcumt_ref[...].T
        dcum_ref[...] = dcum_ref[...] - pltpu.roll(dct, FOX_LANE0, 1)

    sp = lambda c0: pl.BlockSpec((S, LANE), lambda b, p: (b, c0 + p))
    s0 = pl.BlockSpec((S, LANE), lambda b, p: (b, 0))
    comm = carried.take(name) if carried is not None else None
    call = _pc(body, name=name, grid=(batch, 3),
               in_specs=[sp(qb), sp(kb), sp(vb), sp(0), sp(0), sp(0), s0, pl.BlockSpec((8, S), lambda b, p: (b, 0))],
               out_specs=[sp(0)] * 4,
               out_shape=[SDS((T, FOX_W), BF16)] * 3 + [SDS((T, FOX_W), F32)],
               scratch=[pltpu.VMEM((S, LANE), F32), pltpu.VMEM((S, LANE), F32), pltpu.VMEM((LANE, S), F32)],
               sem=("parallel", "parallel"), comm=comm)
    return _run(call, (proj, proj, proj, o, do, lse, cum, cum_t), name, comm, carried)


def _row(v, width=None, at=0):
    v = v.astype(F32)
    width = width or v.shape[0]
    return jnp.pad(v, (at, width - at - v.shape[0]))[None, :]


def _pad8(w4):
    return jnp.pad(w4.astype(F32), ((0, 4), (0, 0)))


def _block_diag(w):
    eye = jnp.eye(w.shape[0], dtype=w.dtype)
    return (w[:, :, None, :] * eye[:, None, :, None]).reshape(LRU_W, LRU_W)


def prep_layer(f):
    cw, cb = f["ssd_conv_w"], f["ssd_conv_b"]
    return LazyDict(
        win=lambda: permute_in_cols(f["w_in"]), wout=lambda: f["w_out"],
        wg=lambda: f["w_gate_t"] if "w_gate_t" in f else f["w_gate"].T,
        wu=lambda: f["w_up_t"] if "w_up_t" in f else f["w_up"].T,
        wd=lambda: f["w_down"], wpg=lambda: f["w_ple_gate"], wpp=lambda: f["w_ple_proj"],
        wax=jnp.concatenate([_block_diag(f["lru_w_a"]), _block_diag(f["lru_w_x"])], axis=1),
        g1=_row(f["norm1_g"]), g2=_row(f["norm2_g"]), g3=_row(f["norm3_g"]),
        cw_xs=_pad8(cw[:, :384]), cb_xs=_row(cb[:384]), cw_b=_pad8(cw[:, 384:640]), cb_b=_row(cb[384:640]),
        cw_c=_pad8(cw[:, 640:]), cb_c=_row(cb[640:]), cw_l=_pad8(f["lru_conv_w"]), cb_l=_row(f["lru_conv_b"]),
        dtbias_row=_row(f["ssd_dt_bias"], LANE), alog_row=_row(f["ssd_a_log"], LANE),
        dexp=jnp.repeat(f["ssd_d"].astype(F32), HEAD_DIM)[None, :], g_ssd=_row(f["ssd_norm_g"]),
        b_ax=_row(jnp.concatenate([f["lru_b_a"], f["lru_b_x"]])), lam=_row(f["lru_lambda"]), g_lru=_row(f["lru_norm_g"]),
        bf_row=_row(f["fox_b_f"], LANE, FOX_LANE0), g_fox=_row(f["fox_norm_g"]), b_pg=_row(f["b_ple_gate"]))


def unprep_grads(g):
    blocks = lambda m: jnp.stack([m[i * 64:(i + 1) * 64, i * 64:(i + 1) * 64] for i in range(4)])
    return dict(
        norm1_g=g["g1"][0], w_in=unpermute_in_cols(g["win"]),
        ssd_conv_w=jnp.concatenate([g["cw_xs"][:4], g["cw_b"][:4], g["cw_c"][:4]], axis=1),
        ssd_conv_b=jnp.concatenate([g["cb_xs"][0], g["cb_b"][0], g["cb_c"][0]]),
        ssd_dt_bias=g["dtbias_row"][0, :N_HEADS], ssd_a_log=g["alog_row"][0, :N_HEADS],
        ssd_d=jnp.sum(g["dexp"].reshape(N_HEADS, HEAD_DIM), axis=1), ssd_norm_g=g["g_ssd"][0],
        lru_conv_w=g["cw_l"][:4], lru_conv_b=g["cb_l"][0],
        lru_w_a=blocks(g["wax"][:, :LRU_W]), lru_b_a=g["b_ax"][0, :LRU_W],
        lru_w_x=blocks(g["wax"][:, LRU_W:]), lru_b_x=g["b_ax"][0, LRU_W:],
        lru_lambda=g["lam"][0], lru_norm_g=g["g_lru"][0],
        fox_b_f=g["bf_row"][0, FOX_LANE0:FOX_LANE0 + N_HEADS], fox_norm_g=g["g_fox"][0],
        w_out=g["wout"], norm2_g=g["g2"][0], w_gate=g["wg"].T, w_up=g["wu"].T, w_down=g["wd"],
        norm3_g=g["g3"][0], w_ple_gate=g["wpg"], b_ple_gate=g["b_pg"][0], w_ple_proj=g["wpp"])


def _view(a, off, width):
    return (a, off // width, width)


def _add_epilogue(acc, e):
    return (acc + e,)


def mixer_fwd(proj, w, batch, tag, carried=None):
    sm = _view(proj, OFF_SM, LANE)
    conv = functools.partial(seq_conv, batch=batch)
    cmc = conv(f"{tag}_conv_c", proj, OFF_C, 256, w["cw_c"], w["cb_c"], silu=True, out_dtype=BF16)
    bmc = conv(f"{tag}_conv_b", proj, OFF_B, 256, w["cw_b"], w["cb_b"], silu=True, out_dtype=BF16)
    xs_act = conv(f"{tag}_conv_xs", proj, OFF_XS, SSD_W, w["cw_xs"], w["cb_xs"], silu=True, out_dtype=F32)
    xl = conv(f"{tag}_conv_l", proj, OFF_LX, LRU_W, w["cw_l"], w["cb_l"], silu=False, out_dtype=F32)
    adt, xd = rowwise(f"{tag}_ssd_elt", _ssd_elt, [sm, xs_act], [w["dtbias_row"], w["alog_row"]],
                      [(LANE, F32), (SSD_W, BF16)])
    cum_a, cum_at = seq_cumsum(f"{tag}_cum_a", adt, batch=batch, trow=0)
    yraw = ssd_attn_fwd(f"{tag}_ssd_fwd", cmc, bmc, xd, cum_a, cum_at, batch=batch, carried=carried)
    logf = rowwise(f"{tag}_fox_elt", _fox_elt, [sm], [w["bf_row"]], [(LANE, F32)])
    cum_f, cum_ft = seq_cumsum(f"{tag}_cum_f", logf, batch=batch, trow=FOX_LANE0)
    o, lse = fox_attn_fwd(f"{tag}_fox_fwd", proj, cum_f, cum_ft, batch=batch, carried=carried)
    pre = mm(xl, w["wax"], name=f"{tag}_mm_lru_gates")
    a, u = rowwise(f"{tag}_lru_elt", _lru_elt, [xl, pre], [w["b_ax"], w["lam"]], [(LRU_W, F32), (LRU_W, F32)])
    hl = lru_scan(f"{tag}_lru_scan", a, u, batch=batch)
    ycat = rowwise(f"{tag}_mix_post", _mix_post,
                   [yraw, xs_act, _view(proj, OFF_Z, SSD_W), hl, _view(proj, OFF_LG, LRU_W), o],
                   [w["dexp"], w["g_ssd"], w["g_lru"], w["g_fox"]], [(D_MODEL, BF16)], tr=256)
    saved = dict(cmc=cmc, bmc=bmc, xs_act=xs_act, xl=xl, xd=xd, cum_a=cum_a, cum_at=cum_at, yraw=yraw,
                 cum_f=cum_f, cum_ft=cum_ft, o=o, lse=lse, pre=pre, a=a, hl=hl)
    return ycat, saved


def mixer_bwd(dycat, proj, w, s, batch, tag, carried=None):
    sm = _view(proj, OFF_SM, LANE)
    g = {}

    def post_bwd(yraw, xs_act, z, hl, lg, o, dyc, dexp, g_ssd, g_lru, g_fox):
        return jax.vjp(_mix_post, yraw, xs_act, z, hl, lg, o, dexp, g_ssd, g_lru, g_fox)[1](dyc)

    (dyraw, dxs1, dz, dhl, dlg, do, g["dexp"], g["g_ssd"], g["g_lru"], g["g_fox"]) = rowwise(
        f"{tag}_mix_post_bwd", post_bwd,
        [s["yraw"], s["xs_act"], _view(proj, OFF_Z, SSD_W), s["hl"], _view(proj, OFF_LG, LRU_W), s["o"], dycat],
        [w["dexp"], w["g_ssd"], w["g_lru"], w["g_fox"]],
        [(SSD_W, F32), (SSD_W, F32), (SSD_W, BF16), (LRU_W, F32), (LRU_W, BF16), (FOX_W, F32)],
        [SSD_W, SSD_W, LRU_W, FOX_W], tr=256)

    dq, dk, dv, dcum3 = fox_attn_bwd(f"{tag}_fox_bwd", proj, s["o"], do, s["lse"], s["cum_f"], s["cum_ft"], batch=batch,
                                     carried=carried)
    dlogf = seq_cumsum(f"{tag}_rcum_f", dcum3, batch=batch, reverse=True, nsum=3)

    dxd, dbm, dcm, dcum_a = ssd_attn_bwd(f"{tag}_ssd_bwd", s["cmc"], s["bmc"], s["xd"], s["cum_a"], s["cum_at"], dyraw,
                                         batch=batch, carried=carried)
    dadt = seq_cumsum(f"{tag}_rcum_a", dcum_a, batch=batch, reverse=True)

    def ssd_elt_bwd(small, xs_act, dadt_, dxd_, dxs1_, dtbias, alog):
        dsm, dxs, ddtb, dalog = jax.vjp(_ssd_elt, small, xs_act, dtbias, alog)[1]((dadt_, dxd_))
        return dsm, dxs + dxs1_, ddtb, dalog

    dsm_s, dxs_act, g["dtbias_row"], g["alog_row"] = rowwise(
        f"{tag}_ssd_elt_bwd", ssd_elt_bwd, [sm, s["xs_act"], dadt, dxd, dxs1], [w["dtbias_row"], w["alog_row"]],
        [(LANE, F32), (SSD_W, F32)], [LANE, LANE])

    def fox_elt_bwd(small, dlogf_, dsm_s_, bf_row):
        dsm, dbf = jax.vjp(_fox_elt, small, bf_row)[1](dlogf_)
        return dsm + dsm_s_, dbf

    dsm, g["bf_row"] = rowwise(f"{tag}_fox_elt_bwd", fox_elt_bwd, [sm, dlogf, dsm_s], [w["bf_row"]],
                               [(LANE, BF16)], [LANE])

    cbwd = functools.partial(seq_conv_bwd, batch=batch)
    dxs_raw, g["cw_xs"], g["cb_xs"] = cbwd(f"{tag}_conv_xs_bwd", proj, OFF_XS, SSD_W, w["cw_xs"], w["cb_xs"], dxs_act, silu=True)
    db_raw, g["cw_b"], g["cb_b"] = cbwd(f"{tag}_conv_b_bwd", proj, OFF_B, 256, w["cw_b"], w["cb_b"], dbm, silu=True)
    dc_raw, g["cw_c"], g["cb_c"] = cbwd(f"{tag}_conv_c_bwd", proj, OFF_C, 256, w["cw_c"], w["cb_c"], dcm, silu=True)

    da, du = lru_scan_bwd(f"{tag}_lru_scan_bwd", s["a"], s["hl"], dhl, batch=batch)

    def lru_elt_bwd(xl, pre, da_, du_, b_ax, lam):
        return jax.vjp(_lru_elt, xl, pre, b_ax, lam)[1]((da_, du_))

    dxl1, dpre, g["b_ax"], g["lam"] = rowwise(
        f"{tag}_lru_elt_bwd", lru_elt_bwd, [s["xl"], s["pre"], da, du], [w["b_ax"], w["lam"]],
        [(LRU_W, F32), (2 * LRU_W, BF16)], [2 * LRU_W, LRU_W])
    g["wax"] = mm(s["xl"], dpre, ta=True, name=f"{tag}_mm_dwax")
    dxl = mm(dpre, w["wax"], tb=True, extras=[dxl1], epilogue=_add_epilogue, name=f"{tag}_mm_dxl")
    dlx_raw, g["cw_l"], g["cb_l"] = cbwd(f"{tag}_conv_l_bwd", proj, OFF_LX, LRU_W, w["cw_l"], w["cb_l"], dxl, silu=False)

    dproj = jnp.concatenate([db_raw, dc_raw, dlx_raw, dlg, dsm, dz, dxs_raw, dq, dk, dv], axis=1)
    return dproj, g


def layer_fwd(h0, p_l, w, batch, tag, carried=None):
    u1 = rowwise(f"{tag}_rms1", _rms, [h0], [w["g1"]], [(D_MODEL, BF16)], carried=carried)
    proj = mm(u1, w["win"], name=f"{tag}_mm_in", carried=carried)
    ycat, ms = mixer_fwd(proj, w, batch, tag, carried)
    add_norm = dict(epilogue=_add_rms_epilogue, out_dtypes=(F32, BF16), tm=512, tn=D_MODEL)
    h1, u2 = mm(ycat, w["wout"], extras=[h0], col_params=[w["g2"]], name=f"{tag}_mm_out", **add_norm)
    gate, up, act = mm(u2, [w["wg"], w["wu"]], tb=True, out_dtypes=(BF16, BF16, BF16), epilogue=_swiglu_epilogue,
                       tm=512, tn=D_FF // 2, name=f"{tag}_mm_gu", carried=carried)
    h2, u3 = mm(act, w["wd"], extras=[h1], col_params=[w["g3"]], tk=D_FF, name=f"{tag}_mm_down", **add_norm)
    pp = mm(p_l, w["wpp"], name=f"{tag}_mm_pp")
    h3, pg = mm(u3, w["wpg"], extras=[pp, h2], col_params=[w["b_pg"]], epilogue=_ple_epilogue,
                out_dtypes=(F32, F32), tm=512, name=f"{tag}_mm_pg")
    saved = dict(h0=h0, u1=u1, proj=proj, ycat=ycat, h1=h1, u2=u2, gate=gate, up=up, act=act, h2=h2, u3=u3, pg=pg,
                 pp=pp, mixer=ms)
    return h3, saved


def _add_rms_epilogue(acc, res, g):
    h = res + acc
    return h, _rms(h, g)


def _swiglu_epilogue(acc_g, acc_u):
    return acc_g, acc_u, _silu(acc_g) * acc_u


def _swiglu_bwd_epilogue(dact, gate, up):
    g, u = gate.astype(F32), up.astype(F32)
    s = jax.nn.sigmoid(g)
    silu = g * s
    return dact * u * (s + silu * (1.0 - s)), dact * silu


def _ple_epilogue(acc, pp, h2, b):
    return h2 + _ple(acc, pp, b), acc


def _rms_bwd_epilogue(du, h, dres, g):
    r = lax.rsqrt(jnp.mean(h * h, axis=-1, keepdims=True) + EPS)
    n = h * r
    t = du * n
    dh = r * (du * g - n * jnp.mean(t * g, axis=-1, keepdims=True))
    return dh + dres, _colsum(t)


def layer_bwd(dh3, p_l, w, s, batch, tag, carried=None, on_early_grads=None, on_w_in_grad=None):
    def ple_bwd(pg, pp, dh, b):
        return jax.vjp(_ple, pg, pp, b)[1](dh)

    norm_bwd = dict(epilogue=_rms_bwd_epilogue, partials=1, tm=512, tn=D_MODEL, tb=True)

    d_pg, d_pp, g_bpg = rowwise(f"{tag}_ple_bwd", ple_bwd, [s["pg"], s["pp"], dh3], [w["b_pg"]],
                                [(D_MODEL, BF16), (D_MODEL, BF16)], [D_MODEL])
    g = dict(b_pg=g_bpg)
    g["wpp"] = mm(p_l, d_pp, ta=True, name=f"{tag}_mm_dwpp")
    g["wpg"] = mm(s["u3"], d_pg, ta=True, name=f"{tag}_mm_dwpg", carried=carried)
    dh2, dg3 = mm(d_pg, w["wpg"], extras=[s["h2"], dh3], col_params=[w["g3"]], name=f"{tag}_mm_du3", **norm_bwd)
    g["g3"] = sum_slices(f"{tag}_sum_dg3", dg3)

    d_gate, d_up = mm(dh2, w["wd"], tb=True, extras=[s["gate"], s["up"]], epilogue=_swiglu_bwd_epilogue,
                      out_dtypes=(BF16, BF16), tm=512, tn=D_FF // 2, name=f"{tag}_mm_dact")
    g["wd"] = mm(s["act"], dh2, ta=True, name=f"{tag}_mm_dwd")
    g["wg"] = mm(d_gate, s["u2"], ta=True, name=f"{tag}_mm_dwg")
    g["wu"] = mm(d_up, s["u2"], ta=True, name=f"{tag}_mm_dwu")
    dh1, dg2 = mm([d_gate, d_up], [w["wg"], w["wu"]], extras=[s["h1"], dh2], col_params=[w["g2"]],
                  name=f"{tag}_mm_du2", **{**norm_bwd, "tb": False, "tm": 256, "tk": D_FF})
    g["g2"] = sum_slices(f"{tag}_sum_dg2", dg2)

    dycat = mm(dh1, w["wout"], tb=True, name=f"{tag}_mm_dycat")
    g["wout"] = mm(s["ycat"], dh1, ta=True, name=f"{tag}_mm_dwout")
    if on_early_grads is not None:
        on_early_grads(g)
    dproj, gm = mixer_bwd(dycat, s["proj"], w, s["mixer"], batch, tag, carried)
    g.update(gm)
    g["win"] = mm(s["u1"], dproj, ta=True, name=f"{tag}_mm_dwin")
    if on_w_in_grad is not None:
        on_w_in_grad(g["win"])
    dh0, dg1 = mm(dproj, w["win"], extras=[s["h0"], dh1], col_params=[w["g1"]], name=f"{tag}_mm_du1",
                  carried=carried, tk=PW, **norm_bwd)
    g["g1"] = sum_slices(f"{tag}_sum_dg1", dg1)
    return dh0, g


def _loss_fwd_bwd(h, tgt, gf):
    def f(h_, gf_):
        e = _rms(h_, gf_) - tgt
        return 0.5 * jnp.sum(jnp.mean(e * e, axis=-1, keepdims=True), axis=0, keepdims=True)

    loss, vj = jax.vjp(f, h, gf)
    dh, dgf = vj(jnp.ones((1, 1), F32))
    return dh, jnp.broadcast_to(loss, (1, LANE)), dgf


def local_step(x, p, tgt, layers, final_g, carried=None, on_layer_grads=None, on_early_grads=None, on_w_in_grad=None):
    batch, S, _ = x.shape
    T = batch * S
    h = x.reshape(T, D_MODEL)
    saved, weights = [], []
    for l, w in enumerate(layers):
        w = w() if callable(w) else w
        weights.append(w)
        h, s = layer_fwd(h, p[l].reshape(T, PLE_DIM), w, batch, f"l{l}", carried)
        saved.append(s)
    dh, loss, dgf = rowwise("loss", _loss_fwd_bwd, [h, tgt.reshape(T, D_MODEL)], [_row(final_g)],
                            [(D_MODEL, F32)], [LANE, D_MODEL], tr=256)
    grads = [None] * len(layers)
    for l in reversed(range(len(layers))):
        early = functools.partial(on_early_grads, l) if on_early_grads is not None else None
        w_in_hook = functools.partial(on_w_in_grad, l) if on_w_in_grad is not None else None
        dh, grads[l] = layer_bwd(dh, p[l].reshape(T, PLE_DIM), weights[l], saved[l], batch, f"l{l}", carried, early,
                                 w_in_hook)
        if on_layer_grads is not None:
            on_layer_grads(l, grads[l])
    return loss[0, 0], dh.reshape(batch, S, D_MODEL), grads, dgf[0]


MESH = pl.DeviceIdType.MESH
N_DEV = 8
N_CHIP = 4
ANY = pl.BlockSpec(memory_space=pl.ANY)


def _pos():
    return lax.axis_index("x"), lax.axis_index("y"), lax.axis_index("c")


def _comm_call(body, name, out_shape, n_in, scratch):
    return pl.pallas_call(body, name=name, out_shape=out_shape, in_specs=[ANY] * n_in, out_specs=ANY,
                          scratch_shapes=scratch)


def all_gather8(name, blk):
    def body(x_ref, out_ref, send_sems, recv_sems, local_sem):
        x, y, c = _pos()
        me, sibling = (x, y, c), (x, y, 1 - c)
        chips = [(1 - x, y), (x, 1 - y), (1 - x, 1 - y)]

        def rows(px, py, pcore):
            return out_ref.at[4 * px + 2 * py + pcore]

        def copy(k, block, to, src=None):
            return pltpu.make_async_remote_copy(
                src_ref=rows(*block) if src is None else src, dst_ref=rows(*block),
                send_sem=send_sems.at[k], recv_sem=recv_sems.at[k], device_id=to, device_id_type=MESH)

        mine = pltpu.make_async_copy(x_ref, rows(*me), local_sem)
        mine.start()
        first = [copy(0, me, sibling, src=x_ref)]
        first += [copy(1 + j, me, (*chip, c), src=x_ref) for j, chip in enumerate(chips)]
        for cp in first:
            cp.start()
        passed = [copy(4 + j, (*chip, c), sibling) for j, chip in enumerate(chips)]
        for j, chip in enumerate(chips):
            copy(1 + j, (*chip, c), me).wait_recv()
            passed[j].start()
        copy(0, sibling, me).wait_recv()
        for j, chip in enumerate(chips):
            copy(4 + j, (*chip, 1 - c), me).wait_recv()
        for cp in first + passed:
            cp.wait_send()
        mine.wait()

    return _comm_call(body, name, SDS((N_DEV,) + blk.shape, blk.dtype), 1,
                      [pltpu.SemaphoreType.DMA((7,)), pltpu.SemaphoreType.DMA((7,)), pltpu.SemaphoreType.DMA])(blk)


class Exchange:
    def __init__(self, inputs, out_shapes, sems, start, wait, aliases=None):
        self.inputs, self.out_shapes, self.sems = list(inputs), list(out_shapes), list(sems)
        self.start, self.wait, self.aliases = start, wait, dict(aliases or {})


def combine(a, b):
    ai, ao, as_ = len(a.inputs), len(a.out_shapes), len(a.sems)

    def split(cins, couts, sems):
        return (cins[:ai], couts[:ao], sems[:as_]), (cins[ai:], couts[ao:], sems[as_:])

    def start(cins, couts, sems):
        pa, pb = split(cins, couts, sems)
        a.start(*pa)
        b.start(*pb)

    def wait(cins, couts, sems):
        pa, pb = split(cins, couts, sems)
        a.wait(*pa)
        b.wait(*pb)

    aliases = dict(a.aliases)
    aliases.update({ai + i: ao + o for i, o in b.aliases.items()})
    return Exchange(a.inputs + b.inputs, a.out_shapes + b.out_shapes, a.sems + b.sems, start, wait, aliases)


def run_exchange(name, ex):
    n_ci, n_co = len(ex.inputs), len(ex.out_shapes)

    def body(*refs):
        cins, couts, csems = refs[:n_ci], refs[n_ci:n_ci + n_co], refs[n_ci + n_co:]
        ex.start(cins, couts, csems)
        ex.wait(cins, couts, csems)

    return pl.pallas_call(body, name=name, out_shape=ex.out_shapes, in_specs=[ANY] * n_ci, out_specs=[ANY] * n_co,
                          scratch_shapes=ex.sems, input_output_aliases=ex.aliases)(*ex.inputs)


def _peers():
    x, y, c = _pos()
    return x, y, c, 2 * x + y, [(1 - x, y), (x, 1 - y), (1 - x, 1 - y)]


def _remote(src, dst, send_sem, recv_sem, to):
    return pltpu.make_async_remote_copy(src_ref=src, dst_ref=dst, send_sem=send_sem, recv_sem=recv_sem,
                                        device_id=to, device_id_type=MESH)


def gather_spread(shards, layer):
    n_t = len(shards)
    halves = [s.shape[1] // 2 for s in shards]

    def copies(cins, couts, sems):
        send_sems, recv_sems, local_sems = sems
        x, y, c, my_chip, chips = _peers()
        local, sends, recvs = [], [], []
        for t in range(n_t):
            h = halves[t]
            src = cins[t].at[layer, pl.ds(c * h, h)]
            mine = couts[t].at[my_chip, pl.ds(c * h, h)]
            local.append(pltpu.make_async_copy(src, mine, local_sems.at[t]))
            sends.append(_remote(src, mine, send_sems.at[0, t], recv_sems.at[0, t], (x, y, 1 - c)))
            recvs.append(_remote(src, couts[t].at[my_chip, pl.ds((1 - c) * h, h)], send_sems.at[0, t],
                                 recv_sems.at[0, t], (x, y, 1 - c)))
            for j, (px, py) in enumerate(chips):
                sends.append(_remote(src, mine, send_sems.at[1 + j, t], recv_sems.at[1 + j, t], (px, py, c)))
                recvs.append(_remote(src, couts[t].at[2 * px + py, pl.ds(c * h, h)], send_sems.at[1 + j, t],
                                     recv_sems.at[1 + j, t], (px, py, c)))
        return local, sends, recvs

    def start(cins, couts, sems):
        local, sends, _ = copies(cins, couts, sems)
        for cp in local + sends:
            cp.start()

    def wait(cins, couts, sems):
        local, sends, recvs = copies(cins, couts, sems)
        for cp in recvs:
            cp.wait_recv()
        for cp in sends:
            cp.wait_send()
        for cp in local:
            cp.wait()

    return Exchange(shards, [SDS((N_CHIP,) + s.shape[1:], s.dtype) for s in shards],
                    [pltpu.SemaphoreType.DMA((4, n_t)), pltpu.SemaphoreType.DMA((4, n_t)),
                     pltpu.SemaphoreType.DMA((n_t,))], start, wait)


def gather_pass_on(slots):
    n_t = len(slots)
    halves = [s.shape[1] // 2 for s in slots]

    def copies(cins, couts, sems):
        send_sems, recv_sems = sems
        x, y, c, my_chip, chips = _peers()
        sends, recvs = [], []
        for t in range(n_t):
            h = halves[t]
            for j, (px, py) in enumerate(chips):
                k = 2 * px + py
                sends.append(_remote(cins[t].at[k, pl.ds(c * h, h)], couts[t].at[k, pl.ds(c * h, h)],
                                     send_sems.at[j, t], recv_sems.at[j, t], (x, y, 1 - c)))
                recvs.append(_remote(cins[t].at[k, pl.ds(c * h, h)], couts[t].at[k, pl.ds((1 - c) * h, h)],
                                     send_sems.at[j, t], recv_sems.at[j, t], (x, y, 1 - c)))
        return sends, recvs

    def start(cins, couts, sems):
        for cp in copies(cins, couts, sems)[0]:
            cp.start()

    def wait(cins, couts, sems):
        sends, recvs = copies(cins, couts, sems)
        for cp in recvs:
            cp.wait_recv()
        for cp in sends:
            cp.wait_send()

    return Exchange(slots, [SDS(s.shape, s.dtype) for s in slots],
                    [pltpu.SemaphoreType.DMA((3, n_t)), pltpu.SemaphoreType.DMA((3, n_t))], start, wait,
                    aliases={t: t for t in range(n_t)})


def chips_exchange(vs):
    n_t = len(vs)

    def copies(cins, couts, sems):
        send_sems, recv_sems, local_sems = sems
        x, y, c, my_chip, chips = _peers()
        local = [pltpu.make_async_copy(cins[t].at[my_chip], couts[t].at[my_chip], local_sems.at[t]) for t in range(n_t)]
        sends, recvs = [], []
        for k, (px, py) in enumerate(chips):
            for t in range(n_t):
                sends.append(_remote(cins[t].at[2 * px + py], couts[t].at[my_chip], send_sems.at[k, t],
                                     recv_sems.at[k, t], (px, py, c)))
                recvs.append(_remote(cins[t].at[my_chip], couts[t].at[2 * px + py], send_sems.at[k, t],
                                     recv_sems.at[k, t], (px, py, c)))
        return local, sends, recvs

    def start(cins, couts, sems):
        local, sends, _ = copies(cins, couts, sems)
        for cp in local + sends:
            cp.start()

    def wait(cins, couts, sems):
        local, sends, recvs = copies(cins, couts, sems)
        for cp in recvs:
            cp.wait_recv()
        for cp in sends:
            cp.wait_send()
        for cp in local:
            cp.wait()

    return Exchange(vs, [SDS(v.shape, v.dtype) for v in vs],
                    [pltpu.SemaphoreType.DMA((3, n_t)), pltpu.SemaphoreType.DMA((3, n_t)),
                     pltpu.SemaphoreType.DMA((n_t,))], start, wait)


def sibling_exchange(vs):
    n_t = len(vs)

    def copies(cins, couts, sems):
        x, y, c = _pos()
        return [_remote(cins[t], couts[t], sems[0].at[t], sems[1].at[t], (x, y, 1 - c)) for t in range(n_t)]

    def start(cins, couts, sems):
        for cp in copies(cins, couts, sems):
            cp.start()

    def wait(cins, couts, sems):
        for cp in copies(cins, couts, sems):
            cp.wait()

    return Exchange(vs, [SDS(v.shape, v.dtype) for v in vs],
                    [pltpu.SemaphoreType.DMA((n_t,)), pltpu.SemaphoreType.DMA((n_t,))], start, wait)


def swap_with_sibling(name, vs):
    return run_exchange(name, sibling_exchange(vs))


_ROW_BLOCKS = (1024, 704, 512, 352, 256, 128, 64, 32, 16, 8)


def sum_slices(name, v, tr=512):
    n, R, C = v.shape
    tr = _pick(R, _ROW_BLOCKS)

    def body(v_ref, o_ref):
        acc = v_ref[0].astype(F32)
        for k in range(1, n):
            acc = acc + v_ref[k].astype(F32)
        o_ref[...] = acc

    return _pc(body, name=name, grid=(R // tr,), in_specs=[pl.BlockSpec((n, tr, C), lambda i: (0, i, 0))],
               out_specs=pl.BlockSpec((tr, C), lambda i: (i, 0)), out_shape=SDS((R, C), F32), sem=("parallel",))(v)


def add_slices(name, a, b, out_dtype):
    n, R, C = a.shape
    tr = _pick(R, _ROW_BLOCKS)

    def body(a_ref, b_ref, o_ref):
        o_ref[...] = (a_ref[...].astype(F32) + b_ref[...].astype(F32)).astype(o_ref.dtype)

    spec = pl.BlockSpec((1, tr, C), lambda k, i: (k, i, 0))
    return _pc(body, name=name, grid=(n, R // tr), in_specs=[spec, spec], out_specs=spec,
               out_shape=SDS(a.shape, out_dtype), sem=("parallel", "parallel"))(a, b)


def adamw(name, w, g, m, v):
    L, R, C = w.shape
    tr = _pick(R, (512, 352, 256, 128, 64, 32, 16, 8))
    c1 = 1.0 / (1.0 - ADAM_B1 ** ADAM_STEP)
    c2 = 1.0 / (1.0 - ADAM_B2 ** ADAM_STEP)

    def body(w_ref, g_ref, m_ref, v_ref, d_ref, nm_ref, nv_ref):
        gv = g_ref[...]
        nm = ADAM_B1 * m_ref[...] + (1.0 - ADAM_B1) * gv
        nv = ADAM_B2 * v_ref[...] + (1.0 - ADAM_B2) * (gv * gv)
        d_ref[...] = -ADAM_LR * ((nm * c1) / (jnp.sqrt(nv * c2) + ADAM_EPS) + ADAM_WD * w_ref[...])
        nm_ref[...] = nm
        nv_ref[...] = nv

    if R < 8:
        tl = 64
        spec = pl.BlockSpec((tl, R, C), lambda i, _: (i, 0, 0))
        grid = (pl.cdiv(L, tl), 1)
    else:
        spec = pl.BlockSpec((1, tr, C), lambda l, i: (l, i, 0))
        grid = (L, R // tr)
    return _pc(body, name=name, grid=grid, in_specs=[spec] * 4, out_specs=[spec] * 3,
               out_shape=[SDS(w.shape, F32)] * 3, sem=("parallel", "parallel"))(w, g, m, v)


WEIGHTS = ["norm1_g", "w_in", "ssd_conv_w", "ssd_conv_b", "ssd_dt_bias", "ssd_a_log", "ssd_d", "ssd_norm_g",
           "lru_conv_w", "lru_conv_b", "lru_w_a", "lru_b_a", "lru_w_x", "lru_b_x", "lru_lambda", "lru_norm_g",
           "fox_b_f", "fox_norm_g", "w_out", "norm2_g", "w_gate", "w_up", "w_down", "norm3_g", "w_ple_gate",
           "b_ple_gate", "w_ple_proj", "final_norm_g"]
BIG = {"w_in": 2, "w_out": 1, "w_gate": 2, "w_up": 2, "w_down": 1, "w_ple_gate": 1, "w_ple_proj": 2}
SHARDED_SMALL = {"ssd_conv_w": 2, "lru_conv_w": 2}
ADAM_VIEW = {"w_in": ((2, 0, 1), (1, 2, 0)), "w_gate": ((0, 2, 1), (0, 2, 1)), "w_up": ((0, 2, 1), (0, 2, 1))}
TRANSPOSED = ("w_gate", "w_up")
SMALL = [n for n in WEIGHTS if n not in BIG]


def _pack(arrs, rows_multiple):
    flat = jnp.concatenate([a.reshape(-1) for a in arrs])
    per = rows_multiple * LANE
    n = -(-flat.shape[0] // per) * per
    return jnp.pad(flat, (0, n - flat.shape[0])).reshape(n // LANE, LANE)


def _unpack(flat2d, shapes):
    flat = flat2d.reshape(-1)
    out, off = [], 0
    for s in shapes:
        n = int(np.prod(s))
        out.append(flat[off:off + n].reshape(s))
        off += n
    return out


def _gather_shards(name, shards, axes, dtype):
    c = lax.axis_index("c")
    packed = _pack([s.astype(dtype) for s in shards], 32)
    half = packed.shape[0] // 2
    mine = lax.dynamic_slice_in_dim(packed, c * half, half, 0)
    got = all_gather8(name, mine).reshape(N_CHIP, 2 * half, LANE)
    per_chip = [_unpack(got[k], [s.shape for s in shards]) for k in range(N_CHIP)]
    return [jnp.concatenate([per_chip[k][i] for k in range(N_CHIP)], axis=ax) for i, ax in enumerate(axes)]


def kernel(x, p, norm1_g, w_in, ssd_conv_w, ssd_conv_b, ssd_dt_bias, ssd_a_log, ssd_d, ssd_norm_g, lru_conv_w, lru_conv_b, lru_w_a, lru_b_a, lru_w_x, lru_b_x, lru_lambda, lru_norm_g, fox_b_f, fox_norm_g, w_out, norm2_g, w_gate, w_up, w_down, norm3_g, w_ple_gate, b_ple_gate, w_ple_proj, final_norm_g, loss_target, m_norm1_g, m_w_in, m_ssd_conv_w, m_ssd_conv_b, m_ssd_dt_bias, m_ssd_a_log, m_ssd_d, m_ssd_norm_g, m_lru_conv_w, m_lru_conv_b, m_lru_w_a, m_lru_b_a, m_lru_w_x, m_lru_b_x, m_lru_lambda, m_lru_norm_g, m_fox_b_f, m_fox_norm_g, m_w_out, m_norm2_g, m_w_gate, m_w_up, m_w_down, m_norm3_g, m_w_ple_gate, m_b_ple_gate, m_w_ple_proj, m_final_norm_g, v_norm1_g, v_w_in, v_ssd_conv_w, v_ssd_conv_b, v_ssd_dt_bias, v_ssd_a_log, v_ssd_d, v_ssd_norm_g, v_lru_conv_w, v_lru_conv_b, v_lru_w_a, v_lru_b_a, v_lru_w_x, v_lru_b_x, v_lru_lambda, v_lru_norm_g, v_fox_b_f, v_fox_norm_g, v_w_out, v_norm2_g, v_w_gate, v_w_up, v_w_down, v_norm3_g, v_w_ple_gate, v_b_ple_gate, v_w_ple_proj, v_final_norm_g):
    args = (norm1_g, w_in, ssd_conv_w, ssd_conv_b, ssd_dt_bias, ssd_a_log, ssd_d, ssd_norm_g, lru_conv_w, lru_conv_b, lru_w_a, lru_b_a, lru_w_x, lru_b_x, lru_lambda, lru_norm_g, fox_b_f, fox_norm_g, w_out, norm2_g, w_gate, w_up, w_down, norm3_g, w_ple_gate, b_ple_gate, w_ple_proj, final_norm_g)
    m_args = (m_norm1_g, m_w_in, m_ssd_conv_w, m_ssd_conv_b, m_ssd_dt_bias, m_ssd_a_log, m_ssd_d, m_ssd_norm_g, m_lru_conv_w, m_lru_conv_b, m_lru_w_a, m_lru_b_a, m_lru_w_x, m_lru_b_x, m_lru_lambda, m_lru_norm_g, m_fox_b_f, m_fox_norm_g, m_w_out, m_norm2_g, m_w_gate, m_w_up, m_w_down, m_norm3_g, m_w_ple_gate, m_b_ple_gate, m_w_ple_proj, m_final_norm_g)
    v_args = (v_norm1_g, v_w_in, v_ssd_conv_w, v_ssd_conv_b, v_ssd_dt_bias, v_ssd_a_log, v_ssd_d, v_ssd_norm_g, v_lru_conv_w, v_lru_conv_b, v_lru_w_a, v_lru_b_a, v_lru_w_x, v_lru_b_x, v_lru_lambda, v_lru_norm_g, v_fox_b_f, v_fox_norm_g, v_w_out, v_norm2_g, v_w_gate, v_w_up, v_w_down, v_norm3_g, v_w_ple_gate, v_b_ple_gate, v_w_ple_proj, v_final_norm_g)
    w = dict(zip(WEIGHTS, args))
    mom = dict(zip(WEIGHTS, m_args))
    var = dict(zip(WEIGHTS, v_args))
    xi, yi, ci = _pos()
    chip = 2 * xi + yi

    big_names = list(BIG)
    later = [n for n in big_names if n != "w_in"]
    by_rows = {n: BIG[n] == 1 or n in TRANSPOSED for n in big_names}
    wb = {n: (jnp.transpose(w[n], (0, 2, 1)) if n in TRANSPOSED else w[n]).astype(BF16) for n in big_names}
    conv_full = dict(zip(SHARDED_SMALL, _gather_shards("gather_conv", [w[n] for n in SHARDED_SMALL],
                                                       list(SHARDED_SMALL.values()), F32)))
    carried = Carried()

    def layer_weights(l, slots_of):
        def assemble(n):
            s4 = slots_of(n)
            return (s4.reshape(-1, s4.shape[-1]) if by_rows[n]
                    else jnp.concatenate([s4[k] for k in range(N_CHIP)], axis=1))

        f = LazyDict({n: (conv_full[n][l] if n in conv_full else w[n][l]) for n in SMALL if n != "final_norm_g"})
        f.update({(n + "_t" if n in TRANSPOSED else n): functools.partial(assemble, n) for n in big_names})
        return prep_layer(f)

    carried.offer("l0_rms1", lambda: gather_spread([wb["w_in"]], 0))

    def w_in0():
        return run_exchange("gather0_in_pass_on", gather_pass_on(carried.results["l0_rms1"]))[0]

    n_early = 2
    carried.offer("l0_mm_in", lambda: gather_spread([wb[n] for n in later[:n_early]], 0))
    carried.offer("l0_ssd_fwd", lambda: gather_spread([wb[n] for n in later[n_early:]], 0))
    carried.offer("l0_fox_fwd", lambda: combine(
        gather_pass_on(carried.results["l0_mm_in"] + carried.results["l0_ssd_fwd"]),
        gather_spread([wb[n] for n in big_names], 1)))
    carried.offer("l0_mm_gu", lambda: gather_pass_on(carried.results["l0_fox_fwd"][len(later):]))
    layers = [
        layer_weights(0, lambda n: w_in0() if n == "w_in" else carried.results["l0_fox_fwd"][later.index(n)]),
        lambda: layer_weights(1, lambda n: carried.results["l0_mm_gu"][big_names.index(n)])]

    def chip_slices(a, n):
        return a.reshape(N_CHIP, -1, a.shape[1]) if by_rows[n] else jnp.stack(jnp.split(a, N_CHIP, axis=1))

    def halves(names, full_grads):
        keep, give = [], []
        for n, a in zip(names, full_grads):
            s4 = chip_slices(a, n)
            h = s4.shape[1] // 2
            keep.append(lax.dynamic_slice_in_dim(s4, ci * h, h, 1))
            give.append(lax.dynamic_slice_in_dim(s4, (1 - ci) * h, h, 1).astype(BF16))
        return keep, give

    def add_halves(tag, names, keep, got):
        return [add_slices(f"add_sibling{tag}_{n}", k_, g_, BF16) for n, k_, g_ in zip(names, keep, got)]

    kernel_key = dict(w_out="wout", w_gate="wg", w_up="wu", w_down="wd", w_ple_gate="wpg", w_ple_proj="wpp")
    n_big = len(big_names)
    gl, summed = [None] * DEPTH, [None] * DEPTH

    def on_layer_grads(l, g_layer):
        gl[l] = unprep_grads(g_layer)
        if l == 1:
            keep, give = halves(big_names, [gl[1]["w_in"]] + [g_layer[kernel_key[n]] for n in later])
            carried.offer("l0_mm_dwpg", lambda: sibling_exchange(give))
            carried.offer("l0_fox_bwd", lambda: chips_exchange(
                add_halves("1", big_names, keep, carried.results["l0_mm_dwpg"])))

    def on_early_grads(l, g_layer):
        if l == 0:
            keep, give = halves(later, [g_layer[kernel_key[n]] for n in later])
            layer1_exchange = carried.offers.pop("l0_fox_bwd")
            carried.offer("l0_fox_bwd", lambda: combine(layer1_exchange(), sibling_exchange(give)))
            def ride_with_ssd_bwd():
                summed[1] = [sum_slices(f"sum_chips1_{n}", a_)
                             for n, a_ in zip(big_names, carried.results["l0_fox_bwd"][:n_big])]
                return combine(
                    chips_exchange(add_halves("0_later", later, keep, carried.results["l0_fox_bwd"][n_big:])),
                    sibling_exchange(summed[1]))

            carried.offer("l0_ssd_bwd", ride_with_ssd_bwd)

    def on_w_in_grad(l, g_win):
        if l == 0:
            keep, give = halves(["w_in"], [unpermute_in_cols(g_win)])
            part = add_halves("0_in", ["w_in"], keep, swap_with_sibling("swap_halves0_in", give))
            carried.offer("l0_mm_du1", lambda: chips_exchange(part))

    loss, grad_x, grads, g_final = local_step(x, p, loss_target, layers, final_norm_g, carried, on_layer_grads,
                                              on_early_grads, on_w_in_grad)
    loss = lax.psum(loss, ("x", "y", "c"))
    arrived0 = dict(zip(later, carried.results["l0_ssd_bwd"][:len(later)]))
    arrived0["w_in"] = carried.results["l0_mm_du1"][0]

    gsmall = {n: jnp.stack([gl[l][n] for l in range(DEPTH)]) for n in SMALL if n != "final_norm_g"}
    gsmall["final_norm_g"] = g_final
    small_shapes = [gsmall[n].shape for n in SMALL]
    gs = _pack([gsmall[n] for n in SMALL], 8)
    gs = sum_slices("sum_small", all_gather8("gather_small_grads", gs))
    gsum = dict(zip(SMALL, _unpack(gs, small_shapes)))
    for n, ax in SHARDED_SMALL.items():
        k = gsum[n].shape[ax] // N_CHIP
        gsum[n] = lax.dynamic_slice_in_dim(gsum[n], chip * k, k, ax)

    summed[0] = [sum_slices(f"sum_chips0_{n}", arrived0[n]) for n in big_names]
    others = [swap_with_sibling("swap_results0", summed[0]), carried.results["l0_ssd_bwd"][len(later):]]
    done = [[jnp.concatenate([jnp.where(ci == 0, m_, o_), jnp.where(ci == 0, o_, m_)], axis=0)
             for m_, o_ in zip(summed[l], others[l])] for l in range(DEPTH)]
    gview = {}
    for t, n in enumerate(big_names):
        g2 = jnp.stack([done[l][t] for l in range(DEPTH)])
        if n in TRANSPOSED:
            gview[n], gsum[n] = g2, jnp.transpose(g2, (0, 2, 1))
        else:
            gsum[n] = g2

    delta, new_m, new_v = {}, {}, {}
    for n in big_names:
        if n in ADAM_VIEW:
            to_view, back = ADAM_VIEW[n]
            gv = gview[n] if n in gview else jnp.transpose(gsum[n], to_view)
            outs = adamw(f"adamw_{n}", jnp.transpose(w[n], to_view), gv, jnp.transpose(mom[n], to_view),
                         jnp.transpose(var[n], to_view))
            delta[n], new_m[n], new_v[n] = [jnp.transpose(o, back) for o in outs]
        else:
            delta[n], new_m[n], new_v[n] = adamw(f"adamw_{n}", w[n], gsum[n], mom[n], var[n])
    shapes = [w[n].shape for n in SMALL]
    pk = lambda d: _pack([d[n] for n in SMALL], 8)[None]
    ds, ms, vs = adamw("adamw_small", pk(w), pk(gsum), pk(mom), pk(var))
    for d, packed in ((delta, ds), (new_m, ms), (new_v, vs)):
        d.update(zip(SMALL, _unpack(packed[0], shapes)))

    return (loss, grad_x, *[gsum[n] for n in WEIGHTS], *[delta[n] for n in WEIGHTS],
            *[new_m[n] for n in WEIGHTS], *[new_v[n] for n in WEIGHTS])
```
